```python
import math
import jax, jax.numpy as jnp
from jax import lax
import numpy as np

D_MODEL = 1024
BATCH = 4
SEQ = 4096
DEPTH = 2

Q_BLOCK = 128
EPS = 1e-6
MLA_HEADS = 8
MLA_Q_LORA = 384
MLA_KV_LORA = 256
MLA_NOPE = 128
MLA_ROPE = 64
MLA_V = 128
ROPE_BASE = 10000.0
DSA_HEADS = 8
DSA_HEAD_DIM = 128
IDX_HEADS = 8
IDX_DIM = 64
DSA_TOPK_MAX = 256
NSA_HEADS = 8
NSA_GROUPS = 2
NSA_HEAD_DIM = 128
CMP_STRIDE = 16
CMP_LEN = 2 * CMP_STRIDE
CMP_HIDDEN = 256
SEL_BLOCK = 64
SEL_COUNT = 16
WINDOW = 512
REL_BUCKETS = 32
REL_MAX_DIST = 128
REL_HEADS = DSA_HEADS + NSA_HEADS
MEM_LEN = 256
CROSS_HEADS = 4
CROSS_HEAD_DIM = D_MODEL // CROSS_HEADS
D_FF = 4 * D_MODEL
N_BRANCH = 3
BRANCH_WIDTH = D_MODEL
NORM_PRE_MIX, NORM_POST_MIX, NORM_PRE_CROSS, NORM_MEM, NORM_POST_CROSS, NORM_PRE_MLP, NORM_POST_MLP = 0, 1, 2, 3, 4, 5, 6
N_NORMS = 7
IN_WIDTHS = (
    MLA_Q_LORA, MLA_KV_LORA, MLA_ROPE,
    DSA_HEADS * DSA_HEAD_DIM, DSA_HEADS * DSA_HEAD_DIM, DSA_HEADS * DSA_HEAD_DIM,
    IDX_HEADS * IDX_DIM, IDX_DIM, IDX_HEADS,
    NSA_HEADS * NSA_HEAD_DIM,
    NSA_GROUPS * NSA_HEAD_DIM, NSA_GROUPS * NSA_HEAD_DIM,
    NSA_GROUPS * NSA_HEAD_DIM, NSA_GROUPS * NSA_HEAD_DIM,
    NSA_GROUPS * NSA_HEAD_DIM, NSA_GROUPS * NSA_HEAD_DIM,
    3 * NSA_HEADS,
    N_BRANCH * D_MODEL,
)
D_IN = sum(IN_WIDTHS)

kernel_name = 'hybrid_mla_dsa_nsa_gated_block'


def split_cols(p):
    offs = np.cumsum(IN_WIDTHS)[:-1].tolist()
    return jnp.split(p, offs, axis=-1)


def rmsnorm(x, g):
    xf = x.astype(jnp.float32)
    y = xf * lax.rsqrt(jnp.mean(xf * xf, axis=-1, keepdims=True) + EPS)
    return (y * g.astype(jnp.float32)).astype(x.dtype)


def masked_softmax(s, mask):
    s = jnp.where(mask, s.astype(jnp.float32), -jnp.inf)
    m = jnp.max(s, axis=-1, keepdims=True)
    m = jnp.where(jnp.isfinite(m), m, 0.0)
    p = jnp.exp(s - m)
    d = jnp.sum(p, axis=-1, keepdims=True)
    return p / jnp.where(d > 0, d, 1.0)


def rel_bucket(dist):
    n = jnp.maximum(dist, 0)
    exact = REL_BUCKETS // 2
    nf = jnp.maximum(n, 1).astype(jnp.float32)
    log_b = exact + (jnp.log(nf / exact) / math.log(REL_MAX_DIST / exact) * (REL_BUCKETS - exact)).astype(jnp.int32)
    return jnp.where(n < exact, n, jnp.minimum(log_b, REL_BUCKETS - 1))


def rope(x, cos, sin):
    x1, x2 = jnp.split(x.astype(jnp.float32), 2, axis=-1)
    return jnp.concatenate([x1 * cos - x2 * sin, x2 * cos + x1 * sin], axis=-1).astype(x.dtype)


def sweep_query_blocks(fn, seq):
    out = lax.map(fn, jnp.arange(seq // Q_BLOCK))
    return jax.tree_util.tree_map(
        lambda a: jnp.moveaxis(a, 0, 1).reshape((a.shape[1], seq) + a.shape[3:]), out)


def mla(q_lat, kv_lat, k_rope, positions, q_norm, kv_norm, w_uq, w_ukv):
    B, S, _ = q_lat.shape
    q = (rmsnorm(q_lat, q_norm) @ w_uq).reshape(B, S, MLA_HEADS, MLA_NOPE + MLA_ROPE)
    kv = (rmsnorm(kv_lat, kv_norm) @ w_ukv).reshape(B, S, MLA_HEADS, MLA_NOPE + MLA_V)
    q_nope, q_rope = q[..., :MLA_NOPE], q[..., MLA_NOPE:]
    k_nope, v = kv[..., :MLA_NOPE], kv[..., MLA_NOPE:]
    inv_freq = ROPE_BASE ** (-jnp.arange(0, MLA_ROPE, 2, dtype=jnp.float32) / MLA_ROPE)
    ang = positions.astype(jnp.float32)[..., None] * inv_freq
    cos, sin = jnp.cos(ang), jnp.sin(ang)
    q_rope = rope(q_rope, cos[:, :, None], sin[:, :, None])
    k_rope = rope(k_rope, cos, sin)
    scale = (MLA_NOPE + MLA_ROPE) ** -0.5
    kpos = jnp.arange(S)

    def block(i):
        t0 = i * Q_BLOCK
        qn = lax.dynamic_slice_in_dim(q_nope, t0, Q_BLOCK, axis=1)
        qr = lax.dynamic_slice_in_dim(q_rope, t0, Q_BLOCK, axis=1)
        s = jnp.einsum('bthd,bshd->bhts', qn, k_nope) + jnp.einsum('bthr,bsr->bhts', qr, k_rope)
        t = t0 + jnp.arange(Q_BLOCK)
        p = masked_softmax(s * scale, kpos[None, :] <= t[:, None]).astype(v.dtype)
        return jnp.einsum('bhts,bshd->bthd', p, v)

    return sweep_query_blocks(block, S).reshape(B, S, MLA_HEADS * MLA_V)


def dsa(q, k, v, iq, ik, iw, tab):
    B, S, _ = q.shape
    q = q.reshape(B, S, DSA_HEADS, DSA_HEAD_DIM)
    k = k.reshape(B, S, DSA_HEADS, DSA_HEAD_DIM)
    v = v.reshape(B, S, DSA_HEADS, DSA_HEAD_DIM)
    iq = iq.reshape(B, S, IDX_HEADS, IDX_DIM)
    topk = min(DSA_TOPK_MAX, S // 4)
    scale = DSA_HEAD_DIM ** -0.5
    kpos = jnp.arange(S)
    gather = jax.vmap(lambda arr, ids: arr[ids])

    def block(i):
        t0 = i * Q_BLOCK
        t = t0 + jnp.arange(Q_BLOCK)
        qb = lax.dynamic_slice_in_dim(q, t0, Q_BLOCK, axis=1)
        iqb = lax.dynamic_slice_in_dim(iq, t0, Q_BLOCK, axis=1)
        iwb = lax.dynamic_slice_in_dim(iw, t0, Q_BLOCK, axis=1)
        score = jnp.einsum('bth,bths->bts', iwb, jax.nn.relu(jnp.einsum('bthd,bsd->bths', iqb, ik)))
        score = jnp.where(kpos[None, :] <= t[:, None], score.astype(jnp.float32), -jnp.inf)
        _, idx = lax.top_k(score, topk)
        valid = idx <= t[None, :, None]
        kg = gather(k, idx)
        vg = gather(v, idx)
        bias = tab[rel_bucket(t[None, :, None] - idx)]
        s = jnp.einsum('bthd,btkhd->bhtk', qb, kg) * scale + jnp.transpose(bias, (0, 3, 1, 2))
        p = masked_softmax(s, valid[:, None]).astype(v.dtype)
        return jnp.einsum('bhtk,btkhd->bthd', p, vg)

    return sweep_query_blocks(block, S).reshape(B, S, DSA_HEADS * DSA_HEAD_DIM)


def nsa(q, kc, vc, ks, vs, kw, vw, gate, cmp_pos, cmp_w1, cmp_w2, tab):
    B, S, _ = q.shape
    G, HG, dk = NSA_GROUPS, NSA_HEADS // NSA_GROUPS, NSA_HEAD_DIM
    q = q.reshape(B, S, G, HG, dk)
    scale = dk ** -0.5
    t_all = jnp.arange(S)

    def compress(a, pos, w1, w2):
        a = a.reshape(B, S // CMP_STRIDE, CMP_STRIDE, G, dk)
        blocks = jnp.concatenate([a[:, :-1], a[:, 1:]], axis=2) + pos[None, None, :, None, :]
        nc_ = blocks.shape[1]
        flat = jnp.transpose(blocks, (0, 1, 3, 2, 4)).reshape(B, nc_, G, CMP_LEN * dk)
        return jax.nn.gelu(flat @ w1) @ w2

    k_cmp = compress(kc, cmp_pos[0], cmp_w1[0], cmp_w2[0])
    v_cmp = compress(vc, cmp_pos[1], cmp_w1[1], cmp_w2[1])
    nc = S // CMP_STRIDE - 1
    c_start = jnp.arange(nc) * CMP_STRIDE
    cmask = (c_start + CMP_LEN - 1)[None, :] <= t_all[:, None]
    p_c = masked_softmax(jnp.einsum('btghd,bcgd->bghtc', q, k_cmp) * scale, cmask)
    o_c = jnp.einsum('bghtc,bcgd->btghd', p_c.astype(v_cmp.dtype), v_cmp)

    ns = S // SEL_BLOCK
    j = jnp.arange(ns)
    j_start = j * SEL_BLOCK
    overlap = ((c_start[:, None] < j_start[None, :] + SEL_BLOCK) &
               (c_start[:, None] + CMP_LEN > j_start[None, :])).astype(jnp.float32)
    imp = jnp.einsum('bghtc,cj->bgtj', p_c, overlap)
    blk_t = t_all // SEL_BLOCK
    forced = (j[None, :] == 0) | (j[None, :] == blk_t[:, None]) | (j[None, :] == blk_t[:, None] - 1)
    imp = jnp.where(forced, jnp.inf, imp)
    imp = jnp.where(j_start[None, :] > t_all[:, None], -jnp.inf, imp)
    n_sel = min(SEL_COUNT, ns)
    _, sel_idx = lax.top_k(imp, n_sel)
    ks_blocks = jnp.transpose(ks.reshape(B, ns, SEL_BLOCK, G, dk), (0, 3, 1, 2, 4))
    vs_blocks = jnp.transpose(vs.reshape(B, ns, SEL_BLOCK, G, dk), (0, 3, 1, 2, 4))
    gather_blocks = jax.vmap(jax.vmap(lambda blk, ids: blk[ids]))

    pad = ((0, 0), (WINDOW, 0), (0, 0), (0, 0))
    kw_pad = jnp.pad(kw.reshape(B, S, G, dk), pad)
    vw_pad = jnp.pad(vw.reshape(B, S, G, dk), pad)

    tab_g = jnp.transpose(tab.reshape(REL_BUCKETS, G, HG), (1, 0, 2))
    g_idx = jnp.arange(G)[None, :, None, None]
    n_keys_sel = n_sel * SEL_BLOCK

    def block(i):
        t0 = i * Q_BLOCK
        tq = t0 + jnp.arange(Q_BLOCK)
        qb = lax.dynamic_slice_in_dim(q, t0, Q_BLOCK, axis=1)
        idx = lax.dynamic_slice_in_dim(sel_idx, t0, Q_BLOCK, axis=2)
        kg = gather_blocks(ks_blocks, idx).reshape(B, G, Q_BLOCK, n_keys_sel, dk)
        vg = gather_blocks(vs_blocks, idx).reshape(B, G, Q_BLOCK, n_keys_sel, dk)
        kpos = (idx[..., None] * SEL_BLOCK + jnp.arange(SEL_BLOCK)).reshape(B, G, Q_BLOCK, n_keys_sel)
        dist = tq[None, None, :, None] - kpos
        bias = tab_g[g_idx, rel_bucket(dist)]
        s = jnp.einsum('btghd,bgtkd->bghtk', qb, kg) * scale + jnp.transpose(bias, (0, 1, 4, 2, 3))
        p = masked_softmax(s, (dist >= 0)[:, :, None]).astype(vg.dtype)
        o_s = jnp.einsum('bghtk,bgtkd->btghd', p, vg)
        kwin = lax.dynamic_slice_in_dim(kw_pad, t0, WINDOW + Q_BLOCK, axis=1)
        vwin = lax.dynamic_slice_in_dim(vw_pad, t0, WINDOW + Q_BLOCK, axis=1)
        spos = t0 - WINDOW + jnp.arange(WINDOW + Q_BLOCK)
        wd = tq[:, None] - spos[None, :]
        wmask = (wd >= 0) & (wd < WINDOW) & (spos[None, :] >= 0)
        wbias = jnp.transpose(tab_g[:, rel_bucket(wd)], (0, 3, 1, 2))[None]
        s_w = jnp.einsum('btghd,bsgd->bghts', qb, kwin) * scale + wbias
        p_w = masked_softmax(s_w, wmask).astype(vwin.dtype)
        o_w = jnp.einsum('bghts,bsgd->btghd', p_w, vwin)
        return o_s, o_w

    o_s, o_w = sweep_query_blocks(block, S)
    gt = jax.nn.sigmoid(gate.reshape(B, S, 3, G, HG))[..., None]
    o = gt[:, :, 0] * o_c + gt[:, :, 1] * o_s + gt[:, :, 2] * o_w
    return o.reshape(B, S, NSA_HEADS * dk)


def cross_attn(h, m, wq, wkv, wo):
    B, S, _ = h.shape
    q = (h @ wq).reshape(B, S, CROSS_HEADS, CROSS_HEAD_DIM)
    kv = (m @ wkv).reshape(B, m.shape[1], 2, CROSS_HEADS, CROSS_HEAD_DIM)
    k, v = kv[:, :, 0], kv[:, :, 1]
    s = jnp.einsum('bthd,bmhd->bhtm', q, k) * CROSS_HEAD_DIM ** -0.5
    p = jax.nn.softmax(s.astype(jnp.float32), axis=-1).astype(v.dtype)
    return jnp.einsum('bhtm,bmhd->bthd', p, v).reshape(B, S, D_MODEL) @ wo


def setup_inputs(seed: int = 0) -> dict:
    key = jax.random.key(seed)
    ks = jax.random.split(key, 20)
    f32 = jnp.float32

    def nrm(k, shape):
        return jax.random.normal(k, shape, f32)

    def w(k, shape, fan_in):
        return nrm(k, shape) * fan_in ** -0.5

    dk = NSA_HEAD_DIM
    return {
        'x': nrm(ks[0], (BATCH, SEQ, D_MODEL)),
        'mem': nrm(ks[1], (BATCH, MEM_LEN, D_MODEL)),
        'positions': jnp.tile(jnp.arange(SEQ, dtype=jnp.int32)[None, :], (BATCH, 1)),
        'rel_bias': 0.5 * nrm(ks[2], (REL_BUCKETS, REL_HEADS)),
        'norm_gains': 1.0 + 0.05 * nrm(ks[3], (DEPTH, N_NORMS, D_MODEL)),
        'w_in': w(ks[4], (DEPTH, D_MODEL, D_IN), D_MODEL),
        'mla_q_norm': 1.0 + 0.05 * nrm(ks[5], (DEPTH, MLA_Q_LORA)),
        'mla_kv_norm': 1.0 + 0.05 * nrm(ks[6], (DEPTH, MLA_KV_LORA)),
        'mla_w_uq': w(ks[7], (DEPTH, MLA_Q_LORA, MLA_HEADS * (MLA_NOPE + MLA_ROPE)), MLA_Q_LORA),
        'mla_w_ukv': w(ks[8], (DEPTH, MLA_KV_LORA, MLA_HEADS * (MLA_NOPE + MLA_V)), MLA_KV_LORA),
        'nsa_cmp_pos': 0.5 * nrm(ks[9], (DEPTH, 2, CMP_LEN, dk)),
        'nsa_cmp_w1': w(ks[10], (DEPTH, 2, CMP_LEN * dk, CMP_HIDDEN), CMP_LEN * dk),
        'nsa_cmp_w2': w(ks[11], (DEPTH, 2, CMP_HIDDEN, dk), CMP_HIDDEN),
        'w_branch': w(ks[12], (DEPTH, N_BRANCH, BRANCH_WIDTH, D_MODEL), BRANCH_WIDTH),
        'w_out': w(ks[13], (DEPTH, D_MODEL, D_MODEL), D_MODEL),
        'cross_wq': w(ks[14], (DEPTH, D_MODEL, D_MODEL), D_MODEL),
        'cross_wkv': w(ks[15], (DEPTH, D_MODEL, 2 * D_MODEL), D_MODEL),
        'cross_wo': w(ks[16], (DEPTH, D_MODEL, D_MODEL), D_MODEL),
        'mlp_w1': w(ks[17], (DEPTH, D_MODEL, D_FF), D_MODEL),
        'mlp_w2': w(ks[18], (DEPTH, D_FF, D_MODEL), D_FF),
    }


def reference(x, mem, positions, rel_bias, norm_gains, w_in, mla_q_norm, mla_kv_norm,
              mla_w_uq, mla_w_ukv, nsa_cmp_pos, nsa_cmp_w1, nsa_cmp_w2, w_branch, w_out,
              cross_wq, cross_wkv, cross_wo, mlp_w1, mlp_w2):
    B, S, _ = x.shape
    tab_dsa = rel_bias[:, :DSA_HEADS]
    tab_nsa = rel_bias[:, DSA_HEADS:]
    for l in range(DEPTH):
        g = norm_gains[l]
        h = rmsnorm(x, g[NORM_PRE_MIX])
        (mla_q_lat, mla_kv_lat, mla_k_rope, dsa_q, dsa_k, dsa_v, idx_q, idx_k, idx_w,
         nsa_q, nsa_kc, nsa_vc, nsa_ks, nsa_vs, nsa_kw, nsa_vw, nsa_gate, branch_gate) = split_cols(h @ w_in[l])
        o_a = mla(mla_q_lat, mla_kv_lat, mla_k_rope, positions, mla_q_norm[l], mla_kv_norm[l],
                  mla_w_uq[l], mla_w_ukv[l])
        o_b = dsa(dsa_q, dsa_k, dsa_v, idx_q, idx_k, idx_w, tab_dsa)
        o_c = nsa(nsa_q, nsa_kc, nsa_vc, nsa_ks, nsa_vs, nsa_kw, nsa_vw, nsa_gate,
                  nsa_cmp_pos[l], nsa_cmp_w1[l], nsa_cmp_w2[l], tab_nsa)
        gates = jax.nn.sigmoid(branch_gate.reshape(B, S, N_BRANCH, D_MODEL))
        merged = (gates[:, :, 0] * (o_a @ w_branch[l, 0]) +
                  gates[:, :, 1] * (o_b @ w_branch[l, 1]) +
                  gates[:, :, 2] * (o_c @ w_branch[l, 2]))
        x = x + rmsnorm(merged @ w_out[l], g[NORM_POST_MIX])
        h = rmsnorm(x, g[NORM_PRE_CROSS])
        m = rmsnorm(mem, g[NORM_MEM])
        x = x + rmsnorm(cross_attn(h, m, cross_wq[l], cross_wkv[l], cross_wo[l]), g[NORM_POST_CROSS])
        h = rmsnorm(x, g[NORM_PRE_MLP])
        x = x + rmsnorm(jnp.square(jax.nn.relu(h @ mlp_w1[l])) @ mlp_w2[l], g[NORM_POST_MLP])
    return x
```

```python
import functools
import math

import numpy as np
import jax
import jax.numpy as jnp
from jax import lax
from jax.experimental import pallas as pl
from jax.experimental.pallas import tpu as pltpu

F32 = jnp.float32
BF16 = jnp.bfloat16

D_MODEL = 1024
EPS = 1e-6
MLA_HEADS = 8
MLA_Q_LORA = 384
MLA_KV_LORA = 256
MLA_NOPE = 128
MLA_ROPE = 64
MLA_V = 128
ROPE_BASE = 10000.0
DSA_HEADS = 8
DSA_HEAD_DIM = 128
IDX_HEADS = 8
IDX_DIM = 64
DSA_TOPK_MAX = 256
NSA_HEADS = 8
NSA_GROUPS = 2
NSA_HG = NSA_HEADS // NSA_GROUPS
NSA_HEAD_DIM = 128
CMP_STRIDE = 16
CMP_LEN = 2 * CMP_STRIDE
CMP_HIDDEN = 256
SEL_BLOCK = 64
SEL_COUNT = 16
WINDOW = 512
REL_BUCKETS = 32
REL_MAX_DIST = 128
REL_HEADS = DSA_HEADS + NSA_HEADS
CROSS_HEADS = 4
CROSS_HEAD_DIM = D_MODEL // CROSS_HEADS
D_FF = 4 * D_MODEL
N_BRANCH = 3

LANE = 128
NEG = -1e30
VMEM_LIMIT = 48 * 1024 * 1024

TQ = 256
BISECT_ITERS = 36

OFF_DSA_Q = 0
OFF_DSA_K = 1024
OFF_DSA_V = 2048
OFF_NSA_Q = 3072
OFF_BGATE = 4096
OFF_IDX_Q = 7168
OFF_NSA_KV = 7680
OFF_MLA = 9216
OFF_IDX_K = 10240
OFF_IDX_W = 10368
OFF_NSA_GATE = 10496
N_PACK = 10752
TN_IN = 1536


def _cparams(sem):
    return pltpu.CompilerParams(dimension_semantics=sem, vmem_limit_bytes=VMEM_LIMIT)


def _rms(x, g):
    return x * lax.rsqrt(jnp.mean(x * x, axis=-1, keepdims=True) + EPS) * g


def _dot(a, b):
    return jnp.dot(a, b, preferred_element_type=F32)


def _dot_nt(a, b):
    return lax.dot_general(a, b, (((1,), (1,)), ((), ())), preferred_element_type=F32)


def _norm_mm_kernel(x_ref, g_ref, w_ref, o_ref, xn_ref):
    @pl.when(pl.program_id(1) == 0)
    def _():
        xn_ref[...] = _rms(x_ref[...].astype(F32), g_ref[...]).astype(BF16)

    o_ref[...] = _dot(xn_ref[...], w_ref[...]).astype(o_ref.dtype)


def norm_matmul(x, g, w, *, tm, tn):
    M, K = x.shape
    N = w.shape[1]
    return pl.pallas_call(
        _norm_mm_kernel,
        grid=(M // tm, N // tn),
        in_specs=[pl.BlockSpec((tm, K), lambda i, j: (i, 0)),
                  pl.BlockSpec((1, K), lambda i, j: (0, 0)),
                  pl.BlockSpec((K, tn), lambda i, j: (0, j))],
        out_specs=pl.BlockSpec((tm, tn), lambda i, j: (i, j)),
        out_shape=jax.ShapeDtypeStruct((M, N), BF16),
        scratch_shapes=[pltpu.VMEM((tm, K), BF16)],
        compiler_params=_cparams(("parallel", "arbitrary")),
        name="norm_matmul",
    )(x, g, w)


def _bias_tiles_kernel(tab_ref, bkt_ref, o_ref):
    h = pl.program_id(0)
    bkt = bkt_ref[...]
    far = tab_ref[REL_BUCKETS - 1, h]
    out = jnp.zeros(bkt.shape, F32)
    for b in range(REL_BUCKETS - 1):
        out = jnp.where(bkt == b, tab_ref[b, h] - far, out)
    o_ref[...] = out


def _bucket_tiles(tq):
    i = np.arange(tq)[:, None]
    j = np.arange(tq)[None, :]
    d = np.stack([i - j, tq + i - j]).astype(np.int32)
    n = jnp.maximum(jnp.asarray(d), 0)
    exact = REL_BUCKETS // 2
    nf = jnp.maximum(n, 1).astype(F32)
    log_b = exact + (jnp.log(nf / exact) / math.log(REL_MAX_DIST / exact) * (REL_BUCKETS - exact)).astype(jnp.int32)
    return jnp.where(n < exact, n, jnp.minimum(log_b, REL_BUCKETS - 1)).astype(jnp.int32)


def bias_tiles(rel_bias, tq):
    assert tq >= REL_MAX_DIST
    bkt = _bucket_tiles(tq)
    return pl.pallas_call(
        _bias_tiles_kernel,
        grid=(REL_HEADS,),
        in_specs=[pl.BlockSpec(memory_space=pltpu.SMEM),
                  pl.BlockSpec((2, tq, tq), lambda h: (0, 0, 0))],
        out_specs=pl.BlockSpec((None, 2, tq, tq), lambda h: (h, 0, 0, 0)),
        out_shape=jax.ShapeDtypeStruct((REL_HEADS, 2, tq, tq), F32),
        compiler_params=_cparams(("arbitrary",)),
        name="bias_tiles",
    )(rel_bias, bkt)


def _mla_prep_kernel(lat_ref, pos_ref, freq_ref, qg_ref, kvg_ref, wqn_ref, wqr_ref, wqrr_ref, wkv_ref,
                     qn_ref, qr_ref, kn_ref, v_ref, kr_ref):
    lat = lat_ref[...].astype(F32)
    qlat = _rms(lat[:, :MLA_Q_LORA], qg_ref[...]).astype(BF16)
    kvlat = _rms(lat[:, MLA_Q_LORA:MLA_Q_LORA + MLA_KV_LORA], kvg_ref[...]).astype(BF16)
    o = MLA_Q_LORA + MLA_KV_LORA
    kr = lat[:, o:o + LANE]
    krr = lat[:, o + LANE:o + 2 * LANE]
    ang = pos_ref[...].astype(F32) * freq_ref[...]
    cos, sin = jnp.cos(ang), jnp.sin(ang)
    cos8 = jnp.concatenate([cos] * MLA_HEADS, axis=1)
    sin8 = jnp.concatenate([sin] * MLA_HEADS, axis=1)
    qn_ref[...] = _dot(qlat, wqn_ref[...]).astype(BF16)
    qr_ref[...] = (_dot(qlat, wqr_ref[...]) * cos8 + _dot(qlat, wqrr_ref[...]) * sin8).astype(BF16)
    kv = _dot(kvlat, wkv_ref[...])
    nk = MLA_HEADS * MLA_NOPE
    kn_ref[...] = kv[:, :nk].astype(BF16)
    v_ref[...] = kv[:, nk:].astype(BF16)
    kr_ref[...] = (kr * cos + krr * sin).astype(BF16)


def mla_prep(proj, pos, freq, qg, kvg, wqn, wqr, wqrr, wkv, *, tm):
    M = proj.shape[0]
    HD = MLA_HEADS * LANE
    full = lambda a: pl.BlockSpec(a.shape, lambda i: (0, 0))
    outs = [jax.ShapeDtypeStruct((M, HD), BF16)] * 4 + [jax.ShapeDtypeStruct((M, LANE), BF16)]
    return pl.pallas_call(
        _mla_prep_kernel,
        grid=(M // tm,),
        in_specs=[pl.BlockSpec((tm, 1024), lambda i: (i, OFF_MLA // 1024)),
                  pl.BlockSpec((tm, 1), lambda i: (i, 0)),
                  full(freq), full(qg), full(kvg), full(wqn), full(wqr), full(wqrr), full(wkv)],
        out_specs=[pl.BlockSpec((tm, HD), lambda i: (i, 0))] * 4 + [pl.BlockSpec((tm, LANE), lambda i: (i, 0))],
        out_shape=outs,
        compiler_params=_cparams(("parallel",)),
        name="mla_prep",
    )(proj, pos, freq, qg, kvg, wqn, wqr, wqrr, wkv)


def _softmax_init(m_ref, l_ref, acc_ref):
    m_ref[...] = jnp.full(m_ref.shape, NEG, F32)
    l_ref[...] = jnp.zeros(l_ref.shape, F32)
    acc_ref[...] = jnp.zeros(acc_ref.shape, F32)


def _softmax_step(s, v, m_ref, l_ref, acc_ref):
    m_prev = m_ref[...]
    m_new = jnp.maximum(m_prev, jnp.max(s, axis=-1, keepdims=True))
    alpha = jnp.exp(m_prev - m_new)
    p = jnp.exp(s - m_new)
    l_ref[...] = alpha * l_ref[...] + jnp.sum(p, axis=-1, keepdims=True)
    acc_ref[...] = alpha * acc_ref[...] + _dot(p.astype(BF16), v)
    m_ref[...] = m_new


def _tri(tq, tk):
    row = lax.broadcasted_iota(jnp.int32, (tq, tk), 0)
    col = lax.broadcasted_iota(jnp.int32, (tq, tk), 1)
    return row, col


def _mla_attn_kernel(qn_ref, qr_ref, kn_ref, kr_ref, v_ref, o_ref, m_ref, l_ref, acc_ref, *, tq, scale):
    qi = pl.program_id(2)
    q = jnp.concatenate([qn_ref[...], qr_ref[...]], axis=-1)
    _softmax_init(m_ref, l_ref, acc_ref)

    def chunk(j):
        sl = pl.ds(pl.multiple_of(j * tq, tq), tq)
        k = jnp.concatenate([kn_ref[sl, :], kr_ref[sl, :]], axis=-1)
        return _dot_nt(q, k) * scale, v_ref[sl, :]

    def body(j, c):
        s, v = chunk(j)
        _softmax_step(s, v, m_ref, l_ref, acc_ref)
        return c

    lax.fori_loop(0, qi, body, 0)
    s, v = chunk(qi)
    row, col = _tri(tq, tq)
    _softmax_step(jnp.where(row >= col, s, NEG), v, m_ref, l_ref, acc_ref)
    o_ref[...] = (acc_ref[...] / l_ref[...]).astype(o_ref.dtype)


def mla_attention(qn, qr, kn, kr, v, *, B, S, tq):
    M = B * S
    nq = S // tq
    H = MLA_HEADS
    scale = (MLA_NOPE + MLA_ROPE) ** -0.5
    qspec = pl.BlockSpec((tq, LANE), lambda b, h, i: (b * nq + i, h))
    kspec = pl.BlockSpec((S, LANE), lambda b, h, i: (b, h))
    return pl.pallas_call(
        functools.partial(_mla_attn_kernel, tq=tq, scale=scale),
        grid=(B, H, nq),
        in_specs=[qspec, qspec, kspec, pl.BlockSpec((S, LANE), lambda b, h, i: (b, 0)), kspec],
        out_specs=qspec,
        out_shape=jax.ShapeDtypeStruct((M, H * MLA_V), BF16),
        scratch_shapes=[pltpu.VMEM((tq, 1), F32), pltpu.VMEM((tq, 1), F32), pltpu.VMEM((tq, MLA_V), F32)],
        compiler_params=_cparams(("parallel", "parallel", "arbitrary")),
        name="mla_attention",
    )(qn, qr, kn, kr, v)


def _dsa_index_kernel(iq_ref, ik_ref, iw_ref, mb_ref, sc_ref, *, tq, tk, S, topk, iters):
    t0 = pl.program_id(1) * tq
    iq = iq_ref[...]
    ik = ik_ref[...]
    iw = iw_ref[...].astype(F32)
    lane = lax.broadcasted_iota(jnp.int32, (tq, LANE), 1)
    score = jnp.zeros((tq, S), F32)
    for h in range(IDX_HEADS):
        blk = iq[:, (h // 2) * LANE:(h // 2 + 1) * LANE]
        keep = (lane >= IDX_DIM) if h % 2 else (lane < IDX_DIM)
        qh = jnp.where(keep, blk, jnp.zeros_like(blk))
        score = score + iw[:, h:h + 1] * jnp.maximum(_dot_nt(qh, ik), 0.0)
    row = lax.broadcasted_iota(jnp.int32, (tq, S), 0) + t0
    col = lax.broadcasted_iota(jnp.int32, (tq, S), 1)
    causal = col <= row
    sc_ref[...] = jnp.where(causal, score, NEG)
    kf = float(topk)

    def count(pred):
        return jnp.sum(jnp.where(pred, 1.0, 0.0), axis=-1, keepdims=True)

    lo0 = jnp.min(jnp.where(causal, score, -NEG), axis=-1, keepdims=True)
    mx = jnp.max(sc_ref[...], axis=-1, keepdims=True)
    hi0 = mx + (jnp.abs(mx) * 1e-6 + 1.0)

    def bisect(_, c):
        lo, hi = c
        mid = 0.5 * (lo + hi)
        ge = count(sc_ref[...] >= mid) >= kf
        return jnp.where(ge, mid, lo), jnp.where(ge, hi, mid)

    lo, hi = lax.fori_loop(0, iters, bisect, (lo0, hi0))
    n_lo = count(sc_ref[...] >= lo)
    tied = jnp.max(n_lo) > kf

    def write(sel):
        mb = jnp.where(sel, 0.0, NEG).astype(mb_ref.dtype)
        for j in range(S // tk):
            mb_ref[j] = mb[:, j * tk:(j + 1) * tk]

    @pl.when(jnp.logical_not(tied))
    def _():
        write(sc_ref[...] >= lo)

    @pl.when(tied)
    def _():
        sc = sc_ref[...]
        above = sc >= hi
        band = (sc >= lo) & jnp.logical_not(above)
        need = kf - count(above)
        colf = col.astype(F32)

        def ibisect(_, c):
            jlo, jhi = c
            jm = jnp.floor(0.5 * (jlo + jhi))
            ok = count(band & (colf <= jm)) >= need
            return jnp.where(ok, jlo, jm), jnp.where(ok, jm, jhi)

        steps = max(1, int(math.ceil(math.log2(S))) + 1)
        _, jhi = lax.fori_loop(0, steps, ibisect,
                               (jnp.full((tq, 1), -1.0, F32), jnp.full((tq, 1), float(S - 1), F32)))
        write(above | (band & (colf <= jhi)))


def dsa_index_mask(proj, *, B, S, tqi, tk, topk):
    nq = S // tqi
    nk = S // tk
    return pl.pallas_call(
        functools.partial(_dsa_index_kernel, tq=tqi, tk=tk, S=S, topk=topk, iters=BISECT_ITERS),
        grid=(B, nq),
        in_specs=[pl.BlockSpec((tqi, 512), lambda b, i: (b * nq + i, OFF_IDX_Q // 512)),
                  pl.BlockSpec((S, LANE), lambda b, i: (b, OFF_IDX_K // LANE)),
                  pl.BlockSpec((tqi, LANE), lambda b, i: (b * nq + i, OFF_IDX_W // LANE))],
        out_specs=pl.BlockSpec((None, nk, tqi, tk), lambda b, i: (b, 0, i, 0)),
        out_shape=jax.ShapeDtypeStruct((B, nk, S, tk), BF16),
        scratch_shapes=[pltpu.VMEM((tqi, S), F32)],
        compiler_params=_cparams(("parallel", "parallel")),
        name="dsa_index_mask",
    )(proj, proj, proj)


def _dsa_attn_kernel(q_ref, k_ref, v_ref, mb_ref, bias_ref, o_ref, m_ref, l_ref, acc_ref, *, tq, scale):
    qi = pl.program_id(2)
    q = q_ref[...]
    _softmax_init(m_ref, l_ref, acc_ref)

    def chunk(j):
        sl = pl.ds(pl.multiple_of(j * tq, tq), tq)
        s = _dot_nt(q, k_ref[sl, :]) * scale + mb_ref[j].astype(F32)
        return s, v_ref[sl, :]

    def body(j, c):
        s, v = chunk(j)
        _softmax_step(s, v, m_ref, l_ref, acc_ref)
        return c

    lax.fori_loop(0, jnp.maximum(qi - 1, 0), body, 0)

    @pl.when(qi >= 1)
    def _():
        s, v = chunk(qi - 1)
        _softmax_step(s + bias_ref[1], v, m_ref, l_ref, acc_ref)

    s, v = chunk(qi)
    _softmax_step(s + bias_ref[0], v, m_ref, l_ref, acc_ref)
    o_ref[...] = (acc_ref[...] / l_ref[...]).astype(o_ref.dtype)


def dsa_attention(proj, maskb, btiles, *, B, S, tq):
    M = B * S
    nq = S // tq
    H = DSA_HEADS
    scale = DSA_HEAD_DIM ** -0.5
    return pl.pallas_call(
        functools.partial(_dsa_attn_kernel, tq=tq, scale=scale),
        grid=(B, H, nq),
        in_specs=[pl.BlockSpec((tq, LANE), lambda b, h, i: (b * nq + i, OFF_DSA_Q // LANE + h)),
                  pl.BlockSpec((S, LANE), lambda b, h, i: (b, OFF_DSA_K // LANE + h)),
                  pl.BlockSpec((S, LANE), lambda b, h, i: (b, OFF_DSA_V // LANE + h)),
                  pl.BlockSpec((None, nq, tq, tq), lambda b, h, i: (b, 0, i, 0)),
                  pl.BlockSpec((None, 2, tq, tq), lambda b, h, i: (h, 0, 0, 0))],
        out_specs=pl.BlockSpec((tq, LANE), lambda b, h, i: (b * nq + i, h)),
        out_shape=jax.ShapeDtypeStruct((M, H * DSA_HEAD_DIM), BF16),
        scratch_shapes=[pltpu.VMEM((tq, 1), F32), pltpu.VMEM((tq, 1), F32), pltpu.VMEM((tq, DSA_HEAD_DIM), F32)],
        compiler_params=_cparams(("parallel", "parallel", "arbitrary")),
        name="dsa_attention",
    )(proj, proj, proj, maskb, btiles)


def _nsa_compress_kernel(x_ref, pos_ref, w1_ref, w2_ref, o_ref, *, ncp):
    dk = NSA_HEAD_DIM
    posw = _dot(pos_ref[...], w1_ref[...])[0:1, :]
    for g in range(NSA_GROUPS):
        lo = jnp.zeros((ncp, CMP_HIDDEN), F32)
        hi = jnp.zeros((ncp, CMP_HIDDEN), F32)
        for l in range(CMP_STRIDE):
            xs = x_ref[:, (l * NSA_GROUPS + g) * dk:(l * NSA_GROUPS + g + 1) * dk]
            lo = lo + _dot(xs, w1_ref[l * dk:(l + 1) * dk, :])
            hi = hi + _dot(xs, w1_ref[(CMP_STRIDE + l) * dk:(CMP_STRIDE + l + 1) * dk, :])
        hid = lo + pltpu.roll(hi, ncp - 1, 0) + posw
        o_ref[g] = _dot(jax.nn.gelu(hid).astype(BF16), w2_ref[...]).astype(o_ref.dtype)


def nsa_compress(xkv, posflat, w1, w2, *, B, S):
    ncp = S // CMP_STRIDE
    G, dk = NSA_GROUPS, NSA_HEAD_DIM
    width = CMP_STRIDE * G * dk
    return pl.pallas_call(
        functools.partial(_nsa_compress_kernel, ncp=ncp),
        grid=(2, B),
        in_specs=[pl.BlockSpec((None, ncp, width), lambda a, b: (a, b, 0)),
                  pl.BlockSpec((None, 8, CMP_LEN * dk), lambda a, b: (a, 0, 0)),
                  pl.BlockSpec((None, CMP_LEN * dk, CMP_HIDDEN), lambda a, b: (a, 0, 0)),
                  pl.BlockSpec((None, CMP_HIDDEN, dk), lambda a, b: (a, 0, 0))],
        out_specs=pl.BlockSpec((None, None, G, ncp, dk), lambda a, b: (a, b, 0, 0, 0)),
        out_shape=jax.ShapeDtypeStruct((2, B, G, ncp, dk), BF16),
        compiler_params=_cparams(("parallel", "parallel")),
        name="nsa_compress",
    )(xkv, posflat, w1, w2)


def _nsa_cmp_kernel(q_ref, kc_ref, vc_ref, ov_ref, oc_ref, sel_ref, *, tq, ncp, ns, n_sel, scale):
    t0 = pl.program_id(2) * tq
    dk = NSA_HEAD_DIM
    kc = kc_ref[...]
    vc = vc_ref[...]
    trow = lax.broadcasted_iota(jnp.int32, (tq, ncp), 0) + t0
    ccol = lax.broadcasted_iota(jnp.int32, (tq, ncp), 1)
    vis = (ccol * CMP_STRIDE + (CMP_LEN - 1)) <= trow
    psum = jnp.zeros((tq, ncp), F32)
    for h in range(NSA_HG):
        s = jnp.where(vis, _dot_nt(q_ref[:, h * dk:(h + 1) * dk], kc) * scale, NEG)
        m = jnp.max(s, axis=-1, keepdims=True)
        p = jnp.where(vis, jnp.exp(s - m), 0.0)
        d = jnp.sum(p, axis=-1, keepdims=True)
        p = p / jnp.where(d > 0, d, 1.0)
        oc_ref[:, h * dk:(h + 1) * dk] = _dot(p.astype(BF16), vc).astype(oc_ref.dtype)
        psum = psum + p
    p_hi = psum.astype(BF16)
    p_lo = (psum - p_hi.astype(F32)).astype(BF16)
    ov = ov_ref[...]
    imp = _dot_nt(ov, p_hi) + _dot_nt(ov, p_lo)
    imp = imp[:ns]
    jrow = lax.broadcasted_iota(jnp.int32, (ns, tq), 0)
    tcol = lax.broadcasted_iota(jnp.int32, (ns, tq), 1) + t0
    blk_t = tcol // SEL_BLOCK
    forced = (jrow == 0) | (jrow == blk_t) | (jrow == blk_t - 1)
    val = jnp.where(forced, -NEG, imp)
    val = jnp.where(jrow * SEL_BLOCK > tcol, NEG, val)
    rank = jnp.zeros((ns, tq), F32)
    for j in range(ns):
        other = val[j:j + 1, :]
        ahead = (other > val) | ((other == val) & (jrow > j))
        rank = rank + jnp.where(ahead, 1.0, 0.0)
    selneg = jnp.where(rank < float(n_sel), 0.0, NEG)
    if ns < LANE:
        selneg = jnp.concatenate([selneg, jnp.zeros((LANE - ns, tq), F32)], axis=0)
    sel_ref[...] = selneg.T.astype(sel_ref.dtype)


def nsa_cmp_attention(proj, kvc, ovt, *, B, S, tq):
    M = B * S
    nq = S // tq
    G, dk = NSA_GROUPS, NSA_HEAD_DIM
    ncp = S // CMP_STRIDE
    ns = S // SEL_BLOCK
    assert ns <= LANE
    n_sel = min(SEL_COUNT, ns)
    gw = NSA_HG * dk
    return pl.pallas_call(
        functools.partial(_nsa_cmp_kernel, tq=tq, ncp=ncp, ns=ns, n_sel=n_sel, scale=dk ** -0.5),
        grid=(B, G, nq),
        in_specs=[pl.BlockSpec((tq, gw), lambda b, g, i: (b * nq + i, OFF_NSA_Q // gw + g)),
                  pl.BlockSpec((None, None, None, ncp, dk), lambda b, g, i: (0, b, g, 0, 0)),
                  pl.BlockSpec((None, None, None, ncp, dk), lambda b, g, i: (1, b, g, 0, 0)),
                  pl.BlockSpec((LANE, ncp), lambda b, g, i: (0, 0))],
        out_specs=[pl.BlockSpec((tq, gw), lambda b, g, i: (b * nq + i, g)),
                   pl.BlockSpec((None, None, tq, LANE), lambda b, g, i: (b, g, i, 0))],
        out_shape=[jax.ShapeDtypeStruct((M, G * gw), BF16),
                   jax.ShapeDtypeStruct((B, G, S, LANE), BF16)],
        compiler_params=_cparams(("parallel", "parallel", "parallel")),
        name="nsa_cmp_attention",
    )(proj, kvc, kvc, ovt)


def _nsa_main_kernel(q_ref, sel_ref, ks_ref, vs_ref, kw_ref, vw_ref, ind_ref, bias_ref, gate_ref, oc_ref,
                     o_ref, m_ref, l_ref, acc_ref, *, tq, nback, scale):
    g = pl.program_id(1)
    qi = pl.program_id(2)
    dk = NSA_HEAD_DIM
    HG = NSA_HG
    q = jnp.concatenate([q_ref[:, h * dk:(h + 1) * dk] for h in range(HG)], axis=0)
    qa = jnp.concatenate([q, jnp.concatenate([sel_ref[...]] * HG, axis=0)], axis=1)
    row, col = _tri(tq, tq)
    row = jnp.concatenate([row] * HG, axis=0)
    col = jnp.concatenate([col] * HG, axis=0)
    bias0 = jnp.concatenate([bias_ref[h, 0] for h in range(HG)], axis=0)
    bias1 = jnp.concatenate([bias_ref[h, 1] for h in range(HG)], axis=0)
    gate = jax.nn.sigmoid(gate_ref[...].astype(F32))

    def finish():
        return acc_ref[...] / l_ref[...]

    _softmax_init(m_ref, l_ref, acc_ref)

    def sel_chunk(j):
        sl = pl.ds(pl.multiple_of(j * tq, tq), tq)
        ka = jnp.concatenate([ks_ref[sl, :], ind_ref[sl, :]], axis=1)
        return _dot_nt(qa, ka) * scale, vs_ref[sl, :]

    def sel_body(j, c):
        s, v = sel_chunk(j)
        _softmax_step(s, v, m_ref, l_ref, acc_ref)
        return c

    lax.fori_loop(0, jnp.maximum(qi - 1, 0), sel_body, 0)

    @pl.when(qi >= 1)
    def _():
        s, v = sel_chunk(qi - 1)
        _softmax_step(s + bias1, v, m_ref, l_ref, acc_ref)

    s, v = sel_chunk(qi)
    _softmax_step(jnp.where(row >= col, s + bias0, NEG), v, m_ref, l_ref, acc_ref)
    o_s = finish()

    _softmax_init(m_ref, l_ref, acc_ref)

    def win_chunk(j):
        sl = pl.ds(pl.multiple_of(j * tq, tq), tq)
        return _dot_nt(q, kw_ref[sl, :]) * scale, vw_ref[sl, :]

    @pl.when(qi >= nback)
    def _():
        s, v = win_chunk(qi - nback)
        if nback == 1:
            s = s + bias1
        _softmax_step(jnp.where(row < col, s, NEG), v, m_ref, l_ref, acc_ref)

    for back in range(nback - 1, 0, -1):
        @pl.when(qi >= back)
        def _(back=back):
            s, v = win_chunk(qi - back)
            if back == 1:
                s = s + bias1
            _softmax_step(s, v, m_ref, l_ref, acc_ref)

    s, v = win_chunk(qi)
    _softmax_step(jnp.where(row >= col, s + bias0, NEG), v, m_ref, l_ref, acc_ref)
    o_w = finish()

    for h in range(HG):
        c = g * HG + h
        sl = slice(h * tq, (h + 1) * tq)
        lane = lax.broadcasted_iota(jnp.int32, gate.shape, 1)
        gc = [jnp.sum(jnp.where(lane == br * NSA_HEADS + c, gate, 0.0), axis=-1, keepdims=True)
              for br in range(3)]
        o = gc[0] * oc_ref[:, h * dk:(h + 1) * dk].astype(F32) + gc[1] * o_s[sl] + gc[2] * o_w[sl]
        o_ref[:, h * dk:(h + 1) * dk] = o.astype(o_ref.dtype)


def nsa_main(proj, selneg, ind, btiles, o_c, *, B, S, tq):
    M = B * S
    nq = S // tq
    G, dk, HG = NSA_GROUPS, NSA_HEAD_DIM, NSA_HG
    gw = HG * dk
    assert WINDOW % tq == 0
    nback = WINDOW // tq
    kv = lambda n: pl.BlockSpec((S, dk), lambda b, g, i, n=n: (b, (OFF_NSA_KV + n * G * dk) // dk + g))
    return pl.pallas_call(
        functools.partial(_nsa_main_kernel, tq=tq, nback=nback, scale=dk ** -0.5),
        grid=(B, G, nq),
        in_specs=[pl.BlockSpec((tq, gw), lambda b, g, i: (b * nq + i, OFF_NSA_Q // gw + g)),
                  pl.BlockSpec((None, None, tq, LANE), lambda b, g, i: (b, g, i, 0)),
                  kv(2), kv(3), kv(4), kv(5),
                  pl.BlockSpec((S, LANE), lambda b, g, i: (0, 0)),
                  pl.BlockSpec((HG, 2, tq, tq), lambda b, g, i: (DSA_HEADS // HG + g, 0, 0, 0)),
                  pl.BlockSpec((tq, LANE), lambda b, g, i: (b * nq + i, OFF_NSA_GATE // LANE)),
                  pl.BlockSpec((tq, gw), lambda b, g, i: (b * nq + i, g))],
        out_specs=pl.BlockSpec((tq, gw), lambda b, g, i: (b * nq + i, g)),
        out_shape=jax.ShapeDtypeStruct((M, G * gw), BF16),
        scratch_shapes=[pltpu.VMEM((HG * tq, 1), F32), pltpu.VMEM((HG * tq, 1), F32),
                        pltpu.VMEM((HG * tq, dk), F32)],
        compiler_params=_cparams(("parallel", "parallel", "arbitrary")),
        name="nsa_main",
    )(proj, selneg, proj, proj, proj, proj, ind, btiles, proj, o_c)


def _merge_kernel(x_ref, oa_ref, ob_ref, oc_ref, g0_ref, g1_ref, g2_ref, wb_ref, wo_ref, gain_ref, o_ref):
    merged = None
    for o_r, g_r, br in ((oa_ref, g0_ref, 0), (ob_ref, g1_ref, 1), (oc_ref, g2_ref, 2)):
        t = jax.nn.sigmoid(g_r[...].astype(F32)) * _dot(o_r[...], wb_ref[br])
        merged = t if merged is None else merged + t
    y = _dot(merged.astype(BF16), wo_ref[...])
    o_ref[...] = x_ref[...] + _rms(y, gain_ref[...])


def merge_branches(x, o_a, o_b, o_c, proj, wb, wo, gain, *, tm):
    M, D = x.shape
    row = pl.BlockSpec((tm, D), lambda i: (i, 0))
    gate = lambda br: pl.BlockSpec((tm, D), lambda i, br=br: (i, OFF_BGATE // D + br))
    return pl.pallas_call(
        _merge_kernel,
        grid=(M // tm,),
        in_specs=[row, row, row, row, gate(0), gate(1), gate(2),
                  pl.BlockSpec(wb.shape, lambda i: (0, 0, 0)),
                  pl.BlockSpec(wo.shape, lambda i: (0, 0)),
                  pl.BlockSpec((1, D), lambda i: (0, 0))],
        out_specs=row,
        out_shape=jax.ShapeDtypeStruct((M, D), F32),
        compiler_params=_cparams(("parallel",)),
        name="merge_branches",
    )(x, o_a, o_b, o_c, proj, proj, proj, wb, wo, gain)


def _cross_kernel(x_ref, kv_ref, wq_ref, wo_ref, gpre_ref, gpost_ref, o_ref, *, scale):
    x = x_ref[...]
    h = _rms(x, gpre_ref[...]).astype(BF16)
    q = _dot(h, wq_ref[...]).astype(BF16)
    dh = CROSS_HEAD_DIM
    outs = []
    for hd in range(CROSS_HEADS):
        k = kv_ref[:, hd * dh:(hd + 1) * dh]
        v = kv_ref[:, D_MODEL + hd * dh:D_MODEL + (hd + 1) * dh]
        s = _dot_nt(q[:, hd * dh:(hd + 1) * dh], k) * scale
        p = jnp.exp(s - jnp.max(s, axis=-1, keepdims=True))
        p = p / jnp.sum(p, axis=-1, keepdims=True)
        outs.append(_dot(p.astype(BF16), v).astype(BF16))
    y = _dot(jnp.concatenate(outs, axis=1), wo_ref[...])
    o_ref[...] = x + _rms(y, gpost_ref[...])


def cross_attention(x, kv, wq, wo, gpre, gpost, *, S, tm):
    M, D = x.shape
    mlen = kv.shape[0] // (M // S)
    per_b = S // tm
    return pl.pallas_call(
        functools.partial(_cross_kernel, scale=CROSS_HEAD_DIM ** -0.5),
        grid=(M // tm,),
        in_specs=[pl.BlockSpec((tm, D), lambda i: (i, 0)),
                  pl.BlockSpec((mlen, 2 * D), lambda i: (i // per_b, 0)),
                  pl.BlockSpec(wq.shape, lambda i: (0, 0)),
                  pl.BlockSpec(wo.shape, lambda i: (0, 0)),
                  pl.BlockSpec((1, D), lambda i: (0, 0)),
                  pl.BlockSpec((1, D), lambda i: (0, 0))],
        out_specs=pl.BlockSpec((tm, D), lambda i: (i, 0)),
        out_shape=jax.ShapeDtypeStruct((M, D), F32),
        compiler_params=_cparams(("parallel",)),
        name="cross_attention",
    )(x, kv, wq, wo, gpre, gpost)


def _mlp_kernel(x_ref, w1_ref, w2_ref, gpre_ref, gpost_ref, o_ref, h_ref, acc_ref):
    j = pl.program_id(1)

    @pl.when(j == 0)
    def _():
        h_ref[...] = _rms(x_ref[...], gpre_ref[...]).astype(BF16)
        acc_ref[...] = jnp.zeros(acc_ref.shape, F32)

    a = jnp.maximum(_dot(h_ref[...], w1_ref[...]), 0.0)
    acc_ref[...] += _dot((a * a).astype(BF16), w2_ref[...])

    @pl.when(j == pl.num_programs(1) - 1)
    def _():
        o_ref[...] = x_ref[...] + _rms(acc_ref[...], gpost_ref[...])


def mlp(x, w1, w2, gpre, gpost, *, tm, tf):
    M, D = x.shape
    FF = w1.shape[1]
    return pl.pallas_call(
        _mlp_kernel,
        grid=(M // tm, FF // tf),
        in_specs=[pl.BlockSpec((tm, D), lambda i, j: (i, 0)),
                  pl.BlockSpec((D, tf), lambda i, j: (0, j)),
                  pl.BlockSpec((tf, D), lambda i, j: (j, 0)),
                  pl.BlockSpec((1, D), lambda i, j: (0, 0)),
                  pl.BlockSpec((1, D), lambda i, j: (0, 0))],
        out_specs=pl.BlockSpec((tm, D), lambda i, j: (i, 0)),
        out_shape=jax.ShapeDtypeStruct((M, D), F32),
        scratch_shapes=[pltpu.VMEM((tm, D), BF16), pltpu.VMEM((tm, D), F32)],
        compiler_params=_cparams(("parallel", "arbitrary")),
        name="mlp",
    )(x, w1, w2, gpre, gpost)


def _rot_half_cols(w):
    half = w.shape[-1] // 2
    return jnp.concatenate([-w[..., half:], w[..., :half]], axis=-1)


def _pack_w_in(w):
    widths = (MLA_Q_LORA, MLA_KV_LORA, MLA_ROPE, 1024, 1024, 1024, IDX_HEADS * IDX_DIM, IDX_DIM, IDX_HEADS,
              1024, 256, 256, 256, 256, 256, 256, 3 * NSA_HEADS, N_BRANCH * D_MODEL)
    offs = np.concatenate([[0], np.cumsum(widths)])
    seg = [w[:, offs[i]:offs[i + 1]] for i in range(len(widths))]
    (q_lat, kv_lat, k_rope, dsa_q, dsa_k, dsa_v, idx_q, idx_k, idx_w,
     nsa_q, kc, vc, ks, vs, kw, vw, nsa_gate, bgate) = seg
    z = lambda n: jnp.zeros((w.shape[0], n), w.dtype)
    cols = [dsa_q, dsa_k, dsa_v, nsa_q, bgate, idx_q, kc, vc, ks, vs, kw, vw,
            q_lat, kv_lat, k_rope, z(LANE - MLA_ROPE), _rot_half_cols(k_rope), z(LANE - MLA_ROPE), z(LANE),
            idx_k, idx_k, idx_w, z(LANE - IDX_HEADS), nsa_gate, z(LANE - 3 * NSA_HEADS)]
    out = jnp.concatenate(cols, axis=1)
    out = jnp.concatenate([out, z(N_PACK - out.shape[1])], axis=1)
    return out.astype(BF16)


def _pack_mla_weights(w_uq, w_ukv):
    H = MLA_HEADS
    uq = w_uq.reshape(MLA_Q_LORA, H, MLA_NOPE + MLA_ROPE)
    nope = uq[:, :, :MLA_NOPE].reshape(MLA_Q_LORA, H * MLA_NOPE)
    rope = uq[:, :, MLA_NOPE:]
    pad = jnp.zeros((MLA_Q_LORA, H, LANE - MLA_ROPE), w_uq.dtype)
    wqr = jnp.concatenate([rope, pad], axis=-1).reshape(MLA_Q_LORA, H * LANE)
    wqrr = jnp.concatenate([_rot_half_cols(rope), pad], axis=-1).reshape(MLA_Q_LORA, H * LANE)
    ukv = w_ukv.reshape(MLA_KV_LORA, H, MLA_NOPE + MLA_V)
    wkv = jnp.concatenate([ukv[:, :, :MLA_NOPE].reshape(MLA_KV_LORA, H * MLA_NOPE),
                           ukv[:, :, MLA_NOPE:].reshape(MLA_KV_LORA, H * MLA_V)], axis=1)
    return nope.astype(BF16), wqr.astype(BF16), wqrr.astype(BF16), wkv.astype(BF16)


def _constants(S):
    half = MLA_ROPE // 2
    inv_freq = ROPE_BASE ** (-np.arange(0, MLA_ROPE, 2, dtype=np.float32) / MLA_ROPE)
    freq = np.zeros((1, LANE), np.float32)
    freq[0, :half] = inv_freq
    freq[0, half:2 * half] = inv_freq
    ncp = S // CMP_STRIDE
    ns = S // SEL_BLOCK
    c_start = np.arange(ncp) * CMP_STRIDE
    j_start = np.arange(ns) * SEL_BLOCK
    overlap = ((c_start[None, :] < j_start[:, None] + SEL_BLOCK) &
               (c_start[None, :] + CMP_LEN > j_start[:, None])).astype(np.float32)
    overlap[:, ncp - 1] = 0.0
    ovt = np.zeros((LANE, ncp), np.float32)
    ovt[:ns] = overlap
    ind = np.zeros((S, LANE), np.float32)
    ind[np.arange(S), np.arange(S) // SEL_BLOCK] = 1.0
    return jnp.asarray(freq), jnp.asarray(ovt, BF16), jnp.asarray(ind, BF16)


def kernel(x, mem, positions, rel_bias, norm_gains, w_in, mla_q_norm, mla_kv_norm, mla_w_uq, mla_w_ukv,
           nsa_cmp_pos, nsa_cmp_w1, nsa_cmp_w2, w_branch, w_out, cross_wq, cross_wkv, cross_wo, mlp_w1, mlp_w2):
    B, S, D = x.shape
    M = B * S
    depth = w_in.shape[0]
    tq = TQ
    assert S % tq == 0 and D == D_MODEL
    topk = min(DSA_TOPK_MAX, S // 4)
    freq, ovt, ind = _constants(S)
    btiles = bias_tiles(rel_bias, tq)
    pos = positions.reshape(M, 1)
    xf = x.reshape(M, D)
    memf = mem.reshape(B * mem.shape[1], D)
    tm_big = 1024 if M % 1024 == 0 else tq
    tm = 512 if M % 512 == 0 else tq
    ncp = S // CMP_STRIDE
    row = lambda v: v.reshape(1, -1)

    for l in range(depth):
        g = norm_gains[l]
        proj = norm_matmul(xf, row(g[0]), _pack_w_in(w_in[l]), tm=tm_big, tn=TN_IN)

        wqn, wqr, wqrr, wkv = _pack_mla_weights(mla_w_uq[l], mla_w_ukv[l])
        qn, qr, kn, vv, kr = mla_prep(proj, pos, freq, row(mla_q_norm[l]), row(mla_kv_norm[l]),
                                      wqn, wqr, wqrr, wkv, tm=tm)
        o_a = mla_attention(qn, qr, kn, kr, vv, B=B, S=S, tq=tq)

        maskb = dsa_index_mask(proj, B=B, S=S, tqi=128, tk=tq, topk=topk)
        o_b = dsa_attention(proj, maskb, btiles, B=B, S=S, tq=tq)

        kvc_in = jnp.stack([
            proj[:, OFF_NSA_KV + a * 256:OFF_NSA_KV + (a + 1) * 256].reshape(B * ncp, CMP_STRIDE * 256)
            for a in range(2)])
        posflat = jnp.broadcast_to(nsa_cmp_pos[l].reshape(2, 1, CMP_LEN * NSA_HEAD_DIM),
                                   (2, 8, CMP_LEN * NSA_HEAD_DIM)).astype(BF16)
        kvc = nsa_compress(kvc_in, posflat, nsa_cmp_w1[l].astype(BF16), nsa_cmp_w2[l].astype(BF16), B=B, S=S)
        o_cmp, selneg = nsa_cmp_attention(proj, kvc, ovt, B=B, S=S, tq=tq)
        o_c = nsa_main(proj, selneg, ind, btiles, o_cmp, B=B, S=S, tq=tq)

        xf = merge_branches(xf, o_a, o_b, o_c, proj, w_branch[l].astype(BF16), w_out[l].astype(BF16),
                            row(g[1]), tm=tm)

        mkv = norm_matmul(memf, row(g[3]), cross_wkv[l].astype(BF16), tm=memf.shape[0] // B, tn=1024)
        xf = cross_attention(xf, mkv, cross_wq[l].astype(BF16), cross_wo[l].astype(BF16),
                             row(g[2]), row(g[4]), S=S, tm=tm)

        xf = mlp(xf, mlp_w1[l].astype(BF16), mlp_w2[l].astype(BF16), row(g[5]), row(g[6]), tm=tm, tf=1024)

    return xf.reshape(B, S, D)
```

```python
import functools
import math

import numpy as np
import jax
import jax.numpy as jnp
from jax import lax
from jax.experimental import pallas as pl
from jax.experimental.pallas import tpu as pltpu

F32 = jnp.float32
BF16 = jnp.bfloat16

D_MODEL = 1024
EPS = 1e-6
MLA_HEADS = 8
MLA_Q_LORA = 384
MLA_KV_LORA = 256
MLA_NOPE = 128
MLA_ROPE = 64
MLA_V = 128
ROPE_BASE = 10000.0
DSA_HEADS = 8
DSA_HEAD_DIM = 128
IDX_HEADS = 8
IDX_DIM = 64
DSA_TOPK_MAX = 256
NSA_HEADS = 8
NSA_GROUPS = 2
NSA_HG = NSA_HEADS // NSA_GROUPS
NSA_HEAD_DIM = 128
CMP_STRIDE = 16
CMP_LEN = 2 * CMP_STRIDE
CMP_HIDDEN = 256
SEL_BLOCK = 64
SEL_COUNT = 16
WINDOW = 512
REL_BUCKETS = 32
REL_MAX_DIST = 128
REL_HEADS = DSA_HEADS + NSA_HEADS
CROSS_HEADS = 4
CROSS_HEAD_DIM = D_MODEL // CROSS_HEADS
D_FF = 4 * D_MODEL
N_BRANCH = 3

LANE = 128
NEG = -1e30
VMEM_LIMIT = 48 * 1024 * 1024

TQ = 512
TQ_NSA = 256
BISECT_ITERS = 36

OFF_DSA_Q = 0
OFF_DSA_K = 1024
OFF_DSA_V = 2048
OFF_NSA_Q = 3072
OFF_BGATE = 4096
OFF_IDX_Q = 7168
OFF_NSA_KV = 7680
OFF_MLA = 9216
OFF_IDX_K = 10240
OFF_IDX_W = 10368
OFF_NSA_GATE = 10496
N_PACK = 10752
TN_IN = 1536


def _cparams(sem):
    return pltpu.CompilerParams(dimension_semantics=sem, vmem_limit_bytes=VMEM_LIMIT)


def _rms(x, g):
    return x * lax.rsqrt(jnp.mean(x * x, axis=-1, keepdims=True) + EPS) * g


def _dot(a, b):
    return jnp.dot(a, b, preferred_element_type=F32)


def _dot_nt(a, b):
    return lax.dot_general(a, b, (((1,), (1,)), ((), ())), preferred_element_type=F32)


def _norm_mm_kernel(x_ref, g_ref, w_ref, o_ref, xn_ref):
    @pl.when(pl.program_id(1) == 0)
    def _():
        xn_ref[...] = _rms(x_ref[...].astype(F32), g_ref[...]).astype(BF16)

    o_ref[...] = _dot(xn_ref[...], w_ref[...]).astype(o_ref.dtype)


def norm_matmul(x, g, w, *, tm, tn):
    M, K = x.shape
    N = w.shape[1]
    return pl.pallas_call(
        _norm_mm_kernel,
        grid=(M // tm, N // tn),
        in_specs=[pl.BlockSpec((tm, K), lambda i, j: (i, 0)),
                  pl.BlockSpec((1, K), lambda i, j: (0, 0)),
                  pl.BlockSpec((K, tn), lambda i, j: (0, j))],
        out_specs=pl.BlockSpec((tm, tn), lambda i, j: (i, j)),
        out_shape=jax.ShapeDtypeStruct((M, N), BF16),
        scratch_shapes=[pltpu.VMEM((tm, K), BF16)],
        compiler_params=_cparams(("parallel", "arbitrary")),
        name="norm_matmul",
    )(x, g, w)


def _bias_tiles_kernel(tab_ref, bkt_ref, o_ref, *, inv_scale):
    h = pl.program_id(0)
    bkt = bkt_ref[...]
    far = tab_ref[REL_BUCKETS - 1, h]
    out = jnp.zeros(bkt.shape, F32)
    for b in range(REL_BUCKETS - 1):
        out = jnp.where(bkt == b, (tab_ref[b, h] - far) * inv_scale, out)
    o_ref[...] = out


def _bucket_tiles(tq):
    i = np.arange(tq)[:, None]
    j = np.arange(tq)[None, :]
    d = np.stack([i - j, tq + i - j]).astype(np.int32)
    n = jnp.maximum(jnp.asarray(d), 0)
    exact = REL_BUCKETS // 2
    nf = jnp.maximum(n, 1).astype(F32)
    log_b = exact + (jnp.log(nf / exact) / math.log(REL_MAX_DIST / exact) * (REL_BUCKETS - exact)).astype(jnp.int32)
    return jnp.where(n < exact, n, jnp.minimum(log_b, REL_BUCKETS - 1)).astype(jnp.int32)


def bias_tiles(rel_bias, tq):
    assert tq >= REL_MAX_DIST
    bkt = _bucket_tiles(tq)
    assert DSA_HEAD_DIM == NSA_HEAD_DIM
    return pl.pallas_call(
        functools.partial(_bias_tiles_kernel, inv_scale=DSA_HEAD_DIM ** 0.5),
        grid=(REL_HEADS,),
        in_specs=[pl.BlockSpec(memory_space=pltpu.SMEM),
                  pl.BlockSpec((2, tq, tq), lambda h: (0, 0, 0))],
        out_specs=pl.BlockSpec((None, 2, tq, tq), lambda h: (h, 0, 0, 0)),
        out_shape=jax.ShapeDtypeStruct((REL_HEADS, 2, tq, tq), F32),
        compiler_params=_cparams(("arbitrary",)),
        name="bias_tiles",
    )(rel_bias, bkt)


def _mla_prep_kernel(lat_ref, pos_ref, freq_ref, qg_ref, kvg_ref, wqn_ref, wqr_ref, wqrr_ref, wkv_ref,
                     qn_ref, qr_ref, kn_ref, v_ref, kr_ref):
    lat = lat_ref[...].astype(F32)
    qlat = _rms(lat[:, :MLA_Q_LORA], qg_ref[...]).astype(BF16)
    kvlat = _rms(lat[:, MLA_Q_LORA:MLA_Q_LORA + MLA_KV_LORA], kvg_ref[...]).astype(BF16)
    o = MLA_Q_LORA + MLA_KV_LORA
    kr = lat[:, o:o + LANE]
    krr = lat[:, o + LANE:o + 2 * LANE]
    ang = pos_ref[...].astype(F32) * freq_ref[...]
    cos, sin = jnp.cos(ang), jnp.sin(ang)
    cos8 = jnp.concatenate([cos] * MLA_HEADS, axis=1)
    sin8 = jnp.concatenate([sin] * MLA_HEADS, axis=1)
    qn_ref[...] = _dot(qlat, wqn_ref[...]).astype(BF16)
    qr_ref[...] = (_dot(qlat, wqr_ref[...]) * cos8 + _dot(qlat, wqrr_ref[...]) * sin8).astype(BF16)
    kv = _dot(kvlat, wkv_ref[...])
    nk = MLA_HEADS * MLA_NOPE
    kn_ref[...] = kv[:, :nk].astype(BF16)
    v_ref[...] = kv[:, nk:].astype(BF16)
    kr_ref[...] = (kr * cos + krr * sin).astype(BF16)


def mla_prep(proj, pos, freq, qg, kvg, wqn, wqr, wqrr, wkv, *, tm):
    M = proj.shape[0]
    HD = MLA_HEADS * LANE
    full = lambda a: pl.BlockSpec(a.shape, lambda i: (0, 0))
    outs = [jax.ShapeDtypeStruct((M, HD), BF16)] * 4 + [jax.ShapeDtypeStruct((M, LANE), BF16)]
    return pl.pallas_call(
        _mla_prep_kernel,
        grid=(M // tm,),
        in_specs=[pl.BlockSpec((tm, 1024), lambda i: (i, OFF_MLA // 1024)),
                  pl.BlockSpec((tm, 1), lambda i: (i, 0)),
                  full(freq), full(qg), full(kvg), full(wqn), full(wqr), full(wqrr), full(wkv)],
        out_specs=[pl.BlockSpec((tm, HD), lambda i: (i, 0))] * 4 + [pl.BlockSpec((tm, LANE), lambda i: (i, 0))],
        out_shape=outs,
        compiler_params=_cparams(("parallel",)),
        name="mla_prep",
    )(proj, pos, freq, qg, kvg, wqn, wqr, wqrr, wkv)


def _softmax_init(m_ref, acc_ref):
    m_ref[...] = jnp.full(m_ref.shape, NEG, F32)
    acc_ref[...] = jnp.zeros(acc_ref.shape, F32)


def _with_ones(v):
    return jnp.concatenate([v, jnp.ones(v.shape, v.dtype)], axis=1)


def _softmax_step(s, v, m_ref, acc_ref, scale):
    c = scale * math.log2(math.e)
    m_prev = m_ref[...]
    m_new = jnp.maximum(m_prev, jnp.max(s, axis=-1, keepdims=True))
    alpha = jnp.exp2((m_prev - m_new) * c)
    p = jnp.exp2((s - jnp.tile(m_new, (1, s.shape[1] // LANE))) * c)
    acc_ref[...] = jnp.tile(alpha, (1, 2)) * acc_ref[...] + _dot(p.astype(BF16), _with_ones(v))
    m_ref[...] = m_new


def _softmax_finish(acc_ref):
    acc = acc_ref[...]
    return acc[:, :LANE] / acc[:, LANE:]


def _tri(tq, tk):
    row = lax.broadcasted_iota(jnp.int32, (tq, tk), 0)
    col = lax.broadcasted_iota(jnp.int32, (tq, tk), 1)
    return row, col


def _mla_attn_kernel(qn_ref, qr_ref, kn_ref, kr_ref, v_ref, o_ref, m_ref, acc_ref, *, tq, scale):
    qi = pl.program_id(2)
    q = jnp.concatenate([qn_ref[...], qr_ref[...]], axis=-1)
    _softmax_init(m_ref, acc_ref)

    def chunk(j):
        sl = pl.ds(pl.multiple_of(j * tq, tq), tq)
        k = jnp.concatenate([kn_ref[sl, :], kr_ref[sl, :]], axis=-1)
        return _dot_nt(q, k), v_ref[sl, :]

    def body(j, c):
        s, v = chunk(j)
        _softmax_step(s, v, m_ref, acc_ref, scale)
        return c

    lax.fori_loop(0, qi, body, 0)
    s, v = chunk(qi)
    row, col = _tri(tq, tq)
    _softmax_step(jnp.where(row >= col, s, NEG), v, m_ref, acc_ref, scale)
    o_ref[...] = _softmax_finish(acc_ref).astype(o_ref.dtype)


def mla_attention(qn, qr, kn, kr, v, *, B, S, tq):
    M = B * S
    nq = S // tq
    H = MLA_HEADS
    scale = (MLA_NOPE + MLA_ROPE) ** -0.5
    qspec = pl.BlockSpec((tq, LANE), lambda b, h, i: (b * nq + i, h))
    kspec = pl.BlockSpec((S, LANE), lambda b, h, i: (b, h))
    return pl.pallas_call(
        functools.partial(_mla_attn_kernel, tq=tq, scale=scale),
        grid=(B, H, nq),
        in_specs=[qspec, qspec, kspec, pl.BlockSpec((S, LANE), lambda b, h, i: (b, 0)), kspec],
        out_specs=qspec,
        out_shape=jax.ShapeDtypeStruct((M, H * MLA_V), BF16),
        scratch_shapes=[pltpu.VMEM((tq, LANE), F32), pltpu.VMEM((tq, 2 * LANE), F32)],
        compiler_params=_cparams(("parallel", "parallel", "arbitrary")),
        name="mla_attention",
    )(qn, qr, kn, kr, v)


def _dsa_index_kernel(iq_ref, ik_ref, iw_ref, mb_ref, sc_ref, *, tq, tk, S, topk, iters):
    t0 = pl.program_id(1) * tq
    iq = iq_ref[...]
    ik = ik_ref[...]
    iw = iw_ref[...].astype(F32)
    lane = lax.broadcasted_iota(jnp.int32, (tq, LANE), 1)
    score = jnp.zeros((tq, S), F32)
    for h in range(IDX_HEADS):
        blk = iq[:, (h // 2) * LANE:(h // 2 + 1) * LANE]
        keep = (lane >= IDX_DIM) if h % 2 else (lane < IDX_DIM)
        qh = jnp.where(keep, blk, jnp.zeros_like(blk))
        score = score + iw[:, h:h + 1] * jnp.maximum(_dot_nt(qh, ik), 0.0)
    row = lax.broadcasted_iota(jnp.int32, (tq, S), 0) + t0
    col = lax.broadcasted_iota(jnp.int32, (tq, S), 1)
    causal = col <= row
    sc_ref[...] = jnp.where(causal, score, NEG)
    kf = float(topk)

    def count(pred):
        return jnp.sum(jnp.where(pred, 1.0, 0.0), axis=-1, keepdims=True)

    lo0 = jnp.min(jnp.where(causal, score, -NEG), axis=-1, keepdims=True)
    mx = jnp.max(sc_ref[...], axis=-1, keepdims=True)
    hi0 = mx + (jnp.abs(mx) * 1e-6 + 1.0)

    def bisect(_, c):
        lo, hi = c
        mid = 0.5 * (lo + hi)
        ge = count(sc_ref[...] >= mid) >= kf
        return jnp.where(ge, mid, lo), jnp.where(ge, hi, mid)

    lo, hi = lax.fori_loop(0, iters, bisect, (lo0, hi0))
    n_lo = count(sc_ref[...] >= lo)
    tied = jnp.max(n_lo) > kf

    def write(sel):
        mb = jnp.where(sel, 0.0, NEG).astype(mb_ref.dtype)
        for j in range(S // tk):
            mb_ref[j] = mb[:, j * tk:(j + 1) * tk]

    @pl.when(jnp.logical_not(tied))
    def _():
        write(sc_ref[...] >= lo)

    @pl.when(tied)
    def _():
        sc = sc_ref[...]
        above = sc >= hi
        band = (sc >= lo) & jnp.logical_not(above)
        need = kf - count(above)
        colf = col.astype(F32)

        def ibisect(_, c):
            jlo, jhi = c
            jm = jnp.floor(0.5 * (jlo + jhi))
            ok = count(band & (colf <= jm)) >= need
            return jnp.where(ok, jlo, jm), jnp.where(ok, jm, jhi)

        steps = max(1, int(math.ceil(math.log2(S))) + 1)
        _, jhi = lax.fori_loop(0, steps, ibisect,
                               (jnp.full((tq, 1), -1.0, F32), jnp.full((tq, 1), float(S - 1), F32)))
        write(above | (band & (colf <= jhi)))


def dsa_index_mask(proj, *, B, S, tqi, tk, topk):
    nq = S // tqi
    nk = S // tk
    return pl.pallas_call(
        functools.partial(_dsa_index_kernel, tq=tqi, tk=tk, S=S, topk=topk, iters=BISECT_ITERS),
        grid=(B, nq),
        in_specs=[pl.BlockSpec((tqi, 512), lambda b, i: (b * nq + i, OFF_IDX_Q // 512)),
                  pl.BlockSpec((S, LANE), lambda b, i: (b, OFF_IDX_K // LANE)),
                  pl.BlockSpec((tqi, LANE), lambda b, i: (b * nq + i, OFF_IDX_W // LANE))],
        out_specs=pl.BlockSpec((None, nk, tqi, tk), lambda b, i: (b, 0, i, 0)),
        out_shape=jax.ShapeDtypeStruct((B, nk, S, tk), BF16),
        scratch_shapes=[pltpu.VMEM((tqi, S), F32)],
        compiler_params=_cparams(("parallel", "parallel")),
        name="dsa_index_mask",
    )(proj, proj, proj)


def _dsa_attn_kernel(q_ref, k_ref, v_ref, mb_ref, bias_ref, o_ref, m_ref, acc_ref, *, tq, scale):
    qi = pl.program_id(2)
    q = q_ref[...]
    _softmax_init(m_ref, acc_ref)

    def chunk(j):
        sl = pl.ds(pl.multiple_of(j * tq, tq), tq)
        s = _dot_nt(q, k_ref[sl, :]) + mb_ref[j].astype(F32)
        return s, v_ref[sl, :]

    def body(j, c):
        s, v = chunk(j)
        _softmax_step(s, v, m_ref, acc_ref, scale)
        return c

    lax.fori_loop(0, jnp.maximum(qi - 1, 0), body, 0)

    @pl.when(qi >= 1)
    def _():
        s, v = chunk(qi - 1)
        _softmax_step(s + bias_ref[1], v, m_ref, acc_ref, scale)

    s, v = chunk(qi)
    _softmax_step(s + bias_ref[0], v, m_ref, acc_ref, scale)
    o_ref[...] = _softmax_finish(acc_ref).astype(o_ref.dtype)


def dsa_attention(proj, maskb, btiles, *, B, S, tq):
    M = B * S
    nq = S // tq
    H = DSA_HEADS
    scale = DSA_HEAD_DIM ** -0.5
    return pl.pallas_call(
        functools.partial(_dsa_attn_kernel, tq=tq, scale=scale),
        grid=(B, H, nq),
        in_specs=[pl.BlockSpec((tq, LANE), lambda b, h, i: (b * nq + i, OFF_DSA_Q // LANE + h)),
                  pl.BlockSpec((S, LANE), lambda b, h, i: (b, OFF_DSA_K // LANE + h)),
                  pl.BlockSpec((S, LANE), lambda b, h, i: (b, OFF_DSA_V // LANE + h)),
                  pl.BlockSpec((None, nq, tq, tq), lambda b, h, i: (b, 0, i, 0)),
                  pl.BlockSpec((None, 2, tq, tq), lambda b, h, i: (h, 0, 0, 0))],
        out_specs=pl.BlockSpec((tq, LANE), lambda b, h, i: (b * nq + i, h)),
        out_shape=jax.ShapeDtypeStruct((M, H * DSA_HEAD_DIM), BF16),
        scratch_shapes=[pltpu.VMEM((tq, LANE), F32), pltpu.VMEM((tq, 2 * LANE), F32)],
        compiler_params=_cparams(("parallel", "parallel", "arbitrary")),
        name="dsa_attention",
    )(proj, proj, proj, maskb, btiles)


def _nsa_compress_kernel(x_ref, pos_ref, w1_ref, w2_ref, o_ref, *, ncp):
    dk = NSA_HEAD_DIM
    posw = _dot(pos_ref[...], w1_ref[...])[0:1, :]
    for g in range(NSA_GROUPS):
        lo = jnp.zeros((ncp, CMP_HIDDEN), F32)
        hi = jnp.zeros((ncp, CMP_HIDDEN), F32)
        for l in range(CMP_STRIDE):
            xs = x_ref[:, (l * NSA_GROUPS + g) * dk:(l * NSA_GROUPS + g + 1) * dk]
            lo = lo + _dot(xs, w1_ref[l * dk:(l + 1) * dk, :])
            hi = hi + _dot(xs, w1_ref[(CMP_STRIDE + l) * dk:(CMP_STRIDE + l + 1) * dk, :])
        hid = lo + pltpu.roll(hi, ncp - 1, 0) + posw
        o_ref[g] = _dot(jax.nn.gelu(hid).astype(BF16), w2_ref[...]).astype(o_ref.dtype)


def nsa_compress(xkv, posflat, w1, w2, *, B, S):
    ncp = S // CMP_STRIDE
    G, dk = NSA_GROUPS, NSA_HEAD_DIM
    width = CMP_STRIDE * G * dk
    return pl.pallas_call(
        functools.partial(_nsa_compress_kernel, ncp=ncp),
        grid=(2, B),
        in_specs=[pl.BlockSpec((None, ncp, width), lambda a, b: (a, b, 0)),
                  pl.BlockSpec((None, 8, CMP_LEN * dk), lambda a, b: (a, 0, 0)),
                  pl.BlockSpec((None, CMP_LEN * dk, CMP_HIDDEN), lambda a, b: (a, 0, 0)),
                  pl.BlockSpec((None, CMP_HIDDEN, dk), lambda a, b: (a, 0, 0))],
        out_specs=pl.BlockSpec((None, None, G, ncp, dk), lambda a, b: (a, b, 0, 0, 0)),
        out_shape=jax.ShapeDtypeStruct((2, B, G, ncp, dk), BF16),
        compiler_params=_cparams(("parallel", "parallel")),
        name="nsa_compress",
    )(xkv, posflat, w1, w2)


def _nsa_cmp_kernel(q_ref, kc_ref, vc_ref, ov_ref, oc_ref, sel_ref, *, tq, ncp, ns, n_sel, scale):
    t0 = pl.program_id(2) * tq
    dk = NSA_HEAD_DIM
    kc = kc_ref[...]
    vc = vc_ref[...]
    trow = lax.broadcasted_iota(jnp.int32, (tq, ncp), 0) + t0
    ccol = lax.broadcasted_iota(jnp.int32, (tq, ncp), 1)
    vis = (ccol * CMP_STRIDE + (CMP_LEN - 1)) <= trow
    psum = jnp.zeros((tq, ncp), F32)
    for h in range(NSA_HG):
        s = jnp.where(vis, _dot_nt(q_ref[:, h * dk:(h + 1) * dk], kc) * scale, NEG)
        m = jnp.max(s, axis=-1, keepdims=True)
        p = jnp.where(vis, jnp.exp(s - m), 0.0)
        d = jnp.sum(p, axis=-1, keepdims=True)
        p = p / jnp.where(d > 0, d, 1.0)
        oc_ref[:, h * dk:(h + 1) * dk] = _dot(p.astype(BF16), vc).astype(oc_ref.dtype)
        psum = psum + p
    p_hi = psum.astype(BF16)
    p_lo = (psum - p_hi.astype(F32)).astype(BF16)
    ov = ov_ref[...]
    imp = _dot_nt(ov, p_hi) + _dot_nt(ov, p_lo)
    imp = imp[:ns]
    jrow = lax.broadcasted_iota(jnp.int32, (ns, tq), 0)
    tcol = lax.broadcasted_iota(jnp.int32, (ns, tq), 1) + t0
    blk_t = tcol // SEL_BLOCK
    forced = (jrow == 0) | (jrow == blk_t) | (jrow == blk_t - 1)
    val = jnp.where(forced, -NEG, imp)
    val = jnp.where(jrow * SEL_BLOCK > tcol, NEG, val)
    rank = jnp.zeros((ns, tq), F32)
    for j in range(ns):
        other = val[j:j + 1, :]
        ahead = (other > val) | ((other == val) & (jrow > j))
        rank = rank + jnp.where(ahead, 1.0, 0.0)
    selneg = jnp.where(rank < float(n_sel), 0.0, NEG)
    if ns < LANE:
        selneg = jnp.concatenate([selneg, jnp.zeros((LANE - ns, tq), F32)], axis=0)
    sel_ref[...] = selneg.T.astype(sel_ref.dtype)


def nsa_cmp_attention(proj, kvc, ovt, *, B, S, tq):
    M = B * S
    nq = S // tq
    G, dk = NSA_GROUPS, NSA_HEAD_DIM
    ncp = S // CMP_STRIDE
    ns = S // SEL_BLOCK
    assert ns <= LANE
    n_sel = min(SEL_COUNT, ns)
    gw = NSA_HG * dk
    return pl.pallas_call(
        functools.partial(_nsa_cmp_kernel, tq=tq, ncp=ncp, ns=ns, n_sel=n_sel, scale=dk ** -0.5),
        grid=(B, G, nq),
        in_specs=[pl.BlockSpec((tq, gw), lambda b, g, i: (b * nq + i, OFF_NSA_Q // gw + g)),
                  pl.BlockSpec((None, None, None, ncp, dk), lambda b, g, i: (0, b, g, 0, 0)),
                  pl.BlockSpec((None, None, None, ncp, dk), lambda b, g, i: (1, b, g, 0, 0)),
                  pl.BlockSpec((LANE, ncp), lambda b, g, i: (0, 0))],
        out_specs=[pl.BlockSpec((tq, gw), lambda b, g, i: (b * nq + i, g)),
                   pl.BlockSpec((None, None, tq, LANE), lambda b, g, i: (b, g, i, 0))],
        out_shape=[jax.ShapeDtypeStruct((M, G * gw), BF16),
                   jax.ShapeDtypeStruct((B, G, S, LANE), BF16)],
        compiler_params=_cparams(("parallel", "parallel", "parallel")),
        name="nsa_cmp_attention",
    )(proj, kvc, kvc, ovt)


def _nsa_main_kernel(q_ref, sel_ref, ks_ref, vs_ref, kw_ref, vw_ref, ind_ref, bias_ref, gate_ref, oc_ref,
                     o_ref, m_ref, acc_ref, *, tq, nback, scale):
    g = pl.program_id(1)
    qi = pl.program_id(2)
    dk = NSA_HEAD_DIM
    HG = NSA_HG
    q = jnp.concatenate([q_ref[:, h * dk:(h + 1) * dk] for h in range(HG)], axis=0)
    qa = jnp.concatenate([q, jnp.concatenate([sel_ref[...]] * HG, axis=0)], axis=1)
    row, col = _tri(tq, tq)
    row = jnp.concatenate([row] * HG, axis=0)
    col = jnp.concatenate([col] * HG, axis=0)
    bias0 = jnp.concatenate([bias_ref[h, 0] for h in range(HG)], axis=0)
    bias1 = jnp.concatenate([bias_ref[h, 1] for h in range(HG)], axis=0)
    gate = jax.nn.sigmoid(gate_ref[...].astype(F32))

    _softmax_init(m_ref, acc_ref)

    def sel_chunk(j):
        sl = pl.ds(pl.multiple_of(j * tq, tq), tq)
        ka = jnp.concatenate([ks_ref[sl, :], ind_ref[sl, :]], axis=1)
        return _dot_nt(qa, ka), vs_ref[sl, :]

    def sel_body(j, c):
        s, v = sel_chunk(j)
        _softmax_step(s, v, m_ref, acc_ref, scale)
        return c

    lax.fori_loop(0, jnp.maximum(qi - 1, 0), sel_body, 0)

    @pl.when(qi >= 1)
    def _():
        s, v = sel_chunk(qi - 1)
        _softmax_step(s + bias1, v, m_ref, acc_ref, scale)

    s, v = sel_chunk(qi)
    _softmax_step(jnp.where(row >= col, s + bias0, NEG), v, m_ref, acc_ref, scale)
    o_s = _softmax_finish(acc_ref)

    _softmax_init(m_ref, acc_ref)

    def win_chunk(j):
        sl = pl.ds(pl.multiple_of(j * tq, tq), tq)
        return _dot_nt(q, kw_ref[sl, :]), vw_ref[sl, :]

    @pl.when(qi >= nback)
    def _():
        s, v = win_chunk(qi - nback)
        if nback == 1:
            s = s + bias1
        _softmax_step(jnp.where(row < col, s, NEG), v, m_ref, acc_ref, scale)

    for back in range(nback - 1, 0, -1):
        @pl.when(qi >= back)
        def _(back=back):
            s, v = win_chunk(qi - back)
            if back == 1:
                s = s + bias1
            _softmax_step(s, v, m_ref, acc_ref, scale)

    s, v = win_chunk(qi)
    _softmax_step(jnp.where(row >= col, s + bias0, NEG), v, m_ref, acc_ref, scale)
    o_w = _softmax_finish(acc_ref)

    for h in range(HG):
        c = g * HG + h
        sl = slice(h * tq, (h + 1) * tq)
        lane = lax.broadcasted_iota(jnp.int32, gate.shape, 1)
        gc = [jnp.sum(jnp.where(lane == br * NSA_HEADS + c, gate, 0.0), axis=-1, keepdims=True)
              for br in range(3)]
        o = gc[0] * oc_ref[:, h * dk:(h + 1) * dk].astype(F32) + gc[1] * o_s[sl] + gc[2] * o_w[sl]
        o_ref[:, h * dk:(h + 1) * dk] = o.astype(o_ref.dtype)


def nsa_main(proj, selneg, ind, btiles, o_c, *, B, S, tq):
    M = B * S
    nq = S // tq
    G, dk, HG = NSA_GROUPS, NSA_HEAD_DIM, NSA_HG
    gw = HG * dk
    assert WINDOW % tq == 0
    nback = WINDOW // tq
    kv = lambda n: pl.BlockSpec((S, dk), lambda b, g, i, n=n: (b, (OFF_NSA_KV + n * G * dk) // dk + g))
    return pl.pallas_call(
        functools.partial(_nsa_main_kernel, tq=tq, nback=nback, scale=dk ** -0.5),
        grid=(B, G, nq),
        in_specs=[pl.BlockSpec((tq, gw), lambda b, g, i: (b * nq + i, OFF_NSA_Q // gw + g)),
                  pl.BlockSpec((None, None, tq, LANE), lambda b, g, i: (b, g, i, 0)),
                  kv(2), kv(3), kv(4), kv(5),
                  pl.BlockSpec((S, LANE), lambda b, g, i: (0, 0)),
                  pl.BlockSpec((HG, 2, tq, tq), lambda b, g, i: (DSA_HEADS // HG + g, 0, 0, 0)),
                  pl.BlockSpec((tq, LANE), lambda b, g, i: (b * nq + i, OFF_NSA_GATE // LANE)),
                  pl.BlockSpec((tq, gw), lambda b, g, i: (b * nq + i, g))],
        out_specs=pl.BlockSpec((tq, gw), lambda b, g, i: (b * nq + i, g)),
        out_shape=jax.ShapeDtypeStruct((M, G * gw), BF16),
        scratch_shapes=[pltpu.VMEM((HG * tq, LANE), F32), pltpu.VMEM((HG * tq, 2 * LANE), F32)],
        compiler_params=_cparams(("parallel", "parallel", "arbitrary")),
        name="nsa_main",
    )(proj, selneg, proj, proj, proj, proj, ind, btiles, proj, o_c)


def _merge_kernel(x_ref, oa_ref, ob_ref, oc_ref, g0_ref, g1_ref, g2_ref, wb_ref, wo_ref, gain_ref, o_ref):
    merged = None
    for o_r, g_r, br in ((oa_ref, g0_ref, 0), (ob_ref, g1_ref, 1), (oc_ref, g2_ref, 2)):
        t = jax.nn.sigmoid(g_r[...].astype(F32)) * _dot(o_r[...], wb_ref[br])
        merged = t if merged is None else merged + t
    y = _dot(merged.astype(BF16), wo_ref[...])
    o_ref[...] = x_ref[...] + _rms(y, gain_ref[...])


def merge_branches(x, o_a, o_b, o_c, proj, wb, wo, gain, *, tm):
    M, D = x.shape
    row = pl.BlockSpec((tm, D), lambda i: (i, 0))
    gate = lambda br: pl.BlockSpec((tm, D), lambda i, br=br: (i, OFF_BGATE // D + br))
    return pl.pallas_call(
        _merge_kernel,
        grid=(M // tm,),
        in_specs=[row, row, row, row, gate(0), gate(1), gate(2),
                  pl.BlockSpec(wb.shape, lambda i: (0, 0, 0)),
                  pl.BlockSpec(wo.shape, lambda i: (0, 0)),
                  pl.BlockSpec((1, D), lambda i: (0, 0))],
        out_specs=row,
        out_shape=jax.ShapeDtypeStruct((M, D), F32),
        compiler_params=_cparams(("parallel",)),
        name="merge_branches",
    )(x, o_a, o_b, o_c, proj, proj, proj, wb, wo, gain)


def _cross_kernel(x_ref, kv_ref, wq_ref, wo_ref, gpre_ref, gpost_ref, o_ref, *, scale):
    x = x_ref[...]
    h = _rms(x, gpre_ref[...]).astype(BF16)
    q = _dot(h, wq_ref[...]).astype(BF16)
    dh = CROSS_HEAD_DIM
    outs = []
    for hd in range(CROSS_HEADS):
        k = kv_ref[:, hd * dh:(hd + 1) * dh]
        v = kv_ref[:, D_MODEL + hd * dh:D_MODEL + (hd + 1) * dh]
        s = _dot_nt(q[:, hd * dh:(hd + 1) * dh], k) * scale
        p = jnp.exp(s - jnp.max(s, axis=-1, keepdims=True))
        p = p / jnp.sum(p, axis=-1, keepdims=True)
        outs.append(_dot(p.astype(BF16), v).astype(BF16))
    y = _dot(jnp.concatenate(outs, axis=1), wo_ref[...])
    o_ref[...] = x + _rms(y, gpost_ref[...])


def cross_attention(x, kv, wq, wo, gpre, gpost, *, S, tm):
    M, D = x.shape
    mlen = kv.shape[0] // (M // S)
    per_b = S // tm
    return pl.pallas_call(
        functools.partial(_cross_kernel, scale=CROSS_HEAD_DIM ** -0.5),
        grid=(M // tm,),
        in_specs=[pl.BlockSpec((tm, D), lambda i: (i, 0)),
                  pl.BlockSpec((mlen, 2 * D), lambda i: (i // per_b, 0)),
                  pl.BlockSpec(wq.shape, lambda i: (0, 0)),
                  pl.BlockSpec(wo.shape, lambda i: (0, 0)),
                  pl.BlockSpec((1, D), lambda i: (0, 0)),
                  pl.BlockSpec((1, D), lambda i: (0, 0))],
        out_specs=pl.BlockSpec((tm, D), lambda i: (i, 0)),
        out_shape=jax.ShapeDtypeStruct((M, D), F32),
        compiler_params=_cparams(("parallel",)),
        name="cross_attention",
    )(x, kv, wq, wo, gpre, gpost)


def _mlp_kernel(x_ref, w1_ref, w2_ref, gpre_ref, gpost_ref, o_ref, h_ref, acc_ref):
    j = pl.program_id(1)

    @pl.when(j == 0)
    def _():
        h_ref[...] = _rms(x_ref[...], gpre_ref[...]).astype(BF16)
        acc_ref[...] = jnp.zeros(acc_ref.shape, F32)

    a = jnp.maximum(_dot(h_ref[...], w1_ref[...]), 0.0)
    acc_ref[...] += _dot((a * a).astype(BF16), w2_ref[...])

    @pl.when(j == pl.num_programs(1) - 1)
    def _():
        o_ref[...] = x_ref[...] + _rms(acc_ref[...], gpost_ref[...])


def mlp(x, w1, w2, gpre, gpost, *, tm, tf):
    M, D = x.shape
    FF = w1.shape[1]
    return pl.pallas_call(
        _mlp_kernel,
        grid=(M // tm, FF // tf),
        in_specs=[pl.BlockSpec((tm, D), lambda i, j: (i, 0)),
                  pl.BlockSpec((D, tf), lambda i, j: (0, j)),
                  pl.BlockSpec((tf, D), lambda i, j: (j, 0)),
                  pl.BlockSpec((1, D), lambda i, j: (0, 0)),
                  pl.BlockSpec((1, D), lambda i, j: (0, 0))],
        out_specs=pl.BlockSpec((tm, D), lambda i, j: (i, 0)),
        out_shape=jax.ShapeDtypeStruct((M, D), F32),
        scratch_shapes=[pltpu.VMEM((tm, D), BF16), pltpu.VMEM((tm, D), F32)],
        compiler_params=_cparams(("parallel", "arbitrary")),
        name="mlp",
    )(x, w1, w2, gpre, gpost)


def _rot_half_cols(w):
    half = w.shape[-1] // 2
    return jnp.concatenate([-w[..., half:], w[..., :half]], axis=-1)


def _pack_w_in(w):
    widths = (MLA_Q_LORA, MLA_KV_LORA, MLA_ROPE, 1024, 1024, 1024, IDX_HEADS * IDX_DIM, IDX_DIM, IDX_HEADS,
              1024, 256, 256, 256, 256, 256, 256, 3 * NSA_HEADS, N_BRANCH * D_MODEL)
    offs = np.concatenate([[0], np.cumsum(widths)])
    seg = [w[:, offs[i]:offs[i + 1]] for i in range(len(widths))]
    (q_lat, kv_lat, k_rope, dsa_q, dsa_k, dsa_v, idx_q, idx_k, idx_w,
     nsa_q, kc, vc, ks, vs, kw, vw, nsa_gate, bgate) = seg
    z = lambda n: jnp.zeros((w.shape[0], n), w.dtype)
    cols = [dsa_q, dsa_k, dsa_v, nsa_q, bgate, idx_q, kc, vc, ks, vs, kw, vw,
            q_lat, kv_lat, k_rope, z(LANE - MLA_ROPE), _rot_half_cols(k_rope), z(LANE - MLA_ROPE), z(LANE),
            idx_k, idx_k, idx_w, z(LANE - IDX_HEADS), nsa_gate, z(LANE - 3 * NSA_HEADS)]
    out = jnp.concatenate(cols, axis=1)
    out = jnp.concatenate([out, z(N_PACK - out.shape[1])], axis=1)
    return out.astype(BF16)


def _pack_mla_weights(w_uq, w_ukv):
    H = MLA_HEADS
    uq = w_uq.reshape(MLA_Q_LORA, H, MLA_NOPE + MLA_ROPE)
    nope = uq[:, :, :MLA_NOPE].reshape(MLA_Q_LORA, H * MLA_NOPE)
    rope = uq[:, :, MLA_NOPE:]
    pad = jnp.zeros((MLA_Q_LORA, H, LANE - MLA_ROPE), w_uq.dtype)
    wqr = jnp.concatenate([rope, pad], axis=-1).reshape(MLA_Q_LORA, H * LANE)
    wqrr = jnp.concatenate([_rot_half_cols(rope), pad], axis=-1).reshape(MLA_Q_LORA, H * LANE)
    ukv = w_ukv.reshape(MLA_KV_LORA, H, MLA_NOPE + MLA_V)
    wkv = jnp.concatenate([ukv[:, :, :MLA_NOPE].reshape(MLA_KV_LORA, H * MLA_NOPE),
                           ukv[:, :, MLA_NOPE:].reshape(MLA_KV_LORA, H * MLA_V)], axis=1)
    return nope.astype(BF16), wqr.astype(BF16), wqrr.astype(BF16), wkv.astype(BF16)


def _constants(S):
    half = MLA_ROPE // 2
    inv_freq = ROPE_BASE ** (-np.arange(0, MLA_ROPE, 2, dtype=np.float32) / MLA_ROPE)
    freq = np.zeros((1, LANE), np.float32)
    freq[0, :half] = inv_freq
    freq[0, half:2 * half] = inv_freq
    ncp = S // CMP_STRIDE
    ns = S // SEL_BLOCK
    c_start = np.arange(ncp) * CMP_STRIDE
    j_start = np.arange(ns) * SEL_BLOCK
    overlap = ((c_start[None, :] < j_start[:, None] + SEL_BLOCK) &
               (c_start[None, :] + CMP_LEN > j_start[:, None])).astype(np.float32)
    overlap[:, ncp - 1] = 0.0
    ovt = np.zeros((LANE, ncp), np.float32)
    ovt[:ns] = overlap
    ind = np.zeros((S, LANE), np.float32)
    ind[np.arange(S), np.arange(S) // SEL_BLOCK] = 1.0
    return jnp.asarray(freq), jnp.asarray(ovt, BF16), jnp.asarray(ind, BF16)


def kernel(x, mem, positions, rel_bias, norm_gains, w_in, mla_q_norm, mla_kv_norm, mla_w_uq, mla_w_ukv,
           nsa_cmp_pos, nsa_cmp_w1, nsa_cmp_w2, w_branch, w_out, cross_wq, cross_wkv, cross_wo, mlp_w1, mlp_w2):
    B, S, D = x.shape
    M = B * S
    depth = w_in.shape[0]
    tq = TQ if S % TQ == 0 else TQ_NSA
    tqn = TQ_NSA
    assert S % tq == 0 and S % tqn == 0 and D == D_MODEL
    topk = min(DSA_TOPK_MAX, S // 4)
    freq, ovt, ind = _constants(S)
    btiles = bias_tiles(rel_bias, tq)
    btiles_n = btiles if tqn == tq else bias_tiles(rel_bias, tqn)
    pos = positions.reshape(M, 1)
    xf = x.reshape(M, D)
    memf = mem.reshape(B * mem.shape[1], D)
    tm_big = 1024 if M % 1024 == 0 else tq
    tm = 512 if M % 512 == 0 else tq
    ncp = S // CMP_STRIDE
    row = lambda v: v.reshape(1, -1)

    for l in range(depth):
        g = norm_gains[l]
        proj = norm_matmul(xf, row(g[0]), _pack_w_in(w_in[l]), tm=tm_big, tn=TN_IN)

        wqn, wqr, wqrr, wkv = _pack_mla_weights(mla_w_uq[l], mla_w_ukv[l])
        qn, qr, kn, vv, kr = mla_prep(proj, pos, freq, row(mla_q_norm[l]), row(mla_kv_norm[l]),
                                      wqn, wqr, wqrr, wkv, tm=tm)
        o_a = mla_attention(qn, qr, kn, kr, vv, B=B, S=S, tq=tq)

        maskb = dsa_index_mask(proj, B=B, S=S, tqi=128, tk=tq, topk=topk)
        o_b = dsa_attention(proj, maskb, btiles, B=B, S=S, tq=tq)

        kvc_in = jnp.stack([
            proj[:, OFF_NSA_KV + a * 256:OFF_NSA_KV + (a + 1) * 256].reshape(B * ncp, CMP_STRIDE * 256)
            for a in range(2)])
        posflat = jnp.broadcast_to(nsa_cmp_pos[l].reshape(2, 1, CMP_LEN * NSA_HEAD_DIM),
                                   (2, 8, CMP_LEN * NSA_HEAD_DIM)).astype(BF16)
        kvc = nsa_compress(kvc_in, posflat, nsa_cmp_w1[l].astype(BF16), nsa_cmp_w2[l].astype(BF16), B=B, S=S)
        o_cmp, selneg = nsa_cmp_attention(proj, kvc, ovt, B=B, S=S, tq=tqn)
        o_c = nsa_main(proj, selneg, ind, btiles_n, o_cmp, B=B, S=S, tq=tqn)

        xf = merge_branches(xf, o_a, o_b, o_c, proj, w_branch[l].astype(BF16), w_out[l].astype(BF16),
                            row(g[1]), tm=tm)

        mkv = norm_matmul(memf, row(g[3]), cross_wkv[l].astype(BF16), tm=memf.shape[0] // B, tn=1024)
        xf = cross_attention(xf, mkv, cross_wq[l].astype(BF16), cross_wo[l].astype(BF16),
                             row(g[2]), row(g[4]), S=S, tm=tm)

        xf = mlp(xf, mlp_w1[l].astype(BF16), mlp_w2[l].astype(BF16), row(g[5]), row(g[6]), tm=tm, tf=1024)

    return xf.reshape(B, S, D)
```

```python
import functools
import math

import numpy as np
import jax
import jax.numpy as jnp
from jax import lax
from jax.experimental import pallas as pl
from jax.experimental.pallas import tpu as pltpu

F32 = jnp.float32
BF16 = jnp.bfloat16

D_MODEL = 1024
EPS = 1e-6
MLA_HEADS = 8
MLA_Q_LORA = 384
MLA_KV_LORA = 256
MLA_NOPE = 128
MLA_ROPE = 64
MLA_V = 128
ROPE_BASE = 10000.0
DSA_HEADS = 8
DSA_HEAD_DIM = 128
IDX_HEADS = 8
IDX_DIM = 64
DSA_TOPK_MAX = 256
NSA_HEADS = 8
NSA_GROUPS = 2
NSA_HG = NSA_HEADS // NSA_GROUPS
NSA_HEAD_DIM = 128
CMP_STRIDE = 16
CMP_LEN = 2 * CMP_STRIDE
CMP_HIDDEN = 256
SEL_BLOCK = 64
SEL_COUNT = 16
WINDOW = 512
REL_BUCKETS = 32
REL_MAX_DIST = 128
REL_HEADS = DSA_HEADS + NSA_HEADS
CROSS_HEADS = 4
CROSS_HEAD_DIM = D_MODEL // CROSS_HEADS
D_FF = 4 * D_MODEL
N_BRANCH = 3

LANE = 128
NEG = -1e30
VMEM_LIMIT = 48 * 1024 * 1024

TQ = 512
TQ_NSA = 256
HPS = 2
NSA_STREAMS = 2
BISECT_ITERS = 40
IDX_WSTEP = 512

OFF_DSA_Q = 0
OFF_DSA_K = 1024
OFF_DSA_V = 2048
OFF_NSA_Q = 3072
OFF_BGATE = 4096
OFF_IDX_Q = 7168
OFF_NSA_KV = 7680
OFF_MLA = 9216
OFF_IDX_K = 10240
OFF_IDX_W = 10368
OFF_NSA_GATE = 10496
N_PACK = 10752
TN_IN = 1536


def _cparams(sem):
    return pltpu.CompilerParams(dimension_semantics=sem, vmem_limit_bytes=VMEM_LIMIT)


def _rms(x, g):
    return x * lax.rsqrt(jnp.mean(x * x, axis=-1, keepdims=True) + EPS) * g


def _dot(a, b):
    return jnp.dot(a, b, preferred_element_type=F32)


def _dot_nt(a, b):
    return lax.dot_general(a, b, (((1,), (1,)), ((), ())), preferred_element_type=F32)


def _norm_mm_kernel(x_ref, g_ref, w_ref, o_ref, xn_ref):
    @pl.when(pl.program_id(1) == 0)
    def _():
        xn_ref[...] = _rms(x_ref[...].astype(F32), g_ref[...]).astype(BF16)

    o_ref[...] = _dot(xn_ref[...], w_ref[...]).astype(o_ref.dtype)


def norm_matmul(x, g, w, *, tm, tn):
    M, K = x.shape
    N = w.shape[1]
    return pl.pallas_call(
        _norm_mm_kernel,
        grid=(M // tm, N // tn),
        in_specs=[pl.BlockSpec((tm, K), lambda i, j: (i, 0)),
                  pl.BlockSpec((1, K), lambda i, j: (0, 0)),
                  pl.BlockSpec((K, tn), lambda i, j: (0, j))],
        out_specs=pl.BlockSpec((tm, tn), lambda i, j: (i, j)),
        out_shape=jax.ShapeDtypeStruct((M, N), BF16),
        scratch_shapes=[pltpu.VMEM((tm, K), BF16)],
        compiler_params=_cparams(("parallel", "arbitrary")),
        name="norm_matmul",
    )(x, g, w)


def _bias_tiles_kernel(tab_ref, bkt_ref, o_ref, *, inv_scale):
    h = pl.program_id(0)
    bkt = bkt_ref[...]
    far = tab_ref[REL_BUCKETS - 1, h]
    out = jnp.zeros(bkt.shape, F32)
    for b in range(REL_BUCKETS - 1):
        out = jnp.where(bkt == b, (tab_ref[b, h] - far) * inv_scale, out)
    o_ref[...] = out


def _bucket_tiles(tq):
    i = np.arange(tq)[:, None]
    j = np.arange(tq)[None, :]
    d = np.stack([i - j, tq + i - j]).astype(np.int32)
    n = jnp.maximum(jnp.asarray(d), 0)
    exact = REL_BUCKETS // 2
    nf = jnp.maximum(n, 1).astype(F32)
    log_b = exact + (jnp.log(nf / exact) / math.log(REL_MAX_DIST / exact) * (REL_BUCKETS - exact)).astype(jnp.int32)
    return jnp.where(n < exact, n, jnp.minimum(log_b, REL_BUCKETS - 1)).astype(jnp.int32)


def bias_tiles(rel_bias, tq):
    assert tq >= REL_MAX_DIST
    bkt = _bucket_tiles(tq)
    assert DSA_HEAD_DIM == NSA_HEAD_DIM
    return pl.pallas_call(
        functools.partial(_bias_tiles_kernel, inv_scale=DSA_HEAD_DIM ** 0.5),
        grid=(REL_HEADS,),
        in_specs=[pl.BlockSpec(memory_space=pltpu.SMEM),
                  pl.BlockSpec((2, tq, tq), lambda h: (0, 0, 0))],
        out_specs=pl.BlockSpec((None, 2, tq, tq), lambda h: (h, 0, 0, 0)),
        out_shape=jax.ShapeDtypeStruct((REL_HEADS, 2, tq, tq), F32),
        compiler_params=_cparams(("arbitrary",)),
        name="bias_tiles",
    )(rel_bias, bkt)


def _mla_prep_kernel(lat_ref, pos_ref, freq_ref, qg_ref, kvg_ref, wqn_ref, wqr_ref, wqrr_ref, wkv_ref,
                     qn_ref, qr_ref, kn_ref, v_ref, kr_ref):
    lat = lat_ref[...].astype(F32)
    qlat = _rms(lat[:, :MLA_Q_LORA], qg_ref[...]).astype(BF16)
    kvlat = _rms(lat[:, MLA_Q_LORA:MLA_Q_LORA + MLA_KV_LORA], kvg_ref[...]).astype(BF16)
    o = MLA_Q_LORA + MLA_KV_LORA
    kr = lat[:, o:o + LANE]
    krr = lat[:, o + LANE:o + 2 * LANE]
    ang = pos_ref[...].astype(F32) * freq_ref[...]
    cos, sin = jnp.cos(ang), jnp.sin(ang)
    cos8 = jnp.concatenate([cos] * MLA_HEADS, axis=1)
    sin8 = jnp.concatenate([sin] * MLA_HEADS, axis=1)
    qn_ref[...] = _dot(qlat, wqn_ref[...]).astype(BF16)
    qr_ref[...] = (_dot(qlat, wqr_ref[...]) * cos8 + _dot(qlat, wqrr_ref[...]) * sin8).astype(BF16)
    kv = _dot(kvlat, wkv_ref[...])
    nk = MLA_HEADS * MLA_NOPE
    kn_ref[...] = kv[:, :nk].astype(BF16)
    v_ref[...] = kv[:, nk:].astype(BF16)
    kr_ref[...] = (kr * cos + krr * sin).astype(BF16)


def mla_prep(proj, pos, freq, qg, kvg, wqn, wqr, wqrr, wkv, *, tm):
    M = proj.shape[0]
    HD = MLA_HEADS * LANE
    full = lambda a: pl.BlockSpec(a.shape, lambda i: (0, 0))
    outs = [jax.ShapeDtypeStruct((M, HD), BF16)] * 4 + [jax.ShapeDtypeStruct((M, LANE), BF16)]
    return pl.pallas_call(
        _mla_prep_kernel,
        grid=(M // tm,),
        in_specs=[pl.BlockSpec((tm, 1024), lambda i: (i, OFF_MLA // 1024)),
                  pl.BlockSpec((tm, 1), lambda i: (i, 0)),
                  full(freq), full(qg), full(kvg), full(wqn), full(wqr), full(wqrr), full(wkv)],
        out_specs=[pl.BlockSpec((tm, HD), lambda i: (i, 0))] * 4 + [pl.BlockSpec((tm, LANE), lambda i: (i, 0))],
        out_shape=outs,
        compiler_params=_cparams(("parallel",)),
        name="mla_prep",
    )(proj, pos, freq, qg, kvg, wqn, wqr, wqrr, wkv)


def _softmax_init(m_ref, acc_ref):
    m_ref[...] = jnp.full(m_ref.shape, NEG, F32)
    acc_ref[...] = jnp.zeros(acc_ref.shape, F32)


def _with_ones(v):
    return jnp.concatenate([v, jnp.ones(v.shape, v.dtype)], axis=1)


def _softmax_step(s, v, m_ref, acc_ref, scale):
    c = scale * math.log2(math.e)
    m_prev = m_ref[...]
    m_new = jnp.maximum(m_prev, jnp.max(s, axis=-1, keepdims=True))
    alpha = jnp.exp2((m_prev - m_new) * c)
    p = jnp.exp2((s - jnp.tile(m_new, (1, s.shape[1] // LANE))) * c)
    acc_ref[...] = jnp.tile(alpha, (1, 2)) * acc_ref[...] + _dot(p.astype(BF16), _with_ones(v))
    m_ref[...] = m_new


def _softmax_finish(acc_ref):
    acc = acc_ref[...]
    return acc[:, :LANE] / acc[:, LANE:]


def _tri(tq, tk):
    row = lax.broadcasted_iota(jnp.int32, (tq, tk), 0)
    col = lax.broadcasted_iota(jnp.int32, (tq, tk), 1)
    return row, col


def _mla_attn_kernel(qn_ref, qr_ref, kn_ref, kr_ref, v_ref, o_ref, m_ref, acc_ref, *, tq, scale):
    qi = pl.program_id(2)
    hs = range(HPS)
    q = [jnp.concatenate([qn_ref[:, h * LANE:(h + 1) * LANE], qr_ref[:, h * LANE:(h + 1) * LANE]], axis=-1)
         for h in hs]
    for h in hs:
        _softmax_init(m_ref.at[h], acc_ref.at[h])

    def scores(j):
        sl = pl.ds(pl.multiple_of(j * tq, tq), tq)
        kr = kr_ref[sl, :]
        return [_dot_nt(q[h], jnp.concatenate([kn_ref[sl, h * LANE:(h + 1) * LANE], kr], axis=-1)) for h in hs], sl

    def body(j, c):
        s, sl = scores(j)
        for h in hs:
            _softmax_step(s[h], v_ref[sl, h * LANE:(h + 1) * LANE], m_ref.at[h], acc_ref.at[h], scale)
        return c

    lax.fori_loop(0, qi, body, 0)
    s, sl = scores(qi)
    row, col = _tri(tq, tq)
    for h in hs:
        _softmax_step(jnp.where(row >= col, s[h], NEG), v_ref[sl, h * LANE:(h + 1) * LANE],
                      m_ref.at[h], acc_ref.at[h], scale)
    for h in hs:
        o_ref[:, h * LANE:(h + 1) * LANE] = _softmax_finish(acc_ref.at[h]).astype(o_ref.dtype)


def mla_attention(qn, qr, kn, kr, v, *, B, S, tq):
    M = B * S
    nq = S // tq
    H = MLA_HEADS
    scale = (MLA_NOPE + MLA_ROPE) ** -0.5
    w = HPS * LANE
    qspec = pl.BlockSpec((tq, w), lambda b, h, i: (b * nq + i, h))
    kspec = pl.BlockSpec((S, w), lambda b, h, i: (b, h))
    return pl.pallas_call(
        functools.partial(_mla_attn_kernel, tq=tq, scale=scale),
        grid=(B, H // HPS, nq),
        in_specs=[qspec, qspec, kspec, pl.BlockSpec((S, LANE), lambda b, h, i: (b, 0)), kspec],
        out_specs=qspec,
        out_shape=jax.ShapeDtypeStruct((M, H * MLA_V), BF16),
        scratch_shapes=[pltpu.VMEM((HPS, tq, LANE), F32), pltpu.VMEM((HPS, tq, 2 * LANE), F32)],
        compiler_params=_cparams(("parallel", "parallel", "arbitrary")),
        name="mla_attention",
    )(qn, qr, kn, kr, v)


def _dsa_index_body(iq_ref, ik_ref, iw_ref, mb_ref, sc_ref, *, t0, W, tq, tk, S, topk, iters):
    iq = iq_ref[...]
    ik = ik_ref[0:W, :]
    iw = iw_ref[...].astype(F32)
    lane = lax.broadcasted_iota(jnp.int32, (tq, LANE), 1)
    score = jnp.zeros((tq, W), F32)
    for h in range(IDX_HEADS):
        blk = iq[:, (h // 2) * LANE:(h // 2 + 1) * LANE]
        keep = (lane >= IDX_DIM) if h % 2 else (lane < IDX_DIM)
        qh = jnp.where(keep, blk, jnp.zeros_like(blk))
        score = score + iw[:, h:h + 1] * jnp.maximum(_dot_nt(qh, ik), 0.0)
    row = lax.broadcasted_iota(jnp.int32, (tq, W), 0) + t0
    col = lax.broadcasted_iota(jnp.int32, (tq, W), 1)
    causal = col <= row
    sc_ref[:, 0:W] = jnp.where(causal, score, NEG)
    n_causal = (lax.broadcasted_iota(jnp.int32, (tq, 1), 0) + (t0 + 1)).astype(F32)
    kf = jnp.minimum(float(topk), n_causal)

    def count(pred):
        return jnp.sum(jnp.where(pred, 1.0, 0.0), axis=-1, keepdims=True)

    lo0 = jnp.min(jnp.where(causal, score, -NEG), axis=-1, keepdims=True)
    mx = jnp.max(sc_ref[:, 0:W], axis=-1, keepdims=True)
    hi0 = mx + (jnp.abs(mx) * 1e-6 + 1.0)

    def bisect(c):
        lo, hi, n_lo = c
        mid = 0.5 * (lo + hi)
        n_mid = count(sc_ref[:, 0:W] >= mid)
        ge = n_mid >= kf
        return jnp.where(ge, mid, lo), jnp.where(ge, hi, mid), jnp.where(ge, n_mid, n_lo)

    def cond(c):
        it, _, _, n_lo = c
        return jnp.logical_and(it < iters, jnp.max(n_lo - kf) > 0.0)

    def body(c):
        it, lo, hi, n_lo = c
        return (it + 2,) + bisect(bisect((lo, hi, n_lo)))

    _, lo, hi, n_lo = lax.while_loop(cond, body, (jnp.int32(0), lo0, hi0, n_causal))
    tied = jnp.max(n_lo - kf) > 0.0

    def write(sel):
        mb = jnp.where(sel, 0.0, NEG).astype(mb_ref.dtype)
        for j in range(S // tk):
            if (j + 1) * tk <= W:
                mb_ref[j] = mb[:, j * tk:(j + 1) * tk]
            elif j * tk >= W:
                mb_ref[j] = jnp.full((tq, tk), NEG, mb_ref.dtype)
            else:
                mb_ref[j, :, 0:W - j * tk] = mb[:, j * tk:W]
                mb_ref[j, :, W - j * tk:tk] = jnp.full((tq, (j + 1) * tk - W), NEG, mb_ref.dtype)

    @pl.when(jnp.logical_not(tied))
    def _():
        write(sc_ref[:, 0:W] >= lo)

    @pl.when(tied)
    def _():
        sc = sc_ref[:, 0:W]
        above = sc >= hi
        band = (sc >= lo) & jnp.logical_not(above)
        need = kf - count(above)
        colf = col.astype(F32)

        def ibisect(_, c):
            jlo, jhi = c
            jm = jnp.floor(0.5 * (jlo + jhi))
            ok = count(band & (colf <= jm)) >= need
            return jnp.where(ok, jlo, jm), jnp.where(ok, jm, jhi)

        steps = max(1, int(math.ceil(math.log2(W))) + 1)
        _, jhi = lax.fori_loop(0, steps, ibisect,
                               (jnp.full((tq, 1), -1.0, F32), jnp.full((tq, 1), float(W - 1), F32)))
        write(above | (band & (colf <= jhi)))


def _dsa_index_kernel(iq_ref, ik_ref, iw_ref, mb_ref, sc_ref, *, tq, wstep, **kw):
    t0 = pl.program_id(1) * tq
    S = kw["S"]
    for v in range(S // wstep):
        @pl.when((t0 + tq - 1) // wstep == v)
        def _(v=v):
            _dsa_index_body(iq_ref, ik_ref, iw_ref, mb_ref, sc_ref, t0=t0, W=(v + 1) * wstep, tq=tq, **kw)


def dsa_index_mask(proj, *, B, S, tqi, tk, topk):
    nq = S // tqi
    nk = S // tk
    return pl.pallas_call(
        functools.partial(_dsa_index_kernel, tq=tqi, wstep=math.gcd(S, IDX_WSTEP), tk=tk, S=S, topk=topk,
                          iters=BISECT_ITERS),
        grid=(B, nq),
        in_specs=[pl.BlockSpec((tqi, 512), lambda b, i: (b * nq + i, OFF_IDX_Q // 512)),
                  pl.BlockSpec((S, LANE), lambda b, i: (b, OFF_IDX_K // LANE)),
                  pl.BlockSpec((tqi, LANE), lambda b, i: (b * nq + i, OFF_IDX_W // LANE))],
        out_specs=pl.BlockSpec((None, nk, tqi, tk), lambda b, i: (b, 0, i, 0)),
        out_shape=jax.ShapeDtypeStruct((B, nk, S, tk), BF16),
        scratch_shapes=[pltpu.VMEM((tqi, S), F32)],
        compiler_params=_cparams(("parallel", "parallel")),
        name="dsa_index_mask",
    )(proj, proj, proj)


def _dsa_attn_kernel(q_ref, k_ref, v_ref, mb_ref, bias_ref, o_ref, m_ref, acc_ref, *, tq, scale):
    qi = pl.program_id(2)
    hs = range(HPS)
    cols = lambda h: slice(h * LANE, (h + 1) * LANE)
    q = [q_ref[:, cols(h)] for h in hs]
    for h in hs:
        _softmax_init(m_ref.at[h], acc_ref.at[h])

    def step(j, bias_idx):
        sl = pl.ds(pl.multiple_of(j * tq, tq), tq)
        mb = mb_ref[j].astype(F32)
        s = [_dot_nt(q[h], k_ref[sl, cols(h)]) + mb for h in hs]
        for h in hs:
            sh = s[h] if bias_idx is None else s[h] + bias_ref[h, bias_idx]
            _softmax_step(sh, v_ref[sl, cols(h)], m_ref.at[h], acc_ref.at[h], scale)

    def body(j, c):
        step(j, None)
        return c

    lax.fori_loop(0, jnp.maximum(qi - 1, 0), body, 0)

    @pl.when(qi >= 1)
    def _():
        step(qi - 1, 1)

    step(qi, 0)
    for h in hs:
        o_ref[:, cols(h)] = _softmax_finish(acc_ref.at[h]).astype(o_ref.dtype)


def dsa_attention(proj, maskb, btiles, *, B, S, tq):
    M = B * S
    nq = S // tq
    H = DSA_HEADS
    scale = DSA_HEAD_DIM ** -0.5
    w = HPS * LANE
    return pl.pallas_call(
        functools.partial(_dsa_attn_kernel, tq=tq, scale=scale),
        grid=(B, H // HPS, nq),
        in_specs=[pl.BlockSpec((tq, w), lambda b, h, i: (b * nq + i, OFF_DSA_Q // w + h)),
                  pl.BlockSpec((S, w), lambda b, h, i: (b, OFF_DSA_K // w + h)),
                  pl.BlockSpec((S, w), lambda b, h, i: (b, OFF_DSA_V // w + h)),
                  pl.BlockSpec((None, nq, tq, tq), lambda b, h, i: (b, 0, i, 0)),
                  pl.BlockSpec((HPS, 2, tq, tq), lambda b, h, i: (h, 0, 0, 0))],
        out_specs=pl.BlockSpec((tq, w), lambda b, h, i: (b * nq + i, h)),
        out_shape=jax.ShapeDtypeStruct((M, H * DSA_HEAD_DIM), BF16),
        scratch_shapes=[pltpu.VMEM((HPS, tq, LANE), F32), pltpu.VMEM((HPS, tq, 2 * LANE), F32)],
        compiler_params=_cparams(("parallel", "parallel", "arbitrary")),
        name="dsa_attention",
    )(proj, proj, proj, maskb, btiles)


def _nsa_compress_kernel(x_ref, pos_ref, w1_ref, w2_ref, o_ref, *, ncp):
    dk = NSA_HEAD_DIM
    posw = _dot(pos_ref[...], w1_ref[...])[0:1, :]
    for g in range(NSA_GROUPS):
        lo = jnp.zeros((ncp, CMP_HIDDEN), F32)
        hi = jnp.zeros((ncp, CMP_HIDDEN), F32)
        for l in range(CMP_STRIDE):
            xs = x_ref[:, (l * NSA_GROUPS + g) * dk:(l * NSA_GROUPS + g + 1) * dk]
            lo = lo + _dot(xs, w1_ref[l * dk:(l + 1) * dk, :])
            hi = hi + _dot(xs, w1_ref[(CMP_STRIDE + l) * dk:(CMP_STRIDE + l + 1) * dk, :])
        hid = lo + pltpu.roll(hi, ncp - 1, 0) + posw
        o_ref[g] = _dot(jax.nn.gelu(hid).astype(BF16), w2_ref[...]).astype(o_ref.dtype)


def nsa_compress(xkv, posflat, w1, w2, *, B, S):
    ncp = S // CMP_STRIDE
    G, dk = NSA_GROUPS, NSA_HEAD_DIM
    width = CMP_STRIDE * G * dk
    return pl.pallas_call(
        functools.partial(_nsa_compress_kernel, ncp=ncp),
        grid=(2, B),
        in_specs=[pl.BlockSpec((None, ncp, width), lambda a, b: (a, b, 0)),
                  pl.BlockSpec((None, 8, CMP_LEN * dk), lambda a, b: (a, 0, 0)),
                  pl.BlockSpec((None, CMP_LEN * dk, CMP_HIDDEN), lambda a, b: (a, 0, 0)),
                  pl.BlockSpec((None, CMP_HIDDEN, dk), lambda a, b: (a, 0, 0))],
        out_specs=pl.BlockSpec((None, None, G, ncp, dk), lambda a, b: (a, b, 0, 0, 0)),
        out_shape=jax.ShapeDtypeStruct((2, B, G, ncp, dk), BF16),
        compiler_params=_cparams(("parallel", "parallel")),
        name="nsa_compress",
    )(xkv, posflat, w1, w2)


def _nsa_cmp_kernel(q_ref, kc_ref, vc_ref, ov_ref, oc_ref, sel_ref, *, tq, ncp, ns, n_sel, scale):
    t0 = pl.program_id(2) * tq
    dk = NSA_HEAD_DIM
    kc = kc_ref[...]
    vc = vc_ref[...]
    trow = lax.broadcasted_iota(jnp.int32, (tq, ncp), 0) + t0
    ccol = lax.broadcasted_iota(jnp.int32, (tq, ncp), 1)
    vis = (ccol * CMP_STRIDE + (CMP_LEN - 1)) <= trow
    psum = jnp.zeros((tq, ncp), F32)
    for h in range(NSA_HG):
        s = jnp.where(vis, _dot_nt(q_ref[:, h * dk:(h + 1) * dk], kc) * scale, NEG)
        m = jnp.max(s, axis=-1, keepdims=True)
        p = jnp.where(vis, jnp.exp(s - m), 0.0)
        d = jnp.sum(p, axis=-1, keepdims=True)
        p = p / jnp.where(d > 0, d, 1.0)
        oc_ref[:, h * dk:(h + 1) * dk] = _dot(p.astype(BF16), vc).astype(oc_ref.dtype)
        psum = psum + p
    p_hi = psum.astype(BF16)
    p_lo = (psum - p_hi.astype(F32)).astype(BF16)
    ov = ov_ref[...]
    imp = _dot_nt(ov, p_hi) + _dot_nt(ov, p_lo)
    imp = imp[:ns]
    jrow = lax.broadcasted_iota(jnp.int32, (ns, tq), 0)
    tcol = lax.broadcasted_iota(jnp.int32, (ns, tq), 1) + t0
    blk_t = tcol // SEL_BLOCK
    forced = (jrow == 0) | (jrow == blk_t) | (jrow == blk_t - 1)
    val = jnp.where(forced, -NEG, imp)
    val = jnp.where(jrow * SEL_BLOCK > tcol, NEG, val)
    rank = jnp.zeros((ns, tq), F32)
    for j in range(ns):
        other = val[j:j + 1, :]
        ahead = (other > val) | ((other == val) & (jrow > j))
        rank = rank + jnp.where(ahead, 1.0, 0.0)
    selneg = jnp.where(rank < float(n_sel), 0.0, NEG)
    if ns < LANE:
        selneg = jnp.concatenate([selneg, jnp.zeros((LANE - ns, tq), F32)], axis=0)
    sel_ref[...] = selneg.T.astype(sel_ref.dtype)


def nsa_cmp_attention(proj, kvc, ovt, *, B, S, tq):
    M = B * S
    nq = S // tq
    G, dk = NSA_GROUPS, NSA_HEAD_DIM
    ncp = S // CMP_STRIDE
    ns = S // SEL_BLOCK
    assert ns <= LANE
    n_sel = min(SEL_COUNT, ns)
    gw = NSA_HG * dk
    return pl.pallas_call(
        functools.partial(_nsa_cmp_kernel, tq=tq, ncp=ncp, ns=ns, n_sel=n_sel, scale=dk ** -0.5),
        grid=(B, G, nq),
        in_specs=[pl.BlockSpec((tq, gw), lambda b, g, i: (b * nq + i, OFF_NSA_Q // gw + g)),
                  pl.BlockSpec((None, None, None, ncp, dk), lambda b, g, i: (0, b, g, 0, 0)),
                  pl.BlockSpec((None, None, None, ncp, dk), lambda b, g, i: (1, b, g, 0, 0)),
                  pl.BlockSpec((LANE, ncp), lambda b, g, i: (0, 0))],
        out_specs=[pl.BlockSpec((tq, gw), lambda b, g, i: (b * nq + i, g)),
                   pl.BlockSpec((None, None, tq, LANE), lambda b, g, i: (b, g, i, 0))],
        out_shape=[jax.ShapeDtypeStruct((M, G * gw), BF16),
                   jax.ShapeDtypeStruct((B, G, S, LANE), BF16)],
        compiler_params=_cparams(("parallel", "parallel", "parallel")),
        name="nsa_cmp_attention",
    )(proj, kvc, kvc, ovt)


def _nsa_main_kernel(q_ref, sel_ref, ks_ref, vs_ref, kw_ref, vw_ref, ind_ref, bias_ref, gate_ref, oc_ref,
                     o_ref, m_ref, acc_ref, *, tq, nback, scale):
    g = pl.program_id(1)
    qi = pl.program_id(2)
    dk = NSA_HEAD_DIM
    HG = NSA_HG
    ns = NSA_STREAMS
    hp = HG // ns
    st = range(ns)
    stack = lambda f: [jnp.concatenate([f(a * hp + i) for i in range(hp)], axis=0) for a in st]
    q = stack(lambda h: q_ref[:, h * dk:(h + 1) * dk])
    sel = jnp.concatenate([sel_ref[...]] * hp, axis=0)
    qa = [jnp.concatenate([q[a], sel], axis=1) for a in st]
    row, col = _tri(tq, tq)
    row = jnp.concatenate([row] * hp, axis=0)
    col = jnp.concatenate([col] * hp, axis=0)
    bias0 = stack(lambda h: bias_ref[h, 0])
    bias1 = stack(lambda h: bias_ref[h, 1])
    gate = jax.nn.sigmoid(gate_ref[...].astype(F32))

    def init():
        for a in st:
            _softmax_init(m_ref.at[a], acc_ref.at[a])

    def steps(s, v, fix):
        for a in st:
            _softmax_step(fix(a, s[a]), v, m_ref.at[a], acc_ref.at[a], scale)

    init()

    def sel_scores(j):
        sl = pl.ds(pl.multiple_of(j * tq, tq), tq)
        ka = jnp.concatenate([ks_ref[sl, :], ind_ref[sl, :]], axis=1)
        return [_dot_nt(qa[a], ka) for a in st], vs_ref[sl, :]

    def sel_body(j, c):
        s, v = sel_scores(j)
        steps(s, v, lambda a, x: x)
        return c

    lax.fori_loop(0, jnp.maximum(qi - 1, 0), sel_body, 0)

    @pl.when(qi >= 1)
    def _():
        s, v = sel_scores(qi - 1)
        steps(s, v, lambda a, x: x + bias1[a])

    s, v = sel_scores(qi)
    steps(s, v, lambda a, x: jnp.where(row >= col, x + bias0[a], NEG))
    o_s = [_softmax_finish(acc_ref.at[a]) for a in st]

    init()

    def win_scores(j):
        sl = pl.ds(pl.multiple_of(j * tq, tq), tq)
        kw = kw_ref[sl, :]
        return [_dot_nt(q[a], kw) for a in st], vw_ref[sl, :]

    @pl.when(qi >= nback)
    def _():
        s, v = win_scores(qi - nback)
        steps(s, v, lambda a, x: jnp.where(row < col, x + bias1[a] if nback == 1 else x, NEG))

    for back in range(nback - 1, 0, -1):
        @pl.when(qi >= back)
        def _(back=back):
            s, v = win_scores(qi - back)
            steps(s, v, lambda a, x: x + bias1[a] if back == 1 else x)

    s, v = win_scores(qi)
    steps(s, v, lambda a, x: jnp.where(row >= col, x + bias0[a], NEG))
    o_w = [_softmax_finish(acc_ref.at[a]) for a in st]

    lane = lax.broadcasted_iota(jnp.int32, gate.shape, 1)
    for h in range(HG):
        c = g * HG + h
        a, sl = h // hp, slice((h % hp) * tq, (h % hp + 1) * tq)
        gc = [jnp.sum(jnp.where(lane == br * NSA_HEADS + c, gate, 0.0), axis=-1, keepdims=True)
              for br in range(3)]
        o = gc[0] * oc_ref[:, h * dk:(h + 1) * dk].astype(F32) + gc[1] * o_s[a][sl] + gc[2] * o_w[a][sl]
        o_ref[:, h * dk:(h + 1) * dk] = o.astype(o_ref.dtype)


def nsa_main(proj, selneg, ind, btiles, o_c, *, B, S, tq):
    M = B * S
    nq = S // tq
    G, dk, HG = NSA_GROUPS, NSA_HEAD_DIM, NSA_HG
    gw = HG * dk
    assert WINDOW % tq == 0
    nback = WINDOW // tq
    kv = lambda n: pl.BlockSpec((S, dk), lambda b, g, i, n=n: (b, (OFF_NSA_KV + n * G * dk) // dk + g))
    return pl.pallas_call(
        functools.partial(_nsa_main_kernel, tq=tq, nback=nback, scale=dk ** -0.5),
        grid=(B, G, nq),
        in_specs=[pl.BlockSpec((tq, gw), lambda b, g, i: (b * nq + i, OFF_NSA_Q // gw + g)),
                  pl.BlockSpec((None, None, tq, LANE), lambda b, g, i: (b, g, i, 0)),
                  kv(2), kv(3), kv(4), kv(5),
                  pl.BlockSpec((S, LANE), lambda b, g, i: (0, 0)),
                  pl.BlockSpec((HG, 2, tq, tq), lambda b, g, i: (DSA_HEADS // HG + g, 0, 0, 0)),
                  pl.BlockSpec((tq, LANE), lambda b, g, i: (b * nq + i, OFF_NSA_GATE // LANE)),
                  pl.BlockSpec((tq, gw), lambda b, g, i: (b * nq + i, g))],
        out_specs=pl.BlockSpec((tq, gw), lambda b, g, i: (b * nq + i, g)),
        out_shape=jax.ShapeDtypeStruct((M, G * gw), BF16),
        scratch_shapes=[pltpu.VMEM((NSA_STREAMS, HG // NSA_STREAMS * tq, LANE), F32),
                        pltpu.VMEM((NSA_STREAMS, HG // NSA_STREAMS * tq, 2 * LANE), F32)],
        compiler_params=_cparams(("parallel", "parallel", "arbitrary")),
        name="nsa_main",
    )(proj, selneg, proj, proj, proj, proj, ind, btiles, proj, o_c)


def _merge_kernel(x_ref, oa_ref, ob_ref, oc_ref, g0_ref, g1_ref, g2_ref, wb_ref, wo_ref, gain_ref, o_ref):
    merged = None
    for o_r, g_r, br in ((oa_ref, g0_ref, 0), (ob_ref, g1_ref, 1), (oc_ref, g2_ref, 2)):
        t = jax.nn.sigmoid(g_r[...].astype(F32)) * _dot(o_r[...], wb_ref[br])
        merged = t if merged is None else merged + t
    y = _dot(merged.astype(BF16), wo_ref[...])
    o_ref[...] = x_ref[...] + _rms(y, gain_ref[...])


def merge_branches(x, o_a, o_b, o_c, proj, wb, wo, gain, *, tm):
    M, D = x.shape
    row = pl.BlockSpec((tm, D), lambda i: (i, 0))
    gate = lambda br: pl.BlockSpec((tm, D), lambda i, br=br: (i, OFF_BGATE // D + br))
    return pl.pallas_call(
        _merge_kernel,
        grid=(M // tm,),
        in_specs=[row, row, row, row, gate(0), gate(1), gate(2),
                  pl.BlockSpec(wb.shape, lambda i: (0, 0, 0)),
                  pl.BlockSpec(wo.shape, lambda i: (0, 0)),
                  pl.BlockSpec((1, D), lambda i: (0, 0))],
        out_specs=row,
        out_shape=jax.ShapeDtypeStruct((M, D), F32),
        compiler_params=_cparams(("parallel",)),
        name="merge_branches",
    )(x, o_a, o_b, o_c, proj, proj, proj, wb, wo, gain)


def _cross_kernel(x_ref, kv_ref, wq_ref, wo_ref, gpre_ref, gpost_ref, o_ref, *, scale):
    x = x_ref[...]
    h = _rms(x, gpre_ref[...]).astype(BF16)
    q = _dot(h, wq_ref[...]).astype(BF16)
    dh = CROSS_HEAD_DIM
    outs = []
    for hd in range(CROSS_HEADS):
        k = kv_ref[:, hd * dh:(hd + 1) * dh]
        v = kv_ref[:, D_MODEL + hd * dh:D_MODEL + (hd + 1) * dh]
        s = _dot_nt(q[:, hd * dh:(hd + 1) * dh], k) * scale
        p = jnp.exp(s - jnp.max(s, axis=-1, keepdims=True))
        p = p / jnp.sum(p, axis=-1, keepdims=True)
        outs.append(_dot(p.astype(BF16), v).astype(BF16))
    y = _dot(jnp.concatenate(outs, axis=1), wo_ref[...])
    o_ref[...] = x + _rms(y, gpost_ref[...])


def cross_attention(x, kv, wq, wo, gpre, gpost, *, S, tm):
    M, D = x.shape
    mlen = kv.shape[0] // (M // S)
    per_b = S // tm
    return pl.pallas_call(
        functools.partial(_cross_kernel, scale=CROSS_HEAD_DIM ** -0.5),
        grid=(M // tm,),
        in_specs=[pl.BlockSpec((tm, D), lambda i: (i, 0)),
                  pl.BlockSpec((mlen, 2 * D), lambda i: (i // per_b, 0)),
                  pl.BlockSpec(wq.shape, lambda i: (0, 0)),
                  pl.BlockSpec(wo.shape, lambda i: (0, 0)),
                  pl.BlockSpec((1, D), lambda i: (0, 0)),
                  pl.BlockSpec((1, D), lambda i: (0, 0))],
        out_specs=pl.BlockSpec((tm, D), lambda i: (i, 0)),
        out_shape=jax.ShapeDtypeStruct((M, D), F32),
        compiler_params=_cparams(("parallel",)),
        name="cross_attention",
    )(x, kv, wq, wo, gpre, gpost)


def _mlp_kernel(x_ref, w1_ref, w2_ref, gpre_ref, gpost_ref, o_ref, h_ref, acc_ref):
    j = pl.program_id(1)

    @pl.when(j == 0)
    def _():
        h_ref[...] = _rms(x_ref[...], gpre_ref[...]).astype(BF16)
        acc_ref[...] = jnp.zeros(acc_ref.shape, F32)

    a = jnp.maximum(_dot(h_ref[...], w1_ref[...]), 0.0)
    acc_ref[...] += _dot((a * a).astype(BF16), w2_ref[...])

    @pl.when(j == pl.num_programs(1) - 1)
    def _():
        o_ref[...] = x_ref[...] + _rms(acc_ref[...], gpost_ref[...])


def mlp(x, w1, w2, gpre, gpost, *, tm, tf):
    M, D = x.shape
    FF = w1.shape[1]
    return pl.pallas_call(
        _mlp_kernel,
        grid=(M // tm, FF // tf),
        in_specs=[pl.BlockSpec((tm, D), lambda i, j: (i, 0)),
                  pl.BlockSpec((D, tf), lambda i, j: (0, j)),
                  pl.BlockSpec((tf, D), lambda i, j: (j, 0)),
                  pl.BlockSpec((1, D), lambda i, j: (0, 0)),
                  pl.BlockSpec((1, D), lambda i, j: (0, 0))],
        out_specs=pl.BlockSpec((tm, D), lambda i, j: (i, 0)),
        out_shape=jax.ShapeDtypeStruct((M, D), F32),
        scratch_shapes=[pltpu.VMEM((tm, D), BF16), pltpu.VMEM((tm, D), F32)],
        compiler_params=_cparams(("parallel", "arbitrary")),
        name="mlp",
    )(x, w1, w2, gpre, gpost)


def _rot_half_cols(w):
    half = w.shape[-1] // 2
    return jnp.concatenate([-w[..., half:], w[..., :half]], axis=-1)


def _pack_w_in(w):
    widths = (MLA_Q_LORA, MLA_KV_LORA, MLA_ROPE, 1024, 1024, 1024, IDX_HEADS * IDX_DIM, IDX_DIM, IDX_HEADS,
              1024, 256, 256, 256, 256, 256, 256, 3 * NSA_HEADS, N_BRANCH * D_MODEL)
    offs = np.concatenate([[0], np.cumsum(widths)])
    seg = [w[:, offs[i]:offs[i + 1]] for i in range(len(widths))]
    (q_lat, kv_lat, k_rope, dsa_q, dsa_k, dsa_v, idx_q, idx_k, idx_w,
     nsa_q, kc, vc, ks, vs, kw, vw, nsa_gate, bgate) = seg
    z = lambda n: jnp.zeros((w.shape[0], n), w.dtype)
    cols = [dsa_q, dsa_k, dsa_v, nsa_q, bgate, idx_q, kc, vc, ks, vs, kw, vw,
            q_lat, kv_lat, k_rope, z(LANE - MLA_ROPE), _rot_half_cols(k_rope), z(LANE - MLA_ROPE), z(LANE),
            idx_k, idx_k, idx_w, z(LANE - IDX_HEADS), nsa_gate, z(LANE - 3 * NSA_HEADS)]
    out = jnp.concatenate(cols, axis=1)
    out = jnp.concatenate([out, z(N_PACK - out.shape[1])], axis=1)
    return out.astype(BF16)


def _pack_mla_weights(w_uq, w_ukv):
    H = MLA_HEADS
    uq = w_uq.reshape(MLA_Q_LORA, H, MLA_NOPE + MLA_ROPE)
    nope = uq[:, :, :MLA_NOPE].reshape(MLA_Q_LORA, H * MLA_NOPE)
    rope = uq[:, :, MLA_NOPE:]
    pad = jnp.zeros((MLA_Q_LORA, H, LANE - MLA_ROPE), w_uq.dtype)
    wqr = jnp.concatenate([rope, pad], axis=-1).reshape(MLA_Q_LORA, H * LANE)
    wqrr = jnp.concatenate([_rot_half_cols(rope), pad], axis=-1).reshape(MLA_Q_LORA, H * LANE)
    ukv = w_ukv.reshape(MLA_KV_LORA, H, MLA_NOPE + MLA_V)
    wkv = jnp.concatenate([ukv[:, :, :MLA_NOPE].reshape(MLA_KV_LORA, H * MLA_NOPE),
                           ukv[:, :, MLA_NOPE:].reshape(MLA_KV_LORA, H * MLA_V)], axis=1)
    return nope.astype(BF16), wqr.astype(BF16), wqrr.astype(BF16), wkv.astype(BF16)


def _constants(S):
    half = MLA_ROPE // 2
    inv_freq = ROPE_BASE ** (-np.arange(0, MLA_ROPE, 2, dtype=np.float32) / MLA_ROPE)
    freq = np.zeros((1, LANE), np.float32)
    freq[0, :half] = inv_freq
    freq[0, half:2 * half] = inv_freq
    ncp = S // CMP_STRIDE
    ns = S // SEL_BLOCK
    c_start = np.arange(ncp) * CMP_STRIDE
    j_start = np.arange(ns) * SEL_BLOCK
    overlap = ((c_start[None, :] < j_start[:, None] + SEL_BLOCK) &
               (c_start[None, :] + CMP_LEN > j_start[:, None])).astype(np.float32)
    overlap[:, ncp - 1] = 0.0
    ovt = np.zeros((LANE, ncp), np.float32)
    ovt[:ns] = overlap
    ind = np.zeros((S, LANE), np.float32)
    ind[np.arange(S), np.arange(S) // SEL_BLOCK] = 1.0
    return jnp.asarray(freq), jnp.asarray(ovt, BF16), jnp.asarray(ind, BF16)


def kernel(x, mem, positions, rel_bias, norm_gains, w_in, mla_q_norm, mla_kv_norm, mla_w_uq, mla_w_ukv,
           nsa_cmp_pos, nsa_cmp_w1, nsa_cmp_w2, w_branch, w_out, cross_wq, cross_wkv, cross_wo, mlp_w1, mlp_w2):
    B, S, D = x.shape
    M = B * S
    depth = w_in.shape[0]
    tq = TQ if S % TQ == 0 else TQ_NSA
    tqn = TQ_NSA
    assert S % tq == 0 and S % tqn == 0 and D == D_MODEL
    topk = min(DSA_TOPK_MAX, S // 4)
    freq, ovt, ind = _constants(S)
    btiles = bias_tiles(rel_bias, tq)
    btiles_n = btiles if tqn == tq else bias_tiles(rel_bias, tqn)
    pos = positions.reshape(M, 1)
    xf = x.reshape(M, D)
    memf = mem.reshape(B * mem.shape[1], D)
    tm_big = 1024 if M % 1024 == 0 else tq
    tm = 512 if M % 512 == 0 else tq
    ncp = S // CMP_STRIDE
    row = lambda v: v.reshape(1, -1)

    for l in range(depth):
        g = norm_gains[l]
        proj = norm_matmul(xf, row(g[0]), _pack_w_in(w_in[l]), tm=tm_big, tn=TN_IN)

        wqn, wqr, wqrr, wkv = _pack_mla_weights(mla_w_uq[l], mla_w_ukv[l])
        qn, qr, kn, vv, kr = mla_prep(proj, pos, freq, row(mla_q_norm[l]), row(mla_kv_norm[l]),
                                      wqn, wqr, wqrr, wkv, tm=tm)
        o_a = mla_attention(qn, qr, kn, kr, vv, B=B, S=S, tq=tq)

        maskb = dsa_index_mask(proj, B=B, S=S, tqi=128, tk=tq, topk=topk)
        o_b = dsa_attention(proj, maskb, btiles, B=B, S=S, tq=tq)

        kvc_in = jnp.stack([
            proj[:, OFF_NSA_KV + a * 256:OFF_NSA_KV + (a + 1) * 256].reshape(B * ncp, CMP_STRIDE * 256)
            for a in range(2)])
        posflat = jnp.broadcast_to(nsa_cmp_pos[l].reshape(2, 1, CMP_LEN * NSA_HEAD_DIM),
                                   (2, 8, CMP_LEN * NSA_HEAD_DIM)).astype(BF16)
        kvc = nsa_compress(kvc_in, posflat, nsa_cmp_w1[l].astype(BF16), nsa_cmp_w2[l].astype(BF16), B=B, S=S)
        o_cmp, selneg = nsa_cmp_attention(proj, kvc, ovt, B=B, S=S, tq=tqn)
        o_c = nsa_main(proj, selneg, ind, btiles_n, o_cmp, B=B, S=S, tq=tqn)

        xf = merge_branches(xf, o_a, o_b, o_c, proj, w_branch[l].astype(BF16), w_out[l].astype(BF16),
                            row(g[1]), tm=tm)

        mkv = norm_matmul(memf, row(g[3]), cross_wkv[l].astype(BF16), tm=memf.shape[0] // B, tn=1024)
        xf = cross_attention(xf, mkv, cross_wq[l].astype(BF16), cross_wo[l].astype(BF16),
                             row(g[2]), row(g[4]), S=S, tm=tm)

        xf = mlp(xf, mlp_w1[l].astype(BF16), mlp_w2[l].astype(BF16), row(g[5]), row(g[6]), tm=tm, tf=1024)

    return xf.reshape(B, S, D)
```

```python
import functools
import math

import numpy as np
import jax
import jax.numpy as jnp
from jax import lax
from jax.experimental import pallas as pl
from jax.experimental.pallas import tpu as pltpu

F32 = jnp.float32
BF16 = jnp.bfloat16

D_MODEL = 1024
EPS = 1e-6
MLA_HEADS = 8
MLA_Q_LORA = 384
MLA_KV_LORA = 256
MLA_NOPE = 128
MLA_ROPE = 64
MLA_V = 128
ROPE_BASE = 10000.0
DSA_HEADS = 8
DSA_HEAD_DIM = 128
IDX_HEADS = 8
IDX_DIM = 64
DSA_TOPK_MAX = 256
NSA_HEADS = 8
NSA_GROUPS = 2
NSA_HG = NSA_HEADS // NSA_GROUPS
NSA_HEAD_DIM = 128
CMP_STRIDE = 16
CMP_LEN = 2 * CMP_STRIDE
CMP_HIDDEN = 256
SEL_BLOCK = 64
SEL_COUNT = 16
WINDOW = 512
REL_BUCKETS = 32
REL_MAX_DIST = 128
REL_HEADS = DSA_HEADS + NSA_HEADS
CROSS_HEADS = 4
CROSS_HEAD_DIM = D_MODEL // CROSS_HEADS
D_FF = 4 * D_MODEL
N_BRANCH = 3

LANE = 128
NEG = -1e30
VMEM_LIMIT = 48 * 1024 * 1024

TQ = 512
TQ_NSA = 256
HPS = 2
NSA_STREAMS = 2
BISECT_ITERS = 40
IDX_WSTEP = 512
TQ_IDX = 256
IDX_CHUNK = 256

OFF_DSA_Q = 0
OFF_DSA_K = 1024
OFF_DSA_V = 2048
OFF_NSA_Q = 3072
OFF_BGATE = 4096
OFF_IDX_Q = 7168
OFF_NSA_KV = 7680
OFF_MLA = 9216
OFF_IDX_K = 10240
OFF_IDX_W = 10368
OFF_NSA_GATE = 10496
N_PACK = 10752
TN_IN = 1536


def _cparams(sem):
    return pltpu.CompilerParams(dimension_semantics=sem, vmem_limit_bytes=VMEM_LIMIT)


def _rms(x, g):
    return x * lax.rsqrt(jnp.mean(x * x, axis=-1, keepdims=True) + EPS) * g


def _dot(a, b):
    return jnp.dot(a, b, preferred_element_type=F32)


def _dot_nt(a, b):
    return lax.dot_general(a, b, (((1,), (1,)), ((), ())), preferred_element_type=F32)


def _norm_mm_kernel(x_ref, g_ref, w_ref, o_ref, xn_ref):
    @pl.when(pl.program_id(1) == 0)
    def _():
        xn_ref[...] = _rms(x_ref[...].astype(F32), g_ref[...]).astype(BF16)

    o_ref[...] = _dot(xn_ref[...], w_ref[...]).astype(o_ref.dtype)


def norm_matmul(x, g, w, *, tm, tn):
    M, K = x.shape
    N = w.shape[1]
    return pl.pallas_call(
        _norm_mm_kernel,
        grid=(M // tm, N // tn),
        in_specs=[pl.BlockSpec((tm, K), lambda i, j: (i, 0)),
                  pl.BlockSpec((1, K), lambda i, j: (0, 0)),
                  pl.BlockSpec((K, tn), lambda i, j: (0, j))],
        out_specs=pl.BlockSpec((tm, tn), lambda i, j: (i, j)),
        out_shape=jax.ShapeDtypeStruct((M, N), BF16),
        scratch_shapes=[pltpu.VMEM((tm, K), BF16)],
        compiler_params=_cparams(("parallel", "arbitrary")),
        name="norm_matmul",
    )(x, g, w)


def _bias_tiles_kernel(tab_ref, bkt_ref, o_ref, *, inv_scale, h0):
    h = pl.program_id(0) + h0
    bkt = bkt_ref[...]
    far = tab_ref[REL_BUCKETS - 1, h]
    out = jnp.zeros(bkt.shape, F32)
    for b in range(REL_BUCKETS - 1):
        out = jnp.where(bkt == b, (tab_ref[b, h] - far) * inv_scale, out)
    o_ref[...] = out


def _bucket_tiles(tq):
    i = np.arange(tq)[:, None]
    j = np.arange(tq)[None, :]
    d = np.stack([i - j, tq + i - j]).astype(np.int32)
    n = jnp.maximum(jnp.asarray(d), 0)
    exact = REL_BUCKETS // 2
    nf = jnp.maximum(n, 1).astype(F32)
    log_b = exact + (jnp.log(nf / exact) / math.log(REL_MAX_DIST / exact) * (REL_BUCKETS - exact)).astype(jnp.int32)
    return jnp.where(n < exact, n, jnp.minimum(log_b, REL_BUCKETS - 1)).astype(jnp.int32)


def bias_tiles(rel_bias, tq, h0, nh):
    assert tq >= REL_MAX_DIST
    bkt = _bucket_tiles(tq)
    assert DSA_HEAD_DIM == NSA_HEAD_DIM
    return pl.pallas_call(
        functools.partial(_bias_tiles_kernel, inv_scale=DSA_HEAD_DIM ** 0.5, h0=h0),
        grid=(nh,),
        in_specs=[pl.BlockSpec(memory_space=pltpu.SMEM),
                  pl.BlockSpec((2, tq, tq), lambda h: (0, 0, 0))],
        out_specs=pl.BlockSpec((None, 2, tq, tq), lambda h: (h, 0, 0, 0)),
        out_shape=jax.ShapeDtypeStruct((nh, 2, tq, tq), F32),
        compiler_params=_cparams(("arbitrary",)),
        name="bias_tiles",
    )(rel_bias, bkt)


def _mla_prep_kernel(lat_ref, pos_ref, freq_ref, qg_ref, kvg_ref, wqn_ref, wqr_ref, wqrr_ref, wkv_ref,
                     qn_ref, qr_ref, kn_ref, v_ref, kr_ref):
    lat = lat_ref[...].astype(F32)
    qlat = _rms(lat[:, :MLA_Q_LORA], qg_ref[...]).astype(BF16)
    kvlat = _rms(lat[:, MLA_Q_LORA:MLA_Q_LORA + MLA_KV_LORA], kvg_ref[...]).astype(BF16)
    o = MLA_Q_LORA + MLA_KV_LORA
    kr = lat[:, o:o + LANE]
    krr = lat[:, o + LANE:o + 2 * LANE]
    ang = pos_ref[...].astype(F32) * freq_ref[...]
    cos, sin = jnp.cos(ang), jnp.sin(ang)
    cos8 = jnp.concatenate([cos] * MLA_HEADS, axis=1)
    sin8 = jnp.concatenate([sin] * MLA_HEADS, axis=1)
    qn_ref[...] = _dot(qlat, wqn_ref[...]).astype(BF16)
    qr_ref[...] = (_dot(qlat, wqr_ref[...]) * cos8 + _dot(qlat, wqrr_ref[...]) * sin8).astype(BF16)
    kv = _dot(kvlat, wkv_ref[...])
    nk = MLA_HEADS * MLA_NOPE
    kn_ref[...] = kv[:, :nk].astype(BF16)
    v_ref[...] = kv[:, nk:].astype(BF16)
    kr_ref[...] = (kr * cos + krr * sin).astype(BF16)


def mla_prep(proj, pos, freq, qg, kvg, wqn, wqr, wqrr, wkv, *, tm):
    M = proj.shape[0]
    HD = MLA_HEADS * LANE
    full = lambda a: pl.BlockSpec(a.shape, lambda i: (0, 0))
    outs = [jax.ShapeDtypeStruct((M, HD), BF16)] * 4 + [jax.ShapeDtypeStruct((M, LANE), BF16)]
    return pl.pallas_call(
        _mla_prep_kernel,
        grid=(M // tm,),
        in_specs=[pl.BlockSpec((tm, 1024), lambda i: (i, OFF_MLA // 1024)),
                  pl.BlockSpec((tm, 1), lambda i: (i, 0)),
                  full(freq), full(qg), full(kvg), full(wqn), full(wqr), full(wqrr), full(wkv)],
        out_specs=[pl.BlockSpec((tm, HD), lambda i: (i, 0))] * 4 + [pl.BlockSpec((tm, LANE), lambda i: (i, 0))],
        out_shape=outs,
        compiler_params=_cparams(("parallel",)),
        name="mla_prep",
    )(proj, pos, freq, qg, kvg, wqn, wqr, wqrr, wkv)


def _softmax_init(m_ref, acc_ref):
    m_ref[...] = jnp.full(m_ref.shape, NEG, F32)
    acc_ref[...] = jnp.zeros(acc_ref.shape, F32)


def _with_ones(v):
    return jnp.concatenate([v, jnp.ones(v.shape, v.dtype)], axis=1)


def _softmax_step(s, v, m_ref, acc_ref, scale):
    c = scale * math.log2(math.e)
    m_prev = m_ref[...]
    m_new = jnp.maximum(m_prev, jnp.max(s, axis=-1, keepdims=True))
    alpha = jnp.exp2((m_prev - m_new) * c)
    p = jnp.exp2((s - jnp.tile(m_new, (1, s.shape[1] // LANE))) * c)
    acc_ref[...] = jnp.tile(alpha, (1, 2)) * acc_ref[...] + _dot(p.astype(BF16), _with_ones(v))
    m_ref[...] = m_new


def _softmax_finish(acc_ref):
    acc = acc_ref[...]
    return acc[:, :LANE] / acc[:, LANE:]


def _tri(tq, tk):
    row = lax.broadcasted_iota(jnp.int32, (tq, tk), 0)
    col = lax.broadcasted_iota(jnp.int32, (tq, tk), 1)
    return row, col


def _mla_attn_kernel(qn_ref, qr_ref, kn_ref, kr_ref, v_ref, o_ref, m_ref, acc_ref, *, tq, scale):
    qi = pl.program_id(2)
    hs = range(HPS)
    q = [jnp.concatenate([qn_ref[:, h * LANE:(h + 1) * LANE], qr_ref[:, h * LANE:(h + 1) * LANE]], axis=-1)
         for h in hs]
    for h in hs:
        _softmax_init(m_ref.at[h], acc_ref.at[h])

    def scores(j):
        sl = pl.ds(pl.multiple_of(j * tq, tq), tq)
        kr = kr_ref[sl, :]
        return [_dot_nt(q[h], jnp.concatenate([kn_ref[sl, h * LANE:(h + 1) * LANE], kr], axis=-1)) for h in hs], sl

    def body(j, c):
        s, sl = scores(j)
        for h in hs:
            _softmax_step(s[h], v_ref[sl, h * LANE:(h + 1) * LANE], m_ref.at[h], acc_ref.at[h], scale)
        return c

    lax.fori_loop(0, qi, body, 0)
    s, sl = scores(qi)
    row, col = _tri(tq, tq)
    for h in hs:
        _softmax_step(jnp.where(row >= col, s[h], NEG), v_ref[sl, h * LANE:(h + 1) * LANE],
                      m_ref.at[h], acc_ref.at[h], scale)
    for h in hs:
        o_ref[:, h * LANE:(h + 1) * LANE] = _softmax_finish(acc_ref.at[h]).astype(o_ref.dtype)


def mla_attention(qn, qr, kn, kr, v, *, B, S, tq):
    M = B * S
    nq = S // tq
    H = MLA_HEADS
    scale = (MLA_NOPE + MLA_ROPE) ** -0.5
    w = HPS * LANE
    qspec = pl.BlockSpec((tq, w), lambda b, h, i: (b * nq + i, h))
    kspec = pl.BlockSpec((S, w), lambda b, h, i: (b, h))
    return pl.pallas_call(
        functools.partial(_mla_attn_kernel, tq=tq, scale=scale),
        grid=(B, H // HPS, nq),
        in_specs=[qspec, qspec, kspec, pl.BlockSpec((S, LANE), lambda b, h, i: (b, 0)), kspec],
        out_specs=qspec,
        out_shape=jax.ShapeDtypeStruct((M, H * MLA_V), BF16),
        scratch_shapes=[pltpu.VMEM((HPS, tq, LANE), F32), pltpu.VMEM((HPS, tq, 2 * LANE), F32)],
        compiler_params=_cparams(("parallel", "parallel", "arbitrary")),
        name="mla_attention",
    )(qn, qr, kn, kr, v)


def _dsa_index_body(iq_ref, ik_ref, iw_ref, mb_ref, sc_ref, *, t0, W, wstep, tq, tk, S, topk, iters):
    iq = iq_ref[...]
    iw = iw_ref[...].astype(F32)
    lane = lax.broadcasted_iota(jnp.int32, (tq, LANE), 1)
    qs = []
    for h in range(IDX_HEADS):
        blk = iq[:, (h // 2) * LANE:(h // 2 + 1) * LANE]
        keep = (lane >= IDX_DIM) if h % 2 else (lane < IDX_DIM)
        qs.append(jnp.where(keep, blk, jnp.zeros_like(blk)))
    q8 = jnp.concatenate(qs, axis=0)
    wcol = [iw[:, h:h + 1] for h in range(IDX_HEADS)]
    cw = IDX_CHUNK
    row = lax.broadcasted_iota(jnp.int32, (tq, cw), 0) + t0
    col0 = lax.broadcasted_iota(jnp.int32, (tq, cw), 1)
    mn = jnp.full((tq, cw), -NEG, F32)
    mx = jnp.full((tq, cw), NEG, F32)
    for c in range(W // cw):
        res = _dot_nt(q8, ik_ref[c * cw:(c + 1) * cw, :])
        sc = wcol[0] * jnp.maximum(res[0:tq], 0.0)
        for h in range(1, IDX_HEADS):
            sc = sc + wcol[h] * jnp.maximum(res[h * tq:(h + 1) * tq], 0.0)
        if (c + 1) * cw > W - wstep:
            causal = (col0 + c * cw) <= row
            mn = jnp.minimum(mn, jnp.where(causal, sc, -NEG))
            sc = jnp.where(causal, sc, NEG)
        else:
            mn = jnp.minimum(mn, sc)
        mx = jnp.maximum(mx, sc)
        sc_ref[:, c * cw:(c + 1) * cw] = sc
    col = lax.broadcasted_iota(jnp.int32, (tq, W), 1)
    n_causal = (lax.broadcasted_iota(jnp.int32, (tq, 1), 0) + (t0 + 1)).astype(F32)
    kf = jnp.minimum(float(topk), n_causal)

    def count(pred):
        return jnp.sum(jnp.where(pred, 1.0, 0.0), axis=-1, keepdims=True)

    lo0 = jnp.min(mn, axis=-1, keepdims=True)
    mx = jnp.max(mx, axis=-1, keepdims=True)
    hi0 = mx + (jnp.abs(mx) * 1e-6 + 1.0)

    def bisect(c):
        lo, hi, n_lo = c
        mid = 0.5 * (lo + hi)
        n_mid = count(sc_ref[:, 0:W] >= mid)
        ge = n_mid >= kf
        return jnp.where(ge, mid, lo), jnp.where(ge, hi, mid), jnp.where(ge, n_mid, n_lo)

    def cond(c):
        it, _, _, n_lo = c
        return jnp.logical_and(it < iters, jnp.max(n_lo - kf) > 0.0)

    def body(c):
        it, lo, hi, n_lo = c
        return (it + 2,) + bisect(bisect((lo, hi, n_lo)))

    _, lo, hi, n_lo = lax.while_loop(cond, body, (jnp.int32(0), lo0, hi0, n_causal))
    tied = jnp.max(n_lo - kf) > 0.0

    def write(sel):
        mb = jnp.where(sel, 0.0, NEG).astype(mb_ref.dtype)
        for j in range(S // tk):
            if (j + 1) * tk <= W:
                mb_ref[j] = mb[:, j * tk:(j + 1) * tk]
            elif j * tk >= W:
                mb_ref[j] = jnp.full((tq, tk), NEG, mb_ref.dtype)
            else:
                mb_ref[j, :, 0:W - j * tk] = mb[:, j * tk:W]
                mb_ref[j, :, W - j * tk:tk] = jnp.full((tq, (j + 1) * tk - W), NEG, mb_ref.dtype)

    @pl.when(jnp.logical_not(tied))
    def _():
        write(sc_ref[:, 0:W] >= lo)

    @pl.when(tied)
    def _():
        sc = sc_ref[:, 0:W]
        above = sc >= hi
        band = (sc >= lo) & jnp.logical_not(above)
        need = kf - count(above)
        colf = col.astype(F32)

        def ibisect(_, c):
            jlo, jhi = c
            jm = jnp.floor(0.5 * (jlo + jhi))
            ok = count(band & (colf <= jm)) >= need
            return jnp.where(ok, jlo, jm), jnp.where(ok, jm, jhi)

        steps = max(1, int(math.ceil(math.log2(W))) + 1)
        _, jhi = lax.fori_loop(0, steps, ibisect,
                               (jnp.full((tq, 1), -1.0, F32), jnp.full((tq, 1), float(W - 1), F32)))
        write(above | (band & (colf <= jhi)))


def _dsa_index_kernel(iq_ref, ik_ref, iw_ref, mb_ref, sc_ref, *, tq, wstep, **kw):
    t0 = pl.program_id(1) * tq
    S = kw["S"]
    for v in range(S // wstep):
        @pl.when((t0 + tq - 1) // wstep == v)
        def _(v=v):
            _dsa_index_body(iq_ref, ik_ref, iw_ref, mb_ref, sc_ref, t0=t0, W=(v + 1) * wstep, wstep=wstep,
                            tq=tq, **kw)


def dsa_index_mask(proj, *, B, S, tqi, tk, topk):
    nq = S // tqi
    nk = S // tk
    wstep = math.gcd(S, IDX_WSTEP)
    assert wstep % tqi == 0 and wstep % IDX_CHUNK == 0 and wstep % tk == 0
    return pl.pallas_call(
        functools.partial(_dsa_index_kernel, tq=tqi, wstep=math.gcd(S, IDX_WSTEP), tk=tk, S=S, topk=topk,
                          iters=BISECT_ITERS),
        grid=(B, nq),
        in_specs=[pl.BlockSpec((tqi, 512), lambda b, i: (b * nq + i, OFF_IDX_Q // 512)),
                  pl.BlockSpec((S, LANE), lambda b, i: (b, OFF_IDX_K // LANE)),
                  pl.BlockSpec((tqi, LANE), lambda b, i: (b * nq + i, OFF_IDX_W // LANE))],
        out_specs=pl.BlockSpec((None, nk, tqi, tk), lambda b, i: (b, 0, i, 0)),
        out_shape=jax.ShapeDtypeStruct((B, nk, S, tk), BF16),
        scratch_shapes=[pltpu.VMEM((tqi, S), F32)],
        compiler_params=_cparams(("parallel", "parallel")),
        name="dsa_index_mask",
    )(proj, proj, proj)


def _dsa_attn_kernel(q_ref, k_ref, v_ref, mb_ref, bias_ref, o_ref, m_ref, acc_ref, *, tq, scale):
    qi = pl.program_id(2)
    hs = range(HPS)
    cols = lambda h: slice(h * LANE, (h + 1) * LANE)
    q = [q_ref[:, cols(h)] for h in hs]
    for h in hs:
        _softmax_init(m_ref.at[h], acc_ref.at[h])

    def step(j, bias_idx):
        sl = pl.ds(pl.multiple_of(j * tq, tq), tq)
        mb = mb_ref[j].astype(F32)
        s = [_dot_nt(q[h], k_ref[sl, cols(h)]) + mb for h in hs]
        for h in hs:
            sh = s[h] if bias_idx is None else s[h] + bias_ref[h, bias_idx]
            _softmax_step(sh, v_ref[sl, cols(h)], m_ref.at[h], acc_ref.at[h], scale)

    def body(j, c):
        step(j, None)
        return c

    lax.fori_loop(0, jnp.maximum(qi - 1, 0), body, 0)

    @pl.when(qi >= 1)
    def _():
        step(qi - 1, 1)

    step(qi, 0)
    for h in hs:
        o_ref[:, cols(h)] = _softmax_finish(acc_ref.at[h]).astype(o_ref.dtype)


def dsa_attention(proj, maskb, btiles, *, B, S, tq):
    M = B * S
    nq = S // tq
    H = DSA_HEADS
    scale = DSA_HEAD_DIM ** -0.5
    w = HPS * LANE
    return pl.pallas_call(
        functools.partial(_dsa_attn_kernel, tq=tq, scale=scale),
        grid=(B, H // HPS, nq),
        in_specs=[pl.BlockSpec((tq, w), lambda b, h, i: (b * nq + i, OFF_DSA_Q // w + h)),
                  pl.BlockSpec((S, w), lambda b, h, i: (b, OFF_DSA_K // w + h)),
                  pl.BlockSpec((S, w), lambda b, h, i: (b, OFF_DSA_V // w + h)),
                  pl.BlockSpec((None, nq, tq, tq), lambda b, h, i: (b, 0, i, 0)),
                  pl.BlockSpec((HPS, 2, tq, tq), lambda b, h, i: (h, 0, 0, 0))],
        out_specs=pl.BlockSpec((tq, w), lambda b, h, i: (b * nq + i, h)),
        out_shape=jax.ShapeDtypeStruct((M, H * DSA_HEAD_DIM), BF16),
        scratch_shapes=[pltpu.VMEM((HPS, tq, LANE), F32), pltpu.VMEM((HPS, tq, 2 * LANE), F32)],
        compiler_params=_cparams(("parallel", "parallel", "arbitrary")),
        name="dsa_attention",
    )(proj, proj, proj, maskb, btiles)


def _nsa_compress_kernel(x_ref, pos_ref, w1_ref, w2_ref, o_ref, *, ncp):
    dk = NSA_HEAD_DIM
    posw = _dot(pos_ref[...], w1_ref[...])[0:1, :]
    for g in range(NSA_GROUPS):
        lo = jnp.zeros((ncp, CMP_HIDDEN), F32)
        hi = jnp.zeros((ncp, CMP_HIDDEN), F32)
        for l in range(CMP_STRIDE):
            xs = x_ref[:, (l * NSA_GROUPS + g) * dk:(l * NSA_GROUPS + g + 1) * dk]
            lo = lo + _dot(xs, w1_ref[l * dk:(l + 1) * dk, :])
            hi = hi + _dot(xs, w1_ref[(CMP_STRIDE + l) * dk:(CMP_STRIDE + l + 1) * dk, :])
        hid = lo + pltpu.roll(hi, ncp - 1, 0) + posw
        o_ref[g] = _dot(jax.nn.gelu(hid).astype(BF16), w2_ref[...]).astype(o_ref.dtype)


def nsa_compress(xkv, posflat, w1, w2, *, B, S):
    ncp = S // CMP_STRIDE
    G, dk = NSA_GROUPS, NSA_HEAD_DIM
    width = CMP_STRIDE * G * dk
    return pl.pallas_call(
        functools.partial(_nsa_compress_kernel, ncp=ncp),
        grid=(2, B),
        in_specs=[pl.BlockSpec((None, ncp, width), lambda a, b: (a, b, 0)),
                  pl.BlockSpec((None, 8, CMP_LEN * dk), lambda a, b: (a, 0, 0)),
                  pl.BlockSpec((None, CMP_LEN * dk, CMP_HIDDEN), lambda a, b: (a, 0, 0)),
                  pl.BlockSpec((None, CMP_HIDDEN, dk), lambda a, b: (a, 0, 0))],
        out_specs=pl.BlockSpec((None, None, G, ncp, dk), lambda a, b: (a, b, 0, 0, 0)),
        out_shape=jax.ShapeDtypeStruct((2, B, G, ncp, dk), BF16),
        compiler_params=_cparams(("parallel", "parallel")),
        name="nsa_compress",
    )(xkv, posflat, w1, w2)


def _nsa_cmp_kernel(q_ref, kc_ref, vc_ref, ov_ref, oc_ref, sel_ref, *, tq, ncp, ns, n_sel, scale):
    t0 = pl.program_id(2) * tq
    dk = NSA_HEAD_DIM
    kc = kc_ref[...]
    vc = vc_ref[...]
    trow = lax.broadcasted_iota(jnp.int32, (tq, ncp), 0) + t0
    ccol = lax.broadcasted_iota(jnp.int32, (tq, ncp), 1)
    vis = (ccol * CMP_STRIDE + (CMP_LEN - 1)) <= trow
    psum = jnp.zeros((tq, ncp), F32)
    for h in range(NSA_HG):
        s = jnp.where(vis, _dot_nt(q_ref[:, h * dk:(h + 1) * dk], kc) * scale, NEG)
        m = jnp.max(s, axis=-1, keepdims=True)
        p = jnp.where(vis, jnp.exp(s - m), 0.0)
        d = jnp.sum(p, axis=-1, keepdims=True)
        p = p / jnp.where(d > 0, d, 1.0)
        oc_ref[:, h * dk:(h + 1) * dk] = _dot(p.astype(BF16), vc).astype(oc_ref.dtype)
        psum = psum + p
    p_hi = psum.astype(BF16)
    p_lo = (psum - p_hi.astype(F32)).astype(BF16)
    ov = ov_ref[...]
    imp = _dot_nt(ov, p_hi) + _dot_nt(ov, p_lo)
    imp = imp[:ns]
    jrow = lax.broadcasted_iota(jnp.int32, (ns, tq), 0)
    tcol = lax.broadcasted_iota(jnp.int32, (ns, tq), 1) + t0
    blk_t = tcol // SEL_BLOCK
    forced = (jrow == 0) | (jrow == blk_t) | (jrow == blk_t - 1)
    val = jnp.where(forced, -NEG, imp)
    val = jnp.where(jrow * SEL_BLOCK > tcol, NEG, val)
    rank = jnp.zeros((ns, tq), F32)
    for j in range(ns):
        other = val[j:j + 1, :]
        ahead = (other > val) | ((other == val) & (jrow > j))
        rank = rank + jnp.where(ahead, 1.0, 0.0)
    selneg = jnp.where(rank < float(n_sel), 0.0, NEG)
    if ns < LANE:
        selneg = jnp.concatenate([selneg, jnp.zeros((LANE - ns, tq), F32)], axis=0)
    sel_ref[...] = selneg.T.astype(sel_ref.dtype)


def nsa_cmp_attention(proj, kvc, ovt, *, B, S, tq):
    M = B * S
    nq = S // tq
    G, dk = NSA_GROUPS, NSA_HEAD_DIM
    ncp = S // CMP_STRIDE
    ns = S // SEL_BLOCK
    assert ns <= LANE
    n_sel = min(SEL_COUNT, ns)
    gw = NSA_HG * dk
    return pl.pallas_call(
        functools.partial(_nsa_cmp_kernel, tq=tq, ncp=ncp, ns=ns, n_sel=n_sel, scale=dk ** -0.5),
        grid=(B, G, nq),
        in_specs=[pl.BlockSpec((tq, gw), lambda b, g, i: (b * nq + i, OFF_NSA_Q // gw + g)),
                  pl.BlockSpec((None, None, None, ncp, dk), lambda b, g, i: (0, b, g, 0, 0)),
                  pl.BlockSpec((None, None, None, ncp, dk), lambda b, g, i: (1, b, g, 0, 0)),
                  pl.BlockSpec((LANE, ncp), lambda b, g, i: (0, 0))],
        out_specs=[pl.BlockSpec((tq, gw), lambda b, g, i: (b * nq + i, g)),
                   pl.BlockSpec((None, None, tq, LANE), lambda b, g, i: (b, g, i, 0))],
        out_shape=[jax.ShapeDtypeStruct((M, G * gw), BF16),
                   jax.ShapeDtypeStruct((B, G, S, LANE), BF16)],
        compiler_params=_cparams(("parallel", "parallel", "parallel")),
        name="nsa_cmp_attention",
    )(proj, kvc, kvc, ovt)


def _nsa_main_kernel(q_ref, sel_ref, ks_ref, vs_ref, kw_ref, vw_ref, ind_ref, bias_ref, gate_ref, oc_ref,
                     o_ref, m_ref, acc_ref, *, tq, nback, scale):
    g = pl.program_id(1)
    qi = pl.program_id(2)
    dk = NSA_HEAD_DIM
    HG = NSA_HG
    ns = NSA_STREAMS
    hp = HG // ns
    st = range(ns)
    stack = lambda f: [jnp.concatenate([f(a * hp + i) for i in range(hp)], axis=0) for a in st]
    q = stack(lambda h: q_ref[:, h * dk:(h + 1) * dk])
    sel = jnp.concatenate([sel_ref[...]] * hp, axis=0)
    qa = [jnp.concatenate([q[a], sel], axis=1) for a in st]
    row, col = _tri(tq, tq)
    row = jnp.concatenate([row] * hp, axis=0)
    col = jnp.concatenate([col] * hp, axis=0)
    bias0 = stack(lambda h: bias_ref[h, 0])
    bias1 = stack(lambda h: bias_ref[h, 1])
    gate = jax.nn.sigmoid(gate_ref[...].astype(F32))

    def init():
        for a in st:
            _softmax_init(m_ref.at[a], acc_ref.at[a])

    def steps(s, v, fix):
        for a in st:
            _softmax_step(fix(a, s[a]), v, m_ref.at[a], acc_ref.at[a], scale)

    init()

    def sel_scores(j, width=tq):
        sl = pl.ds(pl.multiple_of(j * tq, tq), width)
        ka = jnp.concatenate([ks_ref[sl, :], ind_ref[sl, :]], axis=1)
        return [_dot_nt(qa[a], ka) for a in st], vs_ref[sl, :]

    n_far = jnp.maximum(qi - 1, 0)

    def sel_body(p, c):
        s, v = sel_scores(2 * p, 2 * tq)
        steps(s, v, lambda a, x: x)
        return c

    lax.fori_loop(0, n_far // 2, sel_body, 0)

    @pl.when(n_far % 2 == 1)
    def _():
        s, v = sel_scores(n_far - 1)
        steps(s, v, lambda a, x: x)

    @pl.when(qi >= 1)
    def _():
        s, v = sel_scores(qi - 1)
        steps(s, v, lambda a, x: x + bias1[a])

    s, v = sel_scores(qi)
    steps(s, v, lambda a, x: jnp.where(row >= col, x + bias0[a], NEG))
    o_s = [_softmax_finish(acc_ref.at[a]) for a in st]

    init()

    def win_scores(j):
        sl = pl.ds(pl.multiple_of(j * tq, tq), tq)
        kw = kw_ref[sl, :]
        return [_dot_nt(q[a], kw) for a in st], vw_ref[sl, :]

    @pl.when(qi >= nback)
    def _():
        s, v = win_scores(qi - nback)
        steps(s, v, lambda a, x: jnp.where(row < col, x + bias1[a] if nback == 1 else x, NEG))

    for back in range(nback - 1, 0, -1):
        @pl.when(qi >= back)
        def _(back=back):
            s, v = win_scores(qi - back)
            steps(s, v, lambda a, x: x + bias1[a] if back == 1 else x)

    s, v = win_scores(qi)
    steps(s, v, lambda a, x: jnp.where(row >= col, x + bias0[a], NEG))
    o_w = [_softmax_finish(acc_ref.at[a]) for a in st]

    lane = lax.broadcasted_iota(jnp.int32, gate.shape, 1)
    for h in range(HG):
        c = g * HG + h
        a, sl = h // hp, slice((h % hp) * tq, (h % hp + 1) * tq)
        gc = [jnp.sum(jnp.where(lane == br * NSA_HEADS + c, gate, 0.0), axis=-1, keepdims=True)
              for br in range(3)]
        o = gc[0] * oc_ref[:, h * dk:(h + 1) * dk].astype(F32) + gc[1] * o_s[a][sl] + gc[2] * o_w[a][sl]
        o_ref[:, h * dk:(h + 1) * dk] = o.astype(o_ref.dtype)


def nsa_main(proj, selneg, ind, btiles, o_c, *, B, S, tq):
    M = B * S
    nq = S // tq
    G, dk, HG = NSA_GROUPS, NSA_HEAD_DIM, NSA_HG
    gw = HG * dk
    assert WINDOW % tq == 0
    nback = WINDOW // tq
    kv = lambda n: pl.BlockSpec((S, dk), lambda b, g, i, n=n: (b, (OFF_NSA_KV + n * G * dk) // dk + g))
    return pl.pallas_call(
        functools.partial(_nsa_main_kernel, tq=tq, nback=nback, scale=dk ** -0.5),
        grid=(B, G, nq),
        in_specs=[pl.BlockSpec((tq, gw), lambda b, g, i: (b * nq + i, OFF_NSA_Q // gw + g)),
                  pl.BlockSpec((None, None, tq, LANE), lambda b, g, i: (b, g, i, 0)),
                  kv(2), kv(3), kv(4), kv(5),
                  pl.BlockSpec((S, LANE), lambda b, g, i: (0, 0)),
                  pl.BlockSpec((HG, 2, tq, tq), lambda b, g, i: (g, 0, 0, 0)),
                  pl.BlockSpec((tq, LANE), lambda b, g, i: (b * nq + i, OFF_NSA_GATE // LANE)),
                  pl.BlockSpec((tq, gw), lambda b, g, i: (b * nq + i, g))],
        out_specs=pl.BlockSpec((tq, gw), lambda b, g, i: (b * nq + i, g)),
        out_shape=jax.ShapeDtypeStruct((M, G * gw), BF16),
        scratch_shapes=[pltpu.VMEM((NSA_STREAMS, HG // NSA_STREAMS * tq, LANE), F32),
                        pltpu.VMEM((NSA_STREAMS, HG // NSA_STREAMS * tq, 2 * LANE), F32)],
        compiler_params=_cparams(("parallel", "parallel", "arbitrary")),
        name="nsa_main",
    )(proj, selneg, proj, proj, proj, proj, ind, btiles, proj, o_c)


def _merge_kernel(x_ref, oa_ref, ob_ref, oc_ref, g0_ref, g1_ref, g2_ref, wb_ref, wo_ref, gain_ref, o_ref):
    merged = None
    for o_r, g_r, br in ((oa_ref, g0_ref, 0), (ob_ref, g1_ref, 1), (oc_ref, g2_ref, 2)):
        t = jax.nn.sigmoid(g_r[...].astype(F32)) * _dot(o_r[...], wb_ref[br])
        merged = t if merged is None else merged + t
    y = _dot(merged.astype(BF16), wo_ref[...])
    o_ref[...] = x_ref[...] + _rms(y, gain_ref[...])


def merge_branches(x, o_a, o_b, o_c, proj, wb, wo, gain, *, tm):
    M, D = x.shape
    row = pl.BlockSpec((tm, D), lambda i: (i, 0))
    gate = lambda br: pl.BlockSpec((tm, D), lambda i, br=br: (i, OFF_BGATE // D + br))
    return pl.pallas_call(
        _merge_kernel,
        grid=(M // tm,),
        in_specs=[row, row, row, row, gate(0), gate(1), gate(2),
                  pl.BlockSpec(wb.shape, lambda i: (0, 0, 0)),
                  pl.BlockSpec(wo.shape, lambda i: (0, 0)),
                  pl.BlockSpec((1, D), lambda i: (0, 0))],
        out_specs=row,
        out_shape=jax.ShapeDtypeStruct((M, D), F32),
        compiler_params=_cparams(("parallel",)),
        name="merge_branches",
    )(x, o_a, o_b, o_c, proj, proj, proj, wb, wo, gain)


def _cross_kernel(x_ref, kv_ref, wq_ref, wo_ref, gpre_ref, gpost_ref, o_ref, *, scale):
    x = x_ref[...]
    h = _rms(x, gpre_ref[...]).astype(BF16)
    q = _dot(h, wq_ref[...]).astype(BF16)
    dh = CROSS_HEAD_DIM
    outs = []
    for hd in range(CROSS_HEADS):
        k = kv_ref[:, hd * dh:(hd + 1) * dh]
        v = kv_ref[:, D_MODEL + hd * dh:D_MODEL + (hd + 1) * dh]
        s = _dot_nt(q[:, hd * dh:(hd + 1) * dh], k) * scale
        p = jnp.exp(s - jnp.max(s, axis=-1, keepdims=True))
        p = p / jnp.sum(p, axis=-1, keepdims=True)
        outs.append(_dot(p.astype(BF16), v).astype(BF16))
    y = _dot(jnp.concatenate(outs, axis=1), wo_ref[...])
    o_ref[...] = x + _rms(y, gpost_ref[...])


def cross_attention(x, kv, wq, wo, gpre, gpost, *, S, tm):
    M, D = x.shape
    mlen = kv.shape[0] // (M // S)
    per_b = S // tm
    return pl.pallas_call(
        functools.partial(_cross_kernel, scale=CROSS_HEAD_DIM ** -0.5),
        grid=(M // tm,),
        in_specs=[pl.BlockSpec((tm, D), lambda i: (i, 0)),
                  pl.BlockSpec((mlen, 2 * D), lambda i: (i // per_b, 0)),
                  pl.BlockSpec(wq.shape, lambda i: (0, 0)),
                  pl.BlockSpec(wo.shape, lambda i: (0, 0)),
                  pl.BlockSpec((1, D), lambda i: (0, 0)),
                  pl.BlockSpec((1, D), lambda i: (0, 0))],
        out_specs=pl.BlockSpec((tm, D), lambda i: (i, 0)),
        out_shape=jax.ShapeDtypeStruct((M, D), F32),
        compiler_params=_cparams(("parallel",)),
        name="cross_attention",
    )(x, kv, wq, wo, gpre, gpost)


def _mlp_kernel(x_ref, w1_ref, w2_ref, gpre_ref, gpost_ref, o_ref, h_ref, acc_ref):
    j = pl.program_id(1)

    @pl.when(j == 0)
    def _():
        h_ref[...] = _rms(x_ref[...], gpre_ref[...]).astype(BF16)
        acc_ref[...] = jnp.zeros(acc_ref.shape, F32)

    a = jnp.maximum(_dot(h_ref[...], w1_ref[...]), 0.0)
    acc_ref[...] += _dot((a * a).astype(BF16), w2_ref[...])

    @pl.when(j == pl.num_programs(1) - 1)
    def _():
        o_ref[...] = x_ref[...] + _rms(acc_ref[...], gpost_ref[...])


def mlp(x, w1, w2, gpre, gpost, *, tm, tf):
    M, D = x.shape
    FF = w1.shape[1]
    return pl.pallas_call(
        _mlp_kernel,
        grid=(M // tm, FF // tf),
        in_specs=[pl.BlockSpec((tm, D), lambda i, j: (i, 0)),
                  pl.BlockSpec((D, tf), lambda i, j: (0, j)),
                  pl.BlockSpec((tf, D), lambda i, j: (j, 0)),
                  pl.BlockSpec((1, D), lambda i, j: (0, 0)),
                  pl.BlockSpec((1, D), lambda i, j: (0, 0))],
        out_specs=pl.BlockSpec((tm, D), lambda i, j: (i, 0)),
        out_shape=jax.ShapeDtypeStruct((M, D), F32),
        scratch_shapes=[pltpu.VMEM((tm, D), BF16), pltpu.VMEM((tm, D), F32)],
        compiler_params=_cparams(("parallel", "arbitrary")),
        name="mlp",
    )(x, w1, w2, gpre, gpost)


def _rot_half_cols(w):
    half = w.shape[-1] // 2
    return jnp.concatenate([-w[..., half:], w[..., :half]], axis=-1)


def _pack_w_in(w):
    widths = (MLA_Q_LORA, MLA_KV_LORA, MLA_ROPE, 1024, 1024, 1024, IDX_HEADS * IDX_DIM, IDX_DIM, IDX_HEADS,
              1024, 256, 256, 256, 256, 256, 256, 3 * NSA_HEADS, N_BRANCH * D_MODEL)
    offs = np.concatenate([[0], np.cumsum(widths)])
    seg = [w[:, offs[i]:offs[i + 1]] for i in range(len(widths))]
    (q_lat, kv_lat, k_rope, dsa_q, dsa_k, dsa_v, idx_q, idx_k, idx_w,
     nsa_q, kc, vc, ks, vs, kw, vw, nsa_gate, bgate) = seg
    z = lambda n: jnp.zeros((w.shape[0], n), w.dtype)
    cols = [dsa_q, dsa_k, dsa_v, nsa_q, bgate, idx_q, kc, vc, ks, vs, kw, vw,
            q_lat, kv_lat, k_rope, z(LANE - MLA_ROPE), _rot_half_cols(k_rope), z(LANE - MLA_ROPE), z(LANE),
            idx_k, idx_k, idx_w, z(LANE - IDX_HEADS), nsa_gate, z(LANE - 3 * NSA_HEADS)]
    out = jnp.concatenate(cols, axis=1)
    out = jnp.concatenate([out, z(N_PACK - out.shape[1])], axis=1)
    return out.astype(BF16)


def _pack_mla_weights(w_uq, w_ukv):
    H = MLA_HEADS
    uq = w_uq.reshape(MLA_Q_LORA, H, MLA_NOPE + MLA_ROPE)
    nope = uq[:, :, :MLA_NOPE].reshape(MLA_Q_LORA, H * MLA_NOPE)
    rope = uq[:, :, MLA_NOPE:]
    pad = jnp.zeros((MLA_Q_LORA, H, LANE - MLA_ROPE), w_uq.dtype)
    wqr = jnp.concatenate([rope, pad], axis=-1).reshape(MLA_Q_LORA, H * LANE)
    wqrr = jnp.concatenate([_rot_half_cols(rope), pad], axis=-1).reshape(MLA_Q_LORA, H * LANE)
    ukv = w_ukv.reshape(MLA_KV_LORA, H, MLA_NOPE + MLA_V)
    wkv = jnp.concatenate([ukv[:, :, :MLA_NOPE].reshape(MLA_KV_LORA, H * MLA_NOPE),
                           ukv[:, :, MLA_NOPE:].reshape(MLA_KV_LORA, H * MLA_V)], axis=1)
    return nope.astype(BF16), wqr.astype(BF16), wqrr.astype(BF16), wkv.astype(BF16)


def _constants(S):
    half = MLA_ROPE // 2
    inv_freq = ROPE_BASE ** (-np.arange(0, MLA_ROPE, 2, dtype=np.float32) / MLA_ROPE)
    freq = np.zeros((1, LANE), np.float32)
    freq[0, :half] = inv_freq
    freq[0, half:2 * half] = inv_freq
    ncp = S // CMP_STRIDE
    ns = S // SEL_BLOCK
    c_start = np.arange(ncp) * CMP_STRIDE
    j_start = np.arange(ns) * SEL_BLOCK
    overlap = ((c_start[None, :] < j_start[:, None] + SEL_BLOCK) &
               (c_start[None, :] + CMP_LEN > j_start[:, None])).astype(np.float32)
    overlap[:, ncp - 1] = 0.0
    ovt = np.zeros((LANE, ncp), np.float32)
    ovt[:ns] = overlap
    ind = np.zeros((S, LANE), np.float32)
    ind[np.arange(S), np.arange(S) // SEL_BLOCK] = 1.0
    return jnp.asarray(freq), jnp.asarray(ovt, BF16), jnp.asarray(ind, BF16)


def kernel(x, mem, positions, rel_bias, norm_gains, w_in, mla_q_norm, mla_kv_norm, mla_w_uq, mla_w_ukv,
           nsa_cmp_pos, nsa_cmp_w1, nsa_cmp_w2, w_branch, w_out, cross_wq, cross_wkv, cross_wo, mlp_w1, mlp_w2):
    B, S, D = x.shape
    M = B * S
    depth = w_in.shape[0]
    tq = TQ if S % TQ == 0 else TQ_NSA
    tqn = TQ_NSA
    assert S % tq == 0 and S % tqn == 0 and D == D_MODEL
    topk = min(DSA_TOPK_MAX, S // 4)
    freq, ovt, ind = _constants(S)
    btiles = bias_tiles(rel_bias, tq, 0, DSA_HEADS)
    btiles_n = bias_tiles(rel_bias, tqn, DSA_HEADS, NSA_HEADS)
    pos = positions.reshape(M, 1)
    xf = x.reshape(M, D)
    memf = mem.reshape(B * mem.shape[1], D)
    tm_big = 1024 if M % 1024 == 0 else tq
    tm = 512 if M % 512 == 0 else tq
    ncp = S // CMP_STRIDE
    row = lambda v: v.reshape(1, -1)

    for l in range(depth):
        g = norm_gains[l]
        proj = norm_matmul(xf, row(g[0]), _pack_w_in(w_in[l]), tm=tm_big, tn=TN_IN)

        wqn, wqr, wqrr, wkv = _pack_mla_weights(mla_w_uq[l], mla_w_ukv[l])
        qn, qr, kn, vv, kr = mla_prep(proj, pos, freq, row(mla_q_norm[l]), row(mla_kv_norm[l]),
                                      wqn, wqr, wqrr, wkv, tm=tm)
        o_a = mla_attention(qn, qr, kn, kr, vv, B=B, S=S, tq=tq)

        maskb = dsa_index_mask(proj, B=B, S=S, tqi=TQ_IDX, tk=tq, topk=topk)
        o_b = dsa_attention(proj, maskb, btiles, B=B, S=S, tq=tq)

        kvc_in = jnp.stack([
            proj[:, OFF_NSA_KV + a * 256:OFF_NSA_KV + (a + 1) * 256].reshape(B * ncp, CMP_STRIDE * 256)
            for a in range(2)])
        posflat = jnp.broadcast_to(nsa_cmp_pos[l].reshape(2, 1, CMP_LEN * NSA_HEAD_DIM),
                                   (2, 8, CMP_LEN * NSA_HEAD_DIM)).astype(BF16)
        kvc = nsa_compress(kvc_in, posflat, nsa_cmp_w1[l].astype(BF16), nsa_cmp_w2[l].astype(BF16), B=B, S=S)
        o_cmp, selneg = nsa_cmp_attention(proj, kvc, ovt, B=B, S=S, tq=tqn)
        o_c = nsa_main(proj, selneg, ind, btiles_n, o_cmp, B=B, S=S, tq=tqn)

        xf = merge_branches(xf, o_a, o_b, o_c, proj, w_branch[l].astype(BF16), w_out[l].astype(BF16),
                            row(g[1]), tm=tm)

        mkv = norm_matmul(memf, row(g[3]), cross_wkv[l].astype(BF16), tm=memf.shape[0] // B, tn=1024)
        xf = cross_attention(xf, mkv, cross_wq[l].astype(BF16), cross_wo[l].astype(BF16),
                             row(g[2]), row(g[4]), S=S, tm=tm)

        xf = mlp(xf, mlp_w1[l].astype(BF16), mlp_w2[l].astype(BF16), row(g[5]), row(g[6]), tm=tm, tf=1024)

    return xf.reshape(B, S, D)
```

```python
import functools
import math

import numpy as np
import jax
import jax.numpy as jnp
from jax import lax
from jax.experimental import pallas as pl
from jax.experimental.pallas import tpu as pltpu

F32 = jnp.float32
BF16 = jnp.bfloat16

D_MODEL = 1024
EPS = 1e-6
MLA_HEADS = 8
MLA_Q_LORA = 384
MLA_KV_LORA = 256
MLA_NOPE = 128
MLA_ROPE = 64
MLA_V = 128
ROPE_BASE = 10000.0
DSA_HEADS = 8
DSA_HEAD_DIM = 128
IDX_HEADS = 8
IDX_DIM = 64
DSA_TOPK_MAX = 256
NSA_HEADS = 8
NSA_GROUPS = 2
NSA_HG = NSA_HEADS // NSA_GROUPS
NSA_HEAD_DIM = 128
CMP_STRIDE = 16
CMP_LEN = 2 * CMP_STRIDE
CMP_HIDDEN = 256
SEL_BLOCK = 64
SEL_COUNT = 16
WINDOW = 512
REL_BUCKETS = 32
REL_MAX_DIST = 128
REL_HEADS = DSA_HEADS + NSA_HEADS
CROSS_HEADS = 4
CROSS_HEAD_DIM = D_MODEL // CROSS_HEADS
D_FF = 4 * D_MODEL
N_BRANCH = 3

LANE = 128
NEG = -1e30
VMEM_LIMIT = 48 * 1024 * 1024

TQ = 512
TQ_NSA = 256
HPS = 2
NSA_STREAMS = 2
BISECT_ITERS = 40
TQ_IDX = 256
IDX_CHUNK = 256
IDX_ROWS = 64

OFF_DSA_Q = 0
OFF_DSA_K = 1024
OFF_DSA_V = 2048
OFF_NSA_Q = 3072
OFF_BGATE = 4096
OFF_IDX_Q = 7168
OFF_NSA_KV = 7680
OFF_MLA = 9216
OFF_IDX_K = 10240
OFF_IDX_W = 10368
OFF_NSA_GATE = 10496
N_PACK = 10752
TN_IN = 1536


def _cparams(sem):
    return pltpu.CompilerParams(dimension_semantics=sem, vmem_limit_bytes=VMEM_LIMIT)


def _rms(x, g):
    return x * lax.rsqrt(jnp.mean(x * x, axis=-1, keepdims=True) + EPS) * g


def _dot(a, b):
    return jnp.dot(a, b, preferred_element_type=F32)


def _dot_nt(a, b):
    return lax.dot_general(a, b, (((1,), (1,)), ((), ())), preferred_element_type=F32)


def _norm_mm_kernel(x_ref, g_ref, w_ref, o_ref, xn_ref):
    @pl.when(pl.program_id(1) == 0)
    def _():
        xn_ref[...] = _rms(x_ref[...].astype(F32), g_ref[...]).astype(BF16)

    o_ref[...] = _dot(xn_ref[...], w_ref[...]).astype(o_ref.dtype)


def norm_matmul(x, g, w, *, tm, tn):
    M, K = x.shape
    N = w.shape[1]
    return pl.pallas_call(
        _norm_mm_kernel,
        grid=(M // tm, N // tn),
        in_specs=[pl.BlockSpec((tm, K), lambda i, j: (i, 0)),
                  pl.BlockSpec((1, K), lambda i, j: (0, 0)),
                  pl.BlockSpec((K, tn), lambda i, j: (0, j))],
        out_specs=pl.BlockSpec((tm, tn), lambda i, j: (i, j)),
        out_shape=jax.ShapeDtypeStruct((M, N), BF16),
        scratch_shapes=[pltpu.VMEM((tm, K), BF16)],
        compiler_params=_cparams(("parallel", "arbitrary")),
        name="norm_matmul",
    )(x, g, w)


def _bias_tiles_kernel(tab_ref, bkt_ref, o_ref, *, inv_scale, h0):
    h = pl.program_id(0) + h0
    bkt = bkt_ref[...]
    far = tab_ref[REL_BUCKETS - 1, h]
    out = jnp.zeros(bkt.shape, F32)
    for b in range(REL_BUCKETS - 1):
        out = jnp.where(bkt == b, (tab_ref[b, h] - far) * inv_scale, out)
    o_ref[...] = out


def _bucket_tiles(tq):
    i = np.arange(tq)[:, None]
    j = np.arange(tq)[None, :]
    d = np.stack([i - j, tq + i - j]).astype(np.int32)
    n = jnp.maximum(jnp.asarray(d), 0)
    exact = REL_BUCKETS // 2
    nf = jnp.maximum(n, 1).astype(F32)
    log_b = exact + (jnp.log(nf / exact) / math.log(REL_MAX_DIST / exact) * (REL_BUCKETS - exact)).astype(jnp.int32)
    return jnp.where(n < exact, n, jnp.minimum(log_b, REL_BUCKETS - 1)).astype(jnp.int32)


def bias_tiles(rel_bias, tq, h0, nh):
    assert tq >= REL_MAX_DIST
    bkt = _bucket_tiles(tq)
    assert DSA_HEAD_DIM == NSA_HEAD_DIM
    return pl.pallas_call(
        functools.partial(_bias_tiles_kernel, inv_scale=DSA_HEAD_DIM ** 0.5, h0=h0),
        grid=(nh,),
        in_specs=[pl.BlockSpec(memory_space=pltpu.SMEM),
                  pl.BlockSpec((2, tq, tq), lambda h: (0, 0, 0))],
        out_specs=pl.BlockSpec((None, 2, tq, tq), lambda h: (h, 0, 0, 0)),
        out_shape=jax.ShapeDtypeStruct((nh, 2, tq, tq), F32),
        compiler_params=_cparams(("arbitrary",)),
        name="bias_tiles",
    )(rel_bias, bkt)


def _mla_prep_kernel(lat_ref, pos_ref, freq_ref, qg_ref, kvg_ref, wqn_ref, wqr_ref, wqrr_ref, wkv_ref,
                     qn_ref, qr_ref, kn_ref, v_ref, kr_ref):
    lat = lat_ref[...].astype(F32)
    qlat = _rms(lat[:, :MLA_Q_LORA], qg_ref[...]).astype(BF16)
    kvlat = _rms(lat[:, MLA_Q_LORA:MLA_Q_LORA + MLA_KV_LORA], kvg_ref[...]).astype(BF16)
    o = MLA_Q_LORA + MLA_KV_LORA
    kr = lat[:, o:o + LANE]
    krr = lat[:, o + LANE:o + 2 * LANE]
    ang = pos_ref[...].astype(F32) * freq_ref[...]
    cos, sin = jnp.cos(ang), jnp.sin(ang)
    cos8 = jnp.concatenate([cos] * MLA_HEADS, axis=1)
    sin8 = jnp.concatenate([sin] * MLA_HEADS, axis=1)
    qn_ref[...] = _dot(qlat, wqn_ref[...]).astype(BF16)
    qr_ref[...] = (_dot(qlat, wqr_ref[...]) * cos8 + _dot(qlat, wqrr_ref[...]) * sin8).astype(BF16)
    kv = _dot(kvlat, wkv_ref[...])
    nk = MLA_HEADS * MLA_NOPE
    kn_ref[...] = kv[:, :nk].astype(BF16)
    v_ref[...] = kv[:, nk:].astype(BF16)
    kr_ref[...] = (kr * cos + krr * sin).astype(BF16)


def mla_prep(proj, pos, freq, qg, kvg, wqn, wqr, wqrr, wkv, *, tm):
    M = proj.shape[0]
    HD = MLA_HEADS * LANE
    full = lambda a: pl.BlockSpec(a.shape, lambda i: (0, 0))
    outs = [jax.ShapeDtypeStruct((M, HD), BF16)] * 4 + [jax.ShapeDtypeStruct((M, LANE), BF16)]
    return pl.pallas_call(
        _mla_prep_kernel,
        grid=(M // tm,),
        in_specs=[pl.BlockSpec((tm, 1024), lambda i: (i, OFF_MLA // 1024)),
                  pl.BlockSpec((tm, 1), lambda i: (i, 0)),
                  full(freq), full(qg), full(kvg), full(wqn), full(wqr), full(wqrr), full(wkv)],
        out_specs=[pl.BlockSpec((tm, HD), lambda i: (i, 0))] * 4 + [pl.BlockSpec((tm, LANE), lambda i: (i, 0))],
        out_shape=outs,
        compiler_params=_cparams(("parallel",)),
        name="mla_prep",
    )(proj, pos, freq, qg, kvg, wqn, wqr, wqrr, wkv)


def _softmax_init(m_ref, acc_ref):
    m_ref[...] = jnp.full(m_ref.shape, NEG, F32)
    acc_ref[...] = jnp.zeros(acc_ref.shape, F32)


def _with_ones(v):
    return jnp.concatenate([v, jnp.ones(v.shape, v.dtype)], axis=1)


def _softmax_step(s, v, m_ref, acc_ref, scale):
    c = scale * math.log2(math.e)
    m_prev = m_ref[...]
    m_new = jnp.maximum(m_prev, jnp.max(s, axis=-1, keepdims=True))
    alpha = jnp.exp2((m_prev - m_new) * c)
    p = jnp.exp2((s - jnp.tile(m_new, (1, s.shape[1] // LANE))) * c)
    acc_ref[...] = jnp.tile(alpha, (1, 2)) * acc_ref[...] + _dot(p.astype(BF16), _with_ones(v))
    m_ref[...] = m_new


def _softmax_finish(acc_ref):
    acc = acc_ref[...]
    return acc[:, :LANE] / acc[:, LANE:]


def _tri(tq, tk):
    row = lax.broadcasted_iota(jnp.int32, (tq, tk), 0)
    col = lax.broadcasted_iota(jnp.int32, (tq, tk), 1)
    return row, col


def _mla_attn_kernel(qn_ref, qr_ref, kn_ref, kr_ref, v_ref, o_ref, m_ref, acc_ref, *, tq, scale):
    qi = pl.program_id(2)
    hs = range(HPS)
    q = [jnp.concatenate([qn_ref[:, h * LANE:(h + 1) * LANE], qr_ref[:, h * LANE:(h + 1) * LANE]], axis=-1)
         for h in hs]
    for h in hs:
        _softmax_init(m_ref.at[h], acc_ref.at[h])

    def scores(j):
        sl = pl.ds(pl.multiple_of(j * tq, tq), tq)
        kr = kr_ref[sl, :]
        return [_dot_nt(q[h], jnp.concatenate([kn_ref[sl, h * LANE:(h + 1) * LANE], kr], axis=-1)) for h in hs], sl

    def body(j, c):
        s, sl = scores(j)
        for h in hs:
            _softmax_step(s[h], v_ref[sl, h * LANE:(h + 1) * LANE], m_ref.at[h], acc_ref.at[h], scale)
        return c

    lax.fori_loop(0, qi, body, 0)
    s, sl = scores(qi)
    row, col = _tri(tq, tq)
    for h in hs:
        _softmax_step(jnp.where(row >= col, s[h], NEG), v_ref[sl, h * LANE:(h + 1) * LANE],
                      m_ref.at[h], acc_ref.at[h], scale)
    for h in hs:
        o_ref[:, h * LANE:(h + 1) * LANE] = _softmax_finish(acc_ref.at[h]).astype(o_ref.dtype)


def mla_attention(qn, qr, kn, kr, v, *, B, S, tq):
    M = B * S
    nq = S // tq
    H = MLA_HEADS
    scale = (MLA_NOPE + MLA_ROPE) ** -0.5
    w = HPS * LANE
    qspec = pl.BlockSpec((tq, w), lambda b, h, i: (b * nq + i, h))
    kspec = pl.BlockSpec((S, w), lambda b, h, i: (b, h))
    return pl.pallas_call(
        functools.partial(_mla_attn_kernel, tq=tq, scale=scale),
        grid=(B, H // HPS, nq),
        in_specs=[qspec, qspec, kspec, pl.BlockSpec((S, LANE), lambda b, h, i: (b, 0)), kspec],
        out_specs=qspec,
        out_shape=jax.ShapeDtypeStruct((M, H * MLA_V), BF16),
        scratch_shapes=[pltpu.VMEM((HPS, tq, LANE), F32), pltpu.VMEM((HPS, tq, 2 * LANE), F32)],
        compiler_params=_cparams(("parallel", "parallel", "arbitrary")),
        name="mla_attention",
    )(qn, qr, kn, kr, v)


def _dsa_index_kernel(iq_ref, ik_ref, iw_ref, mb_ref, sc_ref, *, tq, tk, S, topk, iters):
    t0 = pl.program_id(1) * tq
    nk = S // tk
    nch = (t0 + tq - 1) // tk + 1
    hw = IDX_CHUNK
    nb = tk // LANE
    iq = iq_ref[...]
    iw = iw_ref[...].astype(F32)
    lane = lax.broadcasted_iota(jnp.int32, (tq, LANE), 1)
    lanef = lax.broadcasted_iota(jnp.int32, (IDX_ROWS, LANE), 1).astype(F32)
    qs = []
    for h in range(IDX_HEADS):
        blk = iq[:, (h // 2) * LANE:(h // 2 + 1) * LANE]
        keep = (lane >= IDX_DIM) if h % 2 else (lane < IDX_DIM)
        qs.append(jnp.where(keep, blk, jnp.zeros_like(blk)))
    q8 = jnp.concatenate(qs, axis=0)
    wcol = [iw[:, h:h + 1] for h in range(IDX_HEADS)]
    row = lax.broadcasted_iota(jnp.int32, (tq, hw), 0) + t0
    col0 = lax.broadcasted_iota(jnp.int32, (tq, hw), 1)

    def lane_fold(x, op):
        out = x[:, 0:LANE]
        for b in range(1, x.shape[1] // LANE):
            out = op(out, x[:, b * LANE:(b + 1) * LANE])
        return out

    def score_chunk(c, carry):
        mn, mx = carry
        for half in range(tk // hw):
            k0 = pl.multiple_of(c * tk + half * hw, hw)
            res = _dot_nt(q8, ik_ref[pl.ds(k0, hw), :])
            sc = wcol[0] * jnp.maximum(res[0:tq], 0.0)
            for h in range(1, IDX_HEADS):
                sc = sc + wcol[h] * jnp.maximum(res[h * tq:(h + 1) * tq], 0.0)
            causal = (col0 + k0) <= row
            mn = jnp.minimum(mn, lane_fold(jnp.where(causal, sc, -NEG), jnp.minimum))
            sc = jnp.where(causal, sc, NEG)
            mx = jnp.maximum(mx, lane_fold(sc, jnp.maximum))
            sc_ref[c, :, half * hw:(half + 1) * hw] = sc
        return mn, mx

    mn, mx = lax.fori_loop(0, nch, score_chunk,
                           (jnp.full((tq, LANE), -NEG, F32), jnp.full((tq, LANE), NEG, F32)))
    n_causal = (lax.broadcasted_iota(jnp.int32, (tq, 1), 0) + (t0 + 1)).astype(F32)
    kf = jnp.minimum(float(topk), n_causal)

    rs = IDX_ROWS

    def blocks(j):
        for r in range(tq // rs):
            for b in range(nb):
                rows = slice(r * rs, (r + 1) * rs)
                yield rows, (j, rows, slice(b * LANE, (b + 1) * LANE)), j * tk + b * LANE

    def count(pred):
        def body(c, acc):
            parts = [acc[r * rs:(r + 1) * rs] for r in range(tq // rs)]
            for rows, idx, off in blocks(c):
                r = rows.start // rs
                parts[r] = parts[r] + jnp.where(pred(rows, sc_ref[idx], off), 1.0, 0.0)
            return jnp.concatenate(parts, axis=0)

        acc = lax.fori_loop(0, nch, body, jnp.zeros((tq, LANE), F32))
        return jnp.sum(acc, axis=-1, keepdims=True)

    def bcast(x):
        strips = [jnp.broadcast_to(x[r * rs:(r + 1) * rs], (rs, LANE)) for r in range(tq // rs)]
        return lambda rows: strips[rows.start // rs]

    lo0 = jnp.min(mn, axis=-1, keepdims=True)
    mx = jnp.max(mx, axis=-1, keepdims=True)
    hi0 = mx + (jnp.abs(mx) * 1e-6 + 1.0)

    def bisect(c):
        lo, hi, n_lo = c
        mid = 0.5 * (lo + hi)
        mid_b = bcast(mid)
        n_mid = count(lambda rows, blk, off: blk >= mid_b(rows))
        ge = n_mid >= kf
        return jnp.where(ge, mid, lo), jnp.where(ge, hi, mid), jnp.where(ge, n_mid, n_lo)

    def cond(c):
        it, _, _, n_lo = c
        return jnp.logical_and(it < iters, jnp.max(n_lo - kf) > 0.0)

    def body(c):
        it, lo, hi, n_lo = c
        return (it + 2,) + bisect(bisect((lo, hi, n_lo)))

    _, lo, hi, n_lo = lax.while_loop(cond, body, (jnp.int32(0), lo0, hi0, n_causal))
    tied = jnp.max(n_lo - kf) > 0.0

    def write(sel):
        def body(j, c):
            for rows, idx, off in blocks(j):
                mb_ref[idx] = jnp.where(sel(rows, sc_ref[idx], off), 0.0, NEG).astype(mb_ref.dtype)
            return c

        lax.fori_loop(0, nch, body, 0)

    def fill(j, c):
        mb_ref[j] = jnp.full((tq, tk), NEG, mb_ref.dtype)
        return c

    lax.fori_loop(nch, nk, fill, 0)
    lo_b = bcast(lo)

    @pl.when(jnp.logical_not(tied))
    def _():
        write(lambda rows, blk, off: blk >= lo_b(rows))

    @pl.when(tied)
    def _():
        hi_b = bcast(hi)
        need = kf - count(lambda rows, blk, off: blk >= hi_b(rows))

        def in_band_upto(jmax_b):
            return lambda rows, blk, off: ((blk >= lo_b(rows)) & (blk < hi_b(rows)) &
                                           (lanef + off.astype(F32) <= jmax_b(rows)))

        def ibisect(_, c):
            jlo, jhi = c
            jm = jnp.floor(0.5 * (jlo + jhi))
            ok = count(in_band_upto(bcast(jm))) >= need
            return jnp.where(ok, jlo, jm), jnp.where(ok, jm, jhi)

        steps = max(1, int(math.ceil(math.log2(S))) + 1)
        _, jhi = lax.fori_loop(0, steps, ibisect,
                               (jnp.full((tq, 1), -1.0, F32), jnp.full((tq, 1), float(S - 1), F32)))
        tie_ok = in_band_upto(bcast(jhi))
        write(lambda rows, blk, off: (blk >= hi_b(rows)) | tie_ok(rows, blk, off))


def dsa_index_mask(proj, *, B, S, tqi, tk, topk):
    nq = S // tqi
    nk = S // tk
    assert tk % tqi == 0 and tk % IDX_CHUNK == 0
    return pl.pallas_call(
        functools.partial(_dsa_index_kernel, tq=tqi, tk=tk, S=S, topk=topk, iters=BISECT_ITERS),
        grid=(B, nq),
        in_specs=[pl.BlockSpec((tqi, 512), lambda b, i: (b * nq + i, OFF_IDX_Q // 512)),
                  pl.BlockSpec((S, LANE), lambda b, i: (b, OFF_IDX_K // LANE)),
                  pl.BlockSpec((tqi, LANE), lambda b, i: (b * nq + i, OFF_IDX_W // LANE))],
        out_specs=pl.BlockSpec((None, nk, tqi, tk), lambda b, i: (b, 0, i, 0)),
        out_shape=jax.ShapeDtypeStruct((B, nk, S, tk), BF16),
        scratch_shapes=[pltpu.VMEM((nk, tqi, tk), F32)],
        compiler_params=_cparams(("parallel", "parallel")),
        name="dsa_index_mask",
    )(proj, proj, proj)


def _dsa_attn_kernel(q_ref, k_ref, v_ref, mb_ref, bias_ref, o_ref, m_ref, acc_ref, *, tq, scale):
    qi = pl.program_id(2)
    hs = range(HPS)
    cols = lambda h: slice(h * LANE, (h + 1) * LANE)
    q = [q_ref[:, cols(h)] for h in hs]
    for h in hs:
        _softmax_init(m_ref.at[h], acc_ref.at[h])

    def step(j, bias_idx):
        sl = pl.ds(pl.multiple_of(j * tq, tq), tq)
        mb = mb_ref[j].astype(F32)
        s = [_dot_nt(q[h], k_ref[sl, cols(h)]) + mb for h in hs]
        for h in hs:
            sh = s[h] if bias_idx is None else s[h] + bias_ref[h, bias_idx]
            _softmax_step(sh, v_ref[sl, cols(h)], m_ref.at[h], acc_ref.at[h], scale)

    def body(j, c):
        step(j, None)
        return c

    lax.fori_loop(0, jnp.maximum(qi - 1, 0), body, 0)

    @pl.when(qi >= 1)
    def _():
        step(qi - 1, 1)

    step(qi, 0)
    for h in hs:
        o_ref[:, cols(h)] = _softmax_finish(acc_ref.at[h]).astype(o_ref.dtype)


def dsa_attention(proj, maskb, btiles, *, B, S, tq):
    M = B * S
    nq = S // tq
    H = DSA_HEADS
    scale = DSA_HEAD_DIM ** -0.5
    w = HPS * LANE
    return pl.pallas_call(
        functools.partial(_dsa_attn_kernel, tq=tq, scale=scale),
        grid=(B, H // HPS, nq),
        in_specs=[pl.BlockSpec((tq, w), lambda b, h, i: (b * nq + i, OFF_DSA_Q // w + h)),
                  pl.BlockSpec((S, w), lambda b, h, i: (b, OFF_DSA_K // w + h)),
                  pl.BlockSpec((S, w), lambda b, h, i: (b, OFF_DSA_V // w + h)),
                  pl.BlockSpec((None, nq, tq, tq), lambda b, h, i: (b, 0, i, 0)),
                  pl.BlockSpec((HPS, 2, tq, tq), lambda b, h, i: (h, 0, 0, 0))],
        out_specs=pl.BlockSpec((tq, w), lambda b, h, i: (b * nq + i, h)),
        out_shape=jax.ShapeDtypeStruct((M, H * DSA_HEAD_DIM), BF16),
        scratch_shapes=[pltpu.VMEM((HPS, tq, LANE), F32), pltpu.VMEM((HPS, tq, 2 * LANE), F32)],
        compiler_params=_cparams(("parallel", "parallel", "arbitrary")),
        name="dsa_attention",
    )(proj, proj, proj, maskb, btiles)


def _nsa_compress_kernel(x_ref, pos_ref, w1_ref, w2_ref, o_ref, *, ncp):
    dk = NSA_HEAD_DIM
    posw = _dot(pos_ref[...], w1_ref[...])[0:1, :]
    for g in range(NSA_GROUPS):
        lo = jnp.zeros((ncp, CMP_HIDDEN), F32)
        hi = jnp.zeros((ncp, CMP_HIDDEN), F32)
        for l in range(CMP_STRIDE):
            xs = x_ref[:, (l * NSA_GROUPS + g) * dk:(l * NSA_GROUPS + g + 1) * dk]
            lo = lo + _dot(xs, w1_ref[l * dk:(l + 1) * dk, :])
            hi = hi + _dot(xs, w1_ref[(CMP_STRIDE + l) * dk:(CMP_STRIDE + l + 1) * dk, :])
        hid = lo + pltpu.roll(hi, ncp - 1, 0) + posw
        o_ref[g] = _dot(jax.nn.gelu(hid).astype(BF16), w2_ref[...]).astype(o_ref.dtype)


def nsa_compress(xkv, posflat, w1, w2, *, B, S):
    ncp = S // CMP_STRIDE
    G, dk = NSA_GROUPS, NSA_HEAD_DIM
    width = CMP_STRIDE * G * dk
    return pl.pallas_call(
        functools.partial(_nsa_compress_kernel, ncp=ncp),
        grid=(2, B),
        in_specs=[pl.BlockSpec((None, ncp, width), lambda a, b: (a, b, 0)),
                  pl.BlockSpec((None, 8, CMP_LEN * dk), lambda a, b: (a, 0, 0)),
                  pl.BlockSpec((None, CMP_LEN * dk, CMP_HIDDEN), lambda a, b: (a, 0, 0)),
                  pl.BlockSpec((None, CMP_HIDDEN, dk), lambda a, b: (a, 0, 0))],
        out_specs=pl.BlockSpec((None, None, G, ncp, dk), lambda a, b: (a, b, 0, 0, 0)),
        out_shape=jax.ShapeDtypeStruct((2, B, G, ncp, dk), BF16),
        compiler_params=_cparams(("parallel", "parallel")),
        name="nsa_compress",
    )(xkv, posflat, w1, w2)


def _nsa_cmp_kernel(q_ref, kc_ref, vc_ref, ov_ref, oc_ref, sel_ref, *, tq, ncp, ns, n_sel, scale):
    t0 = pl.program_id(2) * tq
    dk = NSA_HEAD_DIM
    kc = kc_ref[...]
    vc = vc_ref[...]
    trow = lax.broadcasted_iota(jnp.int32, (tq, ncp), 0) + t0
    ccol = lax.broadcasted_iota(jnp.int32, (tq, ncp), 1)
    vis = (ccol * CMP_STRIDE + (CMP_LEN - 1)) <= trow
    psum = jnp.zeros((tq, ncp), F32)
    for h in range(NSA_HG):
        s = jnp.where(vis, _dot_nt(q_ref[:, h * dk:(h + 1) * dk], kc) * scale, NEG)
        m = jnp.max(s, axis=-1, keepdims=True)
        p = jnp.where(vis, jnp.exp(s - m), 0.0)
        d = jnp.sum(p, axis=-1, keepdims=True)
        p = p / jnp.where(d > 0, d, 1.0)
        oc_ref[:, h * dk:(h + 1) * dk] = _dot(p.astype(BF16), vc).astype(oc_ref.dtype)
        psum = psum + p
    p_hi = psum.astype(BF16)
    p_lo = (psum - p_hi.astype(F32)).astype(BF16)
    ov = ov_ref[...]
    imp = _dot_nt(ov, p_hi) + _dot_nt(ov, p_lo)
    imp = imp[:ns]
    jrow = lax.broadcasted_iota(jnp.int32, (ns, tq), 0)
    tcol = lax.broadcasted_iota(jnp.int32, (ns, tq), 1) + t0
    blk_t = tcol // SEL_BLOCK
    forced = (jrow == 0) | (jrow == blk_t) | (jrow == blk_t - 1)
    val = jnp.where(forced, -NEG, imp)
    val = jnp.where(jrow * SEL_BLOCK > tcol, NEG, val)
    rank = jnp.zeros((ns, tq), F32)
    for j in range(ns):
        other = val[j:j + 1, :]
        ahead = (other > val) | ((other == val) & (jrow > j))
        rank = rank + jnp.where(ahead, 1.0, 0.0)
    selneg = jnp.where(rank < float(n_sel), 0.0, NEG)
    if ns < LANE:
        selneg = jnp.concatenate([selneg, jnp.zeros((LANE - ns, tq), F32)], axis=0)
    sel_ref[...] = selneg.T.astype(sel_ref.dtype)


def nsa_cmp_attention(proj, kvc, ovt, *, B, S, tq):
    M = B * S
    nq = S // tq
    G, dk = NSA_GROUPS, NSA_HEAD_DIM
    ncp = S // CMP_STRIDE
    ns = S // SEL_BLOCK
    assert ns <= LANE
    n_sel = min(SEL_COUNT, ns)
    gw = NSA_HG * dk
    return pl.pallas_call(
        functools.partial(_nsa_cmp_kernel, tq=tq, ncp=ncp, ns=ns, n_sel=n_sel, scale=dk ** -0.5),
        grid=(B, G, nq),
        in_specs=[pl.BlockSpec((tq, gw), lambda b, g, i: (b * nq + i, OFF_NSA_Q // gw + g)),
                  pl.BlockSpec((None, None, None, ncp, dk), lambda b, g, i: (0, b, g, 0, 0)),
                  pl.BlockSpec((None, None, None, ncp, dk), lambda b, g, i: (1, b, g, 0, 0)),
                  pl.BlockSpec((LANE, ncp), lambda b, g, i: (0, 0))],
        out_specs=[pl.BlockSpec((tq, gw), lambda b, g, i: (b * nq + i, g)),
                   pl.BlockSpec((None, None, tq, LANE), lambda b, g, i: (b, g, i, 0))],
        out_shape=[jax.ShapeDtypeStruct((M, G * gw), BF16),
                   jax.ShapeDtypeStruct((B, G, S, LANE), BF16)],
        compiler_params=_cparams(("parallel", "parallel", "parallel")),
        name="nsa_cmp_attention",
    )(proj, kvc, kvc, ovt)


def _nsa_main_kernel(q_ref, sel_ref, ks_ref, vs_ref, kw_ref, vw_ref, ind_ref, bias_ref, gate_ref, oc_ref,
                     o_ref, m_ref, acc_ref, *, tq, nback, scale):
    g = pl.program_id(1)
    qi = pl.program_id(2)
    dk = NSA_HEAD_DIM
    HG = NSA_HG
    ns = NSA_STREAMS
    hp = HG // ns
    st = range(ns)
    stack = lambda f: [jnp.concatenate([f(a * hp + i) for i in range(hp)], axis=0) for a in st]
    q = stack(lambda h: q_ref[:, h * dk:(h + 1) * dk])
    sel = jnp.concatenate([sel_ref[...]] * hp, axis=0)
    qa = [jnp.concatenate([q[a], sel], axis=1) for a in st]
    row, col = _tri(tq, tq)
    row = jnp.concatenate([row] * hp, axis=0)
    col = jnp.concatenate([col] * hp, axis=0)
    bias0 = stack(lambda h: bias_ref[h, 0])
    bias1 = stack(lambda h: bias_ref[h, 1])
    gate = jax.nn.sigmoid(gate_ref[...].astype(F32))

    def init():
        for a in st:
            _softmax_init(m_ref.at[a], acc_ref.at[a])

    def steps(s, v, fix):
        for a in st:
            _softmax_step(fix(a, s[a]), v, m_ref.at[a], acc_ref.at[a], scale)

    init()

    def sel_scores(j, width=tq):
        sl = pl.ds(pl.multiple_of(j * tq, tq), width)
        ka = jnp.concatenate([ks_ref[sl, :], ind_ref[sl, :]], axis=1)
        return [_dot_nt(qa[a], ka) for a in st], vs_ref[sl, :]

    n_far = jnp.maximum(qi - 1, 0)

    def sel_body(p, c):
        s, v = sel_scores(2 * p, 2 * tq)
        steps(s, v, lambda a, x: x)
        return c

    lax.fori_loop(0, n_far // 2, sel_body, 0)

    @pl.when(n_far % 2 == 1)
    def _():
        s, v = sel_scores(n_far - 1)
        steps(s, v, lambda a, x: x)

    @pl.when(qi >= 1)
    def _():
        s, v = sel_scores(qi - 1)
        steps(s, v, lambda a, x: x + bias1[a])

    s, v = sel_scores(qi)
    steps(s, v, lambda a, x: jnp.where(row >= col, x + bias0[a], NEG))
    o_s = [_softmax_finish(acc_ref.at[a]) for a in st]

    init()

    def win_scores(j):
        sl = pl.ds(pl.multiple_of(j * tq, tq), tq)
        kw = kw_ref[sl, :]
        return [_dot_nt(q[a], kw) for a in st], vw_ref[sl, :]

    @pl.when(qi >= nback)
    def _():
        s, v = win_scores(qi - nback)
        steps(s, v, lambda a, x: jnp.where(row < col, x + bias1[a] if nback == 1 else x, NEG))

    for back in range(nback - 1, 0, -1):
        @pl.when(qi >= back)
        def _(back=back):
            s, v = win_scores(qi - back)
            steps(s, v, lambda a, x: x + bias1[a] if back == 1 else x)

    s, v = win_scores(qi)
    steps(s, v, lambda a, x: jnp.where(row >= col, x + bias0[a], NEG))
    o_w = [_softmax_finish(acc_ref.at[a]) for a in st]

    lane = lax.broadcasted_iota(jnp.int32, gate.shape, 1)
    for h in range(HG):
        c = g * HG + h
        a, sl = h // hp, slice((h % hp) * tq, (h % hp + 1) * tq)
        gc = [jnp.sum(jnp.where(lane == br * NSA_HEADS + c, gate, 0.0), axis=-1, keepdims=True)
              for br in range(3)]
        o = gc[0] * oc_ref[:, h * dk:(h + 1) * dk].astype(F32) + gc[1] * o_s[a][sl] + gc[2] * o_w[a][sl]
        o_ref[:, h * dk:(h + 1) * dk] = o.astype(o_ref.dtype)


def nsa_main(proj, selneg, ind, btiles, o_c, *, B, S, tq):
    M = B * S
    nq = S // tq
    G, dk, HG = NSA_GROUPS, NSA_HEAD_DIM, NSA_HG
    gw = HG * dk
    assert WINDOW % tq == 0
    nback = WINDOW // tq
    kv = lambda n: pl.BlockSpec((S, dk), lambda b, g, i, n=n: (b, (OFF_NSA_KV + n * G * dk) // dk + g))
    return pl.pallas_call(
        functools.partial(_nsa_main_kernel, tq=tq, nback=nback, scale=dk ** -0.5),
        grid=(B, G, nq),
        in_specs=[pl.BlockSpec((tq, gw), lambda b, g, i: (b * nq + i, OFF_NSA_Q // gw + g)),
                  pl.BlockSpec((None, None, tq, LANE), lambda b, g, i: (b, g, i, 0)),
                  kv(2), kv(3), kv(4), kv(5),
                  pl.BlockSpec((S, LANE), lambda b, g, i: (0, 0)),
                  pl.BlockSpec((HG, 2, tq, tq), lambda b, g, i: (g, 0, 0, 0)),
                  pl.BlockSpec((tq, LANE), lambda b, g, i: (b * nq + i, OFF_NSA_GATE // LANE)),
                  pl.BlockSpec((tq, gw), lambda b, g, i: (b * nq + i, g))],
        out_specs=pl.BlockSpec((tq, gw), lambda b, g, i: (b * nq + i, g)),
        out_shape=jax.ShapeDtypeStruct((M, G * gw), BF16),
        scratch_shapes=[pltpu.VMEM((NSA_STREAMS, HG // NSA_STREAMS * tq, LANE), F32),
                        pltpu.VMEM((NSA_STREAMS, HG // NSA_STREAMS * tq, 2 * LANE), F32)],
        compiler_params=_cparams(("parallel", "parallel", "arbitrary")),
        name="nsa_main",
    )(proj, selneg, proj, proj, proj, proj, ind, btiles, proj, o_c)


def _merge_kernel(x_ref, oa_ref, ob_ref, oc_ref, g0_ref, g1_ref, g2_ref, wb_ref, wo_ref, gain_ref, o_ref):
    merged = None
    for o_r, g_r, br in ((oa_ref, g0_ref, 0), (ob_ref, g1_ref, 1), (oc_ref, g2_ref, 2)):
        t = jax.nn.sigmoid(g_r[...].astype(F32)) * _dot(o_r[...], wb_ref[br])
        merged = t if merged is None else merged + t
    y = _dot(merged.astype(BF16), wo_ref[...])
    o_ref[...] = x_ref[...] + _rms(y, gain_ref[...])


def merge_branches(x, o_a, o_b, o_c, proj, wb, wo, gain, *, tm):
    M, D = x.shape
    row = pl.BlockSpec((tm, D), lambda i: (i, 0))
    gate = lambda br: pl.BlockSpec((tm, D), lambda i, br=br: (i, OFF_BGATE // D + br))
    return pl.pallas_call(
        _merge_kernel,
        grid=(M // tm,),
        in_specs=[row, row, row, row, gate(0), gate(1), gate(2),
                  pl.BlockSpec(wb.shape, lambda i: (0, 0, 0)),
                  pl.BlockSpec(wo.shape, lambda i: (0, 0)),
                  pl.BlockSpec((1, D), lambda i: (0, 0))],
        out_specs=row,
        out_shape=jax.ShapeDtypeStruct((M, D), F32),
        compiler_params=_cparams(("parallel",)),
        name="merge_branches",
    )(x, o_a, o_b, o_c, proj, proj, proj, wb, wo, gain)


def _cross_kernel(x_ref, kv_ref, wq_ref, wo_ref, gpre_ref, gpost_ref, o_ref, *, scale):
    x = x_ref[...]
    h = _rms(x, gpre_ref[...]).astype(BF16)
    q = _dot(h, wq_ref[...]).astype(BF16)
    dh = CROSS_HEAD_DIM
    outs = []
    for hd in range(CROSS_HEADS):
        k = kv_ref[:, hd * dh:(hd + 1) * dh]
        v = kv_ref[:, D_MODEL + hd * dh:D_MODEL + (hd + 1) * dh]
        s = _dot_nt(q[:, hd * dh:(hd + 1) * dh], k) * scale
        p = jnp.exp(s - jnp.max(s, axis=-1, keepdims=True))
        p = p / jnp.sum(p, axis=-1, keepdims=True)
        outs.append(_dot(p.astype(BF16), v).astype(BF16))
    y = _dot(jnp.concatenate(outs, axis=1), wo_ref[...])
    o_ref[...] = x + _rms(y, gpost_ref[...])


def cross_attention(x, kv, wq, wo, gpre, gpost, *, S, tm):
    M, D = x.shape
    mlen = kv.shape[0] // (M // S)
    per_b = S // tm
    return pl.pallas_call(
        functools.partial(_cross_kernel, scale=CROSS_HEAD_DIM ** -0.5),
        grid=(M // tm,),
        in_specs=[pl.BlockSpec((tm, D), lambda i: (i, 0)),
                  pl.BlockSpec((mlen, 2 * D), lambda i: (i // per_b, 0)),
                  pl.BlockSpec(wq.shape, lambda i: (0, 0)),
                  pl.BlockSpec(wo.shape, lambda i: (0, 0)),
                  pl.BlockSpec((1, D), lambda i: (0, 0)),
                  pl.BlockSpec((1, D), lambda i: (0, 0))],
        out_specs=pl.BlockSpec((tm, D), lambda i: (i, 0)),
        out_shape=jax.ShapeDtypeStruct((M, D), F32),
        compiler_params=_cparams(("parallel",)),
        name="cross_attention",
    )(x, kv, wq, wo, gpre, gpost)


def _mlp_kernel(x_ref, w1_ref, w2_ref, gpre_ref, gpost_ref, o_ref, h_ref, acc_ref):
    j = pl.program_id(1)

    @pl.when(j == 0)
    def _():
        h_ref[...] = _rms(x_ref[...], gpre_ref[...]).astype(BF16)
        acc_ref[...] = jnp.zeros(acc_ref.shape, F32)

    a = jnp.maximum(_dot(h_ref[...], w1_ref[...]), 0.0)
    acc_ref[...] += _dot((a * a).astype(BF16), w2_ref[...])

    @pl.when(j == pl.num_programs(1) - 1)
    def _():
        o_ref[...] = x_ref[...] + _rms(acc_ref[...], gpost_ref[...])


def mlp(x, w1, w2, gpre, gpost, *, tm, tf):
    M, D = x.shape
    FF = w1.shape[1]
    return pl.pallas_call(
        _mlp_kernel,
        grid=(M // tm, FF // tf),
        in_specs=[pl.BlockSpec((tm, D), lambda i, j: (i, 0)),
                  pl.BlockSpec((D, tf), lambda i, j: (0, j)),
                  pl.BlockSpec((tf, D), lambda i, j: (j, 0)),
                  pl.BlockSpec((1, D), lambda i, j: (0, 0)),
                  pl.BlockSpec((1, D), lambda i, j: (0, 0))],
        out_specs=pl.BlockSpec((tm, D), lambda i, j: (i, 0)),
        out_shape=jax.ShapeDtypeStruct((M, D), F32),
        scratch_shapes=[pltpu.VMEM((tm, D), BF16), pltpu.VMEM((tm, D), F32)],
        compiler_params=_cparams(("parallel", "arbitrary")),
        name="mlp",
    )(x, w1, w2, gpre, gpost)


def _rot_half_cols(w):
    half = w.shape[-1] // 2
    return jnp.concatenate([-w[..., half:], w[..., :half]], axis=-1)


def _pack_w_in(w):
    widths = (MLA_Q_LORA, MLA_KV_LORA, MLA_ROPE, 1024, 1024, 1024, IDX_HEADS * IDX_DIM, IDX_DIM, IDX_HEADS,
              1024, 256, 256, 256, 256, 256, 256, 3 * NSA_HEADS, N_BRANCH * D_MODEL)
    offs = np.concatenate([[0], np.cumsum(widths)])
    seg = [w[:, offs[i]:offs[i + 1]] for i in range(len(widths))]
    (q_lat, kv_lat, k_rope, dsa_q, dsa_k, dsa_v, idx_q, idx_k, idx_w,
     nsa_q, kc, vc, ks, vs, kw, vw, nsa_gate, bgate) = seg
    z = lambda n: jnp.zeros((w.shape[0], n), w.dtype)
    cols = [dsa_q, dsa_k, dsa_v, nsa_q, bgate, idx_q, kc, vc, ks, vs, kw, vw,
            q_lat, kv_lat, k_rope, z(LANE - MLA_ROPE), _rot_half_cols(k_rope), z(LANE - MLA_ROPE), z(LANE),
            idx_k, idx_k, idx_w, z(LANE - IDX_HEADS), nsa_gate, z(LANE - 3 * NSA_HEADS)]
    out = jnp.concatenate(cols, axis=1)
    out = jnp.concatenate([out, z(N_PACK - out.shape[1])], axis=1)
    return out.astype(BF16)


def _pack_mla_weights(w_uq, w_ukv):
    H = MLA_HEADS
    uq = w_uq.reshape(MLA_Q_LORA, H, MLA_NOPE + MLA_ROPE)
    nope = uq[:, :, :MLA_NOPE].reshape(MLA_Q_LORA, H * MLA_NOPE)
    rope = uq[:, :, MLA_NOPE:]
    pad = jnp.zeros((MLA_Q_LORA, H, LANE - MLA_ROPE), w_uq.dtype)
    wqr = jnp.concatenate([rope, pad], axis=-1).reshape(MLA_Q_LORA, H * LANE)
    wqrr = jnp.concatenate([_rot_half_cols(rope), pad], axis=-1).reshape(MLA_Q_LORA, H * LANE)
    ukv = w_ukv.reshape(MLA_KV_LORA, H, MLA_NOPE + MLA_V)
    wkv = jnp.concatenate([ukv[:, :, :MLA_NOPE].reshape(MLA_KV_LORA, H * MLA_NOPE),
                           ukv[:, :, MLA_NOPE:].reshape(MLA_KV_LORA, H * MLA_V)], axis=1)
    return nope.astype(BF16), wqr.astype(BF16), wqrr.astype(BF16), wkv.astype(BF16)


def _constants(S):
    half = MLA_ROPE // 2
    inv_freq = ROPE_BASE ** (-np.arange(0, MLA_ROPE, 2, dtype=np.float32) / MLA_ROPE)
    freq = np.zeros((1, LANE), np.float32)
    freq[0, :half] = inv_freq
    freq[0, half:2 * half] = inv_freq
    ncp = S // CMP_STRIDE
    ns = S // SEL_BLOCK
    c_start = np.arange(ncp) * CMP_STRIDE
    j_start = np.arange(ns) * SEL_BLOCK
    overlap = ((c_start[None, :] < j_start[:, None] + SEL_BLOCK) &
               (c_start[None, :] + CMP_LEN > j_start[:, None])).astype(np.float32)
    overlap[:, ncp - 1] = 0.0
    ovt = np.zeros((LANE, ncp), np.float32)
    ovt[:ns] = overlap
    ind = np.zeros((S, LANE), np.float32)
    ind[np.arange(S), np.arange(S) // SEL_BLOCK] = 1.0
    return jnp.asarray(freq), jnp.asarray(ovt, BF16), jnp.asarray(ind, BF16)


def kernel(x, mem, positions, rel_bias, norm_gains, w_in, mla_q_norm, mla_kv_norm, mla_w_uq, mla_w_ukv,
           nsa_cmp_pos, nsa_cmp_w1, nsa_cmp_w2, w_branch, w_out, cross_wq, cross_wkv, cross_wo, mlp_w1, mlp_w2):
    B, S, D = x.shape
    M = B * S
    depth = w_in.shape[0]
    tq = TQ if S % TQ == 0 else TQ_NSA
    tqn = TQ_NSA
    assert S % tq == 0 and S % tqn == 0 and D == D_MODEL
    topk = min(DSA_TOPK_MAX, S // 4)
    freq, ovt, ind = _constants(S)
    btiles = bias_tiles(rel_bias, tq, 0, DSA_HEADS)
    btiles_n = bias_tiles(rel_bias, tqn, DSA_HEADS, NSA_HEADS)
    pos = positions.reshape(M, 1)
    xf = x.reshape(M, D)
    memf = mem.reshape(B * mem.shape[1], D)
    tm_big = 1024 if M % 1024 == 0 else tq
    tm = 512 if M % 512 == 0 else tq
    ncp = S // CMP_STRIDE
    row = lambda v: v.reshape(1, -1)

    for l in range(depth):
        g = norm_gains[l]
        proj = norm_matmul(xf, row(g[0]), _pack_w_in(w_in[l]), tm=tm_big, tn=TN_IN)

        wqn, wqr, wqrr, wkv = _pack_mla_weights(mla_w_uq[l], mla_w_ukv[l])
        qn, qr, kn, vv, kr = mla_prep(proj, pos, freq, row(mla_q_norm[l]), row(mla_kv_norm[l]),
                                      wqn, wqr, wqrr, wkv, tm=tm)
        o_a = mla_attention(qn, qr, kn, kr, vv, B=B, S=S, tq=tq)

        maskb = dsa_index_mask(proj, B=B, S=S, tqi=TQ_IDX, tk=tq, topk=topk)
        o_b = dsa_attention(proj, maskb, btiles, B=B, S=S, tq=tq)

        kvc_in = jnp.stack([
            proj[:, OFF_NSA_KV + a * 256:OFF_NSA_KV + (a + 1) * 256].reshape(B * ncp, CMP_STRIDE * 256)
            for a in range(2)])
        posflat = jnp.broadcast_to(nsa_cmp_pos[l].reshape(2, 1, CMP_LEN * NSA_HEAD_DIM),
                                   (2, 8, CMP_LEN * NSA_HEAD_DIM)).astype(BF16)
        kvc = nsa_compress(kvc_in, posflat, nsa_cmp_w1[l].astype(BF16), nsa_cmp_w2[l].astype(BF16), B=B, S=S)
        o_cmp, selneg = nsa_cmp_attention(proj, kvc, ovt, B=B, S=S, tq=tqn)
        o_c = nsa_main(proj, selneg, ind, btiles_n, o_cmp, B=B, S=S, tq=tqn)

        xf = merge_branches(xf, o_a, o_b, o_c, proj, w_branch[l].astype(BF16), w_out[l].astype(BF16),
                            row(g[1]), tm=tm)

        mkv = norm_matmul(memf, row(g[3]), cross_wkv[l].astype(BF16), tm=memf.shape[0] // B, tn=1024)
        xf = cross_attention(xf, mkv, cross_wq[l].astype(BF16), cross_wo[l].astype(BF16),
                             row(g[2]), row(g[4]), S=S, tm=tm)

        xf = mlp(xf, mlp_w1[l].astype(BF16), mlp_w2[l].astype(BF16), row(g[5]), row(g[6]), tm=tm, tf=1024)

    return xf.reshape(B, S, D)
```

```python
import functools
import math

import numpy as np
import jax
import jax.numpy as jnp
from jax import lax
from jax.experimental import pallas as pl
from jax.experimental.pallas import tpu as pltpu

F32 = jnp.float32
BF16 = jnp.bfloat16

D_MODEL = 1024
EPS = 1e-6
MLA_HEADS = 8
MLA_Q_LORA = 384
MLA_KV_LORA = 256
MLA_NOPE = 128
MLA_ROPE = 64
MLA_V = 128
ROPE_BASE = 10000.0
DSA_HEADS = 8
DSA_HEAD_DIM = 128
IDX_HEADS = 8
IDX_DIM = 64
DSA_TOPK_MAX = 256
NSA_HEADS = 8
NSA_GROUPS = 2
NSA_HG = NSA_HEADS // NSA_GROUPS
NSA_HEAD_DIM = 128
CMP_STRIDE = 16
CMP_LEN = 2 * CMP_STRIDE
CMP_HIDDEN = 256
SEL_BLOCK = 64
SEL_COUNT = 16
WINDOW = 512
REL_BUCKETS = 32
REL_MAX_DIST = 128
REL_HEADS = DSA_HEADS + NSA_HEADS
CROSS_HEADS = 4
CROSS_HEAD_DIM = D_MODEL // CROSS_HEADS
D_FF = 4 * D_MODEL
N_BRANCH = 3

LANE = 128
NEG = -1e30
VMEM_LIMIT = 48 * 1024 * 1024

TQ = 512
TQ_NSA = 256
HPS = 2
NSA_STREAMS = 2
BISECT_ITERS = 32
IDX_TIE_CHECK = 18
TQ_IDX = 256
IDX_CHUNK = 256
IDX_ROWS = 64

OFF_DSA_Q = 0
OFF_DSA_K = 1024
OFF_DSA_V = 2048
OFF_NSA_Q = 3072
OFF_BGATE = 4096
OFF_IDX_Q = 7168
OFF_NSA_KV = 7680
OFF_MLA = 9216
OFF_IDX_K = 10240
OFF_IDX_W = 10368
OFF_NSA_GATE = 10496
N_PACK = 10752
TN_IN = 1536


def _cparams(sem):
    return pltpu.CompilerParams(dimension_semantics=sem, vmem_limit_bytes=VMEM_LIMIT)


def _rms(x, g):
    return x * lax.rsqrt(jnp.mean(x * x, axis=-1, keepdims=True) + EPS) * g


def _dot(a, b):
    return jnp.dot(a, b, preferred_element_type=F32)


def _dot_nt(a, b):
    return lax.dot_general(a, b, (((1,), (1,)), ((), ())), preferred_element_type=F32)


def _norm_mm_kernel(x_ref, g_ref, w_ref, o_ref, xn_ref):
    @pl.when(pl.program_id(1) == 0)
    def _():
        xn_ref[...] = _rms(x_ref[...].astype(F32), g_ref[...]).astype(BF16)

    o_ref[...] = _dot(xn_ref[...], w_ref[...]).astype(o_ref.dtype)


def norm_matmul(x, g, w, *, tm, tn):
    M, K = x.shape
    N = w.shape[1]
    return pl.pallas_call(
        _norm_mm_kernel,
        grid=(M // tm, N // tn),
        in_specs=[pl.BlockSpec((tm, K), lambda i, j: (i, 0)),
                  pl.BlockSpec((1, K), lambda i, j: (0, 0)),
                  pl.BlockSpec((K, tn), lambda i, j: (0, j))],
        out_specs=pl.BlockSpec((tm, tn), lambda i, j: (i, j)),
        out_shape=jax.ShapeDtypeStruct((M, N), BF16),
        scratch_shapes=[pltpu.VMEM((tm, K), BF16)],
        compiler_params=_cparams(("parallel", "arbitrary")),
        name="norm_matmul",
    )(x, g, w)


def _bias_tiles_kernel(tab_ref, bkt_ref, o_ref, *, inv_scale, h0):
    h = pl.program_id(0) + h0
    bkt = bkt_ref[...]
    far = tab_ref[REL_BUCKETS - 1, h]
    out = jnp.zeros(bkt.shape, F32)
    for b in range(REL_BUCKETS - 1):
        out = jnp.where(bkt == b, (tab_ref[b, h] - far) * inv_scale, out)
    o_ref[...] = out


def _bucket_tiles(tq):
    i = np.arange(tq)[:, None]
    j = np.arange(tq)[None, :]
    d = np.stack([i - j, tq + i - j]).astype(np.int32)
    n = jnp.maximum(jnp.asarray(d), 0)
    exact = REL_BUCKETS // 2
    nf = jnp.maximum(n, 1).astype(F32)
    log_b = exact + (jnp.log(nf / exact) / math.log(REL_MAX_DIST / exact) * (REL_BUCKETS - exact)).astype(jnp.int32)
    return jnp.where(n < exact, n, jnp.minimum(log_b, REL_BUCKETS - 1)).astype(jnp.int32)


def bias_tiles(rel_bias, tq, h0, nh):
    assert tq >= REL_MAX_DIST
    bkt = _bucket_tiles(tq)
    assert DSA_HEAD_DIM == NSA_HEAD_DIM
    return pl.pallas_call(
        functools.partial(_bias_tiles_kernel, inv_scale=DSA_HEAD_DIM ** 0.5, h0=h0),
        grid=(nh,),
        in_specs=[pl.BlockSpec(memory_space=pltpu.SMEM),
                  pl.BlockSpec((2, tq, tq), lambda h: (0, 0, 0))],
        out_specs=pl.BlockSpec((None, 2, tq, tq), lambda h: (h, 0, 0, 0)),
        out_shape=jax.ShapeDtypeStruct((nh, 2, tq, tq), F32),
        compiler_params=_cparams(("arbitrary",)),
        name="bias_tiles",
    )(rel_bias, bkt)


def _mla_prep_kernel(lat_ref, pos_ref, freq_ref, qg_ref, kvg_ref, wqn_ref, wqr_ref, wqrr_ref, wkv_ref,
                     qn_ref, qr_ref, kn_ref, v_ref, kr_ref):
    lat = lat_ref[...].astype(F32)
    qlat = _rms(lat[:, :MLA_Q_LORA], qg_ref[...]).astype(BF16)
    kvlat = _rms(lat[:, MLA_Q_LORA:MLA_Q_LORA + MLA_KV_LORA], kvg_ref[...]).astype(BF16)
    o = MLA_Q_LORA + MLA_KV_LORA
    kr = lat[:, o:o + LANE]
    krr = lat[:, o + LANE:o + 2 * LANE]
    ang = pos_ref[...].astype(F32) * freq_ref[...]
    cos, sin = jnp.cos(ang), jnp.sin(ang)
    cos8 = jnp.concatenate([cos] * MLA_HEADS, axis=1)
    sin8 = jnp.concatenate([sin] * MLA_HEADS, axis=1)
    qn_ref[...] = _dot(qlat, wqn_ref[...]).astype(BF16)
    qr_ref[...] = (_dot(qlat, wqr_ref[...]) * cos8 + _dot(qlat, wqrr_ref[...]) * sin8).astype(BF16)
    kv = _dot(kvlat, wkv_ref[...])
    nk = MLA_HEADS * MLA_NOPE
    kn_ref[...] = kv[:, :nk].astype(BF16)
    v_ref[...] = kv[:, nk:].astype(BF16)
    kr_ref[...] = (kr * cos + krr * sin).astype(BF16)


def mla_prep(proj, pos, freq, qg, kvg, wqn, wqr, wqrr, wkv, *, tm):
    M = proj.shape[0]
    HD = MLA_HEADS * LANE
    full = lambda a: pl.BlockSpec(a.shape, lambda i: (0, 0))
    outs = [jax.ShapeDtypeStruct((M, HD), BF16)] * 4 + [jax.ShapeDtypeStruct((M, LANE), BF16)]
    return pl.pallas_call(
        _mla_prep_kernel,
        grid=(M // tm,),
        in_specs=[pl.BlockSpec((tm, 1024), lambda i: (i, OFF_MLA // 1024)),
                  pl.BlockSpec((tm, 1), lambda i: (i, 0)),
                  full(freq), full(qg), full(kvg), full(wqn), full(wqr), full(wqrr), full(wkv)],
        out_specs=[pl.BlockSpec((tm, HD), lambda i: (i, 0))] * 4 + [pl.BlockSpec((tm, LANE), lambda i: (i, 0))],
        out_shape=outs,
        compiler_params=_cparams(("parallel",)),
        name="mla_prep",
    )(proj, pos, freq, qg, kvg, wqn, wqr, wqrr, wkv)


def _softmax_init(m_ref, acc_ref):
    m_ref[...] = jnp.full(m_ref.shape, NEG, F32)
    acc_ref[...] = jnp.zeros(acc_ref.shape, F32)


def _with_ones(v):
    return jnp.concatenate([v, jnp.ones(v.shape, v.dtype)], axis=1)


def _softmax_step(s, v, m_ref, acc_ref, scale):
    c = scale * math.log2(math.e)
    m_prev = m_ref[...]
    m_new = jnp.maximum(m_prev, jnp.max(s, axis=-1, keepdims=True))
    alpha = jnp.exp2((m_prev - m_new) * c)
    p = jnp.exp2((s - jnp.tile(m_new, (1, s.shape[1] // LANE))) * c)
    acc_ref[...] = jnp.tile(alpha, (1, 2)) * acc_ref[...] + _dot(p.astype(BF16), _with_ones(v))
    m_ref[...] = m_new


def _softmax_finish(acc_ref):
    acc = acc_ref[...]
    return acc[:, :LANE] / acc[:, LANE:]


def _tri(tq, tk):
    row = lax.broadcasted_iota(jnp.int32, (tq, tk), 0)
    col = lax.broadcasted_iota(jnp.int32, (tq, tk), 1)
    return row, col


def _mla_attn_kernel(qn_ref, qr_ref, kn_ref, kr_ref, v_ref, o_ref, m_ref, acc_ref, *, tq, scale):
    qi = pl.program_id(2)
    hs = range(HPS)
    q = [jnp.concatenate([qn_ref[:, h * LANE:(h + 1) * LANE], qr_ref[:, h * LANE:(h + 1) * LANE]], axis=-1)
         for h in hs]
    for h in hs:
        _softmax_init(m_ref.at[h], acc_ref.at[h])

    def scores(j):
        sl = pl.ds(pl.multiple_of(j * tq, tq), tq)
        kr = kr_ref[sl, :]
        return [_dot_nt(q[h], jnp.concatenate([kn_ref[sl, h * LANE:(h + 1) * LANE], kr], axis=-1)) for h in hs], sl

    def body(j, c):
        s, sl = scores(j)
        for h in hs:
            _softmax_step(s[h], v_ref[sl, h * LANE:(h + 1) * LANE], m_ref.at[h], acc_ref.at[h], scale)
        return c

    lax.fori_loop(0, qi, body, 0)
    s, sl = scores(qi)
    row, col = _tri(tq, tq)
    for h in hs:
        _softmax_step(jnp.where(row >= col, s[h], NEG), v_ref[sl, h * LANE:(h + 1) * LANE],
                      m_ref.at[h], acc_ref.at[h], scale)
    for h in hs:
        o_ref[:, h * LANE:(h + 1) * LANE] = _softmax_finish(acc_ref.at[h]).astype(o_ref.dtype)


def mla_attention(qn, qr, kn, kr, v, *, B, S, tq):
    M = B * S
    nq = S // tq
    H = MLA_HEADS
    scale = (MLA_NOPE + MLA_ROPE) ** -0.5
    w = HPS * LANE
    qspec = pl.BlockSpec((tq, w), lambda b, h, i: (b * nq + i, h))
    kspec = pl.BlockSpec((S, w), lambda b, h, i: (b, h))
    return pl.pallas_call(
        functools.partial(_mla_attn_kernel, tq=tq, scale=scale),
        grid=(B, H // HPS, nq),
        in_specs=[qspec, qspec, kspec, pl.BlockSpec((S, LANE), lambda b, h, i: (b, 0)), kspec],
        out_specs=qspec,
        out_shape=jax.ShapeDtypeStruct((M, H * MLA_V), BF16),
        scratch_shapes=[pltpu.VMEM((HPS, tq, LANE), F32), pltpu.VMEM((HPS, tq, 2 * LANE), F32)],
        compiler_params=_cparams(("parallel", "parallel", "arbitrary")),
        name="mla_attention",
    )(qn, qr, kn, kr, v)


def _dsa_index_kernel(iq_ref, ik_ref, iw_ref, tri_ref, mb_ref, sc_ref, *, tq, tk, S, topk, iters):
    t0 = pl.program_id(1) * tq
    nk = S // tk
    nch = (t0 + tq - 1) // tk + 1
    hw = IDX_CHUNK
    nb = tk // LANE
    iq = iq_ref[...]
    iw = iw_ref[...].astype(F32)
    lane = lax.broadcasted_iota(jnp.int32, (tq, LANE), 1)
    qs = []
    for h in range(IDX_HEADS):
        blk = iq[:, (h // 2) * LANE:(h // 2 + 1) * LANE]
        keep = (lane >= IDX_DIM) if h % 2 else (lane < IDX_DIM)
        qs.append(jnp.where(keep, blk, jnp.zeros_like(blk)))
    q8 = jnp.concatenate(qs, axis=0)
    wcol = [iw[:, h:h + 1] for h in range(IDX_HEADS)]
    row = lax.broadcasted_iota(jnp.int32, (tq, hw), 0) + t0
    col0 = lax.broadcasted_iota(jnp.int32, (tq, hw), 1)

    def lane_fold(x, op):
        out = x[:, 0:LANE]
        for b in range(1, x.shape[1] // LANE):
            out = op(out, x[:, b * LANE:(b + 1) * LANE])
        return out

    def score_chunk(c, carry):
        mn, mx = carry
        for half in range(tk // hw):
            k0 = pl.multiple_of(c * tk + half * hw, hw)
            res = _dot_nt(q8, ik_ref[pl.ds(k0, hw), :])
            sc = wcol[0] * jnp.maximum(res[0:tq], 0.0)
            for h in range(1, IDX_HEADS):
                sc = sc + wcol[h] * jnp.maximum(res[h * tq:(h + 1) * tq], 0.0)
            causal = (col0 + k0) <= row
            mn = jnp.minimum(mn, lane_fold(jnp.where(causal, sc, -NEG), jnp.minimum))
            sc = jnp.where(causal, sc, NEG)
            mx = jnp.maximum(mx, lane_fold(sc, jnp.maximum))
            sc_ref[c, :, half * hw:(half + 1) * hw] = sc
        return mn, mx

    mn, mx = lax.fori_loop(0, nch, score_chunk,
                           (jnp.full((tq, LANE), -NEG, F32), jnp.full((tq, LANE), NEG, F32)))
    n_causal = (lax.broadcasted_iota(jnp.int32, (tq, 1), 0) + (t0 + 1)).astype(F32)
    kf = jnp.minimum(float(topk), n_causal)

    rs = IDX_ROWS

    def blocks(j):
        for r in range(tq // rs):
            for b in range(nb):
                rows = slice(r * rs, (r + 1) * rs)
                yield rows, (j, rows, slice(b * LANE, (b + 1) * LANE)), j * tk + b * LANE

    def count(pred):
        def body(c, acc):
            parts = [acc[r * rs:(r + 1) * rs] for r in range(tq // rs)]
            for rows, idx, off in blocks(c):
                r = rows.start // rs
                parts[r] = parts[r] + jnp.where(pred(rows, sc_ref[idx], off), 1.0, 0.0)
            return jnp.concatenate(parts, axis=0)

        acc = lax.fori_loop(0, nch, body, jnp.zeros((tq, LANE), F32))
        return jnp.sum(acc, axis=-1, keepdims=True)

    def bcast(x):
        strips = [jnp.broadcast_to(x[r * rs:(r + 1) * rs], (rs, LANE)) for r in range(tq // rs)]
        return lambda rows: strips[rows.start // rs]

    lo0 = jnp.min(mn, axis=-1, keepdims=True)
    mx = jnp.max(mx, axis=-1, keepdims=True)
    hi0 = mx + (jnp.abs(mx) * 1e-6 + 1.0)

    def bisect(c):
        lo, hi, n_lo, n_hi = c
        mid = 0.5 * (lo + hi)
        mid_b = bcast(mid)
        n_mid = count(lambda rows, blk, off: blk >= mid_b(rows))
        ge = n_mid >= kf
        return (jnp.where(ge, mid, lo), jnp.where(ge, hi, mid),
                jnp.where(ge, n_mid, n_lo), jnp.where(ge, n_hi, n_mid))

    def search(state, limit, settled):
        def cond(c):
            return jnp.logical_and(c[0] < limit, jnp.max((c[3] - kf) * (1.0 - settled)) > 0.0)

        def body(c):
            return (c[0] + 2,) + bisect(bisect(c[1:]))

        return lax.while_loop(cond, body, state)

    def band_is_single_value(lo, hi):
        lo_b, hi_b = bcast(lo), bcast(hi)

        def body(c, carry):
            bmin = [carry[0][r * rs:(r + 1) * rs] for r in range(tq // rs)]
            bmax = [carry[1][r * rs:(r + 1) * rs] for r in range(tq // rs)]
            for rows, idx, off in blocks(c):
                r = rows.start // rs
                blk = sc_ref[idx]
                inside = (blk >= lo_b(rows)) & (blk < hi_b(rows))
                bmin[r] = jnp.minimum(bmin[r], jnp.where(inside, blk, -NEG))
                bmax[r] = jnp.maximum(bmax[r], jnp.where(inside, blk, NEG))
            return jnp.concatenate(bmin, axis=0), jnp.concatenate(bmax, axis=0)

        bmin, bmax = lax.fori_loop(0, nch, body,
                                   (jnp.full((tq, LANE), -NEG, F32), jnp.full((tq, LANE), NEG, F32)))
        same = jnp.min(bmin, axis=-1, keepdims=True) == jnp.max(bmax, axis=-1, keepdims=True)
        return jnp.where(same, 1.0, 0.0)

    zero = jnp.zeros((tq, 1), F32)
    state = search((jnp.int32(0), lo0, hi0, n_causal, zero), IDX_TIE_CHECK, zero)
    unresolved = jnp.max(state[3] - kf) > 0.0
    settled = lax.cond(unresolved, lambda: band_is_single_value(state[1], state[2]), lambda: zero)
    _, lo, hi, n_lo, n_hi = search(state, iters, settled)
    tied = jnp.max(n_lo - kf) > 0.0

    def write(sel):
        def body(j, c):
            for rows, idx, off in blocks(j):
                mb_ref[idx] = jnp.where(sel(rows, sc_ref[idx], off), 0.0, NEG).astype(mb_ref.dtype)
            return c

        lax.fori_loop(0, nch, body, 0)

    def fill(j, c):
        mb_ref[j] = jnp.full((tq, tk), NEG, mb_ref.dtype)
        return c

    lax.fori_loop(nch, nk, fill, 0)
    lo_b = bcast(lo)

    @pl.when(jnp.logical_not(tied))
    def _():
        write(lambda rows, blk, off: blk >= lo_b(rows))

    @pl.when(tied)
    def _():
        need = kf - n_hi
        tri = tri_ref[...]

        def body(j, base):
            sc = sc_ref[j]
            above = sc >= hi
            band = (sc >= lo) & jnp.logical_not(above)
            prefix = _dot(jnp.where(band, 1.0, 0.0).astype(BF16), tri) + base
            keep = above | (band & (prefix <= need))
            mb_ref[j] = jnp.where(keep, 0.0, NEG).astype(mb_ref.dtype)
            return prefix[:, tk - 1:tk]

        lax.fori_loop(0, nch, body, zero)


def dsa_index_mask(proj, *, B, S, tqi, tk, topk):
    nq = S // tqi
    nk = S // tk
    assert tk % tqi == 0 and tk % IDX_CHUNK == 0
    tri = jnp.asarray(np.triu(np.ones((tk, tk), np.float32)), BF16)
    return pl.pallas_call(
        functools.partial(_dsa_index_kernel, tq=tqi, tk=tk, S=S, topk=topk, iters=BISECT_ITERS),
        grid=(B, nq),
        in_specs=[pl.BlockSpec((tqi, 512), lambda b, i: (b * nq + i, OFF_IDX_Q // 512)),
                  pl.BlockSpec((S, LANE), lambda b, i: (b, OFF_IDX_K // LANE)),
                  pl.BlockSpec((tqi, LANE), lambda b, i: (b * nq + i, OFF_IDX_W // LANE)),
                  pl.BlockSpec((tk, tk), lambda b, i: (0, 0))],
        out_specs=pl.BlockSpec((None, nk, tqi, tk), lambda b, i: (b, 0, i, 0)),
        out_shape=jax.ShapeDtypeStruct((B, nk, S, tk), BF16),
        scratch_shapes=[pltpu.VMEM((nk, tqi, tk), F32)],
        compiler_params=_cparams(("parallel", "parallel")),
        name="dsa_index_mask",
    )(proj, proj, proj, tri)


def _dsa_attn_kernel(q_ref, k_ref, v_ref, mb_ref, bias_ref, o_ref, m_ref, acc_ref, *, tq, scale):
    qi = pl.program_id(2)
    hs = range(HPS)
    cols = lambda h: slice(h * LANE, (h + 1) * LANE)
    q = [q_ref[:, cols(h)] for h in hs]
    for h in hs:
        _softmax_init(m_ref.at[h], acc_ref.at[h])

    def step(j, bias_idx):
        sl = pl.ds(pl.multiple_of(j * tq, tq), tq)
        mb = mb_ref[j].astype(F32)
        s = [_dot_nt(q[h], k_ref[sl, cols(h)]) + mb for h in hs]
        for h in hs:
            sh = s[h] if bias_idx is None else s[h] + bias_ref[h, bias_idx]
            _softmax_step(sh, v_ref[sl, cols(h)], m_ref.at[h], acc_ref.at[h], scale)

    def body(j, c):
        step(j, None)
        return c

    lax.fori_loop(0, jnp.maximum(qi - 1, 0), body, 0)

    @pl.when(qi >= 1)
    def _():
        step(qi - 1, 1)

    step(qi, 0)
    for h in hs:
        o_ref[:, cols(h)] = _softmax_finish(acc_ref.at[h]).astype(o_ref.dtype)


def dsa_attention(proj, maskb, btiles, *, B, S, tq):
    M = B * S
    nq = S // tq
    H = DSA_HEADS
    scale = DSA_HEAD_DIM ** -0.5
    w = HPS * LANE
    return pl.pallas_call(
        functools.partial(_dsa_attn_kernel, tq=tq, scale=scale),
        grid=(B, H // HPS, nq),
        in_specs=[pl.BlockSpec((tq, w), lambda b, h, i: (b * nq + i, OFF_DSA_Q // w + h)),
                  pl.BlockSpec((S, w), lambda b, h, i: (b, OFF_DSA_K // w + h)),
                  pl.BlockSpec((S, w), lambda b, h, i: (b, OFF_DSA_V // w + h)),
                  pl.BlockSpec((None, nq, tq, tq), lambda b, h, i: (b, 0, i, 0)),
                  pl.BlockSpec((HPS, 2, tq, tq), lambda b, h, i: (h, 0, 0, 0))],
        out_specs=pl.BlockSpec((tq, w), lambda b, h, i: (b * nq + i, h)),
        out_shape=jax.ShapeDtypeStruct((M, H * DSA_HEAD_DIM), BF16),
        scratch_shapes=[pltpu.VMEM((HPS, tq, LANE), F32), pltpu.VMEM((HPS, tq, 2 * LANE), F32)],
        compiler_params=_cparams(("parallel", "parallel", "arbitrary")),
        name="dsa_attention",
    )(proj, proj, proj, maskb, btiles)


def _nsa_compress_kernel(x_ref, pos_ref, w1_ref, w2_ref, o_ref, *, ncp):
    dk = NSA_HEAD_DIM
    posw = _dot(pos_ref[...], w1_ref[...])[0:1, :]
    for g in range(NSA_GROUPS):
        lo = jnp.zeros((ncp, CMP_HIDDEN), F32)
        hi = jnp.zeros((ncp, CMP_HIDDEN), F32)
        for l in range(CMP_STRIDE):
            xs = x_ref[:, (l * NSA_GROUPS + g) * dk:(l * NSA_GROUPS + g + 1) * dk]
            lo = lo + _dot(xs, w1_ref[l * dk:(l + 1) * dk, :])
            hi = hi + _dot(xs, w1_ref[(CMP_STRIDE + l) * dk:(CMP_STRIDE + l + 1) * dk, :])
        hid = lo + pltpu.roll(hi, ncp - 1, 0) + posw
        o_ref[g] = _dot(jax.nn.gelu(hid).astype(BF16), w2_ref[...]).astype(o_ref.dtype)


def nsa_compress(xkv, posflat, w1, w2, *, B, S):
    ncp = S // CMP_STRIDE
    G, dk = NSA_GROUPS, NSA_HEAD_DIM
    width = CMP_STRIDE * G * dk
    return pl.pallas_call(
        functools.partial(_nsa_compress_kernel, ncp=ncp),
        grid=(2, B),
        in_specs=[pl.BlockSpec((None, ncp, width), lambda a, b: (a, b, 0)),
                  pl.BlockSpec((None, 8, CMP_LEN * dk), lambda a, b: (a, 0, 0)),
                  pl.BlockSpec((None, CMP_LEN * dk, CMP_HIDDEN), lambda a, b: (a, 0, 0)),
                  pl.BlockSpec((None, CMP_HIDDEN, dk), lambda a, b: (a, 0, 0))],
        out_specs=pl.BlockSpec((None, None, G, ncp, dk), lambda a, b: (a, b, 0, 0, 0)),
        out_shape=jax.ShapeDtypeStruct((2, B, G, ncp, dk), BF16),
        compiler_params=_cparams(("parallel", "parallel")),
        name="nsa_compress",
    )(xkv, posflat, w1, w2)


def _nsa_cmp_kernel(q_ref, kc_ref, vc_ref, ov_ref, oc_ref, sel_ref, *, tq, ncp, ns, n_sel, scale):
    t0 = pl.program_id(2) * tq
    dk = NSA_HEAD_DIM
    kc = kc_ref[...]
    vc = vc_ref[...]
    trow = lax.broadcasted_iota(jnp.int32, (tq, ncp), 0) + t0
    ccol = lax.broadcasted_iota(jnp.int32, (tq, ncp), 1)
    vis = (ccol * CMP_STRIDE + (CMP_LEN - 1)) <= trow
    psum = jnp.zeros((tq, ncp), F32)
    for h in range(NSA_HG):
        s = jnp.where(vis, _dot_nt(q_ref[:, h * dk:(h + 1) * dk], kc) * scale, NEG)
        m = jnp.max(s, axis=-1, keepdims=True)
        p = jnp.where(vis, jnp.exp(s - m), 0.0)
        d = jnp.sum(p, axis=-1, keepdims=True)
        p = p / jnp.where(d > 0, d, 1.0)
        oc_ref[:, h * dk:(h + 1) * dk] = _dot(p.astype(BF16), vc).astype(oc_ref.dtype)
        psum = psum + p
    p_hi = psum.astype(BF16)
    p_lo = (psum - p_hi.astype(F32)).astype(BF16)
    ov = ov_ref[...]
    imp = _dot_nt(ov, p_hi) + _dot_nt(ov, p_lo)
    imp = imp[:ns]
    jrow = lax.broadcasted_iota(jnp.int32, (ns, tq), 0)
    tcol = lax.broadcasted_iota(jnp.int32, (ns, tq), 1) + t0
    blk_t = tcol // SEL_BLOCK
    forced = (jrow == 0) | (jrow == blk_t) | (jrow == blk_t - 1)
    val = jnp.where(forced, -NEG, imp)
    val = jnp.where(jrow * SEL_BLOCK > tcol, NEG, val)
    rank = jnp.zeros((ns, tq), F32)
    for j in range(ns):
        other = val[j:j + 1, :]
        ahead = (other > val) | ((other == val) & (jrow > j))
        rank = rank + jnp.where(ahead, 1.0, 0.0)
    selneg = jnp.where(rank < float(n_sel), 0.0, NEG)
    if ns < LANE:
        selneg = jnp.concatenate([selneg, jnp.zeros((LANE - ns, tq), F32)], axis=0)
    sel_ref[...] = selneg.T.astype(sel_ref.dtype)


def nsa_cmp_attention(proj, kvc, ovt, *, B, S, tq):
    M = B * S
    nq = S // tq
    G, dk = NSA_GROUPS, NSA_HEAD_DIM
    ncp = S // CMP_STRIDE
    ns = S // SEL_BLOCK
    assert ns <= LANE
    n_sel = min(SEL_COUNT, ns)
    gw = NSA_HG * dk
    return pl.pallas_call(
        functools.partial(_nsa_cmp_kernel, tq=tq, ncp=ncp, ns=ns, n_sel=n_sel, scale=dk ** -0.5),
        grid=(B, G, nq),
        in_specs=[pl.BlockSpec((tq, gw), lambda b, g, i: (b * nq + i, OFF_NSA_Q // gw + g)),
                  pl.BlockSpec((None, None, None, ncp, dk), lambda b, g, i: (0, b, g, 0, 0)),
                  pl.BlockSpec((None, None, None, ncp, dk), lambda b, g, i: (1, b, g, 0, 0)),
                  pl.BlockSpec((LANE, ncp), lambda b, g, i: (0, 0))],
        out_specs=[pl.BlockSpec((tq, gw), lambda b, g, i: (b * nq + i, g)),
                   pl.BlockSpec((None, None, tq, LANE), lambda b, g, i: (b, g, i, 0))],
        out_shape=[jax.ShapeDtypeStruct((M, G * gw), BF16),
                   jax.ShapeDtypeStruct((B, G, S, LANE), BF16)],
        compiler_params=_cparams(("parallel", "parallel", "parallel")),
        name="nsa_cmp_attention",
    )(proj, kvc, kvc, ovt)


def _nsa_main_kernel(q_ref, sel_ref, ks_ref, vs_ref, kw_ref, vw_ref, ind_ref, bias_ref, gate_ref, oc_ref,
                     o_ref, m_ref, acc_ref, *, tq, nback, scale):
    g = pl.program_id(1)
    qi = pl.program_id(2)
    dk = NSA_HEAD_DIM
    HG = NSA_HG
    ns = NSA_STREAMS
    hp = HG // ns
    st = range(ns)
    stack = lambda f: [jnp.concatenate([f(a * hp + i) for i in range(hp)], axis=0) for a in st]
    q = stack(lambda h: q_ref[:, h * dk:(h + 1) * dk])
    sel = jnp.concatenate([sel_ref[...]] * hp, axis=0)
    qa = [jnp.concatenate([q[a], sel], axis=1) for a in st]
    row, col = _tri(tq, tq)
    row = jnp.concatenate([row] * hp, axis=0)
    col = jnp.concatenate([col] * hp, axis=0)
    bias0 = stack(lambda h: bias_ref[h, 0])
    bias1 = stack(lambda h: bias_ref[h, 1])
    gate = jax.nn.sigmoid(gate_ref[...].astype(F32))

    def init():
        for a in st:
            _softmax_init(m_ref.at[a], acc_ref.at[a])

    def steps(s, v, fix):
        for a in st:
            _softmax_step(fix(a, s[a]), v, m_ref.at[a], acc_ref.at[a], scale)

    init()

    def sel_scores(j, width=tq):
        sl = pl.ds(pl.multiple_of(j * tq, tq), width)
        ka = jnp.concatenate([ks_ref[sl, :], ind_ref[sl, :]], axis=1)
        return [_dot_nt(qa[a], ka) for a in st], vs_ref[sl, :]

    n_far = jnp.maximum(qi - 1, 0)

    def sel_body(p, c):
        s, v = sel_scores(2 * p, 2 * tq)
        steps(s, v, lambda a, x: x)
        return c

    lax.fori_loop(0, n_far // 2, sel_body, 0)

    @pl.when(n_far % 2 == 1)
    def _():
        s, v = sel_scores(n_far - 1)
        steps(s, v, lambda a, x: x)

    @pl.when(qi >= 1)
    def _():
        s, v = sel_scores(qi - 1)
        steps(s, v, lambda a, x: x + bias1[a])

    s, v = sel_scores(qi)
    steps(s, v, lambda a, x: jnp.where(row >= col, x + bias0[a], NEG))
    o_s = [_softmax_finish(acc_ref.at[a]) for a in st]

    init()

    def win_scores(j):
        sl = pl.ds(pl.multiple_of(j * tq, tq), tq)
        kw = kw_ref[sl, :]
        return [_dot_nt(q[a], kw) for a in st], vw_ref[sl, :]

    @pl.when(qi >= nback)
    def _():
        s, v = win_scores(qi - nback)
        steps(s, v, lambda a, x: jnp.where(row < col, x + bias1[a] if nback == 1 else x, NEG))

    for back in range(nback - 1, 0, -1):
        @pl.when(qi >= back)
        def _(back=back):
            s, v = win_scores(qi - back)
            steps(s, v, lambda a, x: x + bias1[a] if back == 1 else x)

    s, v = win_scores(qi)
    steps(s, v, lambda a, x: jnp.where(row >= col, x + bias0[a], NEG))
    o_w = [_softmax_finish(acc_ref.at[a]) for a in st]

    lane = lax.broadcasted_iota(jnp.int32, gate.shape, 1)
    for h in range(HG):
        c = g * HG + h
        a, sl = h // hp, slice((h % hp) * tq, (h % hp + 1) * tq)
        gc = [jnp.sum(jnp.where(lane == br * NSA_HEADS + c, gate, 0.0), axis=-1, keepdims=True)
              for br in range(3)]
        o = gc[0] * oc_ref[:, h * dk:(h + 1) * dk].astype(F32) + gc[1] * o_s[a][sl] + gc[2] * o_w[a][sl]
        o_ref[:, h * dk:(h + 1) * dk] = o.astype(o_ref.dtype)


def nsa_main(proj, selneg, ind, btiles, o_c, *, B, S, tq):
    M = B * S
    nq = S // tq
    G, dk, HG = NSA_GROUPS, NSA_HEAD_DIM, NSA_HG
    gw = HG * dk
    assert WINDOW % tq == 0
    nback = WINDOW // tq
    kv = lambda n: pl.BlockSpec((S, dk), lambda b, g, i, n=n: (b, (OFF_NSA_KV + n * G * dk) // dk + g))
    return pl.pallas_call(
        functools.partial(_nsa_main_kernel, tq=tq, nback=nback, scale=dk ** -0.5),
        grid=(B, G, nq),
        in_specs=[pl.BlockSpec((tq, gw), lambda b, g, i: (b * nq + i, OFF_NSA_Q // gw + g)),
                  pl.BlockSpec((None, None, tq, LANE), lambda b, g, i: (b, g, i, 0)),
                  kv(2), kv(3), kv(4), kv(5),
                  pl.BlockSpec((S, LANE), lambda b, g, i: (0, 0)),
                  pl.BlockSpec((HG, 2, tq, tq), lambda b, g, i: (g, 0, 0, 0)),
                  pl.BlockSpec((tq, LANE), lambda b, g, i: (b * nq + i, OFF_NSA_GATE // LANE)),
                  pl.BlockSpec((tq, gw), lambda b, g, i: (b * nq + i, g))],
        out_specs=pl.BlockSpec((tq, gw), lambda b, g, i: (b * nq + i, g)),
        out_shape=jax.ShapeDtypeStruct((M, G * gw), BF16),
        scratch_shapes=[pltpu.VMEM((NSA_STREAMS, HG // NSA_STREAMS * tq, LANE), F32),
                        pltpu.VMEM((NSA_STREAMS, HG // NSA_STREAMS * tq, 2 * LANE), F32)],
        compiler_params=_cparams(("parallel", "parallel", "arbitrary")),
        name="nsa_main",
    )(proj, selneg, proj, proj, proj, proj, ind, btiles, proj, o_c)


def _merge_kernel(x_ref, oa_ref, ob_ref, oc_ref, g0_ref, g1_ref, g2_ref, wb_ref, wo_ref, gain_ref, o_ref):
    merged = None
    for o_r, g_r, br in ((oa_ref, g0_ref, 0), (ob_ref, g1_ref, 1), (oc_ref, g2_ref, 2)):
        t = jax.nn.sigmoid(g_r[...].astype(F32)) * _dot(o_r[...], wb_ref[br])
        merged = t if merged is None else merged + t
    y = _dot(merged.astype(BF16), wo_ref[...])
    o_ref[...] = x_ref[...] + _rms(y, gain_ref[...])


def merge_branches(x, o_a, o_b, o_c, proj, wb, wo, gain, *, tm):
    M, D = x.shape
    row = pl.BlockSpec((tm, D), lambda i: (i, 0))
    gate = lambda br: pl.BlockSpec((tm, D), lambda i, br=br: (i, OFF_BGATE // D + br))
    return pl.pallas_call(
        _merge_kernel,
        grid=(M // tm,),
        in_specs=[row, row, row, row, gate(0), gate(1), gate(2),
                  pl.BlockSpec(wb.shape, lambda i: (0, 0, 0)),
                  pl.BlockSpec(wo.shape, lambda i: (0, 0)),
                  pl.BlockSpec((1, D), lambda i: (0, 0))],
        out_specs=row,
        out_shape=jax.ShapeDtypeStruct((M, D), F32),
        compiler_params=_cparams(("parallel",)),
        name="merge_branches",
    )(x, o_a, o_b, o_c, proj, proj, proj, wb, wo, gain)


def _cross_kernel(x_ref, kv_ref, wq_ref, wo_ref, gpre_ref, gpost_ref, o_ref, *, scale):
    x = x_ref[...]
    h = _rms(x, gpre_ref[...]).astype(BF16)
    q = _dot(h, wq_ref[...]).astype(BF16)
    dh = CROSS_HEAD_DIM
    outs = []
    for hd in range(CROSS_HEADS):
        k = kv_ref[:, hd * dh:(hd + 1) * dh]
        v = kv_ref[:, D_MODEL + hd * dh:D_MODEL + (hd + 1) * dh]
        s = _dot_nt(q[:, hd * dh:(hd + 1) * dh], k) * scale
        p = jnp.exp(s - jnp.max(s, axis=-1, keepdims=True))
        p = p / jnp.sum(p, axis=-1, keepdims=True)
        outs.append(_dot(p.astype(BF16), v).astype(BF16))
    y = _dot(jnp.concatenate(outs, axis=1), wo_ref[...])
    o_ref[...] = x + _rms(y, gpost_ref[...])


def cross_attention(x, kv, wq, wo, gpre, gpost, *, S, tm):
    M, D = x.shape
    mlen = kv.shape[0] // (M // S)
    per_b = S // tm
    return pl.pallas_call(
        functools.partial(_cross_kernel, scale=CROSS_HEAD_DIM ** -0.5),
        grid=(M // tm,),
        in_specs=[pl.BlockSpec((tm, D), lambda i: (i, 0)),
                  pl.BlockSpec((mlen, 2 * D), lambda i: (i // per_b, 0)),
                  pl.BlockSpec(wq.shape, lambda i: (0, 0)),
                  pl.BlockSpec(wo.shape, lambda i: (0, 0)),
                  pl.BlockSpec((1, D), lambda i: (0, 0)),
                  pl.BlockSpec((1, D), lambda i: (0, 0))],
        out_specs=pl.BlockSpec((tm, D), lambda i: (i, 0)),
        out_shape=jax.ShapeDtypeStruct((M, D), F32),
        compiler_params=_cparams(("parallel",)),
        name="cross_attention",
    )(x, kv, wq, wo, gpre, gpost)


def _mlp_kernel(x_ref, w1_ref, w2_ref, gpre_ref, gpost_ref, o_ref, h_ref, acc_ref):
    j = pl.program_id(1)

    @pl.when(j == 0)
    def _():
        h_ref[...] = _rms(x_ref[...], gpre_ref[...]).astype(BF16)
        acc_ref[...] = jnp.zeros(acc_ref.shape, F32)

    a = jnp.maximum(_dot(h_ref[...], w1_ref[...]), 0.0)
    acc_ref[...] += _dot((a * a).astype(BF16), w2_ref[...])

    @pl.when(j == pl.num_programs(1) - 1)
    def _():
        o_ref[...] = x_ref[...] + _rms(acc_ref[...], gpost_ref[...])


def mlp(x, w1, w2, gpre, gpost, *, tm, tf):
    M, D = x.shape
    FF = w1.shape[1]
    return pl.pallas_call(
        _mlp_kernel,
        grid=(M // tm, FF // tf),
        in_specs=[pl.BlockSpec((tm, D), lambda i, j: (i, 0)),
                  pl.BlockSpec((D, tf), lambda i, j: (0, j)),
                  pl.BlockSpec((tf, D), lambda i, j: (j, 0)),
                  pl.BlockSpec((1, D), lambda i, j: (0, 0)),
                  pl.BlockSpec((1, D), lambda i, j: (0, 0))],
        out_specs=pl.BlockSpec((tm, D), lambda i, j: (i, 0)),
        out_shape=jax.ShapeDtypeStruct((M, D), F32),
        scratch_shapes=[pltpu.VMEM((tm, D), BF16), pltpu.VMEM((tm, D), F32)],
        compiler_params=_cparams(("parallel", "arbitrary")),
        name="mlp",
    )(x, w1, w2, gpre, gpost)


def _rot_half_cols(w):
    half = w.shape[-1] // 2
    return jnp.concatenate([-w[..., half:], w[..., :half]], axis=-1)


def _pack_w_in(w):
    widths = (MLA_Q_LORA, MLA_KV_LORA, MLA_ROPE, 1024, 1024, 1024, IDX_HEADS * IDX_DIM, IDX_DIM, IDX_HEADS,
              1024, 256, 256, 256, 256, 256, 256, 3 * NSA_HEADS, N_BRANCH * D_MODEL)
    offs = np.concatenate([[0], np.cumsum(widths)])
    seg = [w[:, offs[i]:offs[i + 1]] for i in range(len(widths))]
    (q_lat, kv_lat, k_rope, dsa_q, dsa_k, dsa_v, idx_q, idx_k, idx_w,
     nsa_q, kc, vc, ks, vs, kw, vw, nsa_gate, bgate) = seg
    z = lambda n: jnp.zeros((w.shape[0], n), w.dtype)
    cols = [dsa_q, dsa_k, dsa_v, nsa_q, bgate, idx_q, kc, vc, ks, vs, kw, vw,
            q_lat, kv_lat, k_rope, z(LANE - MLA_ROPE), _rot_half_cols(k_rope), z(LANE - MLA_ROPE), z(LANE),
            idx_k, idx_k, idx_w, z(LANE - IDX_HEADS), nsa_gate, z(LANE - 3 * NSA_HEADS)]
    out = jnp.concatenate(cols, axis=1)
    out = jnp.concatenate([out, z(N_PACK - out.shape[1])], axis=1)
    return out.astype(BF16)


def _pack_mla_weights(w_uq, w_ukv):
    H = MLA_HEADS
    uq = w_uq.reshape(MLA_Q_LORA, H, MLA_NOPE + MLA_ROPE)
    nope = uq[:, :, :MLA_NOPE].reshape(MLA_Q_LORA, H * MLA_NOPE)
    rope = uq[:, :, MLA_NOPE:]
    pad = jnp.zeros((MLA_Q_LORA, H, LANE - MLA_ROPE), w_uq.dtype)
    wqr = jnp.concatenate([rope, pad], axis=-1).reshape(MLA_Q_LORA, H * LANE)
    wqrr = jnp.concatenate([_rot_half_cols(rope), pad], axis=-1).reshape(MLA_Q_LORA, H * LANE)
    ukv = w_ukv.reshape(MLA_KV_LORA, H, MLA_NOPE + MLA_V)
    wkv = jnp.concatenate([ukv[:, :, :MLA_NOPE].reshape(MLA_KV_LORA, H * MLA_NOPE),
                           ukv[:, :, MLA_NOPE:].reshape(MLA_KV_LORA, H * MLA_V)], axis=1)
    return nope.astype(BF16), wqr.astype(BF16), wqrr.astype(BF16), wkv.astype(BF16)


def _constants(S):
    half = MLA_ROPE // 2
    inv_freq = ROPE_BASE ** (-np.arange(0, MLA_ROPE, 2, dtype=np.float32) / MLA_ROPE)
    freq = np.zeros((1, LANE), np.float32)
    freq[0, :half] = inv_freq
    freq[0, half:2 * half] = inv_freq
    ncp = S // CMP_STRIDE
    ns = S // SEL_BLOCK
    c_start = np.arange(ncp) * CMP_STRIDE
    j_start = np.arange(ns) * SEL_BLOCK
    overlap = ((c_start[None, :] < j_start[:, None] + SEL_BLOCK) &
               (c_start[None, :] + CMP_LEN > j_start[:, None])).astype(np.float32)
    overlap[:, ncp - 1] = 0.0
    ovt = np.zeros((LANE, ncp), np.float32)
    ovt[:ns] = overlap
    ind = np.zeros((S, LANE), np.float32)
    ind[np.arange(S), np.arange(S) // SEL_BLOCK] = 1.0
    return jnp.asarray(freq), jnp.asarray(ovt, BF16), jnp.asarray(ind, BF16)


def kernel(x, mem, positions, rel_bias, norm_gains, w_in, mla_q_norm, mla_kv_norm, mla_w_uq, mla_w_ukv,
           nsa_cmp_pos, nsa_cmp_w1, nsa_cmp_w2, w_branch, w_out, cross_wq, cross_wkv, cross_wo, mlp_w1, mlp_w2):
    B, S, D = x.shape
    M = B * S
    depth = w_in.shape[0]
    tq = TQ if S % TQ == 0 else TQ_NSA
    tqn = TQ_NSA
    assert S % tq == 0 and S % tqn == 0 and D == D_MODEL
    topk = min(DSA_TOPK_MAX, S // 4)
    freq, ovt, ind = _constants(S)
    btiles = bias_tiles(rel_bias, tq, 0, DSA_HEADS)
    btiles_n = bias_tiles(rel_bias, tqn, DSA_HEADS, NSA_HEADS)
    pos = positions.reshape(M, 1)
    xf = x.reshape(M, D)
    memf = mem.reshape(B * mem.shape[1], D)
    tm_big = 1024 if M % 1024 == 0 else tq
    tm = 512 if M % 512 == 0 else tq
    ncp = S // CMP_STRIDE
    row = lambda v: v.reshape(1, -1)

    for l in range(depth):
        g = norm_gains[l]
        proj = norm_matmul(xf, row(g[0]), _pack_w_in(w_in[l]), tm=tm_big, tn=TN_IN)

        wqn, wqr, wqrr, wkv = _pack_mla_weights(mla_w_uq[l], mla_w_ukv[l])
        qn, qr, kn, vv, kr = mla_prep(proj, pos, freq, row(mla_q_norm[l]), row(mla_kv_norm[l]),
                                      wqn, wqr, wqrr, wkv, tm=tm)
        o_a = mla_attention(qn, qr, kn, kr, vv, B=B, S=S, tq=tq)

        maskb = dsa_index_mask(proj, B=B, S=S, tqi=TQ_IDX, tk=tq, topk=topk)
        o_b = dsa_attention(proj, maskb, btiles, B=B, S=S, tq=tq)

        kvc_in = jnp.stack([
            proj[:, OFF_NSA_KV + a * 256:OFF_NSA_KV + (a + 1) * 256].reshape(B * ncp, CMP_STRIDE * 256)
            for a in range(2)])
        posflat = jnp.broadcast_to(nsa_cmp_pos[l].reshape(2, 1, CMP_LEN * NSA_HEAD_DIM),
                                   (2, 8, CMP_LEN * NSA_HEAD_DIM)).astype(BF16)
        kvc = nsa_compress(kvc_in, posflat, nsa_cmp_w1[l].astype(BF16), nsa_cmp_w2[l].astype(BF16), B=B, S=S)
        o_cmp, selneg = nsa_cmp_attention(proj, kvc, ovt, B=B, S=S, tq=tqn)
        o_c = nsa_main(proj, selneg, ind, btiles_n, o_cmp, B=B, S=S, tq=tqn)

        xf = merge_branches(xf, o_a, o_b, o_c, proj, w_branch[l].astype(BF16), w_out[l].astype(BF16),
                            row(g[1]), tm=tm)

        mkv = norm_matmul(memf, row(g[3]), cross_wkv[l].astype(BF16), tm=memf.shape[0] // B, tn=1024)
        xf = cross_attention(xf, mkv, cross_wq[l].astype(BF16), cross_wo[l].astype(BF16),
                             row(g[2]), row(g[4]), S=S, tm=tm)

        xf = mlp(xf, mlp_w1[l].astype(BF16), mlp_w2[l].astype(BF16), row(g[5]), row(g[6]), tm=tm, tf=1024)

    return xf.reshape(B, S, D)
```

```python
import functools
import math

import numpy as np
import jax
import jax.numpy as jnp
from jax import lax
from jax.experimental import pallas as pl
from jax.experimental.pallas import tpu as pltpu

F32 = jnp.float32
BF16 = jnp.bfloat16

D_MODEL = 1024
EPS = 1e-6
MLA_HEADS = 8
MLA_Q_LORA = 384
MLA_KV_LORA = 256
MLA_NOPE = 128
MLA_ROPE = 64
MLA_V = 128
ROPE_BASE = 10000.0
DSA_HEADS = 8
DSA_HEAD_DIM = 128
IDX_HEADS = 8
IDX_DIM = 64
DSA_TOPK_MAX = 256
NSA_HEADS = 8
NSA_GROUPS = 2
NSA_HG = NSA_HEADS // NSA_GROUPS
NSA_HEAD_DIM = 128
CMP_STRIDE = 16
CMP_LEN = 2 * CMP_STRIDE
CMP_HIDDEN = 256
SEL_BLOCK = 64
SEL_COUNT = 16
WINDOW = 512
REL_BUCKETS = 32
REL_MAX_DIST = 128
REL_HEADS = DSA_HEADS + NSA_HEADS
CROSS_HEADS = 4
CROSS_HEAD_DIM = D_MODEL // CROSS_HEADS
D_FF = 4 * D_MODEL
N_BRANCH = 3

LANE = 128
NEG = -1e30
VMEM_LIMIT = 48 * 1024 * 1024

TQ = 512
TQ_NSA = 256
HPS = 2
NSA_STREAMS = 2
BISECT_ITERS = 32
IDX_TIE_CHECK = 18
TQ_IDX = 256
IDX_CHUNK = 256
IDX_ROWS = 64

OFF_DSA_Q = 0
OFF_DSA_K = 1024
OFF_DSA_V = 2048
OFF_NSA_Q = 3072
OFF_BGATE = 4096
OFF_IDX_Q = 7168
OFF_NSA_KV = 7680
OFF_MLA = 9216
OFF_IDX_K = 10240
OFF_IDX_W = 10368
OFF_NSA_GATE = 10496
N_PACK = 10752
TN_IN = 1536


def _cparams(sem):
    return pltpu.CompilerParams(dimension_semantics=sem, vmem_limit_bytes=VMEM_LIMIT)


def _rms(x, g):
    return x * lax.rsqrt(jnp.mean(x * x, axis=-1, keepdims=True) + EPS) * g


def _dot(a, b):
    return jnp.dot(a, b, preferred_element_type=F32)


def _dot_nt(a, b):
    return lax.dot_general(a, b, (((1,), (1,)), ((), ())), preferred_element_type=F32)


def _norm_mm_kernel(x_ref, g_ref, w_ref, o_ref, xn_ref):
    @pl.when(pl.program_id(1) == 0)
    def _():
        xn_ref[...] = _rms(x_ref[...].astype(F32), g_ref[...]).astype(BF16)

    o_ref[...] = _dot(xn_ref[...], w_ref[...]).astype(o_ref.dtype)


def norm_matmul(x, g, w, *, tm, tn):
    M, K = x.shape
    N = w.shape[1]
    return pl.pallas_call(
        _norm_mm_kernel,
        grid=(M // tm, N // tn),
        in_specs=[pl.BlockSpec((tm, K), lambda i, j: (i, 0)),
                  pl.BlockSpec((1, K), lambda i, j: (0, 0)),
                  pl.BlockSpec((K, tn), lambda i, j: (0, j))],
        out_specs=pl.BlockSpec((tm, tn), lambda i, j: (i, j)),
        out_shape=jax.ShapeDtypeStruct((M, N), BF16),
        scratch_shapes=[pltpu.VMEM((tm, K), BF16)],
        compiler_params=_cparams(("parallel", "arbitrary")),
        name="norm_matmul",
    )(x, g, w)


def _bias_tiles_kernel(tab_ref, bkt_ref, o_ref, *, inv_scale, h0):
    h = pl.program_id(0) + h0
    bkt = bkt_ref[...]
    far = tab_ref[REL_BUCKETS - 1, h]
    out = jnp.zeros(bkt.shape, F32)
    for b in range(REL_BUCKETS - 1):
        out = jnp.where(bkt == b, (tab_ref[b, h] - far) * inv_scale, out)
    o_ref[...] = out


def _bucket_tiles(tq):
    i = np.arange(tq)[:, None]
    j = np.arange(tq)[None, :]
    d = np.stack([i - j, tq + i - j]).astype(np.int32)
    n = jnp.maximum(jnp.asarray(d), 0)
    exact = REL_BUCKETS // 2
    nf = jnp.maximum(n, 1).astype(F32)
    log_b = exact + (jnp.log(nf / exact) / math.log(REL_MAX_DIST / exact) * (REL_BUCKETS - exact)).astype(jnp.int32)
    return jnp.where(n < exact, n, jnp.minimum(log_b, REL_BUCKETS - 1)).astype(jnp.int32)


def bias_tiles(rel_bias, tq, h0, nh):
    assert tq >= REL_MAX_DIST
    bkt = _bucket_tiles(tq)
    assert DSA_HEAD_DIM == NSA_HEAD_DIM
    return pl.pallas_call(
        functools.partial(_bias_tiles_kernel, inv_scale=DSA_HEAD_DIM ** 0.5, h0=h0),
        grid=(nh,),
        in_specs=[pl.BlockSpec(memory_space=pltpu.SMEM),
                  pl.BlockSpec((2, tq, tq), lambda h: (0, 0, 0))],
        out_specs=pl.BlockSpec((None, 2, tq, tq), lambda h: (h, 0, 0, 0)),
        out_shape=jax.ShapeDtypeStruct((nh, 2, tq, tq), F32),
        compiler_params=_cparams(("arbitrary",)),
        name="bias_tiles",
    )(rel_bias, bkt)


def _mla_prep_kernel(lat_ref, pos_ref, freq_ref, qg_ref, kvg_ref, wqn_ref, wqr_ref, wqrr_ref, wkv_ref,
                     qn_ref, qr_ref, kn_ref, v_ref, kr_ref):
    lat = lat_ref[...].astype(F32)
    qlat = _rms(lat[:, :MLA_Q_LORA], qg_ref[...]).astype(BF16)
    kvlat = _rms(lat[:, MLA_Q_LORA:MLA_Q_LORA + MLA_KV_LORA], kvg_ref[...]).astype(BF16)
    o = MLA_Q_LORA + MLA_KV_LORA
    kr = lat[:, o:o + LANE]
    krr = lat[:, o + LANE:o + 2 * LANE]
    ang = pos_ref[...].astype(F32) * freq_ref[...]
    cos, sin = jnp.cos(ang), jnp.sin(ang)
    cos8 = jnp.concatenate([cos] * MLA_HEADS, axis=1)
    sin8 = jnp.concatenate([sin] * MLA_HEADS, axis=1)
    qn_ref[...] = _dot(qlat, wqn_ref[...]).astype(BF16)
    qr_ref[...] = (_dot(qlat, wqr_ref[...]) * cos8 + _dot(qlat, wqrr_ref[...]) * sin8).astype(BF16)
    kv = _dot(kvlat, wkv_ref[...])
    nk = MLA_HEADS * MLA_NOPE
    kn_ref[...] = kv[:, :nk].astype(BF16)
    v_ref[...] = kv[:, nk:].astype(BF16)
    kr_ref[...] = (kr * cos + krr * sin).astype(BF16)


def mla_prep(proj, pos, freq, qg, kvg, wqn, wqr, wqrr, wkv, *, tm):
    M = proj.shape[0]
    HD = MLA_HEADS * LANE
    full = lambda a: pl.BlockSpec(a.shape, lambda i: (0, 0))
    outs = [jax.ShapeDtypeStruct((M, HD), BF16)] * 4 + [jax.ShapeDtypeStruct((M, LANE), BF16)]
    return pl.pallas_call(
        _mla_prep_kernel,
        grid=(M // tm,),
        in_specs=[pl.BlockSpec((tm, 1024), lambda i: (i, OFF_MLA // 1024)),
                  pl.BlockSpec((tm, 1), lambda i: (i, 0)),
                  full(freq), full(qg), full(kvg), full(wqn), full(wqr), full(wqrr), full(wkv)],
        out_specs=[pl.BlockSpec((tm, HD), lambda i: (i, 0))] * 4 + [pl.BlockSpec((tm, LANE), lambda i: (i, 0))],
        out_shape=outs,
        compiler_params=_cparams(("parallel",)),
        name="mla_prep",
    )(proj, pos, freq, qg, kvg, wqn, wqr, wqrr, wkv)


def _softmax_init(m_ref, acc_ref):
    m_ref[...] = jnp.full(m_ref.shape, NEG, F32)
    acc_ref[...] = jnp.zeros(acc_ref.shape, F32)


def _with_ones(v):
    return jnp.concatenate([v, jnp.ones(v.shape, v.dtype)], axis=1)


def _softmax_step(s, v, m_ref, acc_ref, scale):
    c = scale * math.log2(math.e)
    m_prev = m_ref[...]
    m_new = jnp.maximum(m_prev, jnp.max(s, axis=-1, keepdims=True))
    alpha = jnp.exp2((m_prev - m_new) * c)
    p = jnp.exp2((s - jnp.tile(m_new, (1, s.shape[1] // LANE))) * c)
    acc_ref[...] = jnp.tile(alpha, (1, 2)) * acc_ref[...] + _dot(p.astype(BF16), _with_ones(v))
    m_ref[...] = m_new


def _softmax_finish(acc_ref):
    acc = acc_ref[...]
    return acc[:, :LANE] / acc[:, LANE:]


def _tri(tq, tk):
    row = lax.broadcasted_iota(jnp.int32, (tq, tk), 0)
    col = lax.broadcasted_iota(jnp.int32, (tq, tk), 1)
    return row, col


def _mla_attn_kernel(qn_ref, qr_ref, kn_ref, kr_ref, v_ref, o_ref, m_ref, acc_ref, *, tq, scale):
    qi = pl.program_id(2)
    hs = range(HPS)
    q = [jnp.concatenate([qn_ref[:, h * LANE:(h + 1) * LANE], qr_ref[:, h * LANE:(h + 1) * LANE]], axis=-1)
         for h in hs]
    for h in hs:
        _softmax_init(m_ref.at[h], acc_ref.at[h])

    def scores(j, width=tq):
        sl = pl.ds(pl.multiple_of(j * tq, tq), width)
        kr = kr_ref[sl, :]
        return [_dot_nt(q[h], jnp.concatenate([kn_ref[sl, h * LANE:(h + 1) * LANE], kr], axis=-1)) for h in hs], sl

    def far(j, width):
        s, sl = scores(j, width)
        for h in hs:
            _softmax_step(s[h], v_ref[sl, h * LANE:(h + 1) * LANE], m_ref.at[h], acc_ref.at[h], scale)

    def body(p, c):
        far(2 * p, 2 * tq)
        return c

    lax.fori_loop(0, qi // 2, body, 0)

    @pl.when(qi % 2 == 1)
    def _():
        far(qi - 1, tq)

    s, sl = scores(qi)
    row, col = _tri(tq, tq)
    for h in hs:
        _softmax_step(jnp.where(row >= col, s[h], NEG), v_ref[sl, h * LANE:(h + 1) * LANE],
                      m_ref.at[h], acc_ref.at[h], scale)
    for h in hs:
        o_ref[:, h * LANE:(h + 1) * LANE] = _softmax_finish(acc_ref.at[h]).astype(o_ref.dtype)


def mla_attention(qn, qr, kn, kr, v, *, B, S, tq):
    M = B * S
    nq = S // tq
    H = MLA_HEADS
    scale = (MLA_NOPE + MLA_ROPE) ** -0.5
    w = HPS * LANE
    qspec = pl.BlockSpec((tq, w), lambda b, h, i: (b * nq + i, h))
    kspec = pl.BlockSpec((S, w), lambda b, h, i: (b, h))
    return pl.pallas_call(
        functools.partial(_mla_attn_kernel, tq=tq, scale=scale),
        grid=(B, H // HPS, nq),
        in_specs=[qspec, qspec, kspec, pl.BlockSpec((S, LANE), lambda b, h, i: (b, 0)), kspec],
        out_specs=qspec,
        out_shape=jax.ShapeDtypeStruct((M, H * MLA_V), BF16),
        scratch_shapes=[pltpu.VMEM((HPS, tq, LANE), F32), pltpu.VMEM((HPS, tq, 2 * LANE), F32)],
        compiler_params=_cparams(("parallel", "parallel", "arbitrary")),
        name="mla_attention",
    )(qn, qr, kn, kr, v)


def _dsa_index_kernel(iq_ref, ik_ref, iw_ref, tri_ref, mb_ref, sc_ref, *, tq, tk, S, topk, iters):
    t0 = pl.program_id(1) * tq
    nk = S // tk
    nch = (t0 + tq - 1) // tk + 1
    hw = IDX_CHUNK
    nb = tk // LANE
    iq = iq_ref[...]
    iw = iw_ref[...].astype(F32)
    lane = lax.broadcasted_iota(jnp.int32, (tq, LANE), 1)
    qs = []
    for h in range(IDX_HEADS):
        blk = iq[:, (h // 2) * LANE:(h // 2 + 1) * LANE]
        keep = (lane >= IDX_DIM) if h % 2 else (lane < IDX_DIM)
        qs.append(jnp.where(keep, blk, jnp.zeros_like(blk)))
    q8 = jnp.concatenate(qs, axis=0)
    wcol = [iw[:, h:h + 1] for h in range(IDX_HEADS)]
    row = lax.broadcasted_iota(jnp.int32, (tq, hw), 0) + t0
    col0 = lax.broadcasted_iota(jnp.int32, (tq, hw), 1)

    def lane_fold(x, op):
        out = x[:, 0:LANE]
        for b in range(1, x.shape[1] // LANE):
            out = op(out, x[:, b * LANE:(b + 1) * LANE])
        return out

    def score_chunk(c, carry):
        mn, mx = carry
        for half in range(tk // hw):
            k0 = pl.multiple_of(c * tk + half * hw, hw)
            res = _dot_nt(q8, ik_ref[pl.ds(k0, hw), :])
            sc = wcol[0] * jnp.maximum(res[0:tq], 0.0)
            for h in range(1, IDX_HEADS):
                sc = sc + wcol[h] * jnp.maximum(res[h * tq:(h + 1) * tq], 0.0)
            causal = (col0 + k0) <= row
            mn = jnp.minimum(mn, lane_fold(jnp.where(causal, sc, -NEG), jnp.minimum))
            sc = jnp.where(causal, sc, NEG)
            mx = jnp.maximum(mx, lane_fold(sc, jnp.maximum))
            sc_ref[c, :, half * hw:(half + 1) * hw] = sc
        return mn, mx

    mn, mx = lax.fori_loop(0, nch, score_chunk,
                           (jnp.full((tq, LANE), -NEG, F32), jnp.full((tq, LANE), NEG, F32)))
    n_causal = (lax.broadcasted_iota(jnp.int32, (tq, 1), 0) + (t0 + 1)).astype(F32)
    kf = jnp.minimum(float(topk), n_causal)

    rs = IDX_ROWS

    def blocks(j):
        for r in range(tq // rs):
            for b in range(nb):
                rows = slice(r * rs, (r + 1) * rs)
                yield rows, (j, rows, slice(b * LANE, (b + 1) * LANE)), j * tk + b * LANE

    def count(pred):
        def body(c, acc):
            parts = [acc[r * rs:(r + 1) * rs] for r in range(tq // rs)]
            for rows, idx, off in blocks(c):
                r = rows.start // rs
                parts[r] = parts[r] + jnp.where(pred(rows, sc_ref[idx], off), 1.0, 0.0)
            return jnp.concatenate(parts, axis=0)

        acc = lax.fori_loop(0, nch, body, jnp.zeros((tq, LANE), F32))
        return jnp.sum(acc, axis=-1, keepdims=True)

    def bcast(x):
        strips = [jnp.broadcast_to(x[r * rs:(r + 1) * rs], (rs, LANE)) for r in range(tq // rs)]
        return lambda rows: strips[rows.start // rs]

    lo0 = jnp.min(mn, axis=-1, keepdims=True)
    mx = jnp.max(mx, axis=-1, keepdims=True)
    hi0 = mx + (jnp.abs(mx) * 1e-6 + 1.0)

    def bisect(c):
        lo, hi, n_lo, n_hi = c
        mid = 0.5 * (lo + hi)
        mid_b = bcast(mid)
        n_mid = count(lambda rows, blk, off: blk >= mid_b(rows))
        ge = n_mid >= kf
        return (jnp.where(ge, mid, lo), jnp.where(ge, hi, mid),
                jnp.where(ge, n_mid, n_lo), jnp.where(ge, n_hi, n_mid))

    def search(state, limit, settled):
        def cond(c):
            return jnp.logical_and(c[0] < limit, jnp.max((c[3] - kf) * (1.0 - settled)) > 0.0)

        def body(c):
            return (c[0] + 2,) + bisect(bisect(c[1:]))

        return lax.while_loop(cond, body, state)

    def band_is_single_value(lo, hi):
        lo_b, hi_b = bcast(lo), bcast(hi)

        def body(c, carry):
            bmin = [carry[0][r * rs:(r + 1) * rs] for r in range(tq // rs)]
            bmax = [carry[1][r * rs:(r + 1) * rs] for r in range(tq // rs)]
            for rows, idx, off in blocks(c):
                r = rows.start // rs
                blk = sc_ref[idx]
                inside = (blk >= lo_b(rows)) & (blk < hi_b(rows))
                bmin[r] = jnp.minimum(bmin[r], jnp.where(inside, blk, -NEG))
                bmax[r] = jnp.maximum(bmax[r], jnp.where(inside, blk, NEG))
            return jnp.concatenate(bmin, axis=0), jnp.concatenate(bmax, axis=0)

        bmin, bmax = lax.fori_loop(0, nch, body,
                                   (jnp.full((tq, LANE), -NEG, F32), jnp.full((tq, LANE), NEG, F32)))
        same = jnp.min(bmin, axis=-1, keepdims=True) == jnp.max(bmax, axis=-1, keepdims=True)
        return jnp.where(same, 1.0, 0.0)

    zero = jnp.zeros((tq, 1), F32)
    state = search((jnp.int32(0), lo0, hi0, n_causal, zero), IDX_TIE_CHECK, zero)
    unresolved = jnp.max(state[3] - kf) > 0.0
    settled = lax.cond(unresolved, lambda: band_is_single_value(state[1], state[2]), lambda: zero)
    _, lo, hi, n_lo, n_hi = search(state, iters, settled)
    tied = jnp.max(n_lo - kf) > 0.0

    def write(sel):
        def body(j, c):
            for rows, idx, off in blocks(j):
                mb_ref[idx] = jnp.where(sel(rows, sc_ref[idx], off), 0.0, NEG).astype(mb_ref.dtype)
            return c

        lax.fori_loop(0, nch, body, 0)

    def fill(j, c):
        mb_ref[j] = jnp.full((tq, tk), NEG, mb_ref.dtype)
        return c

    lax.fori_loop(nch, nk, fill, 0)
    lo_b = bcast(lo)

    @pl.when(jnp.logical_not(tied))
    def _():
        write(lambda rows, blk, off: blk >= lo_b(rows))

    @pl.when(tied)
    def _():
        need = kf - n_hi
        tri = tri_ref[...]

        def body(j, base):
            sc = sc_ref[j]
            above = sc >= hi
            band = (sc >= lo) & jnp.logical_not(above)
            prefix = _dot(jnp.where(band, 1.0, 0.0).astype(BF16), tri) + base
            keep = above | (band & (prefix <= need))
            mb_ref[j] = jnp.where(keep, 0.0, NEG).astype(mb_ref.dtype)
            return prefix[:, tk - 1:tk]

        lax.fori_loop(0, nch, body, zero)


def dsa_index_mask(proj, *, B, S, tqi, tk, topk):
    nq = S // tqi
    nk = S // tk
    assert tk % tqi == 0 and tk % IDX_CHUNK == 0
    tri = jnp.asarray(np.triu(np.ones((tk, tk), np.float32)), BF16)
    return pl.pallas_call(
        functools.partial(_dsa_index_kernel, tq=tqi, tk=tk, S=S, topk=topk, iters=BISECT_ITERS),
        grid=(B, nq),
        in_specs=[pl.BlockSpec((tqi, 512), lambda b, i: (b * nq + i, OFF_IDX_Q // 512)),
                  pl.BlockSpec((S, LANE), lambda b, i: (b, OFF_IDX_K // LANE)),
                  pl.BlockSpec((tqi, LANE), lambda b, i: (b * nq + i, OFF_IDX_W // LANE)),
                  pl.BlockSpec((tk, tk), lambda b, i: (0, 0))],
        out_specs=pl.BlockSpec((None, nk, tqi, tk), lambda b, i: (b, 0, i, 0)),
        out_shape=jax.ShapeDtypeStruct((B, nk, S, tk), BF16),
        scratch_shapes=[pltpu.VMEM((nk, tqi, tk), F32)],
        compiler_params=_cparams(("parallel", "parallel")),
        name="dsa_index_mask",
    )(proj, proj, proj, tri)


def _dsa_attn_kernel(q_ref, k_ref, v_ref, mb_ref, bias_ref, o_ref, m_ref, acc_ref, *, tq, scale):
    qi = pl.program_id(2)
    hs = range(HPS)
    cols = lambda h: slice(h * LANE, (h + 1) * LANE)
    q = [q_ref[:, cols(h)] for h in hs]
    for h in hs:
        _softmax_init(m_ref.at[h], acc_ref.at[h])

    def step(j, nc, bias):
        sl = pl.ds(j * tq if isinstance(j, int) else pl.multiple_of(j * tq, tq), nc * tq)
        mb = mb_ref[j] if nc == 1 else jnp.concatenate([mb_ref[j], mb_ref[j + 1]], axis=1)
        mb = mb.astype(F32)
        s = [_dot_nt(q[h], k_ref[sl, cols(h)]) + mb for h in hs]
        for h in hs:
            sh = s[h] if bias is None else s[h] + bias(h)
            _softmax_step(sh, v_ref[sl, cols(h)], m_ref.at[h], acc_ref.at[h], scale)

    n_far = jnp.maximum(qi - 1, 0)

    def body(p, c):
        step(2 * p, 2, None)
        return c

    lax.fori_loop(0, n_far // 2, body, 0)

    @pl.when(n_far % 2 == 1)
    def _():
        step(n_far - 1, 1, None)

    @pl.when(qi == 0)
    def _():
        step(0, 1, lambda h: bias_ref[h, 0])

    @pl.when(qi >= 1)
    def _():
        step(qi - 1, 2, lambda h: jnp.concatenate([bias_ref[h, 1], bias_ref[h, 0]], axis=1))

    for h in hs:
        o_ref[:, cols(h)] = _softmax_finish(acc_ref.at[h]).astype(o_ref.dtype)


def dsa_attention(proj, maskb, btiles, *, B, S, tq):
    M = B * S
    nq = S // tq
    H = DSA_HEADS
    scale = DSA_HEAD_DIM ** -0.5
    w = HPS * LANE
    return pl.pallas_call(
        functools.partial(_dsa_attn_kernel, tq=tq, scale=scale),
        grid=(B, H // HPS, nq),
        in_specs=[pl.BlockSpec((tq, w), lambda b, h, i: (b * nq + i, OFF_DSA_Q // w + h)),
                  pl.BlockSpec((S, w), lambda b, h, i: (b, OFF_DSA_K // w + h)),
                  pl.BlockSpec((S, w), lambda b, h, i: (b, OFF_DSA_V // w + h)),
                  pl.BlockSpec((None, nq, tq, tq), lambda b, h, i: (b, 0, i, 0)),
                  pl.BlockSpec((HPS, 2, tq, tq), lambda b, h, i: (h, 0, 0, 0))],
        out_specs=pl.BlockSpec((tq, w), lambda b, h, i: (b * nq + i, h)),
        out_shape=jax.ShapeDtypeStruct((M, H * DSA_HEAD_DIM), BF16),
        scratch_shapes=[pltpu.VMEM((HPS, tq, LANE), F32), pltpu.VMEM((HPS, tq, 2 * LANE), F32)],
        compiler_params=_cparams(("parallel", "parallel", "arbitrary")),
        name="dsa_attention",
    )(proj, proj, proj, maskb, btiles)


def _nsa_compress_kernel(x_ref, pos_ref, w1_ref, w2_ref, o_ref, *, ncp):
    dk = NSA_HEAD_DIM
    posw = _dot(pos_ref[...], w1_ref[...])[0:1, :]
    for g in range(NSA_GROUPS):
        lo = jnp.zeros((ncp, CMP_HIDDEN), F32)
        hi = jnp.zeros((ncp, CMP_HIDDEN), F32)
        for l in range(CMP_STRIDE):
            xs = x_ref[:, (l * NSA_GROUPS + g) * dk:(l * NSA_GROUPS + g + 1) * dk]
            lo = lo + _dot(xs, w1_ref[l * dk:(l + 1) * dk, :])
            hi = hi + _dot(xs, w1_ref[(CMP_STRIDE + l) * dk:(CMP_STRIDE + l + 1) * dk, :])
        hid = lo + pltpu.roll(hi, ncp - 1, 0) + posw
        o_ref[g] = _dot(jax.nn.gelu(hid).astype(BF16), w2_ref[...]).astype(o_ref.dtype)


def nsa_compress(xkv, posflat, w1, w2, *, B, S):
    ncp = S // CMP_STRIDE
    G, dk = NSA_GROUPS, NSA_HEAD_DIM
    width = CMP_STRIDE * G * dk
    return pl.pallas_call(
        functools.partial(_nsa_compress_kernel, ncp=ncp),
        grid=(2, B),
        in_specs=[pl.BlockSpec((None, ncp, width), lambda a, b: (a, b, 0)),
                  pl.BlockSpec((None, 8, CMP_LEN * dk), lambda a, b: (a, 0, 0)),
                  pl.BlockSpec((None, CMP_LEN * dk, CMP_HIDDEN), lambda a, b: (a, 0, 0)),
                  pl.BlockSpec((None, CMP_HIDDEN, dk), lambda a, b: (a, 0, 0))],
        out_specs=pl.BlockSpec((None, None, G, ncp, dk), lambda a, b: (a, b, 0, 0, 0)),
        out_shape=jax.ShapeDtypeStruct((2, B, G, ncp, dk), BF16),
        compiler_params=_cparams(("parallel", "parallel")),
        name="nsa_compress",
    )(xkv, posflat, w1, w2)


def _nsa_cmp_kernel(q_ref, kc_ref, vc_ref, ov_ref, oc_ref, sel_ref, *, tq, ncp, ns, n_sel, scale):
    t0 = pl.program_id(2) * tq
    dk = NSA_HEAD_DIM
    kc = kc_ref[...]
    vc = vc_ref[...]
    trow = lax.broadcasted_iota(jnp.int32, (tq, ncp), 0) + t0
    ccol = lax.broadcasted_iota(jnp.int32, (tq, ncp), 1)
    vis = (ccol * CMP_STRIDE + (CMP_LEN - 1)) <= trow
    psum = jnp.zeros((tq, ncp), F32)
    for h in range(NSA_HG):
        s = jnp.where(vis, _dot_nt(q_ref[:, h * dk:(h + 1) * dk], kc) * scale, NEG)
        m = jnp.max(s, axis=-1, keepdims=True)
        p = jnp.where(vis, jnp.exp(s - m), 0.0)
        d = jnp.sum(p, axis=-1, keepdims=True)
        p = p / jnp.where(d > 0, d, 1.0)
        oc_ref[:, h * dk:(h + 1) * dk] = _dot(p.astype(BF16), vc).astype(oc_ref.dtype)
        psum = psum + p
    p_hi = psum.astype(BF16)
    p_lo = (psum - p_hi.astype(F32)).astype(BF16)
    ov = ov_ref[...]
    imp = _dot_nt(ov, p_hi) + _dot_nt(ov, p_lo)
    imp = imp[:ns]
    jrow = lax.broadcasted_iota(jnp.int32, (ns, tq), 0)
    tcol = lax.broadcasted_iota(jnp.int32, (ns, tq), 1) + t0
    blk_t = tcol // SEL_BLOCK
    forced = (jrow == 0) | (jrow == blk_t) | (jrow == blk_t - 1)
    val = jnp.where(forced, -NEG, imp)
    val = jnp.where(jrow * SEL_BLOCK > tcol, NEG, val)
    rank = jnp.zeros((ns, tq), F32)
    for j in range(ns):
        other = val[j:j + 1, :]
        ahead = (other > val) | ((other == val) & (jrow > j))
        rank = rank + jnp.where(ahead, 1.0, 0.0)
    selneg = jnp.where(rank < float(n_sel), 0.0, NEG)
    if ns < LANE:
        selneg = jnp.concatenate([selneg, jnp.zeros((LANE - ns, tq), F32)], axis=0)
    sel_ref[...] = selneg.T.astype(sel_ref.dtype)


def nsa_cmp_attention(proj, kvc, ovt, *, B, S, tq):
    M = B * S
    nq = S // tq
    G, dk = NSA_GROUPS, NSA_HEAD_DIM
    ncp = S // CMP_STRIDE
    ns = S // SEL_BLOCK
    assert ns <= LANE
    n_sel = min(SEL_COUNT, ns)
    gw = NSA_HG * dk
    return pl.pallas_call(
        functools.partial(_nsa_cmp_kernel, tq=tq, ncp=ncp, ns=ns, n_sel=n_sel, scale=dk ** -0.5),
        grid=(B, G, nq),
        in_specs=[pl.BlockSpec((tq, gw), lambda b, g, i: (b * nq + i, OFF_NSA_Q // gw + g)),
                  pl.BlockSpec((None, None, None, ncp, dk), lambda b, g, i: (0, b, g, 0, 0)),
                  pl.BlockSpec((None, None, None, ncp, dk), lambda b, g, i: (1, b, g, 0, 0)),
                  pl.BlockSpec((LANE, ncp), lambda b, g, i: (0, 0))],
        out_specs=[pl.BlockSpec((tq, gw), lambda b, g, i: (b * nq + i, g)),
                   pl.BlockSpec((None, None, tq, LANE), lambda b, g, i: (b, g, i, 0))],
        out_shape=[jax.ShapeDtypeStruct((M, G * gw), BF16),
                   jax.ShapeDtypeStruct((B, G, S, LANE), BF16)],
        compiler_params=_cparams(("parallel", "parallel", "parallel")),
        name="nsa_cmp_attention",
    )(proj, kvc, kvc, ovt)


def _nsa_main_kernel(q_ref, sel_ref, ks_ref, vs_ref, kw_ref, vw_ref, ind_ref, bias_ref, gate_ref, oc_ref,
                     o_ref, m_ref, acc_ref, *, tq, nback, scale):
    g = pl.program_id(1)
    qi = pl.program_id(2)
    dk = NSA_HEAD_DIM
    HG = NSA_HG
    ns = NSA_STREAMS
    hp = HG // ns
    st = range(ns)
    stack = lambda f: [jnp.concatenate([f(a * hp + i) for i in range(hp)], axis=0) for a in st]
    q = stack(lambda h: q_ref[:, h * dk:(h + 1) * dk])
    sel = jnp.concatenate([sel_ref[...]] * hp, axis=0)
    qa = [jnp.concatenate([q[a], sel], axis=1) for a in st]
    bias0 = stack(lambda h: bias_ref[h, 0])
    bias1 = stack(lambda h: bias_ref[h, 1])
    gate = jax.nn.sigmoid(gate_ref[...].astype(F32))

    def init():
        for a in st:
            _softmax_init(m_ref.at[a], acc_ref.at[a])

    def steps(s, v, fix):
        for a in st:
            _softmax_step(fix(a, s[a]), v, m_ref.at[a], acc_ref.at[a], scale)

    def near_step(scores, first, nc, window):
        s, v = scores(first, nc * tq)
        r = jnp.concatenate([lax.broadcasted_iota(jnp.int32, (tq, nc * tq), 0)] * hp, axis=0)
        c = lax.broadcasted_iota(jnp.int32, (hp * tq, nc * tq), 1)
        ok = c <= r + (nc - 1) * tq
        if window and nc * tq > WINDOW:
            ok = ok & (c > r + (nc - 1) * tq - WINDOW)

        def fix(a, x):
            pads = [jnp.zeros((hp * tq, (nc - 2) * tq), F32)] if nc > 2 else []
            bias = bias0[a] if nc == 1 else jnp.concatenate(pads + [bias1[a], bias0[a]], axis=1)
            return jnp.where(ok, x + bias, NEG)

        steps(s, v, fix)

    def near(scores, nspan, window):
        for nc in range(1, nspan):
            @pl.when(qi == nc - 1)
            def _(nc=nc):
                near_step(scores, 0, nc, window)

        @pl.when(qi >= nspan - 1)
        def _():
            near_step(scores, qi - (nspan - 1), nspan, window)

    init()

    def rows_from(j, width):
        return pl.ds(j * tq if isinstance(j, int) else pl.multiple_of(j * tq, tq), width)

    def sel_scores(j, width=tq):
        sl = rows_from(j, width)
        ka = jnp.concatenate([ks_ref[sl, :], ind_ref[sl, :]], axis=1)
        return [_dot_nt(qa[a], ka) for a in st], vs_ref[sl, :]

    n_far = jnp.maximum(qi - 1, 0)

    def sel_body(p, c):
        s, v = sel_scores(2 * p, 2 * tq)
        steps(s, v, lambda a, x: x)
        return c

    lax.fori_loop(0, n_far // 2, sel_body, 0)

    @pl.when(n_far % 2 == 1)
    def _():
        s, v = sel_scores(n_far - 1)
        steps(s, v, lambda a, x: x)

    near(sel_scores, 2, False)
    o_s = [_softmax_finish(acc_ref.at[a]) for a in st]

    init()

    def win_scores(j, width):
        sl = rows_from(j, width)
        kw = kw_ref[sl, :]
        return [_dot_nt(q[a], kw) for a in st], vw_ref[sl, :]

    near(win_scores, nback + 1, True)
    o_w = [_softmax_finish(acc_ref.at[a]) for a in st]

    lane = lax.broadcasted_iota(jnp.int32, gate.shape, 1)
    for h in range(HG):
        c = g * HG + h
        a, sl = h // hp, slice((h % hp) * tq, (h % hp + 1) * tq)
        gc = [jnp.sum(jnp.where(lane == br * NSA_HEADS + c, gate, 0.0), axis=-1, keepdims=True)
              for br in range(3)]
        o = gc[0] * oc_ref[:, h * dk:(h + 1) * dk].astype(F32) + gc[1] * o_s[a][sl] + gc[2] * o_w[a][sl]
        o_ref[:, h * dk:(h + 1) * dk] = o.astype(o_ref.dtype)


def nsa_main(proj, selneg, ind, btiles, o_c, *, B, S, tq):
    M = B * S
    nq = S // tq
    G, dk, HG = NSA_GROUPS, NSA_HEAD_DIM, NSA_HG
    gw = HG * dk
    assert WINDOW % tq == 0
    nback = WINDOW // tq
    kv = lambda n: pl.BlockSpec((S, dk), lambda b, g, i, n=n: (b, (OFF_NSA_KV + n * G * dk) // dk + g))
    return pl.pallas_call(
        functools.partial(_nsa_main_kernel, tq=tq, nback=nback, scale=dk ** -0.5),
        grid=(B, G, nq),
        in_specs=[pl.BlockSpec((tq, gw), lambda b, g, i: (b * nq + i, OFF_NSA_Q // gw + g)),
                  pl.BlockSpec((None, None, tq, LANE), lambda b, g, i: (b, g, i, 0)),
                  kv(2), kv(3), kv(4), kv(5),
                  pl.BlockSpec((S, LANE), lambda b, g, i: (0, 0)),
                  pl.BlockSpec((HG, 2, tq, tq), lambda b, g, i: (g, 0, 0, 0)),
                  pl.BlockSpec((tq, LANE), lambda b, g, i: (b * nq + i, OFF_NSA_GATE // LANE)),
                  pl.BlockSpec((tq, gw), lambda b, g, i: (b * nq + i, g))],
        out_specs=pl.BlockSpec((tq, gw), lambda b, g, i: (b * nq + i, g)),
        out_shape=jax.ShapeDtypeStruct((M, G * gw), BF16),
        scratch_shapes=[pltpu.VMEM((NSA_STREAMS, HG // NSA_STREAMS * tq, LANE), F32),
                        pltpu.VMEM((NSA_STREAMS, HG // NSA_STREAMS * tq, 2 * LANE), F32)],
        compiler_params=_cparams(("parallel", "parallel", "arbitrary")),
        name="nsa_main",
    )(proj, selneg, proj, proj, proj, proj, ind, btiles, proj, o_c)


def _merge_kernel(x_ref, oa_ref, ob_ref, oc_ref, g0_ref, g1_ref, g2_ref, wb_ref, wo_ref, gain_ref, o_ref):
    merged = None
    for o_r, g_r, br in ((oa_ref, g0_ref, 0), (ob_ref, g1_ref, 1), (oc_ref, g2_ref, 2)):
        t = jax.nn.sigmoid(g_r[...].astype(F32)) * _dot(o_r[...], wb_ref[br])
        merged = t if merged is None else merged + t
    y = _dot(merged.astype(BF16), wo_ref[...])
    o_ref[...] = x_ref[...] + _rms(y, gain_ref[...])


def merge_branches(x, o_a, o_b, o_c, proj, wb, wo, gain, *, tm):
    M, D = x.shape
    row = pl.BlockSpec((tm, D), lambda i: (i, 0))
    gate = lambda br: pl.BlockSpec((tm, D), lambda i, br=br: (i, OFF_BGATE // D + br))
    return pl.pallas_call(
        _merge_kernel,
        grid=(M // tm,),
        in_specs=[row, row, row, row, gate(0), gate(1), gate(2),
                  pl.BlockSpec(wb.shape, lambda i: (0, 0, 0)),
                  pl.BlockSpec(wo.shape, lambda i: (0, 0)),
                  pl.BlockSpec((1, D), lambda i: (0, 0))],
        out_specs=row,
        out_shape=jax.ShapeDtypeStruct((M, D), F32),
        compiler_params=_cparams(("parallel",)),
        name="merge_branches",
    )(x, o_a, o_b, o_c, proj, proj, proj, wb, wo, gain)


def _cross_kernel(x_ref, kv_ref, wq_ref, wo_ref, gpre_ref, gpost_ref, o_ref, *, scale):
    x = x_ref[...]
    h = _rms(x, gpre_ref[...]).astype(BF16)
    q = _dot(h, wq_ref[...]).astype(BF16)
    dh = CROSS_HEAD_DIM
    outs = []
    for hd in range(CROSS_HEADS):
        k = kv_ref[:, hd * dh:(hd + 1) * dh]
        v = kv_ref[:, D_MODEL + hd * dh:D_MODEL + (hd + 1) * dh]
        s = _dot_nt(q[:, hd * dh:(hd + 1) * dh], k) * scale
        p = jnp.exp(s - jnp.max(s, axis=-1, keepdims=True))
        p = p / jnp.sum(p, axis=-1, keepdims=True)
        outs.append(_dot(p.astype(BF16), v).astype(BF16))
    y = _dot(jnp.concatenate(outs, axis=1), wo_ref[...])
    o_ref[...] = x + _rms(y, gpost_ref[...])


def cross_attention(x, kv, wq, wo, gpre, gpost, *, S, tm):
    M, D = x.shape
    mlen = kv.shape[0] // (M // S)
    per_b = S // tm
    return pl.pallas_call(
        functools.partial(_cross_kernel, scale=CROSS_HEAD_DIM ** -0.5),
        grid=(M // tm,),
        in_specs=[pl.BlockSpec((tm, D), lambda i: (i, 0)),
                  pl.BlockSpec((mlen, 2 * D), lambda i: (i // per_b, 0)),
                  pl.BlockSpec(wq.shape, lambda i: (0, 0)),
                  pl.BlockSpec(wo.shape, lambda i: (0, 0)),
                  pl.BlockSpec((1, D), lambda i: (0, 0)),
                  pl.BlockSpec((1, D), lambda i: (0, 0))],
        out_specs=pl.BlockSpec((tm, D), lambda i: (i, 0)),
        out_shape=jax.ShapeDtypeStruct((M, D), F32),
        compiler_params=_cparams(("parallel",)),
        name="cross_attention",
    )(x, kv, wq, wo, gpre, gpost)


def _mlp_kernel(x_ref, w1_ref, w2_ref, gpre_ref, gpost_ref, o_ref, h_ref, acc_ref):
    j = pl.program_id(1)

    @pl.when(j == 0)
    def _():
        h_ref[...] = _rms(x_ref[...], gpre_ref[...]).astype(BF16)
        acc_ref[...] = jnp.zeros(acc_ref.shape, F32)

    a = jnp.maximum(_dot(h_ref[...], w1_ref[...]), 0.0)
    acc_ref[...] += _dot((a * a).astype(BF16), w2_ref[...])

    @pl.when(j == pl.num_programs(1) - 1)
    def _():
        o_ref[...] = x_ref[...] + _rms(acc_ref[...], gpost_ref[...])


def mlp(x, w1, w2, gpre, gpost, *, tm, tf):
    M, D = x.shape
    FF = w1.shape[1]
    return pl.pallas_call(
        _mlp_kernel,
        grid=(M // tm, FF // tf),
        in_specs=[pl.BlockSpec((tm, D), lambda i, j: (i, 0)),
                  pl.BlockSpec((D, tf), lambda i, j: (0, j)),
                  pl.BlockSpec((tf, D), lambda i, j: (j, 0)),
                  pl.BlockSpec((1, D), lambda i, j: (0, 0)),
                  pl.BlockSpec((1, D), lambda i, j: (0, 0))],
        out_specs=pl.BlockSpec((tm, D), lambda i, j: (i, 0)),
        out_shape=jax.ShapeDtypeStruct((M, D), F32),
        scratch_shapes=[pltpu.VMEM((tm, D), BF16), pltpu.VMEM((tm, D), F32)],
        compiler_params=_cparams(("parallel", "arbitrary")),
        name="mlp",
    )(x, w1, w2, gpre, gpost)


def _rot_half_cols(w):
    half = w.shape[-1] // 2
    return jnp.concatenate([-w[..., half:], w[..., :half]], axis=-1)


def _pack_w_in(w):
    widths = (MLA_Q_LORA, MLA_KV_LORA, MLA_ROPE, 1024, 1024, 1024, IDX_HEADS * IDX_DIM, IDX_DIM, IDX_HEADS,
              1024, 256, 256, 256, 256, 256, 256, 3 * NSA_HEADS, N_BRANCH * D_MODEL)
    offs = np.concatenate([[0], np.cumsum(widths)])
    seg = [w[:, offs[i]:offs[i + 1]] for i in range(len(widths))]
    (q_lat, kv_lat, k_rope, dsa_q, dsa_k, dsa_v, idx_q, idx_k, idx_w,
     nsa_q, kc, vc, ks, vs, kw, vw, nsa_gate, bgate) = seg
    z = lambda n: jnp.zeros((w.shape[0], n), w.dtype)
    cols = [dsa_q, dsa_k, dsa_v, nsa_q, bgate, idx_q, kc, vc, ks, vs, kw, vw,
            q_lat, kv_lat, k_rope, z(LANE - MLA_ROPE), _rot_half_cols(k_rope), z(LANE - MLA_ROPE), z(LANE),
            idx_k, idx_k, idx_w, z(LANE - IDX_HEADS), nsa_gate, z(LANE - 3 * NSA_HEADS)]
    out = jnp.concatenate(cols, axis=1)
    out = jnp.concatenate([out, z(N_PACK - out.shape[1])], axis=1)
    return out.astype(BF16)


def _pack_mla_weights(w_uq, w_ukv):
    H = MLA_HEADS
    uq = w_uq.reshape(MLA_Q_LORA, H, MLA_NOPE + MLA_ROPE)
    nope = uq[:, :, :MLA_NOPE].reshape(MLA_Q_LORA, H * MLA_NOPE)
    rope = uq[:, :, MLA_NOPE:]
    pad = jnp.zeros((MLA_Q_LORA, H, LANE - MLA_ROPE), w_uq.dtype)
    wqr = jnp.concatenate([rope, pad], axis=-1).reshape(MLA_Q_LORA, H * LANE)
    wqrr = jnp.concatenate([_rot_half_cols(rope), pad], axis=-1).reshape(MLA_Q_LORA, H * LANE)
    ukv = w_ukv.reshape(MLA_KV_LORA, H, MLA_NOPE + MLA_V)
    wkv = jnp.concatenate([ukv[:, :, :MLA_NOPE].reshape(MLA_KV_LORA, H * MLA_NOPE),
                           ukv[:, :, MLA_NOPE:].reshape(MLA_KV_LORA, H * MLA_V)], axis=1)
    return nope.astype(BF16), wqr.astype(BF16), wqrr.astype(BF16), wkv.astype(BF16)


def _constants(S):
    half = MLA_ROPE // 2
    inv_freq = ROPE_BASE ** (-np.arange(0, MLA_ROPE, 2, dtype=np.float32) / MLA_ROPE)
    freq = np.zeros((1, LANE), np.float32)
    freq[0, :half] = inv_freq
    freq[0, half:2 * half] = inv_freq
    ncp = S // CMP_STRIDE
    ns = S // SEL_BLOCK
    c_start = np.arange(ncp) * CMP_STRIDE
    j_start = np.arange(ns) * SEL_BLOCK
    overlap = ((c_start[None, :] < j_start[:, None] + SEL_BLOCK) &
               (c_start[None, :] + CMP_LEN > j_start[:, None])).astype(np.float32)
    overlap[:, ncp - 1] = 0.0
    ovt = np.zeros((LANE, ncp), np.float32)
    ovt[:ns] = overlap
    ind = np.zeros((S, LANE), np.float32)
    ind[np.arange(S), np.arange(S) // SEL_BLOCK] = 1.0
    return jnp.asarray(freq), jnp.asarray(ovt, BF16), jnp.asarray(ind, BF16)


def kernel(x, mem, positions, rel_bias, norm_gains, w_in, mla_q_norm, mla_kv_norm, mla_w_uq, mla_w_ukv,
           nsa_cmp_pos, nsa_cmp_w1, nsa_cmp_w2, w_branch, w_out, cross_wq, cross_wkv, cross_wo, mlp_w1, mlp_w2):
    B, S, D = x.shape
    M = B * S
    depth = w_in.shape[0]
    tq = TQ if S % TQ == 0 else TQ_NSA
    tqn = TQ_NSA
    assert S % tq == 0 and S % tqn == 0 and D == D_MODEL
    topk = min(DSA_TOPK_MAX, S // 4)
    freq, ovt, ind = _constants(S)
    btiles = bias_tiles(rel_bias, tq, 0, DSA_HEADS)
    btiles_n = bias_tiles(rel_bias, tqn, DSA_HEADS, NSA_HEADS)
    pos = positions.reshape(M, 1)
    xf = x.reshape(M, D)
    memf = mem.reshape(B * mem.shape[1], D)
    tm_big = 1024 if M % 1024 == 0 else tq
    tm = 512 if M % 512 == 0 else tq
    ncp = S // CMP_STRIDE
    row = lambda v: v.reshape(1, -1)

    for l in range(depth):
        g = norm_gains[l]
        proj = norm_matmul(xf, row(g[0]), _pack_w_in(w_in[l]), tm=tm_big, tn=TN_IN)

        wqn, wqr, wqrr, wkv = _pack_mla_weights(mla_w_uq[l], mla_w_ukv[l])
        qn, qr, kn, vv, kr = mla_prep(proj, pos, freq, row(mla_q_norm[l]), row(mla_kv_norm[l]),
                                      wqn, wqr, wqrr, wkv, tm=tm)
        o_a = mla_attention(qn, qr, kn, kr, vv, B=B, S=S, tq=tq)

        maskb = dsa_index_mask(proj, B=B, S=S, tqi=TQ_IDX, tk=tq, topk=topk)
        o_b = dsa_attention(proj, maskb, btiles, B=B, S=S, tq=tq)

        kvc_in = jnp.stack([
            proj[:, OFF_NSA_KV + a * 256:OFF_NSA_KV + (a + 1) * 256].reshape(B * ncp, CMP_STRIDE * 256)
            for a in range(2)])
        posflat = jnp.broadcast_to(nsa_cmp_pos[l].reshape(2, 1, CMP_LEN * NSA_HEAD_DIM),
                                   (2, 8, CMP_LEN * NSA_HEAD_DIM)).astype(BF16)
        kvc = nsa_compress(kvc_in, posflat, nsa_cmp_w1[l].astype(BF16), nsa_cmp_w2[l].astype(BF16), B=B, S=S)
        o_cmp, selneg = nsa_cmp_attention(proj, kvc, ovt, B=B, S=S, tq=tqn)
        o_c = nsa_main(proj, selneg, ind, btiles_n, o_cmp, B=B, S=S, tq=tqn)

        xf = merge_branches(xf, o_a, o_b, o_c, proj, w_branch[l].astype(BF16), w_out[l].astype(BF16),
                            row(g[1]), tm=tm)

        mkv = norm_matmul(memf, row(g[3]), cross_wkv[l].astype(BF16), tm=memf.shape[0] // B, tn=1024)
        xf = cross_attention(xf, mkv, cross_wq[l].astype(BF16), cross_wo[l].astype(BF16),
                             row(g[2]), row(g[4]), S=S, tm=tm)

        xf = mlp(xf, mlp_w1[l].astype(BF16), mlp_w2[l].astype(BF16), row(g[5]), row(g[6]), tm=tm, tf=1024)

    return xf.reshape(B, S, D)
```

```python
import functools
import math

import numpy as np
import jax
import jax.numpy as jnp
from jax import lax
from jax.experimental import pallas as pl
from jax.experimental.pallas import tpu as pltpu

F32 = jnp.float32
BF16 = jnp.bfloat16

D_MODEL = 1024
EPS = 1e-6
MLA_HEADS = 8
MLA_Q_LORA = 384
MLA_KV_LORA = 256
MLA_NOPE = 128
MLA_ROPE = 64
MLA_V = 128
ROPE_BASE = 10000.0
DSA_HEADS = 8
DSA_HEAD_DIM = 128
IDX_HEADS = 8
IDX_DIM = 64
DSA_TOPK_MAX = 256
NSA_HEADS = 8
NSA_GROUPS = 2
NSA_HG = NSA_HEADS // NSA_GROUPS
NSA_HEAD_DIM = 128
CMP_STRIDE = 16
CMP_LEN = 2 * CMP_STRIDE
CMP_HIDDEN = 256
SEL_BLOCK = 64
SEL_COUNT = 16
WINDOW = 512
REL_BUCKETS = 32
REL_MAX_DIST = 128
REL_HEADS = DSA_HEADS + NSA_HEADS
CROSS_HEADS = 4
CROSS_HEAD_DIM = D_MODEL // CROSS_HEADS
D_FF = 4 * D_MODEL
N_BRANCH = 3

LANE = 128
NEG = -1e30
VMEM_LIMIT = 48 * 1024 * 1024

TQ = 512
TQ_NSA = 256
HPS = 2
NSA_STREAMS = 2
BISECT_ITERS = 32
IDX_TIE_CHECK = 16
TQ_IDX = 256
IDX_CHUNK = 256
IDX_ROWS = 64

OFF_DSA_Q = 0
OFF_DSA_K = 1024
OFF_DSA_V = 2048
OFF_NSA_Q = 3072
OFF_BGATE = 4096
OFF_IDX_Q = 7168
OFF_NSA_KV = 7680
OFF_MLA = 9216
OFF_IDX_K = 10240
OFF_IDX_W = 10368
OFF_NSA_GATE = 10496
N_PACK = 10752
TN_IN = 1536


def _cparams(sem):
    return pltpu.CompilerParams(dimension_semantics=sem, vmem_limit_bytes=VMEM_LIMIT)


def _rms(x, g):
    return x * lax.rsqrt(jnp.mean(x * x, axis=-1, keepdims=True) + EPS) * g


def _dot(a, b):
    return jnp.dot(a, b, preferred_element_type=F32)


def _dot_nt(a, b):
    return lax.dot_general(a, b, (((1,), (1,)), ((), ())), preferred_element_type=F32)


def _norm_mm_kernel(x_ref, g_ref, w_ref, o_ref, xn_ref):
    @pl.when(pl.program_id(1) == 0)
    def _():
        xn_ref[...] = _rms(x_ref[...].astype(F32), g_ref[...]).astype(BF16)

    o_ref[...] = _dot(xn_ref[...], w_ref[...]).astype(o_ref.dtype)


def norm_matmul(x, g, w, *, tm, tn):
    M, K = x.shape
    N = w.shape[1]
    return pl.pallas_call(
        _norm_mm_kernel,
        grid=(M // tm, N // tn),
        in_specs=[pl.BlockSpec((tm, K), lambda i, j: (i, 0)),
                  pl.BlockSpec((1, K), lambda i, j: (0, 0)),
                  pl.BlockSpec((K, tn), lambda i, j: (0, j))],
        out_specs=pl.BlockSpec((tm, tn), lambda i, j: (i, j)),
        out_shape=jax.ShapeDtypeStruct((M, N), BF16),
        scratch_shapes=[pltpu.VMEM((tm, K), BF16)],
        compiler_params=_cparams(("parallel", "arbitrary")),
        name="norm_matmul",
    )(x, g, w)


def _bias_tiles_kernel(tab_ref, bkt_ref, o_ref, *, inv_scale, h0):
    h = pl.program_id(0) + h0
    bkt = bkt_ref[...]
    far = tab_ref[REL_BUCKETS - 1, h]
    out = jnp.zeros(bkt.shape, F32)
    for b in range(REL_BUCKETS - 1):
        out = jnp.where(bkt == b, (tab_ref[b, h] - far) * inv_scale, out)
    o_ref[...] = out


def _bucket_tiles(tq):
    i = np.arange(tq)[:, None]
    j = np.arange(tq)[None, :]
    d = np.stack([i - j, tq + i - j]).astype(np.int32)
    n = jnp.maximum(jnp.asarray(d), 0)
    exact = REL_BUCKETS // 2
    nf = jnp.maximum(n, 1).astype(F32)
    log_b = exact + (jnp.log(nf / exact) / math.log(REL_MAX_DIST / exact) * (REL_BUCKETS - exact)).astype(jnp.int32)
    return jnp.where(n < exact, n, jnp.minimum(log_b, REL_BUCKETS - 1)).astype(jnp.int32)


def bias_tiles(rel_bias, tq, h0, nh):
    assert tq >= REL_MAX_DIST
    bkt = _bucket_tiles(tq)
    assert DSA_HEAD_DIM == NSA_HEAD_DIM
    return pl.pallas_call(
        functools.partial(_bias_tiles_kernel, inv_scale=DSA_HEAD_DIM ** 0.5, h0=h0),
        grid=(nh,),
        in_specs=[pl.BlockSpec(memory_space=pltpu.SMEM),
                  pl.BlockSpec((2, tq, tq), lambda h: (0, 0, 0))],
        out_specs=pl.BlockSpec((None, 2, tq, tq), lambda h: (h, 0, 0, 0)),
        out_shape=jax.ShapeDtypeStruct((nh, 2, tq, tq), F32),
        compiler_params=_cparams(("arbitrary",)),
        name="bias_tiles",
    )(rel_bias, bkt)


def _mla_prep_kernel(lat_ref, pos_ref, freq_ref, qg_ref, kvg_ref, wqn_ref, wqr_ref, wqrr_ref, wkv_ref,
                     qn_ref, qr_ref, kn_ref, v_ref, kr_ref):
    lat = lat_ref[...].astype(F32)
    qlat = _rms(lat[:, :MLA_Q_LORA], qg_ref[...]).astype(BF16)
    kvlat = _rms(lat[:, MLA_Q_LORA:MLA_Q_LORA + MLA_KV_LORA], kvg_ref[...]).astype(BF16)
    o = MLA_Q_LORA + MLA_KV_LORA
    kr = lat[:, o:o + LANE]
    krr = lat[:, o + LANE:o + 2 * LANE]
    ang = pos_ref[...].astype(F32) * freq_ref[...]
    cos, sin = jnp.cos(ang), jnp.sin(ang)
    cos8 = jnp.concatenate([cos] * MLA_HEADS, axis=1)
    sin8 = jnp.concatenate([sin] * MLA_HEADS, axis=1)
    qn_ref[...] = _dot(qlat, wqn_ref[...]).astype(BF16)
    qr_ref[...] = (_dot(qlat, wqr_ref[...]) * cos8 + _dot(qlat, wqrr_ref[...]) * sin8).astype(BF16)
    kv = _dot(kvlat, wkv_ref[...])
    nk = MLA_HEADS * MLA_NOPE
    kn_ref[...] = kv[:, :nk].astype(BF16)
    v_ref[...] = kv[:, nk:].astype(BF16)
    kr_ref[...] = (kr * cos + krr * sin).astype(BF16)


def mla_prep(proj, pos, freq, qg, kvg, wqn, wqr, wqrr, wkv, *, tm):
    M = proj.shape[0]
    HD = MLA_HEADS * LANE
    full = lambda a: pl.BlockSpec(a.shape, lambda i: (0, 0))
    outs = [jax.ShapeDtypeStruct((M, HD), BF16)] * 4 + [jax.ShapeDtypeStruct((M, LANE), BF16)]
    return pl.pallas_call(
        _mla_prep_kernel,
        grid=(M // tm,),
        in_specs=[pl.BlockSpec((tm, 1024), lambda i: (i, OFF_MLA // 1024)),
                  pl.BlockSpec((tm, 1), lambda i: (i, 0)),
                  full(freq), full(qg), full(kvg), full(wqn), full(wqr), full(wqrr), full(wkv)],
        out_specs=[pl.BlockSpec((tm, HD), lambda i: (i, 0))] * 4 + [pl.BlockSpec((tm, LANE), lambda i: (i, 0))],
        out_shape=outs,
        compiler_params=_cparams(("parallel",)),
        name="mla_prep",
    )(proj, pos, freq, qg, kvg, wqn, wqr, wqrr, wkv)


def _softmax_init(m_ref, acc_ref):
    m_ref[...] = jnp.full(m_ref.shape, NEG, F32)
    acc_ref[...] = jnp.zeros(acc_ref.shape, F32)


def _with_ones(v):
    return jnp.concatenate([v, jnp.ones(v.shape, v.dtype)], axis=1)


def _softmax_step(s, v, m_ref, acc_ref, scale):
    c = scale * math.log2(math.e)
    m_prev = m_ref[...]
    m_new = jnp.maximum(m_prev, jnp.max(s, axis=-1, keepdims=True))
    alpha = jnp.exp2((m_prev - m_new) * c)
    p = jnp.exp2((s - jnp.tile(m_new, (1, s.shape[1] // LANE))) * c)
    acc_ref[...] = jnp.tile(alpha, (1, 2)) * acc_ref[...] + _dot(p.astype(BF16), _with_ones(v))
    m_ref[...] = m_new


def _softmax_finish(acc_ref):
    acc = acc_ref[...]
    return acc[:, :LANE] / acc[:, LANE:]


def _tri(tq, tk):
    row = lax.broadcasted_iota(jnp.int32, (tq, tk), 0)
    col = lax.broadcasted_iota(jnp.int32, (tq, tk), 1)
    return row, col


def _mla_attn_kernel(qn_ref, qr_ref, kn_ref, kr_ref, v_ref, o_ref, m_ref, acc_ref, *, tq, scale):
    qi = pl.program_id(2)
    hs = range(HPS)
    q = [jnp.concatenate([qn_ref[:, h * LANE:(h + 1) * LANE], qr_ref[:, h * LANE:(h + 1) * LANE]], axis=-1)
         for h in hs]
    for h in hs:
        _softmax_init(m_ref.at[h], acc_ref.at[h])

    def scores(j, width=tq):
        sl = pl.ds(pl.multiple_of(j * tq, tq), width)
        kr = kr_ref[sl, :]
        return [_dot_nt(q[h], jnp.concatenate([kn_ref[sl, h * LANE:(h + 1) * LANE], kr], axis=-1)) for h in hs], sl

    def far(j, width):
        s, sl = scores(j, width)
        for h in hs:
            _softmax_step(s[h], v_ref[sl, h * LANE:(h + 1) * LANE], m_ref.at[h], acc_ref.at[h], scale)

    def body(p, c):
        far(2 * p, 2 * tq)
        return c

    lax.fori_loop(0, qi // 2, body, 0)

    @pl.when(qi % 2 == 1)
    def _():
        far(qi - 1, tq)

    s, sl = scores(qi)
    row, col = _tri(tq, tq)
    for h in hs:
        _softmax_step(jnp.where(row >= col, s[h], NEG), v_ref[sl, h * LANE:(h + 1) * LANE],
                      m_ref.at[h], acc_ref.at[h], scale)
    for h in hs:
        o_ref[:, h * LANE:(h + 1) * LANE] = _softmax_finish(acc_ref.at[h]).astype(o_ref.dtype)


def mla_attention(qn, qr, kn, kr, v, *, B, S, tq):
    M = B * S
    nq = S // tq
    H = MLA_HEADS
    scale = (MLA_NOPE + MLA_ROPE) ** -0.5
    w = HPS * LANE
    qspec = pl.BlockSpec((tq, w), lambda b, h, i: (b * nq + i, h))
    kspec = pl.BlockSpec((S, w), lambda b, h, i: (b, h))
    return pl.pallas_call(
        functools.partial(_mla_attn_kernel, tq=tq, scale=scale),
        grid=(B, H // HPS, nq),
        in_specs=[qspec, qspec, kspec, pl.BlockSpec((S, LANE), lambda b, h, i: (b, 0)), kspec],
        out_specs=qspec,
        out_shape=jax.ShapeDtypeStruct((M, H * MLA_V), BF16),
        scratch_shapes=[pltpu.VMEM((HPS, tq, LANE), F32), pltpu.VMEM((HPS, tq, 2 * LANE), F32)],
        compiler_params=_cparams(("parallel", "parallel", "arbitrary")),
        name="mla_attention",
    )(qn, qr, kn, kr, v)


def _dsa_index_kernel(iq_ref, ik_ref, iw_ref, tri_ref, mb_ref, sc_ref, *, tq, tk, S, topk, iters):
    t0 = pl.program_id(1) * tq
    nk = S // tk
    nch = (t0 + tq - 1) // tk + 1
    hw = IDX_CHUNK
    nb = tk // LANE
    iq = iq_ref[...]
    iw = iw_ref[...].astype(F32)
    lane = lax.broadcasted_iota(jnp.int32, (tq, LANE), 1)
    qs = []
    for h in range(IDX_HEADS):
        blk = iq[:, (h // 2) * LANE:(h // 2 + 1) * LANE]
        keep = (lane >= IDX_DIM) if h % 2 else (lane < IDX_DIM)
        qs.append(jnp.where(keep, blk, jnp.zeros_like(blk)))
    q8 = jnp.concatenate(qs, axis=0)
    wcol = [iw[:, h:h + 1] for h in range(IDX_HEADS)]
    row = lax.broadcasted_iota(jnp.int32, (tq, hw), 0) + t0
    col0 = lax.broadcasted_iota(jnp.int32, (tq, hw), 1)

    def lane_fold(x, op):
        out = x[:, 0:LANE]
        for b in range(1, x.shape[1] // LANE):
            out = op(out, x[:, b * LANE:(b + 1) * LANE])
        return out

    def score_chunk(c, carry):
        mn, mx = carry
        for half in range(tk // hw):
            k0 = pl.multiple_of(c * tk + half * hw, hw)
            res = _dot_nt(q8, ik_ref[pl.ds(k0, hw), :])
            sc = wcol[0] * jnp.maximum(res[0:tq], 0.0)
            for h in range(1, IDX_HEADS):
                sc = sc + wcol[h] * jnp.maximum(res[h * tq:(h + 1) * tq], 0.0)
            causal = (col0 + k0) <= row
            mn = jnp.minimum(mn, lane_fold(jnp.where(causal, sc, -NEG), jnp.minimum))
            sc = jnp.where(causal, sc, NEG)
            mx = jnp.maximum(mx, lane_fold(sc, jnp.maximum))
            sc_ref[c, :, half * hw:(half + 1) * hw] = sc
        return mn, mx

    mn, mx = lax.fori_loop(0, nch, score_chunk,
                           (jnp.full((tq, LANE), -NEG, F32), jnp.full((tq, LANE), NEG, F32)))
    n_causal = (lax.broadcasted_iota(jnp.int32, (tq, 1), 0) + (t0 + 1)).astype(F32)
    kf = jnp.minimum(float(topk), n_causal)

    rs = IDX_ROWS

    def blocks(j):
        for r in range(tq // rs):
            for b in range(nb):
                rows = slice(r * rs, (r + 1) * rs)
                yield rows, (j, rows, slice(b * LANE, (b + 1) * LANE)), j * tk + b * LANE

    def count(pred):
        def body(c, acc):
            parts = [acc[r * rs:(r + 1) * rs] for r in range(tq // rs)]
            for rows, idx, off in blocks(c):
                r = rows.start // rs
                parts[r] = parts[r] + jnp.where(pred(rows, sc_ref[idx], off), 1.0, 0.0)
            return jnp.concatenate(parts, axis=0)

        acc = lax.fori_loop(0, nch, body, jnp.zeros((tq, LANE), F32))
        return jnp.sum(acc, axis=-1, keepdims=True)

    def bcast(x):
        strips = [jnp.broadcast_to(x[r * rs:(r + 1) * rs], (rs, LANE)) for r in range(tq // rs)]
        return lambda rows: strips[rows.start // rs]

    lo0 = jnp.min(mn, axis=-1, keepdims=True)
    mx = jnp.max(mx, axis=-1, keepdims=True)
    hi0 = mx + (jnp.abs(mx) * 1e-6 + 1.0)

    def bisect(c):
        lo, hi, n_lo, n_hi = c
        mid = 0.5 * (lo + hi)
        mid_b = bcast(mid)
        n_mid = count(lambda rows, blk, off: blk >= mid_b(rows))
        ge = n_mid >= kf
        return (jnp.where(ge, mid, lo), jnp.where(ge, hi, mid),
                jnp.where(ge, n_mid, n_lo), jnp.where(ge, n_hi, n_mid))

    def open_rows(n_lo, n_hi, settled):
        return (n_lo - kf) * jnp.where(n_lo - n_hi > 2.0, 1.0, 0.0) * (1.0 - settled)

    def search(state, limit, settled):
        def cond(c):
            return jnp.logical_and(c[0] < limit, jnp.max(open_rows(c[3], c[4], settled)) > 0.0)

        def body(c):
            return (c[0] + 2,) + bisect(bisect(c[1:]))

        return lax.while_loop(cond, body, state)

    def band_min_max(lo, hi):
        lo_b, hi_b = bcast(lo), bcast(hi)

        def body(c, carry):
            bmin = [carry[0][r * rs:(r + 1) * rs] for r in range(tq // rs)]
            bmax = [carry[1][r * rs:(r + 1) * rs] for r in range(tq // rs)]
            for rows, idx, off in blocks(c):
                r = rows.start // rs
                blk = sc_ref[idx]
                inside = (blk >= lo_b(rows)) & (blk < hi_b(rows))
                bmin[r] = jnp.minimum(bmin[r], jnp.where(inside, blk, -NEG))
                bmax[r] = jnp.maximum(bmax[r], jnp.where(inside, blk, NEG))
            return jnp.concatenate(bmin, axis=0), jnp.concatenate(bmax, axis=0)

        bmin, bmax = lax.fori_loop(0, nch, body,
                                   (jnp.full((tq, LANE), -NEG, F32), jnp.full((tq, LANE), NEG, F32)))
        return jnp.min(bmin, axis=-1, keepdims=True), jnp.max(bmax, axis=-1, keepdims=True)

    def finish(state):
        _, lo, hi, n_lo, n_hi = state
        over = n_lo > kf
        bmin, bmax = lax.cond(jnp.max(n_lo - kf) > 0.0, lambda: band_min_max(lo, hi),
                              lambda: (lo, lo))
        single = bmin == bmax
        split = over & (n_lo - n_hi == 2.0) & jnp.logical_not(single)
        unsplit = over & jnp.logical_not(split)
        return jnp.where(split, bmax, lo), jnp.where(over & single, 1.0, 0.0), jnp.where(unsplit, 1.0, 0.0)

    zero = jnp.zeros((tq, 1), F32)
    state1 = search((jnp.int32(0), lo0, hi0, n_causal, zero), IDX_TIE_CHECK, zero)
    first = finish(state1)
    state = search(state1, iters, first[1])
    lo, _, unsplit = lax.cond(state[0] > state1[0], lambda: finish(state), lambda: first)
    hi, n_hi = state[2], state[4]
    tied = jnp.max(unsplit) > 0.0

    def write(sel):
        def body(j, c):
            for rows, idx, off in blocks(j):
                mb_ref[idx] = jnp.where(sel(rows, sc_ref[idx], off), 0.0, NEG).astype(mb_ref.dtype)
            return c

        lax.fori_loop(0, nch, body, 0)

    def fill(j, c):
        mb_ref[j] = jnp.full((tq, tk), NEG, mb_ref.dtype)
        return c

    lax.fori_loop(nch, nk, fill, 0)
    lo_b = bcast(lo)

    @pl.when(jnp.logical_not(tied))
    def _():
        write(lambda rows, blk, off: blk >= lo_b(rows))

    @pl.when(tied)
    def _():
        need = kf - n_hi
        tri = tri_ref[...]

        def body(j, base):
            sc = sc_ref[j]
            above = sc >= hi
            band = (sc >= lo) & jnp.logical_not(above)
            prefix = _dot(jnp.where(band, 1.0, 0.0).astype(BF16), tri) + base
            keep = above | (band & (prefix <= need))
            mb_ref[j] = jnp.where(keep, 0.0, NEG).astype(mb_ref.dtype)
            return prefix[:, tk - 1:tk]

        lax.fori_loop(0, nch, body, zero)


def dsa_index_mask(proj, *, B, S, tqi, tk, topk):
    nq = S // tqi
    nk = S // tk
    assert tk % tqi == 0 and tk % IDX_CHUNK == 0
    tri = jnp.asarray(np.triu(np.ones((tk, tk), np.float32)), BF16)
    return pl.pallas_call(
        functools.partial(_dsa_index_kernel, tq=tqi, tk=tk, S=S, topk=topk, iters=BISECT_ITERS),
        grid=(B, nq),
        in_specs=[pl.BlockSpec((tqi, 512), lambda b, i: (b * nq + i, OFF_IDX_Q // 512)),
                  pl.BlockSpec((S, LANE), lambda b, i: (b, OFF_IDX_K // LANE)),
                  pl.BlockSpec((tqi, LANE), lambda b, i: (b * nq + i, OFF_IDX_W // LANE)),
                  pl.BlockSpec((tk, tk), lambda b, i: (0, 0))],
        out_specs=pl.BlockSpec((None, nk, tqi, tk), lambda b, i: (b, 0, i, 0)),
        out_shape=jax.ShapeDtypeStruct((B, nk, S, tk), BF16),
        scratch_shapes=[pltpu.VMEM((nk, tqi, tk), F32)],
        compiler_params=_cparams(("parallel", "parallel")),
        name="dsa_index_mask",
    )(proj, proj, proj, tri)


def _dsa_attn_kernel(q_ref, k_ref, v_ref, mb_ref, bias_ref, o_ref, m_ref, acc_ref, *, tq, scale):
    qi = pl.program_id(2)
    hs = range(HPS)
    cols = lambda h: slice(h * LANE, (h + 1) * LANE)
    q = [q_ref[:, cols(h)] for h in hs]
    for h in hs:
        _softmax_init(m_ref.at[h], acc_ref.at[h])

    def step(j, nc, bias):
        sl = pl.ds(j * tq if isinstance(j, int) else pl.multiple_of(j * tq, tq), nc * tq)
        mb = mb_ref[j] if nc == 1 else jnp.concatenate([mb_ref[j], mb_ref[j + 1]], axis=1)
        mb = mb.astype(F32)
        s = [_dot_nt(q[h], k_ref[sl, cols(h)]) + mb for h in hs]
        for h in hs:
            sh = s[h] if bias is None else s[h] + bias(h)
            _softmax_step(sh, v_ref[sl, cols(h)], m_ref.at[h], acc_ref.at[h], scale)

    n_far = jnp.maximum(qi - 1, 0)

    def body(p, c):
        step(2 * p, 2, None)
        return c

    lax.fori_loop(0, n_far // 2, body, 0)

    @pl.when(n_far % 2 == 1)
    def _():
        step(n_far - 1, 1, None)

    @pl.when(qi == 0)
    def _():
        step(0, 1, lambda h: bias_ref[h, 0])

    @pl.when(qi >= 1)
    def _():
        step(qi - 1, 2, lambda h: jnp.concatenate([bias_ref[h, 1], bias_ref[h, 0]], axis=1))

    for h in hs:
        o_ref[:, cols(h)] = _softmax_finish(acc_ref.at[h]).astype(o_ref.dtype)


def dsa_attention(proj, maskb, btiles, *, B, S, tq):
    M = B * S
    nq = S // tq
    H = DSA_HEADS
    scale = DSA_HEAD_DIM ** -0.5
    w = HPS * LANE
    return pl.pallas_call(
        functools.partial(_dsa_attn_kernel, tq=tq, scale=scale),
        grid=(B, H // HPS, nq),
        in_specs=[pl.BlockSpec((tq, w), lambda b, h, i: (b * nq + i, OFF_DSA_Q // w + h)),
                  pl.BlockSpec((S, w), lambda b, h, i: (b, OFF_DSA_K // w + h)),
                  pl.BlockSpec((S, w), lambda b, h, i: (b, OFF_DSA_V // w + h)),
                  pl.BlockSpec((None, nq, tq, tq), lambda b, h, i: (b, 0, i, 0)),
                  pl.BlockSpec((HPS, 2, tq, tq), lambda b, h, i: (h, 0, 0, 0))],
        out_specs=pl.BlockSpec((tq, w), lambda b, h, i: (b * nq + i, h)),
        out_shape=jax.ShapeDtypeStruct((M, H * DSA_HEAD_DIM), BF16),
        scratch_shapes=[pltpu.VMEM((HPS, tq, LANE), F32), pltpu.VMEM((HPS, tq, 2 * LANE), F32)],
        compiler_params=_cparams(("parallel", "parallel", "arbitrary")),
        name="dsa_attention",
    )(proj, proj, proj, maskb, btiles)


def _nsa_compress_kernel(x_ref, pos_ref, w1_ref, w2_ref, o_ref, *, ncp):
    dk = NSA_HEAD_DIM
    posw = _dot(pos_ref[...], w1_ref[...])[0:1, :]
    for g in range(NSA_GROUPS):
        lo = jnp.zeros((ncp, CMP_HIDDEN), F32)
        hi = jnp.zeros((ncp, CMP_HIDDEN), F32)
        for l in range(CMP_STRIDE):
            xs = x_ref[:, (l * NSA_GROUPS + g) * dk:(l * NSA_GROUPS + g + 1) * dk]
            lo = lo + _dot(xs, w1_ref[l * dk:(l + 1) * dk, :])
            hi = hi + _dot(xs, w1_ref[(CMP_STRIDE + l) * dk:(CMP_STRIDE + l + 1) * dk, :])
        hid = lo + pltpu.roll(hi, ncp - 1, 0) + posw
        o_ref[g] = _dot(jax.nn.gelu(hid).astype(BF16), w2_ref[...]).astype(o_ref.dtype)


def nsa_compress(xkv, posflat, w1, w2, *, B, S):
    ncp = S // CMP_STRIDE
    G, dk = NSA_GROUPS, NSA_HEAD_DIM
    width = CMP_STRIDE * G * dk
    return pl.pallas_call(
        functools.partial(_nsa_compress_kernel, ncp=ncp),
        grid=(2, B),
        in_specs=[pl.BlockSpec((None, ncp, width), lambda a, b: (a, b, 0)),
                  pl.BlockSpec((None, 8, CMP_LEN * dk), lambda a, b: (a, 0, 0)),
                  pl.BlockSpec((None, CMP_LEN * dk, CMP_HIDDEN), lambda a, b: (a, 0, 0)),
                  pl.BlockSpec((None, CMP_HIDDEN, dk), lambda a, b: (a, 0, 0))],
        out_specs=pl.BlockSpec((None, None, G, ncp, dk), lambda a, b: (a, b, 0, 0, 0)),
        out_shape=jax.ShapeDtypeStruct((2, B, G, ncp, dk), BF16),
        compiler_params=_cparams(("parallel", "parallel")),
        name="nsa_compress",
    )(xkv, posflat, w1, w2)


def _nsa_cmp_kernel(q_ref, kc_ref, vc_ref, ov_ref, oc_ref, sel_ref, *, tq, ncp, ns, n_sel, scale):
    t0 = pl.program_id(2) * tq
    dk = NSA_HEAD_DIM
    kc = kc_ref[...]
    vc = vc_ref[...]
    trow = lax.broadcasted_iota(jnp.int32, (tq, ncp), 0) + t0
    ccol = lax.broadcasted_iota(jnp.int32, (tq, ncp), 1)
    vis = (ccol * CMP_STRIDE + (CMP_LEN - 1)) <= trow
    psum = jnp.zeros((tq, ncp), F32)
    for h in range(NSA_HG):
        s = jnp.where(vis, _dot_nt(q_ref[:, h * dk:(h + 1) * dk], kc) * scale, NEG)
        m = jnp.max(s, axis=-1, keepdims=True)
        p = jnp.where(vis, jnp.exp(s - m), 0.0)
        d = jnp.sum(p, axis=-1, keepdims=True)
        p = p / jnp.where(d > 0, d, 1.0)
        oc_ref[:, h * dk:(h + 1) * dk] = _dot(p.astype(BF16), vc).astype(oc_ref.dtype)
        psum = psum + p
    p_hi = psum.astype(BF16)
    p_lo = (psum - p_hi.astype(F32)).astype(BF16)
    ov = ov_ref[...]
    imp = _dot_nt(ov, p_hi) + _dot_nt(ov, p_lo)
    imp = imp[:ns]
    jrow = lax.broadcasted_iota(jnp.int32, (ns, tq), 0)
    tcol = lax.broadcasted_iota(jnp.int32, (ns, tq), 1) + t0
    blk_t = tcol // SEL_BLOCK
    forced = (jrow == 0) | (jrow == blk_t) | (jrow == blk_t - 1)
    val = jnp.where(forced, -NEG, imp)
    val = jnp.where(jrow * SEL_BLOCK > tcol, NEG, val)
    rank = jnp.zeros((ns, tq), F32)
    for j in range(ns):
        other = val[j:j + 1, :]
        ahead = (other > val) | ((other == val) & (jrow > j))
        rank = rank + jnp.where(ahead, 1.0, 0.0)
    selneg = jnp.where(rank < float(n_sel), 0.0, NEG)
    if ns < LANE:
        selneg = jnp.concatenate([selneg, jnp.zeros((LANE - ns, tq), F32)], axis=0)
    sel_ref[...] = selneg.T.astype(sel_ref.dtype)


def nsa_cmp_attention(proj, kvc, ovt, *, B, S, tq):
    M = B * S
    nq = S // tq
    G, dk = NSA_GROUPS, NSA_HEAD_DIM
    ncp = S // CMP_STRIDE
    ns = S // SEL_BLOCK
    assert ns <= LANE
    n_sel = min(SEL_COUNT, ns)
    gw = NSA_HG * dk
    return pl.pallas_call(
        functools.partial(_nsa_cmp_kernel, tq=tq, ncp=ncp, ns=ns, n_sel=n_sel, scale=dk ** -0.5),
        grid=(B, G, nq),
        in_specs=[pl.BlockSpec((tq, gw), lambda b, g, i: (b * nq + i, OFF_NSA_Q // gw + g)),
                  pl.BlockSpec((None, None, None, ncp, dk), lambda b, g, i: (0, b, g, 0, 0)),
                  pl.BlockSpec((None, None, None, ncp, dk), lambda b, g, i: (1, b, g, 0, 0)),
                  pl.BlockSpec((LANE, ncp), lambda b, g, i: (0, 0))],
        out_specs=[pl.BlockSpec((tq, gw), lambda b, g, i: (b * nq + i, g)),
                   pl.BlockSpec((None, None, tq, LANE), lambda b, g, i: (b, g, i, 0))],
        out_shape=[jax.ShapeDtypeStruct((M, G * gw), BF16),
                   jax.ShapeDtypeStruct((B, G, S, LANE), BF16)],
        compiler_params=_cparams(("parallel", "parallel", "parallel")),
        name="nsa_cmp_attention",
    )(proj, kvc, kvc, ovt)


def _nsa_main_kernel(q_ref, sel_ref, ks_ref, vs_ref, kw_ref, vw_ref, ind_ref, bias_ref, gate_ref, oc_ref,
                     o_ref, m_ref, acc_ref, *, tq, nback, scale):
    g = pl.program_id(1)
    qi = pl.program_id(2)
    dk = NSA_HEAD_DIM
    HG = NSA_HG
    ns = NSA_STREAMS
    hp = HG // ns
    st = range(ns)
    stack = lambda f: [jnp.concatenate([f(a * hp + i) for i in range(hp)], axis=0) for a in st]
    q = stack(lambda h: q_ref[:, h * dk:(h + 1) * dk])
    sel = jnp.concatenate([sel_ref[...]] * hp, axis=0)
    qa = [jnp.concatenate([q[a], sel], axis=1) for a in st]
    bias0 = stack(lambda h: bias_ref[h, 0])
    bias1 = stack(lambda h: bias_ref[h, 1])
    gate = jax.nn.sigmoid(gate_ref[...].astype(F32))

    def init():
        for a in st:
            _softmax_init(m_ref.at[a], acc_ref.at[a])

    def steps(s, v, fix):
        for a in st:
            _softmax_step(fix(a, s[a]), v, m_ref.at[a], acc_ref.at[a], scale)

    def near_step(scores, first, nc, window):
        s, v = scores(first, nc * tq)
        r = jnp.concatenate([lax.broadcasted_iota(jnp.int32, (tq, nc * tq), 0)] * hp, axis=0)
        c = lax.broadcasted_iota(jnp.int32, (hp * tq, nc * tq), 1)
        ok = c <= r + (nc - 1) * tq
        if window and nc * tq > WINDOW:
            ok = ok & (c > r + (nc - 1) * tq - WINDOW)

        def fix(a, x):
            pads = [jnp.zeros((hp * tq, (nc - 2) * tq), F32)] if nc > 2 else []
            bias = bias0[a] if nc == 1 else jnp.concatenate(pads + [bias1[a], bias0[a]], axis=1)
            return jnp.where(ok, x + bias, NEG)

        steps(s, v, fix)

    def near(scores, nspan, window):
        for nc in range(1, nspan):
            @pl.when(qi == nc - 1)
            def _(nc=nc):
                near_step(scores, 0, nc, window)

        @pl.when(qi >= nspan - 1)
        def _():
            near_step(scores, qi - (nspan - 1), nspan, window)

    init()

    def rows_from(j, width):
        return pl.ds(j * tq if isinstance(j, int) else pl.multiple_of(j * tq, tq), width)

    def sel_scores(j, width=tq):
        sl = rows_from(j, width)
        ka = jnp.concatenate([ks_ref[sl, :], ind_ref[sl, :]], axis=1)
        return [_dot_nt(qa[a], ka) for a in st], vs_ref[sl, :]

    n_far = jnp.maximum(qi - 1, 0)

    def sel_body(p, c):
        s, v = sel_scores(2 * p, 2 * tq)
        steps(s, v, lambda a, x: x)
        return c

    lax.fori_loop(0, n_far // 2, sel_body, 0)

    @pl.when(n_far % 2 == 1)
    def _():
        s, v = sel_scores(n_far - 1)
        steps(s, v, lambda a, x: x)

    near(sel_scores, 2, False)
    o_s = [_softmax_finish(acc_ref.at[a]) for a in st]

    init()

    def win_scores(j, width):
        sl = rows_from(j, width)
        kw = kw_ref[sl, :]
        return [_dot_nt(q[a], kw) for a in st], vw_ref[sl, :]

    near(win_scores, nback + 1, True)
    o_w = [_softmax_finish(acc_ref.at[a]) for a in st]

    lane = lax.broadcasted_iota(jnp.int32, gate.shape, 1)
    for h in range(HG):
        c = g * HG + h
        a, sl = h // hp, slice((h % hp) * tq, (h % hp + 1) * tq)
        gc = [jnp.sum(jnp.where(lane == br * NSA_HEADS + c, gate, 0.0), axis=-1, keepdims=True)
              for br in range(3)]
        o = gc[0] * oc_ref[:, h * dk:(h + 1) * dk].astype(F32) + gc[1] * o_s[a][sl] + gc[2] * o_w[a][sl]
        o_ref[:, h * dk:(h + 1) * dk] = o.astype(o_ref.dtype)


def nsa_main(proj, selneg, ind, btiles, o_c, *, B, S, tq):
    M = B * S
    nq = S // tq
    G, dk, HG = NSA_GROUPS, NSA_HEAD_DIM, NSA_HG
    gw = HG * dk
    assert WINDOW % tq == 0
    nback = WINDOW // tq
    kv = lambda n: pl.BlockSpec((S, dk), lambda b, g, i, n=n: (b, (OFF_NSA_KV + n * G * dk) // dk + g))
    return pl.pallas_call(
        functools.partial(_nsa_main_kernel, tq=tq, nback=nback, scale=dk ** -0.5),
        grid=(B, G, nq),
        in_specs=[pl.BlockSpec((tq, gw), lambda b, g, i: (b * nq + i, OFF_NSA_Q // gw + g)),
                  pl.BlockSpec((None, None, tq, LANE), lambda b, g, i: (b, g, i, 0)),
                  kv(2), kv(3), kv(4), kv(5),
                  pl.BlockSpec((S, LANE), lambda b, g, i: (0, 0)),
                  pl.BlockSpec((HG, 2, tq, tq), lambda b, g, i: (g, 0, 0, 0)),
                  pl.BlockSpec((tq, LANE), lambda b, g, i: (b * nq + i, OFF_NSA_GATE // LANE)),
                  pl.BlockSpec((tq, gw), lambda b, g, i: (b * nq + i, g))],
        out_specs=pl.BlockSpec((tq, gw), lambda b, g, i: (b * nq + i, g)),
        out_shape=jax.ShapeDtypeStruct((M, G * gw), BF16),
        scratch_shapes=[pltpu.VMEM((NSA_STREAMS, HG // NSA_STREAMS * tq, LANE), F32),
                        pltpu.VMEM((NSA_STREAMS, HG // NSA_STREAMS * tq, 2 * LANE), F32)],
        compiler_params=_cparams(("parallel", "parallel", "arbitrary")),
        name="nsa_main",
    )(proj, selneg, proj, proj, proj, proj, ind, btiles, proj, o_c)


def _merge_kernel(x_ref, oa_ref, ob_ref, oc_ref, g0_ref, g1_ref, g2_ref, wb_ref, wo_ref, gain_ref, o_ref):
    merged = None
    for o_r, g_r, br in ((oa_ref, g0_ref, 0), (ob_ref, g1_ref, 1), (oc_ref, g2_ref, 2)):
        t = jax.nn.sigmoid(g_r[...].astype(F32)) * _dot(o_r[...], wb_ref[br])
        merged = t if merged is None else merged + t
    y = _dot(merged.astype(BF16), wo_ref[...])
    o_ref[...] = x_ref[...] + _rms(y, gain_ref[...])


def merge_branches(x, o_a, o_b, o_c, proj, wb, wo, gain, *, tm):
    M, D = x.shape
    row = pl.BlockSpec((tm, D), lambda i: (i, 0))
    gate = lambda br: pl.BlockSpec((tm, D), lambda i, br=br: (i, OFF_BGATE // D + br))
    return pl.pallas_call(
        _merge_kernel,
        grid=(M // tm,),
        in_specs=[row, row, row, row, gate(0), gate(1), gate(2),
                  pl.BlockSpec(wb.shape, lambda i: (0, 0, 0)),
                  pl.BlockSpec(wo.shape, lambda i: (0, 0)),
                  pl.BlockSpec((1, D), lambda i: (0, 0))],
        out_specs=row,
        out_shape=jax.ShapeDtypeStruct((M, D), F32),
        compiler_params=_cparams(("parallel",)),
        name="merge_branches",
    )(x, o_a, o_b, o_c, proj, proj, proj, wb, wo, gain)


def _cross_kernel(x_ref, kv_ref, wq_ref, wo_ref, gpre_ref, gpost_ref, o_ref, *, scale):
    x = x_ref[...]
    h = _rms(x, gpre_ref[...]).astype(BF16)
    q = _dot(h, wq_ref[...]).astype(BF16)
    dh = CROSS_HEAD_DIM
    outs = []
    for hd in range(CROSS_HEADS):
        k = kv_ref[:, hd * dh:(hd + 1) * dh]
        v = kv_ref[:, D_MODEL + hd * dh:D_MODEL + (hd + 1) * dh]
        s = _dot_nt(q[:, hd * dh:(hd + 1) * dh], k) * scale
        p = jnp.exp(s - jnp.max(s, axis=-1, keepdims=True))
        p = p / jnp.sum(p, axis=-1, keepdims=True)
        outs.append(_dot(p.astype(BF16), v).astype(BF16))
    y = _dot(jnp.concatenate(outs, axis=1), wo_ref[...])
    o_ref[...] = x + _rms(y, gpost_ref[...])


def cross_attention(x, kv, wq, wo, gpre, gpost, *, S, tm):
    M, D = x.shape
    mlen = kv.shape[0] // (M // S)
    per_b = S // tm
    return pl.pallas_call(
        functools.partial(_cross_kernel, scale=CROSS_HEAD_DIM ** -0.5),
        grid=(M // tm,),
        in_specs=[pl.BlockSpec((tm, D), lambda i: (i, 0)),
                  pl.BlockSpec((mlen, 2 * D), lambda i: (i // per_b, 0)),
                  pl.BlockSpec(wq.shape, lambda i: (0, 0)),
                  pl.BlockSpec(wo.shape, lambda i: (0, 0)),
                  pl.BlockSpec((1, D), lambda i: (0, 0)),
                  pl.BlockSpec((1, D), lambda i: (0, 0))],
        out_specs=pl.BlockSpec((tm, D), lambda i: (i, 0)),
        out_shape=jax.ShapeDtypeStruct((M, D), F32),
        compiler_params=_cparams(("parallel",)),
        name="cross_attention",
    )(x, kv, wq, wo, gpre, gpost)


def _mlp_kernel(x_ref, w1_ref, w2_ref, gpre_ref, gpost_ref, o_ref, h_ref, acc_ref):
    j = pl.program_id(1)

    @pl.when(j == 0)
    def _():
        h_ref[...] = _rms(x_ref[...], gpre_ref[...]).astype(BF16)
        acc_ref[...] = jnp.zeros(acc_ref.shape, F32)

    a = jnp.maximum(_dot(h_ref[...], w1_ref[...]), 0.0)
    acc_ref[...] += _dot((a * a).astype(BF16), w2_ref[...])

    @pl.when(j == pl.num_programs(1) - 1)
    def _():
        o_ref[...] = x_ref[...] + _rms(acc_ref[...], gpost_ref[...])


def mlp(x, w1, w2, gpre, gpost, *, tm, tf):
    M, D = x.shape
    FF = w1.shape[1]
    return pl.pallas_call(
        _mlp_kernel,
        grid=(M // tm, FF // tf),
        in_specs=[pl.BlockSpec((tm, D), lambda i, j: (i, 0)),
                  pl.BlockSpec((D, tf), lambda i, j: (0, j)),
                  pl.BlockSpec((tf, D), lambda i, j: (j, 0)),
                  pl.BlockSpec((1, D), lambda i, j: (0, 0)),
                  pl.BlockSpec((1, D), lambda i, j: (0, 0))],
        out_specs=pl.BlockSpec((tm, D), lambda i, j: (i, 0)),
        out_shape=jax.ShapeDtypeStruct((M, D), F32),
        scratch_shapes=[pltpu.VMEM((tm, D), BF16), pltpu.VMEM((tm, D), F32)],
        compiler_params=_cparams(("parallel", "arbitrary")),
        name="mlp",
    )(x, w1, w2, gpre, gpost)


def _rot_half_cols(w):
    half = w.shape[-1] // 2
    return jnp.concatenate([-w[..., half:], w[..., :half]], axis=-1)


def _pack_w_in(w):
    widths = (MLA_Q_LORA, MLA_KV_LORA, MLA_ROPE, 1024, 1024, 1024, IDX_HEADS * IDX_DIM, IDX_DIM, IDX_HEADS,
              1024, 256, 256, 256, 256, 256, 256, 3 * NSA_HEADS, N_BRANCH * D_MODEL)
    offs = np.concatenate([[0], np.cumsum(widths)])
    seg = [w[:, offs[i]:offs[i + 1]] for i in range(len(widths))]
    (q_lat, kv_lat, k_rope, dsa_q, dsa_k, dsa_v, idx_q, idx_k, idx_w,
     nsa_q, kc, vc, ks, vs, kw, vw, nsa_gate, bgate) = seg
    z = lambda n: jnp.zeros((w.shape[0], n), w.dtype)
    cols = [dsa_q, dsa_k, dsa_v, nsa_q, bgate, idx_q, kc, vc, ks, vs, kw, vw,
            q_lat, kv_lat, k_rope, z(LANE - MLA_ROPE), _rot_half_cols(k_rope), z(LANE - MLA_ROPE), z(LANE),
            idx_k, idx_k, idx_w, z(LANE - IDX_HEADS), nsa_gate, z(LANE - 3 * NSA_HEADS)]
    out = jnp.concatenate(cols, axis=1)
    out = jnp.concatenate([out, z(N_PACK - out.shape[1])], axis=1)
    return out.astype(BF16)


def _pack_mla_weights(w_uq, w_ukv):
    H = MLA_HEADS
    uq = w_uq.reshape(MLA_Q_LORA, H, MLA_NOPE + MLA_ROPE)
    nope = uq[:, :, :MLA_NOPE].reshape(MLA_Q_LORA, H * MLA_NOPE)
    rope = uq[:, :, MLA_NOPE:]
    pad = jnp.zeros((MLA_Q_LORA, H, LANE - MLA_ROPE), w_uq.dtype)
    wqr = jnp.concatenate([rope, pad], axis=-1).reshape(MLA_Q_LORA, H * LANE)
    wqrr = jnp.concatenate([_rot_half_cols(rope), pad], axis=-1).reshape(MLA_Q_LORA, H * LANE)
    ukv = w_ukv.reshape(MLA_KV_LORA, H, MLA_NOPE + MLA_V)
    wkv = jnp.concatenate([ukv[:, :, :MLA_NOPE].reshape(MLA_KV_LORA, H * MLA_NOPE),
                           ukv[:, :, MLA_NOPE:].reshape(MLA_KV_LORA, H * MLA_V)], axis=1)
    return nope.astype(BF16), wqr.astype(BF16), wqrr.astype(BF16), wkv.astype(BF16)


def _constants(S):
    half = MLA_ROPE // 2
    inv_freq = ROPE_BASE ** (-np.arange(0, MLA_ROPE, 2, dtype=np.float32) / MLA_ROPE)
    freq = np.zeros((1, LANE), np.float32)
    freq[0, :half] = inv_freq
    freq[0, half:2 * half] = inv_freq
    ncp = S // CMP_STRIDE
    ns = S // SEL_BLOCK
    c_start = np.arange(ncp) * CMP_STRIDE
    j_start = np.arange(ns) * SEL_BLOCK
    overlap = ((c_start[None, :] < j_start[:, None] + SEL_BLOCK) &
               (c_start[None, :] + CMP_LEN > j_start[:, None])).astype(np.float32)
    overlap[:, ncp - 1] = 0.0
    ovt = np.zeros((LANE, ncp), np.float32)
    ovt[:ns] = overlap
    ind = np.zeros((S, LANE), np.float32)
    ind[np.arange(S), np.arange(S) // SEL_BLOCK] = 1.0
    return jnp.asarray(freq), jnp.asarray(ovt, BF16), jnp.asarray(ind, BF16)


def kernel(x, mem, positions, rel_bias, norm_gains, w_in, mla_q_norm, mla_kv_norm, mla_w_uq, mla_w_ukv,
           nsa_cmp_pos, nsa_cmp_w1, nsa_cmp_w2, w_branch, w_out, cross_wq, cross_wkv, cross_wo, mlp_w1, mlp_w2):
    B, S, D = x.shape
    M = B * S
    depth = w_in.shape[0]
    tq = TQ if S % TQ == 0 else TQ_NSA
    tqn = TQ_NSA
    assert S % tq == 0 and S % tqn == 0 and D == D_MODEL
    topk = min(DSA_TOPK_MAX, S // 4)
    freq, ovt, ind = _constants(S)
    btiles = bias_tiles(rel_bias, tq, 0, DSA_HEADS)
    btiles_n = bias_tiles(rel_bias, tqn, DSA_HEADS, NSA_HEADS)
    pos = positions.reshape(M, 1)
    xf = x.reshape(M, D)
    memf = mem.reshape(B * mem.shape[1], D)
    tm_big = 1024 if M % 1024 == 0 else tq
    tm = 512 if M % 512 == 0 else tq
    ncp = S // CMP_STRIDE
    row = lambda v: v.reshape(1, -1)

    for l in range(depth):
        g = norm_gains[l]
        proj = norm_matmul(xf, row(g[0]), _pack_w_in(w_in[l]), tm=tm_big, tn=TN_IN)

        wqn, wqr, wqrr, wkv = _pack_mla_weights(mla_w_uq[l], mla_w_ukv[l])
        qn, qr, kn, vv, kr = mla_prep(proj, pos, freq, row(mla_q_norm[l]), row(mla_kv_norm[l]),
                                      wqn, wqr, wqrr, wkv, tm=tm)
        o_a = mla_attention(qn, qr, kn, kr, vv, B=B, S=S, tq=tq)

        maskb = dsa_index_mask(proj, B=B, S=S, tqi=TQ_IDX, tk=tq, topk=topk)
        o_b = dsa_attention(proj, maskb, btiles, B=B, S=S, tq=tq)

        kvc_in = jnp.stack([
            proj[:, OFF_NSA_KV + a * 256:OFF_NSA_KV + (a + 1) * 256].reshape(B * ncp, CMP_STRIDE * 256)
            for a in range(2)])
        posflat = jnp.broadcast_to(nsa_cmp_pos[l].reshape(2, 1, CMP_LEN * NSA_HEAD_DIM),
                                   (2, 8, CMP_LEN * NSA_HEAD_DIM)).astype(BF16)
        kvc = nsa_compress(kvc_in, posflat, nsa_cmp_w1[l].astype(BF16), nsa_cmp_w2[l].astype(BF16), B=B, S=S)
        o_cmp, selneg = nsa_cmp_attention(proj, kvc, ovt, B=B, S=S, tq=tqn)
        o_c = nsa_main(proj, selneg, ind, btiles_n, o_cmp, B=B, S=S, tq=tqn)

        xf = merge_branches(xf, o_a, o_b, o_c, proj, w_branch[l].astype(BF16), w_out[l].astype(BF16),
                            row(g[1]), tm=tm)

        mkv = norm_matmul(memf, row(g[3]), cross_wkv[l].astype(BF16), tm=memf.shape[0] // B, tn=1024)
        xf = cross_attention(xf, mkv, cross_wq[l].astype(BF16), cross_wo[l].astype(BF16),
                             row(g[2]), row(g[4]), S=S, tm=tm)

        xf = mlp(xf, mlp_w1[l].astype(BF16), mlp_w2[l].astype(BF16), row(g[5]), row(g[6]), tm=tm, tf=1024)

    return xf.reshape(B, S, D)
```

```python
import functools
import math

import numpy as np
import jax
import jax.numpy as jnp
from jax import lax
from jax.experimental import pallas as pl
from jax.experimental.pallas import tpu as pltpu

F32 = jnp.float32
BF16 = jnp.bfloat16

D_MODEL = 1024
EPS = 1e-6
MLA_HEADS = 8
MLA_Q_LORA = 384
MLA_KV_LORA = 256
MLA_NOPE = 128
MLA_ROPE = 64
MLA_V = 128
ROPE_BASE = 10000.0
DSA_HEADS = 8
DSA_HEAD_DIM = 128
IDX_HEADS = 8
IDX_DIM = 64
DSA_TOPK_MAX = 256
NSA_HEADS = 8
NSA_GROUPS = 2
NSA_HG = NSA_HEADS // NSA_GROUPS
NSA_HEAD_DIM = 128
CMP_STRIDE = 16
CMP_LEN = 2 * CMP_STRIDE
CMP_HIDDEN = 256
SEL_BLOCK = 64
SEL_COUNT = 16
WINDOW = 512
REL_BUCKETS = 32
REL_MAX_DIST = 128
REL_HEADS = DSA_HEADS + NSA_HEADS
CROSS_HEADS = 4
CROSS_HEAD_DIM = D_MODEL // CROSS_HEADS
D_FF = 4 * D_MODEL
N_BRANCH = 3

LANE = 128
NEG = -1e30
VMEM_LIMIT = 48 * 1024 * 1024

TQ = 512
TQ_NSA = 256
HPS = 2
NSA_STREAMS = 2
BISECT_ITERS = 32
IDX_TIE_CHECK = 16
TQ_IDX = 512
IDX_CHUNK = 256
IDX_ROWS = 64

OFF_DSA_Q = 0
OFF_DSA_K = 1024
OFF_DSA_V = 2048
OFF_NSA_Q = 3072
OFF_BGATE = 4096
OFF_IDX_Q = 7168
OFF_NSA_KV = 7680
OFF_MLA = 9216
OFF_IDX_K = 10240
OFF_IDX_W = 10368
OFF_NSA_GATE = 10496
N_PACK = 10752
TN_IN = 1536


def _cparams(sem):
    return pltpu.CompilerParams(dimension_semantics=sem, vmem_limit_bytes=VMEM_LIMIT)


def _rms(x, g):
    return x * lax.rsqrt(jnp.mean(x * x, axis=-1, keepdims=True) + EPS) * g


def _dot(a, b):
    return jnp.dot(a, b, preferred_element_type=F32)


def _dot_nt(a, b):
    return lax.dot_general(a, b, (((1,), (1,)), ((), ())), preferred_element_type=F32)


def _norm_mm_kernel(x_ref, g_ref, w_ref, o_ref, xn_ref):
    @pl.when(pl.program_id(1) == 0)
    def _():
        xn_ref[...] = _rms(x_ref[...].astype(F32), g_ref[...]).astype(BF16)

    o_ref[...] = _dot(xn_ref[...], w_ref[...]).astype(o_ref.dtype)


def norm_matmul(x, g, w, *, tm, tn):
    M, K = x.shape
    N = w.shape[1]
    return pl.pallas_call(
        _norm_mm_kernel,
        grid=(M // tm, N // tn),
        in_specs=[pl.BlockSpec((tm, K), lambda i, j: (i, 0)),
                  pl.BlockSpec((1, K), lambda i, j: (0, 0)),
                  pl.BlockSpec((K, tn), lambda i, j: (0, j))],
        out_specs=pl.BlockSpec((tm, tn), lambda i, j: (i, j)),
        out_shape=jax.ShapeDtypeStruct((M, N), BF16),
        scratch_shapes=[pltpu.VMEM((tm, K), BF16)],
        compiler_params=_cparams(("parallel", "arbitrary")),
        name="norm_matmul",
    )(x, g, w)


def _bias_tiles_kernel(tab_ref, bkt_ref, o_ref, *, inv_scale, h0, tq, nspan, masked):
    h = pl.program_id(0) + h0
    bkt = bkt_ref[...]
    far = tab_ref[REL_BUCKETS - 1, h]
    out = jnp.zeros(bkt.shape, F32)
    for b in range(REL_BUCKETS - 1):
        out = jnp.where(bkt == b, (tab_ref[b, h] - far) * inv_scale, out)
    diag, prev = out[0], out[1]
    r, c = _tri(tq, tq)
    if masked:
        diag = jnp.where(r >= c, diag, NEG)
    o_ref[:, (nspan - 1) * tq:nspan * tq] = diag
    o_ref[:, (nspan - 2) * tq:(nspan - 1) * tq] = prev
    for k in range(nspan - 2):
        edge = jnp.where(r < c, 0.0, NEG) if (masked and k == 0) else jnp.zeros((tq, tq), F32)
        o_ref[:, k * tq:(k + 1) * tq] = edge


def _bucket_tiles(tq):
    i = np.arange(tq)[:, None]
    j = np.arange(tq)[None, :]
    d = np.stack([i - j, tq + i - j]).astype(np.int32)
    n = jnp.maximum(jnp.asarray(d), 0)
    exact = REL_BUCKETS // 2
    nf = jnp.maximum(n, 1).astype(F32)
    log_b = exact + (jnp.log(nf / exact) / math.log(REL_MAX_DIST / exact) * (REL_BUCKETS - exact)).astype(jnp.int32)
    return jnp.where(n < exact, n, jnp.minimum(log_b, REL_BUCKETS - 1)).astype(jnp.int32)


def bias_tiles(rel_bias, tq, h0, nh, *, nspan, masked, stack):
    assert tq >= REL_MAX_DIST
    bkt = _bucket_tiles(tq)
    assert DSA_HEAD_DIM == NSA_HEAD_DIM
    return pl.pallas_call(
        functools.partial(_bias_tiles_kernel, inv_scale=DSA_HEAD_DIM ** 0.5, h0=h0, tq=tq, nspan=nspan,
                          masked=masked),
        grid=(nh,),
        in_specs=[pl.BlockSpec(memory_space=pltpu.SMEM),
                  pl.BlockSpec((2, tq, tq), lambda h: (0, 0, 0))],
        out_specs=pl.BlockSpec((None, tq, nspan * tq), lambda h: (h // stack, h % stack, 0)),
        out_shape=jax.ShapeDtypeStruct((nh // stack, stack * tq, nspan * tq), F32),
        compiler_params=_cparams(("arbitrary",)),
        name="bias_tiles",
    )(rel_bias, bkt)


def _mla_prep_kernel(lat_ref, pos_ref, freq_ref, qg_ref, kvg_ref, wqn_ref, wqr_ref, wqrr_ref, wkv_ref,
                     qn_ref, qr_ref, kn_ref, v_ref, kr_ref):
    lat = lat_ref[...].astype(F32)
    qlat = _rms(lat[:, :MLA_Q_LORA], qg_ref[...]).astype(BF16)
    kvlat = _rms(lat[:, MLA_Q_LORA:MLA_Q_LORA + MLA_KV_LORA], kvg_ref[...]).astype(BF16)
    o = MLA_Q_LORA + MLA_KV_LORA
    kr = lat[:, o:o + LANE]
    krr = lat[:, o + LANE:o + 2 * LANE]
    ang = pos_ref[...].astype(F32) * freq_ref[...]
    cos, sin = jnp.cos(ang), jnp.sin(ang)
    cos8 = jnp.concatenate([cos] * MLA_HEADS, axis=1)
    sin8 = jnp.concatenate([sin] * MLA_HEADS, axis=1)
    qn_ref[...] = _dot(qlat, wqn_ref[...]).astype(BF16)
    qr_ref[...] = (_dot(qlat, wqr_ref[...]) * cos8 + _dot(qlat, wqrr_ref[...]) * sin8).astype(BF16)
    kv = _dot(kvlat, wkv_ref[...])
    nk = MLA_HEADS * MLA_NOPE
    kn_ref[...] = kv[:, :nk].astype(BF16)
    v_ref[...] = kv[:, nk:].astype(BF16)
    kr_ref[...] = (kr * cos + krr * sin).astype(BF16)


def mla_prep(proj, pos, freq, qg, kvg, wqn, wqr, wqrr, wkv, *, tm):
    M = proj.shape[0]
    HD = MLA_HEADS * LANE
    full = lambda a: pl.BlockSpec(a.shape, lambda i: (0, 0))
    outs = [jax.ShapeDtypeStruct((M, HD), BF16)] * 4 + [jax.ShapeDtypeStruct((M, LANE), BF16)]
    return pl.pallas_call(
        _mla_prep_kernel,
        grid=(M // tm,),
        in_specs=[pl.BlockSpec((tm, 1024), lambda i: (i, OFF_MLA // 1024)),
                  pl.BlockSpec((tm, 1), lambda i: (i, 0)),
                  full(freq), full(qg), full(kvg), full(wqn), full(wqr), full(wqrr), full(wkv)],
        out_specs=[pl.BlockSpec((tm, HD), lambda i: (i, 0))] * 4 + [pl.BlockSpec((tm, LANE), lambda i: (i, 0))],
        out_shape=outs,
        compiler_params=_cparams(("parallel",)),
        name="mla_prep",
    )(proj, pos, freq, qg, kvg, wqn, wqr, wqrr, wkv)


def _softmax_init(m_ref, acc_ref):
    m_ref[...] = jnp.full(m_ref.shape, NEG, F32)
    acc_ref[...] = jnp.zeros(acc_ref.shape, F32)


def _with_ones(v):
    return jnp.concatenate([v, jnp.ones(v.shape, v.dtype)], axis=1)


def _softmax_step(s, v, m_ref, acc_ref, scale):
    c = scale * math.log2(math.e)
    m_prev = m_ref[...]
    m_new = jnp.maximum(m_prev, jnp.max(s, axis=-1, keepdims=True))
    alpha = jnp.exp2((m_prev - m_new) * c)
    p = jnp.exp2((s - jnp.tile(m_new, (1, s.shape[1] // LANE))) * c)
    acc_ref[...] = jnp.tile(alpha, (1, 2)) * acc_ref[...] + _dot(p.astype(BF16), _with_ones(v))
    m_ref[...] = m_new


def _softmax_finish(acc_ref):
    acc = acc_ref[...]
    return acc[:, :LANE] / acc[:, LANE:]


def _tri(tq, tk):
    row = lax.broadcasted_iota(jnp.int32, (tq, tk), 0)
    col = lax.broadcasted_iota(jnp.int32, (tq, tk), 1)
    return row, col


def _mla_attn_kernel(qn_ref, qr_ref, kn_ref, kr_ref, v_ref, o_ref, m_ref, acc_ref, *, tq, scale):
    qi = pl.program_id(2)
    hs = range(HPS)
    q = [jnp.concatenate([qn_ref[:, h * LANE:(h + 1) * LANE], qr_ref[:, h * LANE:(h + 1) * LANE]], axis=-1)
         for h in hs]
    for h in hs:
        _softmax_init(m_ref.at[h], acc_ref.at[h])

    def scores(j, width=tq):
        sl = pl.ds(pl.multiple_of(j * tq, tq), width)
        kr = kr_ref[sl, :]
        return [_dot_nt(q[h], jnp.concatenate([kn_ref[sl, h * LANE:(h + 1) * LANE], kr], axis=-1)) for h in hs], sl

    def far(j, width):
        s, sl = scores(j, width)
        for h in hs:
            _softmax_step(s[h], v_ref[sl, h * LANE:(h + 1) * LANE], m_ref.at[h], acc_ref.at[h], scale)

    def body(p, c):
        far(2 * p, 2 * tq)
        return c

    lax.fori_loop(0, qi // 2, body, 0)

    @pl.when(qi % 2 == 1)
    def _():
        far(qi - 1, tq)

    s, sl = scores(qi)
    row, col = _tri(tq, tq)
    for h in hs:
        _softmax_step(jnp.where(row >= col, s[h], NEG), v_ref[sl, h * LANE:(h + 1) * LANE],
                      m_ref.at[h], acc_ref.at[h], scale)
    for h in hs:
        o_ref[:, h * LANE:(h + 1) * LANE] = _softmax_finish(acc_ref.at[h]).astype(o_ref.dtype)


def mla_attention(qn, qr, kn, kr, v, *, B, S, tq):
    M = B * S
    nq = S // tq
    H = MLA_HEADS
    scale = (MLA_NOPE + MLA_ROPE) ** -0.5
    w = HPS * LANE
    qspec = pl.BlockSpec((tq, w), lambda b, h, i: (b * nq + i, h))
    kspec = pl.BlockSpec((S, w), lambda b, h, i: (b, h))
    return pl.pallas_call(
        functools.partial(_mla_attn_kernel, tq=tq, scale=scale),
        grid=(B, H // HPS, nq),
        in_specs=[qspec, qspec, kspec, pl.BlockSpec((S, LANE), lambda b, h, i: (b, 0)), kspec],
        out_specs=qspec,
        out_shape=jax.ShapeDtypeStruct((M, H * MLA_V), BF16),
        scratch_shapes=[pltpu.VMEM((HPS, tq, LANE), F32), pltpu.VMEM((HPS, tq, 2 * LANE), F32)],
        compiler_params=_cparams(("parallel", "parallel", "arbitrary")),
        name="mla_attention",
    )(qn, qr, kn, kr, v)


def _dsa_index_kernel(iq_ref, ik_ref, iw_ref, tri_ref, mb_ref, sc_ref, *, tq, tk, S, topk, iters):
    t0 = pl.program_id(1) * tq
    nk = S // tk
    nch = (t0 + tq - 1) // tk + 1
    hw = IDX_CHUNK
    nb = tk // LANE
    iq = iq_ref[...]
    iw = iw_ref[...].astype(F32)
    lane = lax.broadcasted_iota(jnp.int32, (tq, LANE), 1)
    qs = []
    for h in range(IDX_HEADS):
        blk = iq[:, (h // 2) * LANE:(h // 2 + 1) * LANE]
        keep = (lane >= IDX_DIM) if h % 2 else (lane < IDX_DIM)
        qs.append(jnp.where(keep, blk, jnp.zeros_like(blk)))
    q8 = jnp.concatenate(qs, axis=0)
    wcol = [iw[:, h:h + 1] for h in range(IDX_HEADS)]
    row = lax.broadcasted_iota(jnp.int32, (tq, hw), 0) + t0
    col0 = lax.broadcasted_iota(jnp.int32, (tq, hw), 1)

    def lane_fold(x, op):
        out = x[:, 0:LANE]
        for b in range(1, x.shape[1] // LANE):
            out = op(out, x[:, b * LANE:(b + 1) * LANE])
        return out

    def score_chunk(c, carry):
        mn, mx = carry
        for half in range(tk // hw):
            k0 = pl.multiple_of(c * tk + half * hw, hw)
            res = _dot_nt(q8, ik_ref[pl.ds(k0, hw), :])
            sc = wcol[0] * jnp.maximum(res[0:tq], 0.0)
            for h in range(1, IDX_HEADS):
                sc = sc + wcol[h] * jnp.maximum(res[h * tq:(h + 1) * tq], 0.0)
            causal = (col0 + k0) <= row
            mn = jnp.minimum(mn, lane_fold(jnp.where(causal, sc, -NEG), jnp.minimum))
            sc = jnp.where(causal, sc, NEG)
            mx = jnp.maximum(mx, lane_fold(sc, jnp.maximum))
            sc_ref[c, :, half * hw:(half + 1) * hw] = sc
        return mn, mx

    mn, mx = lax.fori_loop(0, nch, score_chunk,
                           (jnp.full((tq, LANE), -NEG, F32), jnp.full((tq, LANE), NEG, F32)))
    n_causal = (lax.broadcasted_iota(jnp.int32, (tq, 1), 0) + (t0 + 1)).astype(F32)
    kf = jnp.minimum(float(topk), n_causal)

    rs = IDX_ROWS

    def blocks(j):
        for r in range(tq // rs):
            for b in range(nb):
                rows = slice(r * rs, (r + 1) * rs)
                yield rows, (j, rows, slice(b * LANE, (b + 1) * LANE)), j * tk + b * LANE

    def count(pred):
        def body(c, acc):
            parts = [acc[r * rs:(r + 1) * rs] for r in range(tq // rs)]
            for rows, idx, off in blocks(c):
                r = rows.start // rs
                parts[r] = parts[r] + jnp.where(pred(rows, sc_ref[idx], off), 1.0, 0.0)
            return jnp.concatenate(parts, axis=0)

        acc = lax.fori_loop(0, nch, body, jnp.zeros((tq, LANE), F32))
        return jnp.sum(acc, axis=-1, keepdims=True)

    def bcast(x):
        strips = [jnp.broadcast_to(x[r * rs:(r + 1) * rs], (rs, LANE)) for r in range(tq // rs)]
        return lambda rows: strips[rows.start // rs]

    lo0 = jnp.min(mn, axis=-1, keepdims=True)
    mx = jnp.max(mx, axis=-1, keepdims=True)
    hi0 = mx + (jnp.abs(mx) * 1e-6 + 1.0)

    def bisect(c):
        lo, hi, n_lo, n_hi = c
        mid = 0.5 * (lo + hi)
        mid_b = bcast(mid)
        n_mid = count(lambda rows, blk, off: blk >= mid_b(rows))
        ge = n_mid >= kf
        return (jnp.where(ge, mid, lo), jnp.where(ge, hi, mid),
                jnp.where(ge, n_mid, n_lo), jnp.where(ge, n_hi, n_mid))

    def open_rows(n_lo, n_hi, settled):
        return (n_lo - kf) * jnp.where(n_lo - n_hi > 2.0, 1.0, 0.0) * (1.0 - settled)

    def search(state, limit, settled):
        def cond(c):
            return jnp.logical_and(c[0] < limit, jnp.max(open_rows(c[3], c[4], settled)) > 0.0)

        def body(c):
            return (c[0] + 2,) + bisect(bisect(c[1:]))

        return lax.while_loop(cond, body, state)

    def band_min_max(lo, hi):
        lo_b, hi_b = bcast(lo), bcast(hi)

        def body(c, carry):
            bmin = [carry[0][r * rs:(r + 1) * rs] for r in range(tq // rs)]
            bmax = [carry[1][r * rs:(r + 1) * rs] for r in range(tq // rs)]
            for rows, idx, off in blocks(c):
                r = rows.start // rs
                blk = sc_ref[idx]
                inside = (blk >= lo_b(rows)) & (blk < hi_b(rows))
                bmin[r] = jnp.minimum(bmin[r], jnp.where(inside, blk, -NEG))
                bmax[r] = jnp.maximum(bmax[r], jnp.where(inside, blk, NEG))
            return jnp.concatenate(bmin, axis=0), jnp.concatenate(bmax, axis=0)

        bmin, bmax = lax.fori_loop(0, nch, body,
                                   (jnp.full((tq, LANE), -NEG, F32), jnp.full((tq, LANE), NEG, F32)))
        return jnp.min(bmin, axis=-1, keepdims=True), jnp.max(bmax, axis=-1, keepdims=True)

    def finish(state):
        _, lo, hi, n_lo, n_hi = state
        over = n_lo > kf
        bmin, bmax = lax.cond(jnp.max(n_lo - kf) > 0.0, lambda: band_min_max(lo, hi),
                              lambda: (lo, lo))
        single = bmin == bmax
        split = over & (n_lo - n_hi == 2.0) & jnp.logical_not(single)
        unsplit = over & jnp.logical_not(split)
        return jnp.where(split, bmax, lo), jnp.where(over & single, 1.0, 0.0), jnp.where(unsplit, 1.0, 0.0)

    zero = jnp.zeros((tq, 1), F32)
    state1 = search((jnp.int32(0), lo0, hi0, n_causal, zero), IDX_TIE_CHECK, zero)
    first = finish(state1)
    state = search(state1, iters, first[1])
    lo, _, unsplit = lax.cond(state[0] > state1[0], lambda: finish(state), lambda: first)
    hi, n_hi = state[2], state[4]
    tied = jnp.max(unsplit) > 0.0

    def write(sel):
        def body(j, c):
            for rows, idx, off in blocks(j):
                mb_ref[idx] = jnp.where(sel(rows, sc_ref[idx], off), 0.0, NEG).astype(mb_ref.dtype)
            return c

        lax.fori_loop(0, nch, body, 0)

    def fill(j, c):
        mb_ref[j] = jnp.full((tq, tk), NEG, mb_ref.dtype)
        return c

    lax.fori_loop(nch, nk, fill, 0)
    lo_b = bcast(lo)

    @pl.when(jnp.logical_not(tied))
    def _():
        write(lambda rows, blk, off: blk >= lo_b(rows))

    @pl.when(tied)
    def _():
        need = kf - n_hi
        tri = tri_ref[...]

        def body(j, base):
            sc = sc_ref[j]
            above = sc >= hi
            band = (sc >= lo) & jnp.logical_not(above)
            prefix = _dot(jnp.where(band, 1.0, 0.0).astype(BF16), tri) + base
            keep = above | (band & (prefix <= need))
            mb_ref[j] = jnp.where(keep, 0.0, NEG).astype(mb_ref.dtype)
            return prefix[:, tk - 1:tk]

        lax.fori_loop(0, nch, body, zero)


def dsa_index_mask(proj, *, B, S, tqi, tk, topk):
    nq = S // tqi
    nk = S // tk
    assert tk % tqi == 0 and tk % IDX_CHUNK == 0
    tri = jnp.asarray(np.triu(np.ones((tk, tk), np.float32)), BF16)
    return pl.pallas_call(
        functools.partial(_dsa_index_kernel, tq=tqi, tk=tk, S=S, topk=topk, iters=BISECT_ITERS),
        grid=(B, nq),
        in_specs=[pl.BlockSpec((tqi, 512), lambda b, i: (b * nq + i, OFF_IDX_Q // 512)),
                  pl.BlockSpec((S, LANE), lambda b, i: (b, OFF_IDX_K // LANE)),
                  pl.BlockSpec((tqi, LANE), lambda b, i: (b * nq + i, OFF_IDX_W // LANE)),
                  pl.BlockSpec((tk, tk), lambda b, i: (0, 0))],
        out_specs=pl.BlockSpec((None, nk, tqi, tk), lambda b, i: (b, 0, i, 0)),
        out_shape=jax.ShapeDtypeStruct((B, nk, S, tk), BF16),
        scratch_shapes=[pltpu.VMEM((nk, tqi, tk), F32)],
        compiler_params=_cparams(("parallel", "parallel")),
        name="dsa_index_mask",
    )(proj, proj, proj, tri)


def _dsa_attn_kernel(q_ref, k_ref, v_ref, mb_ref, bias_ref, o_ref, m_ref, acc_ref, *, tq, scale):
    qi = pl.program_id(2)
    hs = range(HPS)
    cols = lambda h: slice(h * LANE, (h + 1) * LANE)
    q = [q_ref[:, cols(h)] for h in hs]
    for h in hs:
        _softmax_init(m_ref.at[h], acc_ref.at[h])

    def step(j, nc, bias):
        sl = pl.ds(j * tq if isinstance(j, int) else pl.multiple_of(j * tq, tq), nc * tq)
        mb = mb_ref[j] if nc == 1 else jnp.concatenate([mb_ref[j], mb_ref[j + 1]], axis=1)
        mb = mb.astype(F32)
        s = [_dot_nt(q[h], k_ref[sl, cols(h)]) + mb for h in hs]
        for h in hs:
            sh = s[h] if bias is None else s[h] + bias(h)
            _softmax_step(sh, v_ref[sl, cols(h)], m_ref.at[h], acc_ref.at[h], scale)

    n_far = jnp.maximum(qi - 1, 0)

    def body(p, c):
        step(2 * p, 2, None)
        return c

    lax.fori_loop(0, n_far // 2, body, 0)

    @pl.when(n_far % 2 == 1)
    def _():
        step(n_far - 1, 1, None)

    @pl.when(qi == 0)
    def _():
        step(0, 1, lambda h: bias_ref[h, :, tq:2 * tq])

    @pl.when(qi >= 1)
    def _():
        step(qi - 1, 2, lambda h: bias_ref[h])

    for h in hs:
        o_ref[:, cols(h)] = _softmax_finish(acc_ref.at[h]).astype(o_ref.dtype)


def dsa_attention(proj, maskb, btiles, *, B, S, tq):
    M = B * S
    nq = S // tq
    H = DSA_HEADS
    scale = DSA_HEAD_DIM ** -0.5
    w = HPS * LANE
    return pl.pallas_call(
        functools.partial(_dsa_attn_kernel, tq=tq, scale=scale),
        grid=(B, H // HPS, nq),
        in_specs=[pl.BlockSpec((tq, w), lambda b, h, i: (b * nq + i, OFF_DSA_Q // w + h)),
                  pl.BlockSpec((S, w), lambda b, h, i: (b, OFF_DSA_K // w + h)),
                  pl.BlockSpec((S, w), lambda b, h, i: (b, OFF_DSA_V // w + h)),
                  pl.BlockSpec((None, nq, tq, tq), lambda b, h, i: (b, 0, i, 0)),
                  pl.BlockSpec((HPS, tq, 2 * tq), lambda b, h, i: (h, 0, 0))],
        out_specs=pl.BlockSpec((tq, w), lambda b, h, i: (b * nq + i, h)),
        out_shape=jax.ShapeDtypeStruct((M, H * DSA_HEAD_DIM), BF16),
        scratch_shapes=[pltpu.VMEM((HPS, tq, LANE), F32), pltpu.VMEM((HPS, tq, 2 * LANE), F32)],
        compiler_params=_cparams(("parallel", "parallel", "arbitrary")),
        name="dsa_attention",
    )(proj, proj, proj, maskb, btiles)


def _nsa_compress_kernel(x_ref, pos_ref, w1_ref, w2_ref, o_ref, *, ncp):
    dk = NSA_HEAD_DIM
    posw = _dot(pos_ref[...], w1_ref[...])[0:1, :]
    for g in range(NSA_GROUPS):
        lo = jnp.zeros((ncp, CMP_HIDDEN), F32)
        hi = jnp.zeros((ncp, CMP_HIDDEN), F32)
        for l in range(CMP_STRIDE):
            xs = x_ref[:, (l * NSA_GROUPS + g) * dk:(l * NSA_GROUPS + g + 1) * dk]
            lo = lo + _dot(xs, w1_ref[l * dk:(l + 1) * dk, :])
            hi = hi + _dot(xs, w1_ref[(CMP_STRIDE + l) * dk:(CMP_STRIDE + l + 1) * dk, :])
        hid = lo + pltpu.roll(hi, ncp - 1, 0) + posw
        o_ref[g] = _dot(jax.nn.gelu(hid).astype(BF16), w2_ref[...]).astype(o_ref.dtype)


def nsa_compress(xkv, posflat, w1, w2, *, B, S):
    ncp = S // CMP_STRIDE
    G, dk = NSA_GROUPS, NSA_HEAD_DIM
    width = CMP_STRIDE * G * dk
    return pl.pallas_call(
        functools.partial(_nsa_compress_kernel, ncp=ncp),
        grid=(2, B),
        in_specs=[pl.BlockSpec((None, ncp, width), lambda a, b: (a, b, 0)),
                  pl.BlockSpec((None, 8, CMP_LEN * dk), lambda a, b: (a, 0, 0)),
                  pl.BlockSpec((None, CMP_LEN * dk, CMP_HIDDEN), lambda a, b: (a, 0, 0)),
                  pl.BlockSpec((None, CMP_HIDDEN, dk), lambda a, b: (a, 0, 0))],
        out_specs=pl.BlockSpec((None, None, G, ncp, dk), lambda a, b: (a, b, 0, 0, 0)),
        out_shape=jax.ShapeDtypeStruct((2, B, G, ncp, dk), BF16),
        compiler_params=_cparams(("parallel", "parallel")),
        name="nsa_compress",
    )(xkv, posflat, w1, w2)


def _nsa_cmp_kernel(q_ref, kc_ref, vc_ref, ov_ref, oc_ref, sel_ref, *, tq, ncp, ns, n_sel, scale):
    t0 = pl.program_id(2) * tq
    dk = NSA_HEAD_DIM
    kc = kc_ref[...]
    vc = vc_ref[...]
    trow = lax.broadcasted_iota(jnp.int32, (tq, ncp), 0) + t0
    ccol = lax.broadcasted_iota(jnp.int32, (tq, ncp), 1)
    vis = (ccol * CMP_STRIDE + (CMP_LEN - 1)) <= trow
    psum = jnp.zeros((tq, ncp), F32)
    for h in range(NSA_HG):
        s = jnp.where(vis, _dot_nt(q_ref[:, h * dk:(h + 1) * dk], kc) * scale, NEG)
        m = jnp.max(s, axis=-1, keepdims=True)
        p = jnp.where(vis, jnp.exp(s - m), 0.0)
        d = jnp.sum(p, axis=-1, keepdims=True)
        p = p / jnp.where(d > 0, d, 1.0)
        oc_ref[:, h * dk:(h + 1) * dk] = _dot(p.astype(BF16), vc).astype(oc_ref.dtype)
        psum = psum + p
    p_hi = psum.astype(BF16)
    p_lo = (psum - p_hi.astype(F32)).astype(BF16)
    ov = ov_ref[...]
    imp = _dot_nt(ov, p_hi) + _dot_nt(ov, p_lo)
    imp = imp[:ns]
    jrow = lax.broadcasted_iota(jnp.int32, (ns, tq), 0)
    tcol = lax.broadcasted_iota(jnp.int32, (ns, tq), 1) + t0
    blk_t = tcol // SEL_BLOCK
    forced = (jrow == 0) | (jrow == blk_t) | (jrow == blk_t - 1)
    val = jnp.where(forced, -NEG, imp)
    val = jnp.where(jrow * SEL_BLOCK > tcol, NEG, val)
    rank = jnp.zeros((ns, tq), F32)
    for j in range(ns):
        other = val[j:j + 1, :]
        ahead = (other > val) | ((other == val) & (jrow > j))
        rank = rank + jnp.where(ahead, 1.0, 0.0)
    selneg = jnp.where(rank < float(n_sel), 0.0, NEG)
    if ns < LANE:
        selneg = jnp.concatenate([selneg, jnp.zeros((LANE - ns, tq), F32)], axis=0)
    sel_ref[...] = selneg.T.astype(sel_ref.dtype)


def nsa_cmp_attention(proj, kvc, ovt, *, B, S, tq):
    M = B * S
    nq = S // tq
    G, dk = NSA_GROUPS, NSA_HEAD_DIM
    ncp = S // CMP_STRIDE
    ns = S // SEL_BLOCK
    assert ns <= LANE
    n_sel = min(SEL_COUNT, ns)
    gw = NSA_HG * dk
    return pl.pallas_call(
        functools.partial(_nsa_cmp_kernel, tq=tq, ncp=ncp, ns=ns, n_sel=n_sel, scale=dk ** -0.5),
        grid=(B, G, nq),
        in_specs=[pl.BlockSpec((tq, gw), lambda b, g, i: (b * nq + i, OFF_NSA_Q // gw + g)),
                  pl.BlockSpec((None, None, None, ncp, dk), lambda b, g, i: (0, b, g, 0, 0)),
                  pl.BlockSpec((None, None, None, ncp, dk), lambda b, g, i: (1, b, g, 0, 0)),
                  pl.BlockSpec((LANE, ncp), lambda b, g, i: (0, 0))],
        out_specs=[pl.BlockSpec((tq, gw), lambda b, g, i: (b * nq + i, g)),
                   pl.BlockSpec((None, None, tq, LANE), lambda b, g, i: (b, g, i, 0))],
        out_shape=[jax.ShapeDtypeStruct((M, G * gw), BF16),
                   jax.ShapeDtypeStruct((B, G, S, LANE), BF16)],
        compiler_params=_cparams(("parallel", "parallel", "parallel")),
        name="nsa_cmp_attention",
    )(proj, kvc, kvc, ovt)


def _nsa_main_kernel(q_ref, sel_ref, ks_ref, vs_ref, kw_ref, vw_ref, ind_ref, bias_ref, gate_ref, oc_ref,
                     o_ref, m_ref, acc_ref, *, tq, nback, scale):
    g = pl.program_id(1)
    qi = pl.program_id(2)
    dk = NSA_HEAD_DIM
    HG = NSA_HG
    ns = NSA_STREAMS
    hp = HG // ns
    st = range(ns)
    stack = lambda f: [jnp.concatenate([f(a * hp + i) for i in range(hp)], axis=0) for a in st]
    q = stack(lambda h: q_ref[:, h * dk:(h + 1) * dk])
    sel = jnp.concatenate([sel_ref[...]] * hp, axis=0)
    qa = [jnp.concatenate([q[a], sel], axis=1) for a in st]
    gate = jax.nn.sigmoid(gate_ref[...].astype(F32))

    def init():
        for a in st:
            _softmax_init(m_ref.at[a], acc_ref.at[a])

    def steps(s, v, fix):
        for a in st:
            _softmax_step(fix(a, s[a]), v, m_ref.at[a], acc_ref.at[a], scale)

    wspan = nback + 1

    def near_step(scores, first, nc):
        s, v = scores(first, nc * tq)
        steps(s, v, lambda a, x: x + bias_ref[a, :, (wspan - nc) * tq:wspan * tq])

    def near(scores, nspan):
        for nc in range(1, nspan):
            @pl.when(qi == nc - 1)
            def _(nc=nc):
                near_step(scores, 0, nc)

        @pl.when(qi >= nspan - 1)
        def _():
            near_step(scores, qi - (nspan - 1), nspan)

    init()

    def rows_from(j, width):
        return pl.ds(j * tq if isinstance(j, int) else pl.multiple_of(j * tq, tq), width)

    def sel_scores(j, width=tq):
        sl = rows_from(j, width)
        ka = jnp.concatenate([ks_ref[sl, :], ind_ref[sl, :]], axis=1)
        return [_dot_nt(qa[a], ka) for a in st], vs_ref[sl, :]

    n_far = jnp.maximum(qi - 1, 0)

    def sel_body(p, c):
        s, v = sel_scores(2 * p, 2 * tq)
        steps(s, v, lambda a, x: x)
        return c

    lax.fori_loop(0, n_far // 2, sel_body, 0)

    @pl.when(n_far % 2 == 1)
    def _():
        s, v = sel_scores(n_far - 1)
        steps(s, v, lambda a, x: x)

    near(sel_scores, 2)
    o_s =[_softmax_finish(acc_ref.at[a]) for a in st]

    init()

    def win_scores(j, width):
        sl = rows_from(j, width)
        kw = kw_ref[sl, :]
        return [_dot_nt(q[a], kw) for a in st], vw_ref[sl, :]

    near(win_scores, wspan)
    o_w =[_softmax_finish(acc_ref.at[a]) for a in st]

    lane = lax.broadcasted_iota(jnp.int32, gate.shape, 1)
    for h in range(HG):
        c = g * HG + h
        a, sl = h // hp, slice((h % hp) * tq, (h % hp + 1) * tq)
        gc = [jnp.sum(jnp.where(lane == br * NSA_HEADS + c, gate, 0.0), axis=-1, keepdims=True)
              for br in range(3)]
        o = gc[0] * oc_ref[:, h * dk:(h + 1) * dk].astype(F32) + gc[1] * o_s[a][sl] + gc[2] * o_w[a][sl]
        o_ref[:, h * dk:(h + 1) * dk] = o.astype(o_ref.dtype)


def nsa_main(proj, selneg, ind, btiles, o_c, *, B, S, tq):
    M = B * S
    nq = S // tq
    G, dk, HG = NSA_GROUPS, NSA_HEAD_DIM, NSA_HG
    gw = HG * dk
    assert WINDOW % tq == 0 and WINDOW // tq >= 2
    nback = WINDOW // tq
    ns, hp = NSA_STREAMS, HG // NSA_STREAMS
    kv = lambda n: pl.BlockSpec((S, dk), lambda b, g, i, n=n: (b, (OFF_NSA_KV + n * G * dk) // dk + g))
    return pl.pallas_call(
        functools.partial(_nsa_main_kernel, tq=tq, nback=nback, scale=dk ** -0.5),
        grid=(B, G, nq),
        in_specs=[pl.BlockSpec((tq, gw), lambda b, g, i: (b * nq + i, OFF_NSA_Q // gw + g)),
                  pl.BlockSpec((None, None, tq, LANE), lambda b, g, i: (b, g, i, 0)),
                  kv(2), kv(3), kv(4), kv(5),
                  pl.BlockSpec((S, LANE), lambda b, g, i: (0, 0)),
                  pl.BlockSpec((ns, hp * tq, (nback + 1) * tq), lambda b, g, i: (g, 0, 0)),
                  pl.BlockSpec((tq, LANE), lambda b, g, i: (b * nq + i, OFF_NSA_GATE // LANE)),
                  pl.BlockSpec((tq, gw), lambda b, g, i: (b * nq + i, g))],
        out_specs=pl.BlockSpec((tq, gw), lambda b, g, i: (b * nq + i, g)),
        out_shape=jax.ShapeDtypeStruct((M, G * gw), BF16),
        scratch_shapes=[pltpu.VMEM((NSA_STREAMS, HG // NSA_STREAMS * tq, LANE), F32),
                        pltpu.VMEM((NSA_STREAMS, HG // NSA_STREAMS * tq, 2 * LANE), F32)],
        compiler_params=_cparams(("parallel", "parallel", "arbitrary")),
        name="nsa_main",
    )(proj, selneg, proj, proj, proj, proj, ind, btiles, proj, o_c)


def _merge_kernel(x_ref, oa_ref, ob_ref, oc_ref, g0_ref, g1_ref, g2_ref, wb_ref, wo_ref, gain_ref, o_ref):
    merged = None
    for o_r, g_r, br in ((oa_ref, g0_ref, 0), (ob_ref, g1_ref, 1), (oc_ref, g2_ref, 2)):
        t = jax.nn.sigmoid(g_r[...].astype(F32)) * _dot(o_r[...], wb_ref[br])
        merged = t if merged is None else merged + t
    y = _dot(merged.astype(BF16), wo_ref[...])
    o_ref[...] = x_ref[...] + _rms(y, gain_ref[...])


def merge_branches(x, o_a, o_b, o_c, proj, wb, wo, gain, *, tm):
    M, D = x.shape
    row = pl.BlockSpec((tm, D), lambda i: (i, 0))
    gate = lambda br: pl.BlockSpec((tm, D), lambda i, br=br: (i, OFF_BGATE // D + br))
    return pl.pallas_call(
        _merge_kernel,
        grid=(M // tm,),
        in_specs=[row, row, row, row, gate(0), gate(1), gate(2),
                  pl.BlockSpec(wb.shape, lambda i: (0, 0, 0)),
                  pl.BlockSpec(wo.shape, lambda i: (0, 0)),
                  pl.BlockSpec((1, D), lambda i: (0, 0))],
        out_specs=row,
        out_shape=jax.ShapeDtypeStruct((M, D), F32),
        compiler_params=_cparams(("parallel",)),
        name="merge_branches",
    )(x, o_a, o_b, o_c, proj, proj, proj, wb, wo, gain)


def _cross_kernel(x_ref, kv_ref, wq_ref, wo_ref, gpre_ref, gpost_ref, o_ref, *, scale):
    x = x_ref[...]
    h = _rms(x, gpre_ref[...]).astype(BF16)
    q = _dot(h, wq_ref[...]).astype(BF16)
    dh = CROSS_HEAD_DIM
    outs = []
    for hd in range(CROSS_HEADS):
        k = kv_ref[:, hd * dh:(hd + 1) * dh]
        v = kv_ref[:, D_MODEL + hd * dh:D_MODEL + (hd + 1) * dh]
        s = _dot_nt(q[:, hd * dh:(hd + 1) * dh], k) * scale
        p = jnp.exp(s - jnp.max(s, axis=-1, keepdims=True))
        p = p / jnp.sum(p, axis=-1, keepdims=True)
        outs.append(_dot(p.astype(BF16), v).astype(BF16))
    y = _dot(jnp.concatenate(outs, axis=1), wo_ref[...])
    o_ref[...] = x + _rms(y, gpost_ref[...])


def cross_attention(x, kv, wq, wo, gpre, gpost, *, S, tm):
    M, D = x.shape
    mlen = kv.shape[0] // (M // S)
    per_b = S // tm
    return pl.pallas_call(
        functools.partial(_cross_kernel, scale=CROSS_HEAD_DIM ** -0.5),
        grid=(M // tm,),
        in_specs=[pl.BlockSpec((tm, D), lambda i: (i, 0)),
                  pl.BlockSpec((mlen, 2 * D), lambda i: (i // per_b, 0)),
                  pl.BlockSpec(wq.shape, lambda i: (0, 0)),
                  pl.BlockSpec(wo.shape, lambda i: (0, 0)),
                  pl.BlockSpec((1, D), lambda i: (0, 0)),
                  pl.BlockSpec((1, D), lambda i: (0, 0))],
        out_specs=pl.BlockSpec((tm, D), lambda i: (i, 0)),
        out_shape=jax.ShapeDtypeStruct((M, D), F32),
        compiler_params=_cparams(("parallel",)),
        name="cross_attention",
    )(x, kv, wq, wo, gpre, gpost)


def _mlp_kernel(x_ref, w1_ref, w2_ref, gpre_ref, gpost_ref, o_ref, h_ref, acc_ref):
    j = pl.program_id(1)

    @pl.when(j == 0)
    def _():
        h_ref[...] = _rms(x_ref[...], gpre_ref[...]).astype(BF16)
        acc_ref[...] = jnp.zeros(acc_ref.shape, F32)

    a = jnp.maximum(_dot(h_ref[...], w1_ref[...]), 0.0)
    acc_ref[...] += _dot((a * a).astype(BF16), w2_ref[...])

    @pl.when(j == pl.num_programs(1) - 1)
    def _():
        o_ref[...] = x_ref[...] + _rms(acc_ref[...], gpost_ref[...])


def mlp(x, w1, w2, gpre, gpost, *, tm, tf):
    M, D = x.shape
    FF = w1.shape[1]
    return pl.pallas_call(
        _mlp_kernel,
        grid=(M // tm, FF // tf),
        in_specs=[pl.BlockSpec((tm, D), lambda i, j: (i, 0)),
                  pl.BlockSpec((D, tf), lambda i, j: (0, j)),
                  pl.BlockSpec((tf, D), lambda i, j: (j, 0)),
                  pl.BlockSpec((1, D), lambda i, j: (0, 0)),
                  pl.BlockSpec((1, D), lambda i, j: (0, 0))],
        out_specs=pl.BlockSpec((tm, D), lambda i, j: (i, 0)),
        out_shape=jax.ShapeDtypeStruct((M, D), F32),
        scratch_shapes=[pltpu.VMEM((tm, D), BF16), pltpu.VMEM((tm, D), F32)],
        compiler_params=_cparams(("parallel", "arbitrary")),
        name="mlp",
    )(x, w1, w2, gpre, gpost)


def _rot_half_cols(w):
    half = w.shape[-1] // 2
    return jnp.concatenate([-w[..., half:], w[..., :half]], axis=-1)


def _pack_w_in(w):
    widths = (MLA_Q_LORA, MLA_KV_LORA, MLA_ROPE, 1024, 1024, 1024, IDX_HEADS * IDX_DIM, IDX_DIM, IDX_HEADS,
              1024, 256, 256, 256, 256, 256, 256, 3 * NSA_HEADS, N_BRANCH * D_MODEL)
    offs = np.concatenate([[0], np.cumsum(widths)])
    seg = [w[:, offs[i]:offs[i + 1]] for i in range(len(widths))]
    (q_lat, kv_lat, k_rope, dsa_q, dsa_k, dsa_v, idx_q, idx_k, idx_w,
     nsa_q, kc, vc, ks, vs, kw, vw, nsa_gate, bgate) = seg
    z = lambda n: jnp.zeros((w.shape[0], n), w.dtype)
    cols = [dsa_q, dsa_k, dsa_v, nsa_q, bgate, idx_q, kc, vc, ks, vs, kw, vw,
            q_lat, kv_lat, k_rope, z(LANE - MLA_ROPE), _rot_half_cols(k_rope), z(LANE - MLA_ROPE), z(LANE),
            idx_k, idx_k, idx_w, z(LANE - IDX_HEADS), nsa_gate, z(LANE - 3 * NSA_HEADS)]
    out = jnp.concatenate(cols, axis=1)
    out = jnp.concatenate([out, z(N_PACK - out.shape[1])], axis=1)
    return out.astype(BF16)


def _pack_mla_weights(w_uq, w_ukv):
    H = MLA_HEADS
    uq = w_uq.reshape(MLA_Q_LORA, H, MLA_NOPE + MLA_ROPE)
    nope = uq[:, :, :MLA_NOPE].reshape(MLA_Q_LORA, H * MLA_NOPE)
    rope = uq[:, :, MLA_NOPE:]
    pad = jnp.zeros((MLA_Q_LORA, H, LANE - MLA_ROPE), w_uq.dtype)
    wqr = jnp.concatenate([rope, pad], axis=-1).reshape(MLA_Q_LORA, H * LANE)
    wqrr = jnp.concatenate([_rot_half_cols(rope), pad], axis=-1).reshape(MLA_Q_LORA, H * LANE)
    ukv = w_ukv.reshape(MLA_KV_LORA, H, MLA_NOPE + MLA_V)
    wkv = jnp.concatenate([ukv[:, :, :MLA_NOPE].reshape(MLA_KV_LORA, H * MLA_NOPE),
                           ukv[:, :, MLA_NOPE:].reshape(MLA_KV_LORA, H * MLA_V)], axis=1)
    return nope.astype(BF16), wqr.astype(BF16), wqrr.astype(BF16), wkv.astype(BF16)


def _constants(S):
    half = MLA_ROPE // 2
    inv_freq = ROPE_BASE ** (-np.arange(0, MLA_ROPE, 2, dtype=np.float32) / MLA_ROPE)
    freq = np.zeros((1, LANE), np.float32)
    freq[0, :half] = inv_freq
    freq[0, half:2 * half] = inv_freq
    ncp = S // CMP_STRIDE
    ns = S // SEL_BLOCK
    c_start = np.arange(ncp) * CMP_STRIDE
    j_start = np.arange(ns) * SEL_BLOCK
    overlap = ((c_start[None, :] < j_start[:, None] + SEL_BLOCK) &
               (c_start[None, :] + CMP_LEN > j_start[:, None])).astype(np.float32)
    overlap[:, ncp - 1] = 0.0
    ovt = np.zeros((LANE, ncp), np.float32)
    ovt[:ns] = overlap
    ind = np.zeros((S, LANE), np.float32)
    ind[np.arange(S), np.arange(S) // SEL_BLOCK] = 1.0
    return jnp.asarray(freq), jnp.asarray(ovt, BF16), jnp.asarray(ind, BF16)


def kernel(x, mem, positions, rel_bias, norm_gains, w_in, mla_q_norm, mla_kv_norm, mla_w_uq, mla_w_ukv,
           nsa_cmp_pos, nsa_cmp_w1, nsa_cmp_w2, w_branch, w_out, cross_wq, cross_wkv, cross_wo, mlp_w1, mlp_w2):
    B, S, D = x.shape
    M = B * S
    depth = w_in.shape[0]
    tq = TQ if S % TQ == 0 else TQ_NSA
    tqn = TQ_NSA
    assert S % tq == 0 and S % tqn == 0 and D == D_MODEL
    topk = min(DSA_TOPK_MAX, S // 4)
    freq, ovt, ind = _constants(S)
    btiles = bias_tiles(rel_bias, tq, 0, DSA_HEADS, nspan=2, masked=False, stack=1)
    btiles_n = bias_tiles(rel_bias, tqn, DSA_HEADS, NSA_HEADS, nspan=WINDOW // tqn + 1, masked=True,
                          stack=NSA_HG // NSA_STREAMS)
    pos = positions.reshape(M, 1)
    xf = x.reshape(M, D)
    memf = mem.reshape(B * mem.shape[1], D)
    tm_big = 1024 if M % 1024 == 0 else tq
    tm = 512 if M % 512 == 0 else tq
    ncp = S // CMP_STRIDE
    row = lambda v: v.reshape(1, -1)

    for l in range(depth):
        g = norm_gains[l]
        proj = norm_matmul(xf, row(g[0]), _pack_w_in(w_in[l]), tm=tm_big, tn=TN_IN)

        wqn, wqr, wqrr, wkv = _pack_mla_weights(mla_w_uq[l], mla_w_ukv[l])
        qn, qr, kn, vv, kr = mla_prep(proj, pos, freq, row(mla_q_norm[l]), row(mla_kv_norm[l]),
                                      wqn, wqr, wqrr, wkv, tm=tm)
        o_a = mla_attention(qn, qr, kn, kr, vv, B=B, S=S, tq=tq)

        maskb = dsa_index_mask(proj, B=B, S=S, tqi=TQ_IDX, tk=tq, topk=topk)
        o_b = dsa_attention(proj, maskb, btiles, B=B, S=S, tq=tq)

        kvc_in = jnp.stack([
            proj[:, OFF_NSA_KV + a * 256:OFF_NSA_KV + (a + 1) * 256].reshape(B * ncp, CMP_STRIDE * 256)
            for a in range(2)])
        posflat = jnp.broadcast_to(nsa_cmp_pos[l].reshape(2, 1, CMP_LEN * NSA_HEAD_DIM),
                                   (2, 8, CMP_LEN * NSA_HEAD_DIM)).astype(BF16)
        kvc = nsa_compress(kvc_in, posflat, nsa_cmp_w1[l].astype(BF16), nsa_cmp_w2[l].astype(BF16), B=B, S=S)
        o_cmp, selneg = nsa_cmp_attention(proj, kvc, ovt, B=B, S=S, tq=tqn)
        o_c = nsa_main(proj, selneg, ind, btiles_n, o_cmp, B=B, S=S, tq=tqn)

        xf = merge_branches(xf, o_a, o_b, o_c, proj, w_branch[l].astype(BF16), w_out[l].astype(BF16),
                            row(g[1]), tm=tm)

        mkv = norm_matmul(memf, row(g[3]), cross_wkv[l].astype(BF16), tm=memf.shape[0] // B, tn=1024)
        xf = cross_attention(xf, mkv, cross_wq[l].astype(BF16), cross_wo[l].astype(BF16),
                             row(g[2]), row(g[4]), S=S, tm=tm)

        xf = mlp(xf, mlp_w1[l].astype(BF16), mlp_w2[l].astype(BF16), row(g[5]), row(g[6]), tm=tm_big, tf=1024)

    return xf.reshape(B, S, D)
```

```python
import functools
import math

import numpy as np
import jax
import jax.numpy as jnp
from jax import lax
from jax.experimental import pallas as pl
from jax.experimental.pallas import tpu as pltpu

F32 = jnp.float32
BF16 = jnp.bfloat16

D_MODEL = 1024
EPS = 1e-6
MLA_HEADS = 8
MLA_Q_LORA = 384
MLA_KV_LORA = 256
MLA_NOPE = 128
MLA_ROPE = 64
MLA_V = 128
ROPE_BASE = 10000.0
DSA_HEADS = 8
DSA_HEAD_DIM = 128
IDX_HEADS = 8
IDX_DIM = 64
DSA_TOPK_MAX = 256
NSA_HEADS = 8
NSA_GROUPS = 2
NSA_HG = NSA_HEADS // NSA_GROUPS
NSA_HEAD_DIM = 128
CMP_STRIDE = 16
CMP_LEN = 2 * CMP_STRIDE
CMP_HIDDEN = 256
SEL_BLOCK = 64
SEL_COUNT = 16
WINDOW = 512
REL_BUCKETS = 32
REL_MAX_DIST = 128
REL_HEADS = DSA_HEADS + NSA_HEADS
CROSS_HEADS = 4
CROSS_HEAD_DIM = D_MODEL // CROSS_HEADS
D_FF = 4 * D_MODEL
N_BRANCH = 3

LANE = 128
NEG = -1e30
VMEM_LIMIT = 48 * 1024 * 1024

TQ = 512
TQ_NSA = 256
HPS = 2
NSA_STREAMS = 2
BISECT_ITERS = 32
IDX_TIE_CHECK = 16
TQ_IDX = 512
IDX_CHUNK = 256
IDX_ROWS = 64

OFF_DSA_Q = 0
OFF_DSA_K = 1024
OFF_DSA_V = 2048
OFF_NSA_Q = 3072
OFF_BGATE = 4096
OFF_IDX_Q = 7168
OFF_NSA_KV = 7680
OFF_MLA = 9216
OFF_IDX_K = 10240
OFF_IDX_W = 10368
OFF_NSA_GATE = 10496
N_PACK = 10752
TN_IN = 1536


def _cparams(sem):
    return pltpu.CompilerParams(dimension_semantics=sem, vmem_limit_bytes=VMEM_LIMIT)


def _rms(x, g):
    return x * lax.rsqrt(jnp.mean(x * x, axis=-1, keepdims=True) + EPS) * g


def _dot(a, b):
    return jnp.dot(a, b, preferred_element_type=F32)


def _dot_nt(a, b):
    return lax.dot_general(a, b, (((1,), (1,)), ((), ())), preferred_element_type=F32)


def _norm_mm_kernel(x_ref, g_ref, w_ref, o_ref, xn_ref):
    @pl.when(pl.program_id(1) == 0)
    def _():
        xn_ref[...] = _rms(x_ref[...].astype(F32), g_ref[...]).astype(BF16)

    o_ref[...] = _dot(xn_ref[...], w_ref[...]).astype(o_ref.dtype)


def norm_matmul(x, g, w, *, tm, tn):
    M, K = x.shape
    N = w.shape[1]
    return pl.pallas_call(
        _norm_mm_kernel,
        grid=(M // tm, N // tn),
        in_specs=[pl.BlockSpec((tm, K), lambda i, j: (i, 0)),
                  pl.BlockSpec((1, K), lambda i, j: (0, 0)),
                  pl.BlockSpec((K, tn), lambda i, j: (0, j))],
        out_specs=pl.BlockSpec((tm, tn), lambda i, j: (i, j)),
        out_shape=jax.ShapeDtypeStruct((M, N), BF16),
        scratch_shapes=[pltpu.VMEM((tm, K), BF16)],
        compiler_params=_cparams(("parallel", "arbitrary")),
        name="norm_matmul",
    )(x, g, w)


def _bias_tiles_kernel(tab_ref, bkt_ref, o_ref, *, inv_scale, h0, tq, nspan, masked):
    h = pl.program_id(0) + h0
    bkt = bkt_ref[...]
    far = tab_ref[REL_BUCKETS - 1, h]
    out = jnp.zeros(bkt.shape, F32)
    for b in range(REL_BUCKETS - 1):
        out = jnp.where(bkt == b, (tab_ref[b, h] - far) * inv_scale, out)
    diag, prev = out[0], out[1]
    r, c = _tri(tq, tq)
    if masked:
        diag = jnp.where(r >= c, diag, NEG)
    o_ref[:, (nspan - 1) * tq:nspan * tq] = diag
    o_ref[:, (nspan - 2) * tq:(nspan - 1) * tq] = prev
    for k in range(nspan - 2):
        edge = jnp.where(r < c, 0.0, NEG) if (masked and k == 0) else jnp.zeros((tq, tq), F32)
        o_ref[:, k * tq:(k + 1) * tq] = edge


def _bucket_tiles(tq):
    i = np.arange(tq)[:, None]
    j = np.arange(tq)[None, :]
    d = np.stack([i - j, tq + i - j]).astype(np.int32)
    n = jnp.maximum(jnp.asarray(d), 0)
    exact = REL_BUCKETS // 2
    nf = jnp.maximum(n, 1).astype(F32)
    log_b = exact + (jnp.log(nf / exact) / math.log(REL_MAX_DIST / exact) * (REL_BUCKETS - exact)).astype(jnp.int32)
    return jnp.where(n < exact, n, jnp.minimum(log_b, REL_BUCKETS - 1)).astype(jnp.int32)


def bias_tiles(rel_bias, tq, h0, nh, *, nspan, masked, stack):
    assert tq >= REL_MAX_DIST
    bkt = _bucket_tiles(tq)
    assert DSA_HEAD_DIM == NSA_HEAD_DIM
    return pl.pallas_call(
        functools.partial(_bias_tiles_kernel, inv_scale=DSA_HEAD_DIM ** 0.5, h0=h0, tq=tq, nspan=nspan,
                          masked=masked),
        grid=(nh,),
        in_specs=[pl.BlockSpec(memory_space=pltpu.SMEM),
                  pl.BlockSpec((2, tq, tq), lambda h: (0, 0, 0))],
        out_specs=pl.BlockSpec((None, tq, nspan * tq), lambda h: (h // stack, h % stack, 0)),
        out_shape=jax.ShapeDtypeStruct((nh // stack, stack * tq, nspan * tq), F32),
        compiler_params=_cparams(("arbitrary",)),
        name="bias_tiles",
    )(rel_bias, bkt)


def _mla_prep_kernel(lat_ref, pos_ref, freq_ref, qg_ref, kvg_ref, wqn_ref, wqr_ref, wqrr_ref, wkv_ref,
                     qn_ref, qr_ref, kn_ref, v_ref, kr_ref):
    lat = lat_ref[...].astype(F32)
    qlat = _rms(lat[:, :MLA_Q_LORA], qg_ref[...]).astype(BF16)
    kvlat = _rms(lat[:, MLA_Q_LORA:MLA_Q_LORA + MLA_KV_LORA], kvg_ref[...]).astype(BF16)
    o = MLA_Q_LORA + MLA_KV_LORA
    kr = lat[:, o:o + LANE]
    krr = lat[:, o + LANE:o + 2 * LANE]
    ang = pos_ref[...].astype(F32) * freq_ref[...]
    cos, sin = jnp.cos(ang), jnp.sin(ang)
    cos8 = jnp.concatenate([cos] * MLA_HEADS, axis=1)
    sin8 = jnp.concatenate([sin] * MLA_HEADS, axis=1)
    qn_ref[...] = _dot(qlat, wqn_ref[...]).astype(BF16)
    qr_ref[...] = (_dot(qlat, wqr_ref[...]) * cos8 + _dot(qlat, wqrr_ref[...]) * sin8).astype(BF16)
    kv = _dot(kvlat, wkv_ref[...])
    nk = MLA_HEADS * MLA_NOPE
    kn_ref[...] = kv[:, :nk].astype(BF16)
    v_ref[...] = kv[:, nk:].astype(BF16)
    kr_ref[...] = (kr * cos + krr * sin).astype(BF16)


def mla_prep(proj, pos, freq, qg, kvg, wqn, wqr, wqrr, wkv, *, tm):
    M = proj.shape[0]
    HD = MLA_HEADS * LANE
    full = lambda a: pl.BlockSpec(a.shape, lambda i: (0, 0))
    outs = [jax.ShapeDtypeStruct((M, HD), BF16)] * 4 + [jax.ShapeDtypeStruct((M, LANE), BF16)]
    return pl.pallas_call(
        _mla_prep_kernel,
        grid=(M // tm,),
        in_specs=[pl.BlockSpec((tm, 1024), lambda i: (i, OFF_MLA // 1024)),
                  pl.BlockSpec((tm, 1), lambda i: (i, 0)),
                  full(freq), full(qg), full(kvg), full(wqn), full(wqr), full(wqrr), full(wkv)],
        out_specs=[pl.BlockSpec((tm, HD), lambda i: (i, 0))] * 4 + [pl.BlockSpec((tm, LANE), lambda i: (i, 0))],
        out_shape=outs,
        compiler_params=_cparams(("parallel",)),
        name="mla_prep",
    )(proj, pos, freq, qg, kvg, wqn, wqr, wqrr, wkv)


def _softmax_init(m_ref, acc_ref):
    m_ref[...] = jnp.full(m_ref.shape, NEG, F32)
    acc_ref[...] = jnp.zeros(acc_ref.shape, F32)


def _with_ones(v):
    return jnp.concatenate([v, jnp.ones(v.shape, v.dtype)], axis=1)


def _softmax_step(s, v, m_ref, acc_ref, scale):
    c = scale * math.log2(math.e)
    m_prev = m_ref[...]
    m_new = jnp.maximum(m_prev, jnp.max(s, axis=-1, keepdims=True))
    alpha = jnp.exp2((m_prev - m_new) * c)
    p = jnp.exp2((s - jnp.tile(m_new, (1, s.shape[1] // LANE))) * c)
    acc_ref[...] = jnp.tile(alpha, (1, 2)) * acc_ref[...] + _dot(p.astype(BF16), _with_ones(v))
    m_ref[...] = m_new


def _softmax_finish(acc_ref):
    acc = acc_ref[...]
    return acc[:, :LANE] / acc[:, LANE:]


def _tri(tq, tk):
    row = lax.broadcasted_iota(jnp.int32, (tq, tk), 0)
    col = lax.broadcasted_iota(jnp.int32, (tq, tk), 1)
    return row, col


def _mla_attn_kernel(qn_ref, qr_ref, kn_ref, kr_ref, v_ref, o_ref, m_ref, acc_ref, *, tq, scale):
    qi = pl.program_id(2)
    hs = range(HPS)
    q = [jnp.concatenate([qn_ref[:, h * LANE:(h + 1) * LANE], qr_ref[:, h * LANE:(h + 1) * LANE]], axis=-1)
         for h in hs]
    for h in hs:
        _softmax_init(m_ref.at[h], acc_ref.at[h])

    def scores(j, width=tq):
        sl = pl.ds(pl.multiple_of(j * tq, tq), width)
        kr = kr_ref[sl, :]
        return [_dot_nt(q[h], jnp.concatenate([kn_ref[sl, h * LANE:(h + 1) * LANE], kr], axis=-1)) for h in hs], sl

    def far(j, width):
        s, sl = scores(j, width)
        for h in hs:
            _softmax_step(s[h], v_ref[sl, h * LANE:(h + 1) * LANE], m_ref.at[h], acc_ref.at[h], scale)

    def body(p, c):
        far(2 * p, 2 * tq)
        return c

    lax.fori_loop(0, qi // 2, body, 0)

    @pl.when(qi % 2 == 1)
    def _():
        far(qi - 1, tq)

    s, sl = scores(qi)
    row, col = _tri(tq, tq)
    for h in hs:
        _softmax_step(jnp.where(row >= col, s[h], NEG), v_ref[sl, h * LANE:(h + 1) * LANE],
                      m_ref.at[h], acc_ref.at[h], scale)
    for h in hs:
        o_ref[:, h * LANE:(h + 1) * LANE] = _softmax_finish(acc_ref.at[h]).astype(o_ref.dtype)


def mla_attention(qn, qr, kn, kr, v, *, B, S, tq):
    M = B * S
    nq = S // tq
    H = MLA_HEADS
    scale = (MLA_NOPE + MLA_ROPE) ** -0.5
    w = HPS * LANE
    qspec = pl.BlockSpec((tq, w), lambda b, h, i: (b * nq + i, h))
    kspec = pl.BlockSpec((S, w), lambda b, h, i: (b, h))
    return pl.pallas_call(
        functools.partial(_mla_attn_kernel, tq=tq, scale=scale),
        grid=(B, H // HPS, nq),
        in_specs=[qspec, qspec, kspec, pl.BlockSpec((S, LANE), lambda b, h, i: (b, 0)), kspec],
        out_specs=qspec,
        out_shape=jax.ShapeDtypeStruct((M, H * MLA_V), BF16),
        scratch_shapes=[pltpu.VMEM((HPS, tq, LANE), F32), pltpu.VMEM((HPS, tq, 2 * LANE), F32)],
        compiler_params=_cparams(("parallel", "parallel", "arbitrary")),
        name="mla_attention",
    )(qn, qr, kn, kr, v)


def _dsa_index_kernel(iq_ref, ik_ref, iw_ref, tri_ref, mb_ref, sc_ref, *, tq, tk, S, topk, iters):
    t0 = pl.program_id(1) * tq
    nk = S // tk
    nch = (t0 + tq - 1) // tk + 1
    hw = IDX_CHUNK
    nb = tk // LANE
    iq = iq_ref[...]
    iw = iw_ref[...].astype(F32)
    lane = lax.broadcasted_iota(jnp.int32, (tq, LANE), 1)
    qs = []
    for h in range(IDX_HEADS):
        blk = iq[:, (h // 2) * LANE:(h // 2 + 1) * LANE]
        keep = (lane >= IDX_DIM) if h % 2 else (lane < IDX_DIM)
        qs.append(jnp.where(keep, blk, jnp.zeros_like(blk)))
    q8 = jnp.concatenate(qs, axis=0)
    wcol = [iw[:, h:h + 1] for h in range(IDX_HEADS)]
    row = lax.broadcasted_iota(jnp.int32, (tq, hw), 0) + t0
    col0 = lax.broadcasted_iota(jnp.int32, (tq, hw), 1)

    def lane_fold(x, op):
        out = x[:, 0:LANE]
        for b in range(1, x.shape[1] // LANE):
            out = op(out, x[:, b * LANE:(b + 1) * LANE])
        return out

    def score_chunk(c, carry):
        mn, mx = carry
        for half in range(tk // hw):
            k0 = pl.multiple_of(c * tk + half * hw, hw)
            res = _dot_nt(q8, ik_ref[pl.ds(k0, hw), :])
            sc = wcol[0] * jnp.maximum(res[0:tq], 0.0)
            for h in range(1, IDX_HEADS):
                sc = sc + wcol[h] * jnp.maximum(res[h * tq:(h + 1) * tq], 0.0)
            causal = (col0 + k0) <= row
            mn = jnp.minimum(mn, lane_fold(jnp.where(causal, sc, -NEG), jnp.minimum))
            sc = jnp.where(causal, sc, NEG)
            mx = jnp.maximum(mx, lane_fold(sc, jnp.maximum))
            sc_ref[c, :, half * hw:(half + 1) * hw] = sc
        return mn, mx

    mn, mx = lax.fori_loop(0, nch, score_chunk,
                           (jnp.full((tq, LANE), -NEG, F32), jnp.full((tq, LANE), NEG, F32)))
    n_causal = (lax.broadcasted_iota(jnp.int32, (tq, 1), 0) + (t0 + 1)).astype(F32)
    kf = jnp.minimum(float(topk), n_causal)

    rs = IDX_ROWS

    def blocks(j):
        for r in range(tq // rs):
            for b in range(nb):
                rows = slice(r * rs, (r + 1) * rs)
                yield rows, (j, rows, slice(b * LANE, (b + 1) * LANE)), j * tk + b * LANE

    def count(pred):
        def body(c, acc):
            parts = [acc[r * rs:(r + 1) * rs] for r in range(tq // rs)]
            for rows, idx, off in blocks(c):
                r = rows.start // rs
                parts[r] = parts[r] + jnp.where(pred(rows, sc_ref[idx], off), 1.0, 0.0)
            return jnp.concatenate(parts, axis=0)

        acc = lax.fori_loop(0, nch, body, jnp.zeros((tq, LANE), F32))
        return jnp.sum(acc, axis=-1, keepdims=True)

    def bcast(x):
        strips = [jnp.broadcast_to(x[r * rs:(r + 1) * rs], (rs, LANE)) for r in range(tq // rs)]
        return lambda rows: strips[rows.start // rs]

    lo0 = jnp.min(mn, axis=-1, keepdims=True)
    mx = jnp.max(mx, axis=-1, keepdims=True)
    hi0 = mx + (jnp.abs(mx) * 1e-6 + 1.0)

    def bisect(c):
        lo, hi, n_lo, n_hi = c
        mid = 0.5 * (lo + hi)
        mid_b = bcast(mid)
        n_mid = count(lambda rows, blk, off: blk >= mid_b(rows))
        ge = n_mid >= kf
        return (jnp.where(ge, mid, lo), jnp.where(ge, hi, mid),
                jnp.where(ge, n_mid, n_lo), jnp.where(ge, n_hi, n_mid))

    def open_rows(n_lo, n_hi, settled):
        return (n_lo - kf) * jnp.where(n_lo - n_hi > 2.0, 1.0, 0.0) * (1.0 - settled)

    def search(state, limit, settled):
        def cond(c):
            return jnp.logical_and(c[0] < limit, jnp.max(open_rows(c[3], c[4], settled)) > 0.0)

        def body(c):
            return (c[0] + 2,) + bisect(bisect(c[1:]))

        return lax.while_loop(cond, body, state)

    def band_min_max(lo, hi):
        lo_b, hi_b = bcast(lo), bcast(hi)

        def body(c, carry):
            bmin = [carry[0][r * rs:(r + 1) * rs] for r in range(tq // rs)]
            bmax = [carry[1][r * rs:(r + 1) * rs] for r in range(tq // rs)]
            for rows, idx, off in blocks(c):
                r = rows.start // rs
                blk = sc_ref[idx]
                inside = (blk >= lo_b(rows)) & (blk < hi_b(rows))
                bmin[r] = jnp.minimum(bmin[r], jnp.where(inside, blk, -NEG))
                bmax[r] = jnp.maximum(bmax[r], jnp.where(inside, blk, NEG))
            return jnp.concatenate(bmin, axis=0), jnp.concatenate(bmax, axis=0)

        bmin, bmax = lax.fori_loop(0, nch, body,
                                   (jnp.full((tq, LANE), -NEG, F32), jnp.full((tq, LANE), NEG, F32)))
        return jnp.min(bmin, axis=-1, keepdims=True), jnp.max(bmax, axis=-1, keepdims=True)

    def finish(state):
        _, lo, hi, n_lo, n_hi = state
        over = n_lo > kf
        bmin, bmax = lax.cond(jnp.max(n_lo - kf) > 0.0, lambda: band_min_max(lo, hi),
                              lambda: (lo, lo))
        single = bmin == bmax
        split = over & (n_lo - n_hi == 2.0) & jnp.logical_not(single)
        unsplit = over & jnp.logical_not(split)
        return jnp.where(split, bmax, lo), jnp.where(over & single, 1.0, 0.0), jnp.where(unsplit, 1.0, 0.0)

    zero = jnp.zeros((tq, 1), F32)
    state1 = search((jnp.int32(0), lo0, hi0, n_causal, zero), IDX_TIE_CHECK, zero)
    first = finish(state1)
    state = search(state1, iters, first[1])
    lo, _, unsplit = lax.cond(state[0] > state1[0], lambda: finish(state), lambda: first)
    hi, n_hi = state[2], state[4]
    tied = jnp.max(unsplit) > 0.0

    def write(sel):
        def body(j, c):
            for rows, idx, off in blocks(j):
                mb_ref[idx] = jnp.where(sel(rows, sc_ref[idx], off), 0.0, NEG).astype(mb_ref.dtype)
            return c

        lax.fori_loop(0, nch, body, 0)

    def fill(j, c):
        mb_ref[j] = jnp.full((tq, tk), NEG, mb_ref.dtype)
        return c

    lax.fori_loop(nch, nk, fill, 0)
    lo_b = bcast(lo)

    @pl.when(jnp.logical_not(tied))
    def _():
        write(lambda rows, blk, off: blk >= lo_b(rows))

    @pl.when(tied)
    def _():
        need = kf - n_hi
        tri = tri_ref[...]

        def body(j, base):
            sc = sc_ref[j]
            above = sc >= hi
            band = (sc >= lo) & jnp.logical_not(above)
            prefix = _dot(jnp.where(band, 1.0, 0.0).astype(BF16), tri) + base
            keep = above | (band & (prefix <= need))
            mb_ref[j] = jnp.where(keep, 0.0, NEG).astype(mb_ref.dtype)
            return prefix[:, tk - 1:tk]

        lax.fori_loop(0, nch, body, zero)


def _dsa_index_kernel_t(iq_ref, ik_ref, iw_ref, tri_ref, mb_ref, sc_ref, *, tq, tk, S, topk, iters):
    t0 = pl.program_id(1) * tq
    nk = S // tk
    nch = (t0 + tq - 1) // tk + 1
    hw = IDX_CHUNK
    iq = iq_ref[...]
    lane = lax.broadcasted_iota(jnp.int32, (tq, LANE), 1)
    qs = []
    for h in range(IDX_HEADS):
        blk = iq[:, (h // 2) * LANE:(h // 2 + 1) * LANE]
        keep = (lane >= IDX_DIM) if h % 2 else (lane < IDX_DIM)
        qs.append(jnp.where(keep, blk, jnp.zeros_like(blk)))
    q8 = jnp.concatenate(qs, axis=0)
    iwt = iw_ref[...].astype(F32).T
    wrow = [iwt[h:h + 1, :] for h in range(IDX_HEADS)]
    qpos = lax.broadcasted_iota(jnp.int32, (hw, tq), 1) + t0
    krow = lax.broadcasted_iota(jnp.int32, (hw, tq), 0)

    def fold(x, op):
        out = x[0:8]
        for r in range(1, x.shape[0] // 8):
            out = op(out, x[r * 8:(r + 1) * 8])
        return out

    def score_chunk(c, carry):
        mn, mx = carry
        for half in range(tk // hw):
            k0 = pl.multiple_of(c * tk + half * hw, hw)
            res = _dot_nt(ik_ref[pl.ds(k0, hw), :], q8)
            sc = wrow[0] * jnp.maximum(res[:, 0:tq], 0.0)
            for h in range(1, IDX_HEADS):
                sc = sc + wrow[h] * jnp.maximum(res[:, h * tq:(h + 1) * tq], 0.0)
            causal = (krow + k0) <= qpos
            mn = jnp.minimum(mn, fold(jnp.where(causal, sc, -NEG), jnp.minimum))
            sc = jnp.where(causal, sc, NEG)
            mx = jnp.maximum(mx, fold(sc, jnp.maximum))
            sc_ref[c, half * hw:(half + 1) * hw, :] = sc
        return mn, mx

    mn, mx = lax.fori_loop(0, nch, score_chunk, (jnp.full((8, tq), -NEG, F32), jnp.full((8, tq), NEG, F32)))
    n_causal = (lax.broadcasted_iota(jnp.int32, (1, tq), 1) + (t0 + 1)).astype(F32)
    kf = jnp.minimum(float(topk), n_causal)

    def count(pred):
        def body(c, acc):
            return acc + fold(jnp.where(pred(sc_ref[c]), 1.0, 0.0), jnp.add)

        return jnp.sum(lax.fori_loop(0, nch, body, jnp.zeros((8, tq), F32)), axis=0, keepdims=True)

    lo0 = jnp.min(mn, axis=0, keepdims=True)
    mx = jnp.max(mx, axis=0, keepdims=True)
    hi0 = mx + (jnp.abs(mx) * 1e-6 + 1.0)

    def bisect(c):
        lo, hi, n_lo, n_hi = c
        mid = 0.5 * (lo + hi)
        n_mid = count(lambda blk: blk >= mid)
        ge = n_mid >= kf
        return (jnp.where(ge, mid, lo), jnp.where(ge, hi, mid),
                jnp.where(ge, n_mid, n_lo), jnp.where(ge, n_hi, n_mid))

    def open_rows(n_lo, n_hi, settled):
        return (n_lo - kf) * jnp.where(n_lo - n_hi > 2.0, 1.0, 0.0) * (1.0 - settled)

    def search(state, limit, settled):
        def cond(c):
            return jnp.logical_and(c[0] < limit, jnp.max(open_rows(c[3], c[4], settled)) > 0.0)

        def body(c):
            return (c[0] + 2,) + bisect(bisect(c[1:]))

        return lax.while_loop(cond, body, state)

    def band_min_max(lo, hi):
        def body(c, carry):
            blk = sc_ref[c]
            inside = (blk >= lo) & (blk < hi)
            return (jnp.minimum(carry[0], fold(jnp.where(inside, blk, -NEG), jnp.minimum)),
                    jnp.maximum(carry[1], fold(jnp.where(inside, blk, NEG), jnp.maximum)))

        bmin, bmax = lax.fori_loop(0, nch, body, (jnp.full((8, tq), -NEG, F32), jnp.full((8, tq), NEG, F32)))
        return jnp.min(bmin, axis=0, keepdims=True), jnp.max(bmax, axis=0, keepdims=True)

    def finish(state):
        _, lo, hi, n_lo, n_hi = state
        over = n_lo > kf
        bmin, bmax = lax.cond(jnp.max(n_lo - kf) > 0.0, lambda: band_min_max(lo, hi), lambda: (lo, lo))
        single = bmin == bmax
        split = over & (n_lo - n_hi == 2.0) & jnp.logical_not(single)
        unsplit = over & jnp.logical_not(split)
        return jnp.where(split, bmax, lo), jnp.where(over & single, 1.0, 0.0), jnp.where(unsplit, 1.0, 0.0)

    zero = jnp.zeros((1, tq), F32)
    state1 = search((jnp.int32(0), lo0, hi0, n_causal, zero), IDX_TIE_CHECK, zero)
    first = finish(state1)
    state = search(state1, iters, first[1])
    lo, _, unsplit = lax.cond(state[0] > state1[0], lambda: finish(state), lambda: first)
    hi, n_hi = state[2], state[4]
    tied = jnp.max(unsplit) > 0.0

    def fill(j, c):
        mb_ref[j] = jnp.full((tq, tk), NEG, mb_ref.dtype)
        return c

    lax.fori_loop(nch, nk, fill, 0)

    def emit(j, keep):
        mb_ref[j] = jnp.where(keep, 0.0, NEG).T.astype(mb_ref.dtype)

    @pl.when(jnp.logical_not(tied))
    def _():
        def body(j, c):
            emit(j, sc_ref[j] >= lo)
            return c

        lax.fori_loop(0, nch, body, 0)

    @pl.when(tied)
    def _():
        need = kf - n_hi
        tri = tri_ref[...]

        def body(j, base):
            sc = sc_ref[j]
            above = sc >= hi
            band = (sc >= lo) & jnp.logical_not(above)
            prefix = _dot(tri, jnp.where(band, 1.0, 0.0).astype(BF16)) + base
            emit(j, above | (band & (prefix <= need)))
            return prefix[tk - 1:tk, :]

        lax.fori_loop(0, nch, body, zero)


def dsa_index_mask(proj, *, B, S, tqi, tk, topk):
    nq = S // tqi
    nk = S // tk
    assert tk % tqi == 0 and tk % IDX_CHUNK == 0
    tri = jnp.asarray(np.tril(np.ones((tk, tk), np.float32)), BF16)
    return pl.pallas_call(
        functools.partial(_dsa_index_kernel_t, tq=tqi, tk=tk, S=S, topk=topk, iters=BISECT_ITERS),
        grid=(B, nq),
        in_specs=[pl.BlockSpec((tqi, 512), lambda b, i: (b * nq + i, OFF_IDX_Q // 512)),
                  pl.BlockSpec((S, LANE), lambda b, i: (b, OFF_IDX_K // LANE)),
                  pl.BlockSpec((tqi, LANE), lambda b, i: (b * nq + i, OFF_IDX_W // LANE)),
                  pl.BlockSpec((tk, tk), lambda b, i: (0, 0))],
        out_specs=pl.BlockSpec((None, nk, tqi, tk), lambda b, i: (b, 0, i, 0)),
        out_shape=jax.ShapeDtypeStruct((B, nk, S, tk), BF16),
        scratch_shapes=[pltpu.VMEM((nk, tk, tqi), F32)],
        compiler_params=_cparams(("parallel", "parallel")),
        name="dsa_index_mask",
    )(proj, proj, proj, tri)


def _dsa_attn_kernel(q_ref, k_ref, v_ref, mb_ref, bias_ref, o_ref, m_ref, acc_ref, *, tq, scale):
    qi = pl.program_id(2)
    hs = range(HPS)
    cols = lambda h: slice(h * LANE, (h + 1) * LANE)
    q = [q_ref[:, cols(h)] for h in hs]
    for h in hs:
        _softmax_init(m_ref.at[h], acc_ref.at[h])

    def step(j, nc, bias):
        sl = pl.ds(j * tq if isinstance(j, int) else pl.multiple_of(j * tq, tq), nc * tq)
        mb = mb_ref[j] if nc == 1 else jnp.concatenate([mb_ref[j], mb_ref[j + 1]], axis=1)
        mb = mb.astype(F32)
        s = [_dot_nt(q[h], k_ref[sl, cols(h)]) + mb for h in hs]
        for h in hs:
            sh = s[h] if bias is None else s[h] + bias(h)
            _softmax_step(sh, v_ref[sl, cols(h)], m_ref.at[h], acc_ref.at[h], scale)

    n_far = jnp.maximum(qi - 1, 0)

    def body(p, c):
        step(2 * p, 2, None)
        return c

    lax.fori_loop(0, n_far // 2, body, 0)

    @pl.when(n_far % 2 == 1)
    def _():
        step(n_far - 1, 1, None)

    @pl.when(qi == 0)
    def _():
        step(0, 1, lambda h: bias_ref[h, :, tq:2 * tq])

    @pl.when(qi >= 1)
    def _():
        step(qi - 1, 2, lambda h: bias_ref[h])

    for h in hs:
        o_ref[:, cols(h)] = _softmax_finish(acc_ref.at[h]).astype(o_ref.dtype)


def dsa_attention(proj, maskb, btiles, *, B, S, tq):
    M = B * S
    nq = S // tq
    H = DSA_HEADS
    scale = DSA_HEAD_DIM ** -0.5
    w = HPS * LANE
    return pl.pallas_call(
        functools.partial(_dsa_attn_kernel, tq=tq, scale=scale),
        grid=(B, H // HPS, nq),
        in_specs=[pl.BlockSpec((tq, w), lambda b, h, i: (b * nq + i, OFF_DSA_Q // w + h)),
                  pl.BlockSpec((S, w), lambda b, h, i: (b, OFF_DSA_K // w + h)),
                  pl.BlockSpec((S, w), lambda b, h, i: (b, OFF_DSA_V // w + h)),
                  pl.BlockSpec((None, nq, tq, tq), lambda b, h, i: (b, 0, i, 0)),
                  pl.BlockSpec((HPS, tq, 2 * tq), lambda b, h, i: (h, 0, 0))],
        out_specs=pl.BlockSpec((tq, w), lambda b, h, i: (b * nq + i, h)),
        out_shape=jax.ShapeDtypeStruct((M, H * DSA_HEAD_DIM), BF16),
        scratch_shapes=[pltpu.VMEM((HPS, tq, LANE), F32), pltpu.VMEM((HPS, tq, 2 * LANE), F32)],
        compiler_params=_cparams(("parallel", "parallel", "arbitrary")),
        name="dsa_attention",
    )(proj, proj, proj, maskb, btiles)


def _nsa_compress_kernel(x_ref, pos_ref, w1_ref, w2_ref, o_ref, *, ncp):
    dk = NSA_HEAD_DIM
    posw = _dot(pos_ref[...], w1_ref[...])[0:1, :]
    for g in range(NSA_GROUPS):
        lo = jnp.zeros((ncp, CMP_HIDDEN), F32)
        hi = jnp.zeros((ncp, CMP_HIDDEN), F32)
        for l in range(CMP_STRIDE):
            xs = x_ref[:, (l * NSA_GROUPS + g) * dk:(l * NSA_GROUPS + g + 1) * dk]
            lo = lo + _dot(xs, w1_ref[l * dk:(l + 1) * dk, :])
            hi = hi + _dot(xs, w1_ref[(CMP_STRIDE + l) * dk:(CMP_STRIDE + l + 1) * dk, :])
        hid = lo + pltpu.roll(hi, ncp - 1, 0) + posw
        o_ref[g] = _dot(jax.nn.gelu(hid).astype(BF16), w2_ref[...]).astype(o_ref.dtype)


def nsa_compress(xkv, posflat, w1, w2, *, B, S):
    ncp = S // CMP_STRIDE
    G, dk = NSA_GROUPS, NSA_HEAD_DIM
    width = CMP_STRIDE * G * dk
    return pl.pallas_call(
        functools.partial(_nsa_compress_kernel, ncp=ncp),
        grid=(2, B),
        in_specs=[pl.BlockSpec((None, ncp, width), lambda a, b: (a, b, 0)),
                  pl.BlockSpec((None, 8, CMP_LEN * dk), lambda a, b: (a, 0, 0)),
                  pl.BlockSpec((None, CMP_LEN * dk, CMP_HIDDEN), lambda a, b: (a, 0, 0)),
                  pl.BlockSpec((None, CMP_HIDDEN, dk), lambda a, b: (a, 0, 0))],
        out_specs=pl.BlockSpec((None, None, G, ncp, dk), lambda a, b: (a, b, 0, 0, 0)),
        out_shape=jax.ShapeDtypeStruct((2, B, G, ncp, dk), BF16),
        compiler_params=_cparams(("parallel", "parallel")),
        name="nsa_compress",
    )(xkv, posflat, w1, w2)


def _nsa_cmp_kernel(q_ref, kc_ref, vc_ref, ov_ref, oc_ref, sel_ref, *, tq, ncp, ns, n_sel, scale):
    t0 = pl.program_id(2) * tq
    dk = NSA_HEAD_DIM
    kc = kc_ref[...]
    vc = vc_ref[...]
    trow = lax.broadcasted_iota(jnp.int32, (tq, ncp), 0) + t0
    ccol = lax.broadcasted_iota(jnp.int32, (tq, ncp), 1)
    vis = (ccol * CMP_STRIDE + (CMP_LEN - 1)) <= trow
    psum = jnp.zeros((tq, ncp), F32)
    for h in range(NSA_HG):
        s = jnp.where(vis, _dot_nt(q_ref[:, h * dk:(h + 1) * dk], kc) * scale, NEG)
        m = jnp.max(s, axis=-1, keepdims=True)
        p = jnp.where(vis, jnp.exp(s - m), 0.0)
        d = jnp.sum(p, axis=-1, keepdims=True)
        p = p / jnp.where(d > 0, d, 1.0)
        oc_ref[:, h * dk:(h + 1) * dk] = _dot(p.astype(BF16), vc).astype(oc_ref.dtype)
        psum = psum + p
    p_hi = psum.astype(BF16)
    p_lo = (psum - p_hi.astype(F32)).astype(BF16)
    ov = ov_ref[...]
    imp = _dot_nt(ov, p_hi) + _dot_nt(ov, p_lo)
    imp = imp[:ns]
    jrow = lax.broadcasted_iota(jnp.int32, (ns, tq), 0)
    tcol = lax.broadcasted_iota(jnp.int32, (ns, tq), 1) + t0
    blk_t = tcol // SEL_BLOCK
    forced = (jrow == 0) | (jrow == blk_t) | (jrow == blk_t - 1)
    val = jnp.where(forced, -NEG, imp)
    val = jnp.where(jrow * SEL_BLOCK > tcol, NEG, val)
    rank = jnp.zeros((ns, tq), F32)
    for j in range(ns):
        other = val[j:j + 1, :]
        ahead = (other > val) | ((other == val) & (jrow > j))
        rank = rank + jnp.where(ahead, 1.0, 0.0)
    selneg = jnp.where(rank < float(n_sel), 0.0, NEG)
    if ns < LANE:
        selneg = jnp.concatenate([selneg, jnp.zeros((LANE - ns, tq), F32)], axis=0)
    sel_ref[...] = selneg.T.astype(sel_ref.dtype)


def nsa_cmp_attention(proj, kvc, ovt, *, B, S, tq):
    M = B * S
    nq = S // tq
    G, dk = NSA_GROUPS, NSA_HEAD_DIM
    ncp = S // CMP_STRIDE
    ns = S // SEL_BLOCK
    assert ns <= LANE
    n_sel = min(SEL_COUNT, ns)
    gw = NSA_HG * dk
    return pl.pallas_call(
        functools.partial(_nsa_cmp_kernel, tq=tq, ncp=ncp, ns=ns, n_sel=n_sel, scale=dk ** -0.5),
        grid=(B, G, nq),
        in_specs=[pl.BlockSpec((tq, gw), lambda b, g, i: (b * nq + i, OFF_NSA_Q // gw + g)),
                  pl.BlockSpec((None, None, None, ncp, dk), lambda b, g, i: (0, b, g, 0, 0)),
                  pl.BlockSpec((None, None, None, ncp, dk), lambda b, g, i: (1, b, g, 0, 0)),
                  pl.BlockSpec((LANE, ncp), lambda b, g, i: (0, 0))],
        out_specs=[pl.BlockSpec((tq, gw), lambda b, g, i: (b * nq + i, g)),
                   pl.BlockSpec((None, None, tq, LANE), lambda b, g, i: (b, g, i, 0))],
        out_shape=[jax.ShapeDtypeStruct((M, G * gw), BF16),
                   jax.ShapeDtypeStruct((B, G, S, LANE), BF16)],
        compiler_params=_cparams(("parallel", "parallel", "parallel")),
        name="nsa_cmp_attention",
    )(proj, kvc, kvc, ovt)


def _nsa_main_kernel(q_ref, sel_ref, ks_ref, vs_ref, kw_ref, vw_ref, ind_ref, bias_ref, gate_ref, oc_ref,
                     o_ref, m_ref, acc_ref, *, tq, nback, scale):
    g = pl.program_id(1)
    qi = pl.program_id(2)
    dk = NSA_HEAD_DIM
    HG = NSA_HG
    ns = NSA_STREAMS
    hp = HG // ns
    st = range(ns)
    stack = lambda f: [jnp.concatenate([f(a * hp + i) for i in range(hp)], axis=0) for a in st]
    q = stack(lambda h: q_ref[:, h * dk:(h + 1) * dk])
    sel = jnp.concatenate([sel_ref[...]] * hp, axis=0)
    qa = [jnp.concatenate([q[a], sel], axis=1) for a in st]
    gate = jax.nn.sigmoid(gate_ref[...].astype(F32))

    def init():
        for a in st:
            _softmax_init(m_ref.at[a], acc_ref.at[a])

    def steps(s, v, fix):
        for a in st:
            _softmax_step(fix(a, s[a]), v, m_ref.at[a], acc_ref.at[a], scale)

    wspan = nback + 1

    def near_step(scores, first, nc):
        s, v = scores(first, nc * tq)
        steps(s, v, lambda a, x: x + bias_ref[a, :, (wspan - nc) * tq:wspan * tq])

    def near(scores, nspan):
        for nc in range(1, nspan):
            @pl.when(qi == nc - 1)
            def _(nc=nc):
                near_step(scores, 0, nc)

        @pl.when(qi >= nspan - 1)
        def _():
            near_step(scores, qi - (nspan - 1), nspan)

    init()

    def rows_from(j, width):
        return pl.ds(j * tq if isinstance(j, int) else pl.multiple_of(j * tq, tq), width)

    def sel_scores(j, width=tq):
        sl = rows_from(j, width)
        ka = jnp.concatenate([ks_ref[sl, :], ind_ref[sl, :]], axis=1)
        return [_dot_nt(qa[a], ka) for a in st], vs_ref[sl, :]

    n_far = jnp.maximum(qi - 1, 0)

    def sel_body(p, c):
        s, v = sel_scores(2 * p, 2 * tq)
        steps(s, v, lambda a, x: x)
        return c

    lax.fori_loop(0, n_far // 2, sel_body, 0)

    @pl.when(n_far % 2 == 1)
    def _():
        s, v = sel_scores(n_far - 1)
        steps(s, v, lambda a, x: x)

    near(sel_scores, 2)
    o_s =[_softmax_finish(acc_ref.at[a]) for a in st]

    init()

    def win_scores(j, width):
        sl = rows_from(j, width)
        kw = kw_ref[sl, :]
        return [_dot_nt(q[a], kw) for a in st], vw_ref[sl, :]

    near(win_scores, wspan)
    o_w =[_softmax_finish(acc_ref.at[a]) for a in st]

    lane = lax.broadcasted_iota(jnp.int32, gate.shape, 1)
    for h in range(HG):
        c = g * HG + h
        a, sl = h // hp, slice((h % hp) * tq, (h % hp + 1) * tq)
        gc = [jnp.sum(jnp.where(lane == br * NSA_HEADS + c, gate, 0.0), axis=-1, keepdims=True)
              for br in range(3)]
        o = gc[0] * oc_ref[:, h * dk:(h + 1) * dk].astype(F32) + gc[1] * o_s[a][sl] + gc[2] * o_w[a][sl]
        o_ref[:, h * dk:(h + 1) * dk] = o.astype(o_ref.dtype)


def nsa_main(proj, selneg, ind, btiles, o_c, *, B, S, tq):
    M = B * S
    nq = S // tq
    G, dk, HG = NSA_GROUPS, NSA_HEAD_DIM, NSA_HG
    gw = HG * dk
    assert WINDOW % tq == 0 and WINDOW // tq >= 2
    nback = WINDOW // tq
    ns, hp = NSA_STREAMS, HG // NSA_STREAMS
    kv = lambda n: pl.BlockSpec((S, dk), lambda b, g, i, n=n: (b, (OFF_NSA_KV + n * G * dk) // dk + g))
    return pl.pallas_call(
        functools.partial(_nsa_main_kernel, tq=tq, nback=nback, scale=dk ** -0.5),
        grid=(B, G, nq),
        in_specs=[pl.BlockSpec((tq, gw), lambda b, g, i: (b * nq + i, OFF_NSA_Q // gw + g)),
                  pl.BlockSpec((None, None, tq, LANE), lambda b, g, i: (b, g, i, 0)),
                  kv(2), kv(3), kv(4), kv(5),
                  pl.BlockSpec((S, LANE), lambda b, g, i: (0, 0)),
                  pl.BlockSpec((ns, hp * tq, (nback + 1) * tq), lambda b, g, i: (g, 0, 0)),
                  pl.BlockSpec((tq, LANE), lambda b, g, i: (b * nq + i, OFF_NSA_GATE // LANE)),
                  pl.BlockSpec((tq, gw), lambda b, g, i: (b * nq + i, g))],
        out_specs=pl.BlockSpec((tq, gw), lambda b, g, i: (b * nq + i, g)),
        out_shape=jax.ShapeDtypeStruct((M, G * gw), BF16),
        scratch_shapes=[pltpu.VMEM((NSA_STREAMS, HG // NSA_STREAMS * tq, LANE), F32),
                        pltpu.VMEM((NSA_STREAMS, HG // NSA_STREAMS * tq, 2 * LANE), F32)],
        compiler_params=_cparams(("parallel", "parallel", "arbitrary")),
        name="nsa_main",
    )(proj, selneg, proj, proj, proj, proj, ind, btiles, proj, o_c)


def _merge_kernel(x_ref, oa_ref, ob_ref, oc_ref, g0_ref, g1_ref, g2_ref, wb_ref, wo_ref, gain_ref, o_ref):
    merged = None
    for o_r, g_r, br in ((oa_ref, g0_ref, 0), (ob_ref, g1_ref, 1), (oc_ref, g2_ref, 2)):
        t = jax.nn.sigmoid(g_r[...].astype(F32)) * _dot(o_r[...], wb_ref[br])
        merged = t if merged is None else merged + t
    y = _dot(merged.astype(BF16), wo_ref[...])
    o_ref[...] = x_ref[...] + _rms(y, gain_ref[...])


def merge_branches(x, o_a, o_b, o_c, proj, wb, wo, gain, *, tm):
    M, D = x.shape
    row = pl.BlockSpec((tm, D), lambda i: (i, 0))
    gate = lambda br: pl.BlockSpec((tm, D), lambda i, br=br: (i, OFF_BGATE // D + br))
    return pl.pallas_call(
        _merge_kernel,
        grid=(M // tm,),
        in_specs=[row, row, row, row, gate(0), gate(1), gate(2),
                  pl.BlockSpec(wb.shape, lambda i: (0, 0, 0)),
                  pl.BlockSpec(wo.shape, lambda i: (0, 0)),
                  pl.BlockSpec((1, D), lambda i: (0, 0))],
        out_specs=row,
        out_shape=jax.ShapeDtypeStruct((M, D), F32),
        compiler_params=_cparams(("parallel",)),
        name="merge_branches",
    )(x, o_a, o_b, o_c, proj, proj, proj, wb, wo, gain)


def _cross_kernel(x_ref, kv_ref, wq_ref, wo_ref, gpre_ref, gpost_ref, o_ref, *, scale):
    x = x_ref[...]
    h = _rms(x, gpre_ref[...]).astype(BF16)
    q = _dot(h, wq_ref[...]).astype(BF16)
    dh = CROSS_HEAD_DIM
    outs = []
    for hd in range(CROSS_HEADS):
        k = kv_ref[:, hd * dh:(hd + 1) * dh]
        v = kv_ref[:, D_MODEL + hd * dh:D_MODEL + (hd + 1) * dh]
        s = _dot_nt(q[:, hd * dh:(hd + 1) * dh], k) * scale
        p = jnp.exp(s - jnp.max(s, axis=-1, keepdims=True))
        p = p / jnp.sum(p, axis=-1, keepdims=True)
        outs.append(_dot(p.astype(BF16), v).astype(BF16))
    y = _dot(jnp.concatenate(outs, axis=1), wo_ref[...])
    o_ref[...] = x + _rms(y, gpost_ref[...])


def cross_attention(x, kv, wq, wo, gpre, gpost, *, S, tm):
    M, D = x.shape
    mlen = kv.shape[0] // (M // S)
    per_b = S // tm
    return pl.pallas_call(
        functools.partial(_cross_kernel, scale=CROSS_HEAD_DIM ** -0.5),
        grid=(M // tm,),
        in_specs=[pl.BlockSpec((tm, D), lambda i: (i, 0)),
                  pl.BlockSpec((mlen, 2 * D), lambda i: (i // per_b, 0)),
                  pl.BlockSpec(wq.shape, lambda i: (0, 0)),
                  pl.BlockSpec(wo.shape, lambda i: (0, 0)),
                  pl.BlockSpec((1, D), lambda i: (0, 0)),
                  pl.BlockSpec((1, D), lambda i: (0, 0))],
        out_specs=pl.BlockSpec((tm, D), lambda i: (i, 0)),
        out_shape=jax.ShapeDtypeStruct((M, D), F32),
        compiler_params=_cparams(("parallel",)),
        name="cross_attention",
    )(x, kv, wq, wo, gpre, gpost)


def _mlp_kernel(x_ref, w1_ref, w2_ref, gpre_ref, gpost_ref, o_ref, h_ref, acc_ref):
    j = pl.program_id(1)

    @pl.when(j == 0)
    def _():
        h_ref[...] = _rms(x_ref[...], gpre_ref[...]).astype(BF16)
        acc_ref[...] = jnp.zeros(acc_ref.shape, F32)

    a = jnp.maximum(_dot(h_ref[...], w1_ref[...]), 0.0)
    acc_ref[...] += _dot((a * a).astype(BF16), w2_ref[...])

    @pl.when(j == pl.num_programs(1) - 1)
    def _():
        o_ref[...] = x_ref[...] + _rms(acc_ref[...], gpost_ref[...])


def mlp(x, w1, w2, gpre, gpost, *, tm, tf):
    M, D = x.shape
    FF = w1.shape[1]
    return pl.pallas_call(
        _mlp_kernel,
        grid=(M // tm, FF // tf),
        in_specs=[pl.BlockSpec((tm, D), lambda i, j: (i, 0)),
                  pl.BlockSpec((D, tf), lambda i, j: (0, j)),
                  pl.BlockSpec((tf, D), lambda i, j: (j, 0)),
                  pl.BlockSpec((1, D), lambda i, j: (0, 0)),
                  pl.BlockSpec((1, D), lambda i, j: (0, 0))],
        out_specs=pl.BlockSpec((tm, D), lambda i, j: (i, 0)),
        out_shape=jax.ShapeDtypeStruct((M, D), F32),
        scratch_shapes=[pltpu.VMEM((tm, D), BF16), pltpu.VMEM((tm, D), F32)],
        compiler_params=_cparams(("parallel", "arbitrary")),
        name="mlp",
    )(x, w1, w2, gpre, gpost)


def _rot_half_cols(w):
    half = w.shape[-1] // 2
    return jnp.concatenate([-w[..., half:], w[..., :half]], axis=-1)


def _pack_w_in(w):
    widths = (MLA_Q_LORA, MLA_KV_LORA, MLA_ROPE, 1024, 1024, 1024, IDX_HEADS * IDX_DIM, IDX_DIM, IDX_HEADS,
              1024, 256, 256, 256, 256, 256, 256, 3 * NSA_HEADS, N_BRANCH * D_MODEL)
    offs = np.concatenate([[0], np.cumsum(widths)])
    seg = [w[:, offs[i]:offs[i + 1]] for i in range(len(widths))]
    (q_lat, kv_lat, k_rope, dsa_q, dsa_k, dsa_v, idx_q, idx_k, idx_w,
     nsa_q, kc, vc, ks, vs, kw, vw, nsa_gate, bgate) = seg
    z = lambda n: jnp.zeros((w.shape[0], n), w.dtype)
    cols = [dsa_q, dsa_k, dsa_v, nsa_q, bgate, idx_q, kc, vc, ks, vs, kw, vw,
            q_lat, kv_lat, k_rope, z(LANE - MLA_ROPE), _rot_half_cols(k_rope), z(LANE - MLA_ROPE), z(LANE),
            idx_k, idx_k, idx_w, z(LANE - IDX_HEADS), nsa_gate, z(LANE - 3 * NSA_HEADS)]
    out = jnp.concatenate(cols, axis=1)
    out = jnp.concatenate([out, z(N_PACK - out.shape[1])], axis=1)
    return out.astype(BF16)


def _pack_mla_weights(w_uq, w_ukv):
    H = MLA_HEADS
    uq = w_uq.reshape(MLA_Q_LORA, H, MLA_NOPE + MLA_ROPE)
    nope = uq[:, :, :MLA_NOPE].reshape(MLA_Q_LORA, H * MLA_NOPE)
    rope = uq[:, :, MLA_NOPE:]
    pad = jnp.zeros((MLA_Q_LORA, H, LANE - MLA_ROPE), w_uq.dtype)
    wqr = jnp.concatenate([rope, pad], axis=-1).reshape(MLA_Q_LORA, H * LANE)
    wqrr = jnp.concatenate([_rot_half_cols(rope), pad], axis=-1).reshape(MLA_Q_LORA, H * LANE)
    ukv = w_ukv.reshape(MLA_KV_LORA, H, MLA_NOPE + MLA_V)
    wkv = jnp.concatenate([ukv[:, :, :MLA_NOPE].reshape(MLA_KV_LORA, H * MLA_NOPE),
                           ukv[:, :, MLA_NOPE:].reshape(MLA_KV_LORA, H * MLA_V)], axis=1)
    return nope.astype(BF16), wqr.astype(BF16), wqrr.astype(BF16), wkv.astype(BF16)


def _constants(S):
    half = MLA_ROPE // 2
    inv_freq = ROPE_BASE ** (-np.arange(0, MLA_ROPE, 2, dtype=np.float32) / MLA_ROPE)
    freq = np.zeros((1, LANE), np.float32)
    freq[0, :half] = inv_freq
    freq[0, half:2 * half] = inv_freq
    ncp = S // CMP_STRIDE
    ns = S // SEL_BLOCK
    c_start = np.arange(ncp) * CMP_STRIDE
    j_start = np.arange(ns) * SEL_BLOCK
    overlap = ((c_start[None, :] < j_start[:, None] + SEL_BLOCK) &
               (c_start[None, :] + CMP_LEN > j_start[:, None])).astype(np.float32)
    overlap[:, ncp - 1] = 0.0
    ovt = np.zeros((LANE, ncp), np.float32)
    ovt[:ns] = overlap
    ind = np.zeros((S, LANE), np.float32)
    ind[np.arange(S), np.arange(S) // SEL_BLOCK] = 1.0
    return jnp.asarray(freq), jnp.asarray(ovt, BF16), jnp.asarray(ind, BF16)


def kernel(x, mem, positions, rel_bias, norm_gains, w_in, mla_q_norm, mla_kv_norm, mla_w_uq, mla_w_ukv,
           nsa_cmp_pos, nsa_cmp_w1, nsa_cmp_w2, w_branch, w_out, cross_wq, cross_wkv, cross_wo, mlp_w1, mlp_w2):
    B, S, D = x.shape
    M = B * S
    depth = w_in.shape[0]
    tq = TQ if S % TQ == 0 else TQ_NSA
    tqn = TQ_NSA
    assert S % tq == 0 and S % tqn == 0 and D == D_MODEL
    topk = min(DSA_TOPK_MAX, S // 4)
    freq, ovt, ind = _constants(S)
    btiles = bias_tiles(rel_bias, tq, 0, DSA_HEADS, nspan=2, masked=False, stack=1)
    btiles_n = bias_tiles(rel_bias, tqn, DSA_HEADS, NSA_HEADS, nspan=WINDOW // tqn + 1, masked=True,
                          stack=NSA_HG // NSA_STREAMS)
    pos = positions.reshape(M, 1)
    xf = x.reshape(M, D)
    memf = mem.reshape(B * mem.shape[1], D)
    tm_big = 1024 if M % 1024 == 0 else tq
    tm = 512 if M % 512 == 0 else tq
    ncp = S // CMP_STRIDE
    row = lambda v: v.reshape(1, -1)

    for l in range(depth):
        g = norm_gains[l]
        proj = norm_matmul(xf, row(g[0]), _pack_w_in(w_in[l]), tm=tm_big, tn=TN_IN)

        wqn, wqr, wqrr, wkv = _pack_mla_weights(mla_w_uq[l], mla_w_ukv[l])
        qn, qr, kn, vv, kr = mla_prep(proj, pos, freq, row(mla_q_norm[l]), row(mla_kv_norm[l]),
                                      wqn, wqr, wqrr, wkv, tm=tm)
        o_a = mla_attention(qn, qr, kn, kr, vv, B=B, S=S, tq=tq)

        maskb = dsa_index_mask(proj, B=B, S=S, tqi=TQ_IDX, tk=tq, topk=topk)
        o_b = dsa_attention(proj, maskb, btiles, B=B, S=S, tq=tq)

        kvc_in = jnp.stack([
            proj[:, OFF_NSA_KV + a * 256:OFF_NSA_KV + (a + 1) * 256].reshape(B * ncp, CMP_STRIDE * 256)
            for a in range(2)])
        posflat = jnp.broadcast_to(nsa_cmp_pos[l].reshape(2, 1, CMP_LEN * NSA_HEAD_DIM),
                                   (2, 8, CMP_LEN * NSA_HEAD_DIM)).astype(BF16)
        kvc = nsa_compress(kvc_in, posflat, nsa_cmp_w1[l].astype(BF16), nsa_cmp_w2[l].astype(BF16), B=B, S=S)
        o_cmp, selneg = nsa_cmp_attention(proj, kvc, ovt, B=B, S=S, tq=tqn)
        o_c = nsa_main(proj, selneg, ind, btiles_n, o_cmp, B=B, S=S, tq=tqn)

        xf = merge_branches(xf, o_a, o_b, o_c, proj, w_branch[l].astype(BF16), w_out[l].astype(BF16),
                            row(g[1]), tm=tm)

        mkv = norm_matmul(memf, row(g[3]), cross_wkv[l].astype(BF16), tm=memf.shape[0] // B, tn=1024)
        xf = cross_attention(xf, mkv, cross_wq[l].astype(BF16), cross_wo[l].astype(BF16),
                             row(g[2]), row(g[4]), S=S, tm=tm)

        xf = mlp(xf, mlp_w1[l].astype(BF16), mlp_w2[l].astype(BF16), row(g[5]), row(g[6]), tm=tm_big, tf=1024)

    return xf.reshape(B, S, D)
```

```python
import functools
import math

import numpy as np
import jax
import jax.numpy as jnp
from jax import lax
from jax.experimental import pallas as pl
from jax.experimental.pallas import tpu as pltpu

F32 = jnp.float32
BF16 = jnp.bfloat16

D_MODEL = 1024
EPS = 1e-6
MLA_HEADS = 8
MLA_Q_LORA = 384
MLA_KV_LORA = 256
MLA_NOPE = 128
MLA_ROPE = 64
MLA_V = 128
ROPE_BASE = 10000.0
DSA_HEADS = 8
DSA_HEAD_DIM = 128
IDX_HEADS = 8
IDX_DIM = 64
DSA_TOPK_MAX = 256
NSA_HEADS = 8
NSA_GROUPS = 2
NSA_HG = NSA_HEADS // NSA_GROUPS
NSA_HEAD_DIM = 128
CMP_STRIDE = 16
CMP_LEN = 2 * CMP_STRIDE
CMP_HIDDEN = 256
SEL_BLOCK = 64
SEL_COUNT = 16
WINDOW = 512
REL_BUCKETS = 32
REL_MAX_DIST = 128
REL_HEADS = DSA_HEADS + NSA_HEADS
CROSS_HEADS = 4
CROSS_HEAD_DIM = D_MODEL // CROSS_HEADS
D_FF = 4 * D_MODEL
N_BRANCH = 3

LANE = 128
NEG = -1e30
VMEM_LIMIT = 48 * 1024 * 1024

TQ = 512
TQ_NSA = 256
HPS = 2
NSA_STREAMS = 2
BISECT_ITERS = 32
IDX_TIE_CHECK = 16
TQ_IDX = 512
IDX_CHUNK = 256
IDX_ROWS = 64

OFF_DSA_Q = 0
OFF_DSA_K = 1024
OFF_DSA_V = 2048
OFF_NSA_Q = 3072
OFF_BGATE = 4096
OFF_IDX_Q = 7168
OFF_NSA_KV = 7680
OFF_MLA = 9216
OFF_IDX_K = 10240
OFF_IDX_W = 10368
OFF_NSA_GATE = 10496
N_PACK = 10752
TN_IN = 1536


def _cparams(sem):
    return pltpu.CompilerParams(dimension_semantics=sem, vmem_limit_bytes=VMEM_LIMIT)


def _rms(x, g):
    return x * lax.rsqrt(jnp.mean(x * x, axis=-1, keepdims=True) + EPS) * g


def _dot(a, b):
    return jnp.dot(a, b, preferred_element_type=F32)


def _dot_nt(a, b):
    return lax.dot_general(a, b, (((1,), (1,)), ((), ())), preferred_element_type=F32)


def _norm_mm_kernel(x_ref, g_ref, w_ref, o_ref, xn_ref):
    @pl.when(pl.program_id(1) == 0)
    def _():
        xn_ref[...] = _rms(x_ref[...].astype(F32), g_ref[...]).astype(BF16)

    o_ref[...] = _dot(xn_ref[...], w_ref[...]).astype(o_ref.dtype)


def norm_matmul(x, g, w, *, tm, tn):
    M, K = x.shape
    N = w.shape[1]
    return pl.pallas_call(
        _norm_mm_kernel,
        grid=(M // tm, N // tn),
        in_specs=[pl.BlockSpec((tm, K), lambda i, j: (i, 0)),
                  pl.BlockSpec((1, K), lambda i, j: (0, 0)),
                  pl.BlockSpec((K, tn), lambda i, j: (0, j))],
        out_specs=pl.BlockSpec((tm, tn), lambda i, j: (i, j)),
        out_shape=jax.ShapeDtypeStruct((M, N), BF16),
        scratch_shapes=[pltpu.VMEM((tm, K), BF16)],
        compiler_params=_cparams(("parallel", "arbitrary")),
        name="norm_matmul",
    )(x, g, w)


def _bias_tiles_kernel(tab_ref, bkt_ref, o_ref, *, inv_scale, h0, tq, nspan, masked):
    h = pl.program_id(0) + h0
    bkt = bkt_ref[...]
    far = tab_ref[REL_BUCKETS - 1, h]
    out = jnp.zeros(bkt.shape, F32)
    for b in range(REL_BUCKETS - 1):
        out = jnp.where(bkt == b, (tab_ref[b, h] - far) * inv_scale, out)
    diag, prev = out[0], out[1]
    r, c = _tri(tq, tq)
    if masked:
        diag = jnp.where(r >= c, diag, NEG)
    o_ref[:, (nspan - 1) * tq:nspan * tq] = diag
    o_ref[:, (nspan - 2) * tq:(nspan - 1) * tq] = prev
    for k in range(nspan - 2):
        edge = jnp.where(r < c, 0.0, NEG) if (masked and k == 0) else jnp.zeros((tq, tq), F32)
        o_ref[:, k * tq:(k + 1) * tq] = edge


def _bucket_tiles(tq):
    i = np.arange(tq)[:, None]
    j = np.arange(tq)[None, :]
    d = np.stack([i - j, tq + i - j]).astype(np.int32)
    n = jnp.maximum(jnp.asarray(d), 0)
    exact = REL_BUCKETS // 2
    nf = jnp.maximum(n, 1).astype(F32)
    log_b = exact + (jnp.log(nf / exact) / math.log(REL_MAX_DIST / exact) * (REL_BUCKETS - exact)).astype(jnp.int32)
    return jnp.where(n < exact, n, jnp.minimum(log_b, REL_BUCKETS - 1)).astype(jnp.int32)


def bias_tiles(rel_bias, tq, h0, nh, *, nspan, masked, stack):
    assert tq >= REL_MAX_DIST
    bkt = _bucket_tiles(tq)
    assert DSA_HEAD_DIM == NSA_HEAD_DIM
    return pl.pallas_call(
        functools.partial(_bias_tiles_kernel, inv_scale=DSA_HEAD_DIM ** 0.5, h0=h0, tq=tq, nspan=nspan,
                          masked=masked),
        grid=(nh,),
        in_specs=[pl.BlockSpec(memory_space=pltpu.SMEM),
                  pl.BlockSpec((2, tq, tq), lambda h: (0, 0, 0))],
        out_specs=pl.BlockSpec((None, tq, nspan * tq), lambda h: (h // stack, h % stack, 0)),
        out_shape=jax.ShapeDtypeStruct((nh // stack, stack * tq, nspan * tq), F32),
        compiler_params=_cparams(("arbitrary",)),
        name="bias_tiles",
    )(rel_bias, bkt)


def _mla_prep_kernel(lat_ref, pos_ref, freq_ref, qg_ref, kvg_ref, wqn_ref, wqr_ref, wqrr_ref, wkv_ref,
                     qn_ref, qr_ref, kn_ref, v_ref, kr_ref):
    lat = lat_ref[...].astype(F32)
    qlat = _rms(lat[:, :MLA_Q_LORA], qg_ref[...]).astype(BF16)
    kvlat = _rms(lat[:, MLA_Q_LORA:MLA_Q_LORA + MLA_KV_LORA], kvg_ref[...]).astype(BF16)
    o = MLA_Q_LORA + MLA_KV_LORA
    kr = lat[:, o:o + LANE]
    krr = lat[:, o + LANE:o + 2 * LANE]
    ang = pos_ref[...].astype(F32) * freq_ref[...]
    cos, sin = jnp.cos(ang), jnp.sin(ang)
    cos8 = jnp.concatenate([cos] * MLA_HEADS, axis=1)
    sin8 = jnp.concatenate([sin] * MLA_HEADS, axis=1)
    qn_ref[...] = _dot(qlat, wqn_ref[...]).astype(BF16)
    qr_ref[...] = (_dot(qlat, wqr_ref[...]) * cos8 + _dot(qlat, wqrr_ref[...]) * sin8).astype(BF16)
    kv = _dot(kvlat, wkv_ref[...])
    nk = MLA_HEADS * MLA_NOPE
    kn_ref[...] = kv[:, :nk].astype(BF16)
    v_ref[...] = kv[:, nk:].astype(BF16)
    kr_ref[...] = (kr * cos + krr * sin).astype(BF16)


def mla_prep(proj, pos, freq, qg, kvg, wqn, wqr, wqrr, wkv, *, tm):
    M = proj.shape[0]
    HD = MLA_HEADS * LANE
    full = lambda a: pl.BlockSpec(a.shape, lambda i: (0, 0))
    outs = [jax.ShapeDtypeStruct((M, HD), BF16)] * 4 + [jax.ShapeDtypeStruct((M, LANE), BF16)]
    return pl.pallas_call(
        _mla_prep_kernel,
        grid=(M // tm,),
        in_specs=[pl.BlockSpec((tm, 1024), lambda i: (i, OFF_MLA // 1024)),
                  pl.BlockSpec((tm, 1), lambda i: (i, 0)),
                  full(freq), full(qg), full(kvg), full(wqn), full(wqr), full(wqrr), full(wkv)],
        out_specs=[pl.BlockSpec((tm, HD), lambda i: (i, 0))] * 4 + [pl.BlockSpec((tm, LANE), lambda i: (i, 0))],
        out_shape=outs,
        compiler_params=_cparams(("parallel",)),
        name="mla_prep",
    )(proj, pos, freq, qg, kvg, wqn, wqr, wqrr, wkv)


def _softmax_init(m_ref, acc_ref):
    m_ref[...] = jnp.full(m_ref.shape, NEG, F32)
    acc_ref[...] = jnp.zeros(acc_ref.shape, F32)


def _with_ones(v):
    return jnp.concatenate([v, jnp.ones(v.shape, v.dtype)], axis=1)


def _softmax_step(s, v, m_ref, acc_ref, scale):
    c = scale * math.log2(math.e)
    m_prev = m_ref[...]
    m_new = jnp.maximum(m_prev, jnp.max(s, axis=-1, keepdims=True))
    alpha = jnp.exp2((m_prev - m_new) * c)
    p = jnp.exp2((s - jnp.tile(m_new, (1, s.shape[1] // LANE))) * c)
    acc_ref[...] = jnp.tile(alpha, (1, 2)) * acc_ref[...] + _dot(p.astype(BF16), _with_ones(v))
    m_ref[...] = m_new


def _softmax_finish(acc_ref):
    acc = acc_ref[...]
    return acc[:, :LANE] / acc[:, LANE:]


def _tri(tq, tk):
    row = lax.broadcasted_iota(jnp.int32, (tq, tk), 0)
    col = lax.broadcasted_iota(jnp.int32, (tq, tk), 1)
    return row, col


def _mla_attn_kernel(qn_ref, qr_ref, kn_ref, kr_ref, v_ref, o_ref, m_ref, acc_ref, *, tq, scale):
    qi = pl.program_id(2)
    hs = range(HPS)
    q = [jnp.concatenate([qn_ref[:, h * LANE:(h + 1) * LANE], qr_ref[:, h * LANE:(h + 1) * LANE]], axis=-1)
         for h in hs]
    for h in hs:
        _softmax_init(m_ref.at[h], acc_ref.at[h])

    def scores(j, width=tq):
        sl = pl.ds(pl.multiple_of(j * tq, tq), width)
        kr = kr_ref[sl, :]
        return [_dot_nt(q[h], jnp.concatenate([kn_ref[sl, h * LANE:(h + 1) * LANE], kr], axis=-1)) for h in hs], sl

    def far(j, width):
        s, sl = scores(j, width)
        for h in hs:
            _softmax_step(s[h], v_ref[sl, h * LANE:(h + 1) * LANE], m_ref.at[h], acc_ref.at[h], scale)

    def body(p, c):
        far(2 * p, 2 * tq)
        return c

    lax.fori_loop(0, qi // 2, body, 0)

    @pl.when(qi % 2 == 1)
    def _():
        far(qi - 1, tq)

    s, sl = scores(qi)
    row, col = _tri(tq, tq)
    for h in hs:
        _softmax_step(jnp.where(row >= col, s[h], NEG), v_ref[sl, h * LANE:(h + 1) * LANE],
                      m_ref.at[h], acc_ref.at[h], scale)
    for h in hs:
        o_ref[:, h * LANE:(h + 1) * LANE] = _softmax_finish(acc_ref.at[h]).astype(o_ref.dtype)


def mla_attention(qn, qr, kn, kr, v, *, B, S, tq):
    M = B * S
    nq = S // tq
    H = MLA_HEADS
    scale = (MLA_NOPE + MLA_ROPE) ** -0.5
    w = HPS * LANE
    qspec = pl.BlockSpec((tq, w), lambda b, h, i: (b * nq + i, h))
    kspec = pl.BlockSpec((S, w), lambda b, h, i: (b, h))
    return pl.pallas_call(
        functools.partial(_mla_attn_kernel, tq=tq, scale=scale),
        grid=(B, H // HPS, nq),
        in_specs=[qspec, qspec, kspec, pl.BlockSpec((S, LANE), lambda b, h, i: (b, 0)), kspec],
        out_specs=qspec,
        out_shape=jax.ShapeDtypeStruct((M, H * MLA_V), BF16),
        scratch_shapes=[pltpu.VMEM((HPS, tq, LANE), F32), pltpu.VMEM((HPS, tq, 2 * LANE), F32)],
        compiler_params=_cparams(("parallel", "parallel", "arbitrary")),
        name="mla_attention",
    )(qn, qr, kn, kr, v)


def _dsa_index_kernel(iq_ref, ik_ref, iw_ref, tri_ref, mb_ref, sc_ref, *, tq, tk, S, topk, iters):
    t0 = pl.program_id(1) * tq
    nk = S // tk
    nch = (t0 + tq - 1) // tk + 1
    hw = IDX_CHUNK
    nb = tk // LANE
    iq = iq_ref[...]
    iw = iw_ref[...].astype(F32)
    lane = lax.broadcasted_iota(jnp.int32, (tq, LANE), 1)
    qs = []
    for h in range(IDX_HEADS):
        blk = iq[:, (h // 2) * LANE:(h // 2 + 1) * LANE]
        keep = (lane >= IDX_DIM) if h % 2 else (lane < IDX_DIM)
        qs.append(jnp.where(keep, blk, jnp.zeros_like(blk)))
    q8 = jnp.concatenate(qs, axis=0)
    wcol = [iw[:, h:h + 1] for h in range(IDX_HEADS)]
    row = lax.broadcasted_iota(jnp.int32, (tq, hw), 0) + t0
    col0 = lax.broadcasted_iota(jnp.int32, (tq, hw), 1)

    def lane_fold(x, op):
        out = x[:, 0:LANE]
        for b in range(1, x.shape[1] // LANE):
            out = op(out, x[:, b * LANE:(b + 1) * LANE])
        return out

    def score_chunk(c, carry):
        mn, mx = carry
        for half in range(tk // hw):
            k0 = pl.multiple_of(c * tk + half * hw, hw)
            res = _dot_nt(q8, ik_ref[pl.ds(k0, hw), :])
            sc = wcol[0] * jnp.maximum(res[0:tq], 0.0)
            for h in range(1, IDX_HEADS):
                sc = sc + wcol[h] * jnp.maximum(res[h * tq:(h + 1) * tq], 0.0)
            causal = (col0 + k0) <= row
            mn = jnp.minimum(mn, lane_fold(jnp.where(causal, sc, -NEG), jnp.minimum))
            sc = jnp.where(causal, sc, NEG)
            mx = jnp.maximum(mx, lane_fold(sc, jnp.maximum))
            sc_ref[c, :, half * hw:(half + 1) * hw] = sc
        return mn, mx

    mn, mx = lax.fori_loop(0, nch, score_chunk,
                           (jnp.full((tq, LANE), -NEG, F32), jnp.full((tq, LANE), NEG, F32)))
    n_causal = (lax.broadcasted_iota(jnp.int32, (tq, 1), 0) + (t0 + 1)).astype(F32)
    kf = jnp.minimum(float(topk), n_causal)

    rs = IDX_ROWS

    def blocks(j):
        for r in range(tq // rs):
            for b in range(nb):
                rows = slice(r * rs, (r + 1) * rs)
                yield rows, (j, rows, slice(b * LANE, (b + 1) * LANE)), j * tk + b * LANE

    def count(pred):
        def body(c, acc):
            parts = [acc[r * rs:(r + 1) * rs] for r in range(tq // rs)]
            for rows, idx, off in blocks(c):
                r = rows.start // rs
                parts[r] = parts[r] + jnp.where(pred(rows, sc_ref[idx], off), 1.0, 0.0)
            return jnp.concatenate(parts, axis=0)

        acc = lax.fori_loop(0, nch, body, jnp.zeros((tq, LANE), F32))
        return jnp.sum(acc, axis=-1, keepdims=True)

    def bcast(x):
        strips = [jnp.broadcast_to(x[r * rs:(r + 1) * rs], (rs, LANE)) for r in range(tq // rs)]
        return lambda rows: strips[rows.start // rs]

    lo0 = jnp.min(mn, axis=-1, keepdims=True)
    mx = jnp.max(mx, axis=-1, keepdims=True)
    hi0 = mx + (jnp.abs(mx) * 1e-6 + 1.0)

    def bisect(c):
        lo, hi, n_lo, n_hi = c
        mid = 0.5 * (lo + hi)
        mid_b = bcast(mid)
        n_mid = count(lambda rows, blk, off: blk >= mid_b(rows))
        ge = n_mid >= kf
        return (jnp.where(ge, mid, lo), jnp.where(ge, hi, mid),
                jnp.where(ge, n_mid, n_lo), jnp.where(ge, n_hi, n_mid))

    def open_rows(n_lo, n_hi, settled):
        return (n_lo - kf) * jnp.where(n_lo - n_hi > 2.0, 1.0, 0.0) * (1.0 - settled)

    def search(state, limit, settled):
        def cond(c):
            return jnp.logical_and(c[0] < limit, jnp.max(open_rows(c[3], c[4], settled)) > 0.0)

        def body(c):
            return (c[0] + 2,) + bisect(bisect(c[1:]))

        return lax.while_loop(cond, body, state)

    def band_min_max(lo, hi):
        lo_b, hi_b = bcast(lo), bcast(hi)

        def body(c, carry):
            bmin = [carry[0][r * rs:(r + 1) * rs] for r in range(tq // rs)]
            bmax = [carry[1][r * rs:(r + 1) * rs] for r in range(tq // rs)]
            for rows, idx, off in blocks(c):
                r = rows.start // rs
                blk = sc_ref[idx]
                inside = (blk >= lo_b(rows)) & (blk < hi_b(rows))
                bmin[r] = jnp.minimum(bmin[r], jnp.where(inside, blk, -NEG))
                bmax[r] = jnp.maximum(bmax[r], jnp.where(inside, blk, NEG))
            return jnp.concatenate(bmin, axis=0), jnp.concatenate(bmax, axis=0)

        bmin, bmax = lax.fori_loop(0, nch, body,
                                   (jnp.full((tq, LANE), -NEG, F32), jnp.full((tq, LANE), NEG, F32)))
        return jnp.min(bmin, axis=-1, keepdims=True), jnp.max(bmax, axis=-1, keepdims=True)

    def finish(state):
        _, lo, hi, n_lo, n_hi = state
        over = n_lo > kf
        bmin, bmax = lax.cond(jnp.max(n_lo - kf) > 0.0, lambda: band_min_max(lo, hi),
                              lambda: (lo, lo))
        single = bmin == bmax
        split = over & (n_lo - n_hi == 2.0) & jnp.logical_not(single)
        unsplit = over & jnp.logical_not(split)
        return jnp.where(split, bmax, lo), jnp.where(over & single, 1.0, 0.0), jnp.where(unsplit, 1.0, 0.0)

    zero = jnp.zeros((tq, 1), F32)
    state1 = search((jnp.int32(0), lo0, hi0, n_causal, zero), IDX_TIE_CHECK, zero)
    first = finish(state1)
    state = search(state1, iters, first[1])
    lo, _, unsplit = lax.cond(state[0] > state1[0], lambda: finish(state), lambda: first)
    hi, n_hi = state[2], state[4]
    tied = jnp.max(unsplit) > 0.0

    def write(sel):
        def body(j, c):
            for rows, idx, off in blocks(j):
                mb_ref[idx] = jnp.where(sel(rows, sc_ref[idx], off), 0.0, NEG).astype(mb_ref.dtype)
            return c

        lax.fori_loop(0, nch, body, 0)

    def fill(j, c):
        mb_ref[j] = jnp.full((tq, tk), NEG, mb_ref.dtype)
        return c

    lax.fori_loop(nch, nk, fill, 0)
    lo_b = bcast(lo)

    @pl.when(jnp.logical_not(tied))
    def _():
        write(lambda rows, blk, off: blk >= lo_b(rows))

    @pl.when(tied)
    def _():
        need = kf - n_hi
        tri = tri_ref[...]

        def body(j, base):
            sc = sc_ref[j]
            above = sc >= hi
            band = (sc >= lo) & jnp.logical_not(above)
            prefix = _dot(jnp.where(band, 1.0, 0.0).astype(BF16), tri) + base
            keep = above | (band & (prefix <= need))
            mb_ref[j] = jnp.where(keep, 0.0, NEG).astype(mb_ref.dtype)
            return prefix[:, tk - 1:tk]

        lax.fori_loop(0, nch, body, zero)


def _dsa_index_kernel_t(iq_ref, ik_ref, iw_ref, tri_ref, mb_ref, sc_ref, *, tq, tk, S, topk, iters):
    t0 = pl.program_id(1) * tq
    nk = S // tk
    nch = (t0 + tq - 1) // tk + 1
    hw = IDX_CHUNK
    iq = iq_ref[...]
    lane = lax.broadcasted_iota(jnp.int32, (tq, LANE), 1)
    qs = []
    for h in range(IDX_HEADS):
        blk = iq[:, (h // 2) * LANE:(h // 2 + 1) * LANE]
        keep = (lane >= IDX_DIM) if h % 2 else (lane < IDX_DIM)
        qs.append(jnp.where(keep, blk, jnp.zeros_like(blk)))
    q8 = jnp.concatenate(qs, axis=0)
    iwt = iw_ref[...].astype(F32).T
    wrow = [iwt[h:h + 1, :] for h in range(IDX_HEADS)]
    qpos = lax.broadcasted_iota(jnp.int32, (hw, tq), 1) + t0
    krow = lax.broadcasted_iota(jnp.int32, (hw, tq), 0)

    def fold(x, op):
        out = x[0:8]
        for r in range(1, x.shape[0] // 8):
            out = op(out, x[r * 8:(r + 1) * 8])
        return out

    def score_chunk(c, carry):
        mn, mx = carry
        for half in range(tk // hw):
            k0 = pl.multiple_of(c * tk + half * hw, hw)
            res = _dot_nt(ik_ref[pl.ds(k0, hw), :], q8)
            sc = wrow[0] * jnp.maximum(res[:, 0:tq], 0.0)
            for h in range(1, IDX_HEADS):
                sc = sc + wrow[h] * jnp.maximum(res[:, h * tq:(h + 1) * tq], 0.0)
            causal = (krow + k0) <= qpos
            mn = jnp.minimum(mn, fold(jnp.where(causal, sc, -NEG), jnp.minimum))
            sc = jnp.where(causal, sc, NEG)
            mx = jnp.maximum(mx, fold(sc, jnp.maximum))
            sc_ref[c, half * hw:(half + 1) * hw, :] = sc
        return mn, mx

    mn, mx = lax.fori_loop(0, nch, score_chunk, (jnp.full((8, tq), -NEG, F32), jnp.full((8, tq), NEG, F32)))
    n_causal = (lax.broadcasted_iota(jnp.int32, (1, tq), 1) + (t0 + 1)).astype(F32)
    kf = jnp.minimum(float(topk), n_causal)

    def count(pred):
        def body(c, acc):
            return acc + fold(jnp.where(pred(sc_ref[c]), 1.0, 0.0), jnp.add)

        return jnp.sum(lax.fori_loop(0, nch, body, jnp.zeros((8, tq), F32)), axis=0, keepdims=True)

    lo0 = jnp.min(mn, axis=0, keepdims=True)
    mx = jnp.max(mx, axis=0, keepdims=True)
    hi0 = mx + (jnp.abs(mx) * 1e-6 + 1.0)

    def bisect(c):
        lo, hi, n_lo, n_hi = c
        mid = 0.5 * (lo + hi)
        n_mid = count(lambda blk: blk >= mid)
        ge = n_mid >= kf
        return (jnp.where(ge, mid, lo), jnp.where(ge, hi, mid),
                jnp.where(ge, n_mid, n_lo), jnp.where(ge, n_hi, n_mid))

    def open_rows(n_lo, n_hi, settled):
        return (n_lo - kf) * jnp.where(n_lo - n_hi > 2.0, 1.0, 0.0) * (1.0 - settled)

    def search(state, limit, settled):
        def cond(c):
            return jnp.logical_and(c[0] < limit, jnp.max(open_rows(c[3], c[4], settled)) > 0.0)

        def body(c):
            return (c[0] + 2,) + bisect(bisect(c[1:]))

        return lax.while_loop(cond, body, state)

    def band_min_max(lo, hi):
        def body(c, carry):
            blk = sc_ref[c]
            inside = (blk >= lo) & (blk < hi)
            return (jnp.minimum(carry[0], fold(jnp.where(inside, blk, -NEG), jnp.minimum)),
                    jnp.maximum(carry[1], fold(jnp.where(inside, blk, NEG), jnp.maximum)))

        bmin, bmax = lax.fori_loop(0, nch, body, (jnp.full((8, tq), -NEG, F32), jnp.full((8, tq), NEG, F32)))
        return jnp.min(bmin, axis=0, keepdims=True), jnp.max(bmax, axis=0, keepdims=True)

    def finish(state):
        _, lo, hi, n_lo, n_hi = state
        over = n_lo > kf
        bmin, bmax = lax.cond(jnp.max(n_lo - kf) > 0.0, lambda: band_min_max(lo, hi), lambda: (lo, lo))
        single = bmin == bmax
        split = over & (n_lo - n_hi == 2.0) & jnp.logical_not(single)
        unsplit = over & jnp.logical_not(split)
        return jnp.where(split, bmax, lo), jnp.where(over & single, 1.0, 0.0), jnp.where(unsplit, 1.0, 0.0)

    zero = jnp.zeros((1, tq), F32)
    state1 = search((jnp.int32(0), lo0, hi0, n_causal, zero), IDX_TIE_CHECK, zero)
    first = finish(state1)
    state = search(state1, iters, first[1])
    lo, _, unsplit = lax.cond(state[0] > state1[0], lambda: finish(state), lambda: first)
    hi, n_hi = state[2], state[4]
    tied = jnp.max(unsplit) > 0.0

    def fill(j, c):
        mb_ref[j] = jnp.full((tq, tk), NEG, mb_ref.dtype)
        return c

    lax.fori_loop(nch, nk, fill, 0)

    def emit(j, keep):
        mb_ref[j] = jnp.where(keep, 0.0, NEG).T.astype(mb_ref.dtype)

    @pl.when(jnp.logical_not(tied))
    def _():
        def body(j, c):
            emit(j, sc_ref[j] >= lo)
            return c

        lax.fori_loop(0, nch, body, 0)

    @pl.when(tied)
    def _():
        need = kf - n_hi
        tri = tri_ref[...]

        def body(j, base):
            sc = sc_ref[j]
            above = sc >= hi
            band = (sc >= lo) & jnp.logical_not(above)
            prefix = _dot(tri, jnp.where(band, 1.0, 0.0).astype(BF16)) + base
            emit(j, above | (band & (prefix <= need)))
            return prefix[tk - 1:tk, :]

        lax.fori_loop(0, nch, body, zero)


def dsa_index_mask(proj, *, B, S, tqi, tk, topk):
    nq = S // tqi
    nk = S // tk
    assert tk % tqi == 0 and tk % IDX_CHUNK == 0
    tri = jnp.asarray(np.tril(np.ones((tk, tk), np.float32)), BF16)
    return pl.pallas_call(
        functools.partial(_dsa_index_kernel_t, tq=tqi, tk=tk, S=S, topk=topk, iters=BISECT_ITERS),
        grid=(B, nq),
        in_specs=[pl.BlockSpec((tqi, 512), lambda b, i: (b * nq + i, OFF_IDX_Q // 512)),
                  pl.BlockSpec((S, LANE), lambda b, i: (b, OFF_IDX_K // LANE)),
                  pl.BlockSpec((tqi, LANE), lambda b, i: (b * nq + i, OFF_IDX_W // LANE)),
                  pl.BlockSpec((tk, tk), lambda b, i: (0, 0))],
        out_specs=pl.BlockSpec((None, nk, tqi, tk), lambda b, i: (b, 0, i, 0)),
        out_shape=jax.ShapeDtypeStruct((B, nk, S, tk), BF16),
        scratch_shapes=[pltpu.VMEM((nk, tk, tqi), F32)],
        compiler_params=_cparams(("parallel", "parallel")),
        name="dsa_index_mask",
    )(proj, proj, proj, tri)


def _dsa_attn_kernel(q_ref, k_ref, v_ref, mb_ref, bias_ref, o_ref, m_ref, acc_ref, *, tq, scale):
    qi = pl.program_id(2)
    hs = range(HPS)
    cols = lambda h: slice(h * LANE, (h + 1) * LANE)
    q = [q_ref[:, cols(h)] for h in hs]
    for h in hs:
        _softmax_init(m_ref.at[h], acc_ref.at[h])

    def step(j, nc, bias):
        sl = pl.ds(j * tq if isinstance(j, int) else pl.multiple_of(j * tq, tq), nc * tq)
        mb = mb_ref[j] if nc == 1 else jnp.concatenate([mb_ref[j], mb_ref[j + 1]], axis=1)
        mb = mb.astype(F32)
        s = [_dot_nt(q[h], k_ref[sl, cols(h)]) + mb for h in hs]
        for h in hs:
            sh = s[h] if bias is None else s[h] + bias(h)
            _softmax_step(sh, v_ref[sl, cols(h)], m_ref.at[h], acc_ref.at[h], scale)

    n_far = jnp.maximum(qi - 1, 0)

    def body(p, c):
        step(2 * p, 2, None)
        return c

    lax.fori_loop(0, n_far // 2, body, 0)

    @pl.when(n_far % 2 == 1)
    def _():
        step(n_far - 1, 1, None)

    @pl.when(qi == 0)
    def _():
        step(0, 1, lambda h: bias_ref[h, :, tq:2 * tq])

    @pl.when(qi >= 1)
    def _():
        step(qi - 1, 2, lambda h: bias_ref[h])

    for h in hs:
        o_ref[:, cols(h)] = _softmax_finish(acc_ref.at[h]).astype(o_ref.dtype)


def dsa_attention(proj, maskb, btiles, *, B, S, tq):
    M = B * S
    nq = S // tq
    H = DSA_HEADS
    scale = DSA_HEAD_DIM ** -0.5
    w = HPS * LANE
    return pl.pallas_call(
        functools.partial(_dsa_attn_kernel, tq=tq, scale=scale),
        grid=(B, H // HPS, nq),
        in_specs=[pl.BlockSpec((tq, w), lambda b, h, i: (b * nq + i, OFF_DSA_Q // w + h)),
                  pl.BlockSpec((S, w), lambda b, h, i: (b, OFF_DSA_K // w + h)),
                  pl.BlockSpec((S, w), lambda b, h, i: (b, OFF_DSA_V // w + h)),
                  pl.BlockSpec((None, nq, tq, tq), lambda b, h, i: (b, 0, i, 0)),
                  pl.BlockSpec((HPS, tq, 2 * tq), lambda b, h, i: (h, 0, 0))],
        out_specs=pl.BlockSpec((tq, w), lambda b, h, i: (b * nq + i, h)),
        out_shape=jax.ShapeDtypeStruct((M, H * DSA_HEAD_DIM), BF16),
        scratch_shapes=[pltpu.VMEM((HPS, tq, LANE), F32), pltpu.VMEM((HPS, tq, 2 * LANE), F32)],
        compiler_params=_cparams(("parallel", "parallel", "arbitrary")),
        name="dsa_attention",
    )(proj, proj, proj, maskb, btiles)


def _nsa_compress_kernel(x_ref, pos_ref, w1_ref, w2_ref, o_ref, *, ncp):
    dk = NSA_HEAD_DIM
    posw = _dot(pos_ref[...], w1_ref[...])[0:1, :]
    for g in range(NSA_GROUPS):
        lo = jnp.zeros((ncp, CMP_HIDDEN), F32)
        hi = jnp.zeros((ncp, CMP_HIDDEN), F32)
        for l in range(CMP_STRIDE):
            xs = x_ref[:, (l * NSA_GROUPS + g) * dk:(l * NSA_GROUPS + g + 1) * dk]
            lo = lo + _dot(xs, w1_ref[l * dk:(l + 1) * dk, :])
            hi = hi + _dot(xs, w1_ref[(CMP_STRIDE + l) * dk:(CMP_STRIDE + l + 1) * dk, :])
        hid = lo + pltpu.roll(hi, ncp - 1, 0) + posw
        o_ref[g] = _dot(jax.nn.gelu(hid).astype(BF16), w2_ref[...]).astype(o_ref.dtype)


def nsa_compress(xkv, posflat, w1, w2, *, B, S):
    ncp = S // CMP_STRIDE
    G, dk = NSA_GROUPS, NSA_HEAD_DIM
    width = CMP_STRIDE * G * dk
    return pl.pallas_call(
        functools.partial(_nsa_compress_kernel, ncp=ncp),
        grid=(2, B),
        in_specs=[pl.BlockSpec((None, ncp, width), lambda a, b: (a, b, 0)),
                  pl.BlockSpec((None, 8, CMP_LEN * dk), lambda a, b: (a, 0, 0)),
                  pl.BlockSpec((None, CMP_LEN * dk, CMP_HIDDEN), lambda a, b: (a, 0, 0)),
                  pl.BlockSpec((None, CMP_HIDDEN, dk), lambda a, b: (a, 0, 0))],
        out_specs=pl.BlockSpec((None, None, G, ncp, dk), lambda a, b: (a, b, 0, 0, 0)),
        out_shape=jax.ShapeDtypeStruct((2, B, G, ncp, dk), BF16),
        compiler_params=_cparams(("parallel", "parallel")),
        name="nsa_compress",
    )(xkv, posflat, w1, w2)


def _nsa_cmp_kernel(q_ref, kc_ref, vc_ref, ov_ref, oc_ref, sel_ref, *, tq, ncp, ns, n_sel, scale):
    t0 = pl.program_id(2) * tq
    dk = NSA_HEAD_DIM
    kc = kc_ref[...]
    vc = vc_ref[...]
    trow = lax.broadcasted_iota(jnp.int32, (tq, ncp), 0) + t0
    ccol = lax.broadcasted_iota(jnp.int32, (tq, ncp), 1)
    vis = (ccol * CMP_STRIDE + (CMP_LEN - 1)) <= trow
    psum = jnp.zeros((tq, ncp), F32)
    for h in range(NSA_HG):
        s = jnp.where(vis, _dot_nt(q_ref[:, h * dk:(h + 1) * dk], kc) * scale, NEG)
        m = jnp.max(s, axis=-1, keepdims=True)
        p = jnp.where(vis, jnp.exp(s - m), 0.0)
        d = jnp.sum(p, axis=-1, keepdims=True)
        p = p / jnp.where(d > 0, d, 1.0)
        oc_ref[:, h * dk:(h + 1) * dk] = _dot(p.astype(BF16), vc).astype(oc_ref.dtype)
        psum = psum + p
    p_hi = psum.astype(BF16)
    p_lo = (psum - p_hi.astype(F32)).astype(BF16)
    ov = ov_ref[...]
    imp = _dot_nt(ov, p_hi) + _dot_nt(ov, p_lo)
    imp = imp[:ns]
    jrow = lax.broadcasted_iota(jnp.int32, (ns, tq), 0)
    tcol = lax.broadcasted_iota(jnp.int32, (ns, tq), 1) + t0
    blk_t = tcol // SEL_BLOCK
    forced = (jrow == 0) | (jrow == blk_t) | (jrow == blk_t - 1)
    val = jnp.where(forced, -NEG, imp)
    val = jnp.where(jrow * SEL_BLOCK > tcol, NEG, val)
    rank = jnp.zeros((ns, tq), F32)
    for j in range(ns):
        other = val[j:j + 1, :]
        ahead = (other > val) | ((other == val) & (jrow > j))
        rank = rank + jnp.where(ahead, 1.0, 0.0)
    selneg = jnp.where(rank < float(n_sel), 0.0, NEG)
    if ns < LANE:
        selneg = jnp.concatenate([selneg, jnp.zeros((LANE - ns, tq), F32)], axis=0)
    sel_ref[...] = selneg.T.astype(sel_ref.dtype)


def nsa_cmp_attention(proj, kvc, ovt, *, B, S, tq):
    M = B * S
    nq = S // tq
    G, dk = NSA_GROUPS, NSA_HEAD_DIM
    ncp = S // CMP_STRIDE
    ns = S // SEL_BLOCK
    assert ns <= LANE
    n_sel = min(SEL_COUNT, ns)
    gw = NSA_HG * dk
    return pl.pallas_call(
        functools.partial(_nsa_cmp_kernel, tq=tq, ncp=ncp, ns=ns, n_sel=n_sel, scale=dk ** -0.5),
        grid=(B, G, nq),
        in_specs=[pl.BlockSpec((tq, gw), lambda b, g, i: (b * nq + i, OFF_NSA_Q // gw + g)),
                  pl.BlockSpec((None, None, None, ncp, dk), lambda b, g, i: (0, b, g, 0, 0)),
                  pl.BlockSpec((None, None, None, ncp, dk), lambda b, g, i: (1, b, g, 0, 0)),
                  pl.BlockSpec((LANE, ncp), lambda b, g, i: (0, 0))],
        out_specs=[pl.BlockSpec((tq, gw), lambda b, g, i: (b * nq + i, g)),
                   pl.BlockSpec((None, None, tq, LANE), lambda b, g, i: (b, g, i, 0))],
        out_shape=[jax.ShapeDtypeStruct((M, G * gw), BF16),
                   jax.ShapeDtypeStruct((B, G, S, LANE), BF16)],
        compiler_params=_cparams(("parallel", "parallel", "parallel")),
        name="nsa_cmp_attention",
    )(proj, kvc, kvc, ovt)


def _nsa_main_kernel(q_ref, sel_ref, ks_ref, vs_ref, kw_ref, vw_ref, ind_ref, bias_ref, gate_ref, oc_ref,
                     o_ref, m_ref, acc_ref, *, tq, nback, scale):
    g = pl.program_id(1)
    qi = pl.program_id(2)
    dk = NSA_HEAD_DIM
    HG = NSA_HG
    ns = NSA_STREAMS
    hp = HG // ns
    st = range(ns)
    stack = lambda f: [jnp.concatenate([f(a * hp + i) for i in range(hp)], axis=0) for a in st]
    q = stack(lambda h: q_ref[:, h * dk:(h + 1) * dk])
    sel = jnp.concatenate([sel_ref[...]] * hp, axis=0)
    qa = [jnp.concatenate([q[a], sel], axis=1) for a in st]
    gate = jax.nn.sigmoid(gate_ref[...].astype(F32))

    for a in range(2 * ns):
        _softmax_init(m_ref.at[a], acc_ref.at[a])

    def steps(s, v, fix, base=0):
        for a in st:
            _softmax_step(fix(a, s[a]), v, m_ref.at[base + a], acc_ref.at[base + a], scale)

    wspan = nback + 1

    def rows_from(j, width):
        return pl.ds(j * tq if isinstance(j, int) else pl.multiple_of(j * tq, tq), width)

    def sel_scores(j, width=tq):
        sl = rows_from(j, width)
        ka = jnp.concatenate([ks_ref[sl, :], ind_ref[sl, :]], axis=1)
        return [_dot_nt(qa[a], ka) for a in st], vs_ref[sl, :]

    n_far = jnp.maximum(qi - 1, 0)

    def sel_body(p, c):
        s, v = sel_scores(2 * p, 2 * tq)
        steps(s, v, lambda a, x: x)
        return c

    lax.fori_loop(0, n_far // 2, sel_body, 0)

    @pl.when(n_far % 2 == 1)
    def _():
        s, v = sel_scores(n_far - 1)
        steps(s, v, lambda a, x: x)

    def win_scores(j, width):
        sl = rows_from(j, width)
        kw = kw_ref[sl, :]
        return [_dot_nt(q[a], kw) for a in st], vw_ref[sl, :]

    def near(first_s, nc_s, first_w, nc_w):
        tile = lambda nc: (lambda a, x: x + bias_ref[a, :, (wspan - nc) * tq:wspan * tq])
        s_s, v_s = sel_scores(first_s, nc_s * tq)
        s_w, v_w = win_scores(first_w, nc_w * tq)
        steps(s_s, v_s, tile(nc_s))
        steps(s_w, v_w, tile(nc_w), base=ns)

    for k in range(wspan - 1):
        @pl.when(qi == k)
        def _(k=k):
            near(max(k - 1, 0), min(k + 1, 2), 0, k + 1)

    @pl.when(qi >= wspan - 1)
    def _():
        near(qi - 1, 2, qi - (wspan - 1), wspan)

    o_s = [_softmax_finish(acc_ref.at[a]) for a in st]
    o_w = [_softmax_finish(acc_ref.at[ns + a]) for a in st]

    lane = lax.broadcasted_iota(jnp.int32, gate.shape, 1)
    for h in range(HG):
        c = g * HG + h
        a, sl = h // hp, slice((h % hp) * tq, (h % hp + 1) * tq)
        gc = [jnp.sum(jnp.where(lane == br * NSA_HEADS + c, gate, 0.0), axis=-1, keepdims=True)
              for br in range(3)]
        o = gc[0] * oc_ref[:, h * dk:(h + 1) * dk].astype(F32) + gc[1] * o_s[a][sl] + gc[2] * o_w[a][sl]
        o_ref[:, h * dk:(h + 1) * dk] = o.astype(o_ref.dtype)


def nsa_main(proj, selneg, ind, btiles, o_c, *, B, S, tq):
    M = B * S
    nq = S // tq
    G, dk, HG = NSA_GROUPS, NSA_HEAD_DIM, NSA_HG
    gw = HG * dk
    assert WINDOW % tq == 0 and WINDOW // tq >= 2
    nback = WINDOW // tq
    ns, hp = NSA_STREAMS, HG // NSA_STREAMS
    kv = lambda n: pl.BlockSpec((S, dk), lambda b, g, i, n=n: (b, (OFF_NSA_KV + n * G * dk) // dk + g))
    return pl.pallas_call(
        functools.partial(_nsa_main_kernel, tq=tq, nback=nback, scale=dk ** -0.5),
        grid=(B, G, nq),
        in_specs=[pl.BlockSpec((tq, gw), lambda b, g, i: (b * nq + i, OFF_NSA_Q // gw + g)),
                  pl.BlockSpec((None, None, tq, LANE), lambda b, g, i: (b, g, i, 0)),
                  kv(2), kv(3), kv(4), kv(5),
                  pl.BlockSpec((S, LANE), lambda b, g, i: (0, 0)),
                  pl.BlockSpec((ns, hp * tq, (nback + 1) * tq), lambda b, g, i: (g, 0, 0)),
                  pl.BlockSpec((tq, LANE), lambda b, g, i: (b * nq + i, OFF_NSA_GATE // LANE)),
                  pl.BlockSpec((tq, gw), lambda b, g, i: (b * nq + i, g))],
        out_specs=pl.BlockSpec((tq, gw), lambda b, g, i: (b * nq + i, g)),
        out_shape=jax.ShapeDtypeStruct((M, G * gw), BF16),
        scratch_shapes=[pltpu.VMEM((2 * ns, hp * tq, LANE), F32),
                        pltpu.VMEM((2 * ns, hp * tq, 2 * LANE), F32)],
        compiler_params=_cparams(("parallel", "parallel", "arbitrary")),
        name="nsa_main",
    )(proj, selneg, proj, proj, proj, proj, ind, btiles, proj, o_c)


def _merge_kernel(x_ref, oa_ref, ob_ref, oc_ref, g0_ref, g1_ref, g2_ref, wb_ref, wo_ref, gain_ref, o_ref):
    merged = None
    for o_r, g_r, br in ((oa_ref, g0_ref, 0), (ob_ref, g1_ref, 1), (oc_ref, g2_ref, 2)):
        t = jax.nn.sigmoid(g_r[...].astype(F32)) * _dot(o_r[...], wb_ref[br])
        merged = t if merged is None else merged + t
    y = _dot(merged.astype(BF16), wo_ref[...])
    o_ref[...] = x_ref[...] + _rms(y, gain_ref[...])


def merge_branches(x, o_a, o_b, o_c, proj, wb, wo, gain, *, tm):
    M, D = x.shape
    row = pl.BlockSpec((tm, D), lambda i: (i, 0))
    gate = lambda br: pl.BlockSpec((tm, D), lambda i, br=br: (i, OFF_BGATE // D + br))
    return pl.pallas_call(
        _merge_kernel,
        grid=(M // tm,),
        in_specs=[row, row, row, row, gate(0), gate(1), gate(2),
                  pl.BlockSpec(wb.shape, lambda i: (0, 0, 0)),
                  pl.BlockSpec(wo.shape, lambda i: (0, 0)),
                  pl.BlockSpec((1, D), lambda i: (0, 0))],
        out_specs=row,
        out_shape=jax.ShapeDtypeStruct((M, D), F32),
        compiler_params=_cparams(("parallel",)),
        name="merge_branches",
    )(x, o_a, o_b, o_c, proj, proj, proj, wb, wo, gain)


def _cross_kernel(x_ref, kv_ref, wq_ref, wo_ref, gpre_ref, gpost_ref, o_ref, *, scale):
    x = x_ref[...]
    h = _rms(x, gpre_ref[...]).astype(BF16)
    q = _dot(h, wq_ref[...]).astype(BF16)
    dh = CROSS_HEAD_DIM
    outs = []
    for hd in range(CROSS_HEADS):
        k = kv_ref[:, hd * dh:(hd + 1) * dh]
        v = kv_ref[:, D_MODEL + hd * dh:D_MODEL + (hd + 1) * dh]
        s = _dot_nt(q[:, hd * dh:(hd + 1) * dh], k) * scale
        p = jnp.exp(s - jnp.max(s, axis=-1, keepdims=True))
        p = p / jnp.sum(p, axis=-1, keepdims=True)
        outs.append(_dot(p.astype(BF16), v).astype(BF16))
    y = _dot(jnp.concatenate(outs, axis=1), wo_ref[...])
    o_ref[...] = x + _rms(y, gpost_ref[...])


def cross_attention(x, kv, wq, wo, gpre, gpost, *, S, tm):
    M, D = x.shape
    mlen = kv.shape[0] // (M // S)
    per_b = S // tm
    return pl.pallas_call(
        functools.partial(_cross_kernel, scale=CROSS_HEAD_DIM ** -0.5),
        grid=(M // tm,),
        in_specs=[pl.BlockSpec((tm, D), lambda i: (i, 0)),
                  pl.BlockSpec((mlen, 2 * D), lambda i: (i // per_b, 0)),
                  pl.BlockSpec(wq.shape, lambda i: (0, 0)),
                  pl.BlockSpec(wo.shape, lambda i: (0, 0)),
                  pl.BlockSpec((1, D), lambda i: (0, 0)),
                  pl.BlockSpec((1, D), lambda i: (0, 0))],
        out_specs=pl.BlockSpec((tm, D), lambda i: (i, 0)),
        out_shape=jax.ShapeDtypeStruct((M, D), F32),
        compiler_params=_cparams(("parallel",)),
        name="cross_attention",
    )(x, kv, wq, wo, gpre, gpost)


def _mlp_kernel(x_ref, w1_ref, w2_ref, gpre_ref, gpost_ref, o_ref, h_ref, acc_ref):
    j = pl.program_id(1)

    @pl.when(j == 0)
    def _():
        h_ref[...] = _rms(x_ref[...], gpre_ref[...]).astype(BF16)
        acc_ref[...] = jnp.zeros(acc_ref.shape, F32)

    a = jnp.maximum(_dot(h_ref[...], w1_ref[...]), 0.0)
    acc_ref[...] += _dot((a * a).astype(BF16), w2_ref[...])

    @pl.when(j == pl.num_programs(1) - 1)
    def _():
        o_ref[...] = x_ref[...] + _rms(acc_ref[...], gpost_ref[...])


def mlp(x, w1, w2, gpre, gpost, *, tm, tf):
    M, D = x.shape
    FF = w1.shape[1]
    return pl.pallas_call(
        _mlp_kernel,
        grid=(M // tm, FF // tf),
        in_specs=[pl.BlockSpec((tm, D), lambda i, j: (i, 0)),
                  pl.BlockSpec((D, tf), lambda i, j: (0, j)),
                  pl.BlockSpec((tf, D), lambda i, j: (j, 0)),
                  pl.BlockSpec((1, D), lambda i, j: (0, 0)),
                  pl.BlockSpec((1, D), lambda i, j: (0, 0))],
        out_specs=pl.BlockSpec((tm, D), lambda i, j: (i, 0)),
        out_shape=jax.ShapeDtypeStruct((M, D), F32),
        scratch_shapes=[pltpu.VMEM((tm, D), BF16), pltpu.VMEM((tm, D), F32)],
        compiler_params=_cparams(("parallel", "arbitrary")),
        name="mlp",
    )(x, w1, w2, gpre, gpost)


def _rot_half_cols(w):
    half = w.shape[-1] // 2
    return jnp.concatenate([-w[..., half:], w[..., :half]], axis=-1)


def _pack_w_in(w):
    widths = (MLA_Q_LORA, MLA_KV_LORA, MLA_ROPE, 1024, 1024, 1024, IDX_HEADS * IDX_DIM, IDX_DIM, IDX_HEADS,
              1024, 256, 256, 256, 256, 256, 256, 3 * NSA_HEADS, N_BRANCH * D_MODEL)
    offs = np.concatenate([[0], np.cumsum(widths)])
    seg = [w[:, offs[i]:offs[i + 1]] for i in range(len(widths))]
    (q_lat, kv_lat, k_rope, dsa_q, dsa_k, dsa_v, idx_q, idx_k, idx_w,
     nsa_q, kc, vc, ks, vs, kw, vw, nsa_gate, bgate) = seg
    z = lambda n: jnp.zeros((w.shape[0], n), w.dtype)
    cols = [dsa_q, dsa_k, dsa_v, nsa_q, bgate, idx_q, kc, vc, ks, vs, kw, vw,
            q_lat, kv_lat, k_rope, z(LANE - MLA_ROPE), _rot_half_cols(k_rope), z(LANE - MLA_ROPE), z(LANE),
            idx_k, idx_k, idx_w, z(LANE - IDX_HEADS), nsa_gate, z(LANE - 3 * NSA_HEADS)]
    out = jnp.concatenate(cols, axis=1)
    out = jnp.concatenate([out, z(N_PACK - out.shape[1])], axis=1)
    return out.astype(BF16)


def _pack_mla_weights(w_uq, w_ukv):
    H = MLA_HEADS
    uq = w_uq.reshape(MLA_Q_LORA, H, MLA_NOPE + MLA_ROPE)
    nope = uq[:, :, :MLA_NOPE].reshape(MLA_Q_LORA, H * MLA_NOPE)
    rope = uq[:, :, MLA_NOPE:]
    pad = jnp.zeros((MLA_Q_LORA, H, LANE - MLA_ROPE), w_uq.dtype)
    wqr = jnp.concatenate([rope, pad], axis=-1).reshape(MLA_Q_LORA, H * LANE)
    wqrr = jnp.concatenate([_rot_half_cols(rope), pad], axis=-1).reshape(MLA_Q_LORA, H * LANE)
    ukv = w_ukv.reshape(MLA_KV_LORA, H, MLA_NOPE + MLA_V)
    wkv = jnp.concatenate([ukv[:, :, :MLA_NOPE].reshape(MLA_KV_LORA, H * MLA_NOPE),
                           ukv[:, :, MLA_NOPE:].reshape(MLA_KV_LORA, H * MLA_V)], axis=1)
    return nope.astype(BF16), wqr.astype(BF16), wqrr.astype(BF16), wkv.astype(BF16)


def _constants(S):
    half = MLA_ROPE // 2
    inv_freq = ROPE_BASE ** (-np.arange(0, MLA_ROPE, 2, dtype=np.float32) / MLA_ROPE)
    freq = np.zeros((1, LANE), np.float32)
    freq[0, :half] = inv_freq
    freq[0, half:2 * half] = inv_freq
    ncp = S // CMP_STRIDE
    ns = S // SEL_BLOCK
    c_start = np.arange(ncp) * CMP_STRIDE
    j_start = np.arange(ns) * SEL_BLOCK
    overlap = ((c_start[None, :] < j_start[:, None] + SEL_BLOCK) &
               (c_start[None, :] + CMP_LEN > j_start[:, None])).astype(np.float32)
    overlap[:, ncp - 1] = 0.0
    ovt = np.zeros((LANE, ncp), np.float32)
    ovt[:ns] = overlap
    ind = np.zeros((S, LANE), np.float32)
    ind[np.arange(S), np.arange(S) // SEL_BLOCK] = 1.0
    return jnp.asarray(freq), jnp.asarray(ovt, BF16), jnp.asarray(ind, BF16)


def kernel(x, mem, positions, rel_bias, norm_gains, w_in, mla_q_norm, mla_kv_norm, mla_w_uq, mla_w_ukv,
           nsa_cmp_pos, nsa_cmp_w1, nsa_cmp_w2, w_branch, w_out, cross_wq, cross_wkv, cross_wo, mlp_w1, mlp_w2):
    B, S, D = x.shape
    M = B * S
    depth = w_in.shape[0]
    tq = TQ if S % TQ == 0 else TQ_NSA
    tqn = TQ_NSA
    assert S % tq == 0 and S % tqn == 0 and D == D_MODEL
    topk = min(DSA_TOPK_MAX, S // 4)
    freq, ovt, ind = _constants(S)
    btiles = bias_tiles(rel_bias, tq, 0, DSA_HEADS, nspan=2, masked=False, stack=1)
    btiles_n = bias_tiles(rel_bias, tqn, DSA_HEADS, NSA_HEADS, nspan=WINDOW // tqn + 1, masked=True,
                          stack=NSA_HG // NSA_STREAMS)
    pos = positions.reshape(M, 1)
    xf = x.reshape(M, D)
    memf = mem.reshape(B * mem.shape[1], D)
    tm_big = 1024 if M % 1024 == 0 else tq
    tm = 512 if M % 512 == 0 else tq
    ncp = S // CMP_STRIDE
    row = lambda v: v.reshape(1, -1)

    for l in range(depth):
        g = norm_gains[l]
        proj = norm_matmul(xf, row(g[0]), _pack_w_in(w_in[l]), tm=tm_big, tn=TN_IN)

        wqn, wqr, wqrr, wkv = _pack_mla_weights(mla_w_uq[l], mla_w_ukv[l])
        qn, qr, kn, vv, kr = mla_prep(proj, pos, freq, row(mla_q_norm[l]), row(mla_kv_norm[l]),
                                      wqn, wqr, wqrr, wkv, tm=tm)
        o_a = mla_attention(qn, qr, kn, kr, vv, B=B, S=S, tq=tq)

        maskb = dsa_index_mask(proj, B=B, S=S, tqi=TQ_IDX, tk=tq, topk=topk)
        o_b = dsa_attention(proj, maskb, btiles, B=B, S=S, tq=tq)

        kvc_in = jnp.stack([
            proj[:, OFF_NSA_KV + a * 256:OFF_NSA_KV + (a + 1) * 256].reshape(B * ncp, CMP_STRIDE * 256)
            for a in range(2)])
        posflat = jnp.broadcast_to(nsa_cmp_pos[l].reshape(2, 1, CMP_LEN * NSA_HEAD_DIM),
                                   (2, 8, CMP_LEN * NSA_HEAD_DIM)).astype(BF16)
        kvc = nsa_compress(kvc_in, posflat, nsa_cmp_w1[l].astype(BF16), nsa_cmp_w2[l].astype(BF16), B=B, S=S)
        o_cmp, selneg = nsa_cmp_attention(proj, kvc, ovt, B=B, S=S, tq=tqn)
        o_c = nsa_main(proj, selneg, ind, btiles_n, o_cmp, B=B, S=S, tq=tqn)

        xf = merge_branches(xf, o_a, o_b, o_c, proj, w_branch[l].astype(BF16), w_out[l].astype(BF16),
                            row(g[1]), tm=tm)

        mkv = norm_matmul(memf, row(g[3]), cross_wkv[l].astype(BF16), tm=memf.shape[0] // B, tn=1024)
        xf = cross_attention(xf, mkv, cross_wq[l].astype(BF16), cross_wo[l].astype(BF16),
                             row(g[2]), row(g[4]), S=S, tm=tm)

        xf = mlp(xf, mlp_w1[l].astype(BF16), mlp_w2[l].astype(BF16), row(g[5]), row(g[6]), tm=tm_big, tf=1024)

    return xf.reshape(B, S, D)
```

```python
import functools
import math

import numpy as np
import jax
import jax.numpy as jnp
from jax import lax
from jax.experimental import pallas as pl
from jax.experimental.pallas import tpu as pltpu

F32 = jnp.float32
BF16 = jnp.bfloat16

D_MODEL = 1024
EPS = 1e-6
MLA_HEADS = 8
MLA_Q_LORA = 384
MLA_KV_LORA = 256
MLA_NOPE = 128
MLA_ROPE = 64
MLA_V = 128
ROPE_BASE = 10000.0
DSA_HEADS = 8
DSA_HEAD_DIM = 128
IDX_HEADS = 8
IDX_DIM = 64
DSA_TOPK_MAX = 256
NSA_HEADS = 8
NSA_GROUPS = 2
NSA_HG = NSA_HEADS // NSA_GROUPS
NSA_HEAD_DIM = 128
CMP_STRIDE = 16
CMP_LEN = 2 * CMP_STRIDE
CMP_HIDDEN = 256
SEL_BLOCK = 64
SEL_COUNT = 16
WINDOW = 512
REL_BUCKETS = 32
REL_MAX_DIST = 128
REL_HEADS = DSA_HEADS + NSA_HEADS
CROSS_HEADS = 4
CROSS_HEAD_DIM = D_MODEL // CROSS_HEADS
D_FF = 4 * D_MODEL
N_BRANCH = 3

LANE = 128
NEG = -1e30
VMEM_LIMIT = 48 * 1024 * 1024

TQ = 512
TQ_NSA = 256
HPS = 2
NSA_STREAMS = 2
BISECT_ITERS = 32
IDX_TIE_CHECK = 16
TQ_IDX = 512
IDX_CHUNK = 256
IDX_ROWS = 64

OFF_DSA_Q = 0
OFF_DSA_K = 1024
OFF_DSA_V = 2048
OFF_NSA_Q = 3072
OFF_BGATE = 4096
OFF_IDX_Q = 7168
OFF_NSA_KV = 7680
OFF_MLA = 9216
OFF_IDX_K = 10240
OFF_IDX_W = 10368
OFF_NSA_GATE = 10496
N_PACK = 10752
TN_IN = 1536


def _cparams(sem):
    return pltpu.CompilerParams(dimension_semantics=sem, vmem_limit_bytes=VMEM_LIMIT)


def _rms(x, g):
    return x * lax.rsqrt(jnp.mean(x * x, axis=-1, keepdims=True) + EPS) * g


def _dot(a, b):
    return jnp.dot(a, b, preferred_element_type=F32)


def _dot_nt(a, b):
    return lax.dot_general(a, b, (((1,), (1,)), ((), ())), preferred_element_type=F32)


def _norm_mm_kernel(x_ref, g_ref, w_ref, o_ref, xn_ref):
    @pl.when(pl.program_id(1) == 0)
    def _():
        xn_ref[...] = _rms(x_ref[...].astype(F32), g_ref[...]).astype(BF16)

    o_ref[...] = _dot(xn_ref[...], w_ref[...]).astype(o_ref.dtype)


def norm_matmul(x, g, w, *, tm, tn):
    M, K = x.shape
    N = w.shape[1]
    return pl.pallas_call(
        _norm_mm_kernel,
        grid=(M // tm, N // tn),
        in_specs=[pl.BlockSpec((tm, K), lambda i, j: (i, 0)),
                  pl.BlockSpec((1, K), lambda i, j: (0, 0)),
                  pl.BlockSpec((K, tn), lambda i, j: (0, j))],
        out_specs=pl.BlockSpec((tm, tn), lambda i, j: (i, j)),
        out_shape=jax.ShapeDtypeStruct((M, N), BF16),
        scratch_shapes=[pltpu.VMEM((tm, K), BF16)],
        compiler_params=_cparams(("parallel", "arbitrary")),
        name="norm_matmul",
    )(x, g, w)


def _bias_tiles_kernel(tab_ref, bkt_ref, o_ref, *, inv_scale, h0, tq, nspan, masked):
    h = pl.program_id(0) + h0
    bkt = bkt_ref[...]
    far = tab_ref[REL_BUCKETS - 1, h]
    out = jnp.zeros(bkt.shape, F32)
    for b in range(REL_BUCKETS - 1):
        out = jnp.where(bkt == b, (tab_ref[b, h] - far) * inv_scale, out)
    diag, prev = out[0], out[1]
    r, c = _tri(tq, tq)
    if masked:
        diag = jnp.where(r >= c, diag, NEG)
    o_ref[:, (nspan - 1) * tq:nspan * tq] = diag
    o_ref[:, (nspan - 2) * tq:(nspan - 1) * tq] = prev
    for k in range(nspan - 2):
        edge = jnp.where(r < c, 0.0, NEG) if (masked and k == 0) else jnp.zeros((tq, tq), F32)
        o_ref[:, k * tq:(k + 1) * tq] = edge


def _bucket_tiles(tq):
    i = np.arange(tq)[:, None]
    j = np.arange(tq)[None, :]
    d = np.stack([i - j, tq + i - j]).astype(np.int32)
    n = jnp.maximum(jnp.asarray(d), 0)
    exact = REL_BUCKETS // 2
    nf = jnp.maximum(n, 1).astype(F32)
    log_b = exact + (jnp.log(nf / exact) / math.log(REL_MAX_DIST / exact) * (REL_BUCKETS - exact)).astype(jnp.int32)
    return jnp.where(n < exact, n, jnp.minimum(log_b, REL_BUCKETS - 1)).astype(jnp.int32)


def bias_tiles(rel_bias, tq, h0, nh, *, nspan, masked, stack):
    assert tq >= REL_MAX_DIST
    bkt = _bucket_tiles(tq)
    assert DSA_HEAD_DIM == NSA_HEAD_DIM
    return pl.pallas_call(
        functools.partial(_bias_tiles_kernel, inv_scale=DSA_HEAD_DIM ** 0.5, h0=h0, tq=tq, nspan=nspan,
                          masked=masked),
        grid=(nh,),
        in_specs=[pl.BlockSpec(memory_space=pltpu.SMEM),
                  pl.BlockSpec((2, tq, tq), lambda h: (0, 0, 0))],
        out_specs=pl.BlockSpec((None, tq, nspan * tq), lambda h: (h // stack, h % stack, 0)),
        out_shape=jax.ShapeDtypeStruct((nh // stack, stack * tq, nspan * tq), F32),
        compiler_params=_cparams(("arbitrary",)),
        name="bias_tiles",
    )(rel_bias, bkt)


def _mla_prep_kernel(lat_ref, pos_ref, freq_ref, qg_ref, kvg_ref, wqn_ref, wqr_ref, wqrr_ref, wkv_ref,
                     qn_ref, qr_ref, kn_ref, v_ref, kr_ref):
    lat = lat_ref[...].astype(F32)
    qlat = _rms(lat[:, :MLA_Q_LORA], qg_ref[...]).astype(BF16)
    kvlat = _rms(lat[:, MLA_Q_LORA:MLA_Q_LORA + MLA_KV_LORA], kvg_ref[...]).astype(BF16)
    o = MLA_Q_LORA + MLA_KV_LORA
    kr = lat[:, o:o + LANE]
    krr = lat[:, o + LANE:o + 2 * LANE]
    ang = pos_ref[...].astype(F32) * freq_ref[...]
    cos, sin = jnp.cos(ang), jnp.sin(ang)
    cos8 = jnp.concatenate([cos] * MLA_HEADS, axis=1)
    sin8 = jnp.concatenate([sin] * MLA_HEADS, axis=1)
    qn_ref[...] = _dot(qlat, wqn_ref[...]).astype(BF16)
    qr_ref[...] = (_dot(qlat, wqr_ref[...]) * cos8 + _dot(qlat, wqrr_ref[...]) * sin8).astype(BF16)
    kv = _dot(kvlat, wkv_ref[...])
    nk = MLA_HEADS * MLA_NOPE
    kn_ref[...] = kv[:, :nk].astype(BF16)
    v_ref[...] = kv[:, nk:].astype(BF16)
    kr_ref[...] = (kr * cos + krr * sin).astype(BF16)


def mla_prep(proj, pos, freq, qg, kvg, wqn, wqr, wqrr, wkv, *, tm):
    M = proj.shape[0]
    HD = MLA_HEADS * LANE
    full = lambda a: pl.BlockSpec(a.shape, lambda i: (0, 0))
    outs = [jax.ShapeDtypeStruct((M, HD), BF16)] * 4 + [jax.ShapeDtypeStruct((M, LANE), BF16)]
    return pl.pallas_call(
        _mla_prep_kernel,
        grid=(M // tm,),
        in_specs=[pl.BlockSpec((tm, 1024), lambda i: (i, OFF_MLA // 1024)),
                  pl.BlockSpec((tm, 1), lambda i: (i, 0)),
                  full(freq), full(qg), full(kvg), full(wqn), full(wqr), full(wqrr), full(wkv)],
        out_specs=[pl.BlockSpec((tm, HD), lambda i: (i, 0))] * 4 + [pl.BlockSpec((tm, LANE), lambda i: (i, 0))],
        out_shape=outs,
        compiler_params=_cparams(("parallel",)),
        name="mla_prep",
    )(proj, pos, freq, qg, kvg, wqn, wqr, wqrr, wkv)


def _softmax_init(m_ref, acc_ref):
    m_ref[...] = jnp.full(m_ref.shape, NEG, F32)
    acc_ref[...] = jnp.zeros(acc_ref.shape, F32)


def _with_ones(v):
    return jnp.concatenate([v, jnp.ones(v.shape, v.dtype)], axis=1)


def _softmax_step(s, v, m_ref, acc_ref, scale):
    c = scale * math.log2(math.e)
    m_prev = m_ref[...]
    m_new = jnp.maximum(m_prev, jnp.max(s, axis=-1, keepdims=True))
    alpha = jnp.exp2((m_prev - m_new) * c)
    p = jnp.exp2((s - jnp.tile(m_new, (1, s.shape[1] // LANE))) * c)
    acc_ref[...] = jnp.tile(alpha, (1, 2)) * acc_ref[...] + _dot(p.astype(BF16), _with_ones(v))
    m_ref[...] = m_new


def _softmax_finish(acc_ref):
    acc = acc_ref[...]
    return acc[:, :LANE] / acc[:, LANE:]


def _tri(tq, tk):
    row = lax.broadcasted_iota(jnp.int32, (tq, tk), 0)
    col = lax.broadcasted_iota(jnp.int32, (tq, tk), 1)
    return row, col


def _mla_attn_kernel(qn_ref, qr_ref, kn_ref, kr_ref, v_ref, o_ref, m_ref, acc_ref, *, tq, scale):
    qi = pl.program_id(2)
    hs = range(HPS)
    q = [jnp.concatenate([qn_ref[:, h * LANE:(h + 1) * LANE], qr_ref[:, h * LANE:(h + 1) * LANE]], axis=-1)
         for h in hs]
    for h in hs:
        _softmax_init(m_ref.at[h], acc_ref.at[h])

    def scores(j, width=tq):
        sl = pl.ds(pl.multiple_of(j * tq, tq), width)
        kr = kr_ref[sl, :]
        return [_dot_nt(q[h], jnp.concatenate([kn_ref[sl, h * LANE:(h + 1) * LANE], kr], axis=-1)) for h in hs], sl

    def far(j, width):
        s, sl = scores(j, width)
        for h in hs:
            _softmax_step(s[h], v_ref[sl, h * LANE:(h + 1) * LANE], m_ref.at[h], acc_ref.at[h], scale)

    def body(p, c):
        far(2 * p, 2 * tq)
        return c

    lax.fori_loop(0, qi // 2, body, 0)

    @pl.when(qi % 2 == 1)
    def _():
        far(qi - 1, tq)

    s, sl = scores(qi)
    row, col = _tri(tq, tq)
    for h in hs:
        _softmax_step(jnp.where(row >= col, s[h], NEG), v_ref[sl, h * LANE:(h + 1) * LANE],
                      m_ref.at[h], acc_ref.at[h], scale)
    for h in hs:
        o_ref[:, h * LANE:(h + 1) * LANE] = _softmax_finish(acc_ref.at[h]).astype(o_ref.dtype)


def mla_attention(qn, qr, kn, kr, v, *, B, S, tq):
    M = B * S
    nq = S // tq
    H = MLA_HEADS
    scale = (MLA_NOPE + MLA_ROPE) ** -0.5
    w = HPS * LANE
    qspec = pl.BlockSpec((tq, w), lambda b, h, i: (b * nq + i, h))
    kspec = pl.BlockSpec((S, w), lambda b, h, i: (b, h))
    return pl.pallas_call(
        functools.partial(_mla_attn_kernel, tq=tq, scale=scale),
        grid=(B, H // HPS, nq),
        in_specs=[qspec, qspec, kspec, pl.BlockSpec((S, LANE), lambda b, h, i: (b, 0)), kspec],
        out_specs=qspec,
        out_shape=jax.ShapeDtypeStruct((M, H * MLA_V), BF16),
        scratch_shapes=[pltpu.VMEM((HPS, tq, LANE), F32), pltpu.VMEM((HPS, tq, 2 * LANE), F32)],
        compiler_params=_cparams(("parallel", "parallel", "arbitrary")),
        name="mla_attention",
    )(qn, qr, kn, kr, v)


def _dsa_index_kernel(iq_ref, ik_ref, iw_ref, tri_ref, mb_ref, sc_ref, *, tq, tk, S, topk, iters):
    t0 = pl.program_id(1) * tq
    nk = S // tk
    nch = (t0 + tq - 1) // tk + 1
    hw = IDX_CHUNK
    nb = tk // LANE
    iq = iq_ref[...]
    iw = iw_ref[...].astype(F32)
    lane = lax.broadcasted_iota(jnp.int32, (tq, LANE), 1)
    qs = []
    for h in range(IDX_HEADS):
        blk = iq[:, (h // 2) * LANE:(h // 2 + 1) * LANE]
        keep = (lane >= IDX_DIM) if h % 2 else (lane < IDX_DIM)
        qs.append(jnp.where(keep, blk, jnp.zeros_like(blk)))
    q8 = jnp.concatenate(qs, axis=0)
    wcol = [iw[:, h:h + 1] for h in range(IDX_HEADS)]
    row = lax.broadcasted_iota(jnp.int32, (tq, hw), 0) + t0
    col0 = lax.broadcasted_iota(jnp.int32, (tq, hw), 1)

    def lane_fold(x, op):
        out = x[:, 0:LANE]
        for b in range(1, x.shape[1] // LANE):
            out = op(out, x[:, b * LANE:(b + 1) * LANE])
        return out

    def score_chunk(c, carry):
        mn, mx = carry
        for half in range(tk // hw):
            k0 = pl.multiple_of(c * tk + half * hw, hw)
            res = _dot_nt(q8, ik_ref[pl.ds(k0, hw), :])
            sc = wcol[0] * jnp.maximum(res[0:tq], 0.0)
            for h in range(1, IDX_HEADS):
                sc = sc + wcol[h] * jnp.maximum(res[h * tq:(h + 1) * tq], 0.0)
            causal = (col0 + k0) <= row
            mn = jnp.minimum(mn, lane_fold(jnp.where(causal, sc, -NEG), jnp.minimum))
            sc = jnp.where(causal, sc, NEG)
            mx = jnp.maximum(mx, lane_fold(sc, jnp.maximum))
            sc_ref[c, :, half * hw:(half + 1) * hw] = sc
        return mn, mx

    mn, mx = lax.fori_loop(0, nch, score_chunk,
                           (jnp.full((tq, LANE), -NEG, F32), jnp.full((tq, LANE), NEG, F32)))
    n_causal = (lax.broadcasted_iota(jnp.int32, (tq, 1), 0) + (t0 + 1)).astype(F32)
    kf = jnp.minimum(float(topk), n_causal)

    rs = IDX_ROWS

    def blocks(j):
        for r in range(tq // rs):
            for b in range(nb):
                rows = slice(r * rs, (r + 1) * rs)
                yield rows, (j, rows, slice(b * LANE, (b + 1) * LANE)), j * tk + b * LANE

    def count(pred):
        def body(c, acc):
            parts = [acc[r * rs:(r + 1) * rs] for r in range(tq // rs)]
            for rows, idx, off in blocks(c):
                r = rows.start // rs
                parts[r] = parts[r] + jnp.where(pred(rows, sc_ref[idx], off), 1.0, 0.0)
            return jnp.concatenate(parts, axis=0)

        acc = lax.fori_loop(0, nch, body, jnp.zeros((tq, LANE), F32))
        return jnp.sum(acc, axis=-1, keepdims=True)

    def bcast(x):
        strips = [jnp.broadcast_to(x[r * rs:(r + 1) * rs], (rs, LANE)) for r in range(tq // rs)]
        return lambda rows: strips[rows.start // rs]

    lo0 = jnp.min(mn, axis=-1, keepdims=True)
    mx = jnp.max(mx, axis=-1, keepdims=True)
    hi0 = mx + (jnp.abs(mx) * 1e-6 + 1.0)

    def bisect(c):
        lo, hi, n_lo, n_hi = c
        mid = 0.5 * (lo + hi)
        mid_b = bcast(mid)
        n_mid = count(lambda rows, blk, off: blk >= mid_b(rows))
        ge = n_mid >= kf
        return (jnp.where(ge, mid, lo), jnp.where(ge, hi, mid),
                jnp.where(ge, n_mid, n_lo), jnp.where(ge, n_hi, n_mid))

    def open_rows(n_lo, n_hi, settled):
        return (n_lo - kf) * jnp.where(n_lo - n_hi > 2.0, 1.0, 0.0) * (1.0 - settled)

    def search(state, limit, settled):
        def cond(c):
            return jnp.logical_and(c[0] < limit, jnp.max(open_rows(c[3], c[4], settled)) > 0.0)

        def body(c):
            return (c[0] + 2,) + bisect(bisect(c[1:]))

        return lax.while_loop(cond, body, state)

    def band_min_max(lo, hi):
        lo_b, hi_b = bcast(lo), bcast(hi)

        def body(c, carry):
            bmin = [carry[0][r * rs:(r + 1) * rs] for r in range(tq // rs)]
            bmax = [carry[1][r * rs:(r + 1) * rs] for r in range(tq // rs)]
            for rows, idx, off in blocks(c):
                r = rows.start // rs
                blk = sc_ref[idx]
                inside = (blk >= lo_b(rows)) & (blk < hi_b(rows))
                bmin[r] = jnp.minimum(bmin[r], jnp.where(inside, blk, -NEG))
                bmax[r] = jnp.maximum(bmax[r], jnp.where(inside, blk, NEG))
            return jnp.concatenate(bmin, axis=0), jnp.concatenate(bmax, axis=0)

        bmin, bmax = lax.fori_loop(0, nch, body,
                                   (jnp.full((tq, LANE), -NEG, F32), jnp.full((tq, LANE), NEG, F32)))
        return jnp.min(bmin, axis=-1, keepdims=True), jnp.max(bmax, axis=-1, keepdims=True)

    def finish(state):
        _, lo, hi, n_lo, n_hi = state
        over = n_lo > kf
        bmin, bmax = lax.cond(jnp.max(n_lo - kf) > 0.0, lambda: band_min_max(lo, hi),
                              lambda: (lo, lo))
        single = bmin == bmax
        split = over & (n_lo - n_hi == 2.0) & jnp.logical_not(single)
        unsplit = over & jnp.logical_not(split)
        return jnp.where(split, bmax, lo), jnp.where(over & single, 1.0, 0.0), jnp.where(unsplit, 1.0, 0.0)

    zero = jnp.zeros((tq, 1), F32)
    state1 = search((jnp.int32(0), lo0, hi0, n_causal, zero), IDX_TIE_CHECK, zero)
    first = finish(state1)
    state = search(state1, iters, first[1])
    lo, _, unsplit = lax.cond(state[0] > state1[0], lambda: finish(state), lambda: first)
    hi, n_hi = state[2], state[4]
    tied = jnp.max(unsplit) > 0.0

    def write(sel):
        def body(j, c):
            for rows, idx, off in blocks(j):
                mb_ref[idx] = jnp.where(sel(rows, sc_ref[idx], off), 0.0, NEG).astype(mb_ref.dtype)
            return c

        lax.fori_loop(0, nch, body, 0)

    def fill(j, c):
        mb_ref[j] = jnp.full((tq, tk), NEG, mb_ref.dtype)
        return c

    lax.fori_loop(nch, nk, fill, 0)
    lo_b = bcast(lo)

    @pl.when(jnp.logical_not(tied))
    def _():
        write(lambda rows, blk, off: blk >= lo_b(rows))

    @pl.when(tied)
    def _():
        need = kf - n_hi
        tri = tri_ref[...]

        def body(j, base):
            sc = sc_ref[j]
            above = sc >= hi
            band = (sc >= lo) & jnp.logical_not(above)
            prefix = _dot(jnp.where(band, 1.0, 0.0).astype(BF16), tri) + base
            keep = above | (band & (prefix <= need))
            mb_ref[j] = jnp.where(keep, 0.0, NEG).astype(mb_ref.dtype)
            return prefix[:, tk - 1:tk]

        lax.fori_loop(0, nch, body, zero)


def _dsa_index_kernel_t(iq_ref, ik_ref, iw_ref, tri_ref, mb_ref, sc_ref, *, tq, tk, S, topk, iters):
    t0 = pl.program_id(1) * tq
    nk = S // tk
    nch = (t0 + tq - 1) // tk + 1
    hw = IDX_CHUNK
    iq = iq_ref[...]
    lane = lax.broadcasted_iota(jnp.int32, (tq, LANE), 1)
    qs = []
    for h in range(IDX_HEADS):
        blk = iq[:, (h // 2) * LANE:(h // 2 + 1) * LANE]
        keep = (lane >= IDX_DIM) if h % 2 else (lane < IDX_DIM)
        qs.append(jnp.where(keep, blk, jnp.zeros_like(blk)))
    q8 = jnp.concatenate(qs, axis=0)
    iwt = iw_ref[...].astype(F32).T
    wrow = [iwt[h:h + 1, :] for h in range(IDX_HEADS)]
    qpos = lax.broadcasted_iota(jnp.int32, (hw, tq), 1) + t0
    krow = lax.broadcasted_iota(jnp.int32, (hw, tq), 0)

    def fold(x, op):
        out = x[0:8]
        for r in range(1, x.shape[0] // 8):
            out = op(out, x[r * 8:(r + 1) * 8])
        return out

    def score_chunk(c, carry):
        mn, mx = carry
        for half in range(tk // hw):
            k0 = pl.multiple_of(c * tk + half * hw, hw)
            res = _dot_nt(ik_ref[pl.ds(k0, hw), :], q8)
            sc = wrow[0] * jnp.maximum(res[:, 0:tq], 0.0)
            for h in range(1, IDX_HEADS):
                sc = sc + wrow[h] * jnp.maximum(res[:, h * tq:(h + 1) * tq], 0.0)
            causal = (krow + k0) <= qpos
            mn = jnp.minimum(mn, fold(jnp.where(causal, sc, -NEG), jnp.minimum))
            sc = jnp.where(causal, sc, NEG)
            mx = jnp.maximum(mx, fold(sc, jnp.maximum))
            sc_ref[c, half * hw:(half + 1) * hw, :] = sc
        return mn, mx

    mn, mx = lax.fori_loop(0, nch, score_chunk, (jnp.full((8, tq), -NEG, F32), jnp.full((8, tq), NEG, F32)))
    n_causal = (lax.broadcasted_iota(jnp.int32, (1, tq), 1) + (t0 + 1)).astype(F32)
    kf = jnp.minimum(float(topk), n_causal)

    def count(pred):
        def body(c, acc):
            return acc + fold(jnp.where(pred(sc_ref[c]), 1.0, 0.0), jnp.add)

        return jnp.sum(lax.fori_loop(0, nch, body, jnp.zeros((8, tq), F32)), axis=0, keepdims=True)

    lo0 = jnp.min(mn, axis=0, keepdims=True)
    mx = jnp.max(mx, axis=0, keepdims=True)
    hi0 = mx + (jnp.abs(mx) * 1e-6 + 1.0)

    def bisect(c):
        lo, hi, n_lo, n_hi = c
        mid = 0.5 * (lo + hi)
        n_mid = count(lambda blk: blk >= mid)
        ge = n_mid >= kf
        return (jnp.where(ge, mid, lo), jnp.where(ge, hi, mid),
                jnp.where(ge, n_mid, n_lo), jnp.where(ge, n_hi, n_mid))

    def open_rows(n_lo, n_hi, settled):
        return (n_lo - kf) * jnp.where(n_lo - n_hi > 2.0, 1.0, 0.0) * (1.0 - settled)

    def search(state, limit, settled):
        def cond(c):
            return jnp.logical_and(c[0] < limit, jnp.max(open_rows(c[3], c[4], settled)) > 0.0)

        def body(c):
            return (c[0] + 2,) + bisect(bisect(c[1:]))

        return lax.while_loop(cond, body, state)

    def band_min_max(lo, hi):
        def body(c, carry):
            blk = sc_ref[c]
            inside = (blk >= lo) & (blk < hi)
            return (jnp.minimum(carry[0], fold(jnp.where(inside, blk, -NEG), jnp.minimum)),
                    jnp.maximum(carry[1], fold(jnp.where(inside, blk, NEG), jnp.maximum)))

        bmin, bmax = lax.fori_loop(0, nch, body, (jnp.full((8, tq), -NEG, F32), jnp.full((8, tq), NEG, F32)))
        return jnp.min(bmin, axis=0, keepdims=True), jnp.max(bmax, axis=0, keepdims=True)

    def finish(state):
        _, lo, hi, n_lo, n_hi = state
        over = n_lo > kf
        bmin, bmax = lax.cond(jnp.max(n_lo - kf) > 0.0, lambda: band_min_max(lo, hi), lambda: (lo, lo))
        single = bmin == bmax
        split = over & (n_lo - n_hi == 2.0) & jnp.logical_not(single)
        unsplit = over & jnp.logical_not(split)
        return jnp.where(split, bmax, lo), jnp.where(over & single, 1.0, 0.0), jnp.where(unsplit, 1.0, 0.0)

    zero = jnp.zeros((1, tq), F32)
    state1 = search((jnp.int32(0), lo0, hi0, n_causal, zero), IDX_TIE_CHECK, zero)
    first = finish(state1)
    state = search(state1, iters, first[1])
    lo, _, unsplit = lax.cond(state[0] > state1[0], lambda: finish(state), lambda: first)
    hi, n_hi = state[2], state[4]
    tied = jnp.max(unsplit) > 0.0

    def fill(j, c):
        mb_ref[j] = jnp.full((tq, tk), NEG, mb_ref.dtype)
        return c

    lax.fori_loop(nch, nk, fill, 0)

    def emit(j, keep):
        mb_ref[j] = jnp.where(keep, 0.0, NEG).T.astype(mb_ref.dtype)

    @pl.when(jnp.logical_not(tied))
    def _():
        def body(j, c):
            emit(j, sc_ref[j] >= lo)
            return c

        lax.fori_loop(0, nch, body, 0)

    @pl.when(tied)
    def _():
        need = kf - n_hi
        tri = tri_ref[...]

        def body(j, base):
            sc = sc_ref[j]
            above = sc >= hi
            band = (sc >= lo) & jnp.logical_not(above)
            prefix = _dot(tri, jnp.where(band, 1.0, 0.0).astype(BF16)) + base
            emit(j, above | (band & (prefix <= need)))
            return prefix[tk - 1:tk, :]

        lax.fori_loop(0, nch, body, zero)


def dsa_index_mask(proj, *, B, S, tqi, tk, topk):
    nq = S // tqi
    nk = S // tk
    assert tk % tqi == 0 and tk % IDX_CHUNK == 0
    tri = jnp.asarray(np.tril(np.ones((tk, tk), np.float32)), BF16)
    return pl.pallas_call(
        functools.partial(_dsa_index_kernel_t, tq=tqi, tk=tk, S=S, topk=topk, iters=BISECT_ITERS),
        grid=(B, nq),
        in_specs=[pl.BlockSpec((tqi, 512), lambda b, i: (b * nq + i, OFF_IDX_Q // 512)),
                  pl.BlockSpec((S, LANE), lambda b, i: (b, OFF_IDX_K // LANE)),
                  pl.BlockSpec((tqi, LANE), lambda b, i: (b * nq + i, OFF_IDX_W // LANE)),
                  pl.BlockSpec((tk, tk), lambda b, i: (0, 0))],
        out_specs=pl.BlockSpec((None, nk, tqi, tk), lambda b, i: (b, 0, i, 0)),
        out_shape=jax.ShapeDtypeStruct((B, nk, S, tk), BF16),
        scratch_shapes=[pltpu.VMEM((nk, tk, tqi), F32)],
        compiler_params=_cparams(("parallel", "parallel")),
        name="dsa_index_mask",
    )(proj, proj, proj, tri)


def _dsa_attn_kernel(q_ref, k_ref, v_ref, mb_ref, bias_ref, o_ref, m_ref, acc_ref, *, tq, scale):
    qi = pl.program_id(2)
    hs = range(HPS)
    cols = lambda h: slice(h * LANE, (h + 1) * LANE)
    q = [q_ref[:, cols(h)] for h in hs]
    for h in hs:
        _softmax_init(m_ref.at[h], acc_ref.at[h])

    def step(j, nc, bias):
        sl = pl.ds(j * tq if isinstance(j, int) else pl.multiple_of(j * tq, tq), nc * tq)
        mb = mb_ref[j] if nc == 1 else jnp.concatenate([mb_ref[j], mb_ref[j + 1]], axis=1)
        mb = mb.astype(F32)
        s = [_dot_nt(q[h], k_ref[sl, cols(h)]) + mb for h in hs]
        for h in hs:
            sh = s[h] if bias is None else s[h] + bias(h)
            _softmax_step(sh, v_ref[sl, cols(h)], m_ref.at[h], acc_ref.at[h], scale)

    n_far = jnp.maximum(qi - 1, 0)

    def body(p, c):
        step(2 * p, 2, None)
        return c

    lax.fori_loop(0, n_far // 2, body, 0)

    @pl.when(n_far % 2 == 1)
    def _():
        step(n_far - 1, 1, None)

    @pl.when(qi == 0)
    def _():
        step(0, 1, lambda h: bias_ref[h, :, tq:2 * tq])

    @pl.when(qi >= 1)
    def _():
        step(qi - 1, 2, lambda h: bias_ref[h])

    for h in hs:
        o_ref[:, cols(h)] = _softmax_finish(acc_ref.at[h]).astype(o_ref.dtype)


def dsa_attention(proj, maskb, btiles, *, B, S, tq):
    M = B * S
    nq = S // tq
    H = DSA_HEADS
    scale = DSA_HEAD_DIM ** -0.5
    w = HPS * LANE
    return pl.pallas_call(
        functools.partial(_dsa_attn_kernel, tq=tq, scale=scale),
        grid=(B, H // HPS, nq),
        in_specs=[pl.BlockSpec((tq, w), lambda b, h, i: (b * nq + i, OFF_DSA_Q // w + h)),
                  pl.BlockSpec((S, w), lambda b, h, i: (b, OFF_DSA_K // w + h)),
                  pl.BlockSpec((S, w), lambda b, h, i: (b, OFF_DSA_V // w + h)),
                  pl.BlockSpec((None, nq, tq, tq), lambda b, h, i: (b, 0, i, 0)),
                  pl.BlockSpec((HPS, tq, 2 * tq), lambda b, h, i: (h, 0, 0))],
        out_specs=pl.BlockSpec((tq, w), lambda b, h, i: (b * nq + i, h)),
        out_shape=jax.ShapeDtypeStruct((M, H * DSA_HEAD_DIM), BF16),
        scratch_shapes=[pltpu.VMEM((HPS, tq, LANE), F32), pltpu.VMEM((HPS, tq, 2 * LANE), F32)],
        compiler_params=_cparams(("parallel", "parallel", "arbitrary")),
        name="dsa_attention",
    )(proj, proj, proj, maskb, btiles)


def _nsa_compress_kernel(x_ref, pos_ref, w1_ref, w2_ref, o_ref, *, ncp):
    dk = NSA_HEAD_DIM
    posw = _dot(pos_ref[...], w1_ref[...])[0:1, :]
    for g in range(NSA_GROUPS):
        lo = jnp.zeros((ncp, CMP_HIDDEN), F32)
        hi = jnp.zeros((ncp, CMP_HIDDEN), F32)
        for l in range(CMP_STRIDE):
            xs = x_ref[:, (l * NSA_GROUPS + g) * dk:(l * NSA_GROUPS + g + 1) * dk]
            lo = lo + _dot(xs, w1_ref[l * dk:(l + 1) * dk, :])
            hi = hi + _dot(xs, w1_ref[(CMP_STRIDE + l) * dk:(CMP_STRIDE + l + 1) * dk, :])
        hid = lo + pltpu.roll(hi, ncp - 1, 0) + posw
        o_ref[g] = _dot(jax.nn.gelu(hid).astype(BF16), w2_ref[...]).astype(o_ref.dtype)


def nsa_compress(xkv, posflat, w1, w2, *, B, S):
    ncp = S // CMP_STRIDE
    G, dk = NSA_GROUPS, NSA_HEAD_DIM
    width = CMP_STRIDE * G * dk
    return pl.pallas_call(
        functools.partial(_nsa_compress_kernel, ncp=ncp),
        grid=(2, B),
        in_specs=[pl.BlockSpec((None, ncp, width), lambda a, b: (a, b, 0)),
                  pl.BlockSpec((None, 8, CMP_LEN * dk), lambda a, b: (a, 0, 0)),
                  pl.BlockSpec((None, CMP_LEN * dk, CMP_HIDDEN), lambda a, b: (a, 0, 0)),
                  pl.BlockSpec((None, CMP_HIDDEN, dk), lambda a, b: (a, 0, 0))],
        out_specs=pl.BlockSpec((None, None, G, ncp, dk), lambda a, b: (a, b, 0, 0, 0)),
        out_shape=jax.ShapeDtypeStruct((2, B, G, ncp, dk), BF16),
        compiler_params=_cparams(("parallel", "parallel")),
        name="nsa_compress",
    )(xkv, posflat, w1, w2)


def _nsa_cmp_kernel(q_ref, kc_ref, vc_ref, ov_ref, oc_ref, sel_ref, *, tq, ncp, ns, n_sel, scale):
    t0 = pl.program_id(2) * tq
    dk = NSA_HEAD_DIM
    kc = kc_ref[...]
    vc = vc_ref[...]
    trow = lax.broadcasted_iota(jnp.int32, (tq, ncp), 0) + t0
    ccol = lax.broadcasted_iota(jnp.int32, (tq, ncp), 1)
    vis = (ccol * CMP_STRIDE + (CMP_LEN - 1)) <= trow
    psum = jnp.zeros((tq, ncp), F32)
    for h in range(NSA_HG):
        s = jnp.where(vis, _dot_nt(q_ref[:, h * dk:(h + 1) * dk], kc) * scale, NEG)
        m = jnp.max(s, axis=-1, keepdims=True)
        p = jnp.where(vis, jnp.exp(s - m), 0.0)
        d = jnp.sum(p, axis=-1, keepdims=True)
        p = p / jnp.where(d > 0, d, 1.0)
        oc_ref[:, h * dk:(h + 1) * dk] = _dot(p.astype(BF16), vc).astype(oc_ref.dtype)
        psum = psum + p
    p_hi = psum.astype(BF16)
    p_lo = (psum - p_hi.astype(F32)).astype(BF16)
    ov = ov_ref[...]
    imp = _dot_nt(ov, p_hi) + _dot_nt(ov, p_lo)
    imp = imp[:ns]
    jrow = lax.broadcasted_iota(jnp.int32, (ns, tq), 0)
    tcol = lax.broadcasted_iota(jnp.int32, (ns, tq), 1) + t0
    blk_t = tcol // SEL_BLOCK
    forced = (jrow == 0) | (jrow == blk_t) | (jrow == blk_t - 1)
    val = jnp.where(forced, -NEG, imp)
    val = jnp.where(jrow * SEL_BLOCK > tcol, NEG, val)
    rank = jnp.zeros((ns, tq), F32)
    for j in range(ns):
        other = val[j:j + 1, :]
        ahead = (other > val) | ((other == val) & (jrow > j))
        rank = rank + jnp.where(ahead, 1.0, 0.0)
    selneg = jnp.where(rank < float(n_sel), 0.0, NEG)
    if ns < LANE:
        selneg = jnp.concatenate([selneg, jnp.zeros((LANE - ns, tq), F32)], axis=0)
    sel_ref[...] = selneg.T.astype(sel_ref.dtype)


def nsa_cmp_attention(proj, kvc, ovt, *, B, S, tq):
    M = B * S
    nq = S // tq
    G, dk = NSA_GROUPS, NSA_HEAD_DIM
    ncp = S // CMP_STRIDE
    ns = S // SEL_BLOCK
    assert ns <= LANE
    n_sel = min(SEL_COUNT, ns)
    gw = NSA_HG * dk
    return pl.pallas_call(
        functools.partial(_nsa_cmp_kernel, tq=tq, ncp=ncp, ns=ns, n_sel=n_sel, scale=dk ** -0.5),
        grid=(B, G, nq),
        in_specs=[pl.BlockSpec((tq, gw), lambda b, g, i: (b * nq + i, OFF_NSA_Q // gw + g)),
                  pl.BlockSpec((None, None, None, ncp, dk), lambda b, g, i: (0, b, g, 0, 0)),
                  pl.BlockSpec((None, None, None, ncp, dk), lambda b, g, i: (1, b, g, 0, 0)),
                  pl.BlockSpec((LANE, ncp), lambda b, g, i: (0, 0))],
        out_specs=[pl.BlockSpec((tq, gw), lambda b, g, i: (b * nq + i, g)),
                   pl.BlockSpec((None, None, tq, LANE), lambda b, g, i: (b, g, i, 0))],
        out_shape=[jax.ShapeDtypeStruct((M, G * gw), BF16),
                   jax.ShapeDtypeStruct((B, G, S, LANE), BF16)],
        compiler_params=_cparams(("parallel", "parallel", "parallel")),
        name="nsa_cmp_attention",
    )(proj, kvc, kvc, ovt)


def _nsa_main_kernel(q_ref, sel_ref, ks_ref, vs_ref, kw_ref, vw_ref, ind_ref, bias_ref, gate_ref, oc_ref,
                     o_ref, m_ref, acc_ref, *, tq, nback, scale):
    g = pl.program_id(1)
    qi = pl.program_id(2)
    dk = NSA_HEAD_DIM
    HG = NSA_HG
    ns = NSA_STREAMS
    hp = HG // ns
    st = range(ns)
    stack = lambda f: [jnp.concatenate([f(a * hp + i) for i in range(hp)], axis=0) for a in st]
    q = stack(lambda h: q_ref[:, h * dk:(h + 1) * dk])
    sel = jnp.concatenate([sel_ref[...]] * hp, axis=0)
    qa = [jnp.concatenate([q[a], sel], axis=1) for a in st]
    gate = jax.nn.sigmoid(gate_ref[...].astype(F32))

    for a in range(2 * ns):
        _softmax_init(m_ref.at[a], acc_ref.at[a])

    def steps(s, v, fix, base=0):
        for a in st:
            _softmax_step(fix(a, s[a]), v, m_ref.at[base + a], acc_ref.at[base + a], scale)

    wspan = nback + 1

    def rows_from(j, width):
        return pl.ds(j * tq if isinstance(j, int) else pl.multiple_of(j * tq, tq), width)

    def sel_scores(j, width=tq):
        sl = rows_from(j, width)
        ka = jnp.concatenate([ks_ref[sl, :], ind_ref[sl, :]], axis=1)
        return [_dot_nt(qa[a], ka) for a in st], vs_ref[sl, :]

    n_far = jnp.maximum(qi - 1, 0)

    def sel_far(j, nc):
        s, v = sel_scores(j, nc * tq)
        steps(s, v, lambda a, x: x)

    def sel_body(p, c):
        sel_far(4 * p, 4)
        return c

    lax.fori_loop(0, n_far // 4, sel_body, 0)

    @pl.when(n_far % 4 >= 2)
    def _():
        sel_far(n_far // 4 * 4, 2)

    @pl.when(n_far % 2 == 1)
    def _():
        sel_far(n_far - 1, 1)

    def win_scores(j, width):
        sl = rows_from(j, width)
        kw = kw_ref[sl, :]
        return [_dot_nt(q[a], kw) for a in st], vw_ref[sl, :]

    def near(first_s, nc_s, first_w, nc_w):
        tile = lambda nc: (lambda a, x: x + bias_ref[a, :, (wspan - nc) * tq:wspan * tq])
        s_s, v_s = sel_scores(first_s, nc_s * tq)
        s_w, v_w = win_scores(first_w, nc_w * tq)
        steps(s_s, v_s, tile(nc_s))
        steps(s_w, v_w, tile(nc_w), base=ns)

    for k in range(wspan - 1):
        @pl.when(qi == k)
        def _(k=k):
            near(max(k - 1, 0), min(k + 1, 2), 0, k + 1)

    @pl.when(qi >= wspan - 1)
    def _():
        near(qi - 1, 2, qi - (wspan - 1), wspan)

    o_s = [_softmax_finish(acc_ref.at[a]) for a in st]
    o_w = [_softmax_finish(acc_ref.at[ns + a]) for a in st]

    lane = lax.broadcasted_iota(jnp.int32, gate.shape, 1)
    for h in range(HG):
        c = g * HG + h
        a, sl = h // hp, slice((h % hp) * tq, (h % hp + 1) * tq)
        gc = [jnp.sum(jnp.where(lane == br * NSA_HEADS + c, gate, 0.0), axis=-1, keepdims=True)
              for br in range(3)]
        o = gc[0] * oc_ref[:, h * dk:(h + 1) * dk].astype(F32) + gc[1] * o_s[a][sl] + gc[2] * o_w[a][sl]
        o_ref[:, h * dk:(h + 1) * dk] = o.astype(o_ref.dtype)


def nsa_main(proj, selneg, ind, btiles, o_c, *, B, S, tq):
    M = B * S
    nq = S // tq
    G, dk, HG = NSA_GROUPS, NSA_HEAD_DIM, NSA_HG
    gw = HG * dk
    assert WINDOW % tq == 0 and WINDOW // tq >= 2
    nback = WINDOW // tq
    ns, hp = NSA_STREAMS, HG // NSA_STREAMS
    kv = lambda n: pl.BlockSpec((S, dk), lambda b, g, i, n=n: (b, (OFF_NSA_KV + n * G * dk) // dk + g))
    return pl.pallas_call(
        functools.partial(_nsa_main_kernel, tq=tq, nback=nback, scale=dk ** -0.5),
        grid=(B, G, nq),
        in_specs=[pl.BlockSpec((tq, gw), lambda b, g, i: (b * nq + i, OFF_NSA_Q // gw + g)),
                  pl.BlockSpec((None, None, tq, LANE), lambda b, g, i: (b, g, i, 0)),
                  kv(2), kv(3), kv(4), kv(5),
                  pl.BlockSpec((S, LANE), lambda b, g, i: (0, 0)),
                  pl.BlockSpec((ns, hp * tq, (nback + 1) * tq), lambda b, g, i: (g, 0, 0)),
                  pl.BlockSpec((tq, LANE), lambda b, g, i: (b * nq + i, OFF_NSA_GATE // LANE)),
                  pl.BlockSpec((tq, gw), lambda b, g, i: (b * nq + i, g))],
        out_specs=pl.BlockSpec((tq, gw), lambda b, g, i: (b * nq + i, g)),
        out_shape=jax.ShapeDtypeStruct((M, G * gw), BF16),
        scratch_shapes=[pltpu.VMEM((2 * ns, hp * tq, LANE), F32),
                        pltpu.VMEM((2 * ns, hp * tq, 2 * LANE), F32)],
        compiler_params=_cparams(("parallel", "parallel", "arbitrary")),
        name="nsa_main",
    )(proj, selneg, proj, proj, proj, proj, ind, btiles, proj, o_c)


def _merge_kernel(x_ref, oa_ref, ob_ref, oc_ref, g0_ref, g1_ref, g2_ref, wb_ref, wo_ref, gain_ref, o_ref):
    merged = None
    for o_r, g_r, br in ((oa_ref, g0_ref, 0), (ob_ref, g1_ref, 1), (oc_ref, g2_ref, 2)):
        t = jax.nn.sigmoid(g_r[...].astype(F32)) * _dot(o_r[...], wb_ref[br])
        merged = t if merged is None else merged + t
    y = _dot(merged.astype(BF16), wo_ref[...])
    o_ref[...] = x_ref[...] + _rms(y, gain_ref[...])


def merge_branches(x, o_a, o_b, o_c, proj, wb, wo, gain, *, tm):
    M, D = x.shape
    row = pl.BlockSpec((tm, D), lambda i: (i, 0))
    gate = lambda br: pl.BlockSpec((tm, D), lambda i, br=br: (i, OFF_BGATE // D + br))
    return pl.pallas_call(
        _merge_kernel,
        grid=(M // tm,),
        in_specs=[row, row, row, row, gate(0), gate(1), gate(2),
                  pl.BlockSpec(wb.shape, lambda i: (0, 0, 0)),
                  pl.BlockSpec(wo.shape, lambda i: (0, 0)),
                  pl.BlockSpec((1, D), lambda i: (0, 0))],
        out_specs=row,
        out_shape=jax.ShapeDtypeStruct((M, D), F32),
        compiler_params=_cparams(("parallel",)),
        name="merge_branches",
    )(x, o_a, o_b, o_c, proj, proj, proj, wb, wo, gain)


def _cross_kernel(x_ref, kv_ref, wq_ref, wo_ref, gpre_ref, gpost_ref, o_ref, *, scale):
    x = x_ref[...]
    h = _rms(x, gpre_ref[...]).astype(BF16)
    q = _dot(h, wq_ref[...]).astype(BF16)
    dh = CROSS_HEAD_DIM
    outs = []
    for hd in range(CROSS_HEADS):
        k = kv_ref[:, hd * dh:(hd + 1) * dh]
        v = kv_ref[:, D_MODEL + hd * dh:D_MODEL + (hd + 1) * dh]
        s = _dot_nt(q[:, hd * dh:(hd + 1) * dh], k) * scale
        p = jnp.exp(s - jnp.max(s, axis=-1, keepdims=True))
        p = p / jnp.sum(p, axis=-1, keepdims=True)
        outs.append(_dot(p.astype(BF16), v).astype(BF16))
    y = _dot(jnp.concatenate(outs, axis=1), wo_ref[...])
    o_ref[...] = x + _rms(y, gpost_ref[...])


def cross_attention(x, kv, wq, wo, gpre, gpost, *, S, tm):
    M, D = x.shape
    mlen = kv.shape[0] // (M // S)
    per_b = S // tm
    return pl.pallas_call(
        functools.partial(_cross_kernel, scale=CROSS_HEAD_DIM ** -0.5),
        grid=(M // tm,),
        in_specs=[pl.BlockSpec((tm, D), lambda i: (i, 0)),
                  pl.BlockSpec((mlen, 2 * D), lambda i: (i // per_b, 0)),
                  pl.BlockSpec(wq.shape, lambda i: (0, 0)),
                  pl.BlockSpec(wo.shape, lambda i: (0, 0)),
                  pl.BlockSpec((1, D), lambda i: (0, 0)),
                  pl.BlockSpec((1, D), lambda i: (0, 0))],
        out_specs=pl.BlockSpec((tm, D), lambda i: (i, 0)),
        out_shape=jax.ShapeDtypeStruct((M, D), F32),
        compiler_params=_cparams(("parallel",)),
        name="cross_attention",
    )(x, kv, wq, wo, gpre, gpost)


def _mlp_kernel(x_ref, w1_ref, w2_ref, gpre_ref, gpost_ref, o_ref, h_ref, acc_ref):
    j = pl.program_id(1)

    @pl.when(j == 0)
    def _():
        h_ref[...] = _rms(x_ref[...], gpre_ref[...]).astype(BF16)
        acc_ref[...] = jnp.zeros(acc_ref.shape, F32)

    a = jnp.maximum(_dot(h_ref[...], w1_ref[...]), 0.0)
    acc_ref[...] += _dot((a * a).astype(BF16), w2_ref[...])

    @pl.when(j == pl.num_programs(1) - 1)
    def _():
        o_ref[...] = x_ref[...] + _rms(acc_ref[...], gpost_ref[...])


def mlp(x, w1, w2, gpre, gpost, *, tm, tf):
    M, D = x.shape
    FF = w1.shape[1]
    return pl.pallas_call(
        _mlp_kernel,
        grid=(M // tm, FF // tf),
        in_specs=[pl.BlockSpec((tm, D), lambda i, j: (i, 0)),
                  pl.BlockSpec((D, tf), lambda i, j: (0, j)),
                  pl.BlockSpec((tf, D), lambda i, j: (j, 0)),
                  pl.BlockSpec((1, D), lambda i, j: (0, 0)),
                  pl.BlockSpec((1, D), lambda i, j: (0, 0))],
        out_specs=pl.BlockSpec((tm, D), lambda i, j: (i, 0)),
        out_shape=jax.ShapeDtypeStruct((M, D), F32),
        scratch_shapes=[pltpu.VMEM((tm, D), BF16), pltpu.VMEM((tm, D), F32)],
        compiler_params=_cparams(("parallel", "arbitrary")),
        name="mlp",
    )(x, w1, w2, gpre, gpost)


def _rot_half_cols(w):
    half = w.shape[-1] // 2
    return jnp.concatenate([-w[..., half:], w[..., :half]], axis=-1)


def _pack_w_in(w):
    widths = (MLA_Q_LORA, MLA_KV_LORA, MLA_ROPE, 1024, 1024, 1024, IDX_HEADS * IDX_DIM, IDX_DIM, IDX_HEADS,
              1024, 256, 256, 256, 256, 256, 256, 3 * NSA_HEADS, N_BRANCH * D_MODEL)
    offs = np.concatenate([[0], np.cumsum(widths)])
    seg = [w[:, offs[i]:offs[i + 1]] for i in range(len(widths))]
    (q_lat, kv_lat, k_rope, dsa_q, dsa_k, dsa_v, idx_q, idx_k, idx_w,
     nsa_q, kc, vc, ks, vs, kw, vw, nsa_gate, bgate) = seg
    z = lambda n: jnp.zeros((w.shape[0], n), w.dtype)
    cols = [dsa_q, dsa_k, dsa_v, nsa_q, bgate, idx_q, kc, vc, ks, vs, kw, vw,
            q_lat, kv_lat, k_rope, z(LANE - MLA_ROPE), _rot_half_cols(k_rope), z(LANE - MLA_ROPE), z(LANE),
            idx_k, idx_k, idx_w, z(LANE - IDX_HEADS), nsa_gate, z(LANE - 3 * NSA_HEADS)]
    out = jnp.concatenate(cols, axis=1)
    out = jnp.concatenate([out, z(N_PACK - out.shape[1])], axis=1)
    return out.astype(BF16)


def _pack_mla_weights(w_uq, w_ukv):
    H = MLA_HEADS
    uq = w_uq.reshape(MLA_Q_LORA, H, MLA_NOPE + MLA_ROPE)
    nope = uq[:, :, :MLA_NOPE].reshape(MLA_Q_LORA, H * MLA_NOPE)
    rope = uq[:, :, MLA_NOPE:]
    pad = jnp.zeros((MLA_Q_LORA, H, LANE - MLA_ROPE), w_uq.dtype)
    wqr = jnp.concatenate([rope, pad], axis=-1).reshape(MLA_Q_LORA, H * LANE)
    wqrr = jnp.concatenate([_rot_half_cols(rope), pad], axis=-1).reshape(MLA_Q_LORA, H * LANE)
    ukv = w_ukv.reshape(MLA_KV_LORA, H, MLA_NOPE + MLA_V)
    wkv = jnp.concatenate([ukv[:, :, :MLA_NOPE].reshape(MLA_KV_LORA, H * MLA_NOPE),
                           ukv[:, :, MLA_NOPE:].reshape(MLA_KV_LORA, H * MLA_V)], axis=1)
    return nope.astype(BF16), wqr.astype(BF16), wqrr.astype(BF16), wkv.astype(BF16)


def _constants(S):
    half = MLA_ROPE // 2
    inv_freq = ROPE_BASE ** (-np.arange(0, MLA_ROPE, 2, dtype=np.float32) / MLA_ROPE)
    freq = np.zeros((1, LANE), np.float32)
    freq[0, :half] = inv_freq
    freq[0, half:2 * half] = inv_freq
    ncp = S // CMP_STRIDE
    ns = S // SEL_BLOCK
    c_start = np.arange(ncp) * CMP_STRIDE
    j_start = np.arange(ns) * SEL_BLOCK
    overlap = ((c_start[None, :] < j_start[:, None] + SEL_BLOCK) &
               (c_start[None, :] + CMP_LEN > j_start[:, None])).astype(np.float32)
    overlap[:, ncp - 1] = 0.0
    ovt = np.zeros((LANE, ncp), np.float32)
    ovt[:ns] = overlap
    ind = np.zeros((S, LANE), np.float32)
    ind[np.arange(S), np.arange(S) // SEL_BLOCK] = 1.0
    return jnp.asarray(freq), jnp.asarray(ovt, BF16), jnp.asarray(ind, BF16)


def kernel(x, mem, positions, rel_bias, norm_gains, w_in, mla_q_norm, mla_kv_norm, mla_w_uq, mla_w_ukv,
           nsa_cmp_pos, nsa_cmp_w1, nsa_cmp_w2, w_branch, w_out, cross_wq, cross_wkv, cross_wo, mlp_w1, mlp_w2):
    B, S, D = x.shape
    M = B * S
    depth = w_in.shape[0]
    tq = TQ if S % TQ == 0 else TQ_NSA
    tqn = TQ_NSA
    assert S % tq == 0 and S % tqn == 0 and D == D_MODEL
    topk = min(DSA_TOPK_MAX, S // 4)
    freq, ovt, ind = _constants(S)
    btiles = bias_tiles(rel_bias, tq, 0, DSA_HEADS, nspan=2, masked=False, stack=1)
    btiles_n = bias_tiles(rel_bias, tqn, DSA_HEADS, NSA_HEADS, nspan=WINDOW // tqn + 1, masked=True,
                          stack=NSA_HG // NSA_STREAMS)
    pos = positions.reshape(M, 1)
    xf = x.reshape(M, D)
    memf = mem.reshape(B * mem.shape[1], D)
    tm_big = 1024 if M % 1024 == 0 else tq
    tm = 512 if M % 512 == 0 else tq
    ncp = S // CMP_STRIDE
    row = lambda v: v.reshape(1, -1)

    for l in range(depth):
        g = norm_gains[l]
        proj = norm_matmul(xf, row(g[0]), _pack_w_in(w_in[l]), tm=tm_big, tn=TN_IN)

        wqn, wqr, wqrr, wkv = _pack_mla_weights(mla_w_uq[l], mla_w_ukv[l])
        qn, qr, kn, vv, kr = mla_prep(proj, pos, freq, row(mla_q_norm[l]), row(mla_kv_norm[l]),
                                      wqn, wqr, wqrr, wkv, tm=tm)
        o_a = mla_attention(qn, qr, kn, kr, vv, B=B, S=S, tq=tq)

        maskb = dsa_index_mask(proj, B=B, S=S, tqi=TQ_IDX, tk=tq, topk=topk)
        o_b = dsa_attention(proj, maskb, btiles, B=B, S=S, tq=tq)

        kvc_in = jnp.stack([
            proj[:, OFF_NSA_KV + a * 256:OFF_NSA_KV + (a + 1) * 256].reshape(B * ncp, CMP_STRIDE * 256)
            for a in range(2)])
        posflat = jnp.broadcast_to(nsa_cmp_pos[l].reshape(2, 1, CMP_LEN * NSA_HEAD_DIM),
                                   (2, 8, CMP_LEN * NSA_HEAD_DIM)).astype(BF16)
        kvc = nsa_compress(kvc_in, posflat, nsa_cmp_w1[l].astype(BF16), nsa_cmp_w2[l].astype(BF16), B=B, S=S)
        o_cmp, selneg = nsa_cmp_attention(proj, kvc, ovt, B=B, S=S, tq=tqn)
        o_c = nsa_main(proj, selneg, ind, btiles_n, o_cmp, B=B, S=S, tq=tqn)

        xf = merge_branches(xf, o_a, o_b, o_c, proj, w_branch[l].astype(BF16), w_out[l].astype(BF16),
                            row(g[1]), tm=tm)

        mkv = norm_matmul(memf, row(g[3]), cross_wkv[l].astype(BF16), tm=memf.shape[0] // B, tn=1024)
        xf = cross_attention(xf, mkv, cross_wq[l].astype(BF16), cross_wo[l].astype(BF16),
                             row(g[2]), row(g[4]), S=S, tm=tm)

        xf = mlp(xf, mlp_w1[l].astype(BF16), mlp_w2[l].astype(BF16), row(g[5]), row(g[6]), tm=tm_big, tf=1024)

    return xf.reshape(B, S, D)
```

```python
import functools
import math

import numpy as np
import jax
import jax.numpy as jnp
from jax import lax
from jax.experimental import pallas as pl
from jax.experimental.pallas import tpu as pltpu

F32 = jnp.float32
BF16 = jnp.bfloat16

D_MODEL = 1024
EPS = 1e-6
MLA_HEADS = 8
MLA_Q_LORA = 384
MLA_KV_LORA = 256
MLA_NOPE = 128
MLA_ROPE = 64
MLA_V = 128
ROPE_BASE = 10000.0
DSA_HEADS = 8
DSA_HEAD_DIM = 128
IDX_HEADS = 8
IDX_DIM = 64
DSA_TOPK_MAX = 256
NSA_HEADS = 8
NSA_GROUPS = 2
NSA_HG = NSA_HEADS // NSA_GROUPS
NSA_HEAD_DIM = 128
CMP_STRIDE = 16
CMP_LEN = 2 * CMP_STRIDE
CMP_HIDDEN = 256
SEL_BLOCK = 64
SEL_COUNT = 16
WINDOW = 512
REL_BUCKETS = 32
REL_MAX_DIST = 128
REL_HEADS = DSA_HEADS + NSA_HEADS
CROSS_HEADS = 4
CROSS_HEAD_DIM = D_MODEL // CROSS_HEADS
D_FF = 4 * D_MODEL
N_BRANCH = 3

LANE = 128
NEG = -1e30
VMEM_LIMIT = 48 * 1024 * 1024

TQ = 512
TQ_NSA = 256
HPS = 2
FAR_GROUP = 4
FAR_GROUP_NSA = 8
NSA_STREAMS = 2
BISECT_ITERS = 32
IDX_TIE_CHECK = 16
TQ_IDX = 512
IDX_CHUNK = 256
IDX_ROWS = 64

OFF_DSA_Q = 0
OFF_DSA_K = 1024
OFF_DSA_V = 2048
OFF_NSA_Q = 3072
OFF_BGATE = 4096
OFF_IDX_Q = 7168
OFF_NSA_KV = 7680
OFF_MLA = 9216
OFF_IDX_K = 10240
OFF_IDX_W = 10368
OFF_NSA_GATE = 10496
N_PACK = 10752
TN_IN = 1536


def _cparams(sem):
    return pltpu.CompilerParams(dimension_semantics=sem, vmem_limit_bytes=VMEM_LIMIT)


def _rms(x, g):
    return x * lax.rsqrt(jnp.mean(x * x, axis=-1, keepdims=True) + EPS) * g


def _dot(a, b):
    return jnp.dot(a, b, preferred_element_type=F32)


def _dot_nt(a, b):
    return lax.dot_general(a, b, (((1,), (1,)), ((), ())), preferred_element_type=F32)


def _norm_mm_kernel(x_ref, g_ref, w_ref, o_ref, xn_ref):
    @pl.when(pl.program_id(1) == 0)
    def _():
        xn_ref[...] = _rms(x_ref[...].astype(F32), g_ref[...]).astype(BF16)

    o_ref[...] = _dot(xn_ref[...], w_ref[...]).astype(o_ref.dtype)


def norm_matmul(x, g, w, *, tm, tn):
    M, K = x.shape
    N = w.shape[1]
    return pl.pallas_call(
        _norm_mm_kernel,
        grid=(M // tm, N // tn),
        in_specs=[pl.BlockSpec((tm, K), lambda i, j: (i, 0)),
                  pl.BlockSpec((1, K), lambda i, j: (0, 0)),
                  pl.BlockSpec((K, tn), lambda i, j: (0, j))],
        out_specs=pl.BlockSpec((tm, tn), lambda i, j: (i, j)),
        out_shape=jax.ShapeDtypeStruct((M, N), BF16),
        scratch_shapes=[pltpu.VMEM((tm, K), BF16)],
        compiler_params=_cparams(("parallel", "arbitrary")),
        name="norm_matmul",
    )(x, g, w)


def _bias_tiles_kernel(tab_ref, bkt_ref, o_ref, *, inv_scale, h0, tq, nspan, masked):
    h = pl.program_id(0) + h0
    bkt = bkt_ref[...]
    far = tab_ref[REL_BUCKETS - 1, h]
    out = jnp.zeros(bkt.shape, F32)
    for b in range(REL_BUCKETS - 1):
        out = jnp.where(bkt == b, (tab_ref[b, h] - far) * inv_scale, out)
    diag, prev = out[0], out[1]
    r, c = _tri(tq, tq)
    if masked:
        diag = jnp.where(r >= c, diag, NEG)
    o_ref[:, (nspan - 1) * tq:nspan * tq] = diag
    o_ref[:, (nspan - 2) * tq:(nspan - 1) * tq] = prev
    for k in range(nspan - 2):
        edge = jnp.where(r < c, 0.0, NEG) if (masked and k == 0) else jnp.zeros((tq, tq), F32)
        o_ref[:, k * tq:(k + 1) * tq] = edge


def _bucket_tiles(tq):
    i = np.arange(tq)[:, None]
    j = np.arange(tq)[None, :]
    d = np.stack([i - j, tq + i - j]).astype(np.int32)
    n = jnp.maximum(jnp.asarray(d), 0)
    exact = REL_BUCKETS // 2
    nf = jnp.maximum(n, 1).astype(F32)
    log_b = exact + (jnp.log(nf / exact) / math.log(REL_MAX_DIST / exact) * (REL_BUCKETS - exact)).astype(jnp.int32)
    return jnp.where(n < exact, n, jnp.minimum(log_b, REL_BUCKETS - 1)).astype(jnp.int32)


def bias_tiles(rel_bias, tq, h0, nh, *, nspan, masked, stack):
    assert tq >= REL_MAX_DIST
    bkt = _bucket_tiles(tq)
    assert DSA_HEAD_DIM == NSA_HEAD_DIM
    return pl.pallas_call(
        functools.partial(_bias_tiles_kernel, inv_scale=DSA_HEAD_DIM ** 0.5, h0=h0, tq=tq, nspan=nspan,
                          masked=masked),
        grid=(nh,),
        in_specs=[pl.BlockSpec(memory_space=pltpu.SMEM),
                  pl.BlockSpec((2, tq, tq), lambda h: (0, 0, 0))],
        out_specs=pl.BlockSpec((None, tq, nspan * tq), lambda h: (h // stack, h % stack, 0)),
        out_shape=jax.ShapeDtypeStruct((nh // stack, stack * tq, nspan * tq), F32),
        compiler_params=_cparams(("arbitrary",)),
        name="bias_tiles",
    )(rel_bias, bkt)


def _mla_prep_kernel(lat_ref, pos_ref, freq_ref, qg_ref, kvg_ref, wqn_ref, wqr_ref, wqrr_ref, wkv_ref,
                     qn_ref, qr_ref, kn_ref, v_ref, kr_ref):
    lat = lat_ref[...].astype(F32)
    qlat = _rms(lat[:, :MLA_Q_LORA], qg_ref[...]).astype(BF16)
    kvlat = _rms(lat[:, MLA_Q_LORA:MLA_Q_LORA + MLA_KV_LORA], kvg_ref[...]).astype(BF16)
    o = MLA_Q_LORA + MLA_KV_LORA
    kr = lat[:, o:o + LANE]
    krr = lat[:, o + LANE:o + 2 * LANE]
    ang = pos_ref[...].astype(F32) * freq_ref[...]
    cos, sin = jnp.cos(ang), jnp.sin(ang)
    cos8 = jnp.concatenate([cos] * MLA_HEADS, axis=1)
    sin8 = jnp.concatenate([sin] * MLA_HEADS, axis=1)
    qn_ref[...] = _dot(qlat, wqn_ref[...]).astype(BF16)
    qr_ref[...] = (_dot(qlat, wqr_ref[...]) * cos8 + _dot(qlat, wqrr_ref[...]) * sin8).astype(BF16)
    kv = _dot(kvlat, wkv_ref[...])
    nk = MLA_HEADS * MLA_NOPE
    kn_ref[...] = kv[:, :nk].astype(BF16)
    v_ref[...] = kv[:, nk:].astype(BF16)
    kr_ref[...] = (kr * cos + krr * sin).astype(BF16)


def mla_prep(proj, pos, freq, qg, kvg, wqn, wqr, wqrr, wkv, *, tm):
    M = proj.shape[0]
    HD = MLA_HEADS * LANE
    full = lambda a: pl.BlockSpec(a.shape, lambda i: (0, 0))
    outs = [jax.ShapeDtypeStruct((M, HD), BF16)] * 4 + [jax.ShapeDtypeStruct((M, LANE), BF16)]
    return pl.pallas_call(
        _mla_prep_kernel,
        grid=(M // tm,),
        in_specs=[pl.BlockSpec((tm, 1024), lambda i: (i, OFF_MLA // 1024)),
                  pl.BlockSpec((tm, 1), lambda i: (i, 0)),
                  full(freq), full(qg), full(kvg), full(wqn), full(wqr), full(wqrr), full(wkv)],
        out_specs=[pl.BlockSpec((tm, HD), lambda i: (i, 0))] * 4 + [pl.BlockSpec((tm, LANE), lambda i: (i, 0))],
        out_shape=outs,
        compiler_params=_cparams(("parallel",)),
        name="mla_prep",
    )(proj, pos, freq, qg, kvg, wqn, wqr, wqrr, wkv)


def _softmax_init(m_ref, acc_ref):
    m_ref[...] = jnp.full(m_ref.shape, NEG, F32)
    acc_ref[...] = jnp.zeros(acc_ref.shape, F32)


def _with_ones(v):
    return jnp.concatenate([v, jnp.ones(v.shape, v.dtype)], axis=1)


def _softmax_step(s, v, m_ref, acc_ref, scale):
    c = scale * math.log2(math.e)
    m_prev = m_ref[...]
    m_new = jnp.maximum(m_prev, jnp.max(s, axis=-1, keepdims=True))
    alpha = jnp.exp2((m_prev - m_new) * c)
    p = jnp.exp2((s - jnp.tile(m_new, (1, s.shape[1] // LANE))) * c)
    acc_ref[...] = jnp.tile(alpha, (1, 2)) * acc_ref[...] + _dot(p.astype(BF16), _with_ones(v))
    m_ref[...] = m_new


def _softmax_finish(acc_ref):
    acc = acc_ref[...]
    return acc[:, :LANE] / acc[:, LANE:]


def _far_chunks(n, step, largest):
    def body(p, c):
        step(largest * p, largest)
        return c

    lax.fori_loop(0, n // largest, body, 0)
    w = largest // 2
    while w >= 1:
        @pl.when(n % (2 * w) >= w)
        def _(w=w):
            step(n // (2 * w) * (2 * w), w)
        w //= 2


def _tri(tq, tk):
    row = lax.broadcasted_iota(jnp.int32, (tq, tk), 0)
    col = lax.broadcasted_iota(jnp.int32, (tq, tk), 1)
    return row, col


def _mla_attn_kernel(qn_ref, qr_ref, kn_ref, kr_ref, v_ref, o_ref, m_ref, acc_ref, *, tq, scale):
    qi = pl.program_id(2)
    hs = range(HPS)
    q = [jnp.concatenate([qn_ref[:, h * LANE:(h + 1) * LANE], qr_ref[:, h * LANE:(h + 1) * LANE]], axis=-1)
         for h in hs]
    for h in hs:
        _softmax_init(m_ref.at[h], acc_ref.at[h])

    def scores(j, width=tq):
        sl = pl.ds(pl.multiple_of(j * tq, tq), width)
        kr = kr_ref[sl, :]
        return [_dot_nt(q[h], jnp.concatenate([kn_ref[sl, h * LANE:(h + 1) * LANE], kr], axis=-1)) for h in hs], sl

    def far(j, nc):
        s, sl = scores(j, nc * tq)
        for h in hs:
            _softmax_step(s[h], v_ref[sl, h * LANE:(h + 1) * LANE], m_ref.at[h], acc_ref.at[h], scale)

    _far_chunks(qi, far, FAR_GROUP)
    s, sl = scores(qi)
    row, col = _tri(tq, tq)
    for h in hs:
        _softmax_step(jnp.where(row >= col, s[h], NEG), v_ref[sl, h * LANE:(h + 1) * LANE],
                      m_ref.at[h], acc_ref.at[h], scale)
    for h in hs:
        o_ref[:, h * LANE:(h + 1) * LANE] = _softmax_finish(acc_ref.at[h]).astype(o_ref.dtype)


def mla_attention(qn, qr, kn, kr, v, *, B, S, tq):
    M = B * S
    nq = S // tq
    H = MLA_HEADS
    scale = (MLA_NOPE + MLA_ROPE) ** -0.5
    w = HPS * LANE
    qspec = pl.BlockSpec((tq, w), lambda b, h, i: (b * nq + i, h))
    kspec = pl.BlockSpec((S, w), lambda b, h, i: (b, h))
    return pl.pallas_call(
        functools.partial(_mla_attn_kernel, tq=tq, scale=scale),
        grid=(B, H // HPS, nq),
        in_specs=[qspec, qspec, kspec, pl.BlockSpec((S, LANE), lambda b, h, i: (b, 0)), kspec],
        out_specs=qspec,
        out_shape=jax.ShapeDtypeStruct((M, H * MLA_V), BF16),
        scratch_shapes=[pltpu.VMEM((HPS, tq, LANE), F32), pltpu.VMEM((HPS, tq, 2 * LANE), F32)],
        compiler_params=_cparams(("parallel", "parallel", "arbitrary")),
        name="mla_attention",
    )(qn, qr, kn, kr, v)


def _dsa_index_kernel(iq_ref, ik_ref, iw_ref, tri_ref, mb_ref, sc_ref, *, tq, tk, S, topk, iters):
    t0 = pl.program_id(1) * tq
    nk = S // tk
    nch = (t0 + tq - 1) // tk + 1
    hw = IDX_CHUNK
    nb = tk // LANE
    iq = iq_ref[...]
    iw = iw_ref[...].astype(F32)
    lane = lax.broadcasted_iota(jnp.int32, (tq, LANE), 1)
    qs = []
    for h in range(IDX_HEADS):
        blk = iq[:, (h // 2) * LANE:(h // 2 + 1) * LANE]
        keep = (lane >= IDX_DIM) if h % 2 else (lane < IDX_DIM)
        qs.append(jnp.where(keep, blk, jnp.zeros_like(blk)))
    q8 = jnp.concatenate(qs, axis=0)
    wcol = [iw[:, h:h + 1] for h in range(IDX_HEADS)]
    row = lax.broadcasted_iota(jnp.int32, (tq, hw), 0) + t0
    col0 = lax.broadcasted_iota(jnp.int32, (tq, hw), 1)

    def lane_fold(x, op):
        out = x[:, 0:LANE]
        for b in range(1, x.shape[1] // LANE):
            out = op(out, x[:, b * LANE:(b + 1) * LANE])
        return out

    def score_chunk(c, carry):
        mn, mx = carry
        for half in range(tk // hw):
            k0 = pl.multiple_of(c * tk + half * hw, hw)
            res = _dot_nt(q8, ik_ref[pl.ds(k0, hw), :])
            sc = wcol[0] * jnp.maximum(res[0:tq], 0.0)
            for h in range(1, IDX_HEADS):
                sc = sc + wcol[h] * jnp.maximum(res[h * tq:(h + 1) * tq], 0.0)
            causal = (col0 + k0) <= row
            mn = jnp.minimum(mn, lane_fold(jnp.where(causal, sc, -NEG), jnp.minimum))
            sc = jnp.where(causal, sc, NEG)
            mx = jnp.maximum(mx, lane_fold(sc, jnp.maximum))
            sc_ref[c, :, half * hw:(half + 1) * hw] = sc
        return mn, mx

    mn, mx = lax.fori_loop(0, nch, score_chunk,
                           (jnp.full((tq, LANE), -NEG, F32), jnp.full((tq, LANE), NEG, F32)))
    n_causal = (lax.broadcasted_iota(jnp.int32, (tq, 1), 0) + (t0 + 1)).astype(F32)
    kf = jnp.minimum(float(topk), n_causal)

    rs = IDX_ROWS

    def blocks(j):
        for r in range(tq // rs):
            for b in range(nb):
                rows = slice(r * rs, (r + 1) * rs)
                yield rows, (j, rows, slice(b * LANE, (b + 1) * LANE)), j * tk + b * LANE

    def count(pred):
        def body(c, acc):
            parts = [acc[r * rs:(r + 1) * rs] for r in range(tq // rs)]
            for rows, idx, off in blocks(c):
                r = rows.start // rs
                parts[r] = parts[r] + jnp.where(pred(rows, sc_ref[idx], off), 1.0, 0.0)
            return jnp.concatenate(parts, axis=0)

        acc = lax.fori_loop(0, nch, body, jnp.zeros((tq, LANE), F32))
        return jnp.sum(acc, axis=-1, keepdims=True)

    def bcast(x):
        strips = [jnp.broadcast_to(x[r * rs:(r + 1) * rs], (rs, LANE)) for r in range(tq // rs)]
        return lambda rows: strips[rows.start // rs]

    lo0 = jnp.min(mn, axis=-1, keepdims=True)
    mx = jnp.max(mx, axis=-1, keepdims=True)
    hi0 = mx + (jnp.abs(mx) * 1e-6 + 1.0)

    def bisect(c):
        lo, hi, n_lo, n_hi = c
        mid = 0.5 * (lo + hi)
        mid_b = bcast(mid)
        n_mid = count(lambda rows, blk, off: blk >= mid_b(rows))
        ge = n_mid >= kf
        return (jnp.where(ge, mid, lo), jnp.where(ge, hi, mid),
                jnp.where(ge, n_mid, n_lo), jnp.where(ge, n_hi, n_mid))

    def open_rows(n_lo, n_hi, settled):
        return (n_lo - kf) * jnp.where(n_lo - n_hi > 2.0, 1.0, 0.0) * (1.0 - settled)

    def search(state, limit, settled):
        def cond(c):
            return jnp.logical_and(c[0] < limit, jnp.max(open_rows(c[3], c[4], settled)) > 0.0)

        def body(c):
            return (c[0] + 2,) + bisect(bisect(c[1:]))

        return lax.while_loop(cond, body, state)

    def band_min_max(lo, hi):
        lo_b, hi_b = bcast(lo), bcast(hi)

        def body(c, carry):
            bmin = [carry[0][r * rs:(r + 1) * rs] for r in range(tq // rs)]
            bmax = [carry[1][r * rs:(r + 1) * rs] for r in range(tq // rs)]
            for rows, idx, off in blocks(c):
                r = rows.start // rs
                blk = sc_ref[idx]
                inside = (blk >= lo_b(rows)) & (blk < hi_b(rows))
                bmin[r] = jnp.minimum(bmin[r], jnp.where(inside, blk, -NEG))
                bmax[r] = jnp.maximum(bmax[r], jnp.where(inside, blk, NEG))
            return jnp.concatenate(bmin, axis=0), jnp.concatenate(bmax, axis=0)

        bmin, bmax = lax.fori_loop(0, nch, body,
                                   (jnp.full((tq, LANE), -NEG, F32), jnp.full((tq, LANE), NEG, F32)))
        return jnp.min(bmin, axis=-1, keepdims=True), jnp.max(bmax, axis=-1, keepdims=True)

    def finish(state):
        _, lo, hi, n_lo, n_hi = state
        over = n_lo > kf
        bmin, bmax = lax.cond(jnp.max(n_lo - kf) > 0.0, lambda: band_min_max(lo, hi),
                              lambda: (lo, lo))
        single = bmin == bmax
        split = over & (n_lo - n_hi == 2.0) & jnp.logical_not(single)
        unsplit = over & jnp.logical_not(split)
        return jnp.where(split, bmax, lo), jnp.where(over & single, 1.0, 0.0), jnp.where(unsplit, 1.0, 0.0)

    zero = jnp.zeros((tq, 1), F32)
    state1 = search((jnp.int32(0), lo0, hi0, n_causal, zero), IDX_TIE_CHECK, zero)
    first = finish(state1)
    state = search(state1, iters, first[1])
    lo, _, unsplit = lax.cond(state[0] > state1[0], lambda: finish(state), lambda: first)
    hi, n_hi = state[2], state[4]
    tied = jnp.max(unsplit) > 0.0

    def write(sel):
        def body(j, c):
            for rows, idx, off in blocks(j):
                mb_ref[idx] = jnp.where(sel(rows, sc_ref[idx], off), 0.0, NEG).astype(mb_ref.dtype)
            return c

        lax.fori_loop(0, nch, body, 0)

    def fill(j, c):
        mb_ref[j] = jnp.full((tq, tk), NEG, mb_ref.dtype)
        return c

    lax.fori_loop(nch, nk, fill, 0)
    lo_b = bcast(lo)

    @pl.when(jnp.logical_not(tied))
    def _():
        write(lambda rows, blk, off: blk >= lo_b(rows))

    @pl.when(tied)
    def _():
        need = kf - n_hi
        tri = tri_ref[...]

        def body(j, base):
            sc = sc_ref[j]
            above = sc >= hi
            band = (sc >= lo) & jnp.logical_not(above)
            prefix = _dot(jnp.where(band, 1.0, 0.0).astype(BF16), tri) + base
            keep = above | (band & (prefix <= need))
            mb_ref[j] = jnp.where(keep, 0.0, NEG).astype(mb_ref.dtype)
            return prefix[:, tk - 1:tk]

        lax.fori_loop(0, nch, body, zero)


def _dsa_index_kernel_t(iq_ref, ik_ref, iw_ref, tri_ref, mb_ref, sc_ref, *, tq, tk, S, topk, iters):
    t0 = pl.program_id(1) * tq
    nk = S // tk
    nch = (t0 + tq - 1) // tk + 1
    hw = IDX_CHUNK
    iq = iq_ref[...]
    lane = lax.broadcasted_iota(jnp.int32, (tq, LANE), 1)
    qs = []
    for h in range(IDX_HEADS):
        blk = iq[:, (h // 2) * LANE:(h // 2 + 1) * LANE]
        keep = (lane >= IDX_DIM) if h % 2 else (lane < IDX_DIM)
        qs.append(jnp.where(keep, blk, jnp.zeros_like(blk)))
    q8 = jnp.concatenate(qs, axis=0)
    iwt = iw_ref[...].astype(F32).T
    wrow = [iwt[h:h + 1, :] for h in range(IDX_HEADS)]
    qpos = lax.broadcasted_iota(jnp.int32, (hw, tq), 1) + t0
    krow = lax.broadcasted_iota(jnp.int32, (hw, tq), 0)

    def fold(x, op):
        out = x[0:8]
        for r in range(1, x.shape[0] // 8):
            out = op(out, x[r * 8:(r + 1) * 8])
        return out

    def score_chunk(c, carry):
        mn, mx = carry
        for half in range(tk // hw):
            k0 = pl.multiple_of(c * tk + half * hw, hw)
            res = _dot_nt(ik_ref[pl.ds(k0, hw), :], q8)
            sc = wrow[0] * jnp.maximum(res[:, 0:tq], 0.0)
            for h in range(1, IDX_HEADS):
                sc = sc + wrow[h] * jnp.maximum(res[:, h * tq:(h + 1) * tq], 0.0)
            causal = (krow + k0) <= qpos
            mn = jnp.minimum(mn, fold(jnp.where(causal, sc, -NEG), jnp.minimum))
            sc = jnp.where(causal, sc, NEG)
            mx = jnp.maximum(mx, fold(sc, jnp.maximum))
            sc_ref[c, half * hw:(half + 1) * hw, :] = sc
        return mn, mx

    mn, mx = lax.fori_loop(0, nch, score_chunk, (jnp.full((8, tq), -NEG, F32), jnp.full((8, tq), NEG, F32)))
    n_causal = (lax.broadcasted_iota(jnp.int32, (1, tq), 1) + (t0 + 1)).astype(F32)
    kf = jnp.minimum(float(topk), n_causal)

    def count(pred):
        def body(c, acc):
            return acc + fold(jnp.where(pred(sc_ref[c]), 1.0, 0.0), jnp.add)

        return jnp.sum(lax.fori_loop(0, nch, body, jnp.zeros((8, tq), F32)), axis=0, keepdims=True)

    lo0 = jnp.min(mn, axis=0, keepdims=True)
    mx = jnp.max(mx, axis=0, keepdims=True)
    hi0 = mx + (jnp.abs(mx) * 1e-6 + 1.0)

    def bisect(c):
        lo, hi, n_lo, n_hi = c
        mid = 0.5 * (lo + hi)
        n_mid = count(lambda blk: blk >= mid)
        ge = n_mid >= kf
        return (jnp.where(ge, mid, lo), jnp.where(ge, hi, mid),
                jnp.where(ge, n_mid, n_lo), jnp.where(ge, n_hi, n_mid))

    def open_rows(n_lo, n_hi, settled):
        return (n_lo - kf) * jnp.where(n_lo - n_hi > 2.0, 1.0, 0.0) * (1.0 - settled)

    def search(state, limit, settled):
        def cond(c):
            return jnp.logical_and(c[0] < limit, jnp.max(open_rows(c[3], c[4], settled)) > 0.0)

        def body(c):
            return (c[0] + 2,) + bisect(bisect(c[1:]))

        return lax.while_loop(cond, body, state)

    def band_min_max(lo, hi):
        def body(c, carry):
            blk = sc_ref[c]
            inside = (blk >= lo) & (blk < hi)
            return (jnp.minimum(carry[0], fold(jnp.where(inside, blk, -NEG), jnp.minimum)),
                    jnp.maximum(carry[1], fold(jnp.where(inside, blk, NEG), jnp.maximum)))

        bmin, bmax = lax.fori_loop(0, nch, body, (jnp.full((8, tq), -NEG, F32), jnp.full((8, tq), NEG, F32)))
        return jnp.min(bmin, axis=0, keepdims=True), jnp.max(bmax, axis=0, keepdims=True)

    def finish(state):
        _, lo, hi, n_lo, n_hi = state
        over = n_lo > kf
        bmin, bmax = lax.cond(jnp.max(n_lo - kf) > 0.0, lambda: band_min_max(lo, hi), lambda: (lo, lo))
        single = bmin == bmax
        split = over & (n_lo - n_hi == 2.0) & jnp.logical_not(single)
        unsplit = over & jnp.logical_not(split)
        return jnp.where(split, bmax, lo), jnp.where(over & single, 1.0, 0.0), jnp.where(unsplit, 1.0, 0.0)

    zero = jnp.zeros((1, tq), F32)
    state1 = search((jnp.int32(0), lo0, hi0, n_causal, zero), IDX_TIE_CHECK, zero)
    first = finish(state1)
    state = search(state1, iters, first[1])
    lo, _, unsplit = lax.cond(state[0] > state1[0], lambda: finish(state), lambda: first)
    hi, n_hi = state[2], state[4]
    tied = jnp.max(unsplit) > 0.0

    def fill(j, c):
        mb_ref[j] = jnp.full((tq, tk), NEG, mb_ref.dtype)
        return c

    lax.fori_loop(nch, nk, fill, 0)

    def emit(j, keep):
        mb_ref[j] = jnp.where(keep, 0.0, NEG).T.astype(mb_ref.dtype)

    @pl.when(jnp.logical_not(tied))
    def _():
        def body(j, c):
            emit(j, sc_ref[j] >= lo)
            return c

        lax.fori_loop(0, nch, body, 0)

    @pl.when(tied)
    def _():
        need = kf - n_hi
        tri = tri_ref[...]

        def body(j, base):
            sc = sc_ref[j]
            above = sc >= hi
            band = (sc >= lo) & jnp.logical_not(above)
            prefix = _dot(tri, jnp.where(band, 1.0, 0.0).astype(BF16)) + base
            emit(j, above | (band & (prefix <= need)))
            return prefix[tk - 1:tk, :]

        lax.fori_loop(0, nch, body, zero)


def dsa_index_mask(proj, *, B, S, tqi, tk, topk):
    nq = S // tqi
    nk = S // tk
    assert tk % tqi == 0 and tk % IDX_CHUNK == 0
    tri = jnp.asarray(np.tril(np.ones((tk, tk), np.float32)), BF16)
    return pl.pallas_call(
        functools.partial(_dsa_index_kernel_t, tq=tqi, tk=tk, S=S, topk=topk, iters=BISECT_ITERS),
        grid=(B, nq),
        in_specs=[pl.BlockSpec((tqi, 512), lambda b, i: (b * nq + i, OFF_IDX_Q // 512)),
                  pl.BlockSpec((S, LANE), lambda b, i: (b, OFF_IDX_K // LANE)),
                  pl.BlockSpec((tqi, LANE), lambda b, i: (b * nq + i, OFF_IDX_W // LANE)),
                  pl.BlockSpec((tk, tk), lambda b, i: (0, 0))],
        out_specs=pl.BlockSpec((None, nk, tqi, tk), lambda b, i: (b, 0, i, 0)),
        out_shape=jax.ShapeDtypeStruct((B, nk, S, tk), BF16),
        scratch_shapes=[pltpu.VMEM((nk, tk, tqi), F32)],
        compiler_params=_cparams(("parallel", "parallel")),
        name="dsa_index_mask",
    )(proj, proj, proj, tri)


def _dsa_attn_kernel(q_ref, k_ref, v_ref, mb_ref, bias_ref, o_ref, m_ref, acc_ref, *, tq, scale):
    qi = pl.program_id(2)
    hs = range(HPS)
    cols = lambda h: slice(h * LANE, (h + 1) * LANE)
    q = [q_ref[:, cols(h)] for h in hs]
    for h in hs:
        _softmax_init(m_ref.at[h], acc_ref.at[h])

    def step(j, nc, bias):
        sl = pl.ds(j * tq if isinstance(j, int) else pl.multiple_of(j * tq, tq), nc * tq)
        mb = mb_ref[j] if nc == 1 else jnp.concatenate([mb_ref[j + i] for i in range(nc)], axis=1)
        mb = mb.astype(F32)
        s = [_dot_nt(q[h], k_ref[sl, cols(h)]) + mb for h in hs]
        for h in hs:
            sh = s[h] if bias is None else s[h] + bias(h)
            _softmax_step(sh, v_ref[sl, cols(h)], m_ref.at[h], acc_ref.at[h], scale)

    _far_chunks(jnp.maximum(qi - 1, 0), lambda j, nc: step(j, nc, None), FAR_GROUP)

    @pl.when(qi == 0)
    def _():
        step(0, 1, lambda h: bias_ref[h, :, tq:2 * tq])

    @pl.when(qi >= 1)
    def _():
        step(qi - 1, 2, lambda h: bias_ref[h])

    for h in hs:
        o_ref[:, cols(h)] = _softmax_finish(acc_ref.at[h]).astype(o_ref.dtype)


def dsa_attention(proj, maskb, btiles, *, B, S, tq):
    M = B * S
    nq = S // tq
    H = DSA_HEADS
    scale = DSA_HEAD_DIM ** -0.5
    w = HPS * LANE
    return pl.pallas_call(
        functools.partial(_dsa_attn_kernel, tq=tq, scale=scale),
        grid=(B, H // HPS, nq),
        in_specs=[pl.BlockSpec((tq, w), lambda b, h, i: (b * nq + i, OFF_DSA_Q // w + h)),
                  pl.BlockSpec((S, w), lambda b, h, i: (b, OFF_DSA_K // w + h)),
                  pl.BlockSpec((S, w), lambda b, h, i: (b, OFF_DSA_V // w + h)),
                  pl.BlockSpec((None, nq, tq, tq), lambda b, h, i: (b, 0, i, 0)),
                  pl.BlockSpec((HPS, tq, 2 * tq), lambda b, h, i: (h, 0, 0))],
        out_specs=pl.BlockSpec((tq, w), lambda b, h, i: (b * nq + i, h)),
        out_shape=jax.ShapeDtypeStruct((M, H * DSA_HEAD_DIM), BF16),
        scratch_shapes=[pltpu.VMEM((HPS, tq, LANE), F32), pltpu.VMEM((HPS, tq, 2 * LANE), F32)],
        compiler_params=_cparams(("parallel", "parallel", "arbitrary")),
        name="dsa_attention",
    )(proj, proj, proj, maskb, btiles)


def _nsa_compress_kernel(x_ref, pos_ref, w1_ref, w2_ref, o_ref, *, ncp):
    dk = NSA_HEAD_DIM
    posw = _dot(pos_ref[...], w1_ref[...])[0:1, :]
    for g in range(NSA_GROUPS):
        lo = jnp.zeros((ncp, CMP_HIDDEN), F32)
        hi = jnp.zeros((ncp, CMP_HIDDEN), F32)
        for l in range(CMP_STRIDE):
            xs = x_ref[:, (l * NSA_GROUPS + g) * dk:(l * NSA_GROUPS + g + 1) * dk]
            lo = lo + _dot(xs, w1_ref[l * dk:(l + 1) * dk, :])
            hi = hi + _dot(xs, w1_ref[(CMP_STRIDE + l) * dk:(CMP_STRIDE + l + 1) * dk, :])
        hid = lo + pltpu.roll(hi, ncp - 1, 0) + posw
        o_ref[g] = _dot(jax.nn.gelu(hid).astype(BF16), w2_ref[...]).astype(o_ref.dtype)


def nsa_compress(xkv, posflat, w1, w2, *, B, S):
    ncp = S // CMP_STRIDE
    G, dk = NSA_GROUPS, NSA_HEAD_DIM
    width = CMP_STRIDE * G * dk
    return pl.pallas_call(
        functools.partial(_nsa_compress_kernel, ncp=ncp),
        grid=(2, B),
        in_specs=[pl.BlockSpec((None, ncp, width), lambda a, b: (a, b, 0)),
                  pl.BlockSpec((None, 8, CMP_LEN * dk), lambda a, b: (a, 0, 0)),
                  pl.BlockSpec((None, CMP_LEN * dk, CMP_HIDDEN), lambda a, b: (a, 0, 0)),
                  pl.BlockSpec((None, CMP_HIDDEN, dk), lambda a, b: (a, 0, 0))],
        out_specs=pl.BlockSpec((None, None, G, ncp, dk), lambda a, b: (a, b, 0, 0, 0)),
        out_shape=jax.ShapeDtypeStruct((2, B, G, ncp, dk), BF16),
        compiler_params=_cparams(("parallel", "parallel")),
        name="nsa_compress",
    )(xkv, posflat, w1, w2)


def _nsa_cmp_kernel(q_ref, kc_ref, vc_ref, ov_ref, oc_ref, sel_ref, *, tq, ncp, ns, n_sel, scale):
    t0 = pl.program_id(2) * tq
    dk = NSA_HEAD_DIM
    kc = kc_ref[...]
    vc = vc_ref[...]
    trow = lax.broadcasted_iota(jnp.int32, (tq, ncp), 0) + t0
    ccol = lax.broadcasted_iota(jnp.int32, (tq, ncp), 1)
    vis = (ccol * CMP_STRIDE + (CMP_LEN - 1)) <= trow
    psum = jnp.zeros((tq, ncp), F32)
    for h in range(NSA_HG):
        s = jnp.where(vis, _dot_nt(q_ref[:, h * dk:(h + 1) * dk], kc) * scale, NEG)
        m = jnp.max(s, axis=-1, keepdims=True)
        p = jnp.where(vis, jnp.exp(s - m), 0.0)
        d = jnp.sum(p, axis=-1, keepdims=True)
        p = p / jnp.where(d > 0, d, 1.0)
        oc_ref[:, h * dk:(h + 1) * dk] = _dot(p.astype(BF16), vc).astype(oc_ref.dtype)
        psum = psum + p
    p_hi = psum.astype(BF16)
    p_lo = (psum - p_hi.astype(F32)).astype(BF16)
    ov = ov_ref[...]
    imp = _dot_nt(ov, p_hi) + _dot_nt(ov, p_lo)
    imp = imp[:ns]
    jrow = lax.broadcasted_iota(jnp.int32, (ns, tq), 0)
    tcol = lax.broadcasted_iota(jnp.int32, (ns, tq), 1) + t0
    blk_t = tcol // SEL_BLOCK
    forced = (jrow == 0) | (jrow == blk_t) | (jrow == blk_t - 1)
    val = jnp.where(forced, -NEG, imp)
    val = jnp.where(jrow * SEL_BLOCK > tcol, NEG, val)
    rank = jnp.zeros((ns, tq), F32)
    for j in range(ns):
        other = val[j:j + 1, :]
        ahead = (other > val) | ((other == val) & (jrow > j))
        rank = rank + jnp.where(ahead, 1.0, 0.0)
    selneg = jnp.where(rank < float(n_sel), 0.0, NEG)
    if ns < LANE:
        selneg = jnp.concatenate([selneg, jnp.zeros((LANE - ns, tq), F32)], axis=0)
    sel_ref[...] = selneg.T.astype(sel_ref.dtype)


def nsa_cmp_attention(proj, kvc, ovt, *, B, S, tq):
    M = B * S
    nq = S // tq
    G, dk = NSA_GROUPS, NSA_HEAD_DIM
    ncp = S // CMP_STRIDE
    ns = S // SEL_BLOCK
    assert ns <= LANE
    n_sel = min(SEL_COUNT, ns)
    gw = NSA_HG * dk
    return pl.pallas_call(
        functools.partial(_nsa_cmp_kernel, tq=tq, ncp=ncp, ns=ns, n_sel=n_sel, scale=dk ** -0.5),
        grid=(B, G, nq),
        in_specs=[pl.BlockSpec((tq, gw), lambda b, g, i: (b * nq + i, OFF_NSA_Q // gw + g)),
                  pl.BlockSpec((None, None, None, ncp, dk), lambda b, g, i: (0, b, g, 0, 0)),
                  pl.BlockSpec((None, None, None, ncp, dk), lambda b, g, i: (1, b, g, 0, 0)),
                  pl.BlockSpec((LANE, ncp), lambda b, g, i: (0, 0))],
        out_specs=[pl.BlockSpec((tq, gw), lambda b, g, i: (b * nq + i, g)),
                   pl.BlockSpec((None, None, tq, LANE), lambda b, g, i: (b, g, i, 0))],
        out_shape=[jax.ShapeDtypeStruct((M, G * gw), BF16),
                   jax.ShapeDtypeStruct((B, G, S, LANE), BF16)],
        compiler_params=_cparams(("parallel", "parallel", "parallel")),
        name="nsa_cmp_attention",
    )(proj, kvc, kvc, ovt)


def _nsa_main_kernel(q_ref, sel_ref, ks_ref, vs_ref, kw_ref, vw_ref, ind_ref, bias_ref, gate_ref, oc_ref,
                     o_ref, m_ref, acc_ref, *, tq, nback, scale):
    g = pl.program_id(1)
    qi = pl.program_id(2)
    dk = NSA_HEAD_DIM
    HG = NSA_HG
    ns = NSA_STREAMS
    hp = HG // ns
    st = range(ns)
    stack = lambda f: [jnp.concatenate([f(a * hp + i) for i in range(hp)], axis=0) for a in st]
    q = stack(lambda h: q_ref[:, h * dk:(h + 1) * dk])
    sel = jnp.concatenate([sel_ref[...]] * hp, axis=0)
    qa = [jnp.concatenate([q[a], sel], axis=1) for a in st]
    gate = jax.nn.sigmoid(gate_ref[...].astype(F32))

    for a in range(2 * ns):
        _softmax_init(m_ref.at[a], acc_ref.at[a])

    def steps(s, v, fix, base=0):
        for a in st:
            _softmax_step(fix(a, s[a]), v, m_ref.at[base + a], acc_ref.at[base + a], scale)

    wspan = nback + 1

    def rows_from(j, width):
        return pl.ds(j * tq if isinstance(j, int) else pl.multiple_of(j * tq, tq), width)

    def sel_scores(j, width=tq):
        sl = rows_from(j, width)
        ka = jnp.concatenate([ks_ref[sl, :], ind_ref[sl, :]], axis=1)
        return [_dot_nt(qa[a], ka) for a in st], vs_ref[sl, :]

    def sel_far(j, nc):
        s, v = sel_scores(j, nc * tq)
        steps(s, v, lambda a, x: x)

    _far_chunks(jnp.maximum(qi - 1, 0), sel_far, FAR_GROUP_NSA)

    def win_scores(j, width):
        sl = rows_from(j, width)
        kw = kw_ref[sl, :]
        return [_dot_nt(q[a], kw) for a in st], vw_ref[sl, :]

    def near(first_s, nc_s, first_w, nc_w):
        tile = lambda nc: (lambda a, x: x + bias_ref[a, :, (wspan - nc) * tq:wspan * tq])
        s_s, v_s = sel_scores(first_s, nc_s * tq)
        s_w, v_w = win_scores(first_w, nc_w * tq)
        steps(s_s, v_s, tile(nc_s))
        steps(s_w, v_w, tile(nc_w), base=ns)

    for k in range(wspan - 1):
        @pl.when(qi == k)
        def _(k=k):
            near(max(k - 1, 0), min(k + 1, 2), 0, k + 1)

    @pl.when(qi >= wspan - 1)
    def _():
        near(qi - 1, 2, qi - (wspan - 1), wspan)

    o_s = [_softmax_finish(acc_ref.at[a]) for a in st]
    o_w = [_softmax_finish(acc_ref.at[ns + a]) for a in st]

    lane = lax.broadcasted_iota(jnp.int32, gate.shape, 1)
    for h in range(HG):
        c = g * HG + h
        a, sl = h // hp, slice((h % hp) * tq, (h % hp + 1) * tq)
        gc = [jnp.sum(jnp.where(lane == br * NSA_HEADS + c, gate, 0.0), axis=-1, keepdims=True)
              for br in range(3)]
        o = gc[0] * oc_ref[:, h * dk:(h + 1) * dk].astype(F32) + gc[1] * o_s[a][sl] + gc[2] * o_w[a][sl]
        o_ref[:, h * dk:(h + 1) * dk] = o.astype(o_ref.dtype)


def nsa_main(proj, selneg, ind, btiles, o_c, *, B, S, tq):
    M = B * S
    nq = S // tq
    G, dk, HG = NSA_GROUPS, NSA_HEAD_DIM, NSA_HG
    gw = HG * dk
    assert WINDOW % tq == 0 and WINDOW // tq >= 2
    nback = WINDOW // tq
    ns, hp = NSA_STREAMS, HG // NSA_STREAMS
    kv = lambda n: pl.BlockSpec((S, dk), lambda b, g, i, n=n: (b, (OFF_NSA_KV + n * G * dk) // dk + g))
    return pl.pallas_call(
        functools.partial(_nsa_main_kernel, tq=tq, nback=nback, scale=dk ** -0.5),
        grid=(B, G, nq),
        in_specs=[pl.BlockSpec((tq, gw), lambda b, g, i: (b * nq + i, OFF_NSA_Q // gw + g)),
                  pl.BlockSpec((None, None, tq, LANE), lambda b, g, i: (b, g, i, 0)),
                  kv(2), kv(3), kv(4), kv(5),
                  pl.BlockSpec((S, LANE), lambda b, g, i: (0, 0)),
                  pl.BlockSpec((ns, hp * tq, (nback + 1) * tq), lambda b, g, i: (g, 0, 0)),
                  pl.BlockSpec((tq, LANE), lambda b, g, i: (b * nq + i, OFF_NSA_GATE // LANE)),
                  pl.BlockSpec((tq, gw), lambda b, g, i: (b * nq + i, g))],
        out_specs=pl.BlockSpec((tq, gw), lambda b, g, i: (b * nq + i, g)),
        out_shape=jax.ShapeDtypeStruct((M, G * gw), BF16),
        scratch_shapes=[pltpu.VMEM((2 * ns, hp * tq, LANE), F32),
                        pltpu.VMEM((2 * ns, hp * tq, 2 * LANE), F32)],
        compiler_params=_cparams(("parallel", "parallel", "arbitrary")),
        name="nsa_main",
    )(proj, selneg, proj, proj, proj, proj, ind, btiles, proj, o_c)


def _merge_kernel(x_ref, oa_ref, ob_ref, oc_ref, g0_ref, g1_ref, g2_ref, wb_ref, wo_ref, gain_ref, o_ref):
    merged = None
    for o_r, g_r, br in ((oa_ref, g0_ref, 0), (ob_ref, g1_ref, 1), (oc_ref, g2_ref, 2)):
        t = jax.nn.sigmoid(g_r[...].astype(F32)) * _dot(o_r[...], wb_ref[br])
        merged = t if merged is None else merged + t
    y = _dot(merged.astype(BF16), wo_ref[...])
    o_ref[...] = x_ref[...] + _rms(y, gain_ref[...])


def merge_branches(x, o_a, o_b, o_c, proj, wb, wo, gain, *, tm):
    M, D = x.shape
    row = pl.BlockSpec((tm, D), lambda i: (i, 0))
    gate = lambda br: pl.BlockSpec((tm, D), lambda i, br=br: (i, OFF_BGATE // D + br))
    return pl.pallas_call(
        _merge_kernel,
        grid=(M // tm,),
        in_specs=[row, row, row, row, gate(0), gate(1), gate(2),
                  pl.BlockSpec(wb.shape, lambda i: (0, 0, 0)),
                  pl.BlockSpec(wo.shape, lambda i: (0, 0)),
                  pl.BlockSpec((1, D), lambda i: (0, 0))],
        out_specs=row,
        out_shape=jax.ShapeDtypeStruct((M, D), F32),
        compiler_params=_cparams(("parallel",)),
        name="merge_branches",
    )(x, o_a, o_b, o_c, proj, proj, proj, wb, wo, gain)


def _cross_kernel(x_ref, kv_ref, wq_ref, wo_ref, gpre_ref, gpost_ref, o_ref, *, scale):
    x = x_ref[...]
    h = _rms(x, gpre_ref[...]).astype(BF16)
    q = _dot(h, wq_ref[...]).astype(BF16)
    dh = CROSS_HEAD_DIM
    outs = []
    for hd in range(CROSS_HEADS):
        k = kv_ref[:, hd * dh:(hd + 1) * dh]
        v = kv_ref[:, D_MODEL + hd * dh:D_MODEL + (hd + 1) * dh]
        s = _dot_nt(q[:, hd * dh:(hd + 1) * dh], k) * scale
        p = jnp.exp(s - jnp.max(s, axis=-1, keepdims=True))
        p = p / jnp.sum(p, axis=-1, keepdims=True)
        outs.append(_dot(p.astype(BF16), v).astype(BF16))
    y = _dot(jnp.concatenate(outs, axis=1), wo_ref[...])
    o_ref[...] = x + _rms(y, gpost_ref[...])


def cross_attention(x, kv, wq, wo, gpre, gpost, *, S, tm):
    M, D = x.shape
    mlen = kv.shape[0] // (M // S)
    per_b = S // tm
    return pl.pallas_call(
        functools.partial(_cross_kernel, scale=CROSS_HEAD_DIM ** -0.5),
        grid=(M // tm,),
        in_specs=[pl.BlockSpec((tm, D), lambda i: (i, 0)),
                  pl.BlockSpec((mlen, 2 * D), lambda i: (i // per_b, 0)),
                  pl.BlockSpec(wq.shape, lambda i: (0, 0)),
                  pl.BlockSpec(wo.shape, lambda i: (0, 0)),
                  pl.BlockSpec((1, D), lambda i: (0, 0)),
                  pl.BlockSpec((1, D), lambda i: (0, 0))],
        out_specs=pl.BlockSpec((tm, D), lambda i: (i, 0)),
        out_shape=jax.ShapeDtypeStruct((M, D), F32),
        compiler_params=_cparams(("parallel",)),
        name="cross_attention",
    )(x, kv, wq, wo, gpre, gpost)


def _mlp_kernel(x_ref, w1_ref, w2_ref, gpre_ref, gpost_ref, o_ref, h_ref, acc_ref):
    j = pl.program_id(1)

    @pl.when(j == 0)
    def _():
        h_ref[...] = _rms(x_ref[...], gpre_ref[...]).astype(BF16)
        acc_ref[...] = jnp.zeros(acc_ref.shape, F32)

    a = jnp.maximum(_dot(h_ref[...], w1_ref[...]), 0.0)
    acc_ref[...] += _dot((a * a).astype(BF16), w2_ref[...])

    @pl.when(j == pl.num_programs(1) - 1)
    def _():
        o_ref[...] = x_ref[...] + _rms(acc_ref[...], gpost_ref[...])


def mlp(x, w1, w2, gpre, gpost, *, tm, tf):
    M, D = x.shape
    FF = w1.shape[1]
    return pl.pallas_call(
        _mlp_kernel,
        grid=(M // tm, FF // tf),
        in_specs=[pl.BlockSpec((tm, D), lambda i, j: (i, 0)),
                  pl.BlockSpec((D, tf), lambda i, j: (0, j)),
                  pl.BlockSpec((tf, D), lambda i, j: (j, 0)),
                  pl.BlockSpec((1, D), lambda i, j: (0, 0)),
                  pl.BlockSpec((1, D), lambda i, j: (0, 0))],
        out_specs=pl.BlockSpec((tm, D), lambda i, j: (i, 0)),
        out_shape=jax.ShapeDtypeStruct((M, D), F32),
        scratch_shapes=[pltpu.VMEM((tm, D), BF16), pltpu.VMEM((tm, D), F32)],
        compiler_params=_cparams(("parallel", "arbitrary")),
        name="mlp",
    )(x, w1, w2, gpre, gpost)


def _rot_half_cols(w):
    half = w.shape[-1] // 2
    return jnp.concatenate([-w[..., half:], w[..., :half]], axis=-1)


def _pack_w_in(w):
    widths = (MLA_Q_LORA, MLA_KV_LORA, MLA_ROPE, 1024, 1024, 1024, IDX_HEADS * IDX_DIM, IDX_DIM, IDX_HEADS,
              1024, 256, 256, 256, 256, 256, 256, 3 * NSA_HEADS, N_BRANCH * D_MODEL)
    offs = np.concatenate([[0], np.cumsum(widths)])
    seg = [w[:, offs[i]:offs[i + 1]] for i in range(len(widths))]
    (q_lat, kv_lat, k_rope, dsa_q, dsa_k, dsa_v, idx_q, idx_k, idx_w,
     nsa_q, kc, vc, ks, vs, kw, vw, nsa_gate, bgate) = seg
    z = lambda n: jnp.zeros((w.shape[0], n), w.dtype)
    cols = [dsa_q, dsa_k, dsa_v, nsa_q, bgate, idx_q, kc, vc, ks, vs, kw, vw,
            q_lat, kv_lat, k_rope, z(LANE - MLA_ROPE), _rot_half_cols(k_rope), z(LANE - MLA_ROPE), z(LANE),
            idx_k, idx_k, idx_w, z(LANE - IDX_HEADS), nsa_gate, z(LANE - 3 * NSA_HEADS)]
    out = jnp.concatenate(cols, axis=1)
    out = jnp.concatenate([out, z(N_PACK - out.shape[1])], axis=1)
    return out.astype(BF16)


def _pack_mla_weights(w_uq, w_ukv):
    H = MLA_HEADS
    uq = w_uq.reshape(MLA_Q_LORA, H, MLA_NOPE + MLA_ROPE)
    nope = uq[:, :, :MLA_NOPE].reshape(MLA_Q_LORA, H * MLA_NOPE)
    rope = uq[:, :, MLA_NOPE:]
    pad = jnp.zeros((MLA_Q_LORA, H, LANE - MLA_ROPE), w_uq.dtype)
    wqr = jnp.concatenate([rope, pad], axis=-1).reshape(MLA_Q_LORA, H * LANE)
    wqrr = jnp.concatenate([_rot_half_cols(rope), pad], axis=-1).reshape(MLA_Q_LORA, H * LANE)
    ukv = w_ukv.reshape(MLA_KV_LORA, H, MLA_NOPE + MLA_V)
    wkv = jnp.concatenate([ukv[:, :, :MLA_NOPE].reshape(MLA_KV_LORA, H * MLA_NOPE),
                           ukv[:, :, MLA_NOPE:].reshape(MLA_KV_LORA, H * MLA_V)], axis=1)
    return nope.astype(BF16), wqr.astype(BF16), wqrr.astype(BF16), wkv.astype(BF16)


def _constants(S):
    half = MLA_ROPE // 2
    inv_freq = ROPE_BASE ** (-np.arange(0, MLA_ROPE, 2, dtype=np.float32) / MLA_ROPE)
    freq = np.zeros((1, LANE), np.float32)
    freq[0, :half] = inv_freq
    freq[0, half:2 * half] = inv_freq
    ncp = S // CMP_STRIDE
    ns = S // SEL_BLOCK
    c_start = np.arange(ncp) * CMP_STRIDE
    j_start = np.arange(ns) * SEL_BLOCK
    overlap = ((c_start[None, :] < j_start[:, None] + SEL_BLOCK) &
               (c_start[None, :] + CMP_LEN > j_start[:, None])).astype(np.float32)
    overlap[:, ncp - 1] = 0.0
    ovt = np.zeros((LANE, ncp), np.float32)
    ovt[:ns] = overlap
    ind = np.zeros((S, LANE), np.float32)
    ind[np.arange(S), np.arange(S) // SEL_BLOCK] = 1.0
    return jnp.asarray(freq), jnp.asarray(ovt, BF16), jnp.asarray(ind, BF16)


def kernel(x, mem, positions, rel_bias, norm_gains, w_in, mla_q_norm, mla_kv_norm, mla_w_uq, mla_w_ukv,
           nsa_cmp_pos, nsa_cmp_w1, nsa_cmp_w2, w_branch, w_out, cross_wq, cross_wkv, cross_wo, mlp_w1, mlp_w2):
    B, S, D = x.shape
    M = B * S
    depth = w_in.shape[0]
    tq = TQ if S % TQ == 0 else TQ_NSA
    tqn = TQ_NSA
    assert S % tq == 0 and S % tqn == 0 and D == D_MODEL
    topk = min(DSA_TOPK_MAX, S // 4)
    freq, ovt, ind = _constants(S)
    btiles = bias_tiles(rel_bias, tq, 0, DSA_HEADS, nspan=2, masked=False, stack=1)
    btiles_n = bias_tiles(rel_bias, tqn, DSA_HEADS, NSA_HEADS, nspan=WINDOW // tqn + 1, masked=True,
                          stack=NSA_HG // NSA_STREAMS)
    pos = positions.reshape(M, 1)
    xf = x.reshape(M, D)
    memf = mem.reshape(B * mem.shape[1], D)
    tm_big = 1024 if M % 1024 == 0 else tq
    tm = 512 if M % 512 == 0 else tq
    ncp = S // CMP_STRIDE
    row = lambda v: v.reshape(1, -1)

    for l in range(depth):
        g = norm_gains[l]
        proj = norm_matmul(xf, row(g[0]), _pack_w_in(w_in[l]), tm=tm_big, tn=TN_IN)

        wqn, wqr, wqrr, wkv = _pack_mla_weights(mla_w_uq[l], mla_w_ukv[l])
        qn, qr, kn, vv, kr = mla_prep(proj, pos, freq, row(mla_q_norm[l]), row(mla_kv_norm[l]),
                                      wqn, wqr, wqrr, wkv, tm=tm)
        o_a = mla_attention(qn, qr, kn, kr, vv, B=B, S=S, tq=tq)

        maskb = dsa_index_mask(proj, B=B, S=S, tqi=TQ_IDX, tk=tq, topk=topk)
        o_b = dsa_attention(proj, maskb, btiles, B=B, S=S, tq=tq)

        kvc_in = jnp.stack([
            proj[:, OFF_NSA_KV + a * 256:OFF_NSA_KV + (a + 1) * 256].reshape(B * ncp, CMP_STRIDE * 256)
            for a in range(2)])
        posflat = jnp.broadcast_to(nsa_cmp_pos[l].reshape(2, 1, CMP_LEN * NSA_HEAD_DIM),
                                   (2, 8, CMP_LEN * NSA_HEAD_DIM)).astype(BF16)
        kvc = nsa_compress(kvc_in, posflat, nsa_cmp_w1[l].astype(BF16), nsa_cmp_w2[l].astype(BF16), B=B, S=S)
        o_cmp, selneg = nsa_cmp_attention(proj, kvc, ovt, B=B, S=S, tq=tqn)
        o_c = nsa_main(proj, selneg, ind, btiles_n, o_cmp, B=B, S=S, tq=tqn)

        xf = merge_branches(xf, o_a, o_b, o_c, proj, w_branch[l].astype(BF16), w_out[l].astype(BF16),
                            row(g[1]), tm=tm)

        mkv = norm_matmul(memf, row(g[3]), cross_wkv[l].astype(BF16), tm=memf.shape[0] // B, tn=1024)
        xf = cross_attention(xf, mkv, cross_wq[l].astype(BF16), cross_wo[l].astype(BF16),
                             row(g[2]), row(g[4]), S=S, tm=tm)

        xf = mlp(xf, mlp_w1[l].astype(BF16), mlp_w2[l].astype(BF16), row(g[5]), row(g[6]), tm=tm_big, tf=1024)

    return xf.reshape(B, S, D)
```

```python
import functools
import math

import numpy as np
import jax
import jax.numpy as jnp
from jax import lax
from jax.experimental import pallas as pl
from jax.experimental.pallas import tpu as pltpu

F32 = jnp.float32
BF16 = jnp.bfloat16

D_MODEL = 1024
EPS = 1e-6
MLA_HEADS = 8
MLA_Q_LORA = 384
MLA_KV_LORA = 256
MLA_NOPE = 128
MLA_ROPE = 64
MLA_V = 128
ROPE_BASE = 10000.0
DSA_HEADS = 8
DSA_HEAD_DIM = 128
IDX_HEADS = 8
IDX_DIM = 64
DSA_TOPK_MAX = 256
NSA_HEADS = 8
NSA_GROUPS = 2
NSA_HG = NSA_HEADS // NSA_GROUPS
NSA_HEAD_DIM = 128
CMP_STRIDE = 16
CMP_LEN = 2 * CMP_STRIDE
CMP_HIDDEN = 256
SEL_BLOCK = 64
SEL_COUNT = 16
WINDOW = 512
REL_BUCKETS = 32
REL_MAX_DIST = 128
REL_HEADS = DSA_HEADS + NSA_HEADS
CROSS_HEADS = 4
CROSS_HEAD_DIM = D_MODEL // CROSS_HEADS
D_FF = 4 * D_MODEL
N_BRANCH = 3

LANE = 128
NEG = -1e30
VMEM_LIMIT = 48 * 1024 * 1024

TQ = 512
TQ_NSA = 256
HPS = 2
FAR_GROUP = 4
FAR_GROUP_NSA = 8
NSA_STREAMS = 2
BISECT_ITERS = 32
IDX_TIE_CHECK = 16
TQ_IDX = 512
IDX_CHUNK = 256

OFF_DSA_Q = 0
OFF_DSA_K = 1024
OFF_DSA_V = 2048
OFF_NSA_Q = 3072
OFF_BGATE = 4096
OFF_IDX_Q = 7168
OFF_NSA_KV = 7680
OFF_MLA = 9216
OFF_IDX_K = 10240
OFF_IDX_W = 10368
OFF_NSA_GATE = 10496
N_PACK = 10752
TN_IN = 1536


def _cparams(sem):
    return pltpu.CompilerParams(dimension_semantics=sem, vmem_limit_bytes=VMEM_LIMIT)


def _rms(x, g):
    return x * lax.rsqrt(jnp.mean(x * x, axis=-1, keepdims=True) + EPS) * g


def _dot(a, b):
    return jnp.dot(a, b, preferred_element_type=F32)


def _dot_nt(a, b):
    return lax.dot_general(a, b, (((1,), (1,)), ((), ())), preferred_element_type=F32)


def _norm_mm_kernel(x_ref, g_ref, w_ref, o_ref, xn_ref):
    @pl.when(pl.program_id(1) == 0)
    def _():
        xn_ref[...] = _rms(x_ref[...].astype(F32), g_ref[...]).astype(BF16)

    o_ref[...] = _dot(xn_ref[...], w_ref[...]).astype(o_ref.dtype)


def norm_matmul(x, g, w, *, tm, tn):
    M, K = x.shape
    N = w.shape[1]
    return pl.pallas_call(
        _norm_mm_kernel,
        grid=(M // tm, N // tn),
        in_specs=[pl.BlockSpec((tm, K), lambda i, j: (i, 0)),
                  pl.BlockSpec((1, K), lambda i, j: (0, 0)),
                  pl.BlockSpec((K, tn), lambda i, j: (0, j))],
        out_specs=pl.BlockSpec((tm, tn), lambda i, j: (i, j)),
        out_shape=jax.ShapeDtypeStruct((M, N), BF16),
        scratch_shapes=[pltpu.VMEM((tm, K), BF16)],
        compiler_params=_cparams(("parallel", "arbitrary")),
        name="norm_matmul",
    )(x, g, w)


def _bias_tiles_kernel(tab_ref, bkt_ref, o_ref, *, inv_scale, h0, tq, nspan, masked):
    h = pl.program_id(0) + h0
    bkt = bkt_ref[...]
    far = tab_ref[REL_BUCKETS - 1, h]
    out = jnp.zeros(bkt.shape, F32)
    for b in range(REL_BUCKETS - 1):
        out = jnp.where(bkt == b, (tab_ref[b, h] - far) * inv_scale, out)
    diag, prev = out[0], out[1]
    r, c = _tri(tq, tq)
    if masked:
        diag = jnp.where(r >= c, diag, NEG)
    o_ref[:, (nspan - 1) * tq:nspan * tq] = diag
    o_ref[:, (nspan - 2) * tq:(nspan - 1) * tq] = prev
    for k in range(nspan - 2):
        edge = jnp.where(r < c, 0.0, NEG) if (masked and k == 0) else jnp.zeros((tq, tq), F32)
        o_ref[:, k * tq:(k + 1) * tq] = edge


def _bucket_tiles(tq):
    i = np.arange(tq)[:, None]
    j = np.arange(tq)[None, :]
    d = np.stack([i - j, tq + i - j]).astype(np.int32)
    n = jnp.maximum(jnp.asarray(d), 0)
    exact = REL_BUCKETS // 2
    nf = jnp.maximum(n, 1).astype(F32)
    log_b = exact + (jnp.log(nf / exact) / math.log(REL_MAX_DIST / exact) * (REL_BUCKETS - exact)).astype(jnp.int32)
    return jnp.where(n < exact, n, jnp.minimum(log_b, REL_BUCKETS - 1)).astype(jnp.int32)


def bias_tiles(rel_bias, tq, h0, nh, *, nspan, masked, stack):
    assert tq >= REL_MAX_DIST
    bkt = _bucket_tiles(tq)
    assert DSA_HEAD_DIM == NSA_HEAD_DIM
    return pl.pallas_call(
        functools.partial(_bias_tiles_kernel, inv_scale=DSA_HEAD_DIM ** 0.5, h0=h0, tq=tq, nspan=nspan,
                          masked=masked),
        grid=(nh,),
        in_specs=[pl.BlockSpec(memory_space=pltpu.SMEM),
                  pl.BlockSpec((2, tq, tq), lambda h: (0, 0, 0))],
        out_specs=pl.BlockSpec((None, tq, nspan * tq), lambda h: (h // stack, h % stack, 0)),
        out_shape=jax.ShapeDtypeStruct((nh // stack, stack * tq, nspan * tq), F32),
        compiler_params=_cparams(("arbitrary",)),
        name="bias_tiles",
    )(rel_bias, bkt)


def _mla_prep_kernel(lat_ref, pos_ref, freq_ref, qg_ref, kvg_ref, wqn_ref, wqr_ref, wqrr_ref, wkv_ref,
                     qn_ref, qr_ref, kn_ref, v_ref, kr_ref):
    lat = lat_ref[...].astype(F32)
    qlat = _rms(lat[:, :MLA_Q_LORA], qg_ref[...]).astype(BF16)
    kvlat = _rms(lat[:, MLA_Q_LORA:MLA_Q_LORA + MLA_KV_LORA], kvg_ref[...]).astype(BF16)
    o = MLA_Q_LORA + MLA_KV_LORA
    kr = lat[:, o:o + LANE]
    krr = lat[:, o + LANE:o + 2 * LANE]
    ang = pos_ref[...].astype(F32) * freq_ref[...]
    cos, sin = jnp.cos(ang), jnp.sin(ang)
    cos8 = jnp.concatenate([cos] * MLA_HEADS, axis=1)
    sin8 = jnp.concatenate([sin] * MLA_HEADS, axis=1)
    qn_ref[...] = _dot(qlat, wqn_ref[...]).astype(BF16)
    qr_ref[...] = (_dot(qlat, wqr_ref[...]) * cos8 + _dot(qlat, wqrr_ref[...]) * sin8).astype(BF16)
    kv = _dot(kvlat, wkv_ref[...])
    nk = MLA_HEADS * MLA_NOPE
    kn_ref[...] = kv[:, :nk].astype(BF16)
    v_ref[...] = kv[:, nk:].astype(BF16)
    kr_ref[...] = (kr * cos + krr * sin).astype(BF16)


def mla_prep(proj, pos, freq, qg, kvg, wqn, wqr, wqrr, wkv, *, tm):
    M = proj.shape[0]
    HD = MLA_HEADS * LANE
    full = lambda a: pl.BlockSpec(a.shape, lambda i: (0, 0))
    outs = [jax.ShapeDtypeStruct((M, HD), BF16)] * 4 + [jax.ShapeDtypeStruct((M, LANE), BF16)]
    return pl.pallas_call(
        _mla_prep_kernel,
        grid=(M // tm,),
        in_specs=[pl.BlockSpec((tm, 1024), lambda i: (i, OFF_MLA // 1024)),
                  pl.BlockSpec((tm, 1), lambda i: (i, 0)),
                  full(freq), full(qg), full(kvg), full(wqn), full(wqr), full(wqrr), full(wkv)],
        out_specs=[pl.BlockSpec((tm, HD), lambda i: (i, 0))] * 4 + [pl.BlockSpec((tm, LANE), lambda i: (i, 0))],
        out_shape=outs,
        compiler_params=_cparams(("parallel",)),
        name="mla_prep",
    )(proj, pos, freq, qg, kvg, wqn, wqr, wqrr, wkv)


def _softmax_init(m_ref, acc_ref):
    m_ref[...] = jnp.full(m_ref.shape, NEG, F32)
    acc_ref[...] = jnp.zeros(acc_ref.shape, F32)


def _with_ones(v):
    return jnp.concatenate([v, jnp.ones(v.shape, v.dtype)], axis=1)


def _softmax_step(s, v, m_ref, acc_ref, scale):
    c = scale * math.log2(math.e)
    m_prev = m_ref[...]
    m_new = jnp.maximum(m_prev, jnp.max(s, axis=-1, keepdims=True))
    alpha = jnp.exp2((m_prev - m_new) * c)
    p = jnp.exp2((s - jnp.tile(m_new, (1, s.shape[1] // LANE))) * c)
    acc_ref[...] = jnp.tile(alpha, (1, 2)) * acc_ref[...] + _dot(p.astype(BF16), _with_ones(v))
    m_ref[...] = m_new


def _softmax_finish(acc_ref):
    acc = acc_ref[...]
    return acc[:, :LANE] / acc[:, LANE:]


def _far_chunks(n, step, largest):
    def body(p, c):
        step(largest * p, largest)
        return c

    lax.fori_loop(0, n // largest, body, 0)
    w = largest // 2
    while w >= 1:
        @pl.when(n % (2 * w) >= w)
        def _(w=w):
            step(n // (2 * w) * (2 * w), w)
        w //= 2


def _tri(tq, tk):
    row = lax.broadcasted_iota(jnp.int32, (tq, tk), 0)
    col = lax.broadcasted_iota(jnp.int32, (tq, tk), 1)
    return row, col


def _mla_attn_kernel(qn_ref, qr_ref, kn_ref, kr_ref, v_ref, o_ref, m_ref, acc_ref, *, tq, scale):
    qi = pl.program_id(2)
    hs = range(HPS)
    q = [jnp.concatenate([qn_ref[:, h * LANE:(h + 1) * LANE], qr_ref[:, h * LANE:(h + 1) * LANE]], axis=-1)
         for h in hs]
    for h in hs:
        _softmax_init(m_ref.at[h], acc_ref.at[h])

    def step(j, nc, causal_last):
        sl = pl.ds(j * tq if isinstance(j, int) else pl.multiple_of(j * tq, tq), nc * tq)
        kr = kr_ref[sl, :]
        if causal_last:
            row, col = _tri(tq, nc * tq)
            ok = col <= row + (nc - 1) * tq
        s = [_dot_nt(q[h], jnp.concatenate([kn_ref[sl, h * LANE:(h + 1) * LANE], kr], axis=-1)) for h in hs]
        for h in hs:
            sh = jnp.where(ok, s[h], NEG) if causal_last else s[h]
            _softmax_step(sh, v_ref[sl, h * LANE:(h + 1) * LANE], m_ref.at[h], acc_ref.at[h], scale)

    _far_chunks(jnp.maximum(qi - 1, 0), lambda j, nc: step(j, nc, False), FAR_GROUP)

    @pl.when(qi == 0)
    def _():
        step(0, 1, True)

    @pl.when(qi >= 1)
    def _():
        step(qi - 1, 2, True)

    for h in hs:
        o_ref[:, h * LANE:(h + 1) * LANE] = _softmax_finish(acc_ref.at[h]).astype(o_ref.dtype)


def mla_attention(qn, qr, kn, kr, v, *, B, S, tq):
    M = B * S
    nq = S // tq
    H = MLA_HEADS
    scale = (MLA_NOPE + MLA_ROPE) ** -0.5
    w = HPS * LANE
    qspec = pl.BlockSpec((tq, w), lambda b, h, i: (b * nq + i, h))
    kspec = pl.BlockSpec((S, w), lambda b, h, i: (b, h))
    return pl.pallas_call(
        functools.partial(_mla_attn_kernel, tq=tq, scale=scale),
        grid=(B, H // HPS, nq),
        in_specs=[qspec, qspec, kspec, pl.BlockSpec((S, LANE), lambda b, h, i: (b, 0)), kspec],
        out_specs=qspec,
        out_shape=jax.ShapeDtypeStruct((M, H * MLA_V), BF16),
        scratch_shapes=[pltpu.VMEM((HPS, tq, LANE), F32), pltpu.VMEM((HPS, tq, 2 * LANE), F32)],
        compiler_params=_cparams(("parallel", "parallel", "arbitrary")),
        name="mla_attention",
    )(qn, qr, kn, kr, v)


def _dsa_index_kernel(iq_ref, ik_ref, iw_ref, tri_ref, mb_ref, sc_ref, *, tq, tk, S, topk, iters):
    t0 = pl.program_id(1) * tq
    nk = S // tk
    nch = (t0 + tq - 1) // tk + 1
    hw = IDX_CHUNK
    iq = iq_ref[...]
    lane = lax.broadcasted_iota(jnp.int32, (tq, LANE), 1)
    qs = []
    for h in range(IDX_HEADS):
        blk = iq[:, (h // 2) * LANE:(h // 2 + 1) * LANE]
        keep = (lane >= IDX_DIM) if h % 2 else (lane < IDX_DIM)
        qs.append(jnp.where(keep, blk, jnp.zeros_like(blk)))
    q8 = jnp.concatenate(qs, axis=0)
    iwt = iw_ref[...].astype(F32).T
    wrow = [iwt[h:h + 1, :] for h in range(IDX_HEADS)]
    qpos = lax.broadcasted_iota(jnp.int32, (hw, tq), 1) + t0
    krow = lax.broadcasted_iota(jnp.int32, (hw, tq), 0)

    def fold(x, op):
        out = x[0:8]
        for r in range(1, x.shape[0] // 8):
            out = op(out, x[r * 8:(r + 1) * 8])
        return out

    def score_chunk(c, carry):
        mn, mx = carry
        for half in range(tk // hw):
            k0 = pl.multiple_of(c * tk + half * hw, hw)
            res = _dot_nt(ik_ref[pl.ds(k0, hw), :], q8)
            sc = wrow[0] * jnp.maximum(res[:, 0:tq], 0.0)
            for h in range(1, IDX_HEADS):
                sc = sc + wrow[h] * jnp.maximum(res[:, h * tq:(h + 1) * tq], 0.0)
            causal = (krow + k0) <= qpos
            mn = jnp.minimum(mn, fold(jnp.where(causal, sc, -NEG), jnp.minimum))
            sc = jnp.where(causal, sc, NEG)
            mx = jnp.maximum(mx, fold(sc, jnp.maximum))
            sc_ref[c, half * hw:(half + 1) * hw, :] = sc
        return mn, mx

    mn, mx = lax.fori_loop(0, nch, score_chunk, (jnp.full((8, tq), -NEG, F32), jnp.full((8, tq), NEG, F32)))
    n_causal = (lax.broadcasted_iota(jnp.int32, (1, tq), 1) + (t0 + 1)).astype(F32)
    kf = jnp.minimum(float(topk), n_causal)

    def count(pred):
        def body(c, acc):
            return acc + fold(jnp.where(pred(sc_ref[c]), 1.0, 0.0), jnp.add)

        return jnp.sum(lax.fori_loop(0, nch, body, jnp.zeros((8, tq), F32)), axis=0, keepdims=True)

    lo0 = jnp.min(mn, axis=0, keepdims=True)
    mx = jnp.max(mx, axis=0, keepdims=True)
    hi0 = mx + (jnp.abs(mx) * 1e-6 + 1.0)

    def bisect(c):
        lo, hi, n_lo, n_hi = c
        mid = 0.5 * (lo + hi)
        n_mid = count(lambda blk: blk >= mid)
        ge = n_mid >= kf
        return (jnp.where(ge, mid, lo), jnp.where(ge, hi, mid),
                jnp.where(ge, n_mid, n_lo), jnp.where(ge, n_hi, n_mid))

    def open_rows(n_lo, n_hi, settled):
        return (n_lo - kf) * jnp.where(n_lo - n_hi > 2.0, 1.0, 0.0) * (1.0 - settled)

    def search(state, limit, settled):
        def cond(c):
            return jnp.logical_and(c[0] < limit, jnp.max(open_rows(c[3], c[4], settled)) > 0.0)

        def body(c):
            return (c[0] + 2,) + bisect(bisect(c[1:]))

        return lax.while_loop(cond, body, state)

    def band_min_max(lo, hi):
        def body(c, carry):
            blk = sc_ref[c]
            inside = (blk >= lo) & (blk < hi)
            return (jnp.minimum(carry[0], fold(jnp.where(inside, blk, -NEG), jnp.minimum)),
                    jnp.maximum(carry[1], fold(jnp.where(inside, blk, NEG), jnp.maximum)))

        bmin, bmax = lax.fori_loop(0, nch, body, (jnp.full((8, tq), -NEG, F32), jnp.full((8, tq), NEG, F32)))
        return jnp.min(bmin, axis=0, keepdims=True), jnp.max(bmax, axis=0, keepdims=True)

    def finish(state):
        _, lo, hi, n_lo, n_hi = state
        over = n_lo > kf
        bmin, bmax = lax.cond(jnp.max(n_lo - kf) > 0.0, lambda: band_min_max(lo, hi), lambda: (lo, lo))
        single = bmin == bmax
        split = over & (n_lo - n_hi == 2.0) & jnp.logical_not(single)
        unsplit = over & jnp.logical_not(split)
        return jnp.where(split, bmax, lo), jnp.where(over & single, 1.0, 0.0), jnp.where(unsplit, 1.0, 0.0)

    zero = jnp.zeros((1, tq), F32)
    state1 = search((jnp.int32(0), lo0, hi0, n_causal, zero), IDX_TIE_CHECK, zero)
    first = finish(state1)
    state = search(state1, iters, first[1])
    lo, _, unsplit = lax.cond(state[0] > state1[0], lambda: finish(state), lambda: first)
    hi, n_hi = state[2], state[4]
    tied = jnp.max(unsplit) > 0.0

    def fill(j, c):
        mb_ref[j] = jnp.full((tq, tk), NEG, mb_ref.dtype)
        return c

    lax.fori_loop(nch, nk, fill, 0)

    def emit(j, keep):
        mb_ref[j] = jnp.where(keep, 0.0, NEG).T.astype(mb_ref.dtype)

    @pl.when(jnp.logical_not(tied))
    def _():
        def body(j, c):
            emit(j, sc_ref[j] >= lo)
            return c

        lax.fori_loop(0, nch, body, 0)

    @pl.when(tied)
    def _():
        need = kf - n_hi
        tri = tri_ref[...]

        def body(j, base):
            sc = sc_ref[j]
            above = sc >= hi
            band = (sc >= lo) & jnp.logical_not(above)
            prefix = _dot(tri, jnp.where(band, 1.0, 0.0).astype(BF16)) + base
            emit(j, above | (band & (prefix <= need)))
            return prefix[tk - 1:tk, :]

        lax.fori_loop(0, nch, body, zero)


def dsa_index_mask(proj, *, B, S, tqi, tk, topk):
    nq = S // tqi
    nk = S // tk
    assert tk % tqi == 0 and tk % IDX_CHUNK == 0
    tri = jnp.asarray(np.tril(np.ones((tk, tk), np.float32)), BF16)
    return pl.pallas_call(
        functools.partial(_dsa_index_kernel, tq=tqi, tk=tk, S=S, topk=topk, iters=BISECT_ITERS),
        grid=(B, nq),
        in_specs=[pl.BlockSpec((tqi, 512), lambda b, i: (b * nq + i, OFF_IDX_Q // 512)),
                  pl.BlockSpec((S, LANE), lambda b, i: (b, OFF_IDX_K // LANE)),
                  pl.BlockSpec((tqi, LANE), lambda b, i: (b * nq + i, OFF_IDX_W // LANE)),
                  pl.BlockSpec((tk, tk), lambda b, i: (0, 0))],
        out_specs=pl.BlockSpec((None, nk, tqi, tk), lambda b, i: (b, 0, i, 0)),
        out_shape=jax.ShapeDtypeStruct((B, nk, S, tk), BF16),
        scratch_shapes=[pltpu.VMEM((nk, tk, tqi), F32)],
        compiler_params=_cparams(("parallel", "parallel")),
        name="dsa_index_mask",
    )(proj, proj, proj, tri)


def _dsa_attn_kernel(q_ref, k_ref, v_ref, mb_ref, bias_ref, o_ref, m_ref, acc_ref, *, tq, scale):
    qi = pl.program_id(2)
    hs = range(HPS)
    cols = lambda h: slice(h * LANE, (h + 1) * LANE)
    q = [q_ref[:, cols(h)] for h in hs]
    for h in hs:
        _softmax_init(m_ref.at[h], acc_ref.at[h])

    def step(j, nc, bias):
        sl = pl.ds(j * tq if isinstance(j, int) else pl.multiple_of(j * tq, tq), nc * tq)
        mb = mb_ref[j] if nc == 1 else jnp.concatenate([mb_ref[j + i] for i in range(nc)], axis=1)
        mb = mb.astype(F32)
        s = [_dot_nt(q[h], k_ref[sl, cols(h)]) + mb for h in hs]
        for h in hs:
            sh = s[h] if bias is None else s[h] + bias(h)
            _softmax_step(sh, v_ref[sl, cols(h)], m_ref.at[h], acc_ref.at[h], scale)

    _far_chunks(jnp.maximum(qi - 1, 0), lambda j, nc: step(j, nc, None), FAR_GROUP)

    @pl.when(qi == 0)
    def _():
        step(0, 1, lambda h: bias_ref[h, :, tq:2 * tq])

    @pl.when(qi >= 1)
    def _():
        step(qi - 1, 2, lambda h: bias_ref[h])

    for h in hs:
        o_ref[:, cols(h)] = _softmax_finish(acc_ref.at[h]).astype(o_ref.dtype)


def dsa_attention(proj, maskb, btiles, *, B, S, tq):
    M = B * S
    nq = S // tq
    H = DSA_HEADS
    scale = DSA_HEAD_DIM ** -0.5
    w = HPS * LANE
    return pl.pallas_call(
        functools.partial(_dsa_attn_kernel, tq=tq, scale=scale),
        grid=(B, H // HPS, nq),
        in_specs=[pl.BlockSpec((tq, w), lambda b, h, i: (b * nq + i, OFF_DSA_Q // w + h)),
                  pl.BlockSpec((S, w), lambda b, h, i: (b, OFF_DSA_K // w + h)),
                  pl.BlockSpec((S, w), lambda b, h, i: (b, OFF_DSA_V // w + h)),
                  pl.BlockSpec((None, nq, tq, tq), lambda b, h, i: (b, 0, i, 0)),
                  pl.BlockSpec((HPS, tq, 2 * tq), lambda b, h, i: (h, 0, 0))],
        out_specs=pl.BlockSpec((tq, w), lambda b, h, i: (b * nq + i, h)),
        out_shape=jax.ShapeDtypeStruct((M, H * DSA_HEAD_DIM), BF16),
        scratch_shapes=[pltpu.VMEM((HPS, tq, LANE), F32), pltpu.VMEM((HPS, tq, 2 * LANE), F32)],
        compiler_params=_cparams(("parallel", "parallel", "arbitrary")),
        name="dsa_attention",
    )(proj, proj, proj, maskb, btiles)


def _nsa_compress_kernel(x_ref, pos_ref, w1_ref, w2_ref, o_ref, *, ncp):
    dk = NSA_HEAD_DIM
    posw = _dot(pos_ref[...], w1_ref[...])[0:1, :]
    for g in range(NSA_GROUPS):
        lo = jnp.zeros((ncp, CMP_HIDDEN), F32)
        hi = jnp.zeros((ncp, CMP_HIDDEN), F32)
        for l in range(CMP_STRIDE):
            xs = x_ref[:, (l * NSA_GROUPS + g) * dk:(l * NSA_GROUPS + g + 1) * dk]
            lo = lo + _dot(xs, w1_ref[l * dk:(l + 1) * dk, :])
            hi = hi + _dot(xs, w1_ref[(CMP_STRIDE + l) * dk:(CMP_STRIDE + l + 1) * dk, :])
        hid = lo + pltpu.roll(hi, ncp - 1, 0) + posw
        o_ref[g] = _dot(jax.nn.gelu(hid).astype(BF16), w2_ref[...]).astype(o_ref.dtype)


def nsa_compress(xkv, posflat, w1, w2, *, B, S):
    ncp = S // CMP_STRIDE
    G, dk = NSA_GROUPS, NSA_HEAD_DIM
    width = CMP_STRIDE * G * dk
    return pl.pallas_call(
        functools.partial(_nsa_compress_kernel, ncp=ncp),
        grid=(2, B),
        in_specs=[pl.BlockSpec((None, ncp, width), lambda a, b: (a, b, 0)),
                  pl.BlockSpec((None, 8, CMP_LEN * dk), lambda a, b: (a, 0, 0)),
                  pl.BlockSpec((None, CMP_LEN * dk, CMP_HIDDEN), lambda a, b: (a, 0, 0)),
                  pl.BlockSpec((None, CMP_HIDDEN, dk), lambda a, b: (a, 0, 0))],
        out_specs=pl.BlockSpec((None, None, G, ncp, dk), lambda a, b: (a, b, 0, 0, 0)),
        out_shape=jax.ShapeDtypeStruct((2, B, G, ncp, dk), BF16),
        compiler_params=_cparams(("parallel", "parallel")),
        name="nsa_compress",
    )(xkv, posflat, w1, w2)


def _nsa_cmp_kernel(q_ref, kc_ref, vc_ref, ov_ref, oc_ref, sel_ref, *, tq, ncp, ns, n_sel, scale):
    t0 = pl.program_id(2) * tq
    dk = NSA_HEAD_DIM
    kc = kc_ref[...]
    vc = vc_ref[...]
    trow = lax.broadcasted_iota(jnp.int32, (tq, ncp), 0) + t0
    ccol = lax.broadcasted_iota(jnp.int32, (tq, ncp), 1)
    vis = (ccol * CMP_STRIDE + (CMP_LEN - 1)) <= trow
    psum = jnp.zeros((tq, ncp), F32)
    for h in range(NSA_HG):
        s = jnp.where(vis, _dot_nt(q_ref[:, h * dk:(h + 1) * dk], kc) * scale, NEG)
        m = jnp.max(s, axis=-1, keepdims=True)
        p = jnp.where(vis, jnp.exp(s - m), 0.0)
        d = jnp.sum(p, axis=-1, keepdims=True)
        p = p / jnp.where(d > 0, d, 1.0)
        oc_ref[:, h * dk:(h + 1) * dk] = _dot(p.astype(BF16), vc).astype(oc_ref.dtype)
        psum = psum + p
    p_hi = psum.astype(BF16)
    p_lo = (psum - p_hi.astype(F32)).astype(BF16)
    ov = ov_ref[...]
    imp = _dot_nt(ov, p_hi) + _dot_nt(ov, p_lo)
    imp = imp[:ns]
    jrow = lax.broadcasted_iota(jnp.int32, (ns, tq), 0)
    tcol = lax.broadcasted_iota(jnp.int32, (ns, tq), 1) + t0
    blk_t = tcol // SEL_BLOCK
    forced = (jrow == 0) | (jrow == blk_t) | (jrow == blk_t - 1)
    val = jnp.where(forced, -NEG, imp)
    val = jnp.where(jrow * SEL_BLOCK > tcol, NEG, val)
    rank = jnp.zeros((ns, tq), F32)
    for j in range(ns):
        other = val[j:j + 1, :]
        ahead = (other > val) | ((other == val) & (jrow > j))
        rank = rank + jnp.where(ahead, 1.0, 0.0)
    selneg = jnp.where(rank < float(n_sel), 0.0, NEG)
    if ns < LANE:
        selneg = jnp.concatenate([selneg, jnp.zeros((LANE - ns, tq), F32)], axis=0)
    sel_ref[...] = selneg.T.astype(sel_ref.dtype)


def nsa_cmp_attention(proj, kvc, ovt, *, B, S, tq):
    M = B * S
    nq = S // tq
    G, dk = NSA_GROUPS, NSA_HEAD_DIM
    ncp = S // CMP_STRIDE
    ns = S // SEL_BLOCK
    assert ns <= LANE
    n_sel = min(SEL_COUNT, ns)
    gw = NSA_HG * dk
    return pl.pallas_call(
        functools.partial(_nsa_cmp_kernel, tq=tq, ncp=ncp, ns=ns, n_sel=n_sel, scale=dk ** -0.5),
        grid=(B, G, nq),
        in_specs=[pl.BlockSpec((tq, gw), lambda b, g, i: (b * nq + i, OFF_NSA_Q // gw + g)),
                  pl.BlockSpec((None, None, None, ncp, dk), lambda b, g, i: (0, b, g, 0, 0)),
                  pl.BlockSpec((None, None, None, ncp, dk), lambda b, g, i: (1, b, g, 0, 0)),
                  pl.BlockSpec((LANE, ncp), lambda b, g, i: (0, 0))],
        out_specs=[pl.BlockSpec((tq, gw), lambda b, g, i: (b * nq + i, g)),
                   pl.BlockSpec((None, None, tq, LANE), lambda b, g, i: (b, g, i, 0))],
        out_shape=[jax.ShapeDtypeStruct((M, G * gw), BF16),
                   jax.ShapeDtypeStruct((B, G, S, LANE), BF16)],
        compiler_params=_cparams(("parallel", "parallel", "parallel")),
        name="nsa_cmp_attention",
    )(proj, kvc, kvc, ovt)


def _nsa_main_kernel(q_ref, sel_ref, ks_ref, vs_ref, kw_ref, vw_ref, ind_ref, bias_ref, gate_ref, oc_ref,
                     o_ref, m_ref, acc_ref, *, tq, nback, scale):
    g = pl.program_id(1)
    qi = pl.program_id(2)
    dk = NSA_HEAD_DIM
    HG = NSA_HG
    ns = NSA_STREAMS
    hp = HG // ns
    st = range(ns)
    stack = lambda f: [jnp.concatenate([f(a * hp + i) for i in range(hp)], axis=0) for a in st]
    q = stack(lambda h: q_ref[:, h * dk:(h + 1) * dk])
    sel = jnp.concatenate([sel_ref[...]] * hp, axis=0)
    qa = [jnp.concatenate([q[a], sel], axis=1) for a in st]
    gate = jax.nn.sigmoid(gate_ref[...].astype(F32))

    for a in range(2 * ns):
        _softmax_init(m_ref.at[a], acc_ref.at[a])

    def steps(s, v, fix, base=0):
        for a in st:
            _softmax_step(fix(a, s[a]), v, m_ref.at[base + a], acc_ref.at[base + a], scale)

    wspan = nback + 1

    def rows_from(j, width):
        return pl.ds(j * tq if isinstance(j, int) else pl.multiple_of(j * tq, tq), width)

    def sel_scores(j, width=tq):
        sl = rows_from(j, width)
        ka = jnp.concatenate([ks_ref[sl, :], ind_ref[sl, :]], axis=1)
        return [_dot_nt(qa[a], ka) for a in st], vs_ref[sl, :]

    def sel_far(j, nc):
        s, v = sel_scores(j, nc * tq)
        steps(s, v, lambda a, x: x)

    _far_chunks(jnp.maximum(qi - 1, 0), sel_far, FAR_GROUP_NSA)

    def win_scores(j, width):
        sl = rows_from(j, width)
        kw = kw_ref[sl, :]
        return [_dot_nt(q[a], kw) for a in st], vw_ref[sl, :]

    def near(first_s, nc_s, first_w, nc_w):
        tile = lambda nc: (lambda a, x: x + bias_ref[a, :, (wspan - nc) * tq:wspan * tq])
        s_s, v_s = sel_scores(first_s, nc_s * tq)
        s_w, v_w = win_scores(first_w, nc_w * tq)
        steps(s_s, v_s, tile(nc_s))
        steps(s_w, v_w, tile(nc_w), base=ns)

    for k in range(wspan - 1):
        @pl.when(qi == k)
        def _(k=k):
            near(max(k - 1, 0), min(k + 1, 2), 0, k + 1)

    @pl.when(qi >= wspan - 1)
    def _():
        near(qi - 1, 2, qi - (wspan - 1), wspan)

    o_s = [_softmax_finish(acc_ref.at[a]) for a in st]
    o_w = [_softmax_finish(acc_ref.at[ns + a]) for a in st]

    lane = lax.broadcasted_iota(jnp.int32, gate.shape, 1)
    for h in range(HG):
        c = g * HG + h
        a, sl = h // hp, slice((h % hp) * tq, (h % hp + 1) * tq)
        gc = [jnp.sum(jnp.where(lane == br * NSA_HEADS + c, gate, 0.0), axis=-1, keepdims=True)
              for br in range(3)]
        o = gc[0] * oc_ref[:, h * dk:(h + 1) * dk].astype(F32) + gc[1] * o_s[a][sl] + gc[2] * o_w[a][sl]
        o_ref[:, h * dk:(h + 1) * dk] = o.astype(o_ref.dtype)


def nsa_main(proj, selneg, ind, btiles, o_c, *, B, S, tq):
    M = B * S
    nq = S // tq
    G, dk, HG = NSA_GROUPS, NSA_HEAD_DIM, NSA_HG
    gw = HG * dk
    assert WINDOW % tq == 0 and WINDOW // tq >= 2
    nback = WINDOW // tq
    ns, hp = NSA_STREAMS, HG // NSA_STREAMS
    kv = lambda n: pl.BlockSpec((S, dk), lambda b, g, i, n=n: (b, (OFF_NSA_KV + n * G * dk) // dk + g))
    return pl.pallas_call(
        functools.partial(_nsa_main_kernel, tq=tq, nback=nback, scale=dk ** -0.5),
        grid=(B, G, nq),
        in_specs=[pl.BlockSpec((tq, gw), lambda b, g, i: (b * nq + i, OFF_NSA_Q // gw + g)),
                  pl.BlockSpec((None, None, tq, LANE), lambda b, g, i: (b, g, i, 0)),
                  kv(2), kv(3), kv(4), kv(5),
                  pl.BlockSpec((S, LANE), lambda b, g, i: (0, 0)),
                  pl.BlockSpec((ns, hp * tq, (nback + 1) * tq), lambda b, g, i: (g, 0, 0)),
                  pl.BlockSpec((tq, LANE), lambda b, g, i: (b * nq + i, OFF_NSA_GATE // LANE)),
                  pl.BlockSpec((tq, gw), lambda b, g, i: (b * nq + i, g))],
        out_specs=pl.BlockSpec((tq, gw), lambda b, g, i: (b * nq + i, g)),
        out_shape=jax.ShapeDtypeStruct((M, G * gw), BF16),
        scratch_shapes=[pltpu.VMEM((2 * ns, hp * tq, LANE), F32),
                        pltpu.VMEM((2 * ns, hp * tq, 2 * LANE), F32)],
        compiler_params=_cparams(("parallel", "parallel", "arbitrary")),
        name="nsa_main",
    )(proj, selneg, proj, proj, proj, proj, ind, btiles, proj, o_c)


def _merge_kernel(x_ref, oa_ref, ob_ref, oc_ref, g0_ref, g1_ref, g2_ref, wb_ref, wo_ref, gain_ref, o_ref):
    merged = None
    for o_r, g_r, br in ((oa_ref, g0_ref, 0), (ob_ref, g1_ref, 1), (oc_ref, g2_ref, 2)):
        t = jax.nn.sigmoid(g_r[...].astype(F32)) * _dot(o_r[...], wb_ref[br])
        merged = t if merged is None else merged + t
    y = _dot(merged.astype(BF16), wo_ref[...])
    o_ref[...] = x_ref[...] + _rms(y, gain_ref[...])


def merge_branches(x, o_a, o_b, o_c, proj, wb, wo, gain, *, tm):
    M, D = x.shape
    row = pl.BlockSpec((tm, D), lambda i: (i, 0))
    gate = lambda br: pl.BlockSpec((tm, D), lambda i, br=br: (i, OFF_BGATE // D + br))
    return pl.pallas_call(
        _merge_kernel,
        grid=(M // tm,),
        in_specs=[row, row, row, row, gate(0), gate(1), gate(2),
                  pl.BlockSpec(wb.shape, lambda i: (0, 0, 0)),
                  pl.BlockSpec(wo.shape, lambda i: (0, 0)),
                  pl.BlockSpec((1, D), lambda i: (0, 0))],
        out_specs=row,
        out_shape=jax.ShapeDtypeStruct((M, D), F32),
        compiler_params=_cparams(("parallel",)),
        name="merge_branches",
    )(x, o_a, o_b, o_c, proj, proj, proj, wb, wo, gain)


def _cross_kernel(x_ref, kv_ref, wq_ref, wo_ref, gpre_ref, gpost_ref, o_ref, *, scale):
    x = x_ref[...]
    h = _rms(x, gpre_ref[...]).astype(BF16)
    q = _dot(h, wq_ref[...]).astype(BF16)
    dh = CROSS_HEAD_DIM
    outs = []
    for hd in range(CROSS_HEADS):
        k = kv_ref[:, hd * dh:(hd + 1) * dh]
        v = kv_ref[:, D_MODEL + hd * dh:D_MODEL + (hd + 1) * dh]
        s = _dot_nt(q[:, hd * dh:(hd + 1) * dh], k) * scale
        p = jnp.exp(s - jnp.max(s, axis=-1, keepdims=True))
        p = p / jnp.sum(p, axis=-1, keepdims=True)
        outs.append(_dot(p.astype(BF16), v).astype(BF16))
    y = _dot(jnp.concatenate(outs, axis=1), wo_ref[...])
    o_ref[...] = x + _rms(y, gpost_ref[...])


def cross_attention(x, kv, wq, wo, gpre, gpost, *, S, tm):
    M, D = x.shape
    mlen = kv.shape[0] // (M // S)
    per_b = S // tm
    return pl.pallas_call(
        functools.partial(_cross_kernel, scale=CROSS_HEAD_DIM ** -0.5),
        grid=(M // tm,),
        in_specs=[pl.BlockSpec((tm, D), lambda i: (i, 0)),
                  pl.BlockSpec((mlen, 2 * D), lambda i: (i // per_b, 0)),
                  pl.BlockSpec(wq.shape, lambda i: (0, 0)),
                  pl.BlockSpec(wo.shape, lambda i: (0, 0)),
                  pl.BlockSpec((1, D), lambda i: (0, 0)),
                  pl.BlockSpec((1, D), lambda i: (0, 0))],
        out_specs=pl.BlockSpec((tm, D), lambda i: (i, 0)),
        out_shape=jax.ShapeDtypeStruct((M, D), F32),
        compiler_params=_cparams(("parallel",)),
        name="cross_attention",
    )(x, kv, wq, wo, gpre, gpost)


def _mlp_kernel(x_ref, w1_ref, w2_ref, gpre_ref, gpost_ref, o_ref, h_ref, acc_ref):
    j = pl.program_id(1)

    @pl.when(j == 0)
    def _():
        h_ref[...] = _rms(x_ref[...], gpre_ref[...]).astype(BF16)
        acc_ref[...] = jnp.zeros(acc_ref.shape, F32)

    a = jnp.maximum(_dot(h_ref[...], w1_ref[...]), 0.0)
    acc_ref[...] += _dot((a * a).astype(BF16), w2_ref[...])

    @pl.when(j == pl.num_programs(1) - 1)
    def _():
        o_ref[...] = x_ref[...] + _rms(acc_ref[...], gpost_ref[...])


def mlp(x, w1, w2, gpre, gpost, *, tm, tf):
    M, D = x.shape
    FF = w1.shape[1]
    return pl.pallas_call(
        _mlp_kernel,
        grid=(M // tm, FF // tf),
        in_specs=[pl.BlockSpec((tm, D), lambda i, j: (i, 0)),
                  pl.BlockSpec((D, tf), lambda i, j: (0, j)),
                  pl.BlockSpec((tf, D), lambda i, j: (j, 0)),
                  pl.BlockSpec((1, D), lambda i, j: (0, 0)),
                  pl.BlockSpec((1, D), lambda i, j: (0, 0))],
        out_specs=pl.BlockSpec((tm, D), lambda i, j: (i, 0)),
        out_shape=jax.ShapeDtypeStruct((M, D), F32),
        scratch_shapes=[pltpu.VMEM((tm, D), BF16), pltpu.VMEM((tm, D), F32)],
        compiler_params=_cparams(("parallel", "arbitrary")),
        name="mlp",
    )(x, w1, w2, gpre, gpost)


def _rot_half_cols(w):
    half = w.shape[-1] // 2
    return jnp.concatenate([-w[..., half:], w[..., :half]], axis=-1)


def _pack_w_in(w):
    widths = (MLA_Q_LORA, MLA_KV_LORA, MLA_ROPE, 1024, 1024, 1024, IDX_HEADS * IDX_DIM, IDX_DIM, IDX_HEADS,
              1024, 256, 256, 256, 256, 256, 256, 3 * NSA_HEADS, N_BRANCH * D_MODEL)
    offs = np.concatenate([[0], np.cumsum(widths)])
    seg = [w[:, offs[i]:offs[i + 1]] for i in range(len(widths))]
    (q_lat, kv_lat, k_rope, dsa_q, dsa_k, dsa_v, idx_q, idx_k, idx_w,
     nsa_q, kc, vc, ks, vs, kw, vw, nsa_gate, bgate) = seg
    z = lambda n: jnp.zeros((w.shape[0], n), w.dtype)
    cols = [dsa_q, dsa_k, dsa_v, nsa_q, bgate, idx_q, kc, vc, ks, vs, kw, vw,
            q_lat, kv_lat, k_rope, z(LANE - MLA_ROPE), _rot_half_cols(k_rope), z(LANE - MLA_ROPE), z(LANE),
            idx_k, idx_k, idx_w, z(LANE - IDX_HEADS), nsa_gate, z(LANE - 3 * NSA_HEADS)]
    out = jnp.concatenate(cols, axis=1)
    out = jnp.concatenate([out, z(N_PACK - out.shape[1])], axis=1)
    return out.astype(BF16)


def _pack_mla_weights(w_uq, w_ukv):
    H = MLA_HEADS
    uq = w_uq.reshape(MLA_Q_LORA, H, MLA_NOPE + MLA_ROPE)
    nope = uq[:, :, :MLA_NOPE].reshape(MLA_Q_LORA, H * MLA_NOPE)
    rope = uq[:, :, MLA_NOPE:]
    pad = jnp.zeros((MLA_Q_LORA, H, LANE - MLA_ROPE), w_uq.dtype)
    wqr = jnp.concatenate([rope, pad], axis=-1).reshape(MLA_Q_LORA, H * LANE)
    wqrr = jnp.concatenate([_rot_half_cols(rope), pad], axis=-1).reshape(MLA_Q_LORA, H * LANE)
    ukv = w_ukv.reshape(MLA_KV_LORA, H, MLA_NOPE + MLA_V)
    wkv = jnp.concatenate([ukv[:, :, :MLA_NOPE].reshape(MLA_KV_LORA, H * MLA_NOPE),
                           ukv[:, :, MLA_NOPE:].reshape(MLA_KV_LORA, H * MLA_V)], axis=1)
    return nope.astype(BF16), wqr.astype(BF16), wqrr.astype(BF16), wkv.astype(BF16)


def _constants(S):
    half = MLA_ROPE // 2
    inv_freq = ROPE_BASE ** (-np.arange(0, MLA_ROPE, 2, dtype=np.float32) / MLA_ROPE)
    freq = np.zeros((1, LANE), np.float32)
    freq[0, :half] = inv_freq
    freq[0, half:2 * half] = inv_freq
    ncp = S // CMP_STRIDE
    ns = S // SEL_BLOCK
    c_start = np.arange(ncp) * CMP_STRIDE
    j_start = np.arange(ns) * SEL_BLOCK
    overlap = ((c_start[None, :] < j_start[:, None] + SEL_BLOCK) &
               (c_start[None, :] + CMP_LEN > j_start[:, None])).astype(np.float32)
    overlap[:, ncp - 1] = 0.0
    ovt = np.zeros((LANE, ncp), np.float32)
    ovt[:ns] = overlap
    ind = np.zeros((S, LANE), np.float32)
    ind[np.arange(S), np.arange(S) // SEL_BLOCK] = 1.0
    return jnp.asarray(freq), jnp.asarray(ovt, BF16), jnp.asarray(ind, BF16)


def kernel(x, mem, positions, rel_bias, norm_gains, w_in, mla_q_norm, mla_kv_norm, mla_w_uq, mla_w_ukv,
           nsa_cmp_pos, nsa_cmp_w1, nsa_cmp_w2, w_branch, w_out, cross_wq, cross_wkv, cross_wo, mlp_w1, mlp_w2):
    B, S, D = x.shape
    M = B * S
    depth = w_in.shape[0]
    tq = TQ if S % TQ == 0 else TQ_NSA
    tqn = TQ_NSA
    assert S % tq == 0 and S % tqn == 0 and D == D_MODEL
    topk = min(DSA_TOPK_MAX, S // 4)
    freq, ovt, ind = _constants(S)
    btiles = bias_tiles(rel_bias, tq, 0, DSA_HEADS, nspan=2, masked=False, stack=1)
    btiles_n = bias_tiles(rel_bias, tqn, DSA_HEADS, NSA_HEADS, nspan=WINDOW // tqn + 1, masked=True,
                          stack=NSA_HG // NSA_STREAMS)
    pos = positions.reshape(M, 1)
    xf = x.reshape(M, D)
    memf = mem.reshape(B * mem.shape[1], D)
    tm_big = 1024 if M % 1024 == 0 else tq
    tm = 512 if M % 512 == 0 else tq
    ncp = S // CMP_STRIDE
    row = lambda v: v.reshape(1, -1)

    for l in range(depth):
        g = norm_gains[l]
        proj = norm_matmul(xf, row(g[0]), _pack_w_in(w_in[l]), tm=tm_big, tn=TN_IN)

        wqn, wqr, wqrr, wkv = _pack_mla_weights(mla_w_uq[l], mla_w_ukv[l])
        qn, qr, kn, vv, kr = mla_prep(proj, pos, freq, row(mla_q_norm[l]), row(mla_kv_norm[l]),
                                      wqn, wqr, wqrr, wkv, tm=tm)
        o_a = mla_attention(qn, qr, kn, kr, vv, B=B, S=S, tq=tq)

        maskb = dsa_index_mask(proj, B=B, S=S, tqi=TQ_IDX, tk=tq, topk=topk)
        o_b = dsa_attention(proj, maskb, btiles, B=B, S=S, tq=tq)

        kvc_in = jnp.stack([
            proj[:, OFF_NSA_KV + a * 256:OFF_NSA_KV + (a + 1) * 256].reshape(B * ncp, CMP_STRIDE * 256)
            for a in range(2)])
        posflat = jnp.broadcast_to(nsa_cmp_pos[l].reshape(2, 1, CMP_LEN * NSA_HEAD_DIM),
                                   (2, 8, CMP_LEN * NSA_HEAD_DIM)).astype(BF16)
        kvc = nsa_compress(kvc_in, posflat, nsa_cmp_w1[l].astype(BF16), nsa_cmp_w2[l].astype(BF16), B=B, S=S)
        o_cmp, selneg = nsa_cmp_attention(proj, kvc, ovt, B=B, S=S, tq=tqn)
        o_c = nsa_main(proj, selneg, ind, btiles_n, o_cmp, B=B, S=S, tq=tqn)

        xf = merge_branches(xf, o_a, o_b, o_c, proj, w_branch[l].astype(BF16), w_out[l].astype(BF16),
                            row(g[1]), tm=tm)

        mkv = norm_matmul(memf, row(g[3]), cross_wkv[l].astype(BF16), tm=memf.shape[0] // B, tn=1024)
        xf = cross_attention(xf, mkv, cross_wq[l].astype(BF16), cross_wo[l].astype(BF16),
                             row(g[2]), row(g[4]), S=S, tm=tm)

        xf = mlp(xf, mlp_w1[l].astype(BF16), mlp_w2[l].astype(BF16), row(g[5]), row(g[6]), tm=tm_big, tf=1024)

    return xf.reshape(B, S, D)
```

```python
import functools
import math

import numpy as np
import jax
import jax.numpy as jnp
from jax import lax
from jax.experimental import pallas as pl
from jax.experimental.pallas import tpu as pltpu

F32 = jnp.float32
BF16 = jnp.bfloat16

D_MODEL = 1024
EPS = 1e-6
MLA_HEADS = 8
MLA_Q_LORA = 384
MLA_KV_LORA = 256
MLA_NOPE = 128
MLA_ROPE = 64
MLA_V = 128
ROPE_BASE = 10000.0
DSA_HEADS = 8
DSA_HEAD_DIM = 128
IDX_HEADS = 8
IDX_DIM = 64
DSA_TOPK_MAX = 256
NSA_HEADS = 8
NSA_GROUPS = 2
NSA_HG = NSA_HEADS // NSA_GROUPS
NSA_HEAD_DIM = 128
CMP_STRIDE = 16
CMP_LEN = 2 * CMP_STRIDE
CMP_HIDDEN = 256
SEL_BLOCK = 64
SEL_COUNT = 16
WINDOW = 512
REL_BUCKETS = 32
REL_MAX_DIST = 128
REL_HEADS = DSA_HEADS + NSA_HEADS
CROSS_HEADS = 4
CROSS_HEAD_DIM = D_MODEL // CROSS_HEADS
D_FF = 4 * D_MODEL
N_BRANCH = 3

LANE = 128
NEG = -1e30
VMEM_LIMIT = 48 * 1024 * 1024

TQ = 512
TQ_NSA = 256
HPS = 2
FAR_GROUP = 4
FAR_GROUP_NSA = 8
NSA_STREAMS = 2
BISECT_ITERS = 32
IDX_TIE_CHECK = 16
TQ_IDX = 512
IDX_CHUNK = 256

OFF_DSA_Q = 0
OFF_DSA_K = 1024
OFF_DSA_V = 2048
OFF_NSA_Q = 3072
OFF_BGATE = 4096
OFF_IDX_Q = 7168
OFF_NSA_KV = 7680
OFF_MLA = 9216
OFF_IDX_K = 10240
OFF_IDX_W = 10368
OFF_NSA_GATE = 10496
N_PACK = 10752
TN_IN = 2688


def _cparams(sem):
    return pltpu.CompilerParams(dimension_semantics=sem, vmem_limit_bytes=VMEM_LIMIT)


def _rms(x, g):
    return x * lax.rsqrt(jnp.mean(x * x, axis=-1, keepdims=True) + EPS) * g


def _dot(a, b):
    return jnp.dot(a, b, preferred_element_type=F32)


def _dot_nt(a, b):
    return lax.dot_general(a, b, (((1,), (1,)), ((), ())), preferred_element_type=F32)


def _norm_mm_kernel(x_ref, g_ref, w_ref, o_ref, xn_ref):
    @pl.when(pl.program_id(1) == 0)
    def _():
        xn_ref[...] = _rms(x_ref[...].astype(F32), g_ref[...]).astype(BF16)

    o_ref[...] = _dot(xn_ref[...], w_ref[...]).astype(o_ref.dtype)


def norm_matmul(x, g, w, *, tm, tn):
    M, K = x.shape
    N = w.shape[1]
    return pl.pallas_call(
        _norm_mm_kernel,
        grid=(M // tm, N // tn),
        in_specs=[pl.BlockSpec((tm, K), lambda i, j: (i, 0)),
                  pl.BlockSpec((1, K), lambda i, j: (0, 0)),
                  pl.BlockSpec((K, tn), lambda i, j: (0, j))],
        out_specs=pl.BlockSpec((tm, tn), lambda i, j: (i, j)),
        out_shape=jax.ShapeDtypeStruct((M, N), BF16),
        scratch_shapes=[pltpu.VMEM((tm, K), BF16)],
        compiler_params=_cparams(("parallel", "arbitrary")),
        name="norm_matmul",
    )(x, g, w)


def _bias_tiles_kernel(tab_ref, bkt_ref, o_ref, *, inv_scale, h0, tq, nspan, masked):
    h = pl.program_id(0) + h0
    bkt = bkt_ref[...]
    far = tab_ref[REL_BUCKETS - 1, h]
    out = jnp.zeros(bkt.shape, F32)
    for b in range(REL_BUCKETS - 1):
        out = jnp.where(bkt == b, (tab_ref[b, h] - far) * inv_scale, out)
    diag, prev = out[0], out[1]
    r, c = _tri(tq, tq)
    if masked:
        diag = jnp.where(r >= c, diag, NEG)
    o_ref[:, (nspan - 1) * tq:nspan * tq] = diag
    o_ref[:, (nspan - 2) * tq:(nspan - 1) * tq] = prev
    for k in range(nspan - 2):
        edge = jnp.where(r < c, 0.0, NEG) if (masked and k == 0) else jnp.zeros((tq, tq), F32)
        o_ref[:, k * tq:(k + 1) * tq] = edge


def _bucket_tiles(tq):
    i = np.arange(tq)[:, None]
    j = np.arange(tq)[None, :]
    d = np.stack([i - j, tq + i - j]).astype(np.int32)
    n = jnp.maximum(jnp.asarray(d), 0)
    exact = REL_BUCKETS // 2
    nf = jnp.maximum(n, 1).astype(F32)
    log_b = exact + (jnp.log(nf / exact) / math.log(REL_MAX_DIST / exact) * (REL_BUCKETS - exact)).astype(jnp.int32)
    return jnp.where(n < exact, n, jnp.minimum(log_b, REL_BUCKETS - 1)).astype(jnp.int32)


def bias_tiles(rel_bias, tq, h0, nh, *, nspan, masked, stack):
    assert tq >= REL_MAX_DIST
    bkt = _bucket_tiles(tq)
    assert DSA_HEAD_DIM == NSA_HEAD_DIM
    return pl.pallas_call(
        functools.partial(_bias_tiles_kernel, inv_scale=DSA_HEAD_DIM ** 0.5, h0=h0, tq=tq, nspan=nspan,
                          masked=masked),
        grid=(nh,),
        in_specs=[pl.BlockSpec(memory_space=pltpu.SMEM),
                  pl.BlockSpec((2, tq, tq), lambda h: (0, 0, 0))],
        out_specs=pl.BlockSpec((None, tq, nspan * tq), lambda h: (h // stack, h % stack, 0)),
        out_shape=jax.ShapeDtypeStruct((nh // stack, stack * tq, nspan * tq), F32),
        compiler_params=_cparams(("arbitrary",)),
        name="bias_tiles",
    )(rel_bias, bkt)


def _mla_prep_kernel(lat_ref, pos_ref, freq_ref, qg_ref, kvg_ref, wqn_ref, wqr_ref, wqrr_ref, wkv_ref,
                     qn_ref, qr_ref, kn_ref, v_ref, kr_ref):
    lat = lat_ref[...].astype(F32)
    qlat = _rms(lat[:, :MLA_Q_LORA], qg_ref[...]).astype(BF16)
    kvlat = _rms(lat[:, MLA_Q_LORA:MLA_Q_LORA + MLA_KV_LORA], kvg_ref[...]).astype(BF16)
    o = MLA_Q_LORA + MLA_KV_LORA
    kr = lat[:, o:o + LANE]
    krr = lat[:, o + LANE:o + 2 * LANE]
    ang = pos_ref[...].astype(F32) * freq_ref[...]
    cos, sin = jnp.cos(ang), jnp.sin(ang)
    cos8 = jnp.concatenate([cos] * MLA_HEADS, axis=1)
    sin8 = jnp.concatenate([sin] * MLA_HEADS, axis=1)
    qn_ref[...] = _dot(qlat, wqn_ref[...]).astype(BF16)
    qr_ref[...] = (_dot(qlat, wqr_ref[...]) * cos8 + _dot(qlat, wqrr_ref[...]) * sin8).astype(BF16)
    kv = _dot(kvlat, wkv_ref[...])
    nk = MLA_HEADS * MLA_NOPE
    kn_ref[...] = kv[:, :nk].astype(BF16)
    v_ref[...] = kv[:, nk:].astype(BF16)
    kr_ref[...] = (kr * cos + krr * sin).astype(BF16)


def mla_prep(proj, pos, freq, qg, kvg, wqn, wqr, wqrr, wkv, *, tm):
    M = proj.shape[0]
    HD = MLA_HEADS * LANE
    full = lambda a: pl.BlockSpec(a.shape, lambda i: (0, 0))
    outs = [jax.ShapeDtypeStruct((M, HD), BF16)] * 4 + [jax.ShapeDtypeStruct((M, LANE), BF16)]
    return pl.pallas_call(
        _mla_prep_kernel,
        grid=(M // tm,),
        in_specs=[pl.BlockSpec((tm, 1024), lambda i: (i, OFF_MLA // 1024)),
                  pl.BlockSpec((tm, 1), lambda i: (i, 0)),
                  full(freq), full(qg), full(kvg), full(wqn), full(wqr), full(wqrr), full(wkv)],
        out_specs=[pl.BlockSpec((tm, HD), lambda i: (i, 0))] * 4 + [pl.BlockSpec((tm, LANE), lambda i: (i, 0))],
        out_shape=outs,
        compiler_params=_cparams(("parallel",)),
        name="mla_prep",
    )(proj, pos, freq, qg, kvg, wqn, wqr, wqrr, wkv)


def _softmax_init(m_ref, acc_ref):
    m_ref[...] = jnp.full(m_ref.shape, NEG, F32)
    acc_ref[...] = jnp.zeros(acc_ref.shape, F32)


def _with_ones(v):
    return jnp.concatenate([v, jnp.ones(v.shape, v.dtype)], axis=1)


def _softmax_step(s, v, m_ref, acc_ref, scale):
    c = scale * math.log2(math.e)
    m_prev = m_ref[...]
    m_new = jnp.maximum(m_prev, jnp.max(s, axis=-1, keepdims=True))
    alpha = jnp.exp2((m_prev - m_new) * c)
    p = jnp.exp2((s - jnp.tile(m_new, (1, s.shape[1] // LANE))) * c)
    acc_ref[...] = jnp.tile(alpha, (1, 2)) * acc_ref[...] + _dot(p.astype(BF16), _with_ones(v))
    m_ref[...] = m_new


def _softmax_finish(acc_ref):
    acc = acc_ref[...]
    return acc[:, :LANE] / acc[:, LANE:]


def _far_chunks(n, step, largest):
    def body(p, c):
        step(largest * p, largest)
        return c

    lax.fori_loop(0, n // largest, body, 0)
    w = largest // 2
    while w >= 1:
        @pl.when(n % (2 * w) >= w)
        def _(w=w):
            step(n // (2 * w) * (2 * w), w)
        w //= 2


def _tri(tq, tk):
    row = lax.broadcasted_iota(jnp.int32, (tq, tk), 0)
    col = lax.broadcasted_iota(jnp.int32, (tq, tk), 1)
    return row, col


def _mla_attn_kernel(qn_ref, qr_ref, kn_ref, kr_ref, v_ref, o_ref, m_ref, acc_ref, *, tq, scale):
    qi = pl.program_id(2)
    hs = range(HPS)
    q = [jnp.concatenate([qn_ref[:, h * LANE:(h + 1) * LANE], qr_ref[:, h * LANE:(h + 1) * LANE]], axis=-1)
         for h in hs]
    for h in hs:
        _softmax_init(m_ref.at[h], acc_ref.at[h])

    def step(j, nc, causal_last):
        sl = pl.ds(j * tq if isinstance(j, int) else pl.multiple_of(j * tq, tq), nc * tq)
        kr = kr_ref[sl, :]
        if causal_last:
            row, col = _tri(tq, nc * tq)
            ok = col <= row + (nc - 1) * tq
        s = [_dot_nt(q[h], jnp.concatenate([kn_ref[sl, h * LANE:(h + 1) * LANE], kr], axis=-1)) for h in hs]
        for h in hs:
            sh = jnp.where(ok, s[h], NEG) if causal_last else s[h]
            _softmax_step(sh, v_ref[sl, h * LANE:(h + 1) * LANE], m_ref.at[h], acc_ref.at[h], scale)

    _far_chunks(jnp.maximum(qi - 1, 0), lambda j, nc: step(j, nc, False), FAR_GROUP)

    @pl.when(qi == 0)
    def _():
        step(0, 1, True)

    @pl.when(qi >= 1)
    def _():
        step(qi - 1, 2, True)

    for h in hs:
        o_ref[:, h * LANE:(h + 1) * LANE] = _softmax_finish(acc_ref.at[h]).astype(o_ref.dtype)


def mla_attention(qn, qr, kn, kr, v, *, B, S, tq):
    M = B * S
    nq = S // tq
    H = MLA_HEADS
    scale = (MLA_NOPE + MLA_ROPE) ** -0.5
    w = HPS * LANE
    qspec = pl.BlockSpec((tq, w), lambda b, h, i: (b * nq + i, h))
    kspec = pl.BlockSpec((S, w), lambda b, h, i: (b, h))
    return pl.pallas_call(
        functools.partial(_mla_attn_kernel, tq=tq, scale=scale),
        grid=(B, H // HPS, nq),
        in_specs=[qspec, qspec, kspec, pl.BlockSpec((S, LANE), lambda b, h, i: (b, 0)), kspec],
        out_specs=qspec,
        out_shape=jax.ShapeDtypeStruct((M, H * MLA_V), BF16),
        scratch_shapes=[pltpu.VMEM((HPS, tq, LANE), F32), pltpu.VMEM((HPS, tq, 2 * LANE), F32)],
        compiler_params=_cparams(("parallel", "parallel", "arbitrary")),
        name="mla_attention",
    )(qn, qr, kn, kr, v)


def _dsa_index_kernel(iq_ref, ik_ref, iw_ref, tri_ref, mb_ref, sc_ref, *, tq, tk, S, topk, iters):
    t0 = pl.program_id(1) * tq
    nk = S // tk
    nch = (t0 + tq - 1) // tk + 1
    hw = IDX_CHUNK
    iq = iq_ref[...]
    lane = lax.broadcasted_iota(jnp.int32, (tq, LANE), 1)
    qs = []
    for h in range(IDX_HEADS):
        blk = iq[:, (h // 2) * LANE:(h // 2 + 1) * LANE]
        keep = (lane >= IDX_DIM) if h % 2 else (lane < IDX_DIM)
        qs.append(jnp.where(keep, blk, jnp.zeros_like(blk)))
    q8 = jnp.concatenate(qs, axis=0)
    iwt = iw_ref[...].astype(F32).T
    wrow = [iwt[h:h + 1, :] for h in range(IDX_HEADS)]
    qpos = lax.broadcasted_iota(jnp.int32, (hw, tq), 1) + t0
    krow = lax.broadcasted_iota(jnp.int32, (hw, tq), 0)

    def fold(x, op):
        out = x[0:8]
        for r in range(1, x.shape[0] // 8):
            out = op(out, x[r * 8:(r + 1) * 8])
        return out

    def score_chunk(c, carry):
        mn, mx = carry
        for half in range(tk // hw):
            k0 = pl.multiple_of(c * tk + half * hw, hw)
            res = _dot_nt(ik_ref[pl.ds(k0, hw), :], q8)
            sc = wrow[0] * jnp.maximum(res[:, 0:tq], 0.0)
            for h in range(1, IDX_HEADS):
                sc = sc + wrow[h] * jnp.maximum(res[:, h * tq:(h + 1) * tq], 0.0)
            causal = (krow + k0) <= qpos
            mn = jnp.minimum(mn, fold(jnp.where(causal, sc, -NEG), jnp.minimum))
            sc = jnp.where(causal, sc, NEG)
            mx = jnp.maximum(mx, fold(sc, jnp.maximum))
            sc_ref[c, half * hw:(half + 1) * hw, :] = sc
        return mn, mx

    mn, mx = lax.fori_loop(0, nch, score_chunk, (jnp.full((8, tq), -NEG, F32), jnp.full((8, tq), NEG, F32)))
    n_causal = (lax.broadcasted_iota(jnp.int32, (1, tq), 1) + (t0 + 1)).astype(F32)
    kf = jnp.minimum(float(topk), n_causal)

    def count(pred):
        def body(c, acc):
            return acc + fold(jnp.where(pred(sc_ref[c]), 1.0, 0.0), jnp.add)

        return jnp.sum(lax.fori_loop(0, nch, body, jnp.zeros((8, tq), F32)), axis=0, keepdims=True)

    lo0 = jnp.min(mn, axis=0, keepdims=True)
    mx = jnp.max(mx, axis=0, keepdims=True)
    hi0 = mx + (jnp.abs(mx) * 1e-6 + 1.0)

    def bisect(c):
        lo, hi, n_lo, n_hi = c
        mid = 0.5 * (lo + hi)
        n_mid = count(lambda blk: blk >= mid)
        ge = n_mid >= kf
        return (jnp.where(ge, mid, lo), jnp.where(ge, hi, mid),
                jnp.where(ge, n_mid, n_lo), jnp.where(ge, n_hi, n_mid))

    def open_rows(n_lo, n_hi, settled):
        return (n_lo - kf) * jnp.where(n_lo - n_hi > 2.0, 1.0, 0.0) * (1.0 - settled)

    def search(state, limit, settled):
        def cond(c):
            return jnp.logical_and(c[0] < limit, jnp.max(open_rows(c[3], c[4], settled)) > 0.0)

        def body(c):
            return (c[0] + 2,) + bisect(bisect(c[1:]))

        return lax.while_loop(cond, body, state)

    def band_min_max(lo, hi):
        def body(c, carry):
            blk = sc_ref[c]
            inside = (blk >= lo) & (blk < hi)
            return (jnp.minimum(carry[0], fold(jnp.where(inside, blk, -NEG), jnp.minimum)),
                    jnp.maximum(carry[1], fold(jnp.where(inside, blk, NEG), jnp.maximum)))

        bmin, bmax = lax.fori_loop(0, nch, body, (jnp.full((8, tq), -NEG, F32), jnp.full((8, tq), NEG, F32)))
        return jnp.min(bmin, axis=0, keepdims=True), jnp.max(bmax, axis=0, keepdims=True)

    def finish(state):
        _, lo, hi, n_lo, n_hi = state
        over = n_lo > kf
        bmin, bmax = lax.cond(jnp.max(n_lo - kf) > 0.0, lambda: band_min_max(lo, hi), lambda: (lo, lo))
        single = bmin == bmax
        split = over & (n_lo - n_hi == 2.0) & jnp.logical_not(single)
        unsplit = over & jnp.logical_not(split)
        return jnp.where(split, bmax, lo), jnp.where(over & single, 1.0, 0.0), jnp.where(unsplit, 1.0, 0.0)

    zero = jnp.zeros((1, tq), F32)
    state1 = search((jnp.int32(0), lo0, hi0, n_causal, zero), IDX_TIE_CHECK, zero)
    first = finish(state1)
    state = search(state1, iters, first[1])
    lo, _, unsplit = lax.cond(state[0] > state1[0], lambda: finish(state), lambda: first)
    hi, n_hi = state[2], state[4]
    tied = jnp.max(unsplit) > 0.0

    def fill(j, c):
        mb_ref[j] = jnp.full((tq, tk), NEG, mb_ref.dtype)
        return c

    lax.fori_loop(nch, nk, fill, 0)

    def emit(j, keep):
        mb_ref[j] = jnp.where(keep, 0.0, NEG).T.astype(mb_ref.dtype)

    @pl.when(jnp.logical_not(tied))
    def _():
        def body(j, c):
            emit(j, sc_ref[j] >= lo)
            return c

        lax.fori_loop(0, nch, body, 0)

    @pl.when(tied)
    def _():
        need = kf - n_hi
        tri = tri_ref[...]

        def body(j, base):
            sc = sc_ref[j]
            above = sc >= hi
            band = (sc >= lo) & jnp.logical_not(above)
            prefix = _dot(tri, jnp.where(band, 1.0, 0.0).astype(BF16)) + base
            emit(j, above | (band & (prefix <= need)))
            return prefix[tk - 1:tk, :]

        lax.fori_loop(0, nch, body, zero)


def dsa_index_mask(proj, *, B, S, tqi, tk, topk):
    nq = S // tqi
    nk = S // tk
    assert tk % tqi == 0 and tk % IDX_CHUNK == 0
    tri = jnp.asarray(np.tril(np.ones((tk, tk), np.float32)), BF16)
    return pl.pallas_call(
        functools.partial(_dsa_index_kernel, tq=tqi, tk=tk, S=S, topk=topk, iters=BISECT_ITERS),
        grid=(B, nq),
        in_specs=[pl.BlockSpec((tqi, 512), lambda b, i: (b * nq + i, OFF_IDX_Q // 512)),
                  pl.BlockSpec((S, LANE), lambda b, i: (b, OFF_IDX_K // LANE)),
                  pl.BlockSpec((tqi, LANE), lambda b, i: (b * nq + i, OFF_IDX_W // LANE)),
                  pl.BlockSpec((tk, tk), lambda b, i: (0, 0))],
        out_specs=pl.BlockSpec((None, nk, tqi, tk), lambda b, i: (b, 0, i, 0)),
        out_shape=jax.ShapeDtypeStruct((B, nk, S, tk), BF16),
        scratch_shapes=[pltpu.VMEM((nk, tk, tqi), F32)],
        compiler_params=_cparams(("parallel", "parallel")),
        name="dsa_index_mask",
    )(proj, proj, proj, tri)


def _dsa_attn_kernel(q_ref, k_ref, v_ref, mb_ref, bias_ref, o_ref, m_ref, acc_ref, *, tq, scale):
    qi = pl.program_id(2)
    hs = range(HPS)
    cols = lambda h: slice(h * LANE, (h + 1) * LANE)
    q = [q_ref[:, cols(h)] for h in hs]
    for h in hs:
        _softmax_init(m_ref.at[h], acc_ref.at[h])

    def step(j, nc, bias):
        sl = pl.ds(j * tq if isinstance(j, int) else pl.multiple_of(j * tq, tq), nc * tq)
        mb = mb_ref[j] if nc == 1 else jnp.concatenate([mb_ref[j + i] for i in range(nc)], axis=1)
        mb = mb.astype(F32)
        s = [_dot_nt(q[h], k_ref[sl, cols(h)]) + mb for h in hs]
        for h in hs:
            sh = s[h] if bias is None else s[h] + bias(h)
            _softmax_step(sh, v_ref[sl, cols(h)], m_ref.at[h], acc_ref.at[h], scale)

    _far_chunks(jnp.maximum(qi - 1, 0), lambda j, nc: step(j, nc, None), FAR_GROUP)

    @pl.when(qi == 0)
    def _():
        step(0, 1, lambda h: bias_ref[h, :, tq:2 * tq])

    @pl.when(qi >= 1)
    def _():
        step(qi - 1, 2, lambda h: bias_ref[h])

    for h in hs:
        o_ref[:, cols(h)] = _softmax_finish(acc_ref.at[h]).astype(o_ref.dtype)


def dsa_attention(proj, maskb, btiles, *, B, S, tq):
    M = B * S
    nq = S // tq
    H = DSA_HEADS
    scale = DSA_HEAD_DIM ** -0.5
    w = HPS * LANE
    return pl.pallas_call(
        functools.partial(_dsa_attn_kernel, tq=tq, scale=scale),
        grid=(B, H // HPS, nq),
        in_specs=[pl.BlockSpec((tq, w), lambda b, h, i: (b * nq + i, OFF_DSA_Q // w + h)),
                  pl.BlockSpec((S, w), lambda b, h, i: (b, OFF_DSA_K // w + h)),
                  pl.BlockSpec((S, w), lambda b, h, i: (b, OFF_DSA_V // w + h)),
                  pl.BlockSpec((None, nq, tq, tq), lambda b, h, i: (b, 0, i, 0)),
                  pl.BlockSpec((HPS, tq, 2 * tq), lambda b, h, i: (h, 0, 0))],
        out_specs=pl.BlockSpec((tq, w), lambda b, h, i: (b * nq + i, h)),
        out_shape=jax.ShapeDtypeStruct((M, H * DSA_HEAD_DIM), BF16),
        scratch_shapes=[pltpu.VMEM((HPS, tq, LANE), F32), pltpu.VMEM((HPS, tq, 2 * LANE), F32)],
        compiler_params=_cparams(("parallel", "parallel", "arbitrary")),
        name="dsa_attention",
    )(proj, proj, proj, maskb, btiles)


def _nsa_compress_kernel(x_ref, pos_ref, w1_ref, w2_ref, o_ref, *, ncp):
    dk = NSA_HEAD_DIM
    posw = _dot(pos_ref[...], w1_ref[...])[0:1, :]
    for g in range(NSA_GROUPS):
        lo = jnp.zeros((ncp, CMP_HIDDEN), F32)
        hi = jnp.zeros((ncp, CMP_HIDDEN), F32)
        for l in range(CMP_STRIDE):
            xs = x_ref[:, (l * NSA_GROUPS + g) * dk:(l * NSA_GROUPS + g + 1) * dk]
            lo = lo + _dot(xs, w1_ref[l * dk:(l + 1) * dk, :])
            hi = hi + _dot(xs, w1_ref[(CMP_STRIDE + l) * dk:(CMP_STRIDE + l + 1) * dk, :])
        hid = lo + pltpu.roll(hi, ncp - 1, 0) + posw
        o_ref[g] = _dot(jax.nn.gelu(hid).astype(BF16), w2_ref[...]).astype(o_ref.dtype)


def nsa_compress(xkv, posflat, w1, w2, *, B, S):
    ncp = S // CMP_STRIDE
    G, dk = NSA_GROUPS, NSA_HEAD_DIM
    width = CMP_STRIDE * G * dk
    return pl.pallas_call(
        functools.partial(_nsa_compress_kernel, ncp=ncp),
        grid=(2, B),
        in_specs=[pl.BlockSpec((None, ncp, width), lambda a, b: (a, b, 0)),
                  pl.BlockSpec((None, 8, CMP_LEN * dk), lambda a, b: (a, 0, 0)),
                  pl.BlockSpec((None, CMP_LEN * dk, CMP_HIDDEN), lambda a, b: (a, 0, 0)),
                  pl.BlockSpec((None, CMP_HIDDEN, dk), lambda a, b: (a, 0, 0))],
        out_specs=pl.BlockSpec((None, None, G, ncp, dk), lambda a, b: (a, b, 0, 0, 0)),
        out_shape=jax.ShapeDtypeStruct((2, B, G, ncp, dk), BF16),
        compiler_params=_cparams(("parallel", "parallel")),
        name="nsa_compress",
    )(xkv, posflat, w1, w2)


def _nsa_cmp_kernel(q_ref, kc_ref, vc_ref, ov_ref, oc_ref, sel_ref, *, tq, ncp, ns, n_sel, scale):
    t0 = pl.program_id(2) * tq
    dk = NSA_HEAD_DIM
    kc = kc_ref[...]
    vc = vc_ref[...]
    trow = lax.broadcasted_iota(jnp.int32, (tq, ncp), 0) + t0
    ccol = lax.broadcasted_iota(jnp.int32, (tq, ncp), 1)
    vis = (ccol * CMP_STRIDE + (CMP_LEN - 1)) <= trow
    psum = jnp.zeros((tq, ncp), F32)
    for h in range(NSA_HG):
        s = jnp.where(vis, _dot_nt(q_ref[:, h * dk:(h + 1) * dk], kc) * scale, NEG)
        m = jnp.max(s, axis=-1, keepdims=True)
        p = jnp.where(vis, jnp.exp(s - m), 0.0)
        d = jnp.sum(p, axis=-1, keepdims=True)
        p = p / jnp.where(d > 0, d, 1.0)
        oc_ref[:, h * dk:(h + 1) * dk] = _dot(p.astype(BF16), vc).astype(oc_ref.dtype)
        psum = psum + p
    p_hi = psum.astype(BF16)
    p_lo = (psum - p_hi.astype(F32)).astype(BF16)
    ov = ov_ref[...]
    imp = _dot_nt(ov, p_hi) + _dot_nt(ov, p_lo)
    imp = imp[:ns]
    jrow = lax.broadcasted_iota(jnp.int32, (ns, tq), 0)
    tcol = lax.broadcasted_iota(jnp.int32, (ns, tq), 1) + t0
    blk_t = tcol // SEL_BLOCK
    forced = (jrow == 0) | (jrow == blk_t) | (jrow == blk_t - 1)
    val = jnp.where(forced, -NEG, imp)
    val = jnp.where(jrow * SEL_BLOCK > tcol, NEG, val)
    rank = jnp.zeros((ns, tq), F32)
    for j in range(ns):
        other = val[j:j + 1, :]
        ahead = (other > val) | ((other == val) & (jrow > j))
        rank = rank + jnp.where(ahead, 1.0, 0.0)
    selneg = jnp.where(rank < float(n_sel), 0.0, NEG)
    if ns < LANE:
        selneg = jnp.concatenate([selneg, jnp.zeros((LANE - ns, tq), F32)], axis=0)
    sel_ref[...] = selneg.T.astype(sel_ref.dtype)


def nsa_cmp_attention(proj, kvc, ovt, *, B, S, tq):
    M = B * S
    nq = S // tq
    G, dk = NSA_GROUPS, NSA_HEAD_DIM
    ncp = S // CMP_STRIDE
    ns = S // SEL_BLOCK
    assert ns <= LANE
    n_sel = min(SEL_COUNT, ns)
    gw = NSA_HG * dk
    return pl.pallas_call(
        functools.partial(_nsa_cmp_kernel, tq=tq, ncp=ncp, ns=ns, n_sel=n_sel, scale=dk ** -0.5),
        grid=(B, G, nq),
        in_specs=[pl.BlockSpec((tq, gw), lambda b, g, i: (b * nq + i, OFF_NSA_Q // gw + g)),
                  pl.BlockSpec((None, None, None, ncp, dk), lambda b, g, i: (0, b, g, 0, 0)),
                  pl.BlockSpec((None, None, None, ncp, dk), lambda b, g, i: (1, b, g, 0, 0)),
                  pl.BlockSpec((LANE, ncp), lambda b, g, i: (0, 0))],
        out_specs=[pl.BlockSpec((tq, gw), lambda b, g, i: (b * nq + i, g)),
                   pl.BlockSpec((None, None, tq, LANE), lambda b, g, i: (b, g, i, 0))],
        out_shape=[jax.ShapeDtypeStruct((M, G * gw), BF16),
                   jax.ShapeDtypeStruct((B, G, S, LANE), BF16)],
        compiler_params=_cparams(("parallel", "parallel", "parallel")),
        name="nsa_cmp_attention",
    )(proj, kvc, kvc, ovt)


def _nsa_main_kernel(q_ref, sel_ref, ks_ref, vs_ref, kw_ref, vw_ref, ind_ref, bias_ref, gate_ref, oc_ref,
                     o_ref, m_ref, acc_ref, *, tq, nback, scale):
    g = pl.program_id(1)
    qi = pl.program_id(2)
    dk = NSA_HEAD_DIM
    HG = NSA_HG
    ns = NSA_STREAMS
    hp = HG // ns
    st = range(ns)
    stack = lambda f: [jnp.concatenate([f(a * hp + i) for i in range(hp)], axis=0) for a in st]
    q = stack(lambda h: q_ref[:, h * dk:(h + 1) * dk])
    sel = jnp.concatenate([sel_ref[...]] * hp, axis=0)
    qa = [jnp.concatenate([q[a], sel], axis=1) for a in st]
    gate = jax.nn.sigmoid(gate_ref[...].astype(F32))

    for a in range(2 * ns):
        _softmax_init(m_ref.at[a], acc_ref.at[a])

    def steps(s, v, fix, base=0):
        for a in st:
            _softmax_step(fix(a, s[a]), v, m_ref.at[base + a], acc_ref.at[base + a], scale)

    wspan = nback + 1

    def rows_from(j, width):
        return pl.ds(j * tq if isinstance(j, int) else pl.multiple_of(j * tq, tq), width)

    def sel_scores(j, width=tq):
        sl = rows_from(j, width)
        ka = jnp.concatenate([ks_ref[sl, :], ind_ref[sl, :]], axis=1)
        return [_dot_nt(qa[a], ka) for a in st], vs_ref[sl, :]

    def sel_far(j, nc):
        s, v = sel_scores(j, nc * tq)
        steps(s, v, lambda a, x: x)

    _far_chunks(jnp.maximum(qi - 1, 0), sel_far, FAR_GROUP_NSA)

    def win_scores(j, width):
        sl = rows_from(j, width)
        kw = kw_ref[sl, :]
        return [_dot_nt(q[a], kw) for a in st], vw_ref[sl, :]

    def near(first_s, nc_s, first_w, nc_w):
        tile = lambda nc: (lambda a, x: x + bias_ref[a, :, (wspan - nc) * tq:wspan * tq])
        s_s, v_s = sel_scores(first_s, nc_s * tq)
        s_w, v_w = win_scores(first_w, nc_w * tq)
        steps(s_s, v_s, tile(nc_s))
        steps(s_w, v_w, tile(nc_w), base=ns)

    for k in range(wspan - 1):
        @pl.when(qi == k)
        def _(k=k):
            near(max(k - 1, 0), min(k + 1, 2), 0, k + 1)

    @pl.when(qi >= wspan - 1)
    def _():
        near(qi - 1, 2, qi - (wspan - 1), wspan)

    o_s = [_softmax_finish(acc_ref.at[a]) for a in st]
    o_w = [_softmax_finish(acc_ref.at[ns + a]) for a in st]

    lane = lax.broadcasted_iota(jnp.int32, gate.shape, 1)
    for h in range(HG):
        c = g * HG + h
        a, sl = h // hp, slice((h % hp) * tq, (h % hp + 1) * tq)
        gc = [jnp.sum(jnp.where(lane == br * NSA_HEADS + c, gate, 0.0), axis=-1, keepdims=True)
              for br in range(3)]
        o = gc[0] * oc_ref[:, h * dk:(h + 1) * dk].astype(F32) + gc[1] * o_s[a][sl] + gc[2] * o_w[a][sl]
        o_ref[:, h * dk:(h + 1) * dk] = o.astype(o_ref.dtype)


def nsa_main(proj, selneg, ind, btiles, o_c, *, B, S, tq):
    M = B * S
    nq = S // tq
    G, dk, HG = NSA_GROUPS, NSA_HEAD_DIM, NSA_HG
    gw = HG * dk
    assert WINDOW % tq == 0 and WINDOW // tq >= 2
    nback = WINDOW // tq
    ns, hp = NSA_STREAMS, HG // NSA_STREAMS
    kv = lambda n: pl.BlockSpec((S, dk), lambda b, g, i, n=n: (b, (OFF_NSA_KV + n * G * dk) // dk + g))
    return pl.pallas_call(
        functools.partial(_nsa_main_kernel, tq=tq, nback=nback, scale=dk ** -0.5),
        grid=(B, G, nq),
        in_specs=[pl.BlockSpec((tq, gw), lambda b, g, i: (b * nq + i, OFF_NSA_Q // gw + g)),
                  pl.BlockSpec((None, None, tq, LANE), lambda b, g, i: (b, g, i, 0)),
                  kv(2), kv(3), kv(4), kv(5),
                  pl.BlockSpec((S, LANE), lambda b, g, i: (0, 0)),
                  pl.BlockSpec((ns, hp * tq, (nback + 1) * tq), lambda b, g, i: (g, 0, 0)),
                  pl.BlockSpec((tq, LANE), lambda b, g, i: (b * nq + i, OFF_NSA_GATE // LANE)),
                  pl.BlockSpec((tq, gw), lambda b, g, i: (b * nq + i, g))],
        out_specs=pl.BlockSpec((tq, gw), lambda b, g, i: (b * nq + i, g)),
        out_shape=jax.ShapeDtypeStruct((M, G * gw), BF16),
        scratch_shapes=[pltpu.VMEM((2 * ns, hp * tq, LANE), F32),
                        pltpu.VMEM((2 * ns, hp * tq, 2 * LANE), F32)],
        compiler_params=_cparams(("parallel", "parallel", "arbitrary")),
        name="nsa_main",
    )(proj, selneg, proj, proj, proj, proj, ind, btiles, proj, o_c)


def _merge_kernel(x_ref, oa_ref, ob_ref, oc_ref, g0_ref, g1_ref, g2_ref, wb_ref, wo_ref, gain_ref, o_ref):
    merged = None
    for o_r, g_r, br in ((oa_ref, g0_ref, 0), (ob_ref, g1_ref, 1), (oc_ref, g2_ref, 2)):
        t = jax.nn.sigmoid(g_r[...].astype(F32)) * _dot(o_r[...], wb_ref[br])
        merged = t if merged is None else merged + t
    y = _dot(merged.astype(BF16), wo_ref[...])
    o_ref[...] = x_ref[...] + _rms(y, gain_ref[...])


def merge_branches(x, o_a, o_b, o_c, proj, wb, wo, gain, *, tm):
    M, D = x.shape
    row = pl.BlockSpec((tm, D), lambda i: (i, 0))
    gate = lambda br: pl.BlockSpec((tm, D), lambda i, br=br: (i, OFF_BGATE // D + br))
    return pl.pallas_call(
        _merge_kernel,
        grid=(M // tm,),
        in_specs=[row, row, row, row, gate(0), gate(1), gate(2),
                  pl.BlockSpec(wb.shape, lambda i: (0, 0, 0)),
                  pl.BlockSpec(wo.shape, lambda i: (0, 0)),
                  pl.BlockSpec((1, D), lambda i: (0, 0))],
        out_specs=row,
        out_shape=jax.ShapeDtypeStruct((M, D), F32),
        compiler_params=_cparams(("parallel",)),
        name="merge_branches",
    )(x, o_a, o_b, o_c, proj, proj, proj, wb, wo, gain)


def _cross_kernel(x_ref, kv_ref, wq_ref, wo_ref, gpre_ref, gpost_ref, o_ref, *, scale):
    x = x_ref[...]
    h = _rms(x, gpre_ref[...]).astype(BF16)
    q = _dot(h, wq_ref[...]).astype(BF16)
    dh = CROSS_HEAD_DIM
    outs = []
    for hd in range(CROSS_HEADS):
        k = kv_ref[:, hd * dh:(hd + 1) * dh]
        v = kv_ref[:, D_MODEL + hd * dh:D_MODEL + (hd + 1) * dh]
        s = _dot_nt(q[:, hd * dh:(hd + 1) * dh], k) * scale
        p = jnp.exp(s - jnp.max(s, axis=-1, keepdims=True))
        p = p / jnp.sum(p, axis=-1, keepdims=True)
        outs.append(_dot(p.astype(BF16), v).astype(BF16))
    y = _dot(jnp.concatenate(outs, axis=1), wo_ref[...])
    o_ref[...] = x + _rms(y, gpost_ref[...])


def cross_attention(x, kv, wq, wo, gpre, gpost, *, S, tm):
    M, D = x.shape
    mlen = kv.shape[0] // (M // S)
    per_b = S // tm
    return pl.pallas_call(
        functools.partial(_cross_kernel, scale=CROSS_HEAD_DIM ** -0.5),
        grid=(M // tm,),
        in_specs=[pl.BlockSpec((tm, D), lambda i: (i, 0)),
                  pl.BlockSpec((mlen, 2 * D), lambda i: (i // per_b, 0)),
                  pl.BlockSpec(wq.shape, lambda i: (0, 0)),
                  pl.BlockSpec(wo.shape, lambda i: (0, 0)),
                  pl.BlockSpec((1, D), lambda i: (0, 0)),
                  pl.BlockSpec((1, D), lambda i: (0, 0))],
        out_specs=pl.BlockSpec((tm, D), lambda i: (i, 0)),
        out_shape=jax.ShapeDtypeStruct((M, D), F32),
        compiler_params=_cparams(("parallel",)),
        name="cross_attention",
    )(x, kv, wq, wo, gpre, gpost)


def _mlp_kernel(x_ref, w1_ref, w2_ref, gpre_ref, gpost_ref, o_ref, h_ref, acc_ref):
    j = pl.program_id(1)

    @pl.when(j == 0)
    def _():
        h_ref[...] = _rms(x_ref[...], gpre_ref[...]).astype(BF16)
        acc_ref[...] = jnp.zeros(acc_ref.shape, F32)

    a = jnp.maximum(_dot(h_ref[...], w1_ref[...]), 0.0)
    acc_ref[...] += _dot((a * a).astype(BF16), w2_ref[...])

    @pl.when(j == pl.num_programs(1) - 1)
    def _():
        o_ref[...] = x_ref[...] + _rms(acc_ref[...], gpost_ref[...])


def mlp(x, w1, w2, gpre, gpost, *, tm, tf):
    M, D = x.shape
    FF = w1.shape[1]
    return pl.pallas_call(
        _mlp_kernel,
        grid=(M // tm, FF // tf),
        in_specs=[pl.BlockSpec((tm, D), lambda i, j: (i, 0)),
                  pl.BlockSpec((D, tf), lambda i, j: (0, j)),
                  pl.BlockSpec((tf, D), lambda i, j: (j, 0)),
                  pl.BlockSpec((1, D), lambda i, j: (0, 0)),
                  pl.BlockSpec((1, D), lambda i, j: (0, 0))],
        out_specs=pl.BlockSpec((tm, D), lambda i, j: (i, 0)),
        out_shape=jax.ShapeDtypeStruct((M, D), F32),
        scratch_shapes=[pltpu.VMEM((tm, D), BF16), pltpu.VMEM((tm, D), F32)],
        compiler_params=_cparams(("parallel", "arbitrary")),
        name="mlp",
    )(x, w1, w2, gpre, gpost)


def _rot_half_cols(w):
    half = w.shape[-1] // 2
    return jnp.concatenate([-w[..., half:], w[..., :half]], axis=-1)


def _pack_w_in(w):
    widths = (MLA_Q_LORA, MLA_KV_LORA, MLA_ROPE, 1024, 1024, 1024, IDX_HEADS * IDX_DIM, IDX_DIM, IDX_HEADS,
              1024, 256, 256, 256, 256, 256, 256, 3 * NSA_HEADS, N_BRANCH * D_MODEL)
    offs = np.concatenate([[0], np.cumsum(widths)])
    seg = [w[:, offs[i]:offs[i + 1]] for i in range(len(widths))]
    (q_lat, kv_lat, k_rope, dsa_q, dsa_k, dsa_v, idx_q, idx_k, idx_w,
     nsa_q, kc, vc, ks, vs, kw, vw, nsa_gate, bgate) = seg
    z = lambda n: jnp.zeros((w.shape[0], n), w.dtype)
    cols = [dsa_q, dsa_k, dsa_v, nsa_q, bgate, idx_q, kc, vc, ks, vs, kw, vw,
            q_lat, kv_lat, k_rope, z(LANE - MLA_ROPE), _rot_half_cols(k_rope), z(LANE - MLA_ROPE), z(LANE),
            idx_k, idx_k, idx_w, z(LANE - IDX_HEADS), nsa_gate, z(LANE - 3 * NSA_HEADS)]
    out = jnp.concatenate(cols, axis=1)
    out = jnp.concatenate([out, z(N_PACK - out.shape[1])], axis=1)
    return out.astype(BF16)


def _pack_mla_weights(w_uq, w_ukv):
    H = MLA_HEADS
    uq = w_uq.reshape(MLA_Q_LORA, H, MLA_NOPE + MLA_ROPE)
    nope = uq[:, :, :MLA_NOPE].reshape(MLA_Q_LORA, H * MLA_NOPE)
    rope = uq[:, :, MLA_NOPE:]
    pad = jnp.zeros((MLA_Q_LORA, H, LANE - MLA_ROPE), w_uq.dtype)
    wqr = jnp.concatenate([rope, pad], axis=-1).reshape(MLA_Q_LORA, H * LANE)
    wqrr = jnp.concatenate([_rot_half_cols(rope), pad], axis=-1).reshape(MLA_Q_LORA, H * LANE)
    ukv = w_ukv.reshape(MLA_KV_LORA, H, MLA_NOPE + MLA_V)
    wkv = jnp.concatenate([ukv[:, :, :MLA_NOPE].reshape(MLA_KV_LORA, H * MLA_NOPE),
                           ukv[:, :, MLA_NOPE:].reshape(MLA_KV_LORA, H * MLA_V)], axis=1)
    return nope.astype(BF16), wqr.astype(BF16), wqrr.astype(BF16), wkv.astype(BF16)


def _constants(S):
    half = MLA_ROPE // 2
    inv_freq = ROPE_BASE ** (-np.arange(0, MLA_ROPE, 2, dtype=np.float32) / MLA_ROPE)
    freq = np.zeros((1, LANE), np.float32)
    freq[0, :half] = inv_freq
    freq[0, half:2 * half] = inv_freq
    ncp = S // CMP_STRIDE
    ns = S // SEL_BLOCK
    c_start = np.arange(ncp) * CMP_STRIDE
    j_start = np.arange(ns) * SEL_BLOCK
    overlap = ((c_start[None, :] < j_start[:, None] + SEL_BLOCK) &
               (c_start[None, :] + CMP_LEN > j_start[:, None])).astype(np.float32)
    overlap[:, ncp - 1] = 0.0
    ovt = np.zeros((LANE, ncp), np.float32)
    ovt[:ns] = overlap
    ind = np.zeros((S, LANE), np.float32)
    ind[np.arange(S), np.arange(S) // SEL_BLOCK] = 1.0
    return jnp.asarray(freq), jnp.asarray(ovt, BF16), jnp.asarray(ind, BF16)


def kernel(x, mem, positions, rel_bias, norm_gains, w_in, mla_q_norm, mla_kv_norm, mla_w_uq, mla_w_ukv,
           nsa_cmp_pos, nsa_cmp_w1, nsa_cmp_w2, w_branch, w_out, cross_wq, cross_wkv, cross_wo, mlp_w1, mlp_w2):
    B, S, D = x.shape
    M = B * S
    depth = w_in.shape[0]
    tq = TQ if S % TQ == 0 else TQ_NSA
    tqn = TQ_NSA
    assert S % tq == 0 and S % tqn == 0 and D == D_MODEL
    topk = min(DSA_TOPK_MAX, S // 4)
    freq, ovt, ind = _constants(S)
    btiles = bias_tiles(rel_bias, tq, 0, DSA_HEADS, nspan=2, masked=False, stack=1)
    btiles_n = bias_tiles(rel_bias, tqn, DSA_HEADS, NSA_HEADS, nspan=WINDOW // tqn + 1, masked=True,
                          stack=NSA_HG // NSA_STREAMS)
    pos = positions.reshape(M, 1)
    xf = x.reshape(M, D)
    memf = mem.reshape(B * mem.shape[1], D)
    tm_big = 1024 if M % 1024 == 0 else tq
    tm = 512 if M % 512 == 0 else tq
    ncp = S // CMP_STRIDE
    row = lambda v: v.reshape(1, -1)

    for l in range(depth):
        g = norm_gains[l]
        proj = norm_matmul(xf, row(g[0]), _pack_w_in(w_in[l]), tm=tm_big, tn=TN_IN)

        wqn, wqr, wqrr, wkv = _pack_mla_weights(mla_w_uq[l], mla_w_ukv[l])
        qn, qr, kn, vv, kr = mla_prep(proj, pos, freq, row(mla_q_norm[l]), row(mla_kv_norm[l]),
                                      wqn, wqr, wqrr, wkv, tm=tm)
        o_a = mla_attention(qn, qr, kn, kr, vv, B=B, S=S, tq=tq)

        maskb = dsa_index_mask(proj, B=B, S=S, tqi=TQ_IDX, tk=tq, topk=topk)
        o_b = dsa_attention(proj, maskb, btiles, B=B, S=S, tq=tq)

        kvc_in = jnp.stack([
            proj[:, OFF_NSA_KV + a * 256:OFF_NSA_KV + (a + 1) * 256].reshape(B * ncp, CMP_STRIDE * 256)
            for a in range(2)])
        posflat = jnp.broadcast_to(nsa_cmp_pos[l].reshape(2, 1, CMP_LEN * NSA_HEAD_DIM),
                                   (2, 8, CMP_LEN * NSA_HEAD_DIM)).astype(BF16)
        kvc = nsa_compress(kvc_in, posflat, nsa_cmp_w1[l].astype(BF16), nsa_cmp_w2[l].astype(BF16), B=B, S=S)
        o_cmp, selneg = nsa_cmp_attention(proj, kvc, ovt, B=B, S=S, tq=tq)
        o_c = nsa_main(proj, selneg, ind, btiles_n, o_cmp, B=B, S=S, tq=tqn)

        xf = merge_branches(xf, o_a, o_b, o_c, proj, w_branch[l].astype(BF16), w_out[l].astype(BF16),
                            row(g[1]), tm=tm)

        mkv = norm_matmul(memf, row(g[3]), cross_wkv[l].astype(BF16), tm=memf.shape[0] // B, tn=1024)
        xf = cross_attention(xf, mkv, cross_wq[l].astype(BF16), cross_wo[l].astype(BF16),
                             row(g[2]), row(g[4]), S=S, tm=tm)

        xf = mlp(xf, mlp_w1[l].astype(BF16), mlp_w2[l].astype(BF16), row(g[5]), row(g[6]), tm=tm_big, tf=1024)

    return xf.reshape(B, S, D)
```

```python
import functools
import math

import numpy as np
import jax
import jax.numpy as jnp
from jax import lax
from jax.experimental import pallas as pl
from jax.experimental.pallas import tpu as pltpu

F32 = jnp.float32
BF16 = jnp.bfloat16

D_MODEL = 1024
EPS = 1e-6
MLA_HEADS = 8
MLA_Q_LORA = 384
MLA_KV_LORA = 256
MLA_NOPE = 128
MLA_ROPE = 64
MLA_V = 128
ROPE_BASE = 10000.0
DSA_HEADS = 8
DSA_HEAD_DIM = 128
IDX_HEADS = 8
IDX_DIM = 64
DSA_TOPK_MAX = 256
NSA_HEADS = 8
NSA_GROUPS = 2
NSA_HG = NSA_HEADS // NSA_GROUPS
NSA_HEAD_DIM = 128
CMP_STRIDE = 16
CMP_LEN = 2 * CMP_STRIDE
CMP_HIDDEN = 256
SEL_BLOCK = 64
SEL_COUNT = 16
WINDOW = 512
REL_BUCKETS = 32
REL_MAX_DIST = 128
REL_HEADS = DSA_HEADS + NSA_HEADS
CROSS_HEADS = 4
CROSS_HEAD_DIM = D_MODEL // CROSS_HEADS
D_FF = 4 * D_MODEL
N_BRANCH = 3

LANE = 128
NEG = -1e30
VMEM_LIMIT = 48 * 1024 * 1024

TQ = 512
TQ_NSA = 256
TQ_CMP = 1024
HPS = 2
FAR_GROUP = 4
FAR_GROUP_NSA = 8
NSA_STREAMS = 2
BISECT_ITERS = 32
IDX_TIE_CHECK = 16
TQ_IDX = 512
IDX_CHUNK = 256

OFF_DSA_Q = 0
OFF_DSA_K = 1024
OFF_DSA_V = 2048
OFF_NSA_Q = 3072
OFF_BGATE = 4096
OFF_IDX_Q = 7168
OFF_NSA_KV = 7680
OFF_MLA = 9216
OFF_IDX_K = 10240
OFF_IDX_W = 10368
OFF_NSA_GATE = 10496
N_PACK = 10752
TN_IN = 2688


def _cparams(sem):
    return pltpu.CompilerParams(dimension_semantics=sem, vmem_limit_bytes=VMEM_LIMIT)


def _rms(x, g):
    return x * lax.rsqrt(jnp.mean(x * x, axis=-1, keepdims=True) + EPS) * g


def _dot(a, b):
    return jnp.dot(a, b, preferred_element_type=F32)


def _dot_nt(a, b):
    return lax.dot_general(a, b, (((1,), (1,)), ((), ())), preferred_element_type=F32)


def _norm_mm_kernel(x_ref, g_ref, w_ref, o_ref, xn_ref):
    @pl.when(pl.program_id(1) == 0)
    def _():
        xn_ref[...] = _rms(x_ref[...].astype(F32), g_ref[...]).astype(BF16)

    o_ref[...] = _dot(xn_ref[...], w_ref[...]).astype(o_ref.dtype)


def norm_matmul(x, g, w, *, tm, tn):
    M, K = x.shape
    N = w.shape[1]
    return pl.pallas_call(
        _norm_mm_kernel,
        grid=(M // tm, N // tn),
        in_specs=[pl.BlockSpec((tm, K), lambda i, j: (i, 0)),
                  pl.BlockSpec((1, K), lambda i, j: (0, 0)),
                  pl.BlockSpec((K, tn), lambda i, j: (0, j))],
        out_specs=pl.BlockSpec((tm, tn), lambda i, j: (i, j)),
        out_shape=jax.ShapeDtypeStruct((M, N), BF16),
        scratch_shapes=[pltpu.VMEM((tm, K), BF16)],
        compiler_params=_cparams(("parallel", "arbitrary")),
        name="norm_matmul",
    )(x, g, w)


def _bias_tiles_kernel(tab_ref, bkt_ref, o_ref, *, inv_scale, h0, tq, nspan, masked):
    h = pl.program_id(0) + h0
    bkt = bkt_ref[...]
    far = tab_ref[REL_BUCKETS - 1, h]
    out = jnp.zeros(bkt.shape, F32)
    for b in range(REL_BUCKETS - 1):
        out = jnp.where(bkt == b, (tab_ref[b, h] - far) * inv_scale, out)
    diag, prev = out[0], out[1]
    r, c = _tri(tq, tq)
    if masked:
        diag = jnp.where(r >= c, diag, NEG)
    o_ref[:, (nspan - 1) * tq:nspan * tq] = diag
    o_ref[:, (nspan - 2) * tq:(nspan - 1) * tq] = prev
    for k in range(nspan - 2):
        edge = jnp.where(r < c, 0.0, NEG) if (masked and k == 0) else jnp.zeros((tq, tq), F32)
        o_ref[:, k * tq:(k + 1) * tq] = edge


def _bucket_tiles(tq):
    i = np.arange(tq)[:, None]
    j = np.arange(tq)[None, :]
    d = np.stack([i - j, tq + i - j]).astype(np.int32)
    n = jnp.maximum(jnp.asarray(d), 0)
    exact = REL_BUCKETS // 2
    nf = jnp.maximum(n, 1).astype(F32)
    log_b = exact + (jnp.log(nf / exact) / math.log(REL_MAX_DIST / exact) * (REL_BUCKETS - exact)).astype(jnp.int32)
    return jnp.where(n < exact, n, jnp.minimum(log_b, REL_BUCKETS - 1)).astype(jnp.int32)


def bias_tiles(rel_bias, tq, h0, nh, *, nspan, masked, stack):
    assert tq >= REL_MAX_DIST
    bkt = _bucket_tiles(tq)
    assert DSA_HEAD_DIM == NSA_HEAD_DIM
    return pl.pallas_call(
        functools.partial(_bias_tiles_kernel, inv_scale=DSA_HEAD_DIM ** 0.5, h0=h0, tq=tq, nspan=nspan,
                          masked=masked),
        grid=(nh,),
        in_specs=[pl.BlockSpec(memory_space=pltpu.SMEM),
                  pl.BlockSpec((2, tq, tq), lambda h: (0, 0, 0))],
        out_specs=pl.BlockSpec((None, tq, nspan * tq), lambda h: (h // stack, h % stack, 0)),
        out_shape=jax.ShapeDtypeStruct((nh // stack, stack * tq, nspan * tq), F32),
        compiler_params=_cparams(("arbitrary",)),
        name="bias_tiles",
    )(rel_bias, bkt)


def _mla_prep_kernel(lat_ref, pos_ref, freq_ref, qg_ref, kvg_ref, wqn_ref, wqr_ref, wqrr_ref, wkv_ref,
                     qn_ref, qr_ref, kn_ref, v_ref, kr_ref):
    lat = lat_ref[...].astype(F32)
    qlat = _rms(lat[:, :MLA_Q_LORA], qg_ref[...]).astype(BF16)
    kvlat = _rms(lat[:, MLA_Q_LORA:MLA_Q_LORA + MLA_KV_LORA], kvg_ref[...]).astype(BF16)
    o = MLA_Q_LORA + MLA_KV_LORA
    kr = lat[:, o:o + LANE]
    krr = lat[:, o + LANE:o + 2 * LANE]
    ang = pos_ref[...].astype(F32) * freq_ref[...]
    cos, sin = jnp.cos(ang), jnp.sin(ang)
    cos8 = jnp.concatenate([cos] * MLA_HEADS, axis=1)
    sin8 = jnp.concatenate([sin] * MLA_HEADS, axis=1)
    qn_ref[...] = _dot(qlat, wqn_ref[...]).astype(BF16)
    qr_ref[...] = (_dot(qlat, wqr_ref[...]) * cos8 + _dot(qlat, wqrr_ref[...]) * sin8).astype(BF16)
    kv = _dot(kvlat, wkv_ref[...])
    nk = MLA_HEADS * MLA_NOPE
    kn_ref[...] = kv[:, :nk].astype(BF16)
    v_ref[...] = kv[:, nk:].astype(BF16)
    kr_ref[...] = (kr * cos + krr * sin).astype(BF16)


def mla_prep(proj, pos, freq, qg, kvg, wqn, wqr, wqrr, wkv, *, tm):
    M = proj.shape[0]
    HD = MLA_HEADS * LANE
    full = lambda a: pl.BlockSpec(a.shape, lambda i: (0, 0))
    outs = [jax.ShapeDtypeStruct((M, HD), BF16)] * 4 + [jax.ShapeDtypeStruct((M, LANE), BF16)]
    return pl.pallas_call(
        _mla_prep_kernel,
        grid=(M // tm,),
        in_specs=[pl.BlockSpec((tm, 1024), lambda i: (i, OFF_MLA // 1024)),
                  pl.BlockSpec((tm, 1), lambda i: (i, 0)),
                  full(freq), full(qg), full(kvg), full(wqn), full(wqr), full(wqrr), full(wkv)],
        out_specs=[pl.BlockSpec((tm, HD), lambda i: (i, 0))] * 4 + [pl.BlockSpec((tm, LANE), lambda i: (i, 0))],
        out_shape=outs,
        compiler_params=_cparams(("parallel",)),
        name="mla_prep",
    )(proj, pos, freq, qg, kvg, wqn, wqr, wqrr, wkv)


def _softmax_init(m_ref, acc_ref):
    m_ref[...] = jnp.full(m_ref.shape, NEG, F32)
    acc_ref[...] = jnp.zeros(acc_ref.shape, F32)


def _with_ones(v):
    return jnp.concatenate([v, jnp.ones(v.shape, v.dtype)], axis=1)


def _softmax_step(s, v, m_ref, acc_ref, scale):
    c = scale * math.log2(math.e)
    m_prev = m_ref[...]
    m_new = jnp.maximum(m_prev, jnp.max(s, axis=-1, keepdims=True))
    alpha = jnp.exp2((m_prev - m_new) * c)
    p = jnp.exp2((s - jnp.tile(m_new, (1, s.shape[1] // LANE))) * c)
    acc_ref[...] = jnp.tile(alpha, (1, 2)) * acc_ref[...] + _dot(p.astype(BF16), _with_ones(v))
    m_ref[...] = m_new


def _softmax_finish(acc_ref):
    acc = acc_ref[...]
    return acc[:, :LANE] / acc[:, LANE:]


def _far_chunks(n, step, largest):
    def body(p, c):
        step(largest * p, largest)
        return c

    lax.fori_loop(0, n // largest, body, 0)
    w = largest // 2
    while w >= 1:
        @pl.when(n % (2 * w) >= w)
        def _(w=w):
            step(n // (2 * w) * (2 * w), w)
        w //= 2


def _tri(tq, tk):
    row = lax.broadcasted_iota(jnp.int32, (tq, tk), 0)
    col = lax.broadcasted_iota(jnp.int32, (tq, tk), 1)
    return row, col


def _mla_attn_kernel(qn_ref, qr_ref, kn_ref, kr_ref, v_ref, o_ref, m_ref, acc_ref, *, tq, scale):
    qi = pl.program_id(2)
    hs = range(HPS)
    q = [jnp.concatenate([qn_ref[:, h * LANE:(h + 1) * LANE], qr_ref[:, h * LANE:(h + 1) * LANE]], axis=-1)
         for h in hs]
    for h in hs:
        _softmax_init(m_ref.at[h], acc_ref.at[h])

    def step(j, nc, causal_last):
        sl = pl.ds(j * tq if isinstance(j, int) else pl.multiple_of(j * tq, tq), nc * tq)
        kr = kr_ref[sl, :]
        if causal_last:
            row, col = _tri(tq, nc * tq)
            ok = col <= row + (nc - 1) * tq
        s = [_dot_nt(q[h], jnp.concatenate([kn_ref[sl, h * LANE:(h + 1) * LANE], kr], axis=-1)) for h in hs]
        for h in hs:
            sh = jnp.where(ok, s[h], NEG) if causal_last else s[h]
            _softmax_step(sh, v_ref[sl, h * LANE:(h + 1) * LANE], m_ref.at[h], acc_ref.at[h], scale)

    _far_chunks(jnp.maximum(qi - 1, 0), lambda j, nc: step(j, nc, False), FAR_GROUP)

    @pl.when(qi == 0)
    def _():
        step(0, 1, True)

    @pl.when(qi >= 1)
    def _():
        step(qi - 1, 2, True)

    for h in hs:
        o_ref[:, h * LANE:(h + 1) * LANE] = _softmax_finish(acc_ref.at[h]).astype(o_ref.dtype)


def mla_attention(qn, qr, kn, kr, v, *, B, S, tq):
    M = B * S
    nq = S // tq
    H = MLA_HEADS
    scale = (MLA_NOPE + MLA_ROPE) ** -0.5
    w = HPS * LANE
    qspec = pl.BlockSpec((tq, w), lambda b, h, i: (b * nq + i, h))
    kspec = pl.BlockSpec((S, w), lambda b, h, i: (b, h))
    return pl.pallas_call(
        functools.partial(_mla_attn_kernel, tq=tq, scale=scale),
        grid=(B, H // HPS, nq),
        in_specs=[qspec, qspec, kspec, pl.BlockSpec((S, LANE), lambda b, h, i: (b, 0)), kspec],
        out_specs=qspec,
        out_shape=jax.ShapeDtypeStruct((M, H * MLA_V), BF16),
        scratch_shapes=[pltpu.VMEM((HPS, tq, LANE), F32), pltpu.VMEM((HPS, tq, 2 * LANE), F32)],
        compiler_params=_cparams(("parallel", "parallel", "arbitrary")),
        name="mla_attention",
    )(qn, qr, kn, kr, v)


def _dsa_index_kernel(iq_ref, ik_ref, iw_ref, tri_ref, mb_ref, sc_ref, *, tq, tk, S, topk, iters):
    t0 = pl.program_id(1) * tq
    nk = S // tk
    nch = (t0 + tq - 1) // tk + 1
    hw = IDX_CHUNK
    iq = iq_ref[...]
    lane = lax.broadcasted_iota(jnp.int32, (tq, LANE), 1)
    qs = []
    for h in range(IDX_HEADS):
        blk = iq[:, (h // 2) * LANE:(h // 2 + 1) * LANE]
        keep = (lane >= IDX_DIM) if h % 2 else (lane < IDX_DIM)
        qs.append(jnp.where(keep, blk, jnp.zeros_like(blk)))
    q8 = jnp.concatenate(qs, axis=0)
    iwt = iw_ref[...].astype(F32).T
    wrow = [iwt[h:h + 1, :] for h in range(IDX_HEADS)]
    qpos = lax.broadcasted_iota(jnp.int32, (hw, tq), 1) + t0
    krow = lax.broadcasted_iota(jnp.int32, (hw, tq), 0)

    def fold(x, op):
        out = x[0:8]
        for r in range(1, x.shape[0] // 8):
            out = op(out, x[r * 8:(r + 1) * 8])
        return out

    def score_chunk(c, carry):
        mn, mx = carry
        for half in range(tk // hw):
            k0 = pl.multiple_of(c * tk + half * hw, hw)
            res = _dot_nt(ik_ref[pl.ds(k0, hw), :], q8)
            sc = wrow[0] * jnp.maximum(res[:, 0:tq], 0.0)
            for h in range(1, IDX_HEADS):
                sc = sc + wrow[h] * jnp.maximum(res[:, h * tq:(h + 1) * tq], 0.0)
            causal = (krow + k0) <= qpos
            mn = jnp.minimum(mn, fold(jnp.where(causal, sc, -NEG), jnp.minimum))
            sc = jnp.where(causal, sc, NEG)
            mx = jnp.maximum(mx, fold(sc, jnp.maximum))
            sc_ref[c, half * hw:(half + 1) * hw, :] = sc
        return mn, mx

    mn, mx = lax.fori_loop(0, nch, score_chunk, (jnp.full((8, tq), -NEG, F32), jnp.full((8, tq), NEG, F32)))
    n_causal = (lax.broadcasted_iota(jnp.int32, (1, tq), 1) + (t0 + 1)).astype(F32)
    kf = jnp.minimum(float(topk), n_causal)

    def count(pred):
        def body(c, acc):
            return acc + fold(jnp.where(pred(sc_ref[c]), 1.0, 0.0), jnp.add)

        return jnp.sum(lax.fori_loop(0, nch, body, jnp.zeros((8, tq), F32)), axis=0, keepdims=True)

    lo0 = jnp.min(mn, axis=0, keepdims=True)
    mx = jnp.max(mx, axis=0, keepdims=True)
    hi0 = mx + (jnp.abs(mx) * 1e-6 + 1.0)

    def bisect(c):
        lo, hi, n_lo, n_hi = c
        mid = 0.5 * (lo + hi)
        n_mid = count(lambda blk: blk >= mid)
        ge = n_mid >= kf
        return (jnp.where(ge, mid, lo), jnp.where(ge, hi, mid),
                jnp.where(ge, n_mid, n_lo), jnp.where(ge, n_hi, n_mid))

    def open_rows(n_lo, n_hi, settled):
        return (n_lo - kf) * jnp.where(n_lo - n_hi > 2.0, 1.0, 0.0) * (1.0 - settled)

    def search(state, limit, settled):
        def cond(c):
            return jnp.logical_and(c[0] < limit, jnp.max(open_rows(c[3], c[4], settled)) > 0.0)

        def body(c):
            return (c[0] + 2,) + bisect(bisect(c[1:]))

        return lax.while_loop(cond, body, state)

    def band_min_max(lo, hi):
        def body(c, carry):
            blk = sc_ref[c]
            inside = (blk >= lo) & (blk < hi)
            return (jnp.minimum(carry[0], fold(jnp.where(inside, blk, -NEG), jnp.minimum)),
                    jnp.maximum(carry[1], fold(jnp.where(inside, blk, NEG), jnp.maximum)))

        bmin, bmax = lax.fori_loop(0, nch, body, (jnp.full((8, tq), -NEG, F32), jnp.full((8, tq), NEG, F32)))
        return jnp.min(bmin, axis=0, keepdims=True), jnp.max(bmax, axis=0, keepdims=True)

    def finish(state):
        _, lo, hi, n_lo, n_hi = state
        over = n_lo > kf
        bmin, bmax = lax.cond(jnp.max(n_lo - kf) > 0.0, lambda: band_min_max(lo, hi), lambda: (lo, lo))
        single = bmin == bmax
        split = over & (n_lo - n_hi == 2.0) & jnp.logical_not(single)
        unsplit = over & jnp.logical_not(split)
        return jnp.where(split, bmax, lo), jnp.where(over & single, 1.0, 0.0), jnp.where(unsplit, 1.0, 0.0)

    zero = jnp.zeros((1, tq), F32)
    state1 = search((jnp.int32(0), lo0, hi0, n_causal, zero), IDX_TIE_CHECK, zero)
    first = finish(state1)
    state = search(state1, iters, first[1])
    lo, _, unsplit = lax.cond(state[0] > state1[0], lambda: finish(state), lambda: first)
    hi, n_hi = state[2], state[4]
    tied = jnp.max(unsplit) > 0.0

    def fill(j, c):
        mb_ref[j] = jnp.full((tq, tk), NEG, mb_ref.dtype)
        return c

    lax.fori_loop(nch, nk, fill, 0)

    def emit(j, keep):
        mb_ref[j] = jnp.where(keep, 0.0, NEG).T.astype(mb_ref.dtype)

    @pl.when(jnp.logical_not(tied))
    def _():
        def body(j, c):
            emit(j, sc_ref[j] >= lo)
            return c

        lax.fori_loop(0, nch, body, 0)

    @pl.when(tied)
    def _():
        need = kf - n_hi
        tri = tri_ref[...]

        def body(j, base):
            sc = sc_ref[j]
            above = sc >= hi
            band = (sc >= lo) & jnp.logical_not(above)
            prefix = _dot(tri, jnp.where(band, 1.0, 0.0).astype(BF16)) + base
            emit(j, above | (band & (prefix <= need)))
            return prefix[tk - 1:tk, :]

        lax.fori_loop(0, nch, body, zero)


def dsa_index_mask(proj, *, B, S, tqi, tk, topk):
    nq = S // tqi
    nk = S // tk
    assert tk % tqi == 0 and tk % IDX_CHUNK == 0
    tri = jnp.asarray(np.tril(np.ones((tk, tk), np.float32)), BF16)
    return pl.pallas_call(
        functools.partial(_dsa_index_kernel, tq=tqi, tk=tk, S=S, topk=topk, iters=BISECT_ITERS),
        grid=(B, nq),
        in_specs=[pl.BlockSpec((tqi, 512), lambda b, i: (b * nq + i, OFF_IDX_Q // 512)),
                  pl.BlockSpec((S, LANE), lambda b, i: (b, OFF_IDX_K // LANE)),
                  pl.BlockSpec((tqi, LANE), lambda b, i: (b * nq + i, OFF_IDX_W // LANE)),
                  pl.BlockSpec((tk, tk), lambda b, i: (0, 0))],
        out_specs=pl.BlockSpec((None, nk, tqi, tk), lambda b, i: (b, 0, i, 0)),
        out_shape=jax.ShapeDtypeStruct((B, nk, S, tk), BF16),
        scratch_shapes=[pltpu.VMEM((nk, tk, tqi), F32)],
        compiler_params=_cparams(("parallel", "parallel")),
        name="dsa_index_mask",
    )(proj, proj, proj, tri)


def _dsa_attn_kernel(q_ref, k_ref, v_ref, mb_ref, bias_ref, o_ref, m_ref, acc_ref, *, tq, scale):
    qi = pl.program_id(2)
    hs = range(HPS)
    cols = lambda h: slice(h * LANE, (h + 1) * LANE)
    q = [q_ref[:, cols(h)] for h in hs]
    for h in hs:
        _softmax_init(m_ref.at[h], acc_ref.at[h])

    def step(j, nc, bias):
        sl = pl.ds(j * tq if isinstance(j, int) else pl.multiple_of(j * tq, tq), nc * tq)
        mb = mb_ref[j] if nc == 1 else jnp.concatenate([mb_ref[j + i] for i in range(nc)], axis=1)
        mb = mb.astype(F32)
        s = [_dot_nt(q[h], k_ref[sl, cols(h)]) + mb for h in hs]
        for h in hs:
            sh = s[h] if bias is None else s[h] + bias(h)
            _softmax_step(sh, v_ref[sl, cols(h)], m_ref.at[h], acc_ref.at[h], scale)

    _far_chunks(jnp.maximum(qi - 1, 0), lambda j, nc: step(j, nc, None), FAR_GROUP)

    @pl.when(qi == 0)
    def _():
        step(0, 1, lambda h: bias_ref[h, :, tq:2 * tq])

    @pl.when(qi >= 1)
    def _():
        step(qi - 1, 2, lambda h: bias_ref[h])

    for h in hs:
        o_ref[:, cols(h)] = _softmax_finish(acc_ref.at[h]).astype(o_ref.dtype)


def dsa_attention(proj, maskb, btiles, *, B, S, tq):
    M = B * S
    nq = S // tq
    H = DSA_HEADS
    scale = DSA_HEAD_DIM ** -0.5
    w = HPS * LANE
    return pl.pallas_call(
        functools.partial(_dsa_attn_kernel, tq=tq, scale=scale),
        grid=(B, H // HPS, nq),
        in_specs=[pl.BlockSpec((tq, w), lambda b, h, i: (b * nq + i, OFF_DSA_Q // w + h)),
                  pl.BlockSpec((S, w), lambda b, h, i: (b, OFF_DSA_K // w + h)),
                  pl.BlockSpec((S, w), lambda b, h, i: (b, OFF_DSA_V // w + h)),
                  pl.BlockSpec((None, nq, tq, tq), lambda b, h, i: (b, 0, i, 0)),
                  pl.BlockSpec((HPS, tq, 2 * tq), lambda b, h, i: (h, 0, 0))],
        out_specs=pl.BlockSpec((tq, w), lambda b, h, i: (b * nq + i, h)),
        out_shape=jax.ShapeDtypeStruct((M, H * DSA_HEAD_DIM), BF16),
        scratch_shapes=[pltpu.VMEM((HPS, tq, LANE), F32), pltpu.VMEM((HPS, tq, 2 * LANE), F32)],
        compiler_params=_cparams(("parallel", "parallel", "arbitrary")),
        name="dsa_attention",
    )(proj, proj, proj, maskb, btiles)


def _nsa_compress_kernel(x_ref, pos_ref, w1_ref, w2_ref, o_ref, *, ncp):
    dk = NSA_HEAD_DIM
    posw = _dot(pos_ref[...], w1_ref[...])[0:1, :]
    for g in range(NSA_GROUPS):
        lo = jnp.zeros((ncp, CMP_HIDDEN), F32)
        hi = jnp.zeros((ncp, CMP_HIDDEN), F32)
        for l in range(CMP_STRIDE):
            xs = x_ref[:, (l * NSA_GROUPS + g) * dk:(l * NSA_GROUPS + g + 1) * dk]
            lo = lo + _dot(xs, w1_ref[l * dk:(l + 1) * dk, :])
            hi = hi + _dot(xs, w1_ref[(CMP_STRIDE + l) * dk:(CMP_STRIDE + l + 1) * dk, :])
        hid = lo + pltpu.roll(hi, ncp - 1, 0) + posw
        o_ref[g] = _dot(jax.nn.gelu(hid).astype(BF16), w2_ref[...]).astype(o_ref.dtype)


def nsa_compress(xkv, posflat, w1, w2, *, B, S):
    ncp = S // CMP_STRIDE
    G, dk = NSA_GROUPS, NSA_HEAD_DIM
    width = CMP_STRIDE * G * dk
    return pl.pallas_call(
        functools.partial(_nsa_compress_kernel, ncp=ncp),
        grid=(2, B),
        in_specs=[pl.BlockSpec((None, ncp, width), lambda a, b: (a, b, 0)),
                  pl.BlockSpec((None, 8, CMP_LEN * dk), lambda a, b: (a, 0, 0)),
                  pl.BlockSpec((None, CMP_LEN * dk, CMP_HIDDEN), lambda a, b: (a, 0, 0)),
                  pl.BlockSpec((None, CMP_HIDDEN, dk), lambda a, b: (a, 0, 0))],
        out_specs=pl.BlockSpec((None, None, G, ncp, dk), lambda a, b: (a, b, 0, 0, 0)),
        out_shape=jax.ShapeDtypeStruct((2, B, G, ncp, dk), BF16),
        compiler_params=_cparams(("parallel", "parallel")),
        name="nsa_compress",
    )(xkv, posflat, w1, w2)


def _nsa_cmp_kernel(q_ref, kc_ref, vc_ref, ov_ref, oc_ref, sel_ref, *, tq, ncp, ns, n_sel, scale):
    t0 = pl.program_id(2) * tq
    dk = NSA_HEAD_DIM
    kc = kc_ref[...]
    vc = vc_ref[...]
    trow = lax.broadcasted_iota(jnp.int32, (tq, ncp), 0) + t0
    ccol = lax.broadcasted_iota(jnp.int32, (tq, ncp), 1)
    vis = (ccol * CMP_STRIDE + (CMP_LEN - 1)) <= trow
    psum = jnp.zeros((tq, ncp), F32)
    for h in range(NSA_HG):
        s = jnp.where(vis, _dot_nt(q_ref[:, h * dk:(h + 1) * dk], kc) * scale, NEG)
        m = jnp.max(s, axis=-1, keepdims=True)
        p = jnp.where(vis, jnp.exp(s - m), 0.0)
        d = jnp.sum(p, axis=-1, keepdims=True)
        p = p / jnp.where(d > 0, d, 1.0)
        oc_ref[:, h * dk:(h + 1) * dk] = _dot(p.astype(BF16), vc).astype(oc_ref.dtype)
        psum = psum + p
    p_hi = psum.astype(BF16)
    p_lo = (psum - p_hi.astype(F32)).astype(BF16)
    ov = ov_ref[...]
    imp = _dot_nt(ov, p_hi) + _dot_nt(ov, p_lo)
    imp = imp[:ns]
    jrow = lax.broadcasted_iota(jnp.int32, (ns, tq), 0)
    tcol = lax.broadcasted_iota(jnp.int32, (ns, tq), 1) + t0
    blk_t = tcol // SEL_BLOCK
    forced = (jrow == 0) | (jrow == blk_t) | (jrow == blk_t - 1)
    val = jnp.where(forced, -NEG, imp)
    val = jnp.where(jrow * SEL_BLOCK > tcol, NEG, val)
    rank = jnp.zeros((ns, tq), F32)
    for j in range(ns):
        other = val[j:j + 1, :]
        ahead = (other > val) | ((other == val) & (jrow > j))
        rank = rank + jnp.where(ahead, 1.0, 0.0)
    selneg = jnp.where(rank < float(n_sel), 0.0, NEG)
    if ns < LANE:
        selneg = jnp.concatenate([selneg, jnp.zeros((LANE - ns, tq), F32)], axis=0)
    sel_ref[...] = selneg.T.astype(sel_ref.dtype)


def nsa_cmp_attention(proj, kvc, ovt, *, B, S, tq):
    M = B * S
    nq = S // tq
    G, dk = NSA_GROUPS, NSA_HEAD_DIM
    ncp = S // CMP_STRIDE
    ns = S // SEL_BLOCK
    assert ns <= LANE
    n_sel = min(SEL_COUNT, ns)
    gw = NSA_HG * dk
    return pl.pallas_call(
        functools.partial(_nsa_cmp_kernel, tq=tq, ncp=ncp, ns=ns, n_sel=n_sel, scale=dk ** -0.5),
        grid=(B, G, nq),
        in_specs=[pl.BlockSpec((tq, gw), lambda b, g, i: (b * nq + i, OFF_NSA_Q // gw + g)),
                  pl.BlockSpec((None, None, None, ncp, dk), lambda b, g, i: (0, b, g, 0, 0)),
                  pl.BlockSpec((None, None, None, ncp, dk), lambda b, g, i: (1, b, g, 0, 0)),
                  pl.BlockSpec((LANE, ncp), lambda b, g, i: (0, 0))],
        out_specs=[pl.BlockSpec((tq, gw), lambda b, g, i: (b * nq + i, g)),
                   pl.BlockSpec((None, None, tq, LANE), lambda b, g, i: (b, g, i, 0))],
        out_shape=[jax.ShapeDtypeStruct((M, G * gw), BF16),
                   jax.ShapeDtypeStruct((B, G, S, LANE), BF16)],
        compiler_params=_cparams(("parallel", "parallel", "parallel")),
        name="nsa_cmp_attention",
    )(proj, kvc, kvc, ovt)


def _nsa_main_kernel(q_ref, sel_ref, ks_ref, vs_ref, kw_ref, vw_ref, ind_ref, bias_ref, gate_ref, oc_ref,
                     o_ref, m_ref, acc_ref, *, tq, nback, scale):
    g = pl.program_id(1)
    qi = pl.program_id(2)
    dk = NSA_HEAD_DIM
    HG = NSA_HG
    ns = NSA_STREAMS
    hp = HG // ns
    st = range(ns)
    stack = lambda f: [jnp.concatenate([f(a * hp + i) for i in range(hp)], axis=0) for a in st]
    q = stack(lambda h: q_ref[:, h * dk:(h + 1) * dk])
    sel = jnp.concatenate([sel_ref[...]] * hp, axis=0)
    qa = [jnp.concatenate([q[a], sel], axis=1) for a in st]
    gate = jax.nn.sigmoid(gate_ref[...].astype(F32))

    for a in range(2 * ns):
        _softmax_init(m_ref.at[a], acc_ref.at[a])

    def steps(s, v, fix, base=0):
        for a in st:
            _softmax_step(fix(a, s[a]), v, m_ref.at[base + a], acc_ref.at[base + a], scale)

    wspan = nback + 1

    def rows_from(j, width):
        return pl.ds(j * tq if isinstance(j, int) else pl.multiple_of(j * tq, tq), width)

    def sel_scores(j, width=tq):
        sl = rows_from(j, width)
        ka = jnp.concatenate([ks_ref[sl, :], ind_ref[sl, :]], axis=1)
        return [_dot_nt(qa[a], ka) for a in st], vs_ref[sl, :]

    def sel_far(j, nc):
        s, v = sel_scores(j, nc * tq)
        steps(s, v, lambda a, x: x)

    _far_chunks(jnp.maximum(qi - 1, 0), sel_far, FAR_GROUP_NSA)

    def win_scores(j, width):
        sl = rows_from(j, width)
        kw = kw_ref[sl, :]
        return [_dot_nt(q[a], kw) for a in st], vw_ref[sl, :]

    def near(first_s, nc_s, first_w, nc_w):
        tile = lambda nc: (lambda a, x: x + bias_ref[a, :, (wspan - nc) * tq:wspan * tq])
        s_s, v_s = sel_scores(first_s, nc_s * tq)
        s_w, v_w = win_scores(first_w, nc_w * tq)
        steps(s_s, v_s, tile(nc_s))
        steps(s_w, v_w, tile(nc_w), base=ns)

    for k in range(wspan - 1):
        @pl.when(qi == k)
        def _(k=k):
            near(max(k - 1, 0), min(k + 1, 2), 0, k + 1)

    @pl.when(qi >= wspan - 1)
    def _():
        near(qi - 1, 2, qi - (wspan - 1), wspan)

    o_s = [_softmax_finish(acc_ref.at[a]) for a in st]
    o_w = [_softmax_finish(acc_ref.at[ns + a]) for a in st]

    lane = lax.broadcasted_iota(jnp.int32, gate.shape, 1)
    for h in range(HG):
        c = g * HG + h
        a, sl = h // hp, slice((h % hp) * tq, (h % hp + 1) * tq)
        gc = [jnp.sum(jnp.where(lane == br * NSA_HEADS + c, gate, 0.0), axis=-1, keepdims=True)
              for br in range(3)]
        o = gc[0] * oc_ref[:, h * dk:(h + 1) * dk].astype(F32) + gc[1] * o_s[a][sl] + gc[2] * o_w[a][sl]
        o_ref[:, h * dk:(h + 1) * dk] = o.astype(o_ref.dtype)


def nsa_main(proj, selneg, ind, btiles, o_c, *, B, S, tq):
    M = B * S
    nq = S // tq
    G, dk, HG = NSA_GROUPS, NSA_HEAD_DIM, NSA_HG
    gw = HG * dk
    assert WINDOW % tq == 0 and WINDOW // tq >= 2
    nback = WINDOW // tq
    ns, hp = NSA_STREAMS, HG // NSA_STREAMS
    kv = lambda n: pl.BlockSpec((S, dk), lambda b, g, i, n=n: (b, (OFF_NSA_KV + n * G * dk) // dk + g))
    return pl.pallas_call(
        functools.partial(_nsa_main_kernel, tq=tq, nback=nback, scale=dk ** -0.5),
        grid=(B, G, nq),
        in_specs=[pl.BlockSpec((tq, gw), lambda b, g, i: (b * nq + i, OFF_NSA_Q // gw + g)),
                  pl.BlockSpec((None, None, tq, LANE), lambda b, g, i: (b, g, i, 0)),
                  kv(2), kv(3), kv(4), kv(5),
                  pl.BlockSpec((S, LANE), lambda b, g, i: (0, 0)),
                  pl.BlockSpec((ns, hp * tq, (nback + 1) * tq), lambda b, g, i: (g, 0, 0)),
                  pl.BlockSpec((tq, LANE), lambda b, g, i: (b * nq + i, OFF_NSA_GATE // LANE)),
                  pl.BlockSpec((tq, gw), lambda b, g, i: (b * nq + i, g))],
        out_specs=pl.BlockSpec((tq, gw), lambda b, g, i: (b * nq + i, g)),
        out_shape=jax.ShapeDtypeStruct((M, G * gw), BF16),
        scratch_shapes=[pltpu.VMEM((2 * ns, hp * tq, LANE), F32),
                        pltpu.VMEM((2 * ns, hp * tq, 2 * LANE), F32)],
        compiler_params=_cparams(("parallel", "parallel", "arbitrary")),
        name="nsa_main",
    )(proj, selneg, proj, proj, proj, proj, ind, btiles, proj, o_c)


def _merge_kernel(x_ref, oa_ref, ob_ref, oc_ref, g0_ref, g1_ref, g2_ref, wb_ref, wo_ref, gain_ref, o_ref):
    merged = None
    for o_r, g_r, br in ((oa_ref, g0_ref, 0), (ob_ref, g1_ref, 1), (oc_ref, g2_ref, 2)):
        t = jax.nn.sigmoid(g_r[...].astype(F32)) * _dot(o_r[...], wb_ref[br])
        merged = t if merged is None else merged + t
    y = _dot(merged.astype(BF16), wo_ref[...])
    o_ref[...] = x_ref[...] + _rms(y, gain_ref[...])


def merge_branches(x, o_a, o_b, o_c, proj, wb, wo, gain, *, tm):
    M, D = x.shape
    row = pl.BlockSpec((tm, D), lambda i: (i, 0))
    gate = lambda br: pl.BlockSpec((tm, D), lambda i, br=br: (i, OFF_BGATE // D + br))
    return pl.pallas_call(
        _merge_kernel,
        grid=(M // tm,),
        in_specs=[row, row, row, row, gate(0), gate(1), gate(2),
                  pl.BlockSpec(wb.shape, lambda i: (0, 0, 0)),
                  pl.BlockSpec(wo.shape, lambda i: (0, 0)),
                  pl.BlockSpec((1, D), lambda i: (0, 0))],
        out_specs=row,
        out_shape=jax.ShapeDtypeStruct((M, D), F32),
        compiler_params=_cparams(("parallel",)),
        name="merge_branches",
    )(x, o_a, o_b, o_c, proj, proj, proj, wb, wo, gain)


def _cross_kernel(x_ref, kv_ref, wq_ref, wo_ref, gpre_ref, gpost_ref, o_ref, *, scale):
    x = x_ref[...]
    h = _rms(x, gpre_ref[...]).astype(BF16)
    q = _dot(h, wq_ref[...]).astype(BF16)
    dh = CROSS_HEAD_DIM
    outs = []
    for hd in range(CROSS_HEADS):
        k = kv_ref[:, hd * dh:(hd + 1) * dh]
        v = kv_ref[:, D_MODEL + hd * dh:D_MODEL + (hd + 1) * dh]
        s = _dot_nt(q[:, hd * dh:(hd + 1) * dh], k) * scale
        p = jnp.exp(s - jnp.max(s, axis=-1, keepdims=True))
        p = p / jnp.sum(p, axis=-1, keepdims=True)
        outs.append(_dot(p.astype(BF16), v).astype(BF16))
    y = _dot(jnp.concatenate(outs, axis=1), wo_ref[...])
    o_ref[...] = x + _rms(y, gpost_ref[...])


def cross_attention(x, kv, wq, wo, gpre, gpost, *, S, tm):
    M, D = x.shape
    mlen = kv.shape[0] // (M // S)
    per_b = S // tm
    return pl.pallas_call(
        functools.partial(_cross_kernel, scale=CROSS_HEAD_DIM ** -0.5),
        grid=(M // tm,),
        in_specs=[pl.BlockSpec((tm, D), lambda i: (i, 0)),
                  pl.BlockSpec((mlen, 2 * D), lambda i: (i // per_b, 0)),
                  pl.BlockSpec(wq.shape, lambda i: (0, 0)),
                  pl.BlockSpec(wo.shape, lambda i: (0, 0)),
                  pl.BlockSpec((1, D), lambda i: (0, 0)),
                  pl.BlockSpec((1, D), lambda i: (0, 0))],
        out_specs=pl.BlockSpec((tm, D), lambda i: (i, 0)),
        out_shape=jax.ShapeDtypeStruct((M, D), F32),
        compiler_params=_cparams(("parallel",)),
        name="cross_attention",
    )(x, kv, wq, wo, gpre, gpost)


def _mlp_kernel(x_ref, w1_ref, w2_ref, gpre_ref, gpost_ref, o_ref, h_ref, acc_ref):
    j = pl.program_id(1)

    @pl.when(j == 0)
    def _():
        h_ref[...] = _rms(x_ref[...], gpre_ref[...]).astype(BF16)
        acc_ref[...] = jnp.zeros(acc_ref.shape, F32)

    a = jnp.maximum(_dot(h_ref[...], w1_ref[...]), 0.0)
    acc_ref[...] += _dot((a * a).astype(BF16), w2_ref[...])

    @pl.when(j == pl.num_programs(1) - 1)
    def _():
        o_ref[...] = x_ref[...] + _rms(acc_ref[...], gpost_ref[...])


def mlp(x, w1, w2, gpre, gpost, *, tm, tf):
    M, D = x.shape
    FF = w1.shape[1]
    return pl.pallas_call(
        _mlp_kernel,
        grid=(M // tm, FF // tf),
        in_specs=[pl.BlockSpec((tm, D), lambda i, j: (i, 0)),
                  pl.BlockSpec((D, tf), lambda i, j: (0, j)),
                  pl.BlockSpec((tf, D), lambda i, j: (j, 0)),
                  pl.BlockSpec((1, D), lambda i, j: (0, 0)),
                  pl.BlockSpec((1, D), lambda i, j: (0, 0))],
        out_specs=pl.BlockSpec((tm, D), lambda i, j: (i, 0)),
        out_shape=jax.ShapeDtypeStruct((M, D), F32),
        scratch_shapes=[pltpu.VMEM((tm, D), BF16), pltpu.VMEM((tm, D), F32)],
        compiler_params=_cparams(("parallel", "arbitrary")),
        name="mlp",
    )(x, w1, w2, gpre, gpost)


def _rot_half_cols(w):
    half = w.shape[-1] // 2
    return jnp.concatenate([-w[..., half:], w[..., :half]], axis=-1)


def _pack_w_in(w):
    widths = (MLA_Q_LORA, MLA_KV_LORA, MLA_ROPE, 1024, 1024, 1024, IDX_HEADS * IDX_DIM, IDX_DIM, IDX_HEADS,
              1024, 256, 256, 256, 256, 256, 256, 3 * NSA_HEADS, N_BRANCH * D_MODEL)
    offs = np.concatenate([[0], np.cumsum(widths)])
    seg = [w[:, offs[i]:offs[i + 1]] for i in range(len(widths))]
    (q_lat, kv_lat, k_rope, dsa_q, dsa_k, dsa_v, idx_q, idx_k, idx_w,
     nsa_q, kc, vc, ks, vs, kw, vw, nsa_gate, bgate) = seg
    z = lambda n: jnp.zeros((w.shape[0], n), w.dtype)
    cols = [dsa_q, dsa_k, dsa_v, nsa_q, bgate, idx_q, kc, vc, ks, vs, kw, vw,
            q_lat, kv_lat, k_rope, z(LANE - MLA_ROPE), _rot_half_cols(k_rope), z(LANE - MLA_ROPE), z(LANE),
            idx_k, idx_k, idx_w, z(LANE - IDX_HEADS), nsa_gate, z(LANE - 3 * NSA_HEADS)]
    out = jnp.concatenate(cols, axis=1)
    out = jnp.concatenate([out, z(N_PACK - out.shape[1])], axis=1)
    return out.astype(BF16)


def _pack_mla_weights(w_uq, w_ukv):
    H = MLA_HEADS
    uq = w_uq.reshape(MLA_Q_LORA, H, MLA_NOPE + MLA_ROPE)
    nope = uq[:, :, :MLA_NOPE].reshape(MLA_Q_LORA, H * MLA_NOPE)
    rope = uq[:, :, MLA_NOPE:]
    pad = jnp.zeros((MLA_Q_LORA, H, LANE - MLA_ROPE), w_uq.dtype)
    wqr = jnp.concatenate([rope, pad], axis=-1).reshape(MLA_Q_LORA, H * LANE)
    wqrr = jnp.concatenate([_rot_half_cols(rope), pad], axis=-1).reshape(MLA_Q_LORA, H * LANE)
    ukv = w_ukv.reshape(MLA_KV_LORA, H, MLA_NOPE + MLA_V)
    wkv = jnp.concatenate([ukv[:, :, :MLA_NOPE].reshape(MLA_KV_LORA, H * MLA_NOPE),
                           ukv[:, :, MLA_NOPE:].reshape(MLA_KV_LORA, H * MLA_V)], axis=1)
    return nope.astype(BF16), wqr.astype(BF16), wqrr.astype(BF16), wkv.astype(BF16)


def _constants(S):
    half = MLA_ROPE // 2
    inv_freq = ROPE_BASE ** (-np.arange(0, MLA_ROPE, 2, dtype=np.float32) / MLA_ROPE)
    freq = np.zeros((1, LANE), np.float32)
    freq[0, :half] = inv_freq
    freq[0, half:2 * half] = inv_freq
    ncp = S // CMP_STRIDE
    ns = S // SEL_BLOCK
    c_start = np.arange(ncp) * CMP_STRIDE
    j_start = np.arange(ns) * SEL_BLOCK
    overlap = ((c_start[None, :] < j_start[:, None] + SEL_BLOCK) &
               (c_start[None, :] + CMP_LEN > j_start[:, None])).astype(np.float32)
    overlap[:, ncp - 1] = 0.0
    ovt = np.zeros((LANE, ncp), np.float32)
    ovt[:ns] = overlap
    ind = np.zeros((S, LANE), np.float32)
    ind[np.arange(S), np.arange(S) // SEL_BLOCK] = 1.0
    return jnp.asarray(freq), jnp.asarray(ovt, BF16), jnp.asarray(ind, BF16)


def kernel(x, mem, positions, rel_bias, norm_gains, w_in, mla_q_norm, mla_kv_norm, mla_w_uq, mla_w_ukv,
           nsa_cmp_pos, nsa_cmp_w1, nsa_cmp_w2, w_branch, w_out, cross_wq, cross_wkv, cross_wo, mlp_w1, mlp_w2):
    B, S, D = x.shape
    M = B * S
    depth = w_in.shape[0]
    tq = TQ if S % TQ == 0 else TQ_NSA
    tqn = TQ_NSA
    assert S % tq == 0 and S % tqn == 0 and D == D_MODEL
    topk = min(DSA_TOPK_MAX, S // 4)
    freq, ovt, ind = _constants(S)
    btiles = bias_tiles(rel_bias, tq, 0, DSA_HEADS, nspan=2, masked=False, stack=1)
    btiles_n = bias_tiles(rel_bias, tqn, DSA_HEADS, NSA_HEADS, nspan=WINDOW // tqn + 1, masked=True,
                          stack=NSA_HG // NSA_STREAMS)
    pos = positions.reshape(M, 1)
    xf = x.reshape(M, D)
    memf = mem.reshape(B * mem.shape[1], D)
    tm_big = 1024 if M % 1024 == 0 else tq
    tm = 512 if M % 512 == 0 else tq
    ncp = S // CMP_STRIDE
    row = lambda v: v.reshape(1, -1)

    for l in range(depth):
        g = norm_gains[l]
        proj = norm_matmul(xf, row(g[0]), _pack_w_in(w_in[l]), tm=tm_big, tn=TN_IN)

        wqn, wqr, wqrr, wkv = _pack_mla_weights(mla_w_uq[l], mla_w_ukv[l])
        qn, qr, kn, vv, kr = mla_prep(proj, pos, freq, row(mla_q_norm[l]), row(mla_kv_norm[l]),
                                      wqn, wqr, wqrr, wkv, tm=tm_big)
        o_a = mla_attention(qn, qr, kn, kr, vv, B=B, S=S, tq=tq)

        maskb = dsa_index_mask(proj, B=B, S=S, tqi=TQ_IDX, tk=tq, topk=topk)
        o_b = dsa_attention(proj, maskb, btiles, B=B, S=S, tq=tq)

        kvc_in = jnp.stack([
            proj[:, OFF_NSA_KV + a * 256:OFF_NSA_KV + (a + 1) * 256].reshape(B * ncp, CMP_STRIDE * 256)
            for a in range(2)])
        posflat = jnp.broadcast_to(nsa_cmp_pos[l].reshape(2, 1, CMP_LEN * NSA_HEAD_DIM),
                                   (2, 8, CMP_LEN * NSA_HEAD_DIM)).astype(BF16)
        kvc = nsa_compress(kvc_in, posflat, nsa_cmp_w1[l].astype(BF16), nsa_cmp_w2[l].astype(BF16), B=B, S=S)
        o_cmp, selneg = nsa_cmp_attention(proj, kvc, ovt, B=B, S=S, tq=TQ_CMP if S % TQ_CMP == 0 else tq)
        o_c = nsa_main(proj, selneg, ind, btiles_n, o_cmp, B=B, S=S, tq=tqn)

        xf = merge_branches(xf, o_a, o_b, o_c, proj, w_branch[l].astype(BF16), w_out[l].astype(BF16),
                            row(g[1]), tm=tm)

        mkv = norm_matmul(memf, row(g[3]), cross_wkv[l].astype(BF16), tm=memf.shape[0] // B, tn=1024)
        xf = cross_attention(xf, mkv, cross_wq[l].astype(BF16), cross_wo[l].astype(BF16),
                             row(g[2]), row(g[4]), S=S, tm=tm_big)

        xf = mlp(xf, mlp_w1[l].astype(BF16), mlp_w2[l].astype(BF16), row(g[5]), row(g[6]), tm=tm_big, tf=1024)

    return xf.reshape(B, S, D)
```

```python
import functools
import math

import numpy as np
import jax
import jax.numpy as jnp
from jax import lax
from jax.experimental import pallas as pl
from jax.experimental.pallas import tpu as pltpu

F32 = jnp.float32
BF16 = jnp.bfloat16

D_MODEL = 1024
EPS = 1e-6
MLA_HEADS = 8
MLA_Q_LORA = 384
MLA_KV_LORA = 256
MLA_NOPE = 128
MLA_ROPE = 64
MLA_V = 128
ROPE_BASE = 10000.0
DSA_HEADS = 8
DSA_HEAD_DIM = 128
IDX_HEADS = 8
IDX_DIM = 64
DSA_TOPK_MAX = 256
NSA_HEADS = 8
NSA_GROUPS = 2
NSA_HG = NSA_HEADS // NSA_GROUPS
NSA_HEAD_DIM = 128
CMP_STRIDE = 16
CMP_LEN = 2 * CMP_STRIDE
CMP_HIDDEN = 256
SEL_BLOCK = 64
SEL_COUNT = 16
WINDOW = 512
REL_BUCKETS = 32
REL_MAX_DIST = 128
REL_HEADS = DSA_HEADS + NSA_HEADS
CROSS_HEADS = 4
CROSS_HEAD_DIM = D_MODEL // CROSS_HEADS
D_FF = 4 * D_MODEL
N_BRANCH = 3

LANE = 128
NEG = -1e30
VMEM_LIMIT = 48 * 1024 * 1024

TQ = 512
TQ_NSA = 256
TQ_CMP = 1024
HPS = 2
FAR_GROUP = 4
FAR_GROUP_NSA = 8
NSA_STREAMS = 2
BISECT_ITERS = 32
IDX_TIE_CHECK = 16
TQ_IDX = 512
IDX_CHUNK = 256

OFF_DSA_Q = 0
OFF_DSA_K = 1024
OFF_DSA_V = 2048
OFF_NSA_Q = 3072
OFF_BGATE = 4096
OFF_IDX_Q = 7168
OFF_NSA_KV = 7680
OFF_MLA = 9216
OFF_IDX_K = 10240
OFF_IDX_W = 10368
OFF_NSA_GATE = 10496
N_PACK = 10752
TN_IN = 2688


def _cparams(sem):
    return pltpu.CompilerParams(dimension_semantics=sem, vmem_limit_bytes=VMEM_LIMIT)


def _rms(x, g):
    return x * lax.rsqrt(jnp.mean(x * x, axis=-1, keepdims=True) + EPS) * g


def _dot(a, b):
    return jnp.dot(a, b, preferred_element_type=F32)


def _dot_nt(a, b):
    return lax.dot_general(a, b, (((1,), (1,)), ((), ())), preferred_element_type=F32)


def _norm_mm_kernel(x_ref, g_ref, w_ref, o_ref, xn_ref):
    @pl.when(pl.program_id(1) == 0)
    def _():
        xn_ref[...] = _rms(x_ref[...].astype(F32), g_ref[...]).astype(BF16)

    o_ref[...] = _dot(xn_ref[...], w_ref[...]).astype(o_ref.dtype)


def norm_matmul(x, g, w, *, tm, tn):
    M, K = x.shape
    N = w.shape[1]
    return pl.pallas_call(
        _norm_mm_kernel,
        grid=(M // tm, N // tn),
        in_specs=[pl.BlockSpec((tm, K), lambda i, j: (i, 0)),
                  pl.BlockSpec((1, K), lambda i, j: (0, 0)),
                  pl.BlockSpec((K, tn), lambda i, j: (0, j))],
        out_specs=pl.BlockSpec((tm, tn), lambda i, j: (i, j)),
        out_shape=jax.ShapeDtypeStruct((M, N), BF16),
        scratch_shapes=[pltpu.VMEM((tm, K), BF16)],
        compiler_params=_cparams(("parallel", "arbitrary")),
        name="norm_matmul",
    )(x, g, w)


def _bias_tiles_kernel(tab_ref, bkt_ref, o_ref, *, inv_scale, h0, tq, nspan, masked):
    h = pl.program_id(0) + h0
    bkt = bkt_ref[...]
    far = tab_ref[REL_BUCKETS - 1, h]
    out = jnp.zeros(bkt.shape, F32)
    for b in range(REL_BUCKETS - 1):
        out = jnp.where(bkt == b, (tab_ref[b, h] - far) * inv_scale, out)
    diag, prev = out[0], out[1]
    r, c = _tri(tq, tq)
    if masked:
        diag = jnp.where(r >= c, diag, NEG)
    o_ref[:, (nspan - 1) * tq:nspan * tq] = diag
    o_ref[:, (nspan - 2) * tq:(nspan - 1) * tq] = prev
    for k in range(nspan - 2):
        edge = jnp.where(r < c, 0.0, NEG) if (masked and k == 0) else jnp.zeros((tq, tq), F32)
        o_ref[:, k * tq:(k + 1) * tq] = edge


def _bucket_tiles(tq):
    i = np.arange(tq)[:, None]
    j = np.arange(tq)[None, :]
    d = np.stack([i - j, tq + i - j]).astype(np.int32)
    n = np.maximum(d, 0)
    exact = REL_BUCKETS // 2
    nf = np.maximum(n, 1).astype(np.float32)
    ratio = np.log(nf / np.float32(exact)) / np.float32(math.log(REL_MAX_DIST / exact))
    log_b = exact + (ratio * np.float32(REL_BUCKETS - exact)).astype(np.int32)
    return jnp.asarray(np.where(n < exact, n, np.minimum(log_b, REL_BUCKETS - 1)).astype(np.int32))


def bias_tiles(rel_bias, tq, h0, nh, *, nspan, masked, stack):
    assert tq >= REL_MAX_DIST
    bkt = _bucket_tiles(tq)
    assert DSA_HEAD_DIM == NSA_HEAD_DIM
    return pl.pallas_call(
        functools.partial(_bias_tiles_kernel, inv_scale=DSA_HEAD_DIM ** 0.5, h0=h0, tq=tq, nspan=nspan,
                          masked=masked),
        grid=(nh,),
        in_specs=[pl.BlockSpec(memory_space=pltpu.SMEM),
                  pl.BlockSpec((2, tq, tq), lambda h: (0, 0, 0))],
        out_specs=pl.BlockSpec((None, tq, nspan * tq), lambda h: (h // stack, h % stack, 0)),
        out_shape=jax.ShapeDtypeStruct((nh // stack, stack * tq, nspan * tq), F32),
        compiler_params=_cparams(("arbitrary",)),
        name="bias_tiles",
    )(rel_bias, bkt)


def _mla_prep_kernel(lat_ref, pos_ref, freq_ref, qg_ref, kvg_ref, wqn_ref, wqr_ref, wqrr_ref, wkv_ref,
                     qn_ref, qr_ref, kn_ref, v_ref, kr_ref):
    lat = lat_ref[...].astype(F32)
    qlat = _rms(lat[:, :MLA_Q_LORA], qg_ref[...]).astype(BF16)
    kvlat = _rms(lat[:, MLA_Q_LORA:MLA_Q_LORA + MLA_KV_LORA], kvg_ref[...]).astype(BF16)
    o = MLA_Q_LORA + MLA_KV_LORA
    kr = lat[:, o:o + LANE]
    krr = lat[:, o + LANE:o + 2 * LANE]
    ang = pos_ref[...].astype(F32) * freq_ref[...]
    cos, sin = jnp.cos(ang), jnp.sin(ang)
    cos8 = jnp.concatenate([cos] * MLA_HEADS, axis=1)
    sin8 = jnp.concatenate([sin] * MLA_HEADS, axis=1)
    qn_ref[...] = _dot(qlat, wqn_ref[...]).astype(BF16)
    qr_ref[...] = (_dot(qlat, wqr_ref[...]) * cos8 + _dot(qlat, wqrr_ref[...]) * sin8).astype(BF16)
    kv = _dot(kvlat, wkv_ref[...])
    nk = MLA_HEADS * MLA_NOPE
    kn_ref[...] = kv[:, :nk].astype(BF16)
    v_ref[...] = kv[:, nk:].astype(BF16)
    kr_ref[...] = (kr * cos + krr * sin).astype(BF16)


def mla_prep(proj, pos, freq, qg, kvg, wqn, wqr, wqrr, wkv, *, tm):
    M = proj.shape[0]
    HD = MLA_HEADS * LANE
    full = lambda a: pl.BlockSpec(a.shape, lambda i: (0, 0))
    outs = [jax.ShapeDtypeStruct((M, HD), BF16)] * 4 + [jax.ShapeDtypeStruct((M, LANE), BF16)]
    return pl.pallas_call(
        _mla_prep_kernel,
        grid=(M // tm,),
        in_specs=[pl.BlockSpec((tm, 1024), lambda i: (i, OFF_MLA // 1024)),
                  pl.BlockSpec((tm, 1), lambda i: (i, 0)),
                  full(freq), full(qg), full(kvg), full(wqn), full(wqr), full(wqrr), full(wkv)],
        out_specs=[pl.BlockSpec((tm, HD), lambda i: (i, 0))] * 4 + [pl.BlockSpec((tm, LANE), lambda i: (i, 0))],
        out_shape=outs,
        compiler_params=_cparams(("parallel",)),
        name="mla_prep",
    )(proj, pos, freq, qg, kvg, wqn, wqr, wqrr, wkv)


def _softmax_init(m_ref, acc_ref):
    m_ref[...] = jnp.full(m_ref.shape, NEG, F32)
    acc_ref[...] = jnp.zeros(acc_ref.shape, F32)


def _with_ones(v):
    return jnp.concatenate([v, jnp.ones(v.shape, v.dtype)], axis=1)


def _softmax_step(s, v, m_ref, acc_ref, scale):
    c = scale * math.log2(math.e)
    m_prev = m_ref[...]
    m_new = jnp.maximum(m_prev, jnp.max(s, axis=-1, keepdims=True))
    alpha = jnp.exp2((m_prev - m_new) * c)
    p = jnp.exp2((s - jnp.tile(m_new, (1, s.shape[1] // LANE))) * c)
    acc_ref[...] = jnp.tile(alpha, (1, 2)) * acc_ref[...] + _dot(p.astype(BF16), _with_ones(v))
    m_ref[...] = m_new


def _softmax_finish(acc_ref):
    acc = acc_ref[...]
    return acc[:, :LANE] / acc[:, LANE:]


def _far_chunks(n, step, largest):
    def body(p, c):
        step(largest * p, largest)
        return c

    lax.fori_loop(0, n // largest, body, 0)
    w = largest // 2
    while w >= 1:
        @pl.when(n % (2 * w) >= w)
        def _(w=w):
            step(n // (2 * w) * (2 * w), w)
        w //= 2


def _tri(tq, tk):
    row = lax.broadcasted_iota(jnp.int32, (tq, tk), 0)
    col = lax.broadcasted_iota(jnp.int32, (tq, tk), 1)
    return row, col


def _mla_attn_kernel(qn_ref, qr_ref, kn_ref, kr_ref, v_ref, o_ref, m_ref, acc_ref, *, tq, scale):
    qi = pl.program_id(2)
    hs = range(HPS)
    q = [jnp.concatenate([qn_ref[:, h * LANE:(h + 1) * LANE], qr_ref[:, h * LANE:(h + 1) * LANE]], axis=-1)
         for h in hs]
    for h in hs:
        _softmax_init(m_ref.at[h], acc_ref.at[h])

    def step(j, nc, causal_last):
        sl = pl.ds(j * tq if isinstance(j, int) else pl.multiple_of(j * tq, tq), nc * tq)
        kr = kr_ref[sl, :]
        if causal_last:
            row, col = _tri(tq, nc * tq)
            ok = col <= row + (nc - 1) * tq
        s = [_dot_nt(q[h], jnp.concatenate([kn_ref[sl, h * LANE:(h + 1) * LANE], kr], axis=-1)) for h in hs]
        for h in hs:
            sh = jnp.where(ok, s[h], NEG) if causal_last else s[h]
            _softmax_step(sh, v_ref[sl, h * LANE:(h + 1) * LANE], m_ref.at[h], acc_ref.at[h], scale)

    _far_chunks(jnp.maximum(qi - 1, 0), lambda j, nc: step(j, nc, False), FAR_GROUP)

    @pl.when(qi == 0)
    def _():
        step(0, 1, True)

    @pl.when(qi >= 1)
    def _():
        step(qi - 1, 2, True)

    for h in hs:
        o_ref[:, h * LANE:(h + 1) * LANE] = _softmax_finish(acc_ref.at[h]).astype(o_ref.dtype)


def mla_attention(qn, qr, kn, kr, v, *, B, S, tq):
    M = B * S
    nq = S // tq
    H = MLA_HEADS
    scale = (MLA_NOPE + MLA_ROPE) ** -0.5
    w = HPS * LANE
    qspec = pl.BlockSpec((tq, w), lambda b, h, i: (b * nq + i, h))
    kspec = pl.BlockSpec((S, w), lambda b, h, i: (b, h))
    return pl.pallas_call(
        functools.partial(_mla_attn_kernel, tq=tq, scale=scale),
        grid=(B, H // HPS, nq),
        in_specs=[qspec, qspec, kspec, pl.BlockSpec((S, LANE), lambda b, h, i: (b, 0)), kspec],
        out_specs=qspec,
        out_shape=jax.ShapeDtypeStruct((M, H * MLA_V), BF16),
        scratch_shapes=[pltpu.VMEM((HPS, tq, LANE), F32), pltpu.VMEM((HPS, tq, 2 * LANE), F32)],
        compiler_params=_cparams(("parallel", "parallel", "arbitrary")),
        name="mla_attention",
    )(qn, qr, kn, kr, v)


def _dsa_index_kernel(iq_ref, ik_ref, iw_ref, tri_ref, mb_ref, sc_ref, *, tq, tk, S, topk, iters):
    t0 = pl.program_id(1) * tq
    nk = S // tk
    nch = (t0 + tq - 1) // tk + 1
    hw = IDX_CHUNK
    iq = iq_ref[...]
    lane = lax.broadcasted_iota(jnp.int32, (tq, LANE), 1)
    qs = []
    for h in range(IDX_HEADS):
        blk = iq[:, (h // 2) * LANE:(h // 2 + 1) * LANE]
        keep = (lane >= IDX_DIM) if h % 2 else (lane < IDX_DIM)
        qs.append(jnp.where(keep, blk, jnp.zeros_like(blk)))
    q8 = jnp.concatenate(qs, axis=0)
    iwt = iw_ref[...].astype(F32).T
    wrow = [iwt[h:h + 1, :] for h in range(IDX_HEADS)]
    qpos = lax.broadcasted_iota(jnp.int32, (hw, tq), 1) + t0
    krow = lax.broadcasted_iota(jnp.int32, (hw, tq), 0)

    def fold(x, op):
        out = x[0:8]
        for r in range(1, x.shape[0] // 8):
            out = op(out, x[r * 8:(r + 1) * 8])
        return out

    def score_chunk(c, carry):
        mn, mx = carry
        for half in range(tk // hw):
            k0 = pl.multiple_of(c * tk + half * hw, hw)
            res = _dot_nt(ik_ref[pl.ds(k0, hw), :], q8)
            sc = wrow[0] * jnp.maximum(res[:, 0:tq], 0.0)
            for h in range(1, IDX_HEADS):
                sc = sc + wrow[h] * jnp.maximum(res[:, h * tq:(h + 1) * tq], 0.0)
            causal = (krow + k0) <= qpos
            mn = jnp.minimum(mn, fold(jnp.where(causal, sc, -NEG), jnp.minimum))
            sc = jnp.where(causal, sc, NEG)
            mx = jnp.maximum(mx, fold(sc, jnp.maximum))
            sc_ref[c, half * hw:(half + 1) * hw, :] = sc
        return mn, mx

    mn, mx = lax.fori_loop(0, nch, score_chunk, (jnp.full((8, tq), -NEG, F32), jnp.full((8, tq), NEG, F32)))
    n_causal = (lax.broadcasted_iota(jnp.int32, (1, tq), 1) + (t0 + 1)).astype(F32)
    kf = jnp.minimum(float(topk), n_causal)

    def count(pred):
        def body(c, acc):
            return acc + fold(jnp.where(pred(sc_ref[c]), 1.0, 0.0), jnp.add)

        return jnp.sum(lax.fori_loop(0, nch, body, jnp.zeros((8, tq), F32)), axis=0, keepdims=True)

    lo0 = jnp.min(mn, axis=0, keepdims=True)
    mx = jnp.max(mx, axis=0, keepdims=True)
    hi0 = mx + (jnp.abs(mx) * 1e-6 + 1.0)

    def bisect(c):
        lo, hi, n_lo, n_hi = c
        mid = 0.5 * (lo + hi)
        n_mid = count(lambda blk: blk >= mid)
        ge = n_mid >= kf
        return (jnp.where(ge, mid, lo), jnp.where(ge, hi, mid),
                jnp.where(ge, n_mid, n_lo), jnp.where(ge, n_hi, n_mid))

    def open_rows(n_lo, n_hi, settled):
        return (n_lo - kf) * jnp.where(n_lo - n_hi > 2.0, 1.0, 0.0) * (1.0 - settled)

    def search(state, limit, settled):
        def cond(c):
            return jnp.logical_and(c[0] < limit, jnp.max(open_rows(c[3], c[4], settled)) > 0.0)

        def body(c):
            return (c[0] + 2,) + bisect(bisect(c[1:]))

        return lax.while_loop(cond, body, state)

    def band_min_max(lo, hi):
        def body(c, carry):
            blk = sc_ref[c]
            inside = (blk >= lo) & (blk < hi)
            return (jnp.minimum(carry[0], fold(jnp.where(inside, blk, -NEG), jnp.minimum)),
                    jnp.maximum(carry[1], fold(jnp.where(inside, blk, NEG), jnp.maximum)))

        bmin, bmax = lax.fori_loop(0, nch, body, (jnp.full((8, tq), -NEG, F32), jnp.full((8, tq), NEG, F32)))
        return jnp.min(bmin, axis=0, keepdims=True), jnp.max(bmax, axis=0, keepdims=True)

    def finish(state):
        _, lo, hi, n_lo, n_hi = state
        over = n_lo > kf
        bmin, bmax = lax.cond(jnp.max(n_lo - kf) > 0.0, lambda: band_min_max(lo, hi), lambda: (lo, lo))
        single = bmin == bmax
        split = over & (n_lo - n_hi == 2.0) & jnp.logical_not(single)
        unsplit = over & jnp.logical_not(split)
        return jnp.where(split, bmax, lo), jnp.where(over & single, 1.0, 0.0), jnp.where(unsplit, 1.0, 0.0)

    zero = jnp.zeros((1, tq), F32)
    state1 = search((jnp.int32(0), lo0, hi0, n_causal, zero), IDX_TIE_CHECK, zero)
    first = finish(state1)
    state = search(state1, iters, first[1])
    lo, _, unsplit = lax.cond(state[0] > state1[0], lambda: finish(state), lambda: first)
    hi, n_hi = state[2], state[4]
    tied = jnp.max(unsplit) > 0.0

    def fill(j, c):
        mb_ref[j] = jnp.full((tq, tk), NEG, mb_ref.dtype)
        return c

    lax.fori_loop(nch, nk, fill, 0)

    def emit(j, keep):
        mb_ref[j] = jnp.where(keep, 0.0, NEG).T.astype(mb_ref.dtype)

    @pl.when(jnp.logical_not(tied))
    def _():
        def body(j, c):
            emit(j, sc_ref[j] >= lo)
            return c

        lax.fori_loop(0, nch, body, 0)

    @pl.when(tied)
    def _():
        need = kf - n_hi
        tri = tri_ref[...]

        def body(j, base):
            sc = sc_ref[j]
            above = sc >= hi
            band = (sc >= lo) & jnp.logical_not(above)
            ones = jnp.where(band, 1.0, 0.0)
            after = base + jnp.sum(fold(ones, jnp.add), axis=0, keepdims=True)
            crossing = jnp.max(jnp.where((base < need) & (after > need), 1.0, 0.0)) > 0.0

            @pl.when(crossing)
            def _():
                prefix = _dot(tri, ones.astype(BF16)) + base
                emit(j, above | (band & (prefix <= need)))

            @pl.when(jnp.logical_not(crossing))
            def _():
                emit(j, above | (band & (after <= need)))

            return after

        lax.fori_loop(0, nch, body, zero)


def dsa_index_mask(proj, *, B, S, tqi, tk, topk):
    nq = S // tqi
    nk = S // tk
    assert tk % tqi == 0 and tk % IDX_CHUNK == 0
    tri = jnp.asarray(np.tril(np.ones((tk, tk), np.float32)), BF16)
    return pl.pallas_call(
        functools.partial(_dsa_index_kernel, tq=tqi, tk=tk, S=S, topk=topk, iters=BISECT_ITERS),
        grid=(B, nq),
        in_specs=[pl.BlockSpec((tqi, 512), lambda b, i: (b * nq + i, OFF_IDX_Q // 512)),
                  pl.BlockSpec((S, LANE), lambda b, i: (b, OFF_IDX_K // LANE)),
                  pl.BlockSpec((tqi, LANE), lambda b, i: (b * nq + i, OFF_IDX_W // LANE)),
                  pl.BlockSpec((tk, tk), lambda b, i: (0, 0))],
        out_specs=pl.BlockSpec((None, nk, tqi, tk), lambda b, i: (b, 0, i, 0)),
        out_shape=jax.ShapeDtypeStruct((B, nk, S, tk), BF16),
        scratch_shapes=[pltpu.VMEM((nk, tk, tqi), F32)],
        compiler_params=_cparams(("parallel", "parallel")),
        name="dsa_index_mask",
    )(proj, proj, proj, tri)


def _dsa_attn_kernel(q_ref, k_ref, v_ref, mb_ref, bias_ref, o_ref, m_ref, acc_ref, *, tq, scale):
    qi = pl.program_id(2)
    hs = range(HPS)
    cols = lambda h: slice(h * LANE, (h + 1) * LANE)
    q = [q_ref[:, cols(h)] for h in hs]
    for h in hs:
        _softmax_init(m_ref.at[h], acc_ref.at[h])

    def step(j, nc, bias):
        sl = pl.ds(j * tq if isinstance(j, int) else pl.multiple_of(j * tq, tq), nc * tq)
        mb = mb_ref[j] if nc == 1 else jnp.concatenate([mb_ref[j + i] for i in range(nc)], axis=1)
        mb = mb.astype(F32)
        s = [_dot_nt(q[h], k_ref[sl, cols(h)]) + mb for h in hs]
        for h in hs:
            sh = s[h] if bias is None else s[h] + bias(h)
            _softmax_step(sh, v_ref[sl, cols(h)], m_ref.at[h], acc_ref.at[h], scale)

    _far_chunks(jnp.maximum(qi - 1, 0), lambda j, nc: step(j, nc, None), FAR_GROUP)

    @pl.when(qi == 0)
    def _():
        step(0, 1, lambda h: bias_ref[h, :, tq:2 * tq])

    @pl.when(qi >= 1)
    def _():
        step(qi - 1, 2, lambda h: bias_ref[h])

    for h in hs:
        o_ref[:, cols(h)] = _softmax_finish(acc_ref.at[h]).astype(o_ref.dtype)


def dsa_attention(proj, maskb, btiles, *, B, S, tq):
    M = B * S
    nq = S // tq
    H = DSA_HEADS
    scale = DSA_HEAD_DIM ** -0.5
    w = HPS * LANE
    return pl.pallas_call(
        functools.partial(_dsa_attn_kernel, tq=tq, scale=scale),
        grid=(B, H // HPS, nq),
        in_specs=[pl.BlockSpec((tq, w), lambda b, h, i: (b * nq + i, OFF_DSA_Q // w + h)),
                  pl.BlockSpec((S, w), lambda b, h, i: (b, OFF_DSA_K // w + h)),
                  pl.BlockSpec((S, w), lambda b, h, i: (b, OFF_DSA_V // w + h)),
                  pl.BlockSpec((None, nq, tq, tq), lambda b, h, i: (b, 0, i, 0)),
                  pl.BlockSpec((HPS, tq, 2 * tq), lambda b, h, i: (h, 0, 0))],
        out_specs=pl.BlockSpec((tq, w), lambda b, h, i: (b * nq + i, h)),
        out_shape=jax.ShapeDtypeStruct((M, H * DSA_HEAD_DIM), BF16),
        scratch_shapes=[pltpu.VMEM((HPS, tq, LANE), F32), pltpu.VMEM((HPS, tq, 2 * LANE), F32)],
        compiler_params=_cparams(("parallel", "parallel", "arbitrary")),
        name="dsa_attention",
    )(proj, proj, proj, maskb, btiles)


def _nsa_compress_kernel(x_ref, pos_ref, w1_ref, w2_ref, o_ref, *, ncp):
    dk = NSA_HEAD_DIM
    posw = _dot(pos_ref[...], w1_ref[...])[0:1, :]
    for g in range(NSA_GROUPS):
        lo = jnp.zeros((ncp, CMP_HIDDEN), F32)
        hi = jnp.zeros((ncp, CMP_HIDDEN), F32)
        for l in range(CMP_STRIDE):
            xs = x_ref[:, (l * NSA_GROUPS + g) * dk:(l * NSA_GROUPS + g + 1) * dk]
            lo = lo + _dot(xs, w1_ref[l * dk:(l + 1) * dk, :])
            hi = hi + _dot(xs, w1_ref[(CMP_STRIDE + l) * dk:(CMP_STRIDE + l + 1) * dk, :])
        hid = lo + pltpu.roll(hi, ncp - 1, 0) + posw
        o_ref[g] = _dot(jax.nn.gelu(hid).astype(BF16), w2_ref[...]).astype(o_ref.dtype)


def nsa_compress(xkv, posflat, w1, w2, *, B, S):
    ncp = S // CMP_STRIDE
    G, dk = NSA_GROUPS, NSA_HEAD_DIM
    width = CMP_STRIDE * G * dk
    return pl.pallas_call(
        functools.partial(_nsa_compress_kernel, ncp=ncp),
        grid=(2, B),
        in_specs=[pl.BlockSpec((None, ncp, width), lambda a, b: (a, b, 0)),
                  pl.BlockSpec((None, 8, CMP_LEN * dk), lambda a, b: (a, 0, 0)),
                  pl.BlockSpec((None, CMP_LEN * dk, CMP_HIDDEN), lambda a, b: (a, 0, 0)),
                  pl.BlockSpec((None, CMP_HIDDEN, dk), lambda a, b: (a, 0, 0))],
        out_specs=pl.BlockSpec((None, None, G, ncp, dk), lambda a, b: (a, b, 0, 0, 0)),
        out_shape=jax.ShapeDtypeStruct((2, B, G, ncp, dk), BF16),
        compiler_params=_cparams(("parallel", "parallel")),
        name="nsa_compress",
    )(xkv, posflat, w1, w2)


def _nsa_cmp_kernel(q_ref, kc_ref, vc_ref, ov_ref, oc_ref, sel_ref, *, tq, ncp, ns, n_sel, scale):
    t0 = pl.program_id(2) * tq
    dk = NSA_HEAD_DIM
    kc = kc_ref[...]
    vc = vc_ref[...]
    trow = lax.broadcasted_iota(jnp.int32, (tq, ncp), 0) + t0
    ccol = lax.broadcasted_iota(jnp.int32, (tq, ncp), 1)
    vis = (ccol * CMP_STRIDE + (CMP_LEN - 1)) <= trow
    psum = jnp.zeros((tq, ncp), F32)
    for h in range(NSA_HG):
        s = jnp.where(vis, _dot_nt(q_ref[:, h * dk:(h + 1) * dk], kc) * scale, NEG)
        m = jnp.max(s, axis=-1, keepdims=True)
        p = jnp.where(vis, jnp.exp(s - m), 0.0)
        d = jnp.sum(p, axis=-1, keepdims=True)
        p = p / jnp.where(d > 0, d, 1.0)
        oc_ref[:, h * dk:(h + 1) * dk] = _dot(p.astype(BF16), vc).astype(oc_ref.dtype)
        psum = psum + p
    p_hi = psum.astype(BF16)
    p_lo = (psum - p_hi.astype(F32)).astype(BF16)
    ov = ov_ref[...]
    imp = _dot_nt(ov, p_hi) + _dot_nt(ov, p_lo)
    imp = imp[:ns]
    jrow = lax.broadcasted_iota(jnp.int32, (ns, tq), 0)
    tcol = lax.broadcasted_iota(jnp.int32, (ns, tq), 1) + t0
    blk_t = tcol // SEL_BLOCK
    forced = (jrow == 0) | (jrow == blk_t) | (jrow == blk_t - 1)
    val = jnp.where(forced, -NEG, imp)
    val = jnp.where(jrow * SEL_BLOCK > tcol, NEG, val)
    rank = jnp.zeros((ns, tq), F32)
    for j in range(ns):
        other = val[j:j + 1, :]
        ahead = (other > val) | ((other == val) & (jrow > j))
        rank = rank + jnp.where(ahead, 1.0, 0.0)
    selneg = jnp.where(rank < float(n_sel), 0.0, NEG)
    if ns < LANE:
        selneg = jnp.concatenate([selneg, jnp.zeros((LANE - ns, tq), F32)], axis=0)
    sel_ref[...] = selneg.T.astype(sel_ref.dtype)


def nsa_cmp_attention(proj, kvc, ovt, *, B, S, tq):
    M = B * S
    nq = S // tq
    G, dk = NSA_GROUPS, NSA_HEAD_DIM
    ncp = S // CMP_STRIDE
    ns = S // SEL_BLOCK
    assert ns <= LANE
    n_sel = min(SEL_COUNT, ns)
    gw = NSA_HG * dk
    return pl.pallas_call(
        functools.partial(_nsa_cmp_kernel, tq=tq, ncp=ncp, ns=ns, n_sel=n_sel, scale=dk ** -0.5),
        grid=(B, G, nq),
        in_specs=[pl.BlockSpec((tq, gw), lambda b, g, i: (b * nq + i, OFF_NSA_Q // gw + g)),
                  pl.BlockSpec((None, None, None, ncp, dk), lambda b, g, i: (0, b, g, 0, 0)),
                  pl.BlockSpec((None, None, None, ncp, dk), lambda b, g, i: (1, b, g, 0, 0)),
                  pl.BlockSpec((LANE, ncp), lambda b, g, i: (0, 0))],
        out_specs=[pl.BlockSpec((tq, gw), lambda b, g, i: (b * nq + i, g)),
                   pl.BlockSpec((None, None, tq, LANE), lambda b, g, i: (b, g, i, 0))],
        out_shape=[jax.ShapeDtypeStruct((M, G * gw), BF16),
                   jax.ShapeDtypeStruct((B, G, S, LANE), BF16)],
        compiler_params=_cparams(("parallel", "parallel", "parallel")),
        name="nsa_cmp_attention",
    )(proj, kvc, kvc, ovt)


def _nsa_main_kernel(q_ref, sel_ref, ks_ref, vs_ref, kw_ref, vw_ref, ind_ref, bias_ref, gate_ref, oc_ref,
                     o_ref, m_ref, acc_ref, *, tq, nback, scale):
    g = pl.program_id(1)
    qi = pl.program_id(2)
    dk = NSA_HEAD_DIM
    HG = NSA_HG
    ns = NSA_STREAMS
    hp = HG // ns
    st = range(ns)
    stack = lambda f: [jnp.concatenate([f(a * hp + i) for i in range(hp)], axis=0) for a in st]
    q = stack(lambda h: q_ref[:, h * dk:(h + 1) * dk])
    sel = jnp.concatenate([sel_ref[...]] * hp, axis=0)
    qa = [jnp.concatenate([q[a], sel], axis=1) for a in st]
    gate = jax.nn.sigmoid(gate_ref[...].astype(F32))

    for a in range(2 * ns):
        _softmax_init(m_ref.at[a], acc_ref.at[a])

    def steps(s, v, fix, base=0):
        for a in st:
            _softmax_step(fix(a, s[a]), v, m_ref.at[base + a], acc_ref.at[base + a], scale)

    wspan = nback + 1

    def rows_from(j, width):
        return pl.ds(j * tq if isinstance(j, int) else pl.multiple_of(j * tq, tq), width)

    def sel_scores(j, width=tq):
        sl = rows_from(j, width)
        ka = jnp.concatenate([ks_ref[sl, :], ind_ref[sl, :]], axis=1)
        return [_dot_nt(qa[a], ka) for a in st], vs_ref[sl, :]

    def sel_far(j, nc):
        s, v = sel_scores(j, nc * tq)
        steps(s, v, lambda a, x: x)

    _far_chunks(jnp.maximum(qi - 1, 0), sel_far, FAR_GROUP_NSA)

    def win_scores(j, width):
        sl = rows_from(j, width)
        kw = kw_ref[sl, :]
        return [_dot_nt(q[a], kw) for a in st], vw_ref[sl, :]

    def near(first_s, nc_s, first_w, nc_w):
        tile = lambda nc: (lambda a, x: x + bias_ref[a, :, (wspan - nc) * tq:wspan * tq])
        s_s, v_s = sel_scores(first_s, nc_s * tq)
        s_w, v_w = win_scores(first_w, nc_w * tq)
        steps(s_s, v_s, tile(nc_s))
        steps(s_w, v_w, tile(nc_w), base=ns)

    for k in range(wspan - 1):
        @pl.when(qi == k)
        def _(k=k):
            near(max(k - 1, 0), min(k + 1, 2), 0, k + 1)

    @pl.when(qi >= wspan - 1)
    def _():
        near(qi - 1, 2, qi - (wspan - 1), wspan)

    o_s = [_softmax_finish(acc_ref.at[a]) for a in st]
    o_w = [_softmax_finish(acc_ref.at[ns + a]) for a in st]

    lane = lax.broadcasted_iota(jnp.int32, gate.shape, 1)
    for h in range(HG):
        c = g * HG + h
        a, sl = h // hp, slice((h % hp) * tq, (h % hp + 1) * tq)
        gc = [jnp.sum(jnp.where(lane == br * NSA_HEADS + c, gate, 0.0), axis=-1, keepdims=True)
              for br in range(3)]
        o = gc[0] * oc_ref[:, h * dk:(h + 1) * dk].astype(F32) + gc[1] * o_s[a][sl] + gc[2] * o_w[a][sl]
        o_ref[:, h * dk:(h + 1) * dk] = o.astype(o_ref.dtype)


def nsa_main(proj, selneg, ind, btiles, o_c, *, B, S, tq):
    M = B * S
    nq = S // tq
    G, dk, HG = NSA_GROUPS, NSA_HEAD_DIM, NSA_HG
    gw = HG * dk
    assert WINDOW % tq == 0 and WINDOW // tq >= 2
    nback = WINDOW // tq
    ns, hp = NSA_STREAMS, HG // NSA_STREAMS
    kv = lambda n: pl.BlockSpec((S, dk), lambda b, g, i, n=n: (b, (OFF_NSA_KV + n * G * dk) // dk + g))
    return pl.pallas_call(
        functools.partial(_nsa_main_kernel, tq=tq, nback=nback, scale=dk ** -0.5),
        grid=(B, G, nq),
        in_specs=[pl.BlockSpec((tq, gw), lambda b, g, i: (b * nq + i, OFF_NSA_Q // gw + g)),
                  pl.BlockSpec((None, None, tq, LANE), lambda b, g, i: (b, g, i, 0)),
                  kv(2), kv(3), kv(4), kv(5),
                  pl.BlockSpec((S, LANE), lambda b, g, i: (0, 0)),
                  pl.BlockSpec((ns, hp * tq, (nback + 1) * tq), lambda b, g, i: (g, 0, 0)),
                  pl.BlockSpec((tq, LANE), lambda b, g, i: (b * nq + i, OFF_NSA_GATE // LANE)),
                  pl.BlockSpec((tq, gw), lambda b, g, i: (b * nq + i, g))],
        out_specs=pl.BlockSpec((tq, gw), lambda b, g, i: (b * nq + i, g)),
        out_shape=jax.ShapeDtypeStruct((M, G * gw), BF16),
        scratch_shapes=[pltpu.VMEM((2 * ns, hp * tq, LANE), F32),
                        pltpu.VMEM((2 * ns, hp * tq, 2 * LANE), F32)],
        compiler_params=_cparams(("parallel", "parallel", "arbitrary")),
        name="nsa_main",
    )(proj, selneg, proj, proj, proj, proj, ind, btiles, proj, o_c)


def _merge_kernel(x_ref, oa_ref, ob_ref, oc_ref, g0_ref, g1_ref, g2_ref, wb_ref, wo_ref, gain_ref, o_ref):
    merged = None
    for o_r, g_r, br in ((oa_ref, g0_ref, 0), (ob_ref, g1_ref, 1), (oc_ref, g2_ref, 2)):
        t = jax.nn.sigmoid(g_r[...].astype(F32)) * _dot(o_r[...], wb_ref[br])
        merged = t if merged is None else merged + t
    y = _dot(merged.astype(BF16), wo_ref[...])
    o_ref[...] = x_ref[...] + _rms(y, gain_ref[...])


def merge_branches(x, o_a, o_b, o_c, proj, wb, wo, gain, *, tm):
    M, D = x.shape
    row = pl.BlockSpec((tm, D), lambda i: (i, 0))
    gate = lambda br: pl.BlockSpec((tm, D), lambda i, br=br: (i, OFF_BGATE // D + br))
    return pl.pallas_call(
        _merge_kernel,
        grid=(M // tm,),
        in_specs=[row, row, row, row, gate(0), gate(1), gate(2),
                  pl.BlockSpec(wb.shape, lambda i: (0, 0, 0)),
                  pl.BlockSpec(wo.shape, lambda i: (0, 0)),
                  pl.BlockSpec((1, D), lambda i: (0, 0))],
        out_specs=row,
        out_shape=jax.ShapeDtypeStruct((M, D), F32),
        compiler_params=_cparams(("parallel",)),
        name="merge_branches",
    )(x, o_a, o_b, o_c, proj, proj, proj, wb, wo, gain)


def _cross_kernel(x_ref, kv_ref, wq_ref, wo_ref, gpre_ref, gpost_ref, o_ref, *, scale):
    x = x_ref[...]
    h = _rms(x, gpre_ref[...]).astype(BF16)
    q = _dot(h, wq_ref[...]).astype(BF16)
    dh = CROSS_HEAD_DIM
    outs = []
    for hd in range(CROSS_HEADS):
        k = kv_ref[:, hd * dh:(hd + 1) * dh]
        v = kv_ref[:, D_MODEL + hd * dh:D_MODEL + (hd + 1) * dh]
        s = _dot_nt(q[:, hd * dh:(hd + 1) * dh], k) * scale
        p = jnp.exp(s - jnp.max(s, axis=-1, keepdims=True))
        p = p / jnp.sum(p, axis=-1, keepdims=True)
        outs.append(_dot(p.astype(BF16), v).astype(BF16))
    y = _dot(jnp.concatenate(outs, axis=1), wo_ref[...])
    o_ref[...] = x + _rms(y, gpost_ref[...])


def cross_attention(x, kv, wq, wo, gpre, gpost, *, S, tm):
    M, D = x.shape
    mlen = kv.shape[0] // (M // S)
    per_b = S // tm
    return pl.pallas_call(
        functools.partial(_cross_kernel, scale=CROSS_HEAD_DIM ** -0.5),
        grid=(M // tm,),
        in_specs=[pl.BlockSpec((tm, D), lambda i: (i, 0)),
                  pl.BlockSpec((mlen, 2 * D), lambda i: (i // per_b, 0)),
                  pl.BlockSpec(wq.shape, lambda i: (0, 0)),
                  pl.BlockSpec(wo.shape, lambda i: (0, 0)),
                  pl.BlockSpec((1, D), lambda i: (0, 0)),
                  pl.BlockSpec((1, D), lambda i: (0, 0))],
        out_specs=pl.BlockSpec((tm, D), lambda i: (i, 0)),
        out_shape=jax.ShapeDtypeStruct((M, D), F32),
        compiler_params=_cparams(("parallel",)),
        name="cross_attention",
    )(x, kv, wq, wo, gpre, gpost)


def _mlp_kernel(x_ref, w1_ref, w2_ref, gpre_ref, gpost_ref, o_ref, h_ref, acc_ref):
    j = pl.program_id(1)

    @pl.when(j == 0)
    def _():
        h_ref[...] = _rms(x_ref[...], gpre_ref[...]).astype(BF16)
        acc_ref[...] = jnp.zeros(acc_ref.shape, F32)

    a = jnp.maximum(_dot(h_ref[...], w1_ref[...]), 0.0)
    acc_ref[...] += _dot((a * a).astype(BF16), w2_ref[...])

    @pl.when(j == pl.num_programs(1) - 1)
    def _():
        o_ref[...] = x_ref[...] + _rms(acc_ref[...], gpost_ref[...])


def mlp(x, w1, w2, gpre, gpost, *, tm, tf):
    M, D = x.shape
    FF = w1.shape[1]
    return pl.pallas_call(
        _mlp_kernel,
        grid=(M // tm, FF // tf),
        in_specs=[pl.BlockSpec((tm, D), lambda i, j: (i, 0)),
                  pl.BlockSpec((D, tf), lambda i, j: (0, j)),
                  pl.BlockSpec((tf, D), lambda i, j: (j, 0)),
                  pl.BlockSpec((1, D), lambda i, j: (0, 0)),
                  pl.BlockSpec((1, D), lambda i, j: (0, 0))],
        out_specs=pl.BlockSpec((tm, D), lambda i, j: (i, 0)),
        out_shape=jax.ShapeDtypeStruct((M, D), F32),
        scratch_shapes=[pltpu.VMEM((tm, D), BF16), pltpu.VMEM((tm, D), F32)],
        compiler_params=_cparams(("parallel", "arbitrary")),
        name="mlp",
    )(x, w1, w2, gpre, gpost)


def _rot_half_cols(w):
    half = w.shape[-1] // 2
    return jnp.concatenate([-w[..., half:], w[..., :half]], axis=-1)


def _pack_w_in(w):
    widths = (MLA_Q_LORA, MLA_KV_LORA, MLA_ROPE, 1024, 1024, 1024, IDX_HEADS * IDX_DIM, IDX_DIM, IDX_HEADS,
              1024, 256, 256, 256, 256, 256, 256, 3 * NSA_HEADS, N_BRANCH * D_MODEL)
    offs = np.concatenate([[0], np.cumsum(widths)])
    seg = [w[:, offs[i]:offs[i + 1]] for i in range(len(widths))]
    (q_lat, kv_lat, k_rope, dsa_q, dsa_k, dsa_v, idx_q, idx_k, idx_w,
     nsa_q, kc, vc, ks, vs, kw, vw, nsa_gate, bgate) = seg
    z = lambda n: jnp.zeros((w.shape[0], n), w.dtype)
    cols = [dsa_q, dsa_k, dsa_v, nsa_q, bgate, idx_q, kc, vc, ks, vs, kw, vw,
            q_lat, kv_lat, k_rope, z(LANE - MLA_ROPE), _rot_half_cols(k_rope), z(LANE - MLA_ROPE), z(LANE),
            idx_k, idx_k, idx_w, z(LANE - IDX_HEADS), nsa_gate, z(LANE - 3 * NSA_HEADS)]
    out = jnp.concatenate(cols, axis=1)
    out = jnp.concatenate([out, z(N_PACK - out.shape[1])], axis=1)
    return out.astype(BF16)


def _pack_mla_weights(w_uq, w_ukv):
    H = MLA_HEADS
    uq = w_uq.reshape(MLA_Q_LORA, H, MLA_NOPE + MLA_ROPE)
    nope = uq[:, :, :MLA_NOPE].reshape(MLA_Q_LORA, H * MLA_NOPE)
    rope = uq[:, :, MLA_NOPE:]
    pad = jnp.zeros((MLA_Q_LORA, H, LANE - MLA_ROPE), w_uq.dtype)
    wqr = jnp.concatenate([rope, pad], axis=-1).reshape(MLA_Q_LORA, H * LANE)
    wqrr = jnp.concatenate([_rot_half_cols(rope), pad], axis=-1).reshape(MLA_Q_LORA, H * LANE)
    ukv = w_ukv.reshape(MLA_KV_LORA, H, MLA_NOPE + MLA_V)
    wkv = jnp.concatenate([ukv[:, :, :MLA_NOPE].reshape(MLA_KV_LORA, H * MLA_NOPE),
                           ukv[:, :, MLA_NOPE:].reshape(MLA_KV_LORA, H * MLA_V)], axis=1)
    return nope.astype(BF16), wqr.astype(BF16), wqrr.astype(BF16), wkv.astype(BF16)


def _constants(S):
    half = MLA_ROPE // 2
    inv_freq = ROPE_BASE ** (-np.arange(0, MLA_ROPE, 2, dtype=np.float32) / MLA_ROPE)
    freq = np.zeros((1, LANE), np.float32)
    freq[0, :half] = inv_freq
    freq[0, half:2 * half] = inv_freq
    ncp = S // CMP_STRIDE
    ns = S // SEL_BLOCK
    c_start = np.arange(ncp) * CMP_STRIDE
    j_start = np.arange(ns) * SEL_BLOCK
    overlap = ((c_start[None, :] < j_start[:, None] + SEL_BLOCK) &
               (c_start[None, :] + CMP_LEN > j_start[:, None])).astype(np.float32)
    overlap[:, ncp - 1] = 0.0
    ovt = np.zeros((LANE, ncp), np.float32)
    ovt[:ns] = overlap
    ind = np.zeros((S, LANE), np.float32)
    ind[np.arange(S), np.arange(S) // SEL_BLOCK] = 1.0
    return jnp.asarray(freq), jnp.asarray(ovt, BF16), jnp.asarray(ind, BF16)


def kernel(x, mem, positions, rel_bias, norm_gains, w_in, mla_q_norm, mla_kv_norm, mla_w_uq, mla_w_ukv,
           nsa_cmp_pos, nsa_cmp_w1, nsa_cmp_w2, w_branch, w_out, cross_wq, cross_wkv, cross_wo, mlp_w1, mlp_w2):
    B, S, D = x.shape
    M = B * S
    depth = w_in.shape[0]
    tq = TQ if S % TQ == 0 else TQ_NSA
    tqn = TQ_NSA
    assert S % tq == 0 and S % tqn == 0 and D == D_MODEL
    topk = min(DSA_TOPK_MAX, S // 4)
    freq, ovt, ind = _constants(S)
    btiles = bias_tiles(rel_bias, tq, 0, DSA_HEADS, nspan=2, masked=False, stack=1)
    btiles_n = bias_tiles(rel_bias, tqn, DSA_HEADS, NSA_HEADS, nspan=WINDOW // tqn + 1, masked=True,
                          stack=NSA_HG // NSA_STREAMS)
    pos = positions.reshape(M, 1)
    xf = x.reshape(M, D)
    memf = mem.reshape(B * mem.shape[1], D)
    tm_big = 1024 if M % 1024 == 0 else tq
    tm = 512 if M % 512 == 0 else tq
    ncp = S // CMP_STRIDE
    row = lambda v: v.reshape(1, -1)

    for l in range(depth):
        g = norm_gains[l]
        proj = norm_matmul(xf, row(g[0]), _pack_w_in(w_in[l]), tm=tm_big, tn=TN_IN)

        wqn, wqr, wqrr, wkv = _pack_mla_weights(mla_w_uq[l], mla_w_ukv[l])
        qn, qr, kn, vv, kr = mla_prep(proj, pos, freq, row(mla_q_norm[l]), row(mla_kv_norm[l]),
                                      wqn, wqr, wqrr, wkv, tm=tm_big)
        o_a = mla_attention(qn, qr, kn, kr, vv, B=B, S=S, tq=tq)

        maskb = dsa_index_mask(proj, B=B, S=S, tqi=TQ_IDX, tk=tq, topk=topk)
        o_b = dsa_attention(proj, maskb, btiles, B=B, S=S, tq=tq)

        kvc_in = jnp.stack([
            proj[:, OFF_NSA_KV + a * 256:OFF_NSA_KV + (a + 1) * 256].reshape(B * ncp, CMP_STRIDE * 256)
            for a in range(2)])
        posflat = jnp.broadcast_to(nsa_cmp_pos[l].reshape(2, 1, CMP_LEN * NSA_HEAD_DIM),
                                   (2, 8, CMP_LEN * NSA_HEAD_DIM)).astype(BF16)
        kvc = nsa_compress(kvc_in, posflat, nsa_cmp_w1[l].astype(BF16), nsa_cmp_w2[l].astype(BF16), B=B, S=S)
        o_cmp, selneg = nsa_cmp_attention(proj, kvc, ovt, B=B, S=S, tq=TQ_CMP if S % TQ_CMP == 0 else tq)
        o_c = nsa_main(proj, selneg, ind, btiles_n, o_cmp, B=B, S=S, tq=tqn)

        xf = merge_branches(xf, o_a, o_b, o_c, proj, w_branch[l].astype(BF16), w_out[l].astype(BF16),
                            row(g[1]), tm=tm)

        mkv = norm_matmul(memf, row(g[3]), cross_wkv[l].astype(BF16), tm=memf.shape[0] // B, tn=1024)
        xf = cross_attention(xf, mkv, cross_wq[l].astype(BF16), cross_wo[l].astype(BF16),
                             row(g[2]), row(g[4]), S=S, tm=tm_big)

        xf = mlp(xf, mlp_w1[l].astype(BF16), mlp_w2[l].astype(BF16), row(g[5]), row(g[6]), tm=tm_big, tf=1024)

    return xf.reshape(B, S, D)
```

```python
import functools
import math

import numpy as np
import jax
import jax.numpy as jnp
from jax import lax
from jax.experimental import pallas as pl
from jax.experimental.pallas import tpu as pltpu

F32 = jnp.float32
BF16 = jnp.bfloat16

D_MODEL = 1024
EPS = 1e-6
MLA_HEADS = 8
MLA_Q_LORA = 384
MLA_KV_LORA = 256
MLA_NOPE = 128
MLA_ROPE = 64
MLA_V = 128
ROPE_BASE = 10000.0
DSA_HEADS = 8
DSA_HEAD_DIM = 128
IDX_HEADS = 8
IDX_DIM = 64
DSA_TOPK_MAX = 256
NSA_HEADS = 8
NSA_GROUPS = 2
NSA_HG = NSA_HEADS // NSA_GROUPS
NSA_HEAD_DIM = 128
CMP_STRIDE = 16
CMP_LEN = 2 * CMP_STRIDE
CMP_HIDDEN = 256
SEL_BLOCK = 64
SEL_COUNT = 16
WINDOW = 512
REL_BUCKETS = 32
REL_MAX_DIST = 128
REL_HEADS = DSA_HEADS + NSA_HEADS
CROSS_HEADS = 4
CROSS_HEAD_DIM = D_MODEL // CROSS_HEADS
D_FF = 4 * D_MODEL
N_BRANCH = 3

LANE = 128
NEG = -1e30
VMEM_LIMIT = 48 * 1024 * 1024

TQ = 512
TQ_NSA = 256
TQ_CMP = 1024
HPS = 2
FAR_GROUP = 4
NEAR = 4
FAR_GROUP_NSA = 8
NSA_STREAMS = 2
BISECT_ITERS = 32
IDX_TIE_CHECK = 16
TQ_IDX = 512
IDX_CHUNK = 256

OFF_DSA_Q = 0
OFF_DSA_K = 1024
OFF_DSA_V = 2048
OFF_NSA_Q = 3072
OFF_BGATE = 4096
OFF_IDX_Q = 7168
OFF_NSA_KV = 7680
OFF_MLA = 9216
OFF_IDX_K = 10240
OFF_IDX_W = 10368
OFF_NSA_GATE = 10496
N_PACK = 10752
TN_IN = 2688


def _cparams(sem):
    return pltpu.CompilerParams(dimension_semantics=sem, vmem_limit_bytes=VMEM_LIMIT)


def _rms(x, g):
    return x * lax.rsqrt(jnp.mean(x * x, axis=-1, keepdims=True) + EPS) * g


def _dot(a, b):
    return jnp.dot(a, b, preferred_element_type=F32)


def _dot_nt(a, b):
    return lax.dot_general(a, b, (((1,), (1,)), ((), ())), preferred_element_type=F32)


def _norm_mm_kernel(x_ref, g_ref, w_ref, o_ref, xn_ref):
    @pl.when(pl.program_id(1) == 0)
    def _():
        xn_ref[...] = _rms(x_ref[...].astype(F32), g_ref[...]).astype(BF16)

    o_ref[...] = _dot(xn_ref[...], w_ref[...]).astype(o_ref.dtype)


def norm_matmul(x, g, w, *, tm, tn):
    M, K = x.shape
    N = w.shape[1]
    return pl.pallas_call(
        _norm_mm_kernel,
        grid=(M // tm, N // tn),
        in_specs=[pl.BlockSpec((tm, K), lambda i, j: (i, 0)),
                  pl.BlockSpec((1, K), lambda i, j: (0, 0)),
                  pl.BlockSpec((K, tn), lambda i, j: (0, j))],
        out_specs=pl.BlockSpec((tm, tn), lambda i, j: (i, j)),
        out_shape=jax.ShapeDtypeStruct((M, N), BF16),
        scratch_shapes=[pltpu.VMEM((tm, K), BF16)],
        compiler_params=_cparams(("parallel", "arbitrary")),
        name="norm_matmul",
    )(x, g, w)


def _bias_tiles_kernel(tab_ref, bkt_ref, o_ref, *, inv_scale, h0, tq, nspan, masked):
    h = pl.program_id(0) + h0
    bkt = bkt_ref[...]
    far = tab_ref[REL_BUCKETS - 1, h]
    out = jnp.zeros(bkt.shape, F32)
    for b in range(REL_BUCKETS - 1):
        out = jnp.where(bkt == b, (tab_ref[b, h] - far) * inv_scale, out)
    diag, prev = out[0], out[1]
    r, c = _tri(tq, tq)
    if masked:
        diag = jnp.where(r >= c, diag, NEG)
    o_ref[:, (nspan - 1) * tq:nspan * tq] = diag
    o_ref[:, (nspan - 2) * tq:(nspan - 1) * tq] = prev
    for k in range(nspan - 2):
        edge = jnp.where(r < c, 0.0, NEG) if (masked and k == 0) else jnp.zeros((tq, tq), F32)
        o_ref[:, k * tq:(k + 1) * tq] = edge


def _bucket_tiles(tq):
    i = np.arange(tq)[:, None]
    j = np.arange(tq)[None, :]
    d = np.stack([i - j, tq + i - j]).astype(np.int32)
    n = jnp.maximum(jnp.asarray(d), 0)
    exact = REL_BUCKETS // 2
    nf = jnp.maximum(n, 1).astype(F32)
    log_b = exact + (jnp.log(nf / exact) / math.log(REL_MAX_DIST / exact) * (REL_BUCKETS - exact)).astype(jnp.int32)
    return jnp.where(n < exact, n, jnp.minimum(log_b, REL_BUCKETS - 1)).astype(jnp.int32)


def bias_tiles(rel_bias, tq, h0, nh, *, nspan, masked, stack):
    assert tq >= REL_MAX_DIST
    bkt = _bucket_tiles(tq)
    assert DSA_HEAD_DIM == NSA_HEAD_DIM
    return pl.pallas_call(
        functools.partial(_bias_tiles_kernel, inv_scale=DSA_HEAD_DIM ** 0.5, h0=h0, tq=tq, nspan=nspan,
                          masked=masked),
        grid=(nh,),
        in_specs=[pl.BlockSpec(memory_space=pltpu.SMEM),
                  pl.BlockSpec((2, tq, tq), lambda h: (0, 0, 0))],
        out_specs=pl.BlockSpec((None, tq, nspan * tq), lambda h: (h // stack, h % stack, 0)),
        out_shape=jax.ShapeDtypeStruct((nh // stack, stack * tq, nspan * tq), F32),
        compiler_params=_cparams(("arbitrary",)),
        name="bias_tiles",
    )(rel_bias, bkt)


def _mla_prep_kernel(lat_ref, pos_ref, freq_ref, qg_ref, kvg_ref, wqn_ref, wqr_ref, wqrr_ref, wkv_ref,
                     qn_ref, qr_ref, kn_ref, v_ref, kr_ref):
    lat = lat_ref[...].astype(F32)
    qlat = _rms(lat[:, :MLA_Q_LORA], qg_ref[...]).astype(BF16)
    kvlat = _rms(lat[:, MLA_Q_LORA:MLA_Q_LORA + MLA_KV_LORA], kvg_ref[...]).astype(BF16)
    o = MLA_Q_LORA + MLA_KV_LORA
    kr = lat[:, o:o + LANE]
    krr = lat[:, o + LANE:o + 2 * LANE]
    ang = pos_ref[...].astype(F32) * freq_ref[...]
    cos, sin = jnp.cos(ang), jnp.sin(ang)
    cos8 = jnp.concatenate([cos] * MLA_HEADS, axis=1)
    sin8 = jnp.concatenate([sin] * MLA_HEADS, axis=1)
    qn_ref[...] = _dot(qlat, wqn_ref[...]).astype(BF16)
    qr_ref[...] = (_dot(qlat, wqr_ref[...]) * cos8 + _dot(qlat, wqrr_ref[...]) * sin8).astype(BF16)
    kv = _dot(kvlat, wkv_ref[...])
    nk = MLA_HEADS * MLA_NOPE
    kn_ref[...] = kv[:, :nk].astype(BF16)
    v_ref[...] = kv[:, nk:].astype(BF16)
    kr_ref[...] = (kr * cos + krr * sin).astype(BF16)


def mla_prep(proj, pos, freq, qg, kvg, wqn, wqr, wqrr, wkv, *, tm):
    M = proj.shape[0]
    HD = MLA_HEADS * LANE
    full = lambda a: pl.BlockSpec(a.shape, lambda i: (0, 0))
    outs = [jax.ShapeDtypeStruct((M, HD), BF16)] * 4 + [jax.ShapeDtypeStruct((M, LANE), BF16)]
    return pl.pallas_call(
        _mla_prep_kernel,
        grid=(M // tm,),
        in_specs=[pl.BlockSpec((tm, 1024), lambda i: (i, OFF_MLA // 1024)),
                  pl.BlockSpec((tm, 1), lambda i: (i, 0)),
                  full(freq), full(qg), full(kvg), full(wqn), full(wqr), full(wqrr), full(wkv)],
        out_specs=[pl.BlockSpec((tm, HD), lambda i: (i, 0))] * 4 + [pl.BlockSpec((tm, LANE), lambda i: (i, 0))],
        out_shape=outs,
        compiler_params=_cparams(("parallel",)),
        name="mla_prep",
    )(proj, pos, freq, qg, kvg, wqn, wqr, wqrr, wkv)


def _softmax_init(m_ref, acc_ref):
    m_ref[...] = jnp.full(m_ref.shape, NEG, F32)
    acc_ref[...] = jnp.zeros(acc_ref.shape, F32)


def _with_ones(v):
    return jnp.concatenate([v, jnp.ones(v.shape, v.dtype)], axis=1)


def _softmax_step(s, v, m_ref, acc_ref, scale):
    c = scale * math.log2(math.e)
    m_prev = m_ref[...]
    m_new = jnp.maximum(m_prev, jnp.max(s, axis=-1, keepdims=True))
    alpha = jnp.exp2((m_prev - m_new) * c)
    p = jnp.exp2((s - jnp.tile(m_new, (1, s.shape[1] // LANE))) * c)
    acc_ref[...] = jnp.tile(alpha, (1, 2)) * acc_ref[...] + _dot(p.astype(BF16), _with_ones(v))
    m_ref[...] = m_new


def _softmax_finish(acc_ref):
    acc = acc_ref[...]
    return acc[:, :LANE] / acc[:, LANE:]


def _far_chunks(n, step, largest):
    def body(p, c):
        step(largest * p, largest)
        return c

    lax.fori_loop(0, n // largest, body, 0)
    w = largest // 2
    while w >= 1:
        @pl.when(n % (2 * w) >= w)
        def _(w=w):
            step(n // (2 * w) * (2 * w), w)
        w //= 2


def _tri(tq, tk):
    row = lax.broadcasted_iota(jnp.int32, (tq, tk), 0)
    col = lax.broadcasted_iota(jnp.int32, (tq, tk), 1)
    return row, col


def _mla_attn_kernel(qn_ref, qr_ref, kn_ref, kr_ref, v_ref, o_ref, m_ref, acc_ref, *, tq, scale):
    qi = pl.program_id(2)
    hs = range(HPS)
    q = [jnp.concatenate([qn_ref[:, h * LANE:(h + 1) * LANE], qr_ref[:, h * LANE:(h + 1) * LANE]], axis=-1)
         for h in hs]
    for h in hs:
        _softmax_init(m_ref.at[h], acc_ref.at[h])

    def step(j, nc, causal_last):
        sl = pl.ds(j * tq if isinstance(j, int) else pl.multiple_of(j * tq, tq), nc * tq)
        kr = kr_ref[sl, :]
        if causal_last:
            row, col = _tri(tq, nc * tq)
            ok = col <= row + (nc - 1) * tq
        s = [_dot_nt(q[h], jnp.concatenate([kn_ref[sl, h * LANE:(h + 1) * LANE], kr], axis=-1)) for h in hs]
        for h in hs:
            sh = jnp.where(ok, s[h], NEG) if causal_last else s[h]
            _softmax_step(sh, v_ref[sl, h * LANE:(h + 1) * LANE], m_ref.at[h], acc_ref.at[h], scale)

    _far_chunks(jnp.maximum(qi - (NEAR - 1), 0), lambda j, nc: step(j, nc, False), FAR_GROUP)
    for k in range(NEAR - 1):
        @pl.when(qi == k)
        def _(k=k):
            step(0, k + 1, True)

    @pl.when(qi >= NEAR - 1)
    def _():
        step(qi - (NEAR - 1), NEAR, True)

    for h in hs:
        o_ref[:, h * LANE:(h + 1) * LANE] = _softmax_finish(acc_ref.at[h]).astype(o_ref.dtype)


def mla_attention(qn, qr, kn, kr, v, *, B, S, tq):
    M = B * S
    nq = S // tq
    H = MLA_HEADS
    scale = (MLA_NOPE + MLA_ROPE) ** -0.5
    w = HPS * LANE
    qspec = pl.BlockSpec((tq, w), lambda b, h, i: (b * nq + i, h))
    kspec = pl.BlockSpec((S, w), lambda b, h, i: (b, h))
    return pl.pallas_call(
        functools.partial(_mla_attn_kernel, tq=tq, scale=scale),
        grid=(B, H // HPS, nq),
        in_specs=[qspec, qspec, kspec, pl.BlockSpec((S, LANE), lambda b, h, i: (b, 0)), kspec],
        out_specs=qspec,
        out_shape=jax.ShapeDtypeStruct((M, H * MLA_V), BF16),
        scratch_shapes=[pltpu.VMEM((HPS, tq, LANE), F32), pltpu.VMEM((HPS, tq, 2 * LANE), F32)],
        compiler_params=_cparams(("parallel", "parallel", "arbitrary")),
        name="mla_attention",
    )(qn, qr, kn, kr, v)


def _dsa_index_kernel(iq_ref, ik_ref, iw_ref, tri_ref, mb_ref, sc_ref, *, tq, tk, S, topk, iters):
    t0 = pl.program_id(1) * tq
    nk = S // tk
    nch = (t0 + tq - 1) // tk + 1
    hw = IDX_CHUNK
    iq = iq_ref[...]
    lane = lax.broadcasted_iota(jnp.int32, (tq, LANE), 1)
    qs = []
    for h in range(IDX_HEADS):
        blk = iq[:, (h // 2) * LANE:(h // 2 + 1) * LANE]
        keep = (lane >= IDX_DIM) if h % 2 else (lane < IDX_DIM)
        qs.append(jnp.where(keep, blk, jnp.zeros_like(blk)))
    q8 = jnp.concatenate(qs, axis=0)
    iwt = iw_ref[...].astype(F32).T
    wrow = [iwt[h:h + 1, :] for h in range(IDX_HEADS)]
    qpos = lax.broadcasted_iota(jnp.int32, (hw, tq), 1) + t0
    krow = lax.broadcasted_iota(jnp.int32, (hw, tq), 0)

    def fold(x, op):
        out = x[0:8]
        for r in range(1, x.shape[0] // 8):
            out = op(out, x[r * 8:(r + 1) * 8])
        return out

    def score_chunk(c, carry):
        mn, mx = carry
        for half in range(tk // hw):
            k0 = pl.multiple_of(c * tk + half * hw, hw)
            res = _dot_nt(ik_ref[pl.ds(k0, hw), :], q8)
            sc = wrow[0] * jnp.maximum(res[:, 0:tq], 0.0)
            for h in range(1, IDX_HEADS):
                sc = sc + wrow[h] * jnp.maximum(res[:, h * tq:(h + 1) * tq], 0.0)
            causal = (krow + k0) <= qpos
            mn = jnp.minimum(mn, fold(jnp.where(causal, sc, -NEG), jnp.minimum))
            sc = jnp.where(causal, sc, NEG)
            mx = jnp.maximum(mx, fold(sc, jnp.maximum))
            sc_ref[c, half * hw:(half + 1) * hw, :] = sc
        return mn, mx

    mn, mx = lax.fori_loop(0, nch, score_chunk, (jnp.full((8, tq), -NEG, F32), jnp.full((8, tq), NEG, F32)))
    n_causal = (lax.broadcasted_iota(jnp.int32, (1, tq), 1) + (t0 + 1)).astype(F32)
    kf = jnp.minimum(float(topk), n_causal)

    def count(pred):
        def body(c, acc):
            return acc + fold(jnp.where(pred(sc_ref[c]), 1.0, 0.0), jnp.add)

        return jnp.sum(lax.fori_loop(0, nch, body, jnp.zeros((8, tq), F32)), axis=0, keepdims=True)

    lo0 = jnp.min(mn, axis=0, keepdims=True)
    mx = jnp.max(mx, axis=0, keepdims=True)
    hi0 = mx + (jnp.abs(mx) * 1e-6 + 1.0)

    def bisect(c):
        lo, hi, n_lo, n_hi = c
        mid = 0.5 * (lo + hi)
        n_mid = count(lambda blk: blk >= mid)
        ge = n_mid >= kf
        return (jnp.where(ge, mid, lo), jnp.where(ge, hi, mid),
                jnp.where(ge, n_mid, n_lo), jnp.where(ge, n_hi, n_mid))

    def open_rows(n_lo, n_hi, settled):
        return (n_lo - kf) * jnp.where(n_lo - n_hi > 2.0, 1.0, 0.0) * (1.0 - settled)

    def search(state, limit, settled):
        def cond(c):
            return jnp.logical_and(c[0] < limit, jnp.max(open_rows(c[3], c[4], settled)) > 0.0)

        def body(c):
            return (c[0] + 2,) + bisect(bisect(c[1:]))

        return lax.while_loop(cond, body, state)

    def band_min_max(lo, hi):
        def body(c, carry):
            blk = sc_ref[c]
            inside = (blk >= lo) & (blk < hi)
            return (jnp.minimum(carry[0], fold(jnp.where(inside, blk, -NEG), jnp.minimum)),
                    jnp.maximum(carry[1], fold(jnp.where(inside, blk, NEG), jnp.maximum)))

        bmin, bmax = lax.fori_loop(0, nch, body, (jnp.full((8, tq), -NEG, F32), jnp.full((8, tq), NEG, F32)))
        return jnp.min(bmin, axis=0, keepdims=True), jnp.max(bmax, axis=0, keepdims=True)

    def finish(state):
        _, lo, hi, n_lo, n_hi = state
        over = n_lo > kf
        bmin, bmax = lax.cond(jnp.max(n_lo - kf) > 0.0, lambda: band_min_max(lo, hi), lambda: (lo, lo))
        single = bmin == bmax
        split = over & (n_lo - n_hi == 2.0) & jnp.logical_not(single)
        unsplit = over & jnp.logical_not(split)
        return jnp.where(split, bmax, lo), jnp.where(over & single, 1.0, 0.0), jnp.where(unsplit, 1.0, 0.0)

    zero = jnp.zeros((1, tq), F32)
    state1 = search((jnp.int32(0), lo0, hi0, n_causal, zero), IDX_TIE_CHECK, zero)
    first = finish(state1)
    state = search(state1, iters, first[1])
    lo, _, unsplit = lax.cond(state[0] > state1[0], lambda: finish(state), lambda: first)
    hi, n_hi = state[2], state[4]
    tied = jnp.max(unsplit) > 0.0

    def fill(j, c):
        mb_ref[j] = jnp.full((tq, tk), NEG, mb_ref.dtype)
        return c

    lax.fori_loop(nch, nk, fill, 0)

    def emit(j, keep):
        mb_ref[j] = jnp.where(keep, 0.0, NEG).T.astype(mb_ref.dtype)

    @pl.when(jnp.logical_not(tied))
    def _():
        def body(j, c):
            emit(j, sc_ref[j] >= lo)
            return c

        lax.fori_loop(0, nch, body, 0)

    @pl.when(tied)
    def _():
        need = kf - n_hi
        tri = tri_ref[...]

        def body(j, base):
            sc = sc_ref[j]
            above = sc >= hi
            band = (sc >= lo) & jnp.logical_not(above)
            prefix = _dot(tri, jnp.where(band, 1.0, 0.0).astype(BF16)) + base
            emit(j, above | (band & (prefix <= need)))
            return prefix[tk - 1:tk, :]

        lax.fori_loop(0, nch, body, zero)


def dsa_index_mask(proj, *, B, S, tqi, tk, topk):
    nq = S // tqi
    nk = S // tk
    assert tk % tqi == 0 and tk % IDX_CHUNK == 0
    tri = jnp.asarray(np.tril(np.ones((tk, tk), np.float32)), BF16)
    return pl.pallas_call(
        functools.partial(_dsa_index_kernel, tq=tqi, tk=tk, S=S, topk=topk, iters=BISECT_ITERS),
        grid=(B, nq),
        in_specs=[pl.BlockSpec((tqi, 512), lambda b, i: (b * nq + i, OFF_IDX_Q // 512)),
                  pl.BlockSpec((S, LANE), lambda b, i: (b, OFF_IDX_K // LANE)),
                  pl.BlockSpec((tqi, LANE), lambda b, i: (b * nq + i, OFF_IDX_W // LANE)),
                  pl.BlockSpec((tk, tk), lambda b, i: (0, 0))],
        out_specs=pl.BlockSpec((None, nk, tqi, tk), lambda b, i: (b, 0, i, 0)),
        out_shape=jax.ShapeDtypeStruct((B, nk, S, tk), BF16),
        scratch_shapes=[pltpu.VMEM((nk, tk, tqi), F32)],
        compiler_params=_cparams(("parallel", "parallel")),
        name="dsa_index_mask",
    )(proj, proj, proj, tri)


def _dsa_attn_kernel(q_ref, k_ref, v_ref, mb_ref, bias_ref, o_ref, m_ref, acc_ref, *, tq, scale):
    qi = pl.program_id(2)
    hs = range(HPS)
    cols = lambda h: slice(h * LANE, (h + 1) * LANE)
    q = [q_ref[:, cols(h)] for h in hs]
    for h in hs:
        _softmax_init(m_ref.at[h], acc_ref.at[h])

    def step(j, nc, bias):
        sl = pl.ds(j * tq if isinstance(j, int) else pl.multiple_of(j * tq, tq), nc * tq)
        mb = mb_ref[j] if nc == 1 else jnp.concatenate([mb_ref[j + i] for i in range(nc)], axis=1)
        mb = mb.astype(F32)
        s = [_dot_nt(q[h], k_ref[sl, cols(h)]) + mb for h in hs]
        for h in hs:
            sh = s[h]
            if bias and nc == 1:
                sh = sh + bias_ref[h, :, tq:2 * tq]
            elif bias and nc == 2:
                sh = sh + bias_ref[h]
            elif bias:
                sh = jnp.concatenate([sh[:, :(nc - 2) * tq], sh[:, (nc - 2) * tq:] + bias_ref[h]], axis=1)
            _softmax_step(sh, v_ref[sl, cols(h)], m_ref.at[h], acc_ref.at[h], scale)

    _far_chunks(jnp.maximum(qi - (NEAR - 1), 0), lambda j, nc: step(j, nc, False), FAR_GROUP)
    for k in range(NEAR - 1):
        @pl.when(qi == k)
        def _(k=k):
            step(0, k + 1, True)

    @pl.when(qi >= NEAR - 1)
    def _():
        step(qi - (NEAR - 1), NEAR, True)

    for h in hs:
        o_ref[:, cols(h)] = _softmax_finish(acc_ref.at[h]).astype(o_ref.dtype)


def dsa_attention(proj, maskb, btiles, *, B, S, tq):
    M = B * S
    nq = S // tq
    H = DSA_HEADS
    scale = DSA_HEAD_DIM ** -0.5
    w = HPS * LANE
    return pl.pallas_call(
        functools.partial(_dsa_attn_kernel, tq=tq, scale=scale),
        grid=(B, H // HPS, nq),
        in_specs=[pl.BlockSpec((tq, w), lambda b, h, i: (b * nq + i, OFF_DSA_Q // w + h)),
                  pl.BlockSpec((S, w), lambda b, h, i: (b, OFF_DSA_K // w + h)),
                  pl.BlockSpec((S, w), lambda b, h, i: (b, OFF_DSA_V // w + h)),
                  pl.BlockSpec((None, nq, tq, tq), lambda b, h, i: (b, 0, i, 0)),
                  pl.BlockSpec((HPS, tq, 2 * tq), lambda b, h, i: (h, 0, 0))],
        out_specs=pl.BlockSpec((tq, w), lambda b, h, i: (b * nq + i, h)),
        out_shape=jax.ShapeDtypeStruct((M, H * DSA_HEAD_DIM), BF16),
        scratch_shapes=[pltpu.VMEM((HPS, tq, LANE), F32), pltpu.VMEM((HPS, tq, 2 * LANE), F32)],
        compiler_params=_cparams(("parallel", "parallel", "arbitrary")),
        name="dsa_attention",
    )(proj, proj, proj, maskb, btiles)


def _nsa_compress_kernel(x_ref, pos_ref, w1_ref, w2_ref, o_ref, *, ncp):
    dk = NSA_HEAD_DIM
    posw = _dot(pos_ref[...], w1_ref[...])[0:1, :]
    for g in range(NSA_GROUPS):
        lo = jnp.zeros((ncp, CMP_HIDDEN), F32)
        hi = jnp.zeros((ncp, CMP_HIDDEN), F32)
        for l in range(CMP_STRIDE):
            xs = x_ref[:, (l * NSA_GROUPS + g) * dk:(l * NSA_GROUPS + g + 1) * dk]
            lo = lo + _dot(xs, w1_ref[l * dk:(l + 1) * dk, :])
            hi = hi + _dot(xs, w1_ref[(CMP_STRIDE + l) * dk:(CMP_STRIDE + l + 1) * dk, :])
        hid = lo + pltpu.roll(hi, ncp - 1, 0) + posw
        o_ref[g] = _dot(jax.nn.gelu(hid).astype(BF16), w2_ref[...]).astype(o_ref.dtype)


def nsa_compress(xkv, posflat, w1, w2, *, B, S):
    ncp = S // CMP_STRIDE
    G, dk = NSA_GROUPS, NSA_HEAD_DIM
    width = CMP_STRIDE * G * dk
    return pl.pallas_call(
        functools.partial(_nsa_compress_kernel, ncp=ncp),
        grid=(2, B),
        in_specs=[pl.BlockSpec((None, ncp, width), lambda a, b: (a, b, 0)),
                  pl.BlockSpec((None, 8, CMP_LEN * dk), lambda a, b: (a, 0, 0)),
                  pl.BlockSpec((None, CMP_LEN * dk, CMP_HIDDEN), lambda a, b: (a, 0, 0)),
                  pl.BlockSpec((None, CMP_HIDDEN, dk), lambda a, b: (a, 0, 0))],
        out_specs=pl.BlockSpec((None, None, G, ncp, dk), lambda a, b: (a, b, 0, 0, 0)),
        out_shape=jax.ShapeDtypeStruct((2, B, G, ncp, dk), BF16),
        compiler_params=_cparams(("parallel", "parallel")),
        name="nsa_compress",
    )(xkv, posflat, w1, w2)


def _nsa_cmp_kernel(q_ref, kc_ref, vc_ref, ov_ref, oc_ref, sel_ref, *, tq, ncp, ns, n_sel, scale):
    t0 = pl.program_id(2) * tq
    dk = NSA_HEAD_DIM
    kc = kc_ref[...]
    vc = vc_ref[...]
    trow = lax.broadcasted_iota(jnp.int32, (tq, ncp), 0) + t0
    ccol = lax.broadcasted_iota(jnp.int32, (tq, ncp), 1)
    vis = (ccol * CMP_STRIDE + (CMP_LEN - 1)) <= trow
    psum = jnp.zeros((tq, ncp), F32)
    for h in range(NSA_HG):
        s = jnp.where(vis, _dot_nt(q_ref[:, h * dk:(h + 1) * dk], kc) * scale, NEG)
        m = jnp.max(s, axis=-1, keepdims=True)
        p = jnp.where(vis, jnp.exp(s - m), 0.0)
        d = jnp.sum(p, axis=-1, keepdims=True)
        p = p / jnp.where(d > 0, d, 1.0)
        oc_ref[:, h * dk:(h + 1) * dk] = _dot(p.astype(BF16), vc).astype(oc_ref.dtype)
        psum = psum + p
    p_hi = psum.astype(BF16)
    p_lo = (psum - p_hi.astype(F32)).astype(BF16)
    ov = ov_ref[...]
    imp = _dot_nt(ov, p_hi) + _dot_nt(ov, p_lo)
    imp = imp[:ns]
    jrow = lax.broadcasted_iota(jnp.int32, (ns, tq), 0)
    tcol = lax.broadcasted_iota(jnp.int32, (ns, tq), 1) + t0
    blk_t = tcol // SEL_BLOCK
    forced = (jrow == 0) | (jrow == blk_t) | (jrow == blk_t - 1)
    val = jnp.where(forced, -NEG, imp)
    val = jnp.where(jrow * SEL_BLOCK > tcol, NEG, val)
    rank = jnp.zeros((ns, tq), F32)
    for j in range(ns):
        other = val[j:j + 1, :]
        ahead = (other > val) | ((other == val) & (jrow > j))
        rank = rank + jnp.where(ahead, 1.0, 0.0)
    selneg = jnp.where(rank < float(n_sel), 0.0, NEG)
    if ns < LANE:
        selneg = jnp.concatenate([selneg, jnp.zeros((LANE - ns, tq), F32)], axis=0)
    sel_ref[...] = selneg.T.astype(sel_ref.dtype)


def nsa_cmp_attention(proj, kvc, ovt, *, B, S, tq):
    M = B * S
    nq = S // tq
    G, dk = NSA_GROUPS, NSA_HEAD_DIM
    ncp = S // CMP_STRIDE
    ns = S // SEL_BLOCK
    assert ns <= LANE
    n_sel = min(SEL_COUNT, ns)
    gw = NSA_HG * dk
    return pl.pallas_call(
        functools.partial(_nsa_cmp_kernel, tq=tq, ncp=ncp, ns=ns, n_sel=n_sel, scale=dk ** -0.5),
        grid=(B, G, nq),
        in_specs=[pl.BlockSpec((tq, gw), lambda b, g, i: (b * nq + i, OFF_NSA_Q // gw + g)),
                  pl.BlockSpec((None, None, None, ncp, dk), lambda b, g, i: (0, b, g, 0, 0)),
                  pl.BlockSpec((None, None, None, ncp, dk), lambda b, g, i: (1, b, g, 0, 0)),
                  pl.BlockSpec((LANE, ncp), lambda b, g, i: (0, 0))],
        out_specs=[pl.BlockSpec((tq, gw), lambda b, g, i: (b * nq + i, g)),
                   pl.BlockSpec((None, None, tq, LANE), lambda b, g, i: (b, g, i, 0))],
        out_shape=[jax.ShapeDtypeStruct((M, G * gw), BF16),
                   jax.ShapeDtypeStruct((B, G, S, LANE), BF16)],
        compiler_params=_cparams(("parallel", "parallel", "parallel")),
        name="nsa_cmp_attention",
    )(proj, kvc, kvc, ovt)


def _nsa_main_kernel(q_ref, sel_ref, ks_ref, vs_ref, kw_ref, vw_ref, ind_ref, bias_ref, gate_ref, oc_ref,
                     o_ref, m_ref, acc_ref, *, tq, nback, scale):
    g = pl.program_id(1)
    qi = pl.program_id(2)
    dk = NSA_HEAD_DIM
    HG = NSA_HG
    ns = NSA_STREAMS
    hp = HG // ns
    st = range(ns)
    stack = lambda f: [jnp.concatenate([f(a * hp + i) for i in range(hp)], axis=0) for a in st]
    q = stack(lambda h: q_ref[:, h * dk:(h + 1) * dk])
    sel = jnp.concatenate([sel_ref[...]] * hp, axis=0)
    qa = [jnp.concatenate([q[a], sel], axis=1) for a in st]
    gate = jax.nn.sigmoid(gate_ref[...].astype(F32))

    for a in range(2 * ns):
        _softmax_init(m_ref.at[a], acc_ref.at[a])

    def steps(s, v, fix, base=0):
        for a in st:
            _softmax_step(fix(a, s[a]), v, m_ref.at[base + a], acc_ref.at[base + a], scale)

    wspan = nback + 1

    def rows_from(j, width):
        return pl.ds(j * tq if isinstance(j, int) else pl.multiple_of(j * tq, tq), width)

    def sel_scores(j, width=tq):
        sl = rows_from(j, width)
        ka = jnp.concatenate([ks_ref[sl, :], ind_ref[sl, :]], axis=1)
        return [_dot_nt(qa[a], ka) for a in st], vs_ref[sl, :]

    def sel_far(j, nc):
        s, v = sel_scores(j, nc * tq)
        steps(s, v, lambda a, x: x)

    _far_chunks(jnp.maximum(qi - 1, 0), sel_far, FAR_GROUP_NSA)

    def win_scores(j, width):
        sl = rows_from(j, width)
        kw = kw_ref[sl, :]
        return [_dot_nt(q[a], kw) for a in st], vw_ref[sl, :]

    def near(first_s, nc_s, first_w, nc_w):
        tile = lambda nc: (lambda a, x: x + bias_ref[a, :, (wspan - nc) * tq:wspan * tq])
        s_s, v_s = sel_scores(first_s, nc_s * tq)
        s_w, v_w = win_scores(first_w, nc_w * tq)
        steps(s_s, v_s, tile(nc_s))
        steps(s_w, v_w, tile(nc_w), base=ns)

    for k in range(wspan - 1):
        @pl.when(qi == k)
        def _(k=k):
            near(max(k - 1, 0), min(k + 1, 2), 0, k + 1)

    @pl.when(qi >= wspan - 1)
    def _():
        near(qi - 1, 2, qi - (wspan - 1), wspan)

    o_s = [_softmax_finish(acc_ref.at[a]) for a in st]
    o_w = [_softmax_finish(acc_ref.at[ns + a]) for a in st]

    lane = lax.broadcasted_iota(jnp.int32, gate.shape, 1)
    for h in range(HG):
        c = g * HG + h
        a, sl = h // hp, slice((h % hp) * tq, (h % hp + 1) * tq)
        gc = [jnp.sum(jnp.where(lane == br * NSA_HEADS + c, gate, 0.0), axis=-1, keepdims=True)
              for br in range(3)]
        o = gc[0] * oc_ref[:, h * dk:(h + 1) * dk].astype(F32) + gc[1] * o_s[a][sl] + gc[2] * o_w[a][sl]
        o_ref[:, h * dk:(h + 1) * dk] = o.astype(o_ref.dtype)


def nsa_main(proj, selneg, ind, btiles, o_c, *, B, S, tq):
    M = B * S
    nq = S // tq
    G, dk, HG = NSA_GROUPS, NSA_HEAD_DIM, NSA_HG
    gw = HG * dk
    assert WINDOW % tq == 0 and WINDOW // tq >= 2
    nback = WINDOW // tq
    ns, hp = NSA_STREAMS, HG // NSA_STREAMS
    kv = lambda n: pl.BlockSpec((S, dk), lambda b, g, i, n=n: (b, (OFF_NSA_KV + n * G * dk) // dk + g))
    return pl.pallas_call(
        functools.partial(_nsa_main_kernel, tq=tq, nback=nback, scale=dk ** -0.5),
        grid=(B, G, nq),
        in_specs=[pl.BlockSpec((tq, gw), lambda b, g, i: (b * nq + i, OFF_NSA_Q // gw + g)),
                  pl.BlockSpec((None, None, tq, LANE), lambda b, g, i: (b, g, i, 0)),
                  kv(2), kv(3), kv(4), kv(5),
                  pl.BlockSpec((S, LANE), lambda b, g, i: (0, 0)),
                  pl.BlockSpec((ns, hp * tq, (nback + 1) * tq), lambda b, g, i: (g, 0, 0)),
                  pl.BlockSpec((tq, LANE), lambda b, g, i: (b * nq + i, OFF_NSA_GATE // LANE)),
                  pl.BlockSpec((tq, gw), lambda b, g, i: (b * nq + i, g))],
        out_specs=pl.BlockSpec((tq, gw), lambda b, g, i: (b * nq + i, g)),
        out_shape=jax.ShapeDtypeStruct((M, G * gw), BF16),
        scratch_shapes=[pltpu.VMEM((2 * ns, hp * tq, LANE), F32),
                        pltpu.VMEM((2 * ns, hp * tq, 2 * LANE), F32)],
        compiler_params=_cparams(("parallel", "parallel", "arbitrary")),
        name="nsa_main",
    )(proj, selneg, proj, proj, proj, proj, ind, btiles, proj, o_c)


def _merge_kernel(x_ref, oa_ref, ob_ref, oc_ref, g0_ref, g1_ref, g2_ref, wb_ref, wo_ref, gain_ref, o_ref):
    merged = None
    for o_r, g_r, br in ((oa_ref, g0_ref, 0), (ob_ref, g1_ref, 1), (oc_ref, g2_ref, 2)):
        t = jax.nn.sigmoid(g_r[...].astype(F32)) * _dot(o_r[...], wb_ref[br])
        merged = t if merged is None else merged + t
    y = _dot(merged.astype(BF16), wo_ref[...])
    o_ref[...] = x_ref[...] + _rms(y, gain_ref[...])


def merge_branches(x, o_a, o_b, o_c, proj, wb, wo, gain, *, tm):
    M, D = x.shape
    row = pl.BlockSpec((tm, D), lambda i: (i, 0))
    gate = lambda br: pl.BlockSpec((tm, D), lambda i, br=br: (i, OFF_BGATE // D + br))
    return pl.pallas_call(
        _merge_kernel,
        grid=(M // tm,),
        in_specs=[row, row, row, row, gate(0), gate(1), gate(2),
                  pl.BlockSpec(wb.shape, lambda i: (0, 0, 0)),
                  pl.BlockSpec(wo.shape, lambda i: (0, 0)),
                  pl.BlockSpec((1, D), lambda i: (0, 0))],
        out_specs=row,
        out_shape=jax.ShapeDtypeStruct((M, D), F32),
        compiler_params=_cparams(("parallel",)),
        name="merge_branches",
    )(x, o_a, o_b, o_c, proj, proj, proj, wb, wo, gain)


def _cross_kernel(x_ref, kv_ref, wq_ref, wo_ref, gpre_ref, gpost_ref, o_ref, *, scale):
    x = x_ref[...]
    h = _rms(x, gpre_ref[...]).astype(BF16)
    q = _dot(h, wq_ref[...]).astype(BF16)
    dh = CROSS_HEAD_DIM
    outs = []
    for hd in range(CROSS_HEADS):
        k = kv_ref[:, hd * dh:(hd + 1) * dh]
        v = kv_ref[:, D_MODEL + hd * dh:D_MODEL + (hd + 1) * dh]
        s = _dot_nt(q[:, hd * dh:(hd + 1) * dh], k) * scale
        p = jnp.exp(s - jnp.max(s, axis=-1, keepdims=True))
        p = p / jnp.sum(p, axis=-1, keepdims=True)
        outs.append(_dot(p.astype(BF16), v).astype(BF16))
    y = _dot(jnp.concatenate(outs, axis=1), wo_ref[...])
    o_ref[...] = x + _rms(y, gpost_ref[...])


def cross_attention(x, kv, wq, wo, gpre, gpost, *, S, tm):
    M, D = x.shape
    mlen = kv.shape[0] // (M // S)
    per_b = S // tm
    return pl.pallas_call(
        functools.partial(_cross_kernel, scale=CROSS_HEAD_DIM ** -0.5),
        grid=(M // tm,),
        in_specs=[pl.BlockSpec((tm, D), lambda i: (i, 0)),
                  pl.BlockSpec((mlen, 2 * D), lambda i: (i // per_b, 0)),
                  pl.BlockSpec(wq.shape, lambda i: (0, 0)),
                  pl.BlockSpec(wo.shape, lambda i: (0, 0)),
                  pl.BlockSpec((1, D), lambda i: (0, 0)),
                  pl.BlockSpec((1, D), lambda i: (0, 0))],
        out_specs=pl.BlockSpec((tm, D), lambda i: (i, 0)),
        out_shape=jax.ShapeDtypeStruct((M, D), F32),
        compiler_params=_cparams(("parallel",)),
        name="cross_attention",
    )(x, kv, wq, wo, gpre, gpost)


def _mlp_kernel(x_ref, w1_ref, w2_ref, gpre_ref, gpost_ref, o_ref, h_ref, acc_ref):
    j = pl.program_id(1)

    @pl.when(j == 0)
    def _():
        h_ref[...] = _rms(x_ref[...], gpre_ref[...]).astype(BF16)
        acc_ref[...] = jnp.zeros(acc_ref.shape, F32)

    a = jnp.maximum(_dot(h_ref[...], w1_ref[...]), 0.0)
    acc_ref[...] += _dot((a * a).astype(BF16), w2_ref[...])

    @pl.when(j == pl.num_programs(1) - 1)
    def _():
        o_ref[...] = x_ref[...] + _rms(acc_ref[...], gpost_ref[...])


def mlp(x, w1, w2, gpre, gpost, *, tm, tf):
    M, D = x.shape
    FF = w1.shape[1]
    return pl.pallas_call(
        _mlp_kernel,
        grid=(M // tm, FF // tf),
        in_specs=[pl.BlockSpec((tm, D), lambda i, j: (i, 0)),
                  pl.BlockSpec((D, tf), lambda i, j: (0, j)),
                  pl.BlockSpec((tf, D), lambda i, j: (j, 0)),
                  pl.BlockSpec((1, D), lambda i, j: (0, 0)),
                  pl.BlockSpec((1, D), lambda i, j: (0, 0))],
        out_specs=pl.BlockSpec((tm, D), lambda i, j: (i, 0)),
        out_shape=jax.ShapeDtypeStruct((M, D), F32),
        scratch_shapes=[pltpu.VMEM((tm, D), BF16), pltpu.VMEM((tm, D), F32)],
        compiler_params=_cparams(("parallel", "arbitrary")),
        name="mlp",
    )(x, w1, w2, gpre, gpost)


def _rot_half_cols(w):
    half = w.shape[-1] // 2
    return jnp.concatenate([-w[..., half:], w[..., :half]], axis=-1)


def _pack_w_in(w):
    widths = (MLA_Q_LORA, MLA_KV_LORA, MLA_ROPE, 1024, 1024, 1024, IDX_HEADS * IDX_DIM, IDX_DIM, IDX_HEADS,
              1024, 256, 256, 256, 256, 256, 256, 3 * NSA_HEADS, N_BRANCH * D_MODEL)
    offs = np.concatenate([[0], np.cumsum(widths)])
    seg = [w[:, offs[i]:offs[i + 1]] for i in range(len(widths))]
    (q_lat, kv_lat, k_rope, dsa_q, dsa_k, dsa_v, idx_q, idx_k, idx_w,
     nsa_q, kc, vc, ks, vs, kw, vw, nsa_gate, bgate) = seg
    z = lambda n: jnp.zeros((w.shape[0], n), w.dtype)
    cols = [dsa_q, dsa_k, dsa_v, nsa_q, bgate, idx_q, kc, vc, ks, vs, kw, vw,
            q_lat, kv_lat, k_rope, z(LANE - MLA_ROPE), _rot_half_cols(k_rope), z(LANE - MLA_ROPE), z(LANE),
            idx_k, idx_k, idx_w, z(LANE - IDX_HEADS), nsa_gate, z(LANE - 3 * NSA_HEADS)]
    out = jnp.concatenate(cols, axis=1)
    out = jnp.concatenate([out, z(N_PACK - out.shape[1])], axis=1)
    return out.astype(BF16)


def _pack_mla_weights(w_uq, w_ukv):
    H = MLA_HEADS
    uq = w_uq.reshape(MLA_Q_LORA, H, MLA_NOPE + MLA_ROPE)
    nope = uq[:, :, :MLA_NOPE].reshape(MLA_Q_LORA, H * MLA_NOPE)
    rope = uq[:, :, MLA_NOPE:]
    pad = jnp.zeros((MLA_Q_LORA, H, LANE - MLA_ROPE), w_uq.dtype)
    wqr = jnp.concatenate([rope, pad], axis=-1).reshape(MLA_Q_LORA, H * LANE)
    wqrr = jnp.concatenate([_rot_half_cols(rope), pad], axis=-1).reshape(MLA_Q_LORA, H * LANE)
    ukv = w_ukv.reshape(MLA_KV_LORA, H, MLA_NOPE + MLA_V)
    wkv = jnp.concatenate([ukv[:, :, :MLA_NOPE].reshape(MLA_KV_LORA, H * MLA_NOPE),
                           ukv[:, :, MLA_NOPE:].reshape(MLA_KV_LORA, H * MLA_V)], axis=1)
    return nope.astype(BF16), wqr.astype(BF16), wqrr.astype(BF16), wkv.astype(BF16)


def _constants(S):
    half = MLA_ROPE // 2
    inv_freq = ROPE_BASE ** (-np.arange(0, MLA_ROPE, 2, dtype=np.float32) / MLA_ROPE)
    freq = np.zeros((1, LANE), np.float32)
    freq[0, :half] = inv_freq
    freq[0, half:2 * half] = inv_freq
    ncp = S // CMP_STRIDE
    ns = S // SEL_BLOCK
    c_start = np.arange(ncp) * CMP_STRIDE
    j_start = np.arange(ns) * SEL_BLOCK
    overlap = ((c_start[None, :] < j_start[:, None] + SEL_BLOCK) &
               (c_start[None, :] + CMP_LEN > j_start[:, None])).astype(np.float32)
    overlap[:, ncp - 1] = 0.0
    ovt = np.zeros((LANE, ncp), np.float32)
    ovt[:ns] = overlap
    ind = np.zeros((S, LANE), np.float32)
    ind[np.arange(S), np.arange(S) // SEL_BLOCK] = 1.0
    return jnp.asarray(freq), jnp.asarray(ovt, BF16), jnp.asarray(ind, BF16)


def kernel(x, mem, positions, rel_bias, norm_gains, w_in, mla_q_norm, mla_kv_norm, mla_w_uq, mla_w_ukv,
           nsa_cmp_pos, nsa_cmp_w1, nsa_cmp_w2, w_branch, w_out, cross_wq, cross_wkv, cross_wo, mlp_w1, mlp_w2):
    B, S, D = x.shape
    M = B * S
    depth = w_in.shape[0]
    tq = TQ if S % TQ == 0 else TQ_NSA
    tqn = TQ_NSA
    assert S % tq == 0 and S % tqn == 0 and D == D_MODEL
    topk = min(DSA_TOPK_MAX, S // 4)
    freq, ovt, ind = _constants(S)
    btiles = bias_tiles(rel_bias, tq, 0, DSA_HEADS, nspan=2, masked=False, stack=1)
    btiles_n = bias_tiles(rel_bias, tqn, DSA_HEADS, NSA_HEADS, nspan=WINDOW // tqn + 1, masked=True,
                          stack=NSA_HG // NSA_STREAMS)
    pos = positions.reshape(M, 1)
    xf = x.reshape(M, D)
    memf = mem.reshape(B * mem.shape[1], D)
    tm_big = 1024 if M % 1024 == 0 else tq
    tm = 512 if M % 512 == 0 else tq
    ncp = S // CMP_STRIDE
    row = lambda v: v.reshape(1, -1)

    for l in range(depth):
        g = norm_gains[l]
        proj = norm_matmul(xf, row(g[0]), _pack_w_in(w_in[l]), tm=tm_big, tn=TN_IN)

        wqn, wqr, wqrr, wkv = _pack_mla_weights(mla_w_uq[l], mla_w_ukv[l])
        qn, qr, kn, vv, kr = mla_prep(proj, pos, freq, row(mla_q_norm[l]), row(mla_kv_norm[l]),
                                      wqn, wqr, wqrr, wkv, tm=tm_big)
        o_a = mla_attention(qn, qr, kn, kr, vv, B=B, S=S, tq=tq)

        maskb = dsa_index_mask(proj, B=B, S=S, tqi=TQ_IDX, tk=tq, topk=topk)
        o_b = dsa_attention(proj, maskb, btiles, B=B, S=S, tq=tq)

        kvc_in = jnp.stack([
            proj[:, OFF_NSA_KV + a * 256:OFF_NSA_KV + (a + 1) * 256].reshape(B * ncp, CMP_STRIDE * 256)
            for a in range(2)])
        posflat = jnp.broadcast_to(nsa_cmp_pos[l].reshape(2, 1, CMP_LEN * NSA_HEAD_DIM),
                                   (2, 8, CMP_LEN * NSA_HEAD_DIM)).astype(BF16)
        kvc = nsa_compress(kvc_in, posflat, nsa_cmp_w1[l].astype(BF16), nsa_cmp_w2[l].astype(BF16), B=B, S=S)
        o_cmp, selneg = nsa_cmp_attention(proj, kvc, ovt, B=B, S=S, tq=TQ_CMP if S % TQ_CMP == 0 else tq)
        o_c = nsa_main(proj, selneg, ind, btiles_n, o_cmp, B=B, S=S, tq=tqn)

        xf = merge_branches(xf, o_a, o_b, o_c, proj, w_branch[l].astype(BF16), w_out[l].astype(BF16),
                            row(g[1]), tm=tm)

        mkv = norm_matmul(memf, row(g[3]), cross_wkv[l].astype(BF16), tm=memf.shape[0] // B, tn=1024)
        xf = cross_attention(xf, mkv, cross_wq[l].astype(BF16), cross_wo[l].astype(BF16),
                             row(g[2]), row(g[4]), S=S, tm=tm_big)

        xf = mlp(xf, mlp_w1[l].astype(BF16), mlp_w2[l].astype(BF16), row(g[5]), row(g[6]), tm=tm_big, tf=1024)

    return xf.reshape(B, S, D)
```

```python
import functools
import math

import numpy as np
import jax
import jax.numpy as jnp
from jax import lax
from jax.experimental import pallas as pl
from jax.experimental.pallas import tpu as pltpu

F32 = jnp.float32
BF16 = jnp.bfloat16

D_MODEL = 1024
EPS = 1e-6
MLA_HEADS = 8
MLA_Q_LORA = 384
MLA_KV_LORA = 256
MLA_NOPE = 128
MLA_ROPE = 64
MLA_V = 128
ROPE_BASE = 10000.0
DSA_HEADS = 8
DSA_HEAD_DIM = 128
IDX_HEADS = 8
IDX_DIM = 64
DSA_TOPK_MAX = 256
NSA_HEADS = 8
NSA_GROUPS = 2
NSA_HG = NSA_HEADS // NSA_GROUPS
NSA_HEAD_DIM = 128
CMP_STRIDE = 16
CMP_LEN = 2 * CMP_STRIDE
CMP_HIDDEN = 256
SEL_BLOCK = 64
SEL_COUNT = 16
WINDOW = 512
REL_BUCKETS = 32
REL_MAX_DIST = 128
REL_HEADS = DSA_HEADS + NSA_HEADS
CROSS_HEADS = 4
CROSS_HEAD_DIM = D_MODEL // CROSS_HEADS
D_FF = 4 * D_MODEL
N_BRANCH = 3

LANE = 128
NEG = -1e30
VMEM_LIMIT = 48 * 1024 * 1024

TQ = 512
TQ_NSA = 256
TQ_CMP = 1024
HPS = 2
FAR_GROUP = 4
NEAR = 4
NEAR_NSA = 4
FAR_GROUP_NSA = 8
NSA_STREAMS = 2
BISECT_ITERS = 32
IDX_TIE_CHECK = 16
TQ_IDX = 512
IDX_CHUNK = 256

OFF_DSA_Q = 0
OFF_DSA_K = 1024
OFF_DSA_V = 2048
OFF_NSA_Q = 3072
OFF_BGATE = 4096
OFF_IDX_Q = 7168
OFF_NSA_KV = 7680
OFF_MLA = 9216
OFF_IDX_K = 10240
OFF_IDX_W = 10368
OFF_NSA_GATE = 10496
N_PACK = 10752
TN_IN = 2688


def _cparams(sem):
    return pltpu.CompilerParams(dimension_semantics=sem, vmem_limit_bytes=VMEM_LIMIT)


def _rms(x, g):
    return x * lax.rsqrt(jnp.mean(x * x, axis=-1, keepdims=True) + EPS) * g


def _dot(a, b):
    return jnp.dot(a, b, preferred_element_type=F32)


def _dot_nt(a, b):
    return lax.dot_general(a, b, (((1,), (1,)), ((), ())), preferred_element_type=F32)


def _norm_mm_kernel(x_ref, g_ref, w_ref, o_ref, xn_ref):
    @pl.when(pl.program_id(1) == 0)
    def _():
        xn_ref[...] = _rms(x_ref[...].astype(F32), g_ref[...]).astype(BF16)

    o_ref[...] = _dot(xn_ref[...], w_ref[...]).astype(o_ref.dtype)


def norm_matmul(x, g, w, *, tm, tn):
    M, K = x.shape
    N = w.shape[1]
    return pl.pallas_call(
        _norm_mm_kernel,
        grid=(M // tm, N // tn),
        in_specs=[pl.BlockSpec((tm, K), lambda i, j: (i, 0)),
                  pl.BlockSpec((1, K), lambda i, j: (0, 0)),
                  pl.BlockSpec((K, tn), lambda i, j: (0, j))],
        out_specs=pl.BlockSpec((tm, tn), lambda i, j: (i, j)),
        out_shape=jax.ShapeDtypeStruct((M, N), BF16),
        scratch_shapes=[pltpu.VMEM((tm, K), BF16)],
        compiler_params=_cparams(("parallel", "arbitrary")),
        name="norm_matmul",
    )(x, g, w)


def _bias_tiles_kernel(tab_ref, bkt_ref, o_ref, *, inv_scale, h0, tq, nspan, masked):
    h = pl.program_id(0) + h0
    bkt = bkt_ref[...]
    far = tab_ref[REL_BUCKETS - 1, h]
    out = jnp.zeros(bkt.shape, F32)
    for b in range(REL_BUCKETS - 1):
        out = jnp.where(bkt == b, (tab_ref[b, h] - far) * inv_scale, out)
    diag, prev = out[0], out[1]
    r, c = _tri(tq, tq)
    if masked:
        diag = jnp.where(r >= c, diag, NEG)
    o_ref[:, (nspan - 1) * tq:nspan * tq] = diag
    o_ref[:, (nspan - 2) * tq:(nspan - 1) * tq] = prev
    for k in range(nspan - 2):
        edge = jnp.where(r < c, 0.0, NEG) if (masked and k == 0) else jnp.zeros((tq, tq), F32)
        o_ref[:, k * tq:(k + 1) * tq] = edge


def _bucket_tiles(tq):
    i = np.arange(tq)[:, None]
    j = np.arange(tq)[None, :]
    d = np.stack([i - j, tq + i - j]).astype(np.int32)
    n = jnp.maximum(jnp.asarray(d), 0)
    exact = REL_BUCKETS // 2
    nf = jnp.maximum(n, 1).astype(F32)
    log_b = exact + (jnp.log(nf / exact) / math.log(REL_MAX_DIST / exact) * (REL_BUCKETS - exact)).astype(jnp.int32)
    return jnp.where(n < exact, n, jnp.minimum(log_b, REL_BUCKETS - 1)).astype(jnp.int32)


def bias_tiles(rel_bias, tq, h0, nh, *, nspan, masked, stack):
    assert tq >= REL_MAX_DIST
    bkt = _bucket_tiles(tq)
    assert DSA_HEAD_DIM == NSA_HEAD_DIM
    return pl.pallas_call(
        functools.partial(_bias_tiles_kernel, inv_scale=DSA_HEAD_DIM ** 0.5, h0=h0, tq=tq, nspan=nspan,
                          masked=masked),
        grid=(nh,),
        in_specs=[pl.BlockSpec(memory_space=pltpu.SMEM),
                  pl.BlockSpec((2, tq, tq), lambda h: (0, 0, 0))],
        out_specs=pl.BlockSpec((None, tq, nspan * tq), lambda h: (h // stack, h % stack, 0)),
        out_shape=jax.ShapeDtypeStruct((nh // stack, stack * tq, nspan * tq), F32),
        compiler_params=_cparams(("arbitrary",)),
        name="bias_tiles",
    )(rel_bias, bkt)


def _mla_prep_kernel(lat_ref, pos_ref, freq_ref, qg_ref, kvg_ref, wqn_ref, wqr_ref, wqrr_ref, wkv_ref,
                     qn_ref, qr_ref, kn_ref, v_ref, kr_ref):
    lat = lat_ref[...].astype(F32)
    qlat = _rms(lat[:, :MLA_Q_LORA], qg_ref[...]).astype(BF16)
    kvlat = _rms(lat[:, MLA_Q_LORA:MLA_Q_LORA + MLA_KV_LORA], kvg_ref[...]).astype(BF16)
    o = MLA_Q_LORA + MLA_KV_LORA
    kr = lat[:, o:o + LANE]
    krr = lat[:, o + LANE:o + 2 * LANE]
    ang = pos_ref[...].astype(F32) * freq_ref[...]
    cos, sin = jnp.cos(ang), jnp.sin(ang)
    cos8 = jnp.concatenate([cos] * MLA_HEADS, axis=1)
    sin8 = jnp.concatenate([sin] * MLA_HEADS, axis=1)
    qn_ref[...] = _dot(qlat, wqn_ref[...]).astype(BF16)
    qr_ref[...] = (_dot(qlat, wqr_ref[...]) * cos8 + _dot(qlat, wqrr_ref[...]) * sin8).astype(BF16)
    kv = _dot(kvlat, wkv_ref[...])
    nk = MLA_HEADS * MLA_NOPE
    kn_ref[...] = kv[:, :nk].astype(BF16)
    v_ref[...] = kv[:, nk:].astype(BF16)
    kr_ref[...] = (kr * cos + krr * sin).astype(BF16)


def mla_prep(proj, pos, freq, qg, kvg, wqn, wqr, wqrr, wkv, *, tm):
    M = proj.shape[0]
    HD = MLA_HEADS * LANE
    full = lambda a: pl.BlockSpec(a.shape, lambda i: (0, 0))
    outs = [jax.ShapeDtypeStruct((M, HD), BF16)] * 4 + [jax.ShapeDtypeStruct((M, LANE), BF16)]
    return pl.pallas_call(
        _mla_prep_kernel,
        grid=(M // tm,),
        in_specs=[pl.BlockSpec((tm, 1024), lambda i: (i, OFF_MLA // 1024)),
                  pl.BlockSpec((tm, 1), lambda i: (i, 0)),
                  full(freq), full(qg), full(kvg), full(wqn), full(wqr), full(wqrr), full(wkv)],
        out_specs=[pl.BlockSpec((tm, HD), lambda i: (i, 0))] * 4 + [pl.BlockSpec((tm, LANE), lambda i: (i, 0))],
        out_shape=outs,
        compiler_params=_cparams(("parallel",)),
        name="mla_prep",
    )(proj, pos, freq, qg, kvg, wqn, wqr, wqrr, wkv)


def _softmax_init(m_ref, acc_ref):
    m_ref[...] = jnp.full(m_ref.shape, NEG, F32)
    acc_ref[...] = jnp.zeros(acc_ref.shape, F32)


def _with_ones(v):
    return jnp.concatenate([v, jnp.ones(v.shape, v.dtype)], axis=1)


def _softmax_step(s, v, m_ref, acc_ref, scale):
    c = scale * math.log2(math.e)
    m_prev = m_ref[...]
    m_new = jnp.maximum(m_prev, jnp.max(s, axis=-1, keepdims=True))
    alpha = jnp.exp2((m_prev - m_new) * c)
    p = jnp.exp2((s - jnp.tile(m_new, (1, s.shape[1] // LANE))) * c)
    acc_ref[...] = jnp.tile(alpha, (1, 2)) * acc_ref[...] + _dot(p.astype(BF16), _with_ones(v))
    m_ref[...] = m_new


def _softmax_finish(acc_ref):
    acc = acc_ref[...]
    return acc[:, :LANE] / acc[:, LANE:]


def _far_chunks(n, step, largest):
    def body(p, c):
        step(largest * p, largest)
        return c

    lax.fori_loop(0, n // largest, body, 0)
    w = largest // 2
    while w >= 1:
        @pl.when(n % (2 * w) >= w)
        def _(w=w):
            step(n // (2 * w) * (2 * w), w)
        w //= 2


def _tri(tq, tk):
    row = lax.broadcasted_iota(jnp.int32, (tq, tk), 0)
    col = lax.broadcasted_iota(jnp.int32, (tq, tk), 1)
    return row, col


def _mla_attn_kernel(qn_ref, qr_ref, kn_ref, kr_ref, v_ref, o_ref, m_ref, acc_ref, *, tq, scale):
    qi = pl.program_id(2)
    hs = range(HPS)
    q = [jnp.concatenate([qn_ref[:, h * LANE:(h + 1) * LANE], qr_ref[:, h * LANE:(h + 1) * LANE]], axis=-1)
         for h in hs]
    for h in hs:
        _softmax_init(m_ref.at[h], acc_ref.at[h])

    def step(j, nc, causal_last):
        sl = pl.ds(j * tq if isinstance(j, int) else pl.multiple_of(j * tq, tq), nc * tq)
        kr = kr_ref[sl, :]
        if causal_last:
            row, col = _tri(tq, nc * tq)
            ok = col <= row + (nc - 1) * tq
        s = [_dot_nt(q[h], jnp.concatenate([kn_ref[sl, h * LANE:(h + 1) * LANE], kr], axis=-1)) for h in hs]
        for h in hs:
            sh = jnp.where(ok, s[h], NEG) if causal_last else s[h]
            _softmax_step(sh, v_ref[sl, h * LANE:(h + 1) * LANE], m_ref.at[h], acc_ref.at[h], scale)

    _far_chunks(jnp.maximum(qi - (NEAR - 1), 0), lambda j, nc: step(j, nc, False), FAR_GROUP)
    for k in range(NEAR - 1):
        @pl.when(qi == k)
        def _(k=k):
            step(0, k + 1, True)

    @pl.when(qi >= NEAR - 1)
    def _():
        step(qi - (NEAR - 1), NEAR, True)

    for h in hs:
        o_ref[:, h * LANE:(h + 1) * LANE] = _softmax_finish(acc_ref.at[h]).astype(o_ref.dtype)


def mla_attention(qn, qr, kn, kr, v, *, B, S, tq):
    M = B * S
    nq = S // tq
    H = MLA_HEADS
    scale = (MLA_NOPE + MLA_ROPE) ** -0.5
    w = HPS * LANE
    qspec = pl.BlockSpec((tq, w), lambda b, h, i: (b * nq + i, h))
    kspec = pl.BlockSpec((S, w), lambda b, h, i: (b, h))
    return pl.pallas_call(
        functools.partial(_mla_attn_kernel, tq=tq, scale=scale),
        grid=(B, H // HPS, nq),
        in_specs=[qspec, qspec, kspec, pl.BlockSpec((S, LANE), lambda b, h, i: (b, 0)), kspec],
        out_specs=qspec,
        out_shape=jax.ShapeDtypeStruct((M, H * MLA_V), BF16),
        scratch_shapes=[pltpu.VMEM((HPS, tq, LANE), F32), pltpu.VMEM((HPS, tq, 2 * LANE), F32)],
        compiler_params=_cparams(("parallel", "parallel", "arbitrary")),
        name="mla_attention",
    )(qn, qr, kn, kr, v)


def _dsa_index_kernel(iq_ref, ik_ref, iw_ref, tri_ref, mb_ref, sc_ref, *, tq, tk, S, topk, iters):
    t0 = pl.program_id(1) * tq
    nk = S // tk
    nch = (t0 + tq - 1) // tk + 1
    hw = IDX_CHUNK
    iq = iq_ref[...]
    lane = lax.broadcasted_iota(jnp.int32, (tq, LANE), 1)
    qs = []
    for h in range(IDX_HEADS):
        blk = iq[:, (h // 2) * LANE:(h // 2 + 1) * LANE]
        keep = (lane >= IDX_DIM) if h % 2 else (lane < IDX_DIM)
        qs.append(jnp.where(keep, blk, jnp.zeros_like(blk)))
    q8 = jnp.concatenate(qs, axis=0)
    iwt = iw_ref[...].astype(F32).T
    wrow = [iwt[h:h + 1, :] for h in range(IDX_HEADS)]
    qpos = lax.broadcasted_iota(jnp.int32, (hw, tq), 1) + t0
    krow = lax.broadcasted_iota(jnp.int32, (hw, tq), 0)

    def fold(x, op):
        out = x[0:8]
        for r in range(1, x.shape[0] // 8):
            out = op(out, x[r * 8:(r + 1) * 8])
        return out

    def score_chunk(c, carry):
        mn, mx = carry
        for half in range(tk // hw):
            k0 = pl.multiple_of(c * tk + half * hw, hw)
            res = _dot_nt(ik_ref[pl.ds(k0, hw), :], q8)
            sc = wrow[0] * jnp.maximum(res[:, 0:tq], 0.0)
            for h in range(1, IDX_HEADS):
                sc = sc + wrow[h] * jnp.maximum(res[:, h * tq:(h + 1) * tq], 0.0)
            causal = (krow + k0) <= qpos
            mn = jnp.minimum(mn, fold(jnp.where(causal, sc, -NEG), jnp.minimum))
            sc = jnp.where(causal, sc, NEG)
            mx = jnp.maximum(mx, fold(sc, jnp.maximum))
            sc_ref[c, half * hw:(half + 1) * hw, :] = sc
        return mn, mx

    mn, mx = lax.fori_loop(0, nch, score_chunk, (jnp.full((8, tq), -NEG, F32), jnp.full((8, tq), NEG, F32)))
    n_causal = (lax.broadcasted_iota(jnp.int32, (1, tq), 1) + (t0 + 1)).astype(F32)
    kf = jnp.minimum(float(topk), n_causal)

    def count(pred):
        def body(c, acc):
            return acc + fold(jnp.where(pred(sc_ref[c]), 1.0, 0.0), jnp.add)

        return jnp.sum(lax.fori_loop(0, nch, body, jnp.zeros((8, tq), F32)), axis=0, keepdims=True)

    lo0 = jnp.min(mn, axis=0, keepdims=True)
    mx = jnp.max(mx, axis=0, keepdims=True)
    hi0 = mx + (jnp.abs(mx) * 1e-6 + 1.0)

    def bisect(c):
        lo, hi, n_lo, n_hi = c
        mid = 0.5 * (lo + hi)
        n_mid = count(lambda blk: blk >= mid)
        ge = n_mid >= kf
        return (jnp.where(ge, mid, lo), jnp.where(ge, hi, mid),
                jnp.where(ge, n_mid, n_lo), jnp.where(ge, n_hi, n_mid))

    def open_rows(n_lo, n_hi, settled):
        return (n_lo - kf) * jnp.where(n_lo - n_hi > 2.0, 1.0, 0.0) * (1.0 - settled)

    def search(state, limit, settled):
        def cond(c):
            return jnp.logical_and(c[0] < limit, jnp.max(open_rows(c[3], c[4], settled)) > 0.0)

        def body(c):
            return (c[0] + 2,) + bisect(bisect(c[1:]))

        return lax.while_loop(cond, body, state)

    def band_min_max(lo, hi):
        def body(c, carry):
            blk = sc_ref[c]
            inside = (blk >= lo) & (blk < hi)
            return (jnp.minimum(carry[0], fold(jnp.where(inside, blk, -NEG), jnp.minimum)),
                    jnp.maximum(carry[1], fold(jnp.where(inside, blk, NEG), jnp.maximum)))

        bmin, bmax = lax.fori_loop(0, nch, body, (jnp.full((8, tq), -NEG, F32), jnp.full((8, tq), NEG, F32)))
        return jnp.min(bmin, axis=0, keepdims=True), jnp.max(bmax, axis=0, keepdims=True)

    def finish(state):
        _, lo, hi, n_lo, n_hi = state
        over = n_lo > kf
        bmin, bmax = lax.cond(jnp.max(n_lo - kf) > 0.0, lambda: band_min_max(lo, hi), lambda: (lo, lo))
        single = bmin == bmax
        split = over & (n_lo - n_hi == 2.0) & jnp.logical_not(single)
        unsplit = over & jnp.logical_not(split)
        return jnp.where(split, bmax, lo), jnp.where(over & single, 1.0, 0.0), jnp.where(unsplit, 1.0, 0.0)

    zero = jnp.zeros((1, tq), F32)
    state1 = search((jnp.int32(0), lo0, hi0, n_causal, zero), IDX_TIE_CHECK, zero)
    first = finish(state1)
    state = search(state1, iters, first[1])
    lo, _, unsplit = lax.cond(state[0] > state1[0], lambda: finish(state), lambda: first)
    hi, n_hi = state[2], state[4]
    tied = jnp.max(unsplit) > 0.0

    def fill(j, c):
        mb_ref[j] = jnp.full((tq, tk), NEG, mb_ref.dtype)
        return c

    lax.fori_loop(nch, nk, fill, 0)

    def emit(j, keep):
        mb_ref[j] = jnp.where(keep, 0.0, NEG).T.astype(mb_ref.dtype)

    @pl.when(jnp.logical_not(tied))
    def _():
        def body(j, c):
            emit(j, sc_ref[j] >= lo)
            return c

        lax.fori_loop(0, nch, body, 0)

    @pl.when(tied)
    def _():
        need = kf - n_hi
        tri = tri_ref[...]

        def body(j, base):
            sc = sc_ref[j]
            above = sc >= hi
            band = (sc >= lo) & jnp.logical_not(above)
            prefix = _dot(tri, jnp.where(band, 1.0, 0.0).astype(BF16)) + base
            emit(j, above | (band & (prefix <= need)))
            return prefix[tk - 1:tk, :]

        lax.fori_loop(0, nch, body, zero)


def dsa_index_mask(proj, *, B, S, tqi, tk, topk):
    nq = S // tqi
    nk = S // tk
    assert tk % tqi == 0 and tk % IDX_CHUNK == 0
    tri = jnp.asarray(np.tril(np.ones((tk, tk), np.float32)), BF16)
    return pl.pallas_call(
        functools.partial(_dsa_index_kernel, tq=tqi, tk=tk, S=S, topk=topk, iters=BISECT_ITERS),
        grid=(B, nq),
        in_specs=[pl.BlockSpec((tqi, 512), lambda b, i: (b * nq + i, OFF_IDX_Q // 512)),
                  pl.BlockSpec((S, LANE), lambda b, i: (b, OFF_IDX_K // LANE)),
                  pl.BlockSpec((tqi, LANE), lambda b, i: (b * nq + i, OFF_IDX_W // LANE)),
                  pl.BlockSpec((tk, tk), lambda b, i: (0, 0))],
        out_specs=pl.BlockSpec((None, nk, tqi, tk), lambda b, i: (b, 0, i, 0)),
        out_shape=jax.ShapeDtypeStruct((B, nk, S, tk), BF16),
        scratch_shapes=[pltpu.VMEM((nk, tk, tqi), F32)],
        compiler_params=_cparams(("parallel", "parallel")),
        name="dsa_index_mask",
    )(proj, proj, proj, tri)


def _dsa_attn_kernel(q_ref, k_ref, v_ref, mb_ref, bias_ref, o_ref, m_ref, acc_ref, *, tq, scale):
    qi = pl.program_id(2)
    hs = range(HPS)
    cols = lambda h: slice(h * LANE, (h + 1) * LANE)
    q = [q_ref[:, cols(h)] for h in hs]
    for h in hs:
        _softmax_init(m_ref.at[h], acc_ref.at[h])

    def step(j, nc, bias):
        sl = pl.ds(j * tq if isinstance(j, int) else pl.multiple_of(j * tq, tq), nc * tq)
        mb = mb_ref[j] if nc == 1 else jnp.concatenate([mb_ref[j + i] for i in range(nc)], axis=1)
        mb = mb.astype(F32)
        s = [_dot_nt(q[h], k_ref[sl, cols(h)]) + mb for h in hs]
        for h in hs:
            sh = s[h]
            if bias and nc == 1:
                sh = sh + bias_ref[h, :, tq:2 * tq]
            elif bias and nc == 2:
                sh = sh + bias_ref[h]
            elif bias:
                sh = jnp.concatenate([sh[:, :(nc - 2) * tq], sh[:, (nc - 2) * tq:] + bias_ref[h]], axis=1)
            _softmax_step(sh, v_ref[sl, cols(h)], m_ref.at[h], acc_ref.at[h], scale)

    _far_chunks(jnp.maximum(qi - (NEAR - 1), 0), lambda j, nc: step(j, nc, False), FAR_GROUP)
    for k in range(NEAR - 1):
        @pl.when(qi == k)
        def _(k=k):
            step(0, k + 1, True)

    @pl.when(qi >= NEAR - 1)
    def _():
        step(qi - (NEAR - 1), NEAR, True)

    for h in hs:
        o_ref[:, cols(h)] = _softmax_finish(acc_ref.at[h]).astype(o_ref.dtype)


def dsa_attention(proj, maskb, btiles, *, B, S, tq):
    M = B * S
    nq = S // tq
    H = DSA_HEADS
    scale = DSA_HEAD_DIM ** -0.5
    w = HPS * LANE
    return pl.pallas_call(
        functools.partial(_dsa_attn_kernel, tq=tq, scale=scale),
        grid=(B, H // HPS, nq),
        in_specs=[pl.BlockSpec((tq, w), lambda b, h, i: (b * nq + i, OFF_DSA_Q // w + h)),
                  pl.BlockSpec((S, w), lambda b, h, i: (b, OFF_DSA_K // w + h)),
                  pl.BlockSpec((S, w), lambda b, h, i: (b, OFF_DSA_V // w + h)),
                  pl.BlockSpec((None, nq, tq, tq), lambda b, h, i: (b, 0, i, 0)),
                  pl.BlockSpec((HPS, tq, 2 * tq), lambda b, h, i: (h, 0, 0))],
        out_specs=pl.BlockSpec((tq, w), lambda b, h, i: (b * nq + i, h)),
        out_shape=jax.ShapeDtypeStruct((M, H * DSA_HEAD_DIM), BF16),
        scratch_shapes=[pltpu.VMEM((HPS, tq, LANE), F32), pltpu.VMEM((HPS, tq, 2 * LANE), F32)],
        compiler_params=_cparams(("parallel", "parallel", "arbitrary")),
        name="dsa_attention",
    )(proj, proj, proj, maskb, btiles)


def _nsa_compress_kernel(x_ref, pos_ref, w1_ref, w2_ref, o_ref, *, ncp):
    dk = NSA_HEAD_DIM
    posw = _dot(pos_ref[...], w1_ref[...])[0:1, :]
    for g in range(NSA_GROUPS):
        lo = jnp.zeros((ncp, CMP_HIDDEN), F32)
        hi = jnp.zeros((ncp, CMP_HIDDEN), F32)
        for l in range(CMP_STRIDE):
            xs = x_ref[:, (l * NSA_GROUPS + g) * dk:(l * NSA_GROUPS + g + 1) * dk]
            lo = lo + _dot(xs, w1_ref[l * dk:(l + 1) * dk, :])
            hi = hi + _dot(xs, w1_ref[(CMP_STRIDE + l) * dk:(CMP_STRIDE + l + 1) * dk, :])
        hid = lo + pltpu.roll(hi, ncp - 1, 0) + posw
        o_ref[g] = _dot(jax.nn.gelu(hid).astype(BF16), w2_ref[...]).astype(o_ref.dtype)


def nsa_compress(xkv, posflat, w1, w2, *, B, S):
    ncp = S // CMP_STRIDE
    G, dk = NSA_GROUPS, NSA_HEAD_DIM
    width = CMP_STRIDE * G * dk
    return pl.pallas_call(
        functools.partial(_nsa_compress_kernel, ncp=ncp),
        grid=(2, B),
        in_specs=[pl.BlockSpec((None, ncp, width), lambda a, b: (a, b, 0)),
                  pl.BlockSpec((None, 8, CMP_LEN * dk), lambda a, b: (a, 0, 0)),
                  pl.BlockSpec((None, CMP_LEN * dk, CMP_HIDDEN), lambda a, b: (a, 0, 0)),
                  pl.BlockSpec((None, CMP_HIDDEN, dk), lambda a, b: (a, 0, 0))],
        out_specs=pl.BlockSpec((None, None, G, ncp, dk), lambda a, b: (a, b, 0, 0, 0)),
        out_shape=jax.ShapeDtypeStruct((2, B, G, ncp, dk), BF16),
        compiler_params=_cparams(("parallel", "parallel")),
        name="nsa_compress",
    )(xkv, posflat, w1, w2)


def _nsa_cmp_kernel(q_ref, kc_ref, vc_ref, ov_ref, oc_ref, sel_ref, *, tq, ncp, ns, n_sel, scale):
    t0 = pl.program_id(2) * tq
    dk = NSA_HEAD_DIM
    kc = kc_ref[...]
    vc = vc_ref[...]
    trow = lax.broadcasted_iota(jnp.int32, (tq, ncp), 0) + t0
    ccol = lax.broadcasted_iota(jnp.int32, (tq, ncp), 1)
    vis = (ccol * CMP_STRIDE + (CMP_LEN - 1)) <= trow
    psum = jnp.zeros((tq, ncp), F32)
    for h in range(NSA_HG):
        s = jnp.where(vis, _dot_nt(q_ref[:, h * dk:(h + 1) * dk], kc) * scale, NEG)
        m = jnp.max(s, axis=-1, keepdims=True)
        p = jnp.where(vis, jnp.exp(s - m), 0.0)
        d = jnp.sum(p, axis=-1, keepdims=True)
        p = p / jnp.where(d > 0, d, 1.0)
        oc_ref[:, h * dk:(h + 1) * dk] = _dot(p.astype(BF16), vc).astype(oc_ref.dtype)
        psum = psum + p
    p_hi = psum.astype(BF16)
    p_lo = (psum - p_hi.astype(F32)).astype(BF16)
    ov = ov_ref[...]
    imp = _dot_nt(ov, p_hi) + _dot_nt(ov, p_lo)
    imp = imp[:ns]
    jrow = lax.broadcasted_iota(jnp.int32, (ns, tq), 0)
    tcol = lax.broadcasted_iota(jnp.int32, (ns, tq), 1) + t0
    blk_t = tcol // SEL_BLOCK
    forced = (jrow == 0) | (jrow == blk_t) | (jrow == blk_t - 1)
    val = jnp.where(forced, -NEG, imp)
    val = jnp.where(jrow * SEL_BLOCK > tcol, NEG, val)
    rank = jnp.zeros((ns, tq), F32)
    for j in range(ns):
        other = val[j:j + 1, :]
        ahead = (other > val) | ((other == val) & (jrow > j))
        rank = rank + jnp.where(ahead, 1.0, 0.0)
    selneg = jnp.where(rank < float(n_sel), 0.0, NEG)
    if ns < LANE:
        selneg = jnp.concatenate([selneg, jnp.zeros((LANE - ns, tq), F32)], axis=0)
    sel_ref[...] = selneg.T.astype(sel_ref.dtype)


def nsa_cmp_attention(proj, kvc, ovt, *, B, S, tq):
    M = B * S
    nq = S // tq
    G, dk = NSA_GROUPS, NSA_HEAD_DIM
    ncp = S // CMP_STRIDE
    ns = S // SEL_BLOCK
    assert ns <= LANE
    n_sel = min(SEL_COUNT, ns)
    gw = NSA_HG * dk
    return pl.pallas_call(
        functools.partial(_nsa_cmp_kernel, tq=tq, ncp=ncp, ns=ns, n_sel=n_sel, scale=dk ** -0.5),
        grid=(B, G, nq),
        in_specs=[pl.BlockSpec((tq, gw), lambda b, g, i: (b * nq + i, OFF_NSA_Q // gw + g)),
                  pl.BlockSpec((None, None, None, ncp, dk), lambda b, g, i: (0, b, g, 0, 0)),
                  pl.BlockSpec((None, None, None, ncp, dk), lambda b, g, i: (1, b, g, 0, 0)),
                  pl.BlockSpec((LANE, ncp), lambda b, g, i: (0, 0))],
        out_specs=[pl.BlockSpec((tq, gw), lambda b, g, i: (b * nq + i, g)),
                   pl.BlockSpec((None, None, tq, LANE), lambda b, g, i: (b, g, i, 0))],
        out_shape=[jax.ShapeDtypeStruct((M, G * gw), BF16),
                   jax.ShapeDtypeStruct((B, G, S, LANE), BF16)],
        compiler_params=_cparams(("parallel", "parallel", "parallel")),
        name="nsa_cmp_attention",
    )(proj, kvc, kvc, ovt)


def _nsa_main_kernel(q_ref, sel_ref, ks_ref, vs_ref, kw_ref, vw_ref, ind_ref, bias_ref, gate_ref, oc_ref,
                     o_ref, m_ref, acc_ref, *, tq, nback, scale):
    g = pl.program_id(1)
    qi = pl.program_id(2)
    dk = NSA_HEAD_DIM
    HG = NSA_HG
    ns = NSA_STREAMS
    hp = HG // ns
    st = range(ns)
    stack = lambda f: [jnp.concatenate([f(a * hp + i) for i in range(hp)], axis=0) for a in st]
    q = stack(lambda h: q_ref[:, h * dk:(h + 1) * dk])
    sel = jnp.concatenate([sel_ref[...]] * hp, axis=0)
    qa = [jnp.concatenate([q[a], sel], axis=1) for a in st]
    gate = jax.nn.sigmoid(gate_ref[...].astype(F32))

    for a in range(2 * ns):
        _softmax_init(m_ref.at[a], acc_ref.at[a])

    def steps(s, v, fix, base=0):
        for a in st:
            _softmax_step(fix(a, s[a]), v, m_ref.at[base + a], acc_ref.at[base + a], scale)

    wspan = nback + 1

    def rows_from(j, width):
        return pl.ds(j * tq if isinstance(j, int) else pl.multiple_of(j * tq, tq), width)

    def sel_scores(j, width=tq):
        sl = rows_from(j, width)
        ka = jnp.concatenate([ks_ref[sl, :], ind_ref[sl, :]], axis=1)
        return [_dot_nt(qa[a], ka) for a in st], vs_ref[sl, :]

    def sel_far(j, nc):
        s, v = sel_scores(j, nc * tq)
        steps(s, v, lambda a, x: x)

    nsel = NEAR_NSA
    _far_chunks(jnp.maximum(qi - (nsel - 1), 0), sel_far, FAR_GROUP_NSA)

    def win_scores(j, width):
        sl = rows_from(j, width)
        kw = kw_ref[sl, :]
        return [_dot_nt(q[a], kw) for a in st], vw_ref[sl, :]

    def near(first_s, nc_s, first_w, nc_w):
        tile = lambda nc: (lambda a, x: x + bias_ref[a, :, (wspan - nc) * tq:wspan * tq])

        def sel_tile(a, x):
            if nc_s <= 2:
                return tile(nc_s)(a, x)
            return jnp.concatenate([x[:, :(nc_s - 2) * tq], tile(2)(a, x[:, (nc_s - 2) * tq:])], axis=1)

        s_s, v_s = sel_scores(first_s, nc_s * tq)
        s_w, v_w = win_scores(first_w, nc_w * tq)
        steps(s_s, v_s, sel_tile)
        steps(s_w, v_w, tile(nc_w), base=ns)

    nmax = max(nsel, wspan)
    for k in range(nmax - 1):
        @pl.when(qi == k)
        def _(k=k):
            nc_s, nc_w = min(k + 1, nsel), min(k + 1, wspan)
            near(k + 1 - nc_s, nc_s, k + 1 - nc_w, nc_w)

    @pl.when(qi >= nmax - 1)
    def _():
        near(qi - (nsel - 1), nsel, qi - (wspan - 1), wspan)

    o_s = [_softmax_finish(acc_ref.at[a]) for a in st]
    o_w = [_softmax_finish(acc_ref.at[ns + a]) for a in st]

    lane = lax.broadcasted_iota(jnp.int32, gate.shape, 1)
    for h in range(HG):
        c = g * HG + h
        a, sl = h // hp, slice((h % hp) * tq, (h % hp + 1) * tq)
        gc = [jnp.sum(jnp.where(lane == br * NSA_HEADS + c, gate, 0.0), axis=-1, keepdims=True)
              for br in range(3)]
        o = gc[0] * oc_ref[:, h * dk:(h + 1) * dk].astype(F32) + gc[1] * o_s[a][sl] + gc[2] * o_w[a][sl]
        o_ref[:, h * dk:(h + 1) * dk] = o.astype(o_ref.dtype)


def nsa_main(proj, selneg, ind, btiles, o_c, *, B, S, tq):
    M = B * S
    nq = S // tq
    G, dk, HG = NSA_GROUPS, NSA_HEAD_DIM, NSA_HG
    gw = HG * dk
    assert WINDOW % tq == 0 and WINDOW // tq >= 2
    nback = WINDOW // tq
    ns, hp = NSA_STREAMS, HG // NSA_STREAMS
    kv = lambda n: pl.BlockSpec((S, dk), lambda b, g, i, n=n: (b, (OFF_NSA_KV + n * G * dk) // dk + g))
    return pl.pallas_call(
        functools.partial(_nsa_main_kernel, tq=tq, nback=nback, scale=dk ** -0.5),
        grid=(B, G, nq),
        in_specs=[pl.BlockSpec((tq, gw), lambda b, g, i: (b * nq + i, OFF_NSA_Q // gw + g)),
                  pl.BlockSpec((None, None, tq, LANE), lambda b, g, i: (b, g, i, 0)),
                  kv(2), kv(3), kv(4), kv(5),
                  pl.BlockSpec((S, LANE), lambda b, g, i: (0, 0)),
                  pl.BlockSpec((ns, hp * tq, (nback + 1) * tq), lambda b, g, i: (g, 0, 0)),
                  pl.BlockSpec((tq, LANE), lambda b, g, i: (b * nq + i, OFF_NSA_GATE // LANE)),
                  pl.BlockSpec((tq, gw), lambda b, g, i: (b * nq + i, g))],
        out_specs=pl.BlockSpec((tq, gw), lambda b, g, i: (b * nq + i, g)),
        out_shape=jax.ShapeDtypeStruct((M, G * gw), BF16),
        scratch_shapes=[pltpu.VMEM((2 * ns, hp * tq, LANE), F32),
                        pltpu.VMEM((2 * ns, hp * tq, 2 * LANE), F32)],
        compiler_params=_cparams(("parallel", "parallel", "arbitrary")),
        name="nsa_main",
    )(proj, selneg, proj, proj, proj, proj, ind, btiles, proj, o_c)


def _merge_kernel(x_ref, oa_ref, ob_ref, oc_ref, g0_ref, g1_ref, g2_ref, wb_ref, wo_ref, gain_ref, o_ref):
    merged = None
    for o_r, g_r, br in ((oa_ref, g0_ref, 0), (ob_ref, g1_ref, 1), (oc_ref, g2_ref, 2)):
        t = jax.nn.sigmoid(g_r[...].astype(F32)) * _dot(o_r[...], wb_ref[br])
        merged = t if merged is None else merged + t
    y = _dot(merged.astype(BF16), wo_ref[...])
    o_ref[...] = x_ref[...] + _rms(y, gain_ref[...])


def merge_branches(x, o_a, o_b, o_c, proj, wb, wo, gain, *, tm):
    M, D = x.shape
    row = pl.BlockSpec((tm, D), lambda i: (i, 0))
    gate = lambda br: pl.BlockSpec((tm, D), lambda i, br=br: (i, OFF_BGATE // D + br))
    return pl.pallas_call(
        _merge_kernel,
        grid=(M // tm,),
        in_specs=[row, row, row, row, gate(0), gate(1), gate(2),
                  pl.BlockSpec(wb.shape, lambda i: (0, 0, 0)),
                  pl.BlockSpec(wo.shape, lambda i: (0, 0)),
                  pl.BlockSpec((1, D), lambda i: (0, 0))],
        out_specs=row,
        out_shape=jax.ShapeDtypeStruct((M, D), F32),
        compiler_params=_cparams(("parallel",)),
        name="merge_branches",
    )(x, o_a, o_b, o_c, proj, proj, proj, wb, wo, gain)


def _cross_kernel(x_ref, kv_ref, wq_ref, wo_ref, gpre_ref, gpost_ref, o_ref, *, scale):
    x = x_ref[...]
    h = _rms(x, gpre_ref[...]).astype(BF16)
    q = _dot(h, wq_ref[...]).astype(BF16)
    dh = CROSS_HEAD_DIM
    outs = []
    for hd in range(CROSS_HEADS):
        k = kv_ref[:, hd * dh:(hd + 1) * dh]
        v = kv_ref[:, D_MODEL + hd * dh:D_MODEL + (hd + 1) * dh]
        s = _dot_nt(q[:, hd * dh:(hd + 1) * dh], k) * scale
        p = jnp.exp(s - jnp.max(s, axis=-1, keepdims=True))
        p = p / jnp.sum(p, axis=-1, keepdims=True)
        outs.append(_dot(p.astype(BF16), v).astype(BF16))
    y = _dot(jnp.concatenate(outs, axis=1), wo_ref[...])
    o_ref[...] = x + _rms(y, gpost_ref[...])


def cross_attention(x, kv, wq, wo, gpre, gpost, *, S, tm):
    M, D = x.shape
    mlen = kv.shape[0] // (M // S)
    per_b = S // tm
    return pl.pallas_call(
        functools.partial(_cross_kernel, scale=CROSS_HEAD_DIM ** -0.5),
        grid=(M // tm,),
        in_specs=[pl.BlockSpec((tm, D), lambda i: (i, 0)),
                  pl.BlockSpec((mlen, 2 * D), lambda i: (i // per_b, 0)),
                  pl.BlockSpec(wq.shape, lambda i: (0, 0)),
                  pl.BlockSpec(wo.shape, lambda i: (0, 0)),
                  pl.BlockSpec((1, D), lambda i: (0, 0)),
                  pl.BlockSpec((1, D), lambda i: (0, 0))],
        out_specs=pl.BlockSpec((tm, D), lambda i: (i, 0)),
        out_shape=jax.ShapeDtypeStruct((M, D), F32),
        compiler_params=_cparams(("parallel",)),
        name="cross_attention",
    )(x, kv, wq, wo, gpre, gpost)


def _mlp_kernel(x_ref, w1_ref, w2_ref, gpre_ref, gpost_ref, o_ref, h_ref, acc_ref):
    j = pl.program_id(1)

    @pl.when(j == 0)
    def _():
        h_ref[...] = _rms(x_ref[...], gpre_ref[...]).astype(BF16)
        acc_ref[...] = jnp.zeros(acc_ref.shape, F32)

    a = jnp.maximum(_dot(h_ref[...], w1_ref[...]), 0.0)
    acc_ref[...] += _dot((a * a).astype(BF16), w2_ref[...])

    @pl.when(j == pl.num_programs(1) - 1)
    def _():
        o_ref[...] = x_ref[...] + _rms(acc_ref[...], gpost_ref[...])


def mlp(x, w1, w2, gpre, gpost, *, tm, tf):
    M, D = x.shape
    FF = w1.shape[1]
    return pl.pallas_call(
        _mlp_kernel,
        grid=(M // tm, FF // tf),
        in_specs=[pl.BlockSpec((tm, D), lambda i, j: (i, 0)),
                  pl.BlockSpec((D, tf), lambda i, j: (0, j)),
                  pl.BlockSpec((tf, D), lambda i, j: (j, 0)),
                  pl.BlockSpec((1, D), lambda i, j: (0, 0)),
                  pl.BlockSpec((1, D), lambda i, j: (0, 0))],
        out_specs=pl.BlockSpec((tm, D), lambda i, j: (i, 0)),
        out_shape=jax.ShapeDtypeStruct((M, D), F32),
        scratch_shapes=[pltpu.VMEM((tm, D), BF16), pltpu.VMEM((tm, D), F32)],
        compiler_params=_cparams(("parallel", "arbitrary")),
        name="mlp",
    )(x, w1, w2, gpre, gpost)


def _rot_half_cols(w):
    half = w.shape[-1] // 2
    return jnp.concatenate([-w[..., half:], w[..., :half]], axis=-1)


def _pack_w_in(w):
    widths = (MLA_Q_LORA, MLA_KV_LORA, MLA_ROPE, 1024, 1024, 1024, IDX_HEADS * IDX_DIM, IDX_DIM, IDX_HEADS,
              1024, 256, 256, 256, 256, 256, 256, 3 * NSA_HEADS, N_BRANCH * D_MODEL)
    offs = np.concatenate([[0], np.cumsum(widths)])
    seg = [w[:, offs[i]:offs[i + 1]] for i in range(len(widths))]
    (q_lat, kv_lat, k_rope, dsa_q, dsa_k, dsa_v, idx_q, idx_k, idx_w,
     nsa_q, kc, vc, ks, vs, kw, vw, nsa_gate, bgate) = seg
    z = lambda n: jnp.zeros((w.shape[0], n), w.dtype)
    cols = [dsa_q, dsa_k, dsa_v, nsa_q, bgate, idx_q, kc, vc, ks, vs, kw, vw,
            q_lat, kv_lat, k_rope, z(LANE - MLA_ROPE), _rot_half_cols(k_rope), z(LANE - MLA_ROPE), z(LANE),
            idx_k, idx_k, idx_w, z(LANE - IDX_HEADS), nsa_gate, z(LANE - 3 * NSA_HEADS)]
    out = jnp.concatenate(cols, axis=1)
    out = jnp.concatenate([out, z(N_PACK - out.shape[1])], axis=1)
    return out.astype(BF16)


def _pack_mla_weights(w_uq, w_ukv):
    H = MLA_HEADS
    uq = w_uq.reshape(MLA_Q_LORA, H, MLA_NOPE + MLA_ROPE)
    nope = uq[:, :, :MLA_NOPE].reshape(MLA_Q_LORA, H * MLA_NOPE)
    rope = uq[:, :, MLA_NOPE:]
    pad = jnp.zeros((MLA_Q_LORA, H, LANE - MLA_ROPE), w_uq.dtype)
    wqr = jnp.concatenate([rope, pad], axis=-1).reshape(MLA_Q_LORA, H * LANE)
    wqrr = jnp.concatenate([_rot_half_cols(rope), pad], axis=-1).reshape(MLA_Q_LORA, H * LANE)
    ukv = w_ukv.reshape(MLA_KV_LORA, H, MLA_NOPE + MLA_V)
    wkv = jnp.concatenate([ukv[:, :, :MLA_NOPE].reshape(MLA_KV_LORA, H * MLA_NOPE),
                           ukv[:, :, MLA_NOPE:].reshape(MLA_KV_LORA, H * MLA_V)], axis=1)
    return nope.astype(BF16), wqr.astype(BF16), wqrr.astype(BF16), wkv.astype(BF16)


def _constants(S):
    half = MLA_ROPE // 2
    inv_freq = ROPE_BASE ** (-np.arange(0, MLA_ROPE, 2, dtype=np.float32) / MLA_ROPE)
    freq = np.zeros((1, LANE), np.float32)
    freq[0, :half] = inv_freq
    freq[0, half:2 * half] = inv_freq
    ncp = S // CMP_STRIDE
    ns = S // SEL_BLOCK
    c_start = np.arange(ncp) * CMP_STRIDE
    j_start = np.arange(ns) * SEL_BLOCK
    overlap = ((c_start[None, :] < j_start[:, None] + SEL_BLOCK) &
               (c_start[None, :] + CMP_LEN > j_start[:, None])).astype(np.float32)
    overlap[:, ncp - 1] = 0.0
    ovt = np.zeros((LANE, ncp), np.float32)
    ovt[:ns] = overlap
    ind = np.zeros((S, LANE), np.float32)
    ind[np.arange(S), np.arange(S) // SEL_BLOCK] = 1.0
    return jnp.asarray(freq), jnp.asarray(ovt, BF16), jnp.asarray(ind, BF16)


def kernel(x, mem, positions, rel_bias, norm_gains, w_in, mla_q_norm, mla_kv_norm, mla_w_uq, mla_w_ukv,
           nsa_cmp_pos, nsa_cmp_w1, nsa_cmp_w2, w_branch, w_out, cross_wq, cross_wkv, cross_wo, mlp_w1, mlp_w2):
    B, S, D = x.shape
    M = B * S
    depth = w_in.shape[0]
    tq = TQ if S % TQ == 0 else TQ_NSA
    tqn = TQ_NSA
    assert S % tq == 0 and S % tqn == 0 and D == D_MODEL
    topk = min(DSA_TOPK_MAX, S // 4)
    freq, ovt, ind = _constants(S)
    btiles = bias_tiles(rel_bias, tq, 0, DSA_HEADS, nspan=2, masked=False, stack=1)
    btiles_n = bias_tiles(rel_bias, tqn, DSA_HEADS, NSA_HEADS, nspan=WINDOW // tqn + 1, masked=True,
                          stack=NSA_HG // NSA_STREAMS)
    pos = positions.reshape(M, 1)
    xf = x.reshape(M, D)
    memf = mem.reshape(B * mem.shape[1], D)
    tm_big = 1024 if M % 1024 == 0 else tq
    tm = 512 if M % 512 == 0 else tq
    ncp = S // CMP_STRIDE
    row = lambda v: v.reshape(1, -1)

    for l in range(depth):
        g = norm_gains[l]
        proj = norm_matmul(xf, row(g[0]), _pack_w_in(w_in[l]), tm=tm_big, tn=TN_IN)

        wqn, wqr, wqrr, wkv = _pack_mla_weights(mla_w_uq[l], mla_w_ukv[l])
        qn, qr, kn, vv, kr = mla_prep(proj, pos, freq, row(mla_q_norm[l]), row(mla_kv_norm[l]),
                                      wqn, wqr, wqrr, wkv, tm=tm_big)
        o_a = mla_attention(qn, qr, kn, kr, vv, B=B, S=S, tq=tq)

        maskb = dsa_index_mask(proj, B=B, S=S, tqi=TQ_IDX, tk=tq, topk=topk)
        o_b = dsa_attention(proj, maskb, btiles, B=B, S=S, tq=tq)

        kvc_in = jnp.stack([
            proj[:, OFF_NSA_KV + a * 256:OFF_NSA_KV + (a + 1) * 256].reshape(B * ncp, CMP_STRIDE * 256)
            for a in range(2)])
        posflat = jnp.broadcast_to(nsa_cmp_pos[l].reshape(2, 1, CMP_LEN * NSA_HEAD_DIM),
                                   (2, 8, CMP_LEN * NSA_HEAD_DIM)).astype(BF16)
        kvc = nsa_compress(kvc_in, posflat, nsa_cmp_w1[l].astype(BF16), nsa_cmp_w2[l].astype(BF16), B=B, S=S)
        o_cmp, selneg = nsa_cmp_attention(proj, kvc, ovt, B=B, S=S, tq=TQ_CMP if S % TQ_CMP == 0 else tq)
        o_c = nsa_main(proj, selneg, ind, btiles_n, o_cmp, B=B, S=S, tq=tqn)

        xf = merge_branches(xf, o_a, o_b, o_c, proj, w_branch[l].astype(BF16), w_out[l].astype(BF16),
                            row(g[1]), tm=tm)

        mkv = norm_matmul(memf, row(g[3]), cross_wkv[l].astype(BF16), tm=memf.shape[0] // B, tn=1024)
        xf = cross_attention(xf, mkv, cross_wq[l].astype(BF16), cross_wo[l].astype(BF16),
                             row(g[2]), row(g[4]), S=S, tm=tm_big)

        xf = mlp(xf, mlp_w1[l].astype(BF16), mlp_w2[l].astype(BF16), row(g[5]), row(g[6]), tm=tm_big, tf=1024)

    return xf.reshape(B, S, D)
```

```python
import functools
import math

import numpy as np
import jax
import jax.numpy as jnp
from jax import lax
from jax.experimental import pallas as pl
from jax.experimental.pallas import tpu as pltpu

F32 = jnp.float32
BF16 = jnp.bfloat16

D_MODEL = 1024
EPS = 1e-6
MLA_HEADS = 8
MLA_Q_LORA = 384
MLA_KV_LORA = 256
MLA_NOPE = 128
MLA_ROPE = 64
MLA_V = 128
ROPE_BASE = 10000.0
DSA_HEADS = 8
DSA_HEAD_DIM = 128
IDX_HEADS = 8
IDX_DIM = 64
DSA_TOPK_MAX = 256
NSA_HEADS = 8
NSA_GROUPS = 2
NSA_HG = NSA_HEADS // NSA_GROUPS
NSA_HEAD_DIM = 128
CMP_STRIDE = 16
CMP_LEN = 2 * CMP_STRIDE
CMP_HIDDEN = 256
SEL_BLOCK = 64
SEL_COUNT = 16
WINDOW = 512
REL_BUCKETS = 32
REL_MAX_DIST = 128
REL_HEADS = DSA_HEADS + NSA_HEADS
CROSS_HEADS = 4
CROSS_HEAD_DIM = D_MODEL // CROSS_HEADS
D_FF = 4 * D_MODEL
N_BRANCH = 3

LANE = 128
NEG = -1e30
VMEM_LIMIT = 48 * 1024 * 1024

TQ = 512
TQ_NSA = 256
TQ_CMP = 1024
HPS = 2
FAR_GROUP = 4
NEAR = 4
NEAR_NSA = 4
FAR_GROUP_NSA = 8
NSA_STREAMS = 2
BISECT_ITERS = 32
IDX_TIE_CHECK = 16
TQ_IDX = 512
IDX_CHUNK = 256

OFF_DSA_Q = 0
OFF_DSA_K = 1024
OFF_DSA_V = 2048
OFF_NSA_Q = 3072
OFF_BGATE = 4096
OFF_IDX_Q = 7168
OFF_NSA_KV = 7680
OFF_MLA = 9216
OFF_IDX_K = 10240
OFF_IDX_W = 10368
OFF_NSA_GATE = 10496
N_PACK = 10752
TN_IN = 2688


def _cparams(sem):
    return pltpu.CompilerParams(dimension_semantics=sem, vmem_limit_bytes=VMEM_LIMIT)


def _rms(x, g):
    return x * lax.rsqrt(jnp.mean(x * x, axis=-1, keepdims=True) + EPS) * g


def _dot(a, b):
    return jnp.dot(a, b, preferred_element_type=F32)


def _dot_nt(a, b):
    return lax.dot_general(a, b, (((1,), (1,)), ((), ())), preferred_element_type=F32)


def _norm_mm_kernel(x_ref, g_ref, w_ref, o_ref, xn_ref):
    @pl.when(pl.program_id(1) == 0)
    def _():
        xn_ref[...] = _rms(x_ref[...].astype(F32), g_ref[...]).astype(BF16)

    o_ref[...] = _dot(xn_ref[...], w_ref[...]).astype(o_ref.dtype)


def norm_matmul(x, g, w, *, tm, tn):
    M, K = x.shape
    N = w.shape[1]
    return pl.pallas_call(
        _norm_mm_kernel,
        grid=(M // tm, N // tn),
        in_specs=[pl.BlockSpec((tm, K), lambda i, j: (i, 0)),
                  pl.BlockSpec((1, K), lambda i, j: (0, 0)),
                  pl.BlockSpec((K, tn), lambda i, j: (0, j))],
        out_specs=pl.BlockSpec((tm, tn), lambda i, j: (i, j)),
        out_shape=jax.ShapeDtypeStruct((M, N), BF16),
        scratch_shapes=[pltpu.VMEM((tm, K), BF16)],
        compiler_params=_cparams(("parallel", "arbitrary")),
        name="norm_matmul",
    )(x, g, w)


def _bias_tiles_kernel(tab_ref, bkt_ref, o_ref, *, inv_scale, h0, tq, nspan, masked):
    h = pl.program_id(0) + h0
    bkt = bkt_ref[...]
    far = tab_ref[REL_BUCKETS - 1, h]
    out = jnp.zeros(bkt.shape, F32)
    for b in range(REL_BUCKETS - 1):
        out = jnp.where(bkt == b, (tab_ref[b, h] - far) * inv_scale, out)
    diag, prev = out[0], out[1]
    r, c = _tri(tq, tq)
    if masked:
        diag = jnp.where(r >= c, diag, NEG)
    o_ref[:, (nspan - 1) * tq:nspan * tq] = diag
    o_ref[:, (nspan - 2) * tq:(nspan - 1) * tq] = prev
    for k in range(nspan - 2):
        edge = jnp.where(r < c, 0.0, NEG) if (masked and k == 0) else jnp.zeros((tq, tq), F32)
        o_ref[:, k * tq:(k + 1) * tq] = edge


def _bucket_tiles(tq):
    i = np.arange(tq)[:, None]
    j = np.arange(tq)[None, :]
    d = np.stack([i - j, tq + i - j]).astype(np.int32)
    n = jnp.maximum(jnp.asarray(d), 0)
    exact = REL_BUCKETS // 2
    nf = jnp.maximum(n, 1).astype(F32)
    log_b = exact + (jnp.log(nf / exact) / math.log(REL_MAX_DIST / exact) * (REL_BUCKETS - exact)).astype(jnp.int32)
    return jnp.where(n < exact, n, jnp.minimum(log_b, REL_BUCKETS - 1)).astype(jnp.int32)


def bias_tiles(rel_bias, tq, h0, nh, *, nspan, masked, stack):
    assert tq >= REL_MAX_DIST
    bkt = _bucket_tiles(tq)
    assert DSA_HEAD_DIM == NSA_HEAD_DIM
    return pl.pallas_call(
        functools.partial(_bias_tiles_kernel, inv_scale=DSA_HEAD_DIM ** 0.5, h0=h0, tq=tq, nspan=nspan,
                          masked=masked),
        grid=(nh,),
        in_specs=[pl.BlockSpec(memory_space=pltpu.SMEM),
                  pl.BlockSpec((2, tq, tq), lambda h: (0, 0, 0))],
        out_specs=pl.BlockSpec((None, tq, nspan * tq), lambda h: (h // stack, h % stack, 0)),
        out_shape=jax.ShapeDtypeStruct((nh // stack, stack * tq, nspan * tq), F32),
        compiler_params=_cparams(("arbitrary",)),
        name="bias_tiles",
    )(rel_bias, bkt)


def _mla_prep_kernel(lat_ref, pos_ref, freq_ref, qg_ref, kvg_ref, wqn_ref, wqr_ref, wqrr_ref, wkv_ref,
                     qn_ref, qr_ref, kn_ref, v_ref, kr_ref):
    lat = lat_ref[...].astype(F32)
    qlat = _rms(lat[:, :MLA_Q_LORA], qg_ref[...]).astype(BF16)
    kvlat = _rms(lat[:, MLA_Q_LORA:MLA_Q_LORA + MLA_KV_LORA], kvg_ref[...]).astype(BF16)
    o = MLA_Q_LORA + MLA_KV_LORA
    kr = lat[:, o:o + LANE]
    krr = lat[:, o + LANE:o + 2 * LANE]
    ang = pos_ref[...].astype(F32) * freq_ref[...]
    cos, sin = jnp.cos(ang), jnp.sin(ang)
    cos8 = jnp.concatenate([cos] * MLA_HEADS, axis=1)
    sin8 = jnp.concatenate([sin] * MLA_HEADS, axis=1)
    qn_ref[...] = _dot(qlat, wqn_ref[...]).astype(BF16)
    qr_ref[...] = (_dot(qlat, wqr_ref[...]) * cos8 + _dot(qlat, wqrr_ref[...]) * sin8).astype(BF16)
    kv = _dot(kvlat, wkv_ref[...])
    nk = MLA_HEADS * MLA_NOPE
    kn_ref[...] = kv[:, :nk].astype(BF16)
    v_ref[...] = kv[:, nk:].astype(BF16)
    kr_ref[...] = (kr * cos + krr * sin).astype(BF16)


def mla_prep(proj, pos, freq, qg, kvg, wqn, wqr, wqrr, wkv, *, tm):
    M = proj.shape[0]
    HD = MLA_HEADS * LANE
    full = lambda a: pl.BlockSpec(a.shape, lambda i: (0, 0))
    outs = [jax.ShapeDtypeStruct((M, HD), BF16)] * 4 + [jax.ShapeDtypeStruct((M, LANE), BF16)]
    return pl.pallas_call(
        _mla_prep_kernel,
        grid=(M // tm,),
        in_specs=[pl.BlockSpec((tm, 1024), lambda i: (i, OFF_MLA // 1024)),
                  pl.BlockSpec((tm, 1), lambda i: (i, 0)),
                  full(freq), full(qg), full(kvg), full(wqn), full(wqr), full(wqrr), full(wkv)],
        out_specs=[pl.BlockSpec((tm, HD), lambda i: (i, 0))] * 4 + [pl.BlockSpec((tm, LANE), lambda i: (i, 0))],
        out_shape=outs,
        compiler_params=_cparams(("parallel",)),
        name="mla_prep",
    )(proj, pos, freq, qg, kvg, wqn, wqr, wqrr, wkv)


def _softmax_init(m_ref, acc_ref):
    m_ref[...] = jnp.full(m_ref.shape, NEG, F32)
    acc_ref[...] = jnp.zeros(acc_ref.shape, F32)


def _with_ones(v):
    return jnp.concatenate([v, jnp.ones(v.shape, v.dtype)], axis=1)


def _softmax_step(s, v, m_ref, acc_ref, scale):
    c = scale * math.log2(math.e)
    m_prev = m_ref[...]
    m_new = jnp.maximum(m_prev, jnp.max(s, axis=-1, keepdims=True))
    alpha = jnp.exp2((m_prev - m_new) * c)
    p = jnp.exp2((s - jnp.tile(m_new, (1, s.shape[1] // LANE))) * c)
    acc_ref[...] = jnp.tile(alpha, (1, 2)) * acc_ref[...] + _dot(p.astype(BF16), _with_ones(v))
    m_ref[...] = m_new


def _softmax_finish(acc_ref):
    acc = acc_ref[...]
    return acc[:, :LANE] / acc[:, LANE:]


def _far_chunks(n, step, largest):
    def body(p, c):
        step(largest * p, largest)
        return c

    lax.fori_loop(0, n // largest, body, 0)
    w = largest // 2
    while w >= 1:
        @pl.when(n % (2 * w) >= w)
        def _(w=w):
            step(n // (2 * w) * (2 * w), w)
        w //= 2


def _tri(tq, tk):
    row = lax.broadcasted_iota(jnp.int32, (tq, tk), 0)
    col = lax.broadcasted_iota(jnp.int32, (tq, tk), 1)
    return row, col


def _mla_attn_kernel(qn_ref, qr_ref, kn_ref, kr_ref, v_ref, o_ref, m_ref, acc_ref, *, tq, scale):
    qi = pl.program_id(2)
    hs = range(HPS)
    q = [jnp.concatenate([qn_ref[:, h * LANE:(h + 1) * LANE], qr_ref[:, h * LANE:(h + 1) * LANE]], axis=-1)
         for h in hs]
    for h in hs:
        _softmax_init(m_ref.at[h], acc_ref.at[h])

    def step(j, nc, causal_last):
        sl = pl.ds(j * tq if isinstance(j, int) else pl.multiple_of(j * tq, tq), nc * tq)
        kr = kr_ref[sl, :]
        if causal_last:
            row, col = _tri(tq, nc * tq)
            ok = col <= row + (nc - 1) * tq
        s = [_dot_nt(q[h], jnp.concatenate([kn_ref[sl, h * LANE:(h + 1) * LANE], kr], axis=-1)) for h in hs]
        for h in hs:
            sh = jnp.where(ok, s[h], NEG) if causal_last else s[h]
            _softmax_step(sh, v_ref[sl, h * LANE:(h + 1) * LANE], m_ref.at[h], acc_ref.at[h], scale)

    _far_chunks(jnp.maximum(qi - (NEAR - 1), 0), lambda j, nc: step(j, nc, False), FAR_GROUP)
    for k in range(NEAR - 1):
        @pl.when(qi == k)
        def _(k=k):
            step(0, k + 1, True)

    @pl.when(qi >= NEAR - 1)
    def _():
        step(qi - (NEAR - 1), NEAR, True)

    for h in hs:
        o_ref[:, h * LANE:(h + 1) * LANE] = _softmax_finish(acc_ref.at[h]).astype(o_ref.dtype)


def mla_attention(qn, qr, kn, kr, v, *, B, S, tq):
    M = B * S
    nq = S // tq
    H = MLA_HEADS
    scale = (MLA_NOPE + MLA_ROPE) ** -0.5
    w = HPS * LANE
    qspec = pl.BlockSpec((tq, w), lambda b, h, i: (b * nq + i, h))
    kspec = pl.BlockSpec((S, w), lambda b, h, i: (b, h))
    return pl.pallas_call(
        functools.partial(_mla_attn_kernel, tq=tq, scale=scale),
        grid=(B, H // HPS, nq),
        in_specs=[qspec, qspec, kspec, pl.BlockSpec((S, LANE), lambda b, h, i: (b, 0)), kspec],
        out_specs=qspec,
        out_shape=jax.ShapeDtypeStruct((M, H * MLA_V), BF16),
        scratch_shapes=[pltpu.VMEM((HPS, tq, LANE), F32), pltpu.VMEM((HPS, tq, 2 * LANE), F32)],
        compiler_params=_cparams(("parallel", "parallel", "arbitrary")),
        name="mla_attention",
    )(qn, qr, kn, kr, v)


def _dsa_index_kernel(iq_ref, ik_ref, iw_ref, tri_ref, mb_ref, sc_ref, *, tq, tk, S, topk, iters):
    t0 = pl.program_id(1) * tq
    nk = S // tk
    nch = (t0 + tq - 1) // tk + 1
    hw = IDX_CHUNK
    iq = iq_ref[...]
    lane = lax.broadcasted_iota(jnp.int32, (tq, LANE), 1)
    qs = []
    for h in range(IDX_HEADS):
        blk = iq[:, (h // 2) * LANE:(h // 2 + 1) * LANE]
        keep = (lane >= IDX_DIM) if h % 2 else (lane < IDX_DIM)
        qs.append(jnp.where(keep, blk, jnp.zeros_like(blk)))
    q8 = jnp.concatenate(qs, axis=0)
    iwt = iw_ref[...].astype(F32).T
    wrow = [iwt[h:h + 1, :] for h in range(IDX_HEADS)]
    qpos = lax.broadcasted_iota(jnp.int32, (hw, tq), 1) + t0
    krow = lax.broadcasted_iota(jnp.int32, (hw, tq), 0)

    def fold(x, op):
        out = x[0:8]
        for r in range(1, x.shape[0] // 8):
            out = op(out, x[r * 8:(r + 1) * 8])
        return out

    def score_chunk(c, carry):
        mn, mx = carry
        for half in range(tk // hw):
            k0 = pl.multiple_of(c * tk + half * hw, hw)
            res = _dot_nt(ik_ref[pl.ds(k0, hw), :], q8)
            sc = wrow[0] * jnp.maximum(res[:, 0:tq], 0.0)
            for h in range(1, IDX_HEADS):
                sc = sc + wrow[h] * jnp.maximum(res[:, h * tq:(h + 1) * tq], 0.0)
            causal = (krow + k0) <= qpos
            mn = jnp.minimum(mn, fold(jnp.where(causal, sc, -NEG), jnp.minimum))
            sc = jnp.where(causal, sc, NEG)
            mx = jnp.maximum(mx, fold(sc, jnp.maximum))
            sc_ref[c, half * hw:(half + 1) * hw, :] = sc
        return mn, mx

    mn, mx = lax.fori_loop(0, nch, score_chunk, (jnp.full((8, tq), -NEG, F32), jnp.full((8, tq), NEG, F32)))
    n_causal = (lax.broadcasted_iota(jnp.int32, (1, tq), 1) + (t0 + 1)).astype(F32)
    kf = jnp.minimum(float(topk), n_causal)

    def count(pred):
        def body(c, acc):
            return acc + fold(jnp.where(pred(sc_ref[c]), 1.0, 0.0), jnp.add)

        return jnp.sum(lax.fori_loop(0, nch, body, jnp.zeros((8, tq), F32)), axis=0, keepdims=True)

    lo0 = jnp.min(mn, axis=0, keepdims=True)
    mx = jnp.max(mx, axis=0, keepdims=True)
    hi0 = mx + (jnp.abs(mx) * 1e-6 + 1.0)

    def bisect(c):
        lo, hi, n_lo, n_hi = c
        mid = 0.5 * (lo + hi)
        n_mid = count(lambda blk: blk >= mid)
        ge = n_mid >= kf
        return (jnp.where(ge, mid, lo), jnp.where(ge, hi, mid),
                jnp.where(ge, n_mid, n_lo), jnp.where(ge, n_hi, n_mid))

    def open_rows(n_lo, n_hi, settled):
        return (n_lo - kf) * jnp.where(n_lo - n_hi > 2.0, 1.0, 0.0) * (1.0 - settled)

    def search(state, limit, settled):
        def cond(c):
            return jnp.logical_and(c[0] < limit, jnp.max(open_rows(c[3], c[4], settled)) > 0.0)

        def body(c):
            return (c[0] + 2,) + bisect(bisect(c[1:]))

        return lax.while_loop(cond, body, state)

    def band_min_max(lo, hi):
        def body(c, carry):
            blk = sc_ref[c]
            inside = (blk >= lo) & (blk < hi)
            return (jnp.minimum(carry[0], fold(jnp.where(inside, blk, -NEG), jnp.minimum)),
                    jnp.maximum(carry[1], fold(jnp.where(inside, blk, NEG), jnp.maximum)))

        bmin, bmax = lax.fori_loop(0, nch, body, (jnp.full((8, tq), -NEG, F32), jnp.full((8, tq), NEG, F32)))
        return jnp.min(bmin, axis=0, keepdims=True), jnp.max(bmax, axis=0, keepdims=True)

    def finish(state):
        _, lo, hi, n_lo, n_hi = state
        over = n_lo > kf
        bmin, bmax = lax.cond(jnp.max(n_lo - kf) > 0.0, lambda: band_min_max(lo, hi), lambda: (lo, lo))
        single = bmin == bmax
        split = over & (n_lo - n_hi == 2.0) & jnp.logical_not(single)
        unsplit = over & jnp.logical_not(split)
        return jnp.where(split, bmax, lo), jnp.where(over & single, 1.0, 0.0), jnp.where(unsplit, 1.0, 0.0)

    zero = jnp.zeros((1, tq), F32)
    state1 = search((jnp.int32(0), lo0, hi0, n_causal, zero), IDX_TIE_CHECK, zero)
    first = finish(state1)
    state = search(state1, iters, first[1])
    lo, _, unsplit = lax.cond(state[0] > state1[0], lambda: finish(state), lambda: first)
    hi, n_hi = state[2], state[4]
    tied = jnp.max(unsplit) > 0.0

    def fill(j, c):
        mb_ref[j] = jnp.full((tq, tk), NEG, mb_ref.dtype)
        return c

    lax.fori_loop(nch, nk, fill, 0)

    def emit(j, keep):
        mb_ref[j] = jnp.where(keep, 0.0, NEG).T.astype(mb_ref.dtype)

    @pl.when(jnp.logical_not(tied))
    def _():
        def body(j, c):
            emit(j, sc_ref[j] >= lo)
            return c

        lax.fori_loop(0, nch, body, 0)

    @pl.when(tied)
    def _():
        need = kf - n_hi
        tri = tri_ref[...]

        def body(j, base):
            sc = sc_ref[j]
            above = sc >= hi
            band = (sc >= lo) & jnp.logical_not(above)
            prefix = _dot(tri, jnp.where(band, 1.0, 0.0).astype(BF16)) + base
            emit(j, above | (band & (prefix <= need)))
            return prefix[tk - 1:tk, :]

        lax.fori_loop(0, nch, body, zero)


def dsa_index_mask(proj, *, B, S, tqi, tk, topk):
    nq = S // tqi
    nk = S // tk
    assert tk % tqi == 0 and tk % IDX_CHUNK == 0
    tri = jnp.asarray(np.tril(np.ones((tk, tk), np.float32)), BF16)
    return pl.pallas_call(
        functools.partial(_dsa_index_kernel, tq=tqi, tk=tk, S=S, topk=topk, iters=BISECT_ITERS),
        grid=(B, nq),
        in_specs=[pl.BlockSpec((tqi, 512), lambda b, i: (b * nq + i, OFF_IDX_Q // 512)),
                  pl.BlockSpec((S, LANE), lambda b, i: (b, OFF_IDX_K // LANE)),
                  pl.BlockSpec((tqi, LANE), lambda b, i: (b * nq + i, OFF_IDX_W // LANE)),
                  pl.BlockSpec((tk, tk), lambda b, i: (0, 0))],
        out_specs=pl.BlockSpec((None, nk, tqi, tk), lambda b, i: (b, 0, i, 0)),
        out_shape=jax.ShapeDtypeStruct((B, nk, S, tk), BF16),
        scratch_shapes=[pltpu.VMEM((nk, tk, tqi), F32)],
        compiler_params=_cparams(("parallel", "parallel")),
        name="dsa_index_mask",
    )(proj, proj, proj, tri)


def _dsa_attn_kernel(q_ref, k_ref, v_ref, mb_ref, bias_ref, o_ref, m_ref, acc_ref, *, tq, scale):
    qi = pl.program_id(2)
    hs = range(HPS)
    cols = lambda h: slice(h * LANE, (h + 1) * LANE)
    q = [q_ref[:, cols(h)] for h in hs]
    for h in hs:
        _softmax_init(m_ref.at[h], acc_ref.at[h])

    def step(j, nc, bias):
        sl = pl.ds(j * tq if isinstance(j, int) else pl.multiple_of(j * tq, tq), nc * tq)
        mb = mb_ref[j] if nc == 1 else jnp.concatenate([mb_ref[j + i] for i in range(nc)], axis=1)
        mb = mb.astype(F32)
        s = [_dot_nt(q[h], k_ref[sl, cols(h)]) + mb for h in hs]
        for h in hs:
            sh = s[h]
            if bias and nc == 1:
                sh = sh + bias_ref[h, :, tq:2 * tq]
            elif bias and nc == 2:
                sh = sh + bias_ref[h]
            elif bias:
                sh = jnp.concatenate([sh[:, :(nc - 2) * tq], sh[:, (nc - 2) * tq:] + bias_ref[h]], axis=1)
            _softmax_step(sh, v_ref[sl, cols(h)], m_ref.at[h], acc_ref.at[h], scale)

    _far_chunks(jnp.maximum(qi - (NEAR - 1), 0), lambda j, nc: step(j, nc, False), FAR_GROUP)
    for k in range(NEAR - 1):
        @pl.when(qi == k)
        def _(k=k):
            step(0, k + 1, True)

    @pl.when(qi >= NEAR - 1)
    def _():
        step(qi - (NEAR - 1), NEAR, True)

    for h in hs:
        o_ref[:, cols(h)] = _softmax_finish(acc_ref.at[h]).astype(o_ref.dtype)


def dsa_attention(proj, maskb, btiles, *, B, S, tq):
    M = B * S
    nq = S // tq
    H = DSA_HEADS
    scale = DSA_HEAD_DIM ** -0.5
    w = HPS * LANE
    return pl.pallas_call(
        functools.partial(_dsa_attn_kernel, tq=tq, scale=scale),
        grid=(B, H // HPS, nq),
        in_specs=[pl.BlockSpec((tq, w), lambda b, h, i: (b * nq + i, OFF_DSA_Q // w + h)),
                  pl.BlockSpec((S, w), lambda b, h, i: (b, OFF_DSA_K // w + h)),
                  pl.BlockSpec((S, w), lambda b, h, i: (b, OFF_DSA_V // w + h)),
                  pl.BlockSpec((None, nq, tq, tq), lambda b, h, i: (b, 0, i, 0)),
                  pl.BlockSpec((HPS, tq, 2 * tq), lambda b, h, i: (h, 0, 0))],
        out_specs=pl.BlockSpec((tq, w), lambda b, h, i: (b * nq + i, h)),
        out_shape=jax.ShapeDtypeStruct((M, H * DSA_HEAD_DIM), BF16),
        scratch_shapes=[pltpu.VMEM((HPS, tq, LANE), F32), pltpu.VMEM((HPS, tq, 2 * LANE), F32)],
        compiler_params=_cparams(("parallel", "parallel", "arbitrary")),
        name="dsa_attention",
    )(proj, proj, proj, maskb, btiles)


def _nsa_compress_kernel(x_ref, pos_ref, w1_ref, w2_ref, o_ref, *, ncp):
    dk = NSA_HEAD_DIM
    posw = _dot(pos_ref[...], w1_ref[...])[0:1, :]
    for g in range(NSA_GROUPS):
        lo = jnp.zeros((ncp, CMP_HIDDEN), F32)
        hi = jnp.zeros((ncp, CMP_HIDDEN), F32)
        for l in range(CMP_STRIDE):
            xs = x_ref[:, l, g * dk:(g + 1) * dk]
            lo = lo + _dot(xs, w1_ref[l * dk:(l + 1) * dk, :])
            hi = hi + _dot(xs, w1_ref[(CMP_STRIDE + l) * dk:(CMP_STRIDE + l + 1) * dk, :])
        hid = lo + pltpu.roll(hi, ncp - 1, 0) + posw
        o_ref[g] = _dot(jax.nn.gelu(hid).astype(BF16), w2_ref[...]).astype(o_ref.dtype)


def nsa_compress(proj, posflat, w1, w2, *, B, S):
    ncp = S // CMP_STRIDE
    G, dk = NSA_GROUPS, NSA_HEAD_DIM
    xkv = proj.reshape(B * ncp, CMP_STRIDE, N_PACK)
    return pl.pallas_call(
        functools.partial(_nsa_compress_kernel, ncp=ncp),
        grid=(2, B),
        in_specs=[pl.BlockSpec((ncp, CMP_STRIDE, G * dk), lambda a, b: (b, 0, OFF_NSA_KV // (G * dk) + a)),
                  pl.BlockSpec((None, 8, CMP_LEN * dk), lambda a, b: (a, 0, 0)),
                  pl.BlockSpec((None, CMP_LEN * dk, CMP_HIDDEN), lambda a, b: (a, 0, 0)),
                  pl.BlockSpec((None, CMP_HIDDEN, dk), lambda a, b: (a, 0, 0))],
        out_specs=pl.BlockSpec((None, None, G, ncp, dk), lambda a, b: (a, b, 0, 0, 0)),
        out_shape=jax.ShapeDtypeStruct((2, B, G, ncp, dk), BF16),
        compiler_params=_cparams(("parallel", "parallel")),
        name="nsa_compress",
    )(xkv, posflat, w1, w2)


def _nsa_cmp_kernel(q_ref, kc_ref, vc_ref, ov_ref, oc_ref, sel_ref, *, tq, ncp, ns, n_sel, scale):
    t0 = pl.program_id(2) * tq
    dk = NSA_HEAD_DIM
    kc = kc_ref[...]
    vc = vc_ref[...]
    trow = lax.broadcasted_iota(jnp.int32, (tq, ncp), 0) + t0
    ccol = lax.broadcasted_iota(jnp.int32, (tq, ncp), 1)
    vis = (ccol * CMP_STRIDE + (CMP_LEN - 1)) <= trow
    psum = jnp.zeros((tq, ncp), F32)
    for h in range(NSA_HG):
        s = jnp.where(vis, _dot_nt(q_ref[:, h * dk:(h + 1) * dk], kc) * scale, NEG)
        m = jnp.max(s, axis=-1, keepdims=True)
        p = jnp.where(vis, jnp.exp(s - m), 0.0)
        d = jnp.sum(p, axis=-1, keepdims=True)
        p = p / jnp.where(d > 0, d, 1.0)
        oc_ref[:, h * dk:(h + 1) * dk] = _dot(p.astype(BF16), vc).astype(oc_ref.dtype)
        psum = psum + p
    p_hi = psum.astype(BF16)
    p_lo = (psum - p_hi.astype(F32)).astype(BF16)
    ov = ov_ref[...]
    imp = _dot_nt(ov, p_hi) + _dot_nt(ov, p_lo)
    imp = imp[:ns]
    jrow = lax.broadcasted_iota(jnp.int32, (ns, tq), 0)
    tcol = lax.broadcasted_iota(jnp.int32, (ns, tq), 1) + t0
    blk_t = tcol // SEL_BLOCK
    forced = (jrow == 0) | (jrow == blk_t) | (jrow == blk_t - 1)
    val = jnp.where(forced, -NEG, imp)
    val = jnp.where(jrow * SEL_BLOCK > tcol, NEG, val)
    rank = jnp.zeros((ns, tq), F32)
    for j in range(ns):
        other = val[j:j + 1, :]
        ahead = (other > val) | ((other == val) & (jrow > j))
        rank = rank + jnp.where(ahead, 1.0, 0.0)
    selneg = jnp.where(rank < float(n_sel), 0.0, NEG)
    if ns < LANE:
        selneg = jnp.concatenate([selneg, jnp.zeros((LANE - ns, tq), F32)], axis=0)
    sel_ref[...] = selneg.T.astype(sel_ref.dtype)


def nsa_cmp_attention(proj, kvc, ovt, *, B, S, tq):
    M = B * S
    nq = S // tq
    G, dk = NSA_GROUPS, NSA_HEAD_DIM
    ncp = S // CMP_STRIDE
    ns = S // SEL_BLOCK
    assert ns <= LANE
    n_sel = min(SEL_COUNT, ns)
    gw = NSA_HG * dk
    return pl.pallas_call(
        functools.partial(_nsa_cmp_kernel, tq=tq, ncp=ncp, ns=ns, n_sel=n_sel, scale=dk ** -0.5),
        grid=(B, G, nq),
        in_specs=[pl.BlockSpec((tq, gw), lambda b, g, i: (b * nq + i, OFF_NSA_Q // gw + g)),
                  pl.BlockSpec((None, None, None, ncp, dk), lambda b, g, i: (0, b, g, 0, 0)),
                  pl.BlockSpec((None, None, None, ncp, dk), lambda b, g, i: (1, b, g, 0, 0)),
                  pl.BlockSpec((LANE, ncp), lambda b, g, i: (0, 0))],
        out_specs=[pl.BlockSpec((tq, gw), lambda b, g, i: (b * nq + i, g)),
                   pl.BlockSpec((None, None, tq, LANE), lambda b, g, i: (b, g, i, 0))],
        out_shape=[jax.ShapeDtypeStruct((M, G * gw), BF16),
                   jax.ShapeDtypeStruct((B, G, S, LANE), BF16)],
        compiler_params=_cparams(("parallel", "parallel", "parallel")),
        name="nsa_cmp_attention",
    )(proj, kvc, kvc, ovt)


def _nsa_main_kernel(q_ref, sel_ref, ks_ref, vs_ref, kw_ref, vw_ref, ind_ref, bias_ref, gate_ref, oc_ref,
                     o_ref, m_ref, acc_ref, *, tq, nback, scale):
    g = pl.program_id(1)
    qi = pl.program_id(2)
    dk = NSA_HEAD_DIM
    HG = NSA_HG
    ns = NSA_STREAMS
    hp = HG // ns
    st = range(ns)
    stack = lambda f: [jnp.concatenate([f(a * hp + i) for i in range(hp)], axis=0) for a in st]
    q = stack(lambda h: q_ref[:, h * dk:(h + 1) * dk])
    sel = jnp.concatenate([sel_ref[...]] * hp, axis=0)
    qa = [jnp.concatenate([q[a], sel], axis=1) for a in st]
    gate = jax.nn.sigmoid(gate_ref[...].astype(F32))

    for a in range(2 * ns):
        _softmax_init(m_ref.at[a], acc_ref.at[a])

    def steps(s, v, fix, base=0):
        for a in st:
            _softmax_step(fix(a, s[a]), v, m_ref.at[base + a], acc_ref.at[base + a], scale)

    wspan = nback + 1

    def rows_from(j, width):
        return pl.ds(j * tq if isinstance(j, int) else pl.multiple_of(j * tq, tq), width)

    def sel_scores(j, width=tq):
        sl = rows_from(j, width)
        ka = jnp.concatenate([ks_ref[sl, :], ind_ref[sl, :]], axis=1)
        return [_dot_nt(qa[a], ka) for a in st], vs_ref[sl, :]

    def sel_far(j, nc):
        s, v = sel_scores(j, nc * tq)
        steps(s, v, lambda a, x: x)

    nsel = NEAR_NSA
    _far_chunks(jnp.maximum(qi - (nsel - 1), 0), sel_far, FAR_GROUP_NSA)

    def win_scores(j, width):
        sl = rows_from(j, width)
        kw = kw_ref[sl, :]
        return [_dot_nt(q[a], kw) for a in st], vw_ref[sl, :]

    def near(first_s, nc_s, first_w, nc_w):
        tile = lambda nc: (lambda a, x: x + bias_ref[a, :, (wspan - nc) * tq:wspan * tq])

        def sel_tile(a, x):
            if nc_s <= 2:
                return tile(nc_s)(a, x)
            return jnp.concatenate([x[:, :(nc_s - 2) * tq], tile(2)(a, x[:, (nc_s - 2) * tq:])], axis=1)

        s_s, v_s = sel_scores(first_s, nc_s * tq)
        s_w, v_w = win_scores(first_w, nc_w * tq)
        steps(s_s, v_s, sel_tile)
        steps(s_w, v_w, tile(nc_w), base=ns)

    nmax = max(nsel, wspan)
    for k in range(nmax - 1):
        @pl.when(qi == k)
        def _(k=k):
            nc_s, nc_w = min(k + 1, nsel), min(k + 1, wspan)
            near(k + 1 - nc_s, nc_s, k + 1 - nc_w, nc_w)

    @pl.when(qi >= nmax - 1)
    def _():
        near(qi - (nsel - 1), nsel, qi - (wspan - 1), wspan)

    o_s = [_softmax_finish(acc_ref.at[a]) for a in st]
    o_w = [_softmax_finish(acc_ref.at[ns + a]) for a in st]

    lane = lax.broadcasted_iota(jnp.int32, gate.shape, 1)
    for h in range(HG):
        c = g * HG + h
        a, sl = h // hp, slice((h % hp) * tq, (h % hp + 1) * tq)
        gc = [jnp.sum(jnp.where(lane == br * NSA_HEADS + c, gate, 0.0), axis=-1, keepdims=True)
              for br in range(3)]
        o = gc[0] * oc_ref[:, h * dk:(h + 1) * dk].astype(F32) + gc[1] * o_s[a][sl] + gc[2] * o_w[a][sl]
        o_ref[:, h * dk:(h + 1) * dk] = o.astype(o_ref.dtype)


def nsa_main(proj, selneg, ind, btiles, o_c, *, B, S, tq):
    M = B * S
    nq = S // tq
    G, dk, HG = NSA_GROUPS, NSA_HEAD_DIM, NSA_HG
    gw = HG * dk
    assert WINDOW % tq == 0 and WINDOW // tq >= 2
    nback = WINDOW // tq
    ns, hp = NSA_STREAMS, HG // NSA_STREAMS
    kv = lambda n: pl.BlockSpec((S, dk), lambda b, g, i, n=n: (b, (OFF_NSA_KV + n * G * dk) // dk + g))
    return pl.pallas_call(
        functools.partial(_nsa_main_kernel, tq=tq, nback=nback, scale=dk ** -0.5),
        grid=(B, G, nq),
        in_specs=[pl.BlockSpec((tq, gw), lambda b, g, i: (b * nq + i, OFF_NSA_Q // gw + g)),
                  pl.BlockSpec((None, None, tq, LANE), lambda b, g, i: (b, g, i, 0)),
                  kv(2), kv(3), kv(4), kv(5),
                  pl.BlockSpec((S, LANE), lambda b, g, i: (0, 0)),
                  pl.BlockSpec((ns, hp * tq, (nback + 1) * tq), lambda b, g, i: (g, 0, 0)),
                  pl.BlockSpec((tq, LANE), lambda b, g, i: (b * nq + i, OFF_NSA_GATE // LANE)),
                  pl.BlockSpec((tq, gw), lambda b, g, i: (b * nq + i, g))],
        out_specs=pl.BlockSpec((tq, gw), lambda b, g, i: (b * nq + i, g)),
        out_shape=jax.ShapeDtypeStruct((M, G * gw), BF16),
        scratch_shapes=[pltpu.VMEM((2 * ns, hp * tq, LANE), F32),
                        pltpu.VMEM((2 * ns, hp * tq, 2 * LANE), F32)],
        compiler_params=_cparams(("parallel", "parallel", "arbitrary")),
        name="nsa_main",
    )(proj, selneg, proj, proj, proj, proj, ind, btiles, proj, o_c)


def _merge_kernel(x_ref, oa_ref, ob_ref, oc_ref, g0_ref, g1_ref, g2_ref, wb_ref, wo_ref, gain_ref, o_ref):
    merged = None
    for o_r, g_r, br in ((oa_ref, g0_ref, 0), (ob_ref, g1_ref, 1), (oc_ref, g2_ref, 2)):
        t = jax.nn.sigmoid(g_r[...].astype(F32)) * _dot(o_r[...], wb_ref[br])
        merged = t if merged is None else merged + t
    y = _dot(merged.astype(BF16), wo_ref[...])
    o_ref[...] = x_ref[...] + _rms(y, gain_ref[...])


def merge_branches(x, o_a, o_b, o_c, proj, wb, wo, gain, *, tm):
    M, D = x.shape
    row = pl.BlockSpec((tm, D), lambda i: (i, 0))
    gate = lambda br: pl.BlockSpec((tm, D), lambda i, br=br: (i, OFF_BGATE // D + br))
    return pl.pallas_call(
        _merge_kernel,
        grid=(M // tm,),
        in_specs=[row, row, row, row, gate(0), gate(1), gate(2),
                  pl.BlockSpec(wb.shape, lambda i: (0, 0, 0)),
                  pl.BlockSpec(wo.shape, lambda i: (0, 0)),
                  pl.BlockSpec((1, D), lambda i: (0, 0))],
        out_specs=row,
        out_shape=jax.ShapeDtypeStruct((M, D), F32),
        compiler_params=_cparams(("parallel",)),
        name="merge_branches",
    )(x, o_a, o_b, o_c, proj, proj, proj, wb, wo, gain)


def _cross_kernel(x_ref, kv_ref, wq_ref, wo_ref, gpre_ref, gpost_ref, o_ref, *, scale):
    x = x_ref[...]
    h = _rms(x, gpre_ref[...]).astype(BF16)
    q = _dot(h, wq_ref[...]).astype(BF16)
    dh = CROSS_HEAD_DIM
    outs = []
    for hd in range(CROSS_HEADS):
        k = kv_ref[:, hd * dh:(hd + 1) * dh]
        v = kv_ref[:, D_MODEL + hd * dh:D_MODEL + (hd + 1) * dh]
        s = _dot_nt(q[:, hd * dh:(hd + 1) * dh], k) * scale
        p = jnp.exp(s - jnp.max(s, axis=-1, keepdims=True))
        p = p / jnp.sum(p, axis=-1, keepdims=True)
        outs.append(_dot(p.astype(BF16), v).astype(BF16))
    y = _dot(jnp.concatenate(outs, axis=1), wo_ref[...])
    o_ref[...] = x + _rms(y, gpost_ref[...])


def cross_attention(x, kv, wq, wo, gpre, gpost, *, S, tm):
    M, D = x.shape
    mlen = kv.shape[0] // (M // S)
    per_b = S // tm
    return pl.pallas_call(
        functools.partial(_cross_kernel, scale=CROSS_HEAD_DIM ** -0.5),
        grid=(M // tm,),
        in_specs=[pl.BlockSpec((tm, D), lambda i: (i, 0)),
                  pl.BlockSpec((mlen, 2 * D), lambda i: (i // per_b, 0)),
                  pl.BlockSpec(wq.shape, lambda i: (0, 0)),
                  pl.BlockSpec(wo.shape, lambda i: (0, 0)),
                  pl.BlockSpec((1, D), lambda i: (0, 0)),
                  pl.BlockSpec((1, D), lambda i: (0, 0))],
        out_specs=pl.BlockSpec((tm, D), lambda i: (i, 0)),
        out_shape=jax.ShapeDtypeStruct((M, D), F32),
        compiler_params=_cparams(("parallel",)),
        name="cross_attention",
    )(x, kv, wq, wo, gpre, gpost)


def _mlp_kernel(x_ref, w1_ref, w2_ref, gpre_ref, gpost_ref, o_ref, h_ref, acc_ref):
    j = pl.program_id(1)

    @pl.when(j == 0)
    def _():
        h_ref[...] = _rms(x_ref[...], gpre_ref[...]).astype(BF16)
        acc_ref[...] = jnp.zeros(acc_ref.shape, F32)

    a = jnp.maximum(_dot(h_ref[...], w1_ref[...]), 0.0)
    acc_ref[...] += _dot((a * a).astype(BF16), w2_ref[...])

    @pl.when(j == pl.num_programs(1) - 1)
    def _():
        o_ref[...] = x_ref[...] + _rms(acc_ref[...], gpost_ref[...])


def mlp(x, w1, w2, gpre, gpost, *, tm, tf):
    M, D = x.shape
    FF = w1.shape[1]
    return pl.pallas_call(
        _mlp_kernel,
        grid=(M // tm, FF // tf),
        in_specs=[pl.BlockSpec((tm, D), lambda i, j: (i, 0)),
                  pl.BlockSpec((D, tf), lambda i, j: (0, j)),
                  pl.BlockSpec((tf, D), lambda i, j: (j, 0)),
                  pl.BlockSpec((1, D), lambda i, j: (0, 0)),
                  pl.BlockSpec((1, D), lambda i, j: (0, 0))],
        out_specs=pl.BlockSpec((tm, D), lambda i, j: (i, 0)),
        out_shape=jax.ShapeDtypeStruct((M, D), F32),
        scratch_shapes=[pltpu.VMEM((tm, D), BF16), pltpu.VMEM((tm, D), F32)],
        compiler_params=_cparams(("parallel", "arbitrary")),
        name="mlp",
    )(x, w1, w2, gpre, gpost)


def _rot_half_cols(w):
    half = w.shape[-1] // 2
    return jnp.concatenate([-w[..., half:], w[..., :half]], axis=-1)


def _pack_w_in(w):
    widths = (MLA_Q_LORA, MLA_KV_LORA, MLA_ROPE, 1024, 1024, 1024, IDX_HEADS * IDX_DIM, IDX_DIM, IDX_HEADS,
              1024, 256, 256, 256, 256, 256, 256, 3 * NSA_HEADS, N_BRANCH * D_MODEL)
    offs = np.concatenate([[0], np.cumsum(widths)])
    seg = [w[:, offs[i]:offs[i + 1]] for i in range(len(widths))]
    (q_lat, kv_lat, k_rope, dsa_q, dsa_k, dsa_v, idx_q, idx_k, idx_w,
     nsa_q, kc, vc, ks, vs, kw, vw, nsa_gate, bgate) = seg
    z = lambda n: jnp.zeros((w.shape[0], n), w.dtype)
    cols = [dsa_q, dsa_k, dsa_v, nsa_q, bgate, idx_q, kc, vc, ks, vs, kw, vw,
            q_lat, kv_lat, k_rope, z(LANE - MLA_ROPE), _rot_half_cols(k_rope), z(LANE - MLA_ROPE), z(LANE),
            idx_k, idx_k, idx_w, z(LANE - IDX_HEADS), nsa_gate, z(LANE - 3 * NSA_HEADS)]
    out = jnp.concatenate(cols, axis=1)
    out = jnp.concatenate([out, z(N_PACK - out.shape[1])], axis=1)
    return out.astype(BF16)


def _pack_mla_weights(w_uq, w_ukv):
    H = MLA_HEADS
    uq = w_uq.reshape(MLA_Q_LORA, H, MLA_NOPE + MLA_ROPE)
    nope = uq[:, :, :MLA_NOPE].reshape(MLA_Q_LORA, H * MLA_NOPE)
    rope = uq[:, :, MLA_NOPE:]
    pad = jnp.zeros((MLA_Q_LORA, H, LANE - MLA_ROPE), w_uq.dtype)
    wqr = jnp.concatenate([rope, pad], axis=-1).reshape(MLA_Q_LORA, H * LANE)
    wqrr = jnp.concatenate([_rot_half_cols(rope), pad], axis=-1).reshape(MLA_Q_LORA, H * LANE)
    ukv = w_ukv.reshape(MLA_KV_LORA, H, MLA_NOPE + MLA_V)
    wkv = jnp.concatenate([ukv[:, :, :MLA_NOPE].reshape(MLA_KV_LORA, H * MLA_NOPE),
                           ukv[:, :, MLA_NOPE:].reshape(MLA_KV_LORA, H * MLA_V)], axis=1)
    return nope.astype(BF16), wqr.astype(BF16), wqrr.astype(BF16), wkv.astype(BF16)


def _constants(S):
    half = MLA_ROPE // 2
    inv_freq = ROPE_BASE ** (-np.arange(0, MLA_ROPE, 2, dtype=np.float32) / MLA_ROPE)
    freq = np.zeros((1, LANE), np.float32)
    freq[0, :half] = inv_freq
    freq[0, half:2 * half] = inv_freq
    ncp = S // CMP_STRIDE
    ns = S // SEL_BLOCK
    c_start = np.arange(ncp) * CMP_STRIDE
    j_start = np.arange(ns) * SEL_BLOCK
    overlap = ((c_start[None, :] < j_start[:, None] + SEL_BLOCK) &
               (c_start[None, :] + CMP_LEN > j_start[:, None])).astype(np.float32)
    overlap[:, ncp - 1] = 0.0
    ovt = np.zeros((LANE, ncp), np.float32)
    ovt[:ns] = overlap
    ind = np.zeros((S, LANE), np.float32)
    ind[np.arange(S), np.arange(S) // SEL_BLOCK] = 1.0
    return jnp.asarray(freq), jnp.asarray(ovt, BF16), jnp.asarray(ind, BF16)


def kernel(x, mem, positions, rel_bias, norm_gains, w_in, mla_q_norm, mla_kv_norm, mla_w_uq, mla_w_ukv,
           nsa_cmp_pos, nsa_cmp_w1, nsa_cmp_w2, w_branch, w_out, cross_wq, cross_wkv, cross_wo, mlp_w1, mlp_w2):
    B, S, D = x.shape
    M = B * S
    depth = w_in.shape[0]
    tq = TQ if S % TQ == 0 else TQ_NSA
    tqn = TQ_NSA
    assert S % tq == 0 and S % tqn == 0 and D == D_MODEL
    topk = min(DSA_TOPK_MAX, S // 4)
    freq, ovt, ind = _constants(S)
    btiles = bias_tiles(rel_bias, tq, 0, DSA_HEADS, nspan=2, masked=False, stack=1)
    btiles_n = bias_tiles(rel_bias, tqn, DSA_HEADS, NSA_HEADS, nspan=WINDOW // tqn + 1, masked=True,
                          stack=NSA_HG // NSA_STREAMS)
    pos = positions.reshape(M, 1)
    xf = x.reshape(M, D)
    memf = mem.reshape(B * mem.shape[1], D)
    tm_big = 1024 if M % 1024 == 0 else tq
    tm = 512 if M % 512 == 0 else tq
    ncp = S // CMP_STRIDE
    row = lambda v: v.reshape(1, -1)

    for l in range(depth):
        g = norm_gains[l]
        proj = norm_matmul(xf, row(g[0]), _pack_w_in(w_in[l]), tm=tm_big, tn=TN_IN)

        wqn, wqr, wqrr, wkv = _pack_mla_weights(mla_w_uq[l], mla_w_ukv[l])
        qn, qr, kn, vv, kr = mla_prep(proj, pos, freq, row(mla_q_norm[l]), row(mla_kv_norm[l]),
                                      wqn, wqr, wqrr, wkv, tm=tm_big)
        o_a = mla_attention(qn, qr, kn, kr, vv, B=B, S=S, tq=tq)

        maskb = dsa_index_mask(proj, B=B, S=S, tqi=TQ_IDX, tk=tq, topk=topk)
        o_b = dsa_attention(proj, maskb, btiles, B=B, S=S, tq=tq)

        posflat = jnp.broadcast_to(nsa_cmp_pos[l].reshape(2, 1, CMP_LEN * NSA_HEAD_DIM),
                                   (2, 8, CMP_LEN * NSA_HEAD_DIM)).astype(BF16)
        kvc = nsa_compress(proj, posflat, nsa_cmp_w1[l].astype(BF16), nsa_cmp_w2[l].astype(BF16), B=B, S=S)
        o_cmp, selneg = nsa_cmp_attention(proj, kvc, ovt, B=B, S=S, tq=TQ_CMP if S % TQ_CMP == 0 else tq)
        o_c = nsa_main(proj, selneg, ind, btiles_n, o_cmp, B=B, S=S, tq=tqn)

        xf = merge_branches(xf, o_a, o_b, o_c, proj, w_branch[l].astype(BF16), w_out[l].astype(BF16),
                            row(g[1]), tm=tm)

        mkv = norm_matmul(memf, row(g[3]), cross_wkv[l].astype(BF16), tm=memf.shape[0] // B, tn=1024)
        xf = cross_attention(xf, mkv, cross_wq[l].astype(BF16), cross_wo[l].astype(BF16),
                             row(g[2]), row(g[4]), S=S, tm=tm_big)

        xf = mlp(xf, mlp_w1[l].astype(BF16), mlp_w2[l].astype(BF16), row(g[5]), row(g[6]), tm=tm_big, tf=1024)

    return xf.reshape(B, S, D)
```

```python
import functools
import math

import numpy as np
import jax
import jax.numpy as jnp
from jax import lax
from jax.experimental import pallas as pl
from jax.experimental.pallas import tpu as pltpu

F32 = jnp.float32
BF16 = jnp.bfloat16

D_MODEL = 1024
EPS = 1e-6
MLA_HEADS = 8
MLA_Q_LORA = 384
MLA_KV_LORA = 256
MLA_NOPE = 128
MLA_ROPE = 64
MLA_V = 128
ROPE_BASE = 10000.0
DSA_HEADS = 8
DSA_HEAD_DIM = 128
IDX_HEADS = 8
IDX_DIM = 64
DSA_TOPK_MAX = 256
NSA_HEADS = 8
NSA_GROUPS = 2
NSA_HG = NSA_HEADS // NSA_GROUPS
NSA_HEAD_DIM = 128
CMP_STRIDE = 16
CMP_LEN = 2 * CMP_STRIDE
CMP_HIDDEN = 256
SEL_BLOCK = 64
SEL_COUNT = 16
WINDOW = 512
REL_BUCKETS = 32
REL_MAX_DIST = 128
REL_HEADS = DSA_HEADS + NSA_HEADS
CROSS_HEADS = 4
CROSS_HEAD_DIM = D_MODEL // CROSS_HEADS
D_FF = 4 * D_MODEL
N_BRANCH = 3

LANE = 128
NEG = -1e30
VMEM_LIMIT = 48 * 1024 * 1024

TQ = 512
TQ_NSA = 256
TQ_CMP = 1024
HPS = 2
FAR_GROUP = 4
NEAR = 4
NEAR_NSA = 4
FAR_GROUP_NSA = 8
NSA_STREAMS = 2
BISECT_ITERS = 32
IDX_TIE_CHECK = 16
TQ_IDX = 512
IDX_CHUNK = 256

OFF_DSA_Q = 0
OFF_DSA_K = 1024
OFF_DSA_V = 2048
OFF_NSA_Q = 3072
OFF_BGATE = 4096
OFF_IDX_Q = 7168
OFF_NSA_KV = 7680
OFF_MLA = 9216
OFF_IDX_K = 10240
OFF_IDX_W = 10368
OFF_NSA_GATE = 10496
N_PACK = 10752
TN_IN = 2688


def _cparams(sem):
    return pltpu.CompilerParams(dimension_semantics=sem, vmem_limit_bytes=VMEM_LIMIT)


def _rms(x, g):
    return x * lax.rsqrt(jnp.mean(x * x, axis=-1, keepdims=True) + EPS) * g


def _dot(a, b):
    return jnp.dot(a, b, preferred_element_type=F32)


def _dot_nt(a, b):
    return lax.dot_general(a, b, (((1,), (1,)), ((), ())), preferred_element_type=F32)


def _norm_mm_kernel(x_ref, g_ref, w_ref, o_ref, xn_ref):
    @pl.when(pl.program_id(1) == 0)
    def _():
        xn_ref[...] = _rms(x_ref[...].astype(F32), g_ref[...]).astype(BF16)

    o_ref[...] = _dot(xn_ref[...], w_ref[...]).astype(o_ref.dtype)


def norm_matmul(x, g, w, *, tm, tn):
    M, K = x.shape
    N = w.shape[1]
    return pl.pallas_call(
        _norm_mm_kernel,
        grid=(M // tm, N // tn),
        in_specs=[pl.BlockSpec((tm, K), lambda i, j: (i, 0)),
                  pl.BlockSpec((1, K), lambda i, j: (0, 0)),
                  pl.BlockSpec((K, tn), lambda i, j: (0, j))],
        out_specs=pl.BlockSpec((tm, tn), lambda i, j: (i, j)),
        out_shape=jax.ShapeDtypeStruct((M, N), BF16),
        scratch_shapes=[pltpu.VMEM((tm, K), BF16)],
        compiler_params=_cparams(("parallel", "arbitrary")),
        name="norm_matmul",
    )(x, g, w)


def _bias_tiles_kernel(tab_ref, bkt_ref, o_ref, *, inv_scale, h0, tq, nspan, masked):
    h = pl.program_id(0) + h0
    bkt = bkt_ref[...]
    far = tab_ref[REL_BUCKETS - 1, h]
    out = jnp.zeros(bkt.shape, F32)
    for b in range(REL_BUCKETS - 1):
        out = jnp.where(bkt == b, (tab_ref[b, h] - far) * inv_scale, out)
    diag, prev = out[0], out[1]
    r, c = _tri(tq, tq)
    if masked:
        diag = jnp.where(r >= c, diag, NEG)
    o_ref[:, (nspan - 1) * tq:nspan * tq] = diag
    o_ref[:, (nspan - 2) * tq:(nspan - 1) * tq] = prev
    for k in range(nspan - 2):
        edge = jnp.where(r < c, 0.0, NEG) if (masked and k == 0) else jnp.zeros((tq, tq), F32)
        o_ref[:, k * tq:(k + 1) * tq] = edge


def _bucket_tiles(tq):
    i = np.arange(tq)[:, None]
    j = np.arange(tq)[None, :]
    d = np.stack([i - j, tq + i - j]).astype(np.int32)
    n = jnp.maximum(jnp.asarray(d), 0)
    exact = REL_BUCKETS // 2
    nf = jnp.maximum(n, 1).astype(F32)
    log_b = exact + (jnp.log(nf / exact) / math.log(REL_MAX_DIST / exact) * (REL_BUCKETS - exact)).astype(jnp.int32)
    return jnp.where(n < exact, n, jnp.minimum(log_b, REL_BUCKETS - 1)).astype(jnp.int32)


def bias_tiles(rel_bias, tq, h0, nh, *, nspan, masked, stack):
    assert tq >= REL_MAX_DIST
    bkt = _bucket_tiles(tq)
    assert DSA_HEAD_DIM == NSA_HEAD_DIM
    return pl.pallas_call(
        functools.partial(_bias_tiles_kernel, inv_scale=DSA_HEAD_DIM ** 0.5, h0=h0, tq=tq, nspan=nspan,
                          masked=masked),
        grid=(nh,),
        in_specs=[pl.BlockSpec(memory_space=pltpu.SMEM),
                  pl.BlockSpec((2, tq, tq), lambda h: (0, 0, 0))],
        out_specs=pl.BlockSpec((None, tq, nspan * tq), lambda h: (h // stack, h % stack, 0)),
        out_shape=jax.ShapeDtypeStruct((nh // stack, stack * tq, nspan * tq), F32),
        compiler_params=_cparams(("arbitrary",)),
        name="bias_tiles",
    )(rel_bias, bkt)


def _mla_prep_kernel(lat_ref, pos_ref, freq_ref, qg_ref, kvg_ref, wqn_ref, wqr_ref, wqrr_ref, wkv_ref,
                     qn_ref, qr_ref, kn_ref, v_ref, kr_ref):
    lat = lat_ref[...].astype(F32)
    qlat = _rms(lat[:, :MLA_Q_LORA], qg_ref[...]).astype(BF16)
    kvlat = _rms(lat[:, MLA_Q_LORA:MLA_Q_LORA + MLA_KV_LORA], kvg_ref[...]).astype(BF16)
    o = MLA_Q_LORA + MLA_KV_LORA
    kr = lat[:, o:o + LANE]
    krr = lat[:, o + LANE:o + 2 * LANE]
    ang = pos_ref[...].astype(F32) * freq_ref[...]
    cos, sin = jnp.cos(ang), jnp.sin(ang)
    cos8 = jnp.concatenate([cos] * MLA_HEADS, axis=1)
    sin8 = jnp.concatenate([sin] * MLA_HEADS, axis=1)
    qn_ref[...] = _dot(qlat, wqn_ref[...]).astype(BF16)
    qr_ref[...] = (_dot(qlat, wqr_ref[...]) * cos8 + _dot(qlat, wqrr_ref[...]) * sin8).astype(BF16)
    kv = _dot(kvlat, wkv_ref[...])
    nk = MLA_HEADS * MLA_NOPE
    kn_ref[...] = kv[:, :nk].astype(BF16)
    v_ref[...] = kv[:, nk:].astype(BF16)
    kr_ref[...] = (kr * cos + krr * sin).astype(BF16)


def mla_prep(proj, pos, freq, qg, kvg, wqn, wqr, wqrr, wkv, *, tm):
    M = proj.shape[0]
    HD = MLA_HEADS * LANE
    full = lambda a: pl.BlockSpec(a.shape, lambda i: (0, 0))
    outs = [jax.ShapeDtypeStruct((M, HD), BF16)] * 4 + [jax.ShapeDtypeStruct((M, LANE), BF16)]
    return pl.pallas_call(
        _mla_prep_kernel,
        grid=(M // tm,),
        in_specs=[pl.BlockSpec((tm, 1024), lambda i: (i, OFF_MLA // 1024)),
                  pl.BlockSpec((tm, 1), lambda i: (i, 0)),
                  full(freq), full(qg), full(kvg), full(wqn), full(wqr), full(wqrr), full(wkv)],
        out_specs=[pl.BlockSpec((tm, HD), lambda i: (i, 0))] * 4 + [pl.BlockSpec((tm, LANE), lambda i: (i, 0))],
        out_shape=outs,
        compiler_params=_cparams(("parallel",)),
        name="mla_prep",
    )(proj, pos, freq, qg, kvg, wqn, wqr, wqrr, wkv)


def _softmax_init(m_ref, acc_ref):
    m_ref[...] = jnp.full(m_ref.shape, NEG, F32)
    acc_ref[...] = jnp.zeros(acc_ref.shape, F32)


def _with_ones(v):
    return jnp.concatenate([v, jnp.ones(v.shape, v.dtype)], axis=1)


def _softmax_step(s, v, m_ref, acc_ref, scale):
    c = scale * math.log2(math.e)
    m_prev = m_ref[...]
    m_new = jnp.maximum(m_prev, jnp.max(s, axis=-1, keepdims=True))
    alpha = jnp.exp2((m_prev - m_new) * c)
    p = jnp.exp2((s - jnp.tile(m_new, (1, s.shape[1] // LANE))) * c)
    acc_ref[...] = jnp.tile(alpha, (1, 2)) * acc_ref[...] + _dot(p.astype(BF16), _with_ones(v))
    m_ref[...] = m_new


def _softmax_finish(acc_ref):
    acc = acc_ref[...]
    return acc[:, :LANE] / acc[:, LANE:]


def _far_chunks(n, step, largest):
    def body(p, c):
        step(largest * p, largest)
        return c

    lax.fori_loop(0, n // largest, body, 0)
    w = largest // 2
    while w >= 1:
        @pl.when(n % (2 * w) >= w)
        def _(w=w):
            step(n // (2 * w) * (2 * w), w)
        w //= 2


def _tri(tq, tk):
    row = lax.broadcasted_iota(jnp.int32, (tq, tk), 0)
    col = lax.broadcasted_iota(jnp.int32, (tq, tk), 1)
    return row, col


def _mla_attn_kernel(qn_ref, qr_ref, kn_ref, kr_ref, v_ref, o_ref, m_ref, acc_ref, *, tq, scale):
    qi = pl.program_id(2)
    hs = range(HPS)
    q = [jnp.concatenate([qn_ref[:, h * LANE:(h + 1) * LANE], qr_ref[:, h * LANE:(h + 1) * LANE]], axis=-1)
         for h in hs]
    for h in hs:
        _softmax_init(m_ref.at[h], acc_ref.at[h])

    def step(j, nc, causal_last):
        sl = pl.ds(j * tq if isinstance(j, int) else pl.multiple_of(j * tq, tq), nc * tq)
        kr = kr_ref[sl, :]
        if causal_last:
            row, col = _tri(tq, nc * tq)
            ok = col <= row + (nc - 1) * tq
        s = [_dot_nt(q[h], jnp.concatenate([kn_ref[sl, h * LANE:(h + 1) * LANE], kr], axis=-1)) for h in hs]
        for h in hs:
            sh = jnp.where(ok, s[h], NEG) if causal_last else s[h]
            _softmax_step(sh, v_ref[sl, h * LANE:(h + 1) * LANE], m_ref.at[h], acc_ref.at[h], scale)

    _far_chunks(jnp.maximum(qi - (NEAR - 1), 0), lambda j, nc: step(j, nc, False), FAR_GROUP)
    for k in range(NEAR - 1):
        @pl.when(qi == k)
        def _(k=k):
            step(0, k + 1, True)

    @pl.when(qi >= NEAR - 1)
    def _():
        step(qi - (NEAR - 1), NEAR, True)

    for h in hs:
        o_ref[:, h * LANE:(h + 1) * LANE] = _softmax_finish(acc_ref.at[h]).astype(o_ref.dtype)


def mla_attention(qn, qr, kn, kr, v, *, B, S, tq):
    M = B * S
    nq = S // tq
    H = MLA_HEADS
    scale = (MLA_NOPE + MLA_ROPE) ** -0.5
    w = HPS * LANE
    qspec = pl.BlockSpec((tq, w), lambda b, h, i: (b * nq + i, h))
    kspec = pl.BlockSpec((S, w), lambda b, h, i: (b, h))
    return pl.pallas_call(
        functools.partial(_mla_attn_kernel, tq=tq, scale=scale),
        grid=(B, H // HPS, nq),
        in_specs=[qspec, qspec, kspec, pl.BlockSpec((S, LANE), lambda b, h, i: (b, 0)), kspec],
        out_specs=qspec,
        out_shape=jax.ShapeDtypeStruct((M, H * MLA_V), BF16),
        scratch_shapes=[pltpu.VMEM((HPS, tq, LANE), F32), pltpu.VMEM((HPS, tq, 2 * LANE), F32)],
        compiler_params=_cparams(("parallel", "parallel", "arbitrary")),
        name="mla_attention",
    )(qn, qr, kn, kr, v)


def _dsa_index_kernel(iq_ref, ik_ref, iw_ref, tri_ref, mb_ref, sc_ref, *, tq, tk, S, topk, iters):
    t0 = pl.program_id(1) * tq
    nk = S // tk
    nch = (t0 + tq - 1) // tk + 1
    hw = IDX_CHUNK
    iq = iq_ref[...]
    lane = lax.broadcasted_iota(jnp.int32, (tq, LANE), 1)
    qs = []
    for h in range(IDX_HEADS):
        blk = iq[:, (h // 2) * LANE:(h // 2 + 1) * LANE]
        keep = (lane >= IDX_DIM) if h % 2 else (lane < IDX_DIM)
        qs.append(jnp.where(keep, blk, jnp.zeros_like(blk)))
    q8 = jnp.concatenate(qs, axis=0)
    iwt = iw_ref[...].astype(F32).T
    wrow = [iwt[h:h + 1, :] for h in range(IDX_HEADS)]
    qpos = lax.broadcasted_iota(jnp.int32, (hw, tq), 1) + t0
    krow = lax.broadcasted_iota(jnp.int32, (hw, tq), 0)

    def fold(x, op):
        out = x[0:8]
        for r in range(1, x.shape[0] // 8):
            out = op(out, x[r * 8:(r + 1) * 8])
        return out

    def score_chunk(c, carry):
        mn, mx = carry
        for half in range(tk // hw):
            k0 = pl.multiple_of(c * tk + half * hw, hw)
            res = _dot_nt(ik_ref[pl.ds(k0, hw), :], q8)
            sc = wrow[0] * jnp.maximum(res[:, 0:tq], 0.0)
            for h in range(1, IDX_HEADS):
                sc = sc + wrow[h] * jnp.maximum(res[:, h * tq:(h + 1) * tq], 0.0)
            causal = (krow + k0) <= qpos
            mn = jnp.minimum(mn, fold(jnp.where(causal, sc, -NEG), jnp.minimum))
            sc = jnp.where(causal, sc, NEG)
            mx = jnp.maximum(mx, fold(sc, jnp.maximum))
            sc_ref[c, half * hw:(half + 1) * hw, :] = sc
        return mn, mx

    mn, mx = lax.fori_loop(0, nch, score_chunk, (jnp.full((8, tq), -NEG, F32), jnp.full((8, tq), NEG, F32)))
    n_causal = (lax.broadcasted_iota(jnp.int32, (1, tq), 1) + (t0 + 1)).astype(F32)
    kf = jnp.minimum(float(topk), n_causal)

    def count(pred):
        def body(c, acc):
            return acc + fold(jnp.where(pred(sc_ref[c]), 1.0, 0.0), jnp.add)

        return jnp.sum(lax.fori_loop(0, nch, body, jnp.zeros((8, tq), F32)), axis=0, keepdims=True)

    lo0 = jnp.min(mn, axis=0, keepdims=True)
    mx = jnp.max(mx, axis=0, keepdims=True)
    hi0 = mx + (jnp.abs(mx) * 1e-6 + 1.0)

    def bisect(c):
        lo, hi, n_lo, n_hi = c
        mid = 0.5 * (lo + hi)
        n_mid = count(lambda blk: blk >= mid)
        ge = n_mid >= kf
        return (jnp.where(ge, mid, lo), jnp.where(ge, hi, mid),
                jnp.where(ge, n_mid, n_lo), jnp.where(ge, n_hi, n_mid))

    def open_rows(n_lo, n_hi, settled):
        return (n_lo - kf) * jnp.where(n_lo - n_hi > 2.0, 1.0, 0.0) * (1.0 - settled)

    def search(state, limit, settled):
        def cond(c):
            return jnp.logical_and(c[0] < limit, jnp.max(open_rows(c[3], c[4], settled)) > 0.0)

        def body(c):
            return (c[0] + 2,) + bisect(bisect(c[1:]))

        return lax.while_loop(cond, body, state)

    def band_min_max(lo, hi):
        def body(c, carry):
            blk = sc_ref[c]
            inside = (blk >= lo) & (blk < hi)
            return (jnp.minimum(carry[0], fold(jnp.where(inside, blk, -NEG), jnp.minimum)),
                    jnp.maximum(carry[1], fold(jnp.where(inside, blk, NEG), jnp.maximum)))

        bmin, bmax = lax.fori_loop(0, nch, body, (jnp.full((8, tq), -NEG, F32), jnp.full((8, tq), NEG, F32)))
        return jnp.min(bmin, axis=0, keepdims=True), jnp.max(bmax, axis=0, keepdims=True)

    def finish(state):
        _, lo, hi, n_lo, n_hi = state
        over = n_lo > kf
        bmin, bmax = lax.cond(jnp.max(n_lo - kf) > 0.0, lambda: band_min_max(lo, hi), lambda: (lo, lo))
        single = bmin == bmax
        split = over & (n_lo - n_hi == 2.0) & jnp.logical_not(single)
        unsplit = over & jnp.logical_not(split)
        return jnp.where(split, bmax, lo), jnp.where(over & single, 1.0, 0.0), jnp.where(unsplit, 1.0, 0.0)

    zero = jnp.zeros((1, tq), F32)
    state1 = search((jnp.int32(0), lo0, hi0, n_causal, zero), IDX_TIE_CHECK, zero)
    first = finish(state1)
    state = search(state1, iters, first[1])
    lo, _, unsplit = lax.cond(state[0] > state1[0], lambda: finish(state), lambda: first)
    hi, n_hi = state[2], state[4]
    tied = jnp.max(unsplit) > 0.0

    def fill(j, c):
        mb_ref[j] = jnp.full((tq, tk), NEG, mb_ref.dtype)
        return c

    lax.fori_loop(nch, nk, fill, 0)

    def emit(j, keep):
        mb_ref[j] = jnp.where(keep, 0.0, NEG).T.astype(mb_ref.dtype)

    @pl.when(jnp.logical_not(tied))
    def _():
        def body(j, c):
            emit(j, sc_ref[j] >= lo)
            return c

        lax.fori_loop(0, nch, body, 0)

    @pl.when(tied)
    def _():
        need = kf - n_hi
        tri = tri_ref[...]

        def body(j, base):
            sc = sc_ref[j]
            above = sc >= hi
            band = (sc >= lo) & jnp.logical_not(above)
            prefix = _dot(tri, jnp.where(band, 1.0, 0.0).astype(BF16)) + base
            emit(j, above | (band & (prefix <= need)))
            return prefix[tk - 1:tk, :]

        lax.fori_loop(0, nch, body, zero)


def dsa_index_mask(proj, *, B, S, tqi, tk, topk):
    nq = S // tqi
    nk = S // tk
    assert tk % tqi == 0 and tk % IDX_CHUNK == 0
    tri = jnp.asarray(np.tril(np.ones((tk, tk), np.float32)), BF16)
    return pl.pallas_call(
        functools.partial(_dsa_index_kernel, tq=tqi, tk=tk, S=S, topk=topk, iters=BISECT_ITERS),
        grid=(B, nq),
        in_specs=[pl.BlockSpec((tqi, 512), lambda b, i: (b * nq + i, OFF_IDX_Q // 512)),
                  pl.BlockSpec((S, LANE), lambda b, i: (b, OFF_IDX_K // LANE)),
                  pl.BlockSpec((tqi, LANE), lambda b, i: (b * nq + i, OFF_IDX_W // LANE)),
                  pl.BlockSpec((tk, tk), lambda b, i: (0, 0))],
        out_specs=pl.BlockSpec((None, nk, tqi, tk), lambda b, i: (b, 0, i, 0)),
        out_shape=jax.ShapeDtypeStruct((B, nk, S, tk), BF16),
        scratch_shapes=[pltpu.VMEM((nk, tk, tqi), F32)],
        compiler_params=_cparams(("parallel", "parallel")),
        name="dsa_index_mask",
    )(proj, proj, proj, tri)


def _dsa_attn_kernel(q_ref, k_ref, v_ref, mb_ref, bias_ref, o_ref, m_ref, acc_ref, *, tq, scale):
    qi = pl.program_id(2)
    hs = range(HPS)
    cols = lambda h: slice(h * LANE, (h + 1) * LANE)
    q = [q_ref[:, cols(h)] for h in hs]
    for h in hs:
        _softmax_init(m_ref.at[h], acc_ref.at[h])

    def step(j, nc, bias):
        sl = pl.ds(j * tq if isinstance(j, int) else pl.multiple_of(j * tq, tq), nc * tq)
        mb = mb_ref[j] if nc == 1 else jnp.concatenate([mb_ref[j + i] for i in range(nc)], axis=1)
        mb = mb.astype(F32)
        s = [_dot_nt(q[h], k_ref[sl, cols(h)]) + mb for h in hs]
        for h in hs:
            sh = s[h]
            if bias and nc == 1:
                sh = sh + bias_ref[h, :, tq:2 * tq]
            elif bias and nc == 2:
                sh = sh + bias_ref[h]
            elif bias:
                sh = jnp.concatenate([sh[:, :(nc - 2) * tq], sh[:, (nc - 2) * tq:] + bias_ref[h]], axis=1)
            _softmax_step(sh, v_ref[sl, cols(h)], m_ref.at[h], acc_ref.at[h], scale)

    _far_chunks(jnp.maximum(qi - (NEAR - 1), 0), lambda j, nc: step(j, nc, False), FAR_GROUP)
    for k in range(NEAR - 1):
        @pl.when(qi == k)
        def _(k=k):
            step(0, k + 1, True)

    @pl.when(qi >= NEAR - 1)
    def _():
        step(qi - (NEAR - 1), NEAR, True)

    for h in hs:
        o_ref[:, cols(h)] = _softmax_finish(acc_ref.at[h]).astype(o_ref.dtype)


def dsa_attention(proj, maskb, btiles, *, B, S, tq):
    M = B * S
    nq = S // tq
    H = DSA_HEADS
    scale = DSA_HEAD_DIM ** -0.5
    w = HPS * LANE
    return pl.pallas_call(
        functools.partial(_dsa_attn_kernel, tq=tq, scale=scale),
        grid=(B, H // HPS, nq),
        in_specs=[pl.BlockSpec((tq, w), lambda b, h, i: (b * nq + i, OFF_DSA_Q // w + h)),
                  pl.BlockSpec((S, w), lambda b, h, i: (b, OFF_DSA_K // w + h)),
                  pl.BlockSpec((S, w), lambda b, h, i: (b, OFF_DSA_V // w + h)),
                  pl.BlockSpec((None, nq, tq, tq), lambda b, h, i: (b, 0, i, 0)),
                  pl.BlockSpec((HPS, tq, 2 * tq), lambda b, h, i: (h, 0, 0))],
        out_specs=pl.BlockSpec((tq, w), lambda b, h, i: (b * nq + i, h)),
        out_shape=jax.ShapeDtypeStruct((M, H * DSA_HEAD_DIM), BF16),
        scratch_shapes=[pltpu.VMEM((HPS, tq, LANE), F32), pltpu.VMEM((HPS, tq, 2 * LANE), F32)],
        compiler_params=_cparams(("parallel", "parallel", "arbitrary")),
        name="dsa_attention",
    )(proj, proj, proj, maskb, btiles)


def _nsa_compress_kernel(x_ref, pos_ref, w1_ref, w2_ref, o_ref, *, ncp):
    dk = NSA_HEAD_DIM
    posw = _dot(pos_ref[...], w1_ref[...])[0:1, :]
    for g in range(NSA_GROUPS):
        lo = jnp.zeros((ncp, CMP_HIDDEN), F32)
        hi = jnp.zeros((ncp, CMP_HIDDEN), F32)
        for l in range(CMP_STRIDE):
            xs = x_ref[:, l, g * dk:(g + 1) * dk]
            lo = lo + _dot(xs, w1_ref[l * dk:(l + 1) * dk, :])
            hi = hi + _dot(xs, w1_ref[(CMP_STRIDE + l) * dk:(CMP_STRIDE + l + 1) * dk, :])
        hid = lo + pltpu.roll(hi, ncp - 1, 0) + posw
        o_ref[g] = _dot(jax.nn.gelu(hid).astype(BF16), w2_ref[...]).astype(o_ref.dtype)


def nsa_compress(proj, posflat, w1, w2, *, B, S):
    ncp = S // CMP_STRIDE
    G, dk = NSA_GROUPS, NSA_HEAD_DIM
    xkv = proj.reshape(B * ncp, CMP_STRIDE, N_PACK)
    return pl.pallas_call(
        functools.partial(_nsa_compress_kernel, ncp=ncp),
        grid=(2, B),
        in_specs=[pl.BlockSpec((ncp, CMP_STRIDE, G * dk), lambda a, b: (b, 0, OFF_NSA_KV // (G * dk) + a)),
                  pl.BlockSpec((None, 8, CMP_LEN * dk), lambda a, b: (a, 0, 0)),
                  pl.BlockSpec((None, CMP_LEN * dk, CMP_HIDDEN), lambda a, b: (a, 0, 0)),
                  pl.BlockSpec((None, CMP_HIDDEN, dk), lambda a, b: (a, 0, 0))],
        out_specs=pl.BlockSpec((None, None, G, ncp, dk), lambda a, b: (a, b, 0, 0, 0)),
        out_shape=jax.ShapeDtypeStruct((2, B, G, ncp, dk), BF16),
        compiler_params=_cparams(("parallel", "parallel")),
        name="nsa_compress",
    )(xkv, posflat, w1, w2)


def _nsa_cmp_kernel(q_ref, kc_ref, vc_ref, ov_ref, oc_ref, sel_ref, *, tq, ncp, ns, n_sel, scale):
    t0 = pl.program_id(2) * tq
    dk = NSA_HEAD_DIM
    kc = kc_ref[...]
    vc = vc_ref[...]
    trow = lax.broadcasted_iota(jnp.int32, (tq, ncp), 0) + t0
    ccol = lax.broadcasted_iota(jnp.int32, (tq, ncp), 1)
    vis = (ccol * CMP_STRIDE + (CMP_LEN - 1)) <= trow
    psum = jnp.zeros((tq, ncp), F32)
    for h in range(NSA_HG):
        s = jnp.where(vis, _dot_nt(q_ref[:, h * dk:(h + 1) * dk], kc) * scale, NEG)
        m = jnp.max(s, axis=-1, keepdims=True)
        p = jnp.where(vis, jnp.exp(s - m), 0.0)
        d = jnp.sum(p, axis=-1, keepdims=True)
        p = p / jnp.where(d > 0, d, 1.0)
        oc_ref[:, h * dk:(h + 1) * dk] = _dot(p.astype(BF16), vc).astype(oc_ref.dtype)
        psum = psum + p
    p_hi = psum.astype(BF16)
    p_lo = (psum - p_hi.astype(F32)).astype(BF16)
    ov = ov_ref[...]
    imp = _dot_nt(ov, p_hi) + _dot_nt(ov, p_lo)
    imp = imp[:ns]
    jrow = lax.broadcasted_iota(jnp.int32, (ns, tq), 0)
    tcol = lax.broadcasted_iota(jnp.int32, (ns, tq), 1) + t0
    blk_t = tcol // SEL_BLOCK
    forced = (jrow == 0) | (jrow == blk_t) | (jrow == blk_t - 1)
    val = jnp.where(forced, -NEG, imp)
    val = jnp.where(jrow * SEL_BLOCK > tcol, NEG, val)
    rank = jnp.zeros((ns, tq), F32)
    for j in range(ns):
        other = val[j:j + 1, :]
        ahead = (other > val) | ((other == val) & (jrow > j))
        rank = rank + jnp.where(ahead, 1.0, 0.0)
    selneg = jnp.where(rank < float(n_sel), 0.0, NEG)
    if ns < LANE:
        selneg = jnp.concatenate([selneg, jnp.zeros((LANE - ns, tq), F32)], axis=0)
    sel_ref[...] = selneg.T.astype(sel_ref.dtype)


def nsa_cmp_attention(proj, kvc, ovt, *, B, S, tq):
    M = B * S
    nq = S // tq
    G, dk = NSA_GROUPS, NSA_HEAD_DIM
    ncp = S // CMP_STRIDE
    ns = S // SEL_BLOCK
    assert ns <= LANE
    n_sel = min(SEL_COUNT, ns)
    gw = NSA_HG * dk
    return pl.pallas_call(
        functools.partial(_nsa_cmp_kernel, tq=tq, ncp=ncp, ns=ns, n_sel=n_sel, scale=dk ** -0.5),
        grid=(B, G, nq),
        in_specs=[pl.BlockSpec((tq, gw), lambda b, g, i: (b * nq + i, OFF_NSA_Q // gw + g)),
                  pl.BlockSpec((None, None, None, ncp, dk), lambda b, g, i: (0, b, g, 0, 0)),
                  pl.BlockSpec((None, None, None, ncp, dk), lambda b, g, i: (1, b, g, 0, 0)),
                  pl.BlockSpec((LANE, ncp), lambda b, g, i: (0, 0))],
        out_specs=[pl.BlockSpec((tq, gw), lambda b, g, i: (b * nq + i, g)),
                   pl.BlockSpec((None, None, tq, LANE), lambda b, g, i: (b, g, i, 0))],
        out_shape=[jax.ShapeDtypeStruct((M, G * gw), BF16),
                   jax.ShapeDtypeStruct((B, G, S, LANE), BF16)],
        compiler_params=_cparams(("parallel", "parallel", "parallel")),
        name="nsa_cmp_attention",
    )(proj, kvc, kvc, ovt)


def _nsa_main_kernel(q_ref, sel_ref, ks_ref, vs_ref, kw_ref, vw_ref, ind_ref, bias_ref, gate_ref, oc_ref,
                     o_ref, m_ref, acc_ref, *, tq, nback, scale):
    g = pl.program_id(1)
    qi = pl.program_id(2)
    dk = NSA_HEAD_DIM
    HG = NSA_HG
    ns = NSA_STREAMS
    hp = HG // ns
    st = range(ns)
    stack = lambda f: [jnp.concatenate([f(a * hp + i) for i in range(hp)], axis=0) for a in st]
    q = stack(lambda h: q_ref[:, h * dk:(h + 1) * dk])
    sel = jnp.concatenate([sel_ref[...]] * hp, axis=0)
    qa = [jnp.concatenate([q[a], sel], axis=1) for a in st]
    gate = jax.nn.sigmoid(gate_ref[...].astype(F32))

    for a in range(2 * ns):
        _softmax_init(m_ref.at[a], acc_ref.at[a])

    def steps(s, v, fix, base=0):
        for a in st:
            _softmax_step(fix(a, s[a]), v, m_ref.at[base + a], acc_ref.at[base + a], scale)

    wspan = nback + 1

    def rows_from(j, width):
        return pl.ds(j * tq if isinstance(j, int) else pl.multiple_of(j * tq, tq), width)

    def sel_scores(j, width=tq):
        sl = rows_from(j, width)
        ka = jnp.concatenate([ks_ref[sl, :], ind_ref[sl, :]], axis=1)
        return [_dot_nt(qa[a], ka) for a in st], vs_ref[sl, :]

    def sel_far(j, nc):
        s, v = sel_scores(j, nc * tq)
        steps(s, v, lambda a, x: x)

    nsel = NEAR_NSA
    _far_chunks(jnp.maximum(qi - (nsel - 1), 0), sel_far, FAR_GROUP_NSA)

    def win_scores(j, width):
        sl = rows_from(j, width)
        kw = kw_ref[sl, :]
        return [_dot_nt(q[a], kw) for a in st], vw_ref[sl, :]

    def near(first_s, nc_s, first_w, nc_w):
        tile = lambda nc: (lambda a, x: x + bias_ref[a, :, (wspan - nc) * tq:wspan * tq])

        def sel_tile(a, x):
            if nc_s <= 2:
                return tile(nc_s)(a, x)
            return jnp.concatenate([x[:, :(nc_s - 2) * tq], tile(2)(a, x[:, (nc_s - 2) * tq:])], axis=1)

        s_s, v_s = sel_scores(first_s, nc_s * tq)
        s_w, v_w = win_scores(first_w, nc_w * tq)
        steps(s_s, v_s, sel_tile)
        steps(s_w, v_w, tile(nc_w), base=ns)

    nmax = max(nsel, wspan)
    for k in range(nmax - 1):
        @pl.when(qi == k)
        def _(k=k):
            nc_s, nc_w = min(k + 1, nsel), min(k + 1, wspan)
            near(k + 1 - nc_s, nc_s, k + 1 - nc_w, nc_w)

    @pl.when(qi >= nmax - 1)
    def _():
        near(qi - (nsel - 1), nsel, qi - (wspan - 1), wspan)

    o_s = [_softmax_finish(acc_ref.at[a]) for a in st]
    o_w = [_softmax_finish(acc_ref.at[ns + a]) for a in st]

    lane = lax.broadcasted_iota(jnp.int32, gate.shape, 1)
    for h in range(HG):
        c = g * HG + h
        a, sl = h // hp, slice((h % hp) * tq, (h % hp + 1) * tq)
        gc = [jnp.sum(jnp.where(lane == br * NSA_HEADS + c, gate, 0.0), axis=-1, keepdims=True)
              for br in range(3)]
        o = gc[0] * oc_ref[:, h * dk:(h + 1) * dk].astype(F32) + gc[1] * o_s[a][sl] + gc[2] * o_w[a][sl]
        o_ref[:, h * dk:(h + 1) * dk] = o.astype(o_ref.dtype)


def nsa_main(proj, selneg, ind, btiles, o_c, *, B, S, tq):
    M = B * S
    nq = S // tq
    G, dk, HG = NSA_GROUPS, NSA_HEAD_DIM, NSA_HG
    gw = HG * dk
    assert WINDOW % tq == 0 and WINDOW // tq >= 2
    nback = WINDOW // tq
    ns, hp = NSA_STREAMS, HG // NSA_STREAMS
    kv = lambda n: pl.BlockSpec((S, dk), lambda b, g, i, n=n: (b, (OFF_NSA_KV + n * G * dk) // dk + g))
    return pl.pallas_call(
        functools.partial(_nsa_main_kernel, tq=tq, nback=nback, scale=dk ** -0.5),
        grid=(B, G, nq),
        in_specs=[pl.BlockSpec((tq, gw), lambda b, g, i: (b * nq + i, OFF_NSA_Q // gw + g)),
                  pl.BlockSpec((None, None, tq, LANE), lambda b, g, i: (b, g, i, 0)),
                  kv(2), kv(3), kv(4), kv(5),
                  pl.BlockSpec((S, LANE), lambda b, g, i: (0, 0)),
                  pl.BlockSpec((ns, hp * tq, (nback + 1) * tq), lambda b, g, i: (g, 0, 0)),
                  pl.BlockSpec((tq, LANE), lambda b, g, i: (b * nq + i, OFF_NSA_GATE // LANE)),
                  pl.BlockSpec((tq, gw), lambda b, g, i: (b * nq + i, g))],
        out_specs=pl.BlockSpec((tq, gw), lambda b, g, i: (b * nq + i, g)),
        out_shape=jax.ShapeDtypeStruct((M, G * gw), BF16),
        scratch_shapes=[pltpu.VMEM((2 * ns, hp * tq, LANE), F32),
                        pltpu.VMEM((2 * ns, hp * tq, 2 * LANE), F32)],
        compiler_params=_cparams(("parallel", "parallel", "arbitrary")),
        name="nsa_main",
    )(proj, selneg, proj, proj, proj, proj, ind, btiles, proj, o_c)


def _merge_kernel(x_ref, oa_ref, ob_ref, oc_ref, g0_ref, g1_ref, g2_ref, wb_ref, wo_ref, gain_ref, o_ref):
    merged = None
    for o_r, g_r, br in ((oa_ref, g0_ref, 0), (ob_ref, g1_ref, 1), (oc_ref, g2_ref, 2)):
        t = jax.nn.sigmoid(g_r[...].astype(F32)) * _dot(o_r[...], wb_ref[br])
        merged = t if merged is None else merged + t
    y = _dot(merged.astype(BF16), wo_ref[...])
    o_ref[...] = x_ref[...] + _rms(y, gain_ref[...])


def merge_branches(x, o_a, o_b, o_c, proj, wb, wo, gain, *, tm):
    M, D = x.shape
    row = pl.BlockSpec((tm, D), lambda i: (i, 0))
    gate = lambda br: pl.BlockSpec((tm, D), lambda i, br=br: (i, OFF_BGATE // D + br))
    return pl.pallas_call(
        _merge_kernel,
        grid=(M // tm,),
        in_specs=[row, row, row, row, gate(0), gate(1), gate(2),
                  pl.BlockSpec(wb.shape, lambda i: (0, 0, 0)),
                  pl.BlockSpec(wo.shape, lambda i: (0, 0)),
                  pl.BlockSpec((1, D), lambda i: (0, 0))],
        out_specs=row,
        out_shape=jax.ShapeDtypeStruct((M, D), F32),
        compiler_params=_cparams(("parallel",)),
        name="merge_branches",
    )(x, o_a, o_b, o_c, proj, proj, proj, wb, wo, gain)


def _cross_kernel(x_ref, kv_ref, wq_ref, wo_ref, gpre_ref, gpost_ref, o_ref, *, scale):
    x = x_ref[...]
    h = _rms(x, gpre_ref[...]).astype(BF16)
    q = _dot(h, wq_ref[...]).astype(BF16)
    dh = CROSS_HEAD_DIM
    outs = []
    for hd in range(CROSS_HEADS):
        k = kv_ref[:, hd * dh:(hd + 1) * dh]
        v = kv_ref[:, D_MODEL + hd * dh:D_MODEL + (hd + 1) * dh]
        s = _dot_nt(q[:, hd * dh:(hd + 1) * dh], k) * scale
        p = jnp.exp(s - jnp.max(s, axis=-1, keepdims=True))
        p = p / jnp.sum(p, axis=-1, keepdims=True)
        outs.append(_dot(p.astype(BF16), v).astype(BF16))
    y = _dot(jnp.concatenate(outs, axis=1), wo_ref[...])
    o_ref[...] = x + _rms(y, gpost_ref[...])


def cross_attention(x, kv, wq, wo, gpre, gpost, *, S, tm):
    M, D = x.shape
    mlen = kv.shape[0] // (M // S)
    per_b = S // tm
    return pl.pallas_call(
        functools.partial(_cross_kernel, scale=CROSS_HEAD_DIM ** -0.5),
        grid=(M // tm,),
        in_specs=[pl.BlockSpec((tm, D), lambda i: (i, 0)),
                  pl.BlockSpec((mlen, 2 * D), lambda i: (i // per_b, 0)),
                  pl.BlockSpec(wq.shape, lambda i: (0, 0)),
                  pl.BlockSpec(wo.shape, lambda i: (0, 0)),
                  pl.BlockSpec((1, D), lambda i: (0, 0)),
                  pl.BlockSpec((1, D), lambda i: (0, 0))],
        out_specs=pl.BlockSpec((tm, D), lambda i: (i, 0)),
        out_shape=jax.ShapeDtypeStruct((M, D), F32),
        compiler_params=_cparams(("parallel",)),
        name="cross_attention",
    )(x, kv, wq, wo, gpre, gpost)


def _mlp_kernel(x_ref, w1_ref, w2_ref, gpre_ref, gpost_ref, o_ref, h_ref, acc_ref):
    j = pl.program_id(1)

    @pl.when(j == 0)
    def _():
        h_ref[...] = _rms(x_ref[...], gpre_ref[...]).astype(BF16)
        acc_ref[...] = jnp.zeros(acc_ref.shape, F32)

    a = jnp.maximum(_dot(h_ref[...], w1_ref[...]), 0.0)
    acc_ref[...] += _dot((a * a).astype(BF16), w2_ref[...])

    @pl.when(j == pl.num_programs(1) - 1)
    def _():
        o_ref[...] = x_ref[...] + _rms(acc_ref[...], gpost_ref[...])


def mlp(x, w1, w2, gpre, gpost, *, tm, tf):
    M, D = x.shape
    FF = w1.shape[1]
    return pl.pallas_call(
        _mlp_kernel,
        grid=(M // tm, FF // tf),
        in_specs=[pl.BlockSpec((tm, D), lambda i, j: (i, 0)),
                  pl.BlockSpec((D, tf), lambda i, j: (0, j)),
                  pl.BlockSpec((tf, D), lambda i, j: (j, 0)),
                  pl.BlockSpec((1, D), lambda i, j: (0, 0)),
                  pl.BlockSpec((1, D), lambda i, j: (0, 0))],
        out_specs=pl.BlockSpec((tm, D), lambda i, j: (i, 0)),
        out_shape=jax.ShapeDtypeStruct((M, D), F32),
        scratch_shapes=[pltpu.VMEM((tm, D), BF16), pltpu.VMEM((tm, D), F32)],
        compiler_params=_cparams(("parallel", "arbitrary")),
        name="mlp",
    )(x, w1, w2, gpre, gpost)


def _rot_half_cols(w):
    half = w.shape[-1] // 2
    return jnp.concatenate([-w[..., half:], w[..., :half]], axis=-1)


def _pack_w_in(w):
    widths = (MLA_Q_LORA, MLA_KV_LORA, MLA_ROPE, 1024, 1024, 1024, IDX_HEADS * IDX_DIM, IDX_DIM, IDX_HEADS,
              1024, 256, 256, 256, 256, 256, 256, 3 * NSA_HEADS, N_BRANCH * D_MODEL)
    offs = np.concatenate([[0], np.cumsum(widths)])
    seg = [w[:, offs[i]:offs[i + 1]] for i in range(len(widths))]
    (q_lat, kv_lat, k_rope, dsa_q, dsa_k, dsa_v, idx_q, idx_k, idx_w,
     nsa_q, kc, vc, ks, vs, kw, vw, nsa_gate, bgate) = seg
    z = lambda n: jnp.zeros((w.shape[0], n), w.dtype)
    cols = [dsa_q, dsa_k, dsa_v, nsa_q, bgate, idx_q, kc, vc, ks, vs, kw, vw,
            q_lat, kv_lat, k_rope, z(LANE - MLA_ROPE), _rot_half_cols(k_rope), z(LANE - MLA_ROPE), z(LANE),
            idx_k, idx_k, idx_w, z(LANE - IDX_HEADS), nsa_gate, z(LANE - 3 * NSA_HEADS)]
    out = jnp.concatenate(cols, axis=1)
    out = jnp.concatenate([out, z(N_PACK - out.shape[1])], axis=1)
    return out.astype(BF16)


def _pack_mla_weights(w_uq, w_ukv):
    H = MLA_HEADS
    uq = w_uq.reshape(MLA_Q_LORA, H, MLA_NOPE + MLA_ROPE)
    nope = uq[:, :, :MLA_NOPE].reshape(MLA_Q_LORA, H * MLA_NOPE)
    rope = uq[:, :, MLA_NOPE:]
    pad = jnp.zeros((MLA_Q_LORA, H, LANE - MLA_ROPE), w_uq.dtype)
    wqr = jnp.concatenate([rope, pad], axis=-1).reshape(MLA_Q_LORA, H * LANE)
    wqrr = jnp.concatenate([_rot_half_cols(rope), pad], axis=-1).reshape(MLA_Q_LORA, H * LANE)
    ukv = w_ukv.reshape(MLA_KV_LORA, H, MLA_NOPE + MLA_V)
    wkv = jnp.concatenate([ukv[:, :, :MLA_NOPE].reshape(MLA_KV_LORA, H * MLA_NOPE),
                           ukv[:, :, MLA_NOPE:].reshape(MLA_KV_LORA, H * MLA_V)], axis=1)
    return nope.astype(BF16), wqr.astype(BF16), wqrr.astype(BF16), wkv.astype(BF16)


def _constants(S):
    half = MLA_ROPE // 2
    inv_freq = ROPE_BASE ** (-np.arange(0, MLA_ROPE, 2, dtype=np.float32) / MLA_ROPE)
    freq = np.zeros((1, LANE), np.float32)
    freq[0, :half] = inv_freq
    freq[0, half:2 * half] = inv_freq
    ncp = S // CMP_STRIDE
    ns = S // SEL_BLOCK
    c_start = np.arange(ncp) * CMP_STRIDE
    j_start = np.arange(ns) * SEL_BLOCK
    overlap = ((c_start[None, :] < j_start[:, None] + SEL_BLOCK) &
               (c_start[None, :] + CMP_LEN > j_start[:, None])).astype(np.float32)
    overlap[:, ncp - 1] = 0.0
    ovt = np.zeros((LANE, ncp), np.float32)
    ovt[:ns] = overlap
    ind = np.zeros((S, LANE), np.float32)
    ind[np.arange(S), np.arange(S) // SEL_BLOCK] = 1.0
    return jnp.asarray(freq), jnp.asarray(ovt, BF16), jnp.asarray(ind, BF16)


def kernel(x, mem, positions, rel_bias, norm_gains, w_in, mla_q_norm, mla_kv_norm, mla_w_uq, mla_w_ukv,
           nsa_cmp_pos, nsa_cmp_w1, nsa_cmp_w2, w_branch, w_out, cross_wq, cross_wkv, cross_wo, mlp_w1, mlp_w2):
    B, S, D = x.shape
    M = B * S
    depth = w_in.shape[0]
    w_in_bf = w_in.astype(BF16)
    tq = TQ if S % TQ == 0 else TQ_NSA
    tqn = TQ_NSA
    assert S % tq == 0 and S % tqn == 0 and D == D_MODEL
    topk = min(DSA_TOPK_MAX, S // 4)
    freq, ovt, ind = _constants(S)
    btiles = bias_tiles(rel_bias, tq, 0, DSA_HEADS, nspan=2, masked=False, stack=1)
    btiles_n = bias_tiles(rel_bias, tqn, DSA_HEADS, NSA_HEADS, nspan=WINDOW // tqn + 1, masked=True,
                          stack=NSA_HG // NSA_STREAMS)
    pos = positions.reshape(M, 1)
    xf = x.reshape(M, D)
    memf = mem.reshape(B * mem.shape[1], D)
    tm_big = 1024 if M % 1024 == 0 else tq
    tm = 512 if M % 512 == 0 else tq
    ncp = S // CMP_STRIDE
    row = lambda v: v.reshape(1, -1)

    for l in range(depth):
        g = norm_gains[l]
        proj = norm_matmul(xf, row(g[0]), _pack_w_in(w_in_bf[l]), tm=tm_big, tn=TN_IN)

        wqn, wqr, wqrr, wkv = _pack_mla_weights(mla_w_uq[l], mla_w_ukv[l])
        qn, qr, kn, vv, kr = mla_prep(proj, pos, freq, row(mla_q_norm[l]), row(mla_kv_norm[l]),
                                      wqn, wqr, wqrr, wkv, tm=tm_big)
        o_a = mla_attention(qn, qr, kn, kr, vv, B=B, S=S, tq=tq)

        maskb = dsa_index_mask(proj, B=B, S=S, tqi=TQ_IDX, tk=tq, topk=topk)
        o_b = dsa_attention(proj, maskb, btiles, B=B, S=S, tq=tq)

        posflat = jnp.broadcast_to(nsa_cmp_pos[l].reshape(2, 1, CMP_LEN * NSA_HEAD_DIM),
                                   (2, 8, CMP_LEN * NSA_HEAD_DIM)).astype(BF16)
        kvc = nsa_compress(proj, posflat, nsa_cmp_w1[l].astype(BF16), nsa_cmp_w2[l].astype(BF16), B=B, S=S)
        o_cmp, selneg = nsa_cmp_attention(proj, kvc, ovt, B=B, S=S, tq=TQ_CMP if S % TQ_CMP == 0 else tq)
        o_c = nsa_main(proj, selneg, ind, btiles_n, o_cmp, B=B, S=S, tq=tqn)

        xf = merge_branches(xf, o_a, o_b, o_c, proj, w_branch[l].astype(BF16), w_out[l].astype(BF16),
                            row(g[1]), tm=tm)

        mkv = norm_matmul(memf, row(g[3]), cross_wkv[l].astype(BF16), tm=memf.shape[0] // B, tn=1024)
        xf = cross_attention(xf, mkv, cross_wq[l].astype(BF16), cross_wo[l].astype(BF16),
                             row(g[2]), row(g[4]), S=S, tm=tm_big)

        xf = mlp(xf, mlp_w1[l].astype(BF16), mlp_w2[l].astype(BF16), row(g[5]), row(g[6]), tm=tm_big, tf=1024)

    return xf.reshape(B, S, D)
```

```python
import functools
import math

import numpy as np
import jax
import jax.numpy as jnp
from jax import lax
from jax.experimental import pallas as pl
from jax.experimental.pallas import tpu as pltpu

F32 = jnp.float32
BF16 = jnp.bfloat16

D_MODEL = 1024
EPS = 1e-6
MLA_HEADS = 8
MLA_Q_LORA = 384
MLA_KV_LORA = 256
MLA_NOPE = 128
MLA_ROPE = 64
MLA_V = 128
ROPE_BASE = 10000.0
DSA_HEADS = 8
DSA_HEAD_DIM = 128
IDX_HEADS = 8
IDX_DIM = 64
DSA_TOPK_MAX = 256
NSA_HEADS = 8
NSA_GROUPS = 2
NSA_HG = NSA_HEADS // NSA_GROUPS
NSA_HEAD_DIM = 128
CMP_STRIDE = 16
CMP_LEN = 2 * CMP_STRIDE
CMP_HIDDEN = 256
SEL_BLOCK = 64
SEL_COUNT = 16
WINDOW = 512
REL_BUCKETS = 32
REL_MAX_DIST = 128
REL_HEADS = DSA_HEADS + NSA_HEADS
CROSS_HEADS = 4
CROSS_HEAD_DIM = D_MODEL // CROSS_HEADS
D_FF = 4 * D_MODEL
N_BRANCH = 3

LANE = 128
NEG = -1e30
VMEM_LIMIT = 48 * 1024 * 1024

TQ = 512
TQ_NSA = 256
TQ_CMP = 1024
HPS = 2
FAR_GROUP = 4
NEAR = 4
NEAR_NSA = 4
FAR_GROUP_NSA = 8
NSA_STREAMS = 2
BISECT_ITERS = 32
IDX_TIE_CHECK = 16
TQ_IDX = 512
IDX_CHUNK = 256

OFF_DSA_Q = 0
OFF_DSA_K = 1024
OFF_DSA_V = 2048
OFF_NSA_Q = 3072
OFF_BGATE = 4096
OFF_IDX_Q = 7168
OFF_NSA_KV = 7680
OFF_MLA = 9216
OFF_IDX_K = 10240
OFF_IDX_W = 10368
OFF_NSA_GATE = 10496
N_PACK = 10752
TN_IN = 2688


def _cparams(sem):
    return pltpu.CompilerParams(dimension_semantics=sem, vmem_limit_bytes=VMEM_LIMIT)


def _rms(x, g):
    return x * lax.rsqrt(jnp.mean(x * x, axis=-1, keepdims=True) + EPS) * g


def _dot(a, b):
    return jnp.dot(a, b, preferred_element_type=F32)


def _dot_nt(a, b):
    return lax.dot_general(a, b, (((1,), (1,)), ((), ())), preferred_element_type=F32)


def _norm_mm_kernel(x_ref, g_ref, w_ref, o_ref, xn_ref):
    @pl.when(pl.program_id(1) == 0)
    def _():
        xn_ref[...] = _rms(x_ref[...].astype(F32), g_ref[...]).astype(BF16)

    o_ref[...] = _dot(xn_ref[...], w_ref[...]).astype(o_ref.dtype)


def norm_matmul(x, g, w, *, tm, tn):
    M, K = x.shape
    N = w.shape[1]
    return pl.pallas_call(
        _norm_mm_kernel,
        grid=(M // tm, N // tn),
        in_specs=[pl.BlockSpec((tm, K), lambda i, j: (i, 0)),
                  pl.BlockSpec((1, K), lambda i, j: (0, 0)),
                  pl.BlockSpec((K, tn), lambda i, j: (0, j))],
        out_specs=pl.BlockSpec((tm, tn), lambda i, j: (i, j)),
        out_shape=jax.ShapeDtypeStruct((M, N), BF16),
        scratch_shapes=[pltpu.VMEM((tm, K), BF16)],
        compiler_params=_cparams(("parallel", "arbitrary")),
        name="norm_matmul",
    )(x, g, w)


def _bias_tiles_kernel(tab_ref, bkt_ref, o_ref, *, inv_scale, h0, tq, nspan, masked):
    h = pl.program_id(0) + h0
    bkt = bkt_ref[...]
    far = tab_ref[REL_BUCKETS - 1, h]
    out = jnp.zeros(bkt.shape, F32)
    for b in range(REL_BUCKETS - 1):
        out = jnp.where(bkt == b, (tab_ref[b, h] - far) * inv_scale, out)
    diag, prev = out[0], out[1]
    r, c = _tri(tq, tq)
    if masked:
        diag = jnp.where(r >= c, diag, NEG)
    o_ref[:, (nspan - 1) * tq:nspan * tq] = diag
    o_ref[:, (nspan - 2) * tq:(nspan - 1) * tq] = prev
    for k in range(nspan - 2):
        edge = jnp.where(r < c, 0.0, NEG) if (masked and k == 0) else jnp.zeros((tq, tq), F32)
        o_ref[:, k * tq:(k + 1) * tq] = edge


def _bucket_tiles(tq):
    i = np.arange(tq)[:, None]
    j = np.arange(tq)[None, :]
    d = np.stack([i - j, tq + i - j]).astype(np.int32)
    n = jnp.maximum(jnp.asarray(d), 0)
    exact = REL_BUCKETS // 2
    nf = jnp.maximum(n, 1).astype(F32)
    log_b = exact + (jnp.log(nf / exact) / math.log(REL_MAX_DIST / exact) * (REL_BUCKETS - exact)).astype(jnp.int32)
    return jnp.where(n < exact, n, jnp.minimum(log_b, REL_BUCKETS - 1)).astype(jnp.int32)


def bias_tiles(rel_bias, tq, h0, nh, *, nspan, masked, stack):
    assert tq >= REL_MAX_DIST
    bkt = _bucket_tiles(tq)
    assert DSA_HEAD_DIM == NSA_HEAD_DIM
    return pl.pallas_call(
        functools.partial(_bias_tiles_kernel, inv_scale=DSA_HEAD_DIM ** 0.5, h0=h0, tq=tq, nspan=nspan,
                          masked=masked),
        grid=(nh,),
        in_specs=[pl.BlockSpec(memory_space=pltpu.SMEM),
                  pl.BlockSpec((2, tq, tq), lambda h: (0, 0, 0))],
        out_specs=pl.BlockSpec((None, tq, nspan * tq), lambda h: (h // stack, h % stack, 0)),
        out_shape=jax.ShapeDtypeStruct((nh // stack, stack * tq, nspan * tq), F32),
        compiler_params=_cparams(("arbitrary",)),
        name="bias_tiles",
    )(rel_bias, bkt)


def _mla_prep_kernel(lat_ref, pos_ref, freq_ref, qg_ref, kvg_ref, wqn_ref, wqr_ref, wqrr_ref, wkv_ref,
                     qn_ref, qr_ref, kn_ref, v_ref, kr_ref):
    lat = lat_ref[...].astype(F32)
    qlat = _rms(lat[:, :MLA_Q_LORA], qg_ref[...]).astype(BF16)
    kvlat = _rms(lat[:, MLA_Q_LORA:MLA_Q_LORA + MLA_KV_LORA], kvg_ref[...]).astype(BF16)
    o = MLA_Q_LORA + MLA_KV_LORA
    kr = lat[:, o:o + LANE]
    krr = lat[:, o + LANE:o + 2 * LANE]
    ang = pos_ref[...].astype(F32) * freq_ref[...]
    cos, sin = jnp.cos(ang), jnp.sin(ang)
    cos8 = jnp.concatenate([cos] * MLA_HEADS, axis=1)
    sin8 = jnp.concatenate([sin] * MLA_HEADS, axis=1)
    qn_ref[...] = _dot(qlat, wqn_ref[...]).astype(BF16)
    qr_ref[...] = (_dot(qlat, wqr_ref[...]) * cos8 + _dot(qlat, wqrr_ref[...]) * sin8).astype(BF16)
    kv = _dot(kvlat, wkv_ref[...])
    nk = MLA_HEADS * MLA_NOPE
    kn_ref[...] = kv[:, :nk].astype(BF16)
    v_ref[...] = kv[:, nk:].astype(BF16)
    kr_ref[...] = (kr * cos + krr * sin).astype(BF16)


def mla_prep(proj, pos, freq, qg, kvg, wqn, wqr, wqrr, wkv, *, tm):
    M = proj.shape[0]
    HD = MLA_HEADS * LANE
    full = lambda a: pl.BlockSpec(a.shape, lambda i: (0, 0))
    outs = [jax.ShapeDtypeStruct((M, HD), BF16)] * 4 + [jax.ShapeDtypeStruct((M, LANE), BF16)]
    return pl.pallas_call(
        _mla_prep_kernel,
        grid=(M // tm,),
        in_specs=[pl.BlockSpec((tm, 1024), lambda i: (i, OFF_MLA // 1024)),
                  pl.BlockSpec((tm, 1), lambda i: (i, 0)),
                  full(freq), full(qg), full(kvg), full(wqn), full(wqr), full(wqrr), full(wkv)],
        out_specs=[pl.BlockSpec((tm, HD), lambda i: (i, 0))] * 4 + [pl.BlockSpec((tm, LANE), lambda i: (i, 0))],
        out_shape=outs,
        compiler_params=_cparams(("parallel",)),
        name="mla_prep",
    )(proj, pos, freq, qg, kvg, wqn, wqr, wqrr, wkv)


def _softmax_init(m_ref, acc_ref):
    m_ref[...] = jnp.full(m_ref.shape, NEG, F32)
    acc_ref[...] = jnp.zeros(acc_ref.shape, F32)


def _with_ones(v):
    return jnp.concatenate([v, jnp.ones(v.shape, v.dtype)], axis=1)


def _softmax_step(s, v, m_ref, acc_ref, scale):
    c = scale * math.log2(math.e)
    m_prev = m_ref[...]
    m_new = jnp.maximum(m_prev, jnp.max(s, axis=-1, keepdims=True))
    alpha = jnp.exp2((m_prev - m_new) * c)
    p = jnp.exp2((s - jnp.tile(m_new, (1, s.shape[1] // LANE))) * c)
    acc_ref[...] = jnp.tile(alpha, (1, 2)) * acc_ref[...] + _dot(p.astype(BF16), _with_ones(v))
    m_ref[...] = m_new


def _softmax_finish(acc_ref):
    acc = acc_ref[...]
    return acc[:, :LANE] / acc[:, LANE:]


def _far_chunks(n, step, largest):
    def body(p, c):
        step(largest * p, largest)
        return c

    lax.fori_loop(0, n // largest, body, 0)
    w = largest // 2
    while w >= 1:
        @pl.when(n % (2 * w) >= w)
        def _(w=w):
            step(n // (2 * w) * (2 * w), w)
        w //= 2


def _tri(tq, tk):
    row = lax.broadcasted_iota(jnp.int32, (tq, tk), 0)
    col = lax.broadcasted_iota(jnp.int32, (tq, tk), 1)
    return row, col


def _mla_attn_kernel(qn_ref, qr_ref, kn_ref, kr_ref, v_ref, o_ref, m_ref, acc_ref, *, tq, scale):
    qi = pl.program_id(2)
    hs = range(HPS)
    q = [jnp.concatenate([qn_ref[:, h * LANE:(h + 1) * LANE], qr_ref[:, h * LANE:(h + 1) * LANE]], axis=-1)
         for h in hs]
    for h in hs:
        _softmax_init(m_ref.at[h], acc_ref.at[h])

    def step(j, nc, causal_last):
        sl = pl.ds(j * tq if isinstance(j, int) else pl.multiple_of(j * tq, tq), nc * tq)
        kr = kr_ref[sl, :]
        if causal_last:
            row, col = _tri(tq, nc * tq)
            ok = col <= row + (nc - 1) * tq
        s = [_dot_nt(q[h], jnp.concatenate([kn_ref[sl, h * LANE:(h + 1) * LANE], kr], axis=-1)) for h in hs]
        for h in hs:
            sh = jnp.where(ok, s[h], NEG) if causal_last else s[h]
            _softmax_step(sh, v_ref[sl, h * LANE:(h + 1) * LANE], m_ref.at[h], acc_ref.at[h], scale)

    _far_chunks(jnp.maximum(qi - (NEAR - 1), 0), lambda j, nc: step(j, nc, False), FAR_GROUP)
    for k in range(NEAR - 1):
        @pl.when(qi == k)
        def _(k=k):
            step(0, k + 1, True)

    @pl.when(qi >= NEAR - 1)
    def _():
        step(qi - (NEAR - 1), NEAR, True)

    for h in hs:
        o_ref[:, h * LANE:(h + 1) * LANE] = _softmax_finish(acc_ref.at[h]).astype(o_ref.dtype)


def mla_attention(qn, qr, kn, kr, v, *, B, S, tq):
    M = B * S
    nq = S // tq
    H = MLA_HEADS
    scale = (MLA_NOPE + MLA_ROPE) ** -0.5
    w = HPS * LANE
    qspec = pl.BlockSpec((tq, w), lambda b, h, i: (b * nq + i, h))
    kspec = pl.BlockSpec((S, w), lambda b, h, i: (b, h))
    return pl.pallas_call(
        functools.partial(_mla_attn_kernel, tq=tq, scale=scale),
        grid=(B, H // HPS, nq),
        in_specs=[qspec, qspec, kspec, pl.BlockSpec((S, LANE), lambda b, h, i: (b, 0)), kspec],
        out_specs=qspec,
        out_shape=jax.ShapeDtypeStruct((M, H * MLA_V), BF16),
        scratch_shapes=[pltpu.VMEM((HPS, tq, LANE), F32), pltpu.VMEM((HPS, tq, 2 * LANE), F32)],
        compiler_params=_cparams(("parallel", "parallel", "arbitrary")),
        name="mla_attention",
    )(qn, qr, kn, kr, v)


def _dsa_index_kernel(iq_ref, ik_ref, iw_ref, tri_ref, mb_ref, sc_ref, *, tq, tk, S, topk, iters):
    t0 = pl.program_id(1) * tq
    nk = S // tk
    nch = (t0 + tq - 1) // tk + 1
    hw = IDX_CHUNK
    iq = iq_ref[...]
    lane = lax.broadcasted_iota(jnp.int32, (tq, LANE), 1)
    qs = []
    for h in range(IDX_HEADS):
        blk = iq[:, (h // 2) * LANE:(h // 2 + 1) * LANE]
        keep = (lane >= IDX_DIM) if h % 2 else (lane < IDX_DIM)
        qs.append(jnp.where(keep, blk, jnp.zeros_like(blk)))
    q8 = jnp.concatenate(qs, axis=0)
    iwt = iw_ref[...].astype(F32).T
    wrow = [iwt[h:h + 1, :] for h in range(IDX_HEADS)]
    qpos = lax.broadcasted_iota(jnp.int32, (hw, tq), 1) + t0
    krow = lax.broadcasted_iota(jnp.int32, (hw, tq), 0)

    def fold(x, op):
        out = x[0:8]
        for r in range(1, x.shape[0] // 8):
            out = op(out, x[r * 8:(r + 1) * 8])
        return out

    def score_chunk(c, carry):
        mn, mx = carry
        for half in range(tk // hw):
            k0 = pl.multiple_of(c * tk + half * hw, hw)
            res = _dot_nt(ik_ref[pl.ds(k0, hw), :], q8)
            sc = wrow[0] * jnp.maximum(res[:, 0:tq], 0.0)
            for h in range(1, IDX_HEADS):
                sc = sc + wrow[h] * jnp.maximum(res[:, h * tq:(h + 1) * tq], 0.0)
            causal = (krow + k0) <= qpos
            mn = jnp.minimum(mn, fold(jnp.where(causal, sc, -NEG), jnp.minimum))
            sc = jnp.where(causal, sc, NEG)
            mx = jnp.maximum(mx, fold(sc, jnp.maximum))
            sc_ref[c, half * hw:(half + 1) * hw, :] = sc
        return mn, mx

    mn, mx = lax.fori_loop(0, nch, score_chunk, (jnp.full((8, tq), -NEG, F32), jnp.full((8, tq), NEG, F32)))
    n_causal = (lax.broadcasted_iota(jnp.int32, (1, tq), 1) + (t0 + 1)).astype(F32)
    kf = jnp.minimum(float(topk), n_causal)

    def count(pred):
        def body(c, acc):
            return acc + fold(jnp.where(pred(sc_ref[c]), 1.0, 0.0), jnp.add)

        return jnp.sum(lax.fori_loop(0, nch, body, jnp.zeros((8, tq), F32)), axis=0, keepdims=True)

    lo0 = jnp.min(mn, axis=0, keepdims=True)
    mx = jnp.max(mx, axis=0, keepdims=True)
    hi0 = mx + (jnp.abs(mx) * 1e-6 + 1.0)

    def bisect(c):
        lo, hi, n_lo, n_hi = c
        mid = 0.5 * (lo + hi)
        n_mid = count(lambda blk: blk >= mid)
        ge = n_mid >= kf
        return (jnp.where(ge, mid, lo), jnp.where(ge, hi, mid),
                jnp.where(ge, n_mid, n_lo), jnp.where(ge, n_hi, n_mid))

    def open_rows(n_lo, n_hi, settled):
        return (n_lo - kf) * jnp.where(n_lo - n_hi > 2.0, 1.0, 0.0) * (1.0 - settled)

    def search(state, limit, settled):
        def cond(c):
            return jnp.logical_and(c[0] < limit, jnp.max(open_rows(c[3], c[4], settled)) > 0.0)

        def body(c):
            return (c[0] + 2,) + bisect(bisect(c[1:]))

        return lax.while_loop(cond, body, state)

    def band_min_max(lo, hi):
        def body(c, carry):
            blk = sc_ref[c]
            inside = (blk >= lo) & (blk < hi)
            return (jnp.minimum(carry[0], fold(jnp.where(inside, blk, -NEG), jnp.minimum)),
                    jnp.maximum(carry[1], fold(jnp.where(inside, blk, NEG), jnp.maximum)))

        bmin, bmax = lax.fori_loop(0, nch, body, (jnp.full((8, tq), -NEG, F32), jnp.full((8, tq), NEG, F32)))
        return jnp.min(bmin, axis=0, keepdims=True), jnp.max(bmax, axis=0, keepdims=True)

    def finish(state):
        _, lo, hi, n_lo, n_hi = state
        over = n_lo > kf
        bmin, bmax = lax.cond(jnp.max(n_lo - kf) > 0.0, lambda: band_min_max(lo, hi), lambda: (lo, lo))
        single = bmin == bmax
        split = over & (n_lo - n_hi == 2.0) & jnp.logical_not(single)
        unsplit = over & jnp.logical_not(split)
        return jnp.where(split, bmax, lo), jnp.where(over & single, 1.0, 0.0), jnp.where(unsplit, 1.0, 0.0)

    zero = jnp.zeros((1, tq), F32)
    state1 = search((jnp.int32(0), lo0, hi0, n_causal, zero), IDX_TIE_CHECK, zero)
    first = finish(state1)
    state = search(state1, iters, first[1])
    lo, _, unsplit = lax.cond(state[0] > state1[0], lambda: finish(state), lambda: first)
    hi, n_hi = state[2], state[4]
    tied = jnp.max(unsplit) > 0.0

    def fill(j, c):
        mb_ref[j] = jnp.full((tq, tk), NEG, mb_ref.dtype)
        return c

    lax.fori_loop(nch, nk, fill, 0)

    def emit(j, keep):
        mb_ref[j] = jnp.where(keep, 0.0, NEG).T.astype(mb_ref.dtype)

    @pl.when(jnp.logical_not(tied))
    def _():
        def body(j, c):
            emit(j, sc_ref[j] >= lo)
            return c

        lax.fori_loop(0, nch, body, 0)

    @pl.when(tied)
    def _():
        need = kf - n_hi
        tri = tri_ref[...]

        def body(j, base):
            sc = sc_ref[j]
            above = sc >= hi
            band = (sc >= lo) & jnp.logical_not(above)
            prefix = _dot(tri, jnp.where(band, 1.0, 0.0).astype(BF16)) + base
            emit(j, above | (band & (prefix <= need)))
            return prefix[tk - 1:tk, :]

        lax.fori_loop(0, nch, body, zero)


def dsa_index_mask(proj, *, B, S, tqi, tk, topk):
    nq = S // tqi
    nk = S // tk
    assert tk % tqi == 0 and tk % IDX_CHUNK == 0
    tri = jnp.asarray(np.tril(np.ones((tk, tk), np.float32)), BF16)
    return pl.pallas_call(
        functools.partial(_dsa_index_kernel, tq=tqi, tk=tk, S=S, topk=topk, iters=BISECT_ITERS),
        grid=(B, nq),
        in_specs=[pl.BlockSpec((tqi, 512), lambda b, i: (b * nq + i, OFF_IDX_Q // 512)),
                  pl.BlockSpec((S, LANE), lambda b, i: (b, OFF_IDX_K // LANE)),
                  pl.BlockSpec((tqi, LANE), lambda b, i: (b * nq + i, OFF_IDX_W // LANE)),
                  pl.BlockSpec((tk, tk), lambda b, i: (0, 0))],
        out_specs=pl.BlockSpec((None, nk, tqi, tk), lambda b, i: (b, 0, i, 0)),
        out_shape=jax.ShapeDtypeStruct((B, nk, S, tk), BF16),
        scratch_shapes=[pltpu.VMEM((nk, tk, tqi), F32)],
        compiler_params=_cparams(("parallel", "parallel")),
        name="dsa_index_mask",
    )(proj, proj, proj, tri)


def _dsa_attn_kernel(q_ref, k_ref, v_ref, mb_ref, bias_ref, o_ref, m_ref, acc_ref, *, tq, scale):
    qi = pl.program_id(2)
    hs = range(HPS)
    cols = lambda h: slice(h * LANE, (h + 1) * LANE)
    q = [q_ref[:, cols(h)] for h in hs]
    for h in hs:
        _softmax_init(m_ref.at[h], acc_ref.at[h])

    def step(j, nc, bias):
        sl = pl.ds(j * tq if isinstance(j, int) else pl.multiple_of(j * tq, tq), nc * tq)
        mb = mb_ref[j] if nc == 1 else jnp.concatenate([mb_ref[j + i] for i in range(nc)], axis=1)
        mb = mb.astype(F32)
        s = [_dot_nt(q[h], k_ref[sl, cols(h)]) + mb for h in hs]
        for h in hs:
            sh = s[h]
            if bias and nc == 1:
                sh = sh + bias_ref[h, :, tq:2 * tq]
            elif bias and nc == 2:
                sh = sh + bias_ref[h]
            elif bias:
                sh = jnp.concatenate([sh[:, :(nc - 2) * tq], sh[:, (nc - 2) * tq:] + bias_ref[h]], axis=1)
            _softmax_step(sh, v_ref[sl, cols(h)], m_ref.at[h], acc_ref.at[h], scale)

    _far_chunks(jnp.maximum(qi - (NEAR - 1), 0), lambda j, nc: step(j, nc, False), FAR_GROUP)
    for k in range(NEAR - 1):
        @pl.when(qi == k)
        def _(k=k):
            step(0, k + 1, True)

    @pl.when(qi >= NEAR - 1)
    def _():
        step(qi - (NEAR - 1), NEAR, True)

    for h in hs:
        o_ref[:, cols(h)] = _softmax_finish(acc_ref.at[h]).astype(o_ref.dtype)


def dsa_attention(proj, maskb, btiles, *, B, S, tq):
    M = B * S
    nq = S // tq
    H = DSA_HEADS
    scale = DSA_HEAD_DIM ** -0.5
    w = HPS * LANE
    return pl.pallas_call(
        functools.partial(_dsa_attn_kernel, tq=tq, scale=scale),
        grid=(B, H // HPS, nq),
        in_specs=[pl.BlockSpec((tq, w), lambda b, h, i: (b * nq + i, OFF_DSA_Q // w + h)),
                  pl.BlockSpec((S, w), lambda b, h, i: (b, OFF_DSA_K // w + h)),
                  pl.BlockSpec((S, w), lambda b, h, i: (b, OFF_DSA_V // w + h)),
                  pl.BlockSpec((None, nq, tq, tq), lambda b, h, i: (b, 0, i, 0)),
                  pl.BlockSpec((HPS, tq, 2 * tq), lambda b, h, i: (h, 0, 0))],
        out_specs=pl.BlockSpec((tq, w), lambda b, h, i: (b * nq + i, h)),
        out_shape=jax.ShapeDtypeStruct((M, H * DSA_HEAD_DIM), BF16),
        scratch_shapes=[pltpu.VMEM((HPS, tq, LANE), F32), pltpu.VMEM((HPS, tq, 2 * LANE), F32)],
        compiler_params=_cparams(("parallel", "parallel", "arbitrary")),
        name="dsa_attention",
    )(proj, proj, proj, maskb, btiles)


def _nsa_compress_kernel(x_ref, pos_ref, w1_ref, w2_ref, o_ref, *, ncp):
    dk = NSA_HEAD_DIM
    posw = _dot(pos_ref[...], w1_ref[...])[0:1, :]
    for g in range(NSA_GROUPS):
        lo = jnp.zeros((ncp, CMP_HIDDEN), F32)
        hi = jnp.zeros((ncp, CMP_HIDDEN), F32)
        for l in range(CMP_STRIDE):
            xs = x_ref[:, l, g * dk:(g + 1) * dk]
            lo = lo + _dot(xs, w1_ref[l * dk:(l + 1) * dk, :])
            hi = hi + _dot(xs, w1_ref[(CMP_STRIDE + l) * dk:(CMP_STRIDE + l + 1) * dk, :])
        hid = lo + pltpu.roll(hi, ncp - 1, 0) + posw
        o_ref[g] = _dot(jax.nn.gelu(hid).astype(BF16), w2_ref[...]).astype(o_ref.dtype)


def nsa_compress(proj, posflat, w1, w2, *, B, S):
    ncp = S // CMP_STRIDE
    G, dk = NSA_GROUPS, NSA_HEAD_DIM
    xkv = proj.reshape(B * ncp, CMP_STRIDE, N_PACK)
    return pl.pallas_call(
        functools.partial(_nsa_compress_kernel, ncp=ncp),
        grid=(2, B),
        in_specs=[pl.BlockSpec((ncp, CMP_STRIDE, G * dk), lambda a, b: (b, 0, OFF_NSA_KV // (G * dk) + a)),
                  pl.BlockSpec((None, 8, CMP_LEN * dk), lambda a, b: (a, 0, 0)),
                  pl.BlockSpec((None, CMP_LEN * dk, CMP_HIDDEN), lambda a, b: (a, 0, 0)),
                  pl.BlockSpec((None, CMP_HIDDEN, dk), lambda a, b: (a, 0, 0))],
        out_specs=pl.BlockSpec((None, None, G, ncp, dk), lambda a, b: (a, b, 0, 0, 0)),
        out_shape=jax.ShapeDtypeStruct((2, B, G, ncp, dk), BF16),
        compiler_params=_cparams(("parallel", "parallel")),
        name="nsa_compress",
    )(xkv, posflat, w1, w2)


def _nsa_cmp_kernel(q_ref, kc_ref, vc_ref, ov_ref, oc_ref, sel_ref, *, tq, ncp, ns, n_sel, scale):
    t0 = pl.program_id(2) * tq
    dk = NSA_HEAD_DIM
    kc = kc_ref[...]
    vc = vc_ref[...]
    trow = lax.broadcasted_iota(jnp.int32, (tq, ncp), 0) + t0
    ccol = lax.broadcasted_iota(jnp.int32, (tq, ncp), 1)
    vis = (ccol * CMP_STRIDE + (CMP_LEN - 1)) <= trow
    psum = jnp.zeros((tq, ncp), F32)
    for h in range(NSA_HG):
        s = jnp.where(vis, _dot_nt(q_ref[:, h * dk:(h + 1) * dk], kc) * scale, NEG)
        m = jnp.max(s, axis=-1, keepdims=True)
        p = jnp.where(vis, jnp.exp(s - m), 0.0)
        d = jnp.sum(p, axis=-1, keepdims=True)
        p = p / jnp.where(d > 0, d, 1.0)
        oc_ref[:, h * dk:(h + 1) * dk] = _dot(p.astype(BF16), vc).astype(oc_ref.dtype)
        psum = psum + p
    p_hi = psum.astype(BF16)
    p_lo = (psum - p_hi.astype(F32)).astype(BF16)
    ov = ov_ref[...]
    imp = _dot_nt(ov, p_hi) + _dot_nt(ov, p_lo)
    imp = imp[:ns]
    jrow = lax.broadcasted_iota(jnp.int32, (ns, tq), 0)
    tcol = lax.broadcasted_iota(jnp.int32, (ns, tq), 1) + t0
    blk_t = tcol // SEL_BLOCK
    forced = (jrow == 0) | (jrow == blk_t) | (jrow == blk_t - 1)
    val = jnp.where(forced, -NEG, imp)
    val = jnp.where(jrow * SEL_BLOCK > tcol, NEG, val)
    rank = jnp.zeros((ns, tq), F32)
    for j in range(ns):
        other = val[j:j + 1, :]
        ahead = (other > val) | ((other == val) & (jrow > j))
        rank = rank + jnp.where(ahead, 1.0, 0.0)
    selneg = jnp.where(rank < float(n_sel), 0.0, NEG)
    if ns < LANE:
        selneg = jnp.concatenate([selneg, jnp.zeros((LANE - ns, tq), F32)], axis=0)
    sel_ref[...] = selneg.T.astype(sel_ref.dtype)


def nsa_cmp_attention(proj, kvc, ovt, *, B, S, tq):
    M = B * S
    nq = S // tq
    G, dk = NSA_GROUPS, NSA_HEAD_DIM
    ncp = S // CMP_STRIDE
    ns = S // SEL_BLOCK
    assert ns <= LANE
    n_sel = min(SEL_COUNT, ns)
    gw = NSA_HG * dk
    return pl.pallas_call(
        functools.partial(_nsa_cmp_kernel, tq=tq, ncp=ncp, ns=ns, n_sel=n_sel, scale=dk ** -0.5),
        grid=(B, G, nq),
        in_specs=[pl.BlockSpec((tq, gw), lambda b, g, i: (b * nq + i, OFF_NSA_Q // gw + g)),
                  pl.BlockSpec((None, None, None, ncp, dk), lambda b, g, i: (0, b, g, 0, 0)),
                  pl.BlockSpec((None, None, None, ncp, dk), lambda b, g, i: (1, b, g, 0, 0)),
                  pl.BlockSpec((LANE, ncp), lambda b, g, i: (0, 0))],
        out_specs=[pl.BlockSpec((tq, gw), lambda b, g, i: (b * nq + i, g)),
                   pl.BlockSpec((None, None, tq, LANE), lambda b, g, i: (b, g, i, 0))],
        out_shape=[jax.ShapeDtypeStruct((M, G * gw), BF16),
                   jax.ShapeDtypeStruct((B, G, S, LANE), BF16)],
        compiler_params=_cparams(("parallel", "parallel", "parallel")),
        name="nsa_cmp_attention",
    )(proj, kvc, kvc, ovt)


def _nsa_main_kernel(q_ref, sel_ref, ks_ref, vs_ref, kw_ref, vw_ref, ind_ref, bias_ref, gate_ref, oc_ref,
                     o_ref, m_ref, acc_ref, *, tq, nback, scale):
    g = pl.program_id(1)
    qi = pl.program_id(2)
    dk = NSA_HEAD_DIM
    HG = NSA_HG
    ns = NSA_STREAMS
    hp = HG // ns
    st = range(ns)
    stack = lambda f: [jnp.concatenate([f(a * hp + i) for i in range(hp)], axis=0) for a in st]
    q = stack(lambda h: q_ref[:, h * dk:(h + 1) * dk])
    sel = jnp.concatenate([sel_ref[...]] * hp, axis=0)
    qa = [jnp.concatenate([q[a], sel], axis=1) for a in st]
    gate = jax.nn.sigmoid(gate_ref[...].astype(F32))

    for a in range(2 * ns):
        _softmax_init(m_ref.at[a], acc_ref.at[a])

    def steps(s, v, fix, base=0):
        for a in st:
            _softmax_step(fix(a, s[a]), v, m_ref.at[base + a], acc_ref.at[base + a], scale)

    wspan = nback + 1

    def rows_from(j, width):
        return pl.ds(j * tq if isinstance(j, int) else pl.multiple_of(j * tq, tq), width)

    def sel_scores(j, width=tq):
        sl = rows_from(j, width)
        ka = jnp.concatenate([ks_ref[sl, :], ind_ref[sl, :]], axis=1)
        return [_dot_nt(qa[a], ka) for a in st], vs_ref[sl, :]

    def sel_far(j, nc):
        s, v = sel_scores(j, nc * tq)
        steps(s, v, lambda a, x: x)

    nsel = NEAR_NSA
    _far_chunks(jnp.maximum(qi - (nsel - 1), 0), sel_far, FAR_GROUP_NSA)

    def win_scores(j, width):
        sl = rows_from(j, width)
        kw = kw_ref[sl, :]
        return [_dot_nt(q[a], kw) for a in st], vw_ref[sl, :]

    def near(first_s, nc_s, first_w, nc_w):
        tile = lambda nc: (lambda a, x: x + bias_ref[a, :, (wspan - nc) * tq:wspan * tq])

        def sel_tile(a, x):
            if nc_s <= 2:
                return tile(nc_s)(a, x)
            return jnp.concatenate([x[:, :(nc_s - 2) * tq], tile(2)(a, x[:, (nc_s - 2) * tq:])], axis=1)

        s_s, v_s = sel_scores(first_s, nc_s * tq)
        s_w, v_w = win_scores(first_w, nc_w * tq)
        steps(s_s, v_s, sel_tile)
        steps(s_w, v_w, tile(nc_w), base=ns)

    nmax = max(nsel, wspan)
    for k in range(nmax - 1):
        @pl.when(qi == k)
        def _(k=k):
            nc_s, nc_w = min(k + 1, nsel), min(k + 1, wspan)
            near(k + 1 - nc_s, nc_s, k + 1 - nc_w, nc_w)

    @pl.when(qi >= nmax - 1)
    def _():
        near(qi - (nsel - 1), nsel, qi - (wspan - 1), wspan)

    o_s = [_softmax_finish(acc_ref.at[a]) for a in st]
    o_w = [_softmax_finish(acc_ref.at[ns + a]) for a in st]

    lane = lax.broadcasted_iota(jnp.int32, gate.shape, 1)
    for h in range(HG):
        c = g * HG + h
        a, sl = h // hp, slice((h % hp) * tq, (h % hp + 1) * tq)
        gc = [jnp.sum(jnp.where(lane == br * NSA_HEADS + c, gate, 0.0), axis=-1, keepdims=True)
              for br in range(3)]
        o = gc[0] * oc_ref[:, h * dk:(h + 1) * dk].astype(F32) + gc[1] * o_s[a][sl] + gc[2] * o_w[a][sl]
        o_ref[:, h * dk:(h + 1) * dk] = o.astype(o_ref.dtype)


def nsa_main(proj, selneg, ind, btiles, o_c, *, B, S, tq):
    M = B * S
    nq = S // tq
    G, dk, HG = NSA_GROUPS, NSA_HEAD_DIM, NSA_HG
    gw = HG * dk
    assert WINDOW % tq == 0 and WINDOW // tq >= 2
    nback = WINDOW // tq
    ns, hp = NSA_STREAMS, HG // NSA_STREAMS
    kv = lambda n: pl.BlockSpec((S, dk), lambda b, g, i, n=n: (b, (OFF_NSA_KV + n * G * dk) // dk + g))
    return pl.pallas_call(
        functools.partial(_nsa_main_kernel, tq=tq, nback=nback, scale=dk ** -0.5),
        grid=(B, G, nq),
        in_specs=[pl.BlockSpec((tq, gw), lambda b, g, i: (b * nq + i, OFF_NSA_Q // gw + g)),
                  pl.BlockSpec((None, None, tq, LANE), lambda b, g, i: (b, g, i, 0)),
                  kv(2), kv(3), kv(4), kv(5),
                  pl.BlockSpec((S, LANE), lambda b, g, i: (0, 0)),
                  pl.BlockSpec((ns, hp * tq, (nback + 1) * tq), lambda b, g, i: (g, 0, 0)),
                  pl.BlockSpec((tq, LANE), lambda b, g, i: (b * nq + i, OFF_NSA_GATE // LANE)),
                  pl.BlockSpec((tq, gw), lambda b, g, i: (b * nq + i, g))],
        out_specs=pl.BlockSpec((tq, gw), lambda b, g, i: (b * nq + i, g)),
        out_shape=jax.ShapeDtypeStruct((M, G * gw), BF16),
        scratch_shapes=[pltpu.VMEM((2 * ns, hp * tq, LANE), F32),
                        pltpu.VMEM((2 * ns, hp * tq, 2 * LANE), F32)],
        compiler_params=_cparams(("parallel", "parallel", "arbitrary")),
        name="nsa_main",
    )(proj, selneg, proj, proj, proj, proj, ind, btiles, proj, o_c)


def _merge_kernel(x_ref, oa_ref, ob_ref, oc_ref, g0_ref, g1_ref, g2_ref, wb_ref, wo_ref, gain_ref, o_ref):
    merged = None
    for o_r, g_r, br in ((oa_ref, g0_ref, 0), (ob_ref, g1_ref, 1), (oc_ref, g2_ref, 2)):
        t = jax.nn.sigmoid(g_r[...].astype(F32)) * _dot(o_r[...], wb_ref[br])
        merged = t if merged is None else merged + t
    y = _dot(merged.astype(BF16), wo_ref[...])
    o_ref[...] = x_ref[...] + _rms(y, gain_ref[...])


def merge_branches(x, o_a, o_b, o_c, proj, wb, wo, gain, *, tm):
    M, D = x.shape
    row = pl.BlockSpec((tm, D), lambda i: (i, 0))
    gate = lambda br: pl.BlockSpec((tm, D), lambda i, br=br: (i, OFF_BGATE // D + br))
    return pl.pallas_call(
        _merge_kernel,
        grid=(M // tm,),
        in_specs=[row, row, row, row, gate(0), gate(1), gate(2),
                  pl.BlockSpec(wb.shape, lambda i: (0, 0, 0)),
                  pl.BlockSpec(wo.shape, lambda i: (0, 0)),
                  pl.BlockSpec((1, D), lambda i: (0, 0))],
        out_specs=row,
        out_shape=jax.ShapeDtypeStruct((M, D), F32),
        compiler_params=_cparams(("parallel",)),
        name="merge_branches",
    )(x, o_a, o_b, o_c, proj, proj, proj, wb, wo, gain)


def _cross_kernel(x_ref, kv_ref, wq_ref, wo_ref, gpre_ref, gpost_ref, o_ref, *, scale):
    x = x_ref[...]
    h = _rms(x, gpre_ref[...]).astype(BF16)
    q = _dot(h, wq_ref[...]).astype(BF16)
    dh = CROSS_HEAD_DIM
    outs = []
    for hd in range(CROSS_HEADS):
        k = kv_ref[:, hd * dh:(hd + 1) * dh]
        v = kv_ref[:, D_MODEL + hd * dh:D_MODEL + (hd + 1) * dh]
        s = _dot_nt(q[:, hd * dh:(hd + 1) * dh], k) * scale
        p = jnp.exp(s - jnp.max(s, axis=-1, keepdims=True))
        p = p / jnp.sum(p, axis=-1, keepdims=True)
        outs.append(_dot(p.astype(BF16), v).astype(BF16))
    y = _dot(jnp.concatenate(outs, axis=1), wo_ref[...])
    o_ref[...] = x + _rms(y, gpost_ref[...])


def cross_attention(x, kv, wq, wo, gpre, gpost, *, S, tm):
    M, D = x.shape
    mlen = kv.shape[0] // (M // S)
    per_b = S // tm
    return pl.pallas_call(
        functools.partial(_cross_kernel, scale=CROSS_HEAD_DIM ** -0.5),
        grid=(M // tm,),
        in_specs=[pl.BlockSpec((tm, D), lambda i: (i, 0)),
                  pl.BlockSpec((mlen, 2 * D), lambda i: (i // per_b, 0)),
                  pl.BlockSpec(wq.shape, lambda i: (0, 0)),
                  pl.BlockSpec(wo.shape, lambda i: (0, 0)),
                  pl.BlockSpec((1, D), lambda i: (0, 0)),
                  pl.BlockSpec((1, D), lambda i: (0, 0))],
        out_specs=pl.BlockSpec((tm, D), lambda i: (i, 0)),
        out_shape=jax.ShapeDtypeStruct((M, D), F32),
        compiler_params=_cparams(("parallel",)),
        name="cross_attention",
    )(x, kv, wq, wo, gpre, gpost)


def _mlp_kernel(x_ref, w1_ref, w2_ref, gpre_ref, gpost_ref, o_ref, h_ref, acc_ref):
    j = pl.program_id(1)

    @pl.when(j == 0)
    def _():
        h_ref[...] = _rms(x_ref[...], gpre_ref[...]).astype(BF16)
        acc_ref[...] = jnp.zeros(acc_ref.shape, F32)

    a = jnp.maximum(_dot(h_ref[...], w1_ref[...]), 0.0)
    acc_ref[...] += _dot((a * a).astype(BF16), w2_ref[...])

    @pl.when(j == pl.num_programs(1) - 1)
    def _():
        o_ref[...] = x_ref[...] + _rms(acc_ref[...], gpost_ref[...])


def mlp(x, w1, w2, gpre, gpost, *, tm, tf):
    M, D = x.shape
    FF = w1.shape[1]
    return pl.pallas_call(
        _mlp_kernel,
        grid=(M // tm, FF // tf),
        in_specs=[pl.BlockSpec((tm, D), lambda i, j: (i, 0)),
                  pl.BlockSpec((D, tf), lambda i, j: (0, j)),
                  pl.BlockSpec((tf, D), lambda i, j: (j, 0)),
                  pl.BlockSpec((1, D), lambda i, j: (0, 0)),
                  pl.BlockSpec((1, D), lambda i, j: (0, 0))],
        out_specs=pl.BlockSpec((tm, D), lambda i, j: (i, 0)),
        out_shape=jax.ShapeDtypeStruct((M, D), F32),
        scratch_shapes=[pltpu.VMEM((tm, D), BF16), pltpu.VMEM((tm, D), F32)],
        compiler_params=_cparams(("parallel", "arbitrary")),
        name="mlp",
    )(x, w1, w2, gpre, gpost)


def _rot_half_cols(w):
    half = w.shape[-1] // 2
    return jnp.concatenate([-w[..., half:], w[..., :half]], axis=-1)


def _pack_w_in(w):
    widths = (MLA_Q_LORA, MLA_KV_LORA, MLA_ROPE, 1024, 1024, 1024, IDX_HEADS * IDX_DIM, IDX_DIM, IDX_HEADS,
              1024, 256, 256, 256, 256, 256, 256, 3 * NSA_HEADS, N_BRANCH * D_MODEL)
    offs = np.concatenate([[0], np.cumsum(widths)])
    seg = [w[..., offs[i]:offs[i + 1]] for i in range(len(widths))]
    (q_lat, kv_lat, k_rope, dsa_q, dsa_k, dsa_v, idx_q, idx_k, idx_w,
     nsa_q, kc, vc, ks, vs, kw, vw, nsa_gate, bgate) = seg
    z = lambda n: jnp.zeros(w.shape[:-1] + (n,), w.dtype)
    cols = [dsa_q, dsa_k, dsa_v, nsa_q, bgate, idx_q, kc, vc, ks, vs, kw, vw,
            q_lat, kv_lat, k_rope, z(LANE - MLA_ROPE), _rot_half_cols(k_rope), z(LANE - MLA_ROPE), z(LANE),
            idx_k, idx_k, idx_w, z(LANE - IDX_HEADS), nsa_gate, z(LANE - 3 * NSA_HEADS)]
    out = jnp.concatenate(cols, axis=-1)
    out = jnp.concatenate([out, z(N_PACK - out.shape[-1])], axis=-1)
    return out.astype(BF16)


def _pack_mla_weights(w_uq, w_ukv):
    H = MLA_HEADS
    uq = w_uq.reshape(MLA_Q_LORA, H, MLA_NOPE + MLA_ROPE)
    nope = uq[:, :, :MLA_NOPE].reshape(MLA_Q_LORA, H * MLA_NOPE)
    rope = uq[:, :, MLA_NOPE:]
    pad = jnp.zeros((MLA_Q_LORA, H, LANE - MLA_ROPE), w_uq.dtype)
    wqr = jnp.concatenate([rope, pad], axis=-1).reshape(MLA_Q_LORA, H * LANE)
    wqrr = jnp.concatenate([_rot_half_cols(rope), pad], axis=-1).reshape(MLA_Q_LORA, H * LANE)
    ukv = w_ukv.reshape(MLA_KV_LORA, H, MLA_NOPE + MLA_V)
    wkv = jnp.concatenate([ukv[:, :, :MLA_NOPE].reshape(MLA_KV_LORA, H * MLA_NOPE),
                           ukv[:, :, MLA_NOPE:].reshape(MLA_KV_LORA, H * MLA_V)], axis=1)
    return nope.astype(BF16), wqr.astype(BF16), wqrr.astype(BF16), wkv.astype(BF16)


def _constants(S):
    half = MLA_ROPE // 2
    inv_freq = ROPE_BASE ** (-np.arange(0, MLA_ROPE, 2, dtype=np.float32) / MLA_ROPE)
    freq = np.zeros((1, LANE), np.float32)
    freq[0, :half] = inv_freq
    freq[0, half:2 * half] = inv_freq
    ncp = S // CMP_STRIDE
    ns = S // SEL_BLOCK
    c_start = np.arange(ncp) * CMP_STRIDE
    j_start = np.arange(ns) * SEL_BLOCK
    overlap = ((c_start[None, :] < j_start[:, None] + SEL_BLOCK) &
               (c_start[None, :] + CMP_LEN > j_start[:, None])).astype(np.float32)
    overlap[:, ncp - 1] = 0.0
    ovt = np.zeros((LANE, ncp), np.float32)
    ovt[:ns] = overlap
    ind = np.zeros((S, LANE), np.float32)
    ind[np.arange(S), np.arange(S) // SEL_BLOCK] = 1.0
    return jnp.asarray(freq), jnp.asarray(ovt, BF16), jnp.asarray(ind, BF16)


def kernel(x, mem, positions, rel_bias, norm_gains, w_in, mla_q_norm, mla_kv_norm, mla_w_uq, mla_w_ukv,
           nsa_cmp_pos, nsa_cmp_w1, nsa_cmp_w2, w_branch, w_out, cross_wq, cross_wkv, cross_wo, mlp_w1, mlp_w2):
    B, S, D = x.shape
    M = B * S
    depth = w_in.shape[0]
    w_in_bf = _pack_w_in(w_in.astype(BF16))
    tq = TQ if S % TQ == 0 else TQ_NSA
    tqn = TQ_NSA
    assert S % tq == 0 and S % tqn == 0 and D == D_MODEL
    topk = min(DSA_TOPK_MAX, S // 4)
    freq, ovt, ind = _constants(S)
    btiles = bias_tiles(rel_bias, tq, 0, DSA_HEADS, nspan=2, masked=False, stack=1)
    btiles_n = bias_tiles(rel_bias, tqn, DSA_HEADS, NSA_HEADS, nspan=WINDOW // tqn + 1, masked=True,
                          stack=NSA_HG // NSA_STREAMS)
    pos = positions.reshape(M, 1)
    xf = x.reshape(M, D)
    memf = mem.reshape(B * mem.shape[1], D)
    tm_big = 1024 if M % 1024 == 0 else tq
    tm = 512 if M % 512 == 0 else tq
    ncp = S // CMP_STRIDE
    row = lambda v: v.reshape(1, -1)

    for l in range(depth):
        g = norm_gains[l]
        proj = norm_matmul(xf, row(g[0]), w_in_bf[l], tm=tm_big, tn=TN_IN)

        wqn, wqr, wqrr, wkv = _pack_mla_weights(mla_w_uq[l], mla_w_ukv[l])
        qn, qr, kn, vv, kr = mla_prep(proj, pos, freq, row(mla_q_norm[l]), row(mla_kv_norm[l]),
                                      wqn, wqr, wqrr, wkv, tm=tm_big)
        o_a = mla_attention(qn, qr, kn, kr, vv, B=B, S=S, tq=tq)

        maskb = dsa_index_mask(proj, B=B, S=S, tqi=TQ_IDX, tk=tq, topk=topk)
        o_b = dsa_attention(proj, maskb, btiles, B=B, S=S, tq=tq)

        posflat = jnp.broadcast_to(nsa_cmp_pos[l].reshape(2, 1, CMP_LEN * NSA_HEAD_DIM),
                                   (2, 8, CMP_LEN * NSA_HEAD_DIM)).astype(BF16)
        kvc = nsa_compress(proj, posflat, nsa_cmp_w1[l].astype(BF16), nsa_cmp_w2[l].astype(BF16), B=B, S=S)
        o_cmp, selneg = nsa_cmp_attention(proj, kvc, ovt, B=B, S=S, tq=TQ_CMP if S % TQ_CMP == 0 else tq)
        o_c = nsa_main(proj, selneg, ind, btiles_n, o_cmp, B=B, S=S, tq=tqn)

        xf = merge_branches(xf, o_a, o_b, o_c, proj, w_branch[l].astype(BF16), w_out[l].astype(BF16),
                            row(g[1]), tm=tm)

        mkv = norm_matmul(memf, row(g[3]), cross_wkv[l].astype(BF16), tm=memf.shape[0] // B, tn=1024)
        xf = cross_attention(xf, mkv, cross_wq[l].astype(BF16), cross_wo[l].astype(BF16),
                             row(g[2]), row(g[4]), S=S, tm=tm_big)

        xf = mlp(xf, mlp_w1[l].astype(BF16), mlp_w2[l].astype(BF16), row(g[5]), row(g[6]), tm=tm_big, tf=1024)

    return xf.reshape(B, S, D)
```
